```python
import math
import jax, jax.numpy as jnp
from jax import lax
import numpy as np

D_MODEL = 1024
BATCH = 8
SEQ = 8192
DEPTH = 2

N_MIXERS = 2
EPS = 1e-6
POOL_WINDOWS = (2, 4, 8, 16)
N_POOL_GROUPS = len(POOL_WINDOWS)
POOL_GROUP_DIM = D_MODEL // N_POOL_GROUPS
HEAD_DIM = 64
N_HEADS = D_MODEL // HEAD_DIM
D_ATTN = N_HEADS * HEAD_DIM
ATTN_PATTERNS = ((128, 1), (512, 4), (2048, 16))
N_ATTN_GROUPS = len(ATTN_PATTERNS)
HEAD_GROUPS = tuple(N_HEADS // N_ATTN_GROUPS + (1 if g < N_HEADS % N_ATTN_GROUPS else 0) for g in range(N_ATTN_GROUPS))
ROPE_THETA = 10000.0
D_FF = -(-8 * D_MODEL // (3 * 256)) * 256
N_POOL_LAYERS = (DEPTH + 1) // 2
N_ATTN_LAYERS = DEPTH // 2
NEG_INF = -1e30

kernel_name = "hybrid_pool_dilated_swa_swiglu"


def rmsnorm(x, g):
    xf = x.astype(jnp.float32)
    y = xf * lax.rsqrt(jnp.mean(xf * xf, axis=-1, keepdims=True) + EPS)
    return (y * g.astype(jnp.float32)).astype(x.dtype)


def rope(t):
    S, hd = t.shape[1], t.shape[-1]
    inv_freq = 1.0 / (ROPE_THETA ** (jnp.arange(0, hd, 2, dtype=jnp.float32) / hd))
    ang = jnp.arange(S, dtype=jnp.float32)[:, None] * inv_freq[None, :]
    ang = jnp.concatenate([ang, ang], axis=-1)[None, :, None, :]
    tf = t.astype(jnp.float32)
    t1, t2 = tf[..., : hd // 2], tf[..., hd // 2 :]
    rot = jnp.concatenate([-t2, t1], axis=-1)
    return (tf * jnp.cos(ang) + rot * jnp.sin(ang)).astype(t.dtype)


def pool_mixer(h, w_in, w_group, scale, w_out):
    B, S, _ = h.shape
    u = jnp.einsum('bsd,de->bse', h, w_in).reshape(B, S, N_POOL_GROUPS, POOL_GROUP_DIM)
    cs = jnp.cumsum(u.astype(jnp.float32), axis=1)
    pos = jnp.arange(S)
    outs = []
    for g, w in enumerate(POOL_WINDOWS):
        c = cs[:, :, g]
        lag = jnp.pad(c, ((0, 0), (w, 0), (0, 0)))[:, :S]
        cnt = jnp.minimum(pos + 1, w).astype(jnp.float32)[None, :, None]
        outs.append((c - lag) / cnt - u[:, :, g].astype(jnp.float32))
    p = jnp.stack(outs, axis=2).astype(h.dtype)
    z = jnp.einsum('bsgc,gce->bsge', p, w_group).reshape(B, S, D_MODEL) * scale
    return jnp.einsum('bsd,de->bse', z, w_out)


def dilated_window_attention(q, k, v, window, dilation):
    B, S, H, hd = q.shape
    w = window // dilation
    L = S // dilation
    nb = -(-L // w)
    Lp = nb * w

    def to_blocks(t):
        t = t.reshape(B, L, dilation, H, hd)
        t = jnp.pad(t, ((0, 0), (0, Lp - L), (0, 0), (0, 0), (0, 0)))
        return t.reshape(B, nb, w, dilation, H, hd)

    qb, kb, vb = to_blocks(q), to_blocks(k), to_blocks(v)

    def with_prev(t):
        prev = jnp.concatenate([jnp.zeros_like(t[:, :1]), t[:, :-1]], axis=1)
        return jnp.concatenate([prev, t], axis=2)

    kc, vc = with_prev(kb), with_prev(vb)
    s = jnp.einsum('bnqrhd,bnkrhd->bnrhqk', qb, kc).astype(jnp.float32)
    qi = jnp.arange(w)[:, None]
    kj = jnp.arange(2 * w)[None, :]
    dist = w + qi - kj
    key_idx = jnp.arange(nb)[:, None, None] * w - w + kj[None]
    mask = (dist >= 0)[None] & (dist <= w)[None] & (key_idx >= 0)
    s = jnp.where(mask[None, :, None, None], s, NEG_INF)
    m = jnp.max(s, axis=-1, keepdims=True)
    e = jnp.exp(s - m)
    den = jnp.sum(e, axis=-1, keepdims=True)
    p = (e / den).astype(v.dtype)
    o = jnp.einsum('bnrhqk,bnkrhd->bnqrhd', p, vc)
    lse = (m + jnp.log(den))[..., 0]
    lse = jnp.transpose(lse, (0, 1, 4, 2, 3))
    o = o.reshape(B, Lp, dilation, H, hd)[:, :L].reshape(B, S, H, hd)
    lse = lse.reshape(B, Lp, dilation, H)[:, :L].reshape(B, S, H)
    return o, lse


def attn_mixer(h, w_qkv, w_out):
    B, S, _ = h.shape
    qkv = jnp.einsum('bsd,de->bse', h, w_qkv).reshape(B, S, 3, N_HEADS, HEAD_DIM)
    q = rope(qkv[:, :, 0]) * jnp.asarray(HEAD_DIM ** -0.5, dtype=h.dtype)
    k = rope(qkv[:, :, 1])
    v = qkv[:, :, 2]
    outs, lses = [], []
    start = 0
    for (window, dilation), n_g in zip(ATTN_PATTERNS, HEAD_GROUPS):
        sl = slice(start, start + n_g)
        o_g, lse_g = dilated_window_attention(q[:, :, sl], k[:, :, sl], v[:, :, sl], window, dilation)
        outs.append(o_g)
        lses.append(jax.nn.logsumexp(lse_g, axis=-1) - math.log(n_g))
        start += n_g
    alpha = jax.nn.softmax(jnp.stack(lses, axis=-1), axis=-1)
    merged = jnp.concatenate(
        [o_g * (N_ATTN_GROUPS * alpha[:, :, g]).astype(o_g.dtype)[:, :, None, None] for g, o_g in enumerate(outs)],
        axis=2,
    ).reshape(B, S, D_ATTN)
    return jnp.einsum('bse,ed->bsd', merged, w_out)


def swiglu(h, w_gate, w_up, w_down):
    g = jnp.einsum('bsd,df->bsf', h, w_gate)
    u = jnp.einsum('bsd,df->bsf', h, w_up)
    return jnp.einsum('bsf,fd->bsd', jax.nn.silu(g) * u, w_down)


def _fwd_setup_inputs(seed: int = 0) -> dict:
    key = jax.random.key(seed)
    ks = jax.random.split(key, 14)
    f32 = jnp.float32
    nrm = lambda k, shape, fan_in: jax.random.normal(k, shape, f32) * (fan_in ** -0.5)
    return {
        "x": jax.random.normal(ks[0], (BATCH, SEQ, D_MODEL), f32),
        "norm_mix": 1.0 + 0.05 * jax.random.normal(ks[1], (DEPTH, D_MODEL), f32),
        "norm_ffn": 1.0 + 0.05 * jax.random.normal(ks[2], (DEPTH, D_MODEL), f32),
        "norm_final": 1.0 + 0.05 * jax.random.normal(ks[3], (D_MODEL,), f32),
        "pool_w_in": nrm(ks[4], (N_POOL_LAYERS, D_MODEL, D_MODEL), D_MODEL),
        "pool_w_group": nrm(ks[5], (N_POOL_LAYERS, N_POOL_GROUPS, POOL_GROUP_DIM, POOL_GROUP_DIM), POOL_GROUP_DIM),
        "pool_scale": 1.0 + 0.1 * jax.random.normal(ks[6], (N_POOL_LAYERS, D_MODEL), f32),
        "pool_w_out": nrm(ks[7], (N_POOL_LAYERS, D_MODEL, D_MODEL), D_MODEL),
        "attn_w_qkv": nrm(ks[8], (N_ATTN_LAYERS, D_MODEL, 3 * D_ATTN), D_MODEL),
        "attn_w_out": nrm(ks[9], (N_ATTN_LAYERS, D_ATTN, D_MODEL), D_ATTN),
        "ffn_w_gate": nrm(ks[10], (DEPTH, D_MODEL, D_FF), D_MODEL),
        "ffn_w_up": nrm(ks[11], (DEPTH, D_MODEL, D_FF), D_MODEL),
        "ffn_w_down": nrm(ks[12], (DEPTH, D_FF, D_MODEL), D_FF),
    }


def _fwd_reference(x, norm_mix, norm_ffn, norm_final, pool_w_in, pool_w_group, pool_scale, pool_w_out,
              attn_w_qkv, attn_w_out, ffn_w_gate, ffn_w_up, ffn_w_down):
    for i in range(DEPTH):
        h = rmsnorm(x, norm_mix[i])
        j = i // N_MIXERS
        if i % N_MIXERS == 0:
            y = pool_mixer(h, pool_w_in[j], pool_w_group[j], pool_scale[j], pool_w_out[j])
        else:
            y = attn_mixer(h, attn_w_qkv[j], attn_w_out[j])
        x = x + y
        h = rmsnorm(x, norm_ffn[i])
        x = x + swiglu(h, ffn_w_gate[i], ffn_w_up[i], ffn_w_down[i])
    return rmsnorm(x, norm_final)


import jax as _jax
import jax.numpy as _jnp

TWIN_FORMAT = 'train_step'
FWD_PARAMS = ['x', 'norm_mix', 'norm_ffn', 'norm_final', 'pool_w_in', 'pool_w_group', 'pool_scale', 'pool_w_out', 'attn_w_qkv', 'attn_w_out', 'ffn_w_gate', 'ffn_w_up', 'ffn_w_down']
TWIN_WEIGHTS = ['norm_mix', 'norm_ffn', 'norm_final', 'pool_w_in', 'pool_w_group', 'pool_scale', 'pool_w_out', 'attn_w_qkv', 'attn_w_out', 'ffn_w_gate', 'ffn_w_up', 'ffn_w_down']
TWIN_DIFF_INPUT = 'x'
TWIN_INPUTS = ['x', 'norm_mix', 'norm_ffn', 'norm_final', 'pool_w_in', 'pool_w_group', 'pool_scale', 'pool_w_out', 'attn_w_qkv', 'attn_w_out', 'ffn_w_gate', 'ffn_w_up', 'ffn_w_down', 'loss_target', 'm_norm_mix', 'm_norm_ffn', 'm_norm_final', 'm_pool_w_in', 'm_pool_w_group', 'm_pool_scale', 'm_pool_w_out', 'm_attn_w_qkv', 'm_attn_w_out', 'm_ffn_w_gate', 'm_ffn_w_up', 'm_ffn_w_down', 'v_norm_mix', 'v_norm_ffn', 'v_norm_final', 'v_pool_w_in', 'v_pool_w_group', 'v_pool_scale', 'v_pool_w_out', 'v_attn_w_qkv', 'v_attn_w_out', 'v_ffn_w_gate', 'v_ffn_w_up', 'v_ffn_w_down']
TWIN_OUTPUTS = ['loss', 'grad_x', 'grad_norm_mix', 'grad_norm_ffn', 'grad_norm_final', 'grad_pool_w_in', 'grad_pool_w_group', 'grad_pool_scale', 'grad_pool_w_out', 'grad_attn_w_qkv', 'grad_attn_w_out', 'grad_ffn_w_gate', 'grad_ffn_w_up', 'grad_ffn_w_down', 'delta_norm_mix', 'delta_norm_ffn', 'delta_norm_final', 'delta_pool_w_in', 'delta_pool_w_group', 'delta_pool_scale', 'delta_pool_w_out', 'delta_attn_w_qkv', 'delta_attn_w_out', 'delta_ffn_w_gate', 'delta_ffn_w_up', 'delta_ffn_w_down', 'new_m_norm_mix', 'new_m_norm_ffn', 'new_m_norm_final', 'new_m_pool_w_in', 'new_m_pool_w_group', 'new_m_pool_scale', 'new_m_pool_w_out', 'new_m_attn_w_qkv', 'new_m_attn_w_out', 'new_m_ffn_w_gate', 'new_m_ffn_w_up', 'new_m_ffn_w_down', 'new_v_norm_mix', 'new_v_norm_ffn', 'new_v_norm_final', 'new_v_pool_w_in', 'new_v_pool_w_group', 'new_v_pool_scale', 'new_v_pool_w_out', 'new_v_attn_w_qkv', 'new_v_attn_w_out', 'new_v_ffn_w_gate', 'new_v_ffn_w_up', 'new_v_ffn_w_down']
TWIN_LEAF_KINDS = {'loss': 'loss', 'grad_x': 'grad_x', 'grad_norm_mix': 'grad_w', 'grad_norm_ffn': 'grad_w', 'grad_norm_final': 'grad_w', 'grad_pool_w_in': 'grad_w', 'grad_pool_w_group': 'grad_w', 'grad_pool_scale': 'grad_w', 'grad_pool_w_out': 'grad_w', 'grad_attn_w_qkv': 'grad_w', 'grad_attn_w_out': 'grad_w', 'grad_ffn_w_gate': 'grad_w', 'grad_ffn_w_up': 'grad_w', 'grad_ffn_w_down': 'grad_w', 'delta_norm_mix': 'delta_w', 'delta_norm_ffn': 'delta_w', 'delta_norm_final': 'delta_w', 'delta_pool_w_in': 'delta_w', 'delta_pool_w_group': 'delta_w', 'delta_pool_scale': 'delta_w', 'delta_pool_w_out': 'delta_w', 'delta_attn_w_qkv': 'delta_w', 'delta_attn_w_out': 'delta_w', 'delta_ffn_w_gate': 'delta_w', 'delta_ffn_w_up': 'delta_w', 'delta_ffn_w_down': 'delta_w', 'new_m_norm_mix': 'new_m', 'new_m_norm_ffn': 'new_m', 'new_m_norm_final': 'new_m', 'new_m_pool_w_in': 'new_m', 'new_m_pool_w_group': 'new_m', 'new_m_pool_scale': 'new_m', 'new_m_pool_w_out': 'new_m', 'new_m_attn_w_qkv': 'new_m', 'new_m_attn_w_out': 'new_m', 'new_m_ffn_w_gate': 'new_m', 'new_m_ffn_w_up': 'new_m', 'new_m_ffn_w_down': 'new_m', 'new_v_norm_mix': 'new_v', 'new_v_norm_ffn': 'new_v', 'new_v_norm_final': 'new_v', 'new_v_pool_w_in': 'new_v', 'new_v_pool_w_group': 'new_v', 'new_v_pool_scale': 'new_v', 'new_v_pool_w_out': 'new_v', 'new_v_attn_w_qkv': 'new_v', 'new_v_attn_w_out': 'new_v', 'new_v_ffn_w_gate': 'new_v', 'new_v_ffn_w_up': 'new_v', 'new_v_ffn_w_down': 'new_v'}


def _forward(args):
    return _fwd_reference(*[args[k] for k in FWD_PARAMS])


def _output_shape():
    def fwd():
        inp = _fwd_setup_inputs(0)
        return _fwd_reference(*[inp[k] for k in FWD_PARAMS])
    out = _jax.eval_shape(fwd)
    return out.shape, out.dtype

N_MICROBATCH = 1
ADAM_LR = 0.001
ADAM_B1 = 0.9
ADAM_B2 = 0.999
ADAM_EPS = 1e-08
ADAM_WD = 0.01
ADAM_STEP = 10
PER_EXAMPLE_BATCH_AXIS = {'x': 0, 'loss_target': 0}
SHARED_INPUTS = []
_WEIGHT_DTYPES = {'norm_mix': _jnp.float32, 'norm_ffn': _jnp.float32, 'norm_final': _jnp.float32, 'pool_w_in': _jnp.float32, 'pool_w_group': _jnp.float32, 'pool_scale': _jnp.float32, 'pool_w_out': _jnp.float32, 'attn_w_qkv': _jnp.float32, 'attn_w_out': _jnp.float32, 'ffn_w_gate': _jnp.float32, 'ffn_w_up': _jnp.float32, 'ffn_w_down': _jnp.float32}
MOMENT_SCALE = {'norm_mix': 1.536189e-01, 'norm_ffn': 1.515991e-01, 'norm_final': 6.408391e+01, 'pool_w_in': 1.987070e-01, 'pool_w_group': 2.018403e-01, 'pool_scale': 2.409214e-01, 'pool_w_out': 2.049159e-01, 'attn_w_qkv': 5.117715e-02, 'attn_w_out': 5.491150e-02, 'ffn_w_gate': 6.510169e-02, 'ffn_w_up': 6.317290e-02, 'ffn_w_down': 1.048327e-01}


def _to_microbatches(a, axis):
    t = _jnp.moveaxis(a, axis, 0)
    t = t.reshape((N_MICROBATCH, t.shape[0] // N_MICROBATCH) + t.shape[1:])
    return _jnp.moveaxis(t, 1, axis + 1)


def setup_inputs(seed: int = 0) -> dict:
    inp = _fwd_setup_inputs(seed)
    key = _jax.random.fold_in(_jax.random.key(seed), 7919)
    shape, _ = _output_shape()
    out = dict(inp)
    out["loss_target"] = _jax.random.normal(_jax.random.fold_in(key, 0), shape, _jnp.float32)
    for i, name in enumerate(TWIN_WEIGHTS):
        w = inp[name].astype(_jnp.float32)
        if MOMENT_SCALE is None:
            s = _jnp.sqrt(_jnp.mean(_jnp.square(w)) + 1e-30)
        else:
            s = MOMENT_SCALE[name]
        km, kv = _jax.random.split(_jax.random.fold_in(key, i + 1))
        out[name] = w
        out["m_" + name] = s * _jax.random.normal(km, w.shape, _jnp.float32)
        out["v_" + name] = (s * s) * _jax.random.uniform(kv, w.shape, _jnp.float32, 0.5, 1.5)
    if N_MICROBATCH > 1:
        for name, axis in PER_EXAMPLE_BATCH_AXIS.items():
            out[name] = _to_microbatches(out[name], axis)
    return {'x': out['x'], 'norm_mix': out['norm_mix'], 'norm_ffn': out['norm_ffn'], 'norm_final': out['norm_final'], 'pool_w_in': out['pool_w_in'], 'pool_w_group': out['pool_w_group'], 'pool_scale': out['pool_scale'], 'pool_w_out': out['pool_w_out'], 'attn_w_qkv': out['attn_w_qkv'], 'attn_w_out': out['attn_w_out'], 'ffn_w_gate': out['ffn_w_gate'], 'ffn_w_up': out['ffn_w_up'], 'ffn_w_down': out['ffn_w_down'], 'loss_target': out['loss_target'], 'm_norm_mix': out['m_norm_mix'], 'm_norm_ffn': out['m_norm_ffn'], 'm_norm_final': out['m_norm_final'], 'm_pool_w_in': out['m_pool_w_in'], 'm_pool_w_group': out['m_pool_w_group'], 'm_pool_scale': out['m_pool_scale'], 'm_pool_w_out': out['m_pool_w_out'], 'm_attn_w_qkv': out['m_attn_w_qkv'], 'm_attn_w_out': out['m_attn_w_out'], 'm_ffn_w_gate': out['m_ffn_w_gate'], 'm_ffn_w_up': out['m_ffn_w_up'], 'm_ffn_w_down': out['m_ffn_w_down'], 'v_norm_mix': out['v_norm_mix'], 'v_norm_ffn': out['v_norm_ffn'], 'v_norm_final': out['v_norm_final'], 'v_pool_w_in': out['v_pool_w_in'], 'v_pool_w_group': out['v_pool_w_group'], 'v_pool_scale': out['v_pool_scale'], 'v_pool_w_out': out['v_pool_w_out'], 'v_attn_w_qkv': out['v_attn_w_qkv'], 'v_attn_w_out': out['v_attn_w_out'], 'v_ffn_w_gate': out['v_ffn_w_gate'], 'v_ffn_w_up': out['v_ffn_w_up'], 'v_ffn_w_down': out['v_ffn_w_down']}


def _loss(weights, diff, rest, loss_target):
    with _jax.named_scope("forward"):
        args = {**rest, TWIN_DIFF_INPUT: diff, **{k: w.astype(_WEIGHT_DTYPES[k]) for k, w in weights.items()}}
        y = _forward(args)
    with _jax.named_scope("loss_head"):
        err = _jnp.square(y.astype(_jnp.float32) - loss_target)
        return 0.5 * _jnp.sum(_jnp.mean(err, axis=-1)) if err.ndim else 0.5 * err


def _adamw(w, g, m, v):
    m = ADAM_B1 * m + (1.0 - ADAM_B1) * g
    v = ADAM_B2 * v + (1.0 - ADAM_B2) * _jnp.square(g)
    m_hat = m / (1.0 - ADAM_B1 ** ADAM_STEP)
    v_hat = v / (1.0 - ADAM_B2 ** ADAM_STEP)
    delta = -ADAM_LR * (m_hat / (_jnp.sqrt(v_hat) + ADAM_EPS) + ADAM_WD * w)
    return delta, m, v


def reference(x, norm_mix, norm_ffn, norm_final, pool_w_in, pool_w_group, pool_scale, pool_w_out, attn_w_qkv, attn_w_out, ffn_w_gate, ffn_w_up, ffn_w_down, loss_target, m_norm_mix, m_norm_ffn, m_norm_final, m_pool_w_in, m_pool_w_group, m_pool_scale, m_pool_w_out, m_attn_w_qkv, m_attn_w_out, m_ffn_w_gate, m_ffn_w_up, m_ffn_w_down, v_norm_mix, v_norm_ffn, v_norm_final, v_pool_w_in, v_pool_w_group, v_pool_scale, v_pool_w_out, v_attn_w_qkv, v_attn_w_out, v_ffn_w_gate, v_ffn_w_up, v_ffn_w_down):
    given = dict(x=x, norm_mix=norm_mix, norm_ffn=norm_ffn, norm_final=norm_final, pool_w_in=pool_w_in, pool_w_group=pool_w_group, pool_scale=pool_scale, pool_w_out=pool_w_out, attn_w_qkv=attn_w_qkv, attn_w_out=attn_w_out, ffn_w_gate=ffn_w_gate, ffn_w_up=ffn_w_up, ffn_w_down=ffn_w_down, loss_target=loss_target, m_norm_mix=m_norm_mix, m_norm_ffn=m_norm_ffn, m_norm_final=m_norm_final, m_pool_w_in=m_pool_w_in, m_pool_w_group=m_pool_w_group, m_pool_scale=m_pool_scale, m_pool_w_out=m_pool_w_out, m_attn_w_qkv=m_attn_w_qkv, m_attn_w_out=m_attn_w_out, m_ffn_w_gate=m_ffn_w_gate, m_ffn_w_up=m_ffn_w_up, m_ffn_w_down=m_ffn_w_down, v_norm_mix=v_norm_mix, v_norm_ffn=v_norm_ffn, v_norm_final=v_norm_final, v_pool_w_in=v_pool_w_in, v_pool_w_group=v_pool_w_group, v_pool_scale=v_pool_scale, v_pool_w_out=v_pool_w_out, v_attn_w_qkv=v_attn_w_qkv, v_attn_w_out=v_attn_w_out, v_ffn_w_gate=v_ffn_w_gate, v_ffn_w_up=v_ffn_w_up, v_ffn_w_down=v_ffn_w_down)
    weights = {n: given[n] for n in TWIN_WEIGHTS}
    shared = {n: given[n] for n in SHARED_INPUTS}
    per_example = {n: given[n] for n in ['x']}
    grad_fn = _jax.value_and_grad(_loss, argnums=(0, 1))

    def one_microbatch(ex, loss_target):
        ex = dict(ex)
        diff = ex.pop(TWIN_DIFF_INPUT)
        return grad_fn(weights, diff, {**shared, **ex}, loss_target)

    if N_MICROBATCH == 1:
        loss, (grad_w, grad_x) = one_microbatch(per_example, given["loss_target"])
    else:
        def body(carry, xs):
            loss_sum, grad_sum = carry
            l_k, (gw_k, gx_k) = one_microbatch(xs[0], xs[1])
            with _jax.named_scope("update"):
                return (loss_sum + l_k, _jax.tree.map(_jnp.add, grad_sum, gw_k)), gx_k

        init = (_jnp.zeros((), _jnp.float32), _jax.tree.map(_jnp.zeros_like, weights))
        (loss, grad_w), grad_x = _jax.lax.scan(body, init, (per_example, given["loss_target"]))
    with _jax.named_scope("update"):
        delta_w, new_m, new_v = {}, {}, {}
        for n in TWIN_WEIGHTS:
            delta_w[n], new_m[n], new_v[n] = _adamw(weights[n], grad_w[n], given["m_" + n], given["v_" + n])
    return (loss, grad_x, *[grad_w[n] for n in TWIN_WEIGHTS], *[delta_w[n] for n in TWIN_WEIGHTS],
            *[new_m[n] for n in TWIN_WEIGHTS], *[new_v[n] for n in TWIN_WEIGHTS])
```

```python
import functools
import math

import jax
import jax.numpy as jnp
from jax import lax
from jax.experimental import pallas as pl
from jax.experimental.pallas import tpu as pltpu

F32 = jnp.float32
BF16 = jnp.bfloat16

D_MODEL = 1024
N_SHARDS = 4
N_DEV = 8
D_FF = 2816
FF_SHARD = D_FF // N_SHARDS
N_HEADS = 16
HEAD_DIM = 64
QKV_SHARD = 3 * D_MODEL // N_SHARDS
POOL_WINDOWS = (2, 4, 8, 16)
POOL_GROUP_DIM = 256
POOL_HALO = 16
ATTN_W = 128
GROUP_LANES = (0, 384, 704, 1024)
GROUP_HEADS = (6, 5, 5)
ROPE_THETA = 10000.0
EPS = 1e-6
NEG_INF = -1e30
LANE = 128
VMEM_LIMIT_BYTES = 60 * 1024 * 1024

ADAM_LR = 0.001
ADAM_B1 = 0.9
ADAM_B2 = 0.999
ADAM_EPS = 1e-08
ADAM_WD = 0.01
ADAM_STEP = 10

NT_DIMS = (((1,), (1,)), ((), ()))
TN_DIMS = (((0,), (0,)), ((), ()))
MESH = pl.DeviceIdType.MESH


def _pcall(body, *, name, out_shape, grid=None, in_specs=None, out_specs=None, scratch_shapes=(),
           semantics=None, aliases=None):
    kw = {}
    if grid is not None:
        kw["grid"] = grid
    if in_specs is not None:
        kw["in_specs"] = in_specs
    if out_specs is not None:
        kw["out_specs"] = out_specs
    if aliases:
        kw["input_output_aliases"] = aliases
    return pl.pallas_call(
        body, name=name, out_shape=out_shape, scratch_shapes=list(scratch_shapes),
        compiler_params=pltpu.CompilerParams(dimension_semantics=semantics, vmem_limit_bytes=VMEM_LIMIT_BYTES),
        **kw)


def _sds(shape, dtype):
    return jax.ShapeDtypeStruct(tuple(shape), dtype)


def _dot(a, b):
    return jnp.dot(a, b, preferred_element_type=F32)


def _dot_nt(a, b):
    return lax.dot_general(a, b, NT_DIMS, preferred_element_type=F32)


def _dot_tn(a, b):
    return lax.dot_general(a, b, TN_DIMS, preferred_element_type=F32)


def _rms_fwd(x, g):
    r = lax.rsqrt(jnp.mean(x * x, axis=-1, keepdims=True) + EPS)
    return x * r * g


def _rms_bwd(dh, x, g):
    r = lax.rsqrt(jnp.mean(x * x, axis=-1, keepdims=True) + EPS)
    xh = x * r
    dg = jnp.sum(dh * xh, axis=0, keepdims=True)
    dxh = dh * g
    dx = r * (dxh - xh * jnp.mean(dxh * xh, axis=-1, keepdims=True))
    return dx, dg


def _sigmoid(x):
    return 1.0 / (1.0 + jnp.exp(-x))


def _tile_rows(t):
    return min(512, t)


def pool_fwd(x, g_row, w_in, w_grp, scale, w_out):
    t_len = x.shape[0]
    tm = _tile_rows(t_len)

    def body(x_ref, g_ref, win_ref, wgrp_ref, scale_ref, wout_ref,
             h_ref, p_ref, zr_ref, z_ref, xo_ref, ubuf):
        t = pl.program_id(0)

        @pl.when(t == 0)
        def _():
            ubuf[pl.ds(0, POOL_HALO), :] = jnp.zeros((POOL_HALO, D_MODEL), F32)

        x_t = x_ref[...]
        h = _rms_fwd(x_t, g_ref[...]).astype(BF16)
        h_ref[...] = h
        ubuf[pl.ds(POOL_HALO, tm), :] = _dot(h, win_ref[...])
        row = t * tm + lax.broadcasted_iota(jnp.int32, (tm, 1), 0)
        for gi, w in enumerate(POOL_WINDOWS):
            cols = pl.ds(gi * POOL_GROUP_DIM, POOL_GROUP_DIM)
            u_g = ubuf[pl.ds(POOL_HALO, tm), cols]
            acc = u_g
            for j in range(1, w):
                acc = acc + ubuf[pl.ds(POOL_HALO - j, tm), cols]
            cnt = jnp.minimum(row + 1, w).astype(F32)
            p_g = (acc / cnt - u_g).astype(BF16)
            p_ref[:, cols] = p_g
            z_g = _dot(p_g, wgrp_ref[gi])
            zr_ref[:, cols] = z_g.astype(BF16)
            z_ref[:, cols] = (z_g * scale_ref[:, cols]).astype(BF16)
        ubuf[pl.ds(0, POOL_HALO), :] = ubuf[pl.ds(tm, POOL_HALO), :]
        xo_ref[...] = x_t + _dot(z_ref[...], wout_ref[...])

    row_spec = pl.BlockSpec((tm, D_MODEL), lambda t: (t, 0))
    full2 = lambda shape: pl.BlockSpec(shape, lambda t: (0,) * len(shape))
    return _pcall(
        body, name="pool_fwd", grid=(t_len // tm,),
        in_specs=[row_spec, full2((1, D_MODEL)), full2((D_MODEL, D_MODEL)),
                  full2((4, POOL_GROUP_DIM, POOL_GROUP_DIM)), full2((1, D_MODEL)), full2((D_MODEL, D_MODEL))],
        out_specs=[row_spec] * 5,
        out_shape=[_sds((t_len, D_MODEL), BF16)] * 4 + [_sds((t_len, D_MODEL), F32)],
        scratch_shapes=[pltpu.VMEM((tm + POOL_HALO, D_MODEL), F32)],
        semantics=("arbitrary",),
    )(x, g_row, w_in, w_grp, scale, w_out)


def ffn_fwd(x, g_row, w_gate, w_up, w_down, name):
    t_len = x.shape[0]
    tm = _tile_rows(t_len)

    def body(x_ref, g_ref, wg_ref, wu_ref, wd_ref, h_ref, go_ref, uo_ref, xo_ref, hbuf, acc):
        s = pl.program_id(1)

        @pl.when(s == 0)
        def _():
            h = _rms_fwd(x_ref[...], g_ref[...]).astype(BF16)
            hbuf[...] = h
            h_ref[...] = h
            acc[...] = jnp.zeros_like(acc)

        h = hbuf[...]
        gate = _dot(h, wg_ref[...])
        up = _dot(h, wu_ref[...])
        go_ref[...] = gate.astype(BF16)
        uo_ref[...] = up.astype(BF16)
        act = (gate * _sigmoid(gate) * up).astype(BF16)
        acc[...] += _dot(act, wd_ref[...])

        @pl.when(s == N_SHARDS - 1)
        def _():
            xo_ref[...] = x_ref[...] + acc[...]

    row_spec = pl.BlockSpec((tm, D_MODEL), lambda t, s: (t, 0))
    col_w = pl.BlockSpec((None, D_MODEL, FF_SHARD), lambda t, s: (s, 0, 0))
    row_w = pl.BlockSpec((None, FF_SHARD, D_MODEL), lambda t, s: (s, 0, 0))
    act_spec = pl.BlockSpec((None, tm, FF_SHARD), lambda t, s: (s, t, 0))
    return _pcall(
        body, name=name, grid=(t_len // tm, N_SHARDS),
        in_specs=[row_spec, pl.BlockSpec((1, D_MODEL), lambda t, s: (0, 0)), col_w, col_w, row_w],
        out_specs=[row_spec, act_spec, act_spec, row_spec],
        out_shape=[_sds((t_len, D_MODEL), BF16), _sds((N_SHARDS, t_len, FF_SHARD), BF16),
                   _sds((N_SHARDS, t_len, FF_SHARD), BF16), _sds((t_len, D_MODEL), F32)],
        scratch_shapes=[pltpu.VMEM((tm, D_MODEL), BF16), pltpu.VMEM((tm, D_MODEL), F32)],
        semantics=("arbitrary", "arbitrary"),
    )(x, g_row, w_gate, w_up, w_down)


def ffn_bwd(dxo, x, g_row, gate, up, w_gate, w_up, w_down, name):
    t_len = x.shape[0]
    tm = min(256, t_len)

    def body(dxo_ref, x_ref, g_ref, gate_ref, up_ref, wg_ref, wu_ref, wd_ref,
             act_ref, dg_ref, du_ref, dx_ref, dn_ref, dxb, dh):
        t = pl.program_id(0)
        s = pl.program_id(1)

        @pl.when(s == 0)
        def _():
            dxb[...] = dxo_ref[...].astype(BF16)
            dh[...] = jnp.zeros_like(dh)

        @pl.when(jnp.logical_and(s == 0, t == 0))
        def _():
            dn_ref[...] = jnp.zeros_like(dn_ref)

        dact = _dot_nt(dxb[...], wd_ref[...])
        gv = gate_ref[...].astype(F32)
        uv = up_ref[...].astype(F32)
        sg = _sigmoid(gv)
        silu = gv * sg
        act_ref[...] = (silu * uv).astype(BF16)
        dgv = (dact * uv * (sg * (1.0 + gv * (1.0 - sg)))).astype(BF16)
        duv = (dact * silu).astype(BF16)
        dg_ref[...] = dgv
        du_ref[...] = duv
        dh[...] += _dot_nt(dgv, wg_ref[...]) + _dot_nt(duv, wu_ref[...])

        @pl.when(s == N_SHARDS - 1)
        def _():
            dx, dn = _rms_bwd(dh[...], x_ref[...], g_ref[...])
            dx_ref[...] = dxo_ref[...] + dx
            dn_ref[...] += dn

    row_spec = pl.BlockSpec((tm, D_MODEL), lambda t, s: (t, 0))
    vec_spec = pl.BlockSpec((1, D_MODEL), lambda t, s: (0, 0))
    col_w = pl.BlockSpec((None, D_MODEL, FF_SHARD), lambda t, s: (s, 0, 0))
    row_w = pl.BlockSpec((None, FF_SHARD, D_MODEL), lambda t, s: (s, 0, 0))
    act_spec = pl.BlockSpec((None, tm, FF_SHARD), lambda t, s: (s, t, 0))
    act_shape = _sds((N_SHARDS, t_len, FF_SHARD), BF16)
    return _pcall(
        body, name=name, grid=(t_len // tm, N_SHARDS),
        in_specs=[row_spec, row_spec, vec_spec, act_spec, act_spec, col_w, col_w, row_w],
        out_specs=[act_spec, act_spec, act_spec, row_spec, vec_spec],
        out_shape=[act_shape, act_shape, act_shape, _sds((t_len, D_MODEL), F32), _sds((1, D_MODEL), F32)],
        scratch_shapes=[pltpu.VMEM((tm, D_MODEL), BF16), pltpu.VMEM((tm, D_MODEL), F32)],
        semantics=("arbitrary", "arbitrary"),
    )(dxo, x, g_row, gate, up, w_gate, w_up, w_down)


def tn_matmul(name, a, b, a_spec, b_spec, out_shape, out_spec, grid):
    def body(a_ref, b_ref, o_ref):
        @pl.when(pl.program_id(len(grid) - 1) == 0)
        def _():
            o_ref[...] = jnp.zeros_like(o_ref)

        res = _dot_tn(a_ref[...].astype(BF16), b_ref[...].astype(BF16))
        o_ref[...] += res.reshape(o_ref.shape)

    return _pcall(body, name=name, grid=grid, in_specs=[a_spec, b_spec], out_specs=out_spec,
                  out_shape=out_shape, semantics=("arbitrary",) * len(grid))(a, b)


def wgrad_full(name, a, b):
    t_len, k = a.shape
    n = b.shape[1]
    tt = _tile_rows(t_len)
    return tn_matmul(name, a, b,
                     pl.BlockSpec((tt, k), lambda t: (t, 0)), pl.BlockSpec((tt, n), lambda t: (t, 0)),
                     _sds((k, n), F32), pl.BlockSpec((k, n), lambda t: (0, 0)), (t_len // tt,))


def wgrad_col_sharded(name, a, b_sh):
    t_len, k = a.shape
    n = b_sh.shape[2]
    tt = _tile_rows(t_len)
    return tn_matmul(name, a, b_sh,
                     pl.BlockSpec((tt, k), lambda s, t: (t, 0)), pl.BlockSpec((None, tt, n), lambda s, t: (s, t, 0)),
                     _sds((N_SHARDS, k, n), F32), pl.BlockSpec((None, k, n), lambda s, t: (s, 0, 0)),
                     (N_SHARDS, t_len // tt))


def wgrad_row_sharded(name, a_sh, b):
    t_len, n = b.shape
    k = a_sh.shape[2]
    tt = _tile_rows(t_len)
    return tn_matmul(name, a_sh, b,
                     pl.BlockSpec((None, tt, k), lambda s, t: (s, t, 0)), pl.BlockSpec((tt, n), lambda s, t: (t, 0)),
                     _sds((N_SHARDS, k, n), F32), pl.BlockSpec((None, k, n), lambda s, t: (s, 0, 0)),
                     (N_SHARDS, t_len // tt))


def wgrad_col_blocks(name, a, b, n_blk):
    t_len, k = a.shape
    tt = _tile_rows(t_len)
    return tn_matmul(name, a, b,
                     pl.BlockSpec((tt, k), lambda s, t: (t, 0)), pl.BlockSpec((tt, n_blk), lambda s, t: (t, s)),
                     _sds((N_SHARDS, k, n_blk), F32), pl.BlockSpec((None, k, n_blk), lambda s, t: (s, 0, 0)),
                     (N_SHARDS, t_len // tt))


def wgrad_pool_group(name, p, dzs):
    t_len = p.shape[0]
    tt = _tile_rows(t_len)
    gd = POOL_GROUP_DIM
    rows = gd // N_SHARDS
    return tn_matmul(name, p, dzs,
                     pl.BlockSpec((tt, gd), lambda g, t: (t, g)), pl.BlockSpec((tt, gd), lambda g, t: (t, g)),
                     _sds((N_SHARDS, 4, rows, gd), F32),
                     pl.BlockSpec((N_SHARDS, None, rows, gd), lambda g, t: (0, g, 0, 0)),
                     (4, t_len // tt))


def rope_tables(t_len):
    inv_freq = 1.0 / (ROPE_THETA ** (jnp.arange(0, HEAD_DIM, 2, dtype=F32) / HEAD_DIM))
    ang = jnp.arange(t_len, dtype=F32)[:, None] * inv_freq[None, :]
    ang = jnp.concatenate([ang] * (2 * 256 // HEAD_DIM), axis=-1)
    return jnp.cos(ang), jnp.sin(ang)


def _rot_half(v):
    n = v.shape[1]
    lane = lax.broadcasted_iota(jnp.int32, v.shape, 1)
    return jnp.where(lane % HEAD_DIM < HEAD_DIM // 2,
                     -pltpu.roll(v, n - HEAD_DIM // 2, 1), pltpu.roll(v, HEAD_DIM // 2, 1))


def qkv_fwd(x, g_row, w_qkv, cos, sin):
    t_len = x.shape[0]
    tm = _tile_rows(t_len)
    nblk = 3 * D_MODEL // 256

    def body(x_ref, g_ref, w_ref, cos_ref, sin_ref, h_ref, o_ref, hbuf):
        nb = pl.program_id(1)

        @pl.when(nb == 0)
        def _():
            h = _rms_fwd(x_ref[...], g_ref[...]).astype(BF16)
            hbuf[...] = h
            h_ref[...] = h

        acc = _dot(hbuf[...], w_ref[...])

        @pl.when(nb < 8)
        def _():
            r = acc * cos_ref[...] + _rot_half(acc) * sin_ref[...]
            sc = jnp.where(nb < 4, HEAD_DIM ** -0.5, 1.0).astype(F32)
            o_ref[...] = (r * sc).astype(BF16)

        @pl.when(nb >= 8)
        def _():
            o_ref[...] = acc.astype(BF16)

    row_spec = pl.BlockSpec((tm, D_MODEL), lambda t, nb: (t, 0))
    tab_spec = pl.BlockSpec((tm, 256), lambda t, nb: (t, 0))
    return _pcall(
        body, name="qkv_fwd", grid=(t_len // tm, nblk),
        in_specs=[row_spec, pl.BlockSpec((1, D_MODEL), lambda t, nb: (0, 0)),
                  pl.BlockSpec((None, D_MODEL, 256), lambda t, nb: (nb // 3, 0, nb % 3)), tab_spec, tab_spec],
        out_specs=[row_spec, pl.BlockSpec((tm, 256), lambda t, nb: (t, nb))],
        out_shape=[_sds((t_len, D_MODEL), BF16), _sds((t_len, 3 * D_MODEL), BF16)],
        scratch_shapes=[pltpu.VMEM((tm, D_MODEL), BF16)],
        semantics=("arbitrary", "arbitrary"),
    )(x, g_row, w_qkv, cos, sin)


def _band_mask(n):
    qi = lax.broadcasted_iota(jnp.int32, (ATTN_W, 2 * ATTN_W), 0)
    kj = lax.broadcasted_iota(jnp.int32, (ATTN_W, 2 * ATTN_W), 1)
    dist = ATTN_W + qi - kj
    return (dist >= 0) & (dist <= ATTN_W) & ((kj >= ATTN_W) | (n > 0))


def _half_masks():
    lane = lax.broadcasted_iota(jnp.int32, (1, LANE), 1)
    return [lane < HEAD_DIM, lane >= HEAD_DIM]


def attn_fwd(qkv, dil, hp0, name):
    t_len = qkv.shape[0]
    l_len = t_len // dil
    nb = l_len // ATTN_W
    qkv_v = qkv.reshape(l_len, dil * 3 * D_MODEL)
    ncol = 3 * D_MODEL // LANE

    def body(q_ref, kp_ref, kc_ref, vp_ref, vc_ref, o_ref, lse_ref):
        n = pl.program_id(2)
        q = q_ref[...]
        k = jnp.concatenate([kp_ref[...], kc_ref[...]], axis=0)
        v = jnp.concatenate([vp_ref[...], vc_ref[...]], axis=0)
        mask = _band_mask(n)
        o = jnp.zeros((ATTN_W, LANE), F32)
        lse = jnp.zeros((ATTN_W, LANE), F32)
        for hm in _half_masks():
            qh = jnp.where(hm, q, jnp.zeros_like(q))
            s = jnp.where(mask, _dot_nt(qh, k), NEG_INF)
            m = jnp.max(s, axis=-1, keepdims=True)
            e = jnp.exp(s - m)
            den = jnp.sum(e, axis=-1, keepdims=True)
            p = (e / den).astype(BF16)
            o = jnp.where(hm, _dot(p, v), o)
            lse = jnp.where(hm, m + jnp.log(den), lse)
        o_ref[...] = o.astype(BF16)
        lse_ref[...] = lse

    blk = (ATTN_W, LANE)
    q_spec = pl.BlockSpec(blk, lambda u, r, n: (n, r * ncol + hp0 + u))
    kp_spec = pl.BlockSpec(blk, lambda u, r, n: (jnp.maximum(n - 1, 0), r * ncol + 8 + hp0 + u))
    kc_spec = pl.BlockSpec(blk, lambda u, r, n: (n, r * ncol + 8 + hp0 + u))
    vp_spec = pl.BlockSpec(blk, lambda u, r, n: (jnp.maximum(n - 1, 0), r * ncol + 16 + hp0 + u))
    vc_spec = pl.BlockSpec(blk, lambda u, r, n: (n, r * ncol + 16 + hp0 + u))
    out_spec = pl.BlockSpec(blk, lambda u, r, n: (n, r * 3 + u))
    o, lse = _pcall(
        body, name=name, grid=(3, dil, nb),
        in_specs=[q_spec, kp_spec, kc_spec, vp_spec, vc_spec],
        out_specs=[out_spec, out_spec],
        out_shape=[_sds((l_len, dil * 384), BF16), _sds((l_len, dil * 384), F32)],
        semantics=("arbitrary", "arbitrary", "arbitrary"),
    )(qkv_v, qkv_v, qkv_v, qkv_v, qkv_v)
    return o.reshape(t_len, 384), lse.reshape(t_len, 384)


def _assemble_heads(a, b, c):
    lane = lax.broadcasted_iota(jnp.int32, (1, LANE), 1)
    shared = jnp.where(lane < HEAD_DIM, b[:, 256:384], c[:, 0:128])
    return jnp.concatenate([a, b[:, 0:256], shared, c[:, 128:384]], axis=1)


def _group_stats(lse):
    lane = lax.broadcasted_iota(jnp.int32, (1, D_MODEL), 1)
    gmask = [(lane >= GROUP_LANES[g]) & (lane < GROUP_LANES[g + 1]) for g in range(3)]
    lses, glse = [], []
    for g in range(3):
        mx = jnp.max(jnp.where(gmask[g], lse, -jnp.inf), axis=-1, keepdims=True)
        sm = jnp.sum(jnp.where(gmask[g], jnp.exp(lse - mx), 0.0), axis=-1, keepdims=True) / HEAD_DIM
        full = mx + jnp.log(sm)
        lses.append(full)
        glse.append(full - math.log(GROUP_HEADS[g]))
    top = jnp.maximum(jnp.maximum(glse[0], glse[1]), glse[2])
    ex = [jnp.exp(v - top) for v in glse]
    tot = ex[0] + ex[1] + ex[2]
    alpha = [v / tot for v in ex]
    pick = lambda vals: jnp.where(gmask[0], vals[0], jnp.where(gmask[1], vals[1], vals[2]))
    return gmask, alpha, pick([3.0 * a for a in alpha]), lse - pick(lses), pick


def attn_out_fwd(x, o_parts, lse_parts, w_out):
    t_len = x.shape[0]
    tm = _tile_rows(t_len)

    def body(x_ref, oa, ob, oc, la, lb, lc, w_ref, xo_ref, mg_ref, o_ref, lse_ref):
        o = _assemble_heads(oa[...], ob[...], oc[...])
        lse = _assemble_heads(la[...], lb[...], lc[...])
        o_ref[...] = o
        lse_ref[...] = lse
        _, _, scale, _, _ = _group_stats(lse)
        merged = (o.astype(F32) * scale).astype(BF16)
        mg_ref[...] = merged
        xo_ref[...] = x_ref[...] + _dot(merged, w_ref[...])

    row_spec = pl.BlockSpec((tm, D_MODEL), lambda t: (t, 0))
    part_spec = pl.BlockSpec((tm, 384), lambda t: (t, 0))
    return _pcall(
        body, name="attn_out_fwd", grid=(t_len // tm,),
        in_specs=[row_spec] + [part_spec] * 6 + [pl.BlockSpec((D_MODEL, D_MODEL), lambda t: (0, 0))],
        out_specs=[row_spec] * 4,
        out_shape=[_sds((t_len, D_MODEL), F32), _sds((t_len, D_MODEL), BF16),
                   _sds((t_len, D_MODEL), BF16), _sds((t_len, D_MODEL), F32)],
        semantics=("arbitrary",),
    )(x, *o_parts, *lse_parts, w_out)


def final_fwd_bwd(x, g_row, target):
    t_len = x.shape[0]
    tm = _tile_rows(t_len)

    def body(x_ref, g_ref, tgt_ref, dx_ref, dn_ref, loss_ref):
        @pl.when(pl.program_id(0) == 0)
        def _():
            dn_ref[...] = jnp.zeros_like(dn_ref)
            loss_ref[...] = jnp.zeros_like(loss_ref)

        x_t = x_ref[...]
        g = g_ref[...]
        diff = _rms_fwd(x_t, g) - tgt_ref[...]
        loss_ref[...] += 0.5 * jnp.sum(jnp.mean(diff * diff, axis=-1, keepdims=True), axis=0, keepdims=True)
        dx, dn = _rms_bwd(diff * (1.0 / D_MODEL), x_t, g)
        dx_ref[...] = dx
        dn_ref[...] += dn

    row_spec = pl.BlockSpec((tm, D_MODEL), lambda t: (t, 0))
    vec_spec = pl.BlockSpec((1, D_MODEL), lambda t: (0, 0))
    return _pcall(
        body, name="final_fwd_bwd", grid=(t_len // tm,),
        in_specs=[row_spec, vec_spec, row_spec],
        out_specs=[row_spec, vec_spec, pl.BlockSpec((1, 1), lambda t: (0, 0))],
        out_shape=[_sds((t_len, D_MODEL), F32), _sds((1, D_MODEL), F32), _sds((1, 1), F32)],
        semantics=("arbitrary",),
    )(x, g_row, target)


def attn_out_bwd(dxo, w_out, o, lse):
    t_len = dxo.shape[0]
    tm = _tile_rows(t_len)

    def body(dx_ref, w_ref, o_ref, lse_ref, do_ref, c_ref):
        dmerged = _dot_nt(dx_ref[...].astype(BF16), w_ref[...])
        o_t = o_ref[...].astype(F32)
        gmask, alpha, scale, lse_rel, pick = _group_stats(lse_ref[...])
        e = dmerged * o_t
        dalpha = [3.0 * jnp.sum(jnp.where(gmask[g], e, 0.0), axis=-1, keepdims=True) for g in range(3)]
        mean_da = alpha[0] * dalpha[0] + alpha[1] * dalpha[1] + alpha[2] * dalpha[2]
        dglse = [alpha[g] * (dalpha[g] - mean_da) for g in range(3)]
        dlse = pick(dglse) * jnp.exp(lse_rel)
        do_ref[...] = (dmerged * scale).astype(BF16)
        es = e * scale
        lane = lax.broadcasted_iota(jnp.int32, (1, LANE), 1)
        first = lane < HEAD_DIM
        for hp in range(D_MODEL // LANE):
            cols = pl.ds(hp * LANE, LANE)
            blk = es[:, hp * LANE:(hp + 1) * LANE]
            s0 = jnp.sum(jnp.where(first, blk, 0.0), axis=-1, keepdims=True)
            s1 = jnp.sum(jnp.where(first, 0.0, blk), axis=-1, keepdims=True)
            c_ref[:, cols] = jnp.where(first, s0, s1) - dlse[:, hp * LANE:(hp + 1) * LANE]

    row_spec = pl.BlockSpec((tm, D_MODEL), lambda t: (t, 0))
    return _pcall(
        body, name="attn_out_bwd", grid=(t_len // tm,),
        in_specs=[row_spec, pl.BlockSpec((D_MODEL, D_MODEL), lambda t: (0, 0)), row_spec, row_spec],
        out_specs=[row_spec, row_spec],
        out_shape=[_sds((t_len, D_MODEL), BF16), _sds((t_len, D_MODEL), F32)],
        semantics=("arbitrary",),
    )(dxo, w_out, o, lse)


def attn_bwd(qkv, do, lse, cst, dil, hp0, name):
    t_len = qkv.shape[0]
    l_len = t_len // dil
    nb = l_len // ATTN_W
    qkv_v = qkv.reshape(l_len, dil * 3 * D_MODEL)
    nat = lambda a: a.reshape(l_len, dil * D_MODEL)
    ncol = 3 * D_MODEL // LANE
    ncol1 = D_MODEL // LANE

    def body(q_ref, kp_ref, kc_ref, vp_ref, vc_ref, do_ref, lse_ref, c_ref,
             dq_ref, dk_ref, dv_ref, dk_carry, dv_carry):
        n = pl.program_id(2)

        @pl.when(n == 0)
        def _():
            dk_carry[...] = jnp.zeros_like(dk_carry)
            dv_carry[...] = jnp.zeros_like(dv_carry)

        @pl.when(n < nb)
        def _():
            q = q_ref[...]
            do_t = do_ref[...]
            k = jnp.concatenate([kp_ref[...], kc_ref[...]], axis=0)
            v = jnp.concatenate([vp_ref[...], vc_ref[...]], axis=0)
            mask = _band_mask(n)
            dq = jnp.zeros((ATTN_W, LANE), F32)
            dk = jnp.zeros((2 * ATTN_W, LANE), F32)
            dv = jnp.zeros((2 * ATTN_W, LANE), F32)
            for hm in _half_masks():
                qh = jnp.where(hm, q, jnp.zeros_like(q))
                doh = jnp.where(hm, do_t, jnp.zeros_like(do_t))
                lse_h = jnp.max(jnp.where(hm, lse_ref[...], -jnp.inf), axis=-1, keepdims=True)
                c_h = jnp.max(jnp.where(hm, c_ref[...], -jnp.inf), axis=-1, keepdims=True)
                s = jnp.where(mask, _dot_nt(qh, k), NEG_INF)
                p = jnp.exp(s - lse_h)
                dp = _dot_nt(doh, v)
                ds = (p * (dp - c_h)).astype(BF16)
                dq = jnp.where(hm, _dot(ds, k), dq)
                dk = dk + _dot_tn(ds, qh)
                dv = dv + _dot_tn(p.astype(BF16), doh)
            dq_ref[...] = dq.astype(BF16)
            dk_ref[...] = (dk_carry[...] + dk[0:ATTN_W]).astype(BF16)
            dv_ref[...] = (dv_carry[...] + dv[0:ATTN_W]).astype(BF16)
            dk_carry[...] = dk[ATTN_W:2 * ATTN_W]
            dv_carry[...] = dv[ATTN_W:2 * ATTN_W]

        @pl.when(n == nb)
        def _():
            dk_ref[...] = dk_carry[...].astype(BF16)
            dv_ref[...] = dv_carry[...].astype(BF16)

    blk = (ATTN_W, LANE)
    cur = lambda n: jnp.minimum(n, nb - 1)
    prev = lambda n: jnp.clip(n - 1, 0, nb - 1)
    q_spec = pl.BlockSpec(blk, lambda u, r, n: (cur(n), r * ncol + hp0 + u))
    kp_spec = pl.BlockSpec(blk, lambda u, r, n: (prev(n), r * ncol + 8 + hp0 + u))
    kc_spec = pl.BlockSpec(blk, lambda u, r, n: (cur(n), r * ncol + 8 + hp0 + u))
    vp_spec = pl.BlockSpec(blk, lambda u, r, n: (prev(n), r * ncol + 16 + hp0 + u))
    vc_spec = pl.BlockSpec(blk, lambda u, r, n: (cur(n), r * ncol + 16 + hp0 + u))
    nat_spec = pl.BlockSpec(blk, lambda u, r, n: (cur(n), r * ncol1 + hp0 + u))
    dq_spec = pl.BlockSpec(blk, lambda u, r, n: (cur(n), r * 3 + u))
    dkv_spec = pl.BlockSpec(blk, lambda u, r, n: (prev(n), r * 3 + u))
    part = _sds((l_len, dil * 384), BF16)
    outs = _pcall(
        body, name=name, grid=(3, dil, nb + 1),
        in_specs=[q_spec, kp_spec, kc_spec, vp_spec, vc_spec, nat_spec, nat_spec, nat_spec],
        out_specs=[dq_spec, dkv_spec, dkv_spec],
        out_shape=[part, part, part],
        scratch_shapes=[pltpu.VMEM(blk, F32), pltpu.VMEM(blk, F32)],
        semantics=("arbitrary", "arbitrary", "arbitrary"),
    )(qkv_v, qkv_v, qkv_v, qkv_v, qkv_v, nat(do), nat(lse), nat(cst))
    return [a.reshape(t_len, 384) for a in outs]


def unrope(dq_parts, dk_parts, dv_parts, cos, sin):
    t_len = cos.shape[0]
    tm = _tile_rows(t_len)

    def body(qa, qb, qc, ka, kb, kc, va, vb, vc, cos_ref, sin_ref, o_ref):
        dq = _assemble_heads(qa[...], qb[...], qc[...]).astype(F32)
        dk = _assemble_heads(ka[...], kb[...], kc[...]).astype(F32)
        o_ref[:, pl.ds(2 * D_MODEL, D_MODEL)] = _assemble_heads(va[...], vb[...], vc[...])
        cos_t = cos_ref[...]
        sin_t = sin_ref[...]
        for j in range(D_MODEL // 256):
            sl = slice(j * 256, (j + 1) * 256)
            dqj = dq[:, sl]
            dkj = dk[:, sl]
            o_ref[:, pl.ds(j * 256, 256)] = (
                (dqj * cos_t - _rot_half(dqj * sin_t)) * (HEAD_DIM ** -0.5)).astype(BF16)
            o_ref[:, pl.ds(D_MODEL + j * 256, 256)] = (dkj * cos_t - _rot_half(dkj * sin_t)).astype(BF16)

    part_spec = pl.BlockSpec((tm, 384), lambda t: (t, 0))
    tab_spec = pl.BlockSpec((tm, 256), lambda t: (t, 0))
    return _pcall(
        body, name="unrope", grid=(t_len // tm,),
        in_specs=[part_spec] * 9 + [tab_spec, tab_spec],
        out_specs=pl.BlockSpec((tm, 3 * D_MODEL), lambda t: (t, 0)),
        out_shape=_sds((t_len, 3 * D_MODEL), BF16),
        semantics=("arbitrary",),
    )(*dq_parts, *dk_parts, *dv_parts, cos, sin)


def qkv_bwd(dqkv, w_qkv, dxo, x, g_row):
    t_len = x.shape[0]
    tm = _tile_rows(t_len)

    def body(dqkv_ref, w_ref, dxo_ref, x_ref, g_ref, dx_ref, dn_ref, dh):
        t = pl.program_id(0)
        s = pl.program_id(1)

        @pl.when(s == 0)
        def _():
            dh[...] = jnp.zeros_like(dh)

        @pl.when(jnp.logical_and(s == 0, t == 0))
        def _():
            dn_ref[...] = jnp.zeros_like(dn_ref)

        dh[...] += _dot_nt(dqkv_ref[...], w_ref[...])

        @pl.when(s == N_SHARDS - 1)
        def _():
            dx, dn = _rms_bwd(dh[...], x_ref[...], g_ref[...])
            dx_ref[...] = dxo_ref[...] + dx
            dn_ref[...] += dn

    row_spec = pl.BlockSpec((tm, D_MODEL), lambda t, s: (t, 0))
    vec_spec = pl.BlockSpec((1, D_MODEL), lambda t, s: (0, 0))
    return _pcall(
        body, name="qkv_bwd", grid=(t_len // tm, N_SHARDS),
        in_specs=[pl.BlockSpec((tm, QKV_SHARD), lambda t, s: (t, s)),
                  pl.BlockSpec((None, D_MODEL, QKV_SHARD), lambda t, s: (s, 0, 0)), row_spec, row_spec, vec_spec],
        out_specs=[row_spec, vec_spec],
        out_shape=[_sds((t_len, D_MODEL), F32), _sds((1, D_MODEL), F32)],
        scratch_shapes=[pltpu.VMEM((tm, D_MODEL), F32)],
        semantics=("arbitrary", "arbitrary"),
    )(dqkv, w_qkv, dxo, x, g_row)


def pool_bwd(dxo, x, g_row, w_in, w_grp, scale, w_out, zr):
    t_len = x.shape[0]
    tm = _tile_rows(t_len)
    nt = t_len // tm

    def body(dxo_ref, x_ref, g_ref, win_ref, wgrp_ref, scale_ref, wout_ref, zr_ref,
             dzs_ref, du_ref, dx_ref, dn_ref, dsc_ref, ebuf):
        i = pl.program_id(0)
        t = nt - 1 - i

        @pl.when(i == 0)
        def _():
            ebuf[pl.ds(tm, POOL_HALO), :] = jnp.zeros((POOL_HALO, D_MODEL), F32)
            dn_ref[...] = jnp.zeros_like(dn_ref)
            dsc_ref[...] = jnp.zeros_like(dsc_ref)

        dxo_t = dxo_ref[...]
        dz = _dot_nt(dxo_t.astype(BF16), wout_ref[...])
        dsc_ref[...] += jnp.sum(dz * zr_ref[...].astype(F32), axis=0, keepdims=True)
        dzs_ref[...] = (dz * scale_ref[...]).astype(BF16)
        row = t * tm + lax.broadcasted_iota(jnp.int32, (tm, 1), 0)
        for gi, w in enumerate(POOL_WINDOWS):
            cols = pl.ds(gi * POOL_GROUP_DIM, POOL_GROUP_DIM)
            dp_g = _dot_nt(dzs_ref[:, cols], wgrp_ref[gi])
            cnt = jnp.minimum(row + 1, w).astype(F32)
            ebuf[pl.ds(0, tm), cols] = dp_g / cnt
            acc = -dp_g
            for j in range(w):
                acc = acc + ebuf[pl.ds(j, tm), cols]
            du_ref[:, cols] = acc.astype(BF16)
        ebuf[pl.ds(tm, POOL_HALO), :] = ebuf[pl.ds(0, POOL_HALO), :]
        dh = _dot_nt(du_ref[...], win_ref[...])
        dx, dn = _rms_bwd(dh, x_ref[...], g_ref[...])
        dx_ref[...] = dxo_t + dx
        dn_ref[...] += dn

    row_spec = pl.BlockSpec((tm, D_MODEL), lambda i: (nt - 1 - i, 0))
    full = lambda shape: pl.BlockSpec(shape, lambda i: (0,) * len(shape))
    vec = full((1, D_MODEL))
    return _pcall(
        body, name="pool_bwd", grid=(nt,),
        in_specs=[row_spec, row_spec, vec, full((D_MODEL, D_MODEL)), full((4, POOL_GROUP_DIM, POOL_GROUP_DIM)),
                  vec, full((D_MODEL, D_MODEL)), row_spec],
        out_specs=[row_spec, row_spec, row_spec, vec, vec],
        out_shape=[_sds((t_len, D_MODEL), BF16), _sds((t_len, D_MODEL), BF16), _sds((t_len, D_MODEL), F32),
                   _sds((1, D_MODEL), F32), _sds((1, D_MODEL), F32)],
        scratch_shapes=[pltpu.VMEM((tm + POOL_HALO, D_MODEL), F32)],
        semantics=("arbitrary",),
    )(dxo, x, g_row, w_in, w_grp, scale, w_out, zr)


def _mesh_pos():
    return lax.axis_index("x"), lax.axis_index("y"), lax.axis_index("c")


def _other_chips(x, y):
    return [(1 - x, y), (x, 1 - y), (1 - x, 1 - y)]


def _remote(src, dst, send_sem, recv_sem, device):
    return pltpu.make_async_remote_copy(src_ref=src, dst_ref=dst, send_sem=send_sem, recv_sem=recv_sem,
                                        device_id=device, device_id_type=MESH)


_ANY = pl.BlockSpec(memory_space=pl.ANY)


def _comm_call(body, name, ins, out_shape, scratch_shapes):
    return pl.pallas_call(
        body, name=name, out_shape=out_shape, in_specs=[_ANY] * len(ins), out_specs=[_ANY] * len(out_shape),
        scratch_shapes=scratch_shapes,
        compiler_params=pltpu.CompilerParams(has_side_effects=True),
    )(*ins)


def gather_weights(shards):
    n = len(shards)
    halves = [s.shape[0] // 2 for s in shards]

    def body(*refs):
        ins, outs = refs[:n], refs[n:2 * n]
        send_sems, recv_sems, local_sems = refs[2 * n:]
        x, y, c = _mesh_pos()
        me = 2 * x + y
        chips = _other_chips(x, y)
        sibling = (x, y, 1 - c)
        started, locs = [], []
        for a in range(n):
            h = halves[a]
            mine = pltpu.make_async_copy(ins[a], outs[a].at[me], local_sems.at[a])
            mine.start()
            locs.append(mine)
            for j, chip in enumerate(chips):
                cp = _remote(ins[a].at[pl.ds(c * h, h)], outs[a].at[me, pl.ds(c * h, h)],
                             send_sems.at[a, j], recv_sems.at[a, j], (*chip, c))
                cp.start()
                started.append(cp)
        for a in range(n):
            h = halves[a]
            for j, chip in enumerate(chips):
                slot = outs[a].at[2 * chip[0] + chip[1], pl.ds(c * h, h)]
                _remote(slot, slot, send_sems.at[a, j], recv_sems.at[a, j], (*chip, c)).wait_recv()
                fw = _remote(slot, slot, send_sems.at[a, 3 + j], recv_sems.at[a, 3 + j], sibling)
                fw.start()
                started.append(fw)
        for a in range(n):
            h = halves[a]
            for j, chip in enumerate(chips):
                slot = outs[a].at[2 * chip[0] + chip[1], pl.ds((1 - c) * h, h)]
                _remote(slot, slot, send_sems.at[a, 3 + j], recv_sems.at[a, 3 + j], sibling).wait_recv()
        for cp in started:
            cp.wait_send()
        for lc in locs:
            lc.wait()

    out_shape = [_sds((N_SHARDS,) + s.shape, s.dtype) for s in shards]
    return _comm_call(body, "gather_weights", list(shards), out_shape,
                      [pltpu.SemaphoreType.DMA((n, 6)), pltpu.SemaphoreType.DMA((n, 6)),
                       pltpu.SemaphoreType.DMA((n,))])


def sibling_send_halves(grads):
    n = len(grads)
    halves = [g.shape[1] // 2 for g in grads]

    def body(*refs):
        ins, outs = refs[:n], refs[n:2 * n]
        send_sems, recv_sems = refs[2 * n:]
        x, y, c = _mesh_pos()
        sibling = (x, y, 1 - c)
        cps = []
        for a in range(n):
            h = halves[a]
            cp = _remote(ins[a].at[:, pl.ds((1 - c) * h, h)], outs[a], send_sems.at[a], recv_sems.at[a], sibling)
            cp.start()
            cps.append(cp)
        for cp in cps:
            cp.wait()

    out_shape = [_sds((N_SHARDS, h, g.shape[2]), g.dtype) for g, h in zip(grads, halves)]
    return _comm_call(body, "rs_sibling_halves", list(grads), out_shape,
                      [pltpu.SemaphoreType.DMA((n,)), pltpu.SemaphoreType.DMA((n,))])


def chip_exchange(parts):
    n = len(parts)

    def body(*refs):
        ins, outs = refs[:n], refs[n:2 * n]
        send_sems, recv_sems, local_sems = refs[2 * n:]
        x, y, c = _mesh_pos()
        me = 2 * x + y
        chips = _other_chips(x, y)
        locs, cps = [], []
        for a in range(n):
            lc = pltpu.make_async_copy(ins[a].at[me], outs[a].at[me], local_sems.at[a])
            lc.start()
            locs.append(lc)
            for j, chip in enumerate(chips):
                cp = _remote(ins[a].at[2 * chip[0] + chip[1]], outs[a].at[me],
                             send_sems.at[a, j], recv_sems.at[a, j], (*chip, c))
                cp.start()
                cps.append(cp)
        for a in range(n):
            for j, chip in enumerate(chips):
                slot = outs[a].at[2 * chip[0] + chip[1]]
                _remote(slot, slot, send_sems.at[a, j], recv_sems.at[a, j], (*chip, c)).wait_recv()
        for cp in cps:
            cp.wait_send()
        for lc in locs:
            lc.wait()

    out_shape = [_sds(p.shape, p.dtype) for p in parts]
    return _comm_call(body, "rs_chip_exchange", list(parts), out_shape,
                      [pltpu.SemaphoreType.DMA((n, 3)), pltpu.SemaphoreType.DMA((n, 3)),
                       pltpu.SemaphoreType.DMA((n,))])


def sibling_share(reduced):
    n = len(reduced)

    def body(*refs):
        ins, outs = refs[:n], refs[n:2 * n]
        send_sems, recv_sems, local_sems = refs[2 * n:]
        x, y, c = _mesh_pos()
        sibling = (x, y, 1 - c)
        locs, cps = [], []
        for a in range(n):
            h = ins[a].shape[0]
            lc = pltpu.make_async_copy(ins[a], outs[a].at[pl.ds(c * h, h)], local_sems.at[a])
            lc.start()
            locs.append(lc)
            cp = _remote(ins[a], outs[a].at[pl.ds(c * h, h)], send_sems.at[a], recv_sems.at[a], sibling)
            cp.start()
            cps.append(cp)
        for a in range(n):
            h = ins[a].shape[0]
            theirs = outs[a].at[pl.ds((1 - c) * h, h)]
            _remote(theirs, theirs, send_sems.at[a], recv_sems.at[a], sibling).wait_recv()
        for cp in cps:
            cp.wait_send()
        for lc in locs:
            lc.wait()

    out_shape = [_sds((2 * r.shape[0], r.shape[1]), r.dtype) for r in reduced]
    return _comm_call(body, "rs_sibling_share", list(reduced), out_shape,
                      [pltpu.SemaphoreType.DMA((n,)), pltpu.SemaphoreType.DMA((n,)),
                       pltpu.SemaphoreType.DMA((n,))])


def allreduce_small(v):
    def body(v_ref, o_ref, buf, send_sems, recv_sems):
        x, y, c = _mesh_pos()
        me = 4 * x + 2 * y + c
        buf[me] = v_ref[...]
        flip = lambda p, f: 1 - p if f else p
        peers = [(flip(x, k & 4), flip(y, k & 2), flip(c, k & 1)) for k in range(1, N_DEV)]
        cps = []
        for k, peer in enumerate(peers):
            cp = _remote(v_ref, buf.at[me], send_sems.at[k], recv_sems.at[k], peer)
            cp.start()
            cps.append(cp)
        for k, peer in enumerate(peers):
            slot = buf.at[4 * peer[0] + 2 * peer[1] + peer[2]]
            _remote(slot, slot, send_sems.at[k], recv_sems.at[k], peer).wait_recv()
        for cp in cps:
            cp.wait_send()
        acc = buf[0]
        for i in range(1, N_DEV):
            acc = acc + buf[i]
        o_ref[...] = acc

    vm = pl.BlockSpec(memory_space=pltpu.VMEM)
    return pl.pallas_call(
        body, name="allreduce_small", out_shape=_sds(v.shape, v.dtype), in_specs=[vm], out_specs=vm,
        scratch_shapes=[pltpu.VMEM((N_DEV,) + v.shape, v.dtype),
                        pltpu.SemaphoreType.DMA((N_DEV - 1,)), pltpu.SemaphoreType.DMA((N_DEV - 1,))],
        compiler_params=pltpu.CompilerParams(has_side_effects=True),
    )(v)


def add_my_half(grad, theirs, c_idx, name):
    _, r, cols = grad.shape
    h = r // 2

    def body(c_ref, g_ref, t_ref, o_ref):
        o_ref[...] = (g_ref[...] + t_ref[...]).astype(BF16)

    slot = pl.BlockSpec((None, h, cols), lambda s, c: (s, 0, 0))
    grid_spec = pltpu.PrefetchScalarGridSpec(
        num_scalar_prefetch=1, grid=(N_SHARDS,),
        in_specs=[pl.BlockSpec((None, h, cols), lambda s, c: (s, c[0], 0)), slot], out_specs=slot)
    return pl.pallas_call(
        body, name=name, grid_spec=grid_spec, out_shape=_sds((N_SHARDS, h, cols), BF16),
        compiler_params=pltpu.CompilerParams(dimension_semantics=("arbitrary",), vmem_limit_bytes=VMEM_LIMIT_BYTES),
    )(c_idx, grad, theirs)


def sum_slots(parts, name):
    _, h, cols = parts.shape

    def body(p_ref, o_ref):
        acc = p_ref[0].astype(F32)
        for k in range(1, N_SHARDS):
            acc = acc + p_ref[k].astype(F32)
        o_ref[...] = acc

    return _pcall(body, name=name, out_shape=_sds((h, cols), F32))(parts)


def adamw(name, grads, w, m, v):
    n_layers, r, cols = w.shape
    tr = r // 2 if r % 16 == 0 else r
    bias1 = 1.0 - ADAM_B1 ** ADAM_STEP
    bias2 = 1.0 - ADAM_B2 ** ADAM_STEP

    def body(*refs):
        g_refs = refs[:n_layers]
        w_ref, m_ref, v_ref, go_ref, d_ref, mo_ref, vo_ref = refs[n_layers:]
        g = g_refs[0][...]
        for layer in range(1, n_layers):
            g = jnp.where(pl.program_id(0) == layer, g_refs[layer][...], g)
        m_new = ADAM_B1 * m_ref[...] + (1.0 - ADAM_B1) * g
        v_new = ADAM_B2 * v_ref[...] + (1.0 - ADAM_B2) * (g * g)
        m_hat = m_new / bias1
        v_hat = v_new / bias2
        go_ref[...] = g
        d_ref[...] = -ADAM_LR * (m_hat / (jnp.sqrt(v_hat) + ADAM_EPS) + ADAM_WD * w_ref[...])
        mo_ref[...] = m_new
        vo_ref[...] = v_new

    g_spec = pl.BlockSpec((tr, cols), lambda l, i: (i, 0))
    lay_spec = pl.BlockSpec((None, tr, cols), lambda l, i: (l, i, 0))
    shape = _sds((n_layers, r, cols), F32)
    return _pcall(
        body, name=name, grid=(n_layers, r // tr),
        in_specs=[g_spec] * n_layers + [lay_spec] * 3, out_specs=[lay_spec] * 4,
        out_shape=[shape] * 4, semantics=("arbitrary", "arbitrary"),
    )(*grads, w, m, v)


def kernel(x, norm_mix, norm_ffn, norm_final, pool_w_in, pool_w_group, pool_scale, pool_w_out, attn_w_qkv, attn_w_out, ffn_w_gate, ffn_w_up, ffn_w_down, loss_target, m_norm_mix, m_norm_ffn, m_norm_final, m_pool_w_in, m_pool_w_group, m_pool_scale, m_pool_w_out, m_attn_w_qkv, m_attn_w_out, m_ffn_w_gate, m_ffn_w_up, m_ffn_w_down, v_norm_mix, v_norm_ffn, v_norm_final, v_pool_w_in, v_pool_w_group, v_pool_scale, v_pool_w_out, v_attn_w_qkv, v_attn_w_out, v_ffn_w_gate, v_ffn_w_up, v_ffn_w_down):
    t_len = x.shape[1]
    x0 = x.reshape(t_len, D_MODEL)
    target = loss_target.reshape(t_len, D_MODEL)
    row = lambda a: a.reshape(1, D_MODEL)

    grp_rows = POOL_GROUP_DIM // N_SHARDS
    shards = [
        pool_w_in[0], pool_w_group[0].reshape(4 * grp_rows, POOL_GROUP_DIM), pool_w_out[0],
        attn_w_qkv[0], attn_w_out[0],
        ffn_w_gate[0], ffn_w_gate[1], ffn_w_up[0], ffn_w_up[1], ffn_w_down[0], ffn_w_down[1],
    ]
    gathered = gather_weights([s.astype(BF16) for s in shards])
    w_in = gathered[0].reshape(D_MODEL, D_MODEL)
    w_grp = gathered[1].reshape(N_SHARDS, 4, grp_rows, POOL_GROUP_DIM).transpose(1, 0, 2, 3).reshape(
        4, POOL_GROUP_DIM, POOL_GROUP_DIM)
    w_out = gathered[2].reshape(D_MODEL, D_MODEL)
    w_qkv = gathered[3]
    w_ao = gathered[4].reshape(D_MODEL, D_MODEL)
    w_gate, w_up, w_down = gathered[5:7], gathered[7:9], gathered[9:11]
    cos, sin = rope_tables(t_len)
    dilations = ((1, 0), (4, 3), (16, 5))

    h0, p, zr, z, x1 = pool_fwd(x0, row(norm_mix[0]), w_in, w_grp, pool_scale, w_out)
    h1, gate0, up0, x2 = ffn_fwd(x1, row(norm_ffn[0]), w_gate[0], w_up[0], w_down[0], "ffn_fwd0")
    h2, qkv = qkv_fwd(x2, row(norm_mix[1]), w_qkv, cos, sin)
    o_parts, lse_parts = [], []
    for dil, hp0 in dilations:
        o_d, lse_d = attn_fwd(qkv, dil, hp0, f"attn_fwd_d{dil}")
        o_parts.append(o_d)
        lse_parts.append(lse_d)
    x3, merged, o_nat, lse_nat = attn_out_fwd(x2, o_parts, lse_parts, w_ao)
    h3, gate1, up1, x4 = ffn_fwd(x3, row(norm_ffn[1]), w_gate[1], w_up[1], w_down[1], "ffn_fwd1")
    dx4, d_norm_final, loss_local = final_fwd_bwd(x4, row(norm_final), target)

    act1, dgate1, dup1, dx3, d_nf1 = ffn_bwd(dx4, x3, row(norm_ffn[1]), gate1, up1,
                                             w_gate[1], w_up[1], w_down[1], "ffn_bwd1")
    g_gate1 = wgrad_col_sharded("wgrad_gate1", h3, dgate1)
    g_up1 = wgrad_col_sharded("wgrad_up1", h3, dup1)
    g_down1 = wgrad_row_sharded("wgrad_down1", act1, dx4)

    do, cst = attn_out_bwd(dx3, w_ao, o_nat, lse_nat)
    g_ao = wgrad_full("wgrad_attn_out", merged, dx3)
    dq_parts, dk_parts, dv_parts = [], [], []
    for dil, hp0 in dilations:
        dq_d, dk_d, dv_d = attn_bwd(qkv, do, lse_nat, cst, dil, hp0, f"attn_bwd_d{dil}")
        dq_parts.append(dq_d)
        dk_parts.append(dk_d)
        dv_parts.append(dv_d)
    dqkv = unrope(dq_parts, dk_parts, dv_parts, cos, sin)
    dx2, d_nm1 = qkv_bwd(dqkv, w_qkv, dx3, x2, row(norm_mix[1]))
    g_qkv = wgrad_col_blocks("wgrad_qkv", h2, dqkv, QKV_SHARD)

    act0, dgate0, dup0, dx1, d_nf0 = ffn_bwd(dx2, x1, row(norm_ffn[0]), gate0, up0,
                                             w_gate[0], w_up[0], w_down[0], "ffn_bwd0")
    g_gate0 = wgrad_col_sharded("wgrad_gate0", h1, dgate0)
    g_up0 = wgrad_col_sharded("wgrad_up0", h1, dup0)
    g_down0 = wgrad_row_sharded("wgrad_down0", act0, dx2)

    dzs, du, dx0, d_nm0, d_scale = pool_bwd(dx1, x0, row(norm_mix[0]), w_in, w_grp, pool_scale, w_out, zr)
    g_out = wgrad_full("wgrad_pool_out", z, dx1)
    g_grp = wgrad_pool_group("wgrad_pool_group", p, dzs)
    g_in = wgrad_full("wgrad_pool_in", h0, du)

    shard_grads = [
        g_in.reshape(N_SHARDS, D_MODEL // N_SHARDS, D_MODEL),
        g_grp.reshape(N_SHARDS, 4 * grp_rows, POOL_GROUP_DIM),
        g_out.reshape(N_SHARDS, D_MODEL // N_SHARDS, D_MODEL),
        g_qkv,
        g_ao.reshape(N_SHARDS, D_MODEL // N_SHARDS, D_MODEL),
        g_gate0, g_gate1, g_up0, g_up1, g_down0, g_down1,
    ]
    names = ["pool_in", "pool_group", "pool_out", "qkv", "attn_out",
             "gate0", "gate1", "up0", "up1", "down0", "down1"]
    c_idx = lax.axis_index("c").astype(jnp.int32).reshape(1)
    theirs = sibling_send_halves(shard_grads)
    partials = [add_my_half(g, t, c_idx, f"rs_add_{nm}") for g, t, nm in zip(shard_grads, theirs, names)]
    received = chip_exchange(partials)
    reduced = [sum_slots(r, f"rs_sum_{nm}") for r, nm in zip(received, names)]
    full = sibling_share(reduced)

    zero_row = jnp.zeros((1, D_MODEL), F32)
    small = jnp.concatenate([d_nm0, d_nm1, d_nf0, d_nf1, d_norm_final, d_scale,
                             jnp.broadcast_to(loss_local, (1, D_MODEL)), zero_row], axis=0)
    small = allreduce_small(small)
    loss = small[6, 0]

    pack = lambda a, b, c, d: jnp.concatenate([a, b, row(c), d, zero_row, zero_row], axis=0)[None]
    sg, sd, sm, sv = adamw("adamw_small", [small],
                           pack(norm_mix, norm_ffn, norm_final, pool_scale),
                           pack(m_norm_mix, m_norm_ffn, m_norm_final, m_pool_scale),
                           pack(v_norm_mix, v_norm_ffn, v_norm_final, v_pool_scale))
    unpack = lambda a: (a[0, 0:2], a[0, 2:4], a[0, 4], a[0, 5:6])

    def update(name, grads, w, m, v):
        n_layers = len(grads)
        shp = (n_layers,) + grads[0].shape
        outs = adamw(name, grads, w.reshape(shp), m.reshape(shp), v.reshape(shp))
        return [o.reshape(w.shape) for o in outs]

    big = [
        update("adamw_pool_in", [full[0]], pool_w_in, m_pool_w_in, v_pool_w_in),
        update("adamw_pool_group", [full[1]], pool_w_group, m_pool_w_group, v_pool_w_group),
        update("adamw_pool_out", [full[2]], pool_w_out, m_pool_w_out, v_pool_w_out),
        update("adamw_qkv", [full[3]], attn_w_qkv, m_attn_w_qkv, v_attn_w_qkv),
        update("adamw_attn_out", [full[4]], attn_w_out, m_attn_w_out, v_attn_w_out),
        update("adamw_gate", [full[5], full[6]], ffn_w_gate, m_ffn_w_gate, v_ffn_w_gate),
        update("adamw_up", [full[7], full[8]], ffn_w_up, m_ffn_w_up, v_ffn_w_up),
        update("adamw_down", [full[9], full[10]], ffn_w_down, m_ffn_w_down, v_ffn_w_down),
    ]

    def leaves(k, small_vals):
        nm, nf, nfin, psc = unpack(small_vals)
        return [nm, nf, nfin, big[0][k], big[1][k], psc, big[2][k], big[3][k], big[4][k],
                big[5][k], big[6][k], big[7][k]]

    grad_x = dx0.reshape(x.shape)
    return (loss, grad_x, *leaves(0, sg), *leaves(1, sd), *leaves(2, sm), *leaves(3, sv))
```

```python
import math

import jax
import jax.numpy as jnp
from jax import lax
from jax.experimental import pallas as pl
from jax.experimental.pallas import tpu as pltpu

F32 = jnp.float32
BF16 = jnp.bfloat16

D_MODEL = 1024
N_SHARDS = 4
N_DEV = 8
D_FF = 2816
FF_SHARD = D_FF // N_SHARDS
HEAD_DIM = 64
QKV_SHARD = 3 * D_MODEL // N_SHARDS
POOL_WINDOWS = (2, 4, 8, 16)
POOL_GROUP_DIM = 256
POOL_HALO = 16
ATTN_W = 128
GROUP_LANES = (0, 384, 704, 1024)
GROUP_HEADS = (6, 5, 5)
GROUP_DIL = (1, 4, 16)
ROPE_THETA = 10000.0
EPS = 1e-6
NEG_INF = -1e30
LANE = 128
VMEM_LIMIT_BYTES = 60 * 1024 * 1024

ADAM_LR = 0.001
ADAM_B1 = 0.9
ADAM_B2 = 0.999
ADAM_EPS = 1e-08
ADAM_WD = 0.01
ADAM_STEP = 10

NT_DIMS = (((1,), (1,)), ((), ()))
TN_DIMS = (((0,), (0,)), ((), ()))
MESH = pl.DeviceIdType.MESH


def _pcall(body, *, name, out_shape, grid=None, in_specs=None, out_specs=None, scratch_shapes=(),
           semantics=None):
    kw = {}
    if grid is not None:
        kw["grid"] = grid
    if in_specs is not None:
        kw["in_specs"] = in_specs
    if out_specs is not None:
        kw["out_specs"] = out_specs
    return pl.pallas_call(
        body, name=name, out_shape=out_shape, scratch_shapes=list(scratch_shapes),
        compiler_params=pltpu.CompilerParams(dimension_semantics=semantics, vmem_limit_bytes=VMEM_LIMIT_BYTES),
        **kw)


def _sds(shape, dtype):
    return jax.ShapeDtypeStruct(tuple(shape), dtype)


def _dot(a, b):
    return jnp.dot(a, b, preferred_element_type=F32)


def _dot_nt(a, b):
    return lax.dot_general(a, b, NT_DIMS, preferred_element_type=F32)


def _dot_tn(a, b):
    return lax.dot_general(a, b, TN_DIMS, preferred_element_type=F32)


def _rms_fwd(x, g):
    r = lax.rsqrt(jnp.mean(x * x, axis=-1, keepdims=True) + EPS)
    return x * r * g


def _rms_bwd(dh, x, g):
    r = lax.rsqrt(jnp.mean(x * x, axis=-1, keepdims=True) + EPS)
    xh = x * r
    dg = jnp.sum(dh * xh, axis=0, keepdims=True)
    dxh = dh * g
    dx = r * (dxh - xh * jnp.mean(dxh * xh, axis=-1, keepdims=True))
    return dx, dg


def _sigmoid(x):
    return 1.0 / (1.0 + jnp.exp(-x))


def _tile_rows(t):
    return min(512, t)


def pool_fwd(x, g_row, w_in, w_grp, scale, w_out):
    t_len = x.shape[0]
    tm = _tile_rows(t_len)

    def body(x_ref, g_ref, win_ref, wgrp_ref, scale_ref, wout_ref,
             h_ref, p_ref, zr_ref, z_ref, xo_ref, ubuf):
        t = pl.program_id(0)

        @pl.when(t == 0)
        def _():
            ubuf[pl.ds(0, POOL_HALO), :] = jnp.zeros((POOL_HALO, D_MODEL), F32)

        x_t = x_ref[...]
        h = _rms_fwd(x_t, g_ref[...]).astype(BF16)
        h_ref[...] = h
        ubuf[pl.ds(POOL_HALO, tm), :] = _dot(h, win_ref[...])
        row = t * tm + lax.broadcasted_iota(jnp.int32, (tm, 1), 0)
        for gi, w in enumerate(POOL_WINDOWS):
            cols = pl.ds(gi * POOL_GROUP_DIM, POOL_GROUP_DIM)
            u_g = ubuf[pl.ds(POOL_HALO, tm), cols]
            acc = u_g
            for j in range(1, w):
                acc = acc + ubuf[pl.ds(POOL_HALO - j, tm), cols]
            cnt = jnp.minimum(row + 1, w).astype(F32)
            p_g = (acc / cnt - u_g).astype(BF16)
            p_ref[:, cols] = p_g
            z_g = _dot(p_g, wgrp_ref[gi])
            zr_ref[:, cols] = z_g.astype(BF16)
            z_ref[:, cols] = (z_g * scale_ref[:, cols]).astype(BF16)
        ubuf[pl.ds(0, POOL_HALO), :] = ubuf[pl.ds(tm, POOL_HALO), :]
        xo_ref[...] = x_t + _dot(z_ref[...], wout_ref[...])

    row_spec = pl.BlockSpec((tm, D_MODEL), lambda t: (t, 0))
    full2 = lambda shape: pl.BlockSpec(shape, lambda t: (0,) * len(shape))
    return _pcall(
        body, name="pool_fwd", grid=(t_len // tm,),
        in_specs=[row_spec, full2((1, D_MODEL)), full2((D_MODEL, D_MODEL)),
                  full2((4, POOL_GROUP_DIM, POOL_GROUP_DIM)), full2((1, D_MODEL)), full2((D_MODEL, D_MODEL))],
        out_specs=[row_spec] * 5,
        out_shape=[_sds((t_len, D_MODEL), BF16)] * 4 + [_sds((t_len, D_MODEL), F32)],
        scratch_shapes=[pltpu.VMEM((tm + POOL_HALO, D_MODEL), F32)],
        semantics=("arbitrary",),
    )(x, g_row, w_in, w_grp, scale, w_out)


def ffn_fwd(x, g_row, w_gate, w_up, w_down, name):
    t_len = x.shape[0]
    tm = _tile_rows(t_len)

    def body(x_ref, g_ref, wg_ref, wu_ref, wd_ref, h_ref, go_ref, uo_ref, xo_ref, hbuf, acc):
        s = pl.program_id(1)

        @pl.when(s == 0)
        def _():
            h = _rms_fwd(x_ref[...], g_ref[...]).astype(BF16)
            hbuf[...] = h
            h_ref[...] = h
            acc[...] = jnp.zeros_like(acc)

        h = hbuf[...]
        gate = _dot(h, wg_ref[...])
        up = _dot(h, wu_ref[...])
        go_ref[...] = gate.astype(BF16)
        uo_ref[...] = up.astype(BF16)
        act = (gate * _sigmoid(gate) * up).astype(BF16)
        acc[...] += _dot(act, wd_ref[...])

        @pl.when(s == N_SHARDS - 1)
        def _():
            xo_ref[...] = x_ref[...] + acc[...]

    row_spec = pl.BlockSpec((tm, D_MODEL), lambda t, s: (t, 0))
    col_w = pl.BlockSpec((None, D_MODEL, FF_SHARD), lambda t, s: (s, 0, 0))
    row_w = pl.BlockSpec((None, FF_SHARD, D_MODEL), lambda t, s: (s, 0, 0))
    act_spec = pl.BlockSpec((None, tm, FF_SHARD), lambda t, s: (s, t, 0))
    return _pcall(
        body, name=name, grid=(t_len // tm, N_SHARDS),
        in_specs=[row_spec, pl.BlockSpec((1, D_MODEL), lambda t, s: (0, 0)), col_w, col_w, row_w],
        out_specs=[row_spec, act_spec, act_spec, row_spec],
        out_shape=[_sds((t_len, D_MODEL), BF16), _sds((N_SHARDS, t_len, FF_SHARD), BF16),
                   _sds((N_SHARDS, t_len, FF_SHARD), BF16), _sds((t_len, D_MODEL), F32)],
        scratch_shapes=[pltpu.VMEM((tm, D_MODEL), BF16), pltpu.VMEM((tm, D_MODEL), F32)],
        semantics=("arbitrary", "arbitrary"),
    )(x, g_row, w_gate, w_up, w_down)


def ffn_bwd(dxo, x, g_row, gate, up, w_gate, w_up, w_down, name):
    t_len = x.shape[0]
    tm = _tile_rows(t_len)

    def body(dxo_ref, x_ref, g_ref, gate_ref, up_ref, wg_ref, wu_ref, wd_ref,
             act_ref, dg_ref, du_ref, dx_ref, dn_ref, dxb, dh):
        t = pl.program_id(0)
        s = pl.program_id(1)

        @pl.when(s == 0)
        def _():
            dxb[...] = dxo_ref[...].astype(BF16)
            dh[...] = jnp.zeros_like(dh)

        @pl.when(jnp.logical_and(s == 0, t == 0))
        def _():
            dn_ref[...] = jnp.zeros_like(dn_ref)

        dact = _dot_nt(dxb[...], wd_ref[...])
        gv = gate_ref[...].astype(F32)
        uv = up_ref[...].astype(F32)
        sg = _sigmoid(gv)
        silu = gv * sg
        act_ref[...] = (silu * uv).astype(BF16)
        dgv = (dact * uv * (sg * (1.0 + gv * (1.0 - sg)))).astype(BF16)
        duv = (dact * silu).astype(BF16)
        dg_ref[...] = dgv
        du_ref[...] = duv
        dh[...] += _dot_nt(dgv, wg_ref[...]) + _dot_nt(duv, wu_ref[...])

        @pl.when(s == N_SHARDS - 1)
        def _():
            dx, dn = _rms_bwd(dh[...], x_ref[...], g_ref[...])
            dx_ref[...] = dxo_ref[...] + dx
            dn_ref[...] += dn

    row_spec = pl.BlockSpec((tm, D_MODEL), lambda t, s: (t, 0))
    vec_spec = pl.BlockSpec((1, D_MODEL), lambda t, s: (0, 0))
    col_w = pl.BlockSpec((None, D_MODEL, FF_SHARD), lambda t, s: (s, 0, 0))
    row_w = pl.BlockSpec((None, FF_SHARD, D_MODEL), lambda t, s: (s, 0, 0))
    act_spec = pl.BlockSpec((None, tm, FF_SHARD), lambda t, s: (s, t, 0))
    act_shape = _sds((N_SHARDS, t_len, FF_SHARD), BF16)
    return _pcall(
        body, name=name, grid=(t_len // tm, N_SHARDS),
        in_specs=[row_spec, row_spec, vec_spec, act_spec, act_spec, col_w, col_w, row_w],
        out_specs=[act_spec, act_spec, act_spec, row_spec, vec_spec],
        out_shape=[act_shape, act_shape, act_shape, _sds((t_len, D_MODEL), F32), _sds((1, D_MODEL), F32)],
        scratch_shapes=[pltpu.VMEM((tm, D_MODEL), BF16), pltpu.VMEM((tm, D_MODEL), F32)],
        semantics=("arbitrary", "arbitrary"),
    )(dxo, x, g_row, gate, up, w_gate, w_up, w_down)


def tn_matmul(name, a, b, a_spec, b_spec, out_shape, out_spec, grid):
    def body(a_ref, b_ref, o_ref):
        @pl.when(pl.program_id(len(grid) - 1) == 0)
        def _():
            o_ref[...] = jnp.zeros_like(o_ref)

        res = _dot_tn(a_ref[...].astype(BF16), b_ref[...].astype(BF16))
        o_ref[...] += res.reshape(o_ref.shape)

    return _pcall(body, name=name, grid=grid, in_specs=[a_spec, b_spec], out_specs=out_spec,
                  out_shape=out_shape, semantics=("arbitrary",) * len(grid))(a, b)


def wgrad_full(name, a, b):
    t_len, k = a.shape
    n = b.shape[1]
    tt = _tile_rows(t_len)
    return tn_matmul(name, a, b,
                     pl.BlockSpec((tt, k), lambda t: (t, 0)), pl.BlockSpec((tt, n), lambda t: (t, 0)),
                     _sds((k, n), F32), pl.BlockSpec((k, n), lambda t: (0, 0)), (t_len // tt,))


def wgrad_col_sharded(name, a, b_sh):
    t_len, k = a.shape
    n_sh, _, n = b_sh.shape
    tt = _tile_rows(t_len)
    return tn_matmul(name, a, b_sh,
                     pl.BlockSpec((tt, k), lambda s, t: (t, 0)), pl.BlockSpec((None, tt, n), lambda s, t: (s, t, 0)),
                     _sds((n_sh, k, n), F32), pl.BlockSpec((None, k, n), lambda s, t: (s, 0, 0)),
                     (n_sh, t_len // tt))


def wgrad_row_sharded(name, a_sh, b):
    t_len, n = b.shape
    n_sh, _, k = a_sh.shape
    tt = _tile_rows(t_len)
    return tn_matmul(name, a_sh, b,
                     pl.BlockSpec((None, tt, k), lambda s, t: (s, t, 0)), pl.BlockSpec((tt, n), lambda s, t: (t, 0)),
                     _sds((n_sh, k, n), F32), pl.BlockSpec((None, k, n), lambda s, t: (s, 0, 0)),
                     (n_sh, t_len // tt))


def wgrad_pool_group(name, p, dzs):
    t_len = p.shape[0]
    tt = _tile_rows(t_len)
    gd = POOL_GROUP_DIM
    rows = gd // N_SHARDS
    return tn_matmul(name, p, dzs,
                     pl.BlockSpec((tt, gd), lambda g, t: (t, g)), pl.BlockSpec((tt, gd), lambda g, t: (t, g)),
                     _sds((N_SHARDS, 4, rows, gd), F32),
                     pl.BlockSpec((N_SHARDS, None, rows, gd), lambda g, t: (0, g, 0, 0)),
                     (4, t_len // tt))


PAD_LANES = 384
QKV_PAD = 3 * PAD_LANES
N_SLABS = QKV_PAD // LANE
GROUP_REAL = tuple(GROUP_LANES[g + 1] - GROUP_LANES[g] for g in range(3))
Q_BLOCK = 512


def pad_groups(w, axis):
    parts = []
    for g in range(3):
        blk = lax.slice_in_dim(w, GROUP_LANES[g], GROUP_LANES[g + 1], axis=axis)
        pad = [(0, 0)] * w.ndim
        pad[axis] = (0, PAD_LANES - GROUP_REAL[g])
        parts.append(jnp.pad(blk, pad))
    return parts


def unpad_groups(parts, axis):
    return jnp.concatenate([lax.slice_in_dim(p, 0, GROUP_REAL[g], axis=axis) for g, p in enumerate(parts)],
                           axis=axis)


def pad_qkv_weight(w_qkv_sh):
    w = jnp.transpose(w_qkv_sh, (1, 0, 2)).reshape(D_MODEL, 3 * D_MODEL)
    q, k, v = (pad_groups(w[:, i * D_MODEL:(i + 1) * D_MODEL], 1) for i in range(3))
    return jnp.stack([jnp.concatenate([q[g], k[g], v[g]], axis=1) for g in range(3)])


def unpad_qkv_grad(g_pad):
    cols = [unpad_groups([g_pad[g][:, i * PAD_LANES:(i + 1) * PAD_LANES] for g in range(3)], 1) for i in range(3)]
    w = jnp.concatenate(cols, axis=1)
    return jnp.transpose(w.reshape(D_MODEL, N_SHARDS, QKV_SHARD), (1, 0, 2))


def rope_tables(t_len):
    inv_freq = 1.0 / (ROPE_THETA ** (jnp.arange(0, HEAD_DIM, 2, dtype=F32) / HEAD_DIM))
    ang = jnp.arange(t_len, dtype=F32)[:, None] * inv_freq[None, :]
    ang = jnp.concatenate([ang] * (2 * LANE // HEAD_DIM), axis=-1)
    return jnp.cos(ang), jnp.sin(ang)


def _rot_half(v):
    n = v.shape[1]
    lane = lax.broadcasted_iota(jnp.int32, v.shape, 1)
    return jnp.where(lane % HEAD_DIM < HEAD_DIM // 2,
                     -pltpu.roll(v, n - HEAD_DIM // 2, 1), pltpu.roll(v, HEAD_DIM // 2, 1))


def _lane_cols(j):
    return slice(j * LANE, (j + 1) * LANE)


def _to_residue_major(slab, j_src, dst_ref, j_dst, dil, rows):
    for r in range(dil):
        dst_ref[r, :, _lane_cols(j_dst)] = slab[j_src, pl.ds(r, rows // dil, stride=dil), :].astype(dst_ref.dtype)


def _to_natural(src_ref, j_src, slab, j_dst, dil, rows):
    for r in range(dil):
        slab[j_dst, pl.ds(r, rows // dil, stride=dil), :] = src_ref[r, :, _lane_cols(j_src)].astype(F32)


def qkv_fwd(x, g_row, w_pad, cos, sin):
    t_len = x.shape[0]
    tm = _tile_rows(t_len)

    def body(x_ref, g_ref, w_ref, cos_ref, sin_ref, h_ref, o1_ref, o4_ref, o16_ref, hbuf, slab):
        g = pl.program_id(1)

        @pl.when(g == 0)
        def _():
            h = _rms_fwd(x_ref[...], g_ref[...]).astype(BF16)
            hbuf[...] = h
            h_ref[...] = h

        acc = _dot(hbuf[...], w_ref[...])
        cos_t = cos_ref[...]
        sin_t = sin_ref[...]
        for j in range(N_SLABS):
            a = acc[:, _lane_cols(j)]
            if j < 6:
                a = a * cos_t + _rot_half(a) * sin_t
            if j < 3:
                a = a * (HEAD_DIM ** -0.5)
            slab[j] = a
        for gi, (dil, o_ref) in enumerate(zip(GROUP_DIL, (o1_ref, o4_ref, o16_ref))):
            @pl.when(g == gi)
            def _(dil=dil, o_ref=o_ref):
                for j in range(N_SLABS):
                    _to_residue_major(slab, j, o_ref, j, dil, tm)

    row_spec = pl.BlockSpec((tm, D_MODEL), lambda t, g: (t, 0))
    tab_spec = pl.BlockSpec((tm, LANE), lambda t, g: (t, 0))
    out_specs = [row_spec] + [pl.BlockSpec((d, tm // d, QKV_PAD), lambda t, g: (0, t, 0)) for d in GROUP_DIL]
    out_shape = [_sds((t_len, D_MODEL), BF16)] + [_sds((d, t_len // d, QKV_PAD), BF16) for d in GROUP_DIL]
    return _pcall(
        body, name="qkv_fwd", grid=(t_len // tm, 3),
        in_specs=[row_spec, pl.BlockSpec((1, D_MODEL), lambda t, g: (0, 0)),
                  pl.BlockSpec((None, D_MODEL, QKV_PAD), lambda t, g: (g, 0, 0)), tab_spec, tab_spec],
        out_specs=out_specs, out_shape=out_shape,
        scratch_shapes=[pltpu.VMEM((tm, D_MODEL), BF16), pltpu.VMEM((N_SLABS, tm, LANE), F32)],
        semantics=("arbitrary", "arbitrary"),
    )(x, g_row, w_pad, cos, sin)


def _band_mask(n):
    qi = lax.broadcasted_iota(jnp.int32, (ATTN_W, 2 * ATTN_W), 0)
    kj = lax.broadcasted_iota(jnp.int32, (ATTN_W, 2 * ATTN_W), 1)
    dist = ATTN_W + qi - kj
    return (dist >= 0) & (dist <= ATTN_W) & ((kj >= ATTN_W) | (n > 0))


def _half_masks():
    lane = lax.broadcasted_iota(jnp.int32, (1, LANE), 1)
    return [lane < HEAD_DIM, lane >= HEAD_DIM]


def _live_halves(gi, j):
    hms = _half_masks()
    return hms if (gi == 0 or j < 2) else hms[:1]


def attn_fwd(qkv_g, gi, name):
    dil, l_len, _ = qkv_g.shape
    qb = min(Q_BLOCK, l_len)
    nsub = qb // ATTN_W

    def body(q_ref, kc_ref, kp_ref, vc_ref, vp_ref, o_ref, lse_ref, kbuf, vbuf):
        n = pl.program_id(1)
        kbuf[pl.ds(0, ATTN_W), :] = kp_ref[...]
        kbuf[pl.ds(ATTN_W, qb), :] = kc_ref[...]
        vbuf[pl.ds(0, ATTN_W), :] = vp_ref[...]
        vbuf[pl.ds(ATTN_W, qb), :] = vc_ref[...]

        def sub(b, carry):
            r0 = pl.multiple_of(b * ATTN_W, ATTN_W)
            mask = _band_mask(n + b)
            for j in range(3):
                cols = _lane_cols(j)
                q = q_ref[pl.ds(r0, ATTN_W), cols]
                k = kbuf[pl.ds(r0, 2 * ATTN_W), cols]
                v = vbuf[pl.ds(r0, 2 * ATTN_W), cols]
                o = jnp.zeros((ATTN_W, LANE), F32)
                lse = jnp.zeros((ATTN_W, LANE), F32)
                for hm in _live_halves(gi, j):
                    qh = jnp.where(hm, q, jnp.zeros_like(q))
                    s = jnp.where(mask, _dot_nt(qh, k), NEG_INF)
                    m = jnp.max(s, axis=-1, keepdims=True)
                    e = jnp.exp(s - m)
                    den = jnp.sum(e, axis=-1, keepdims=True)
                    p = (e / den).astype(BF16)
                    o = jnp.where(hm, _dot(p, v), o)
                    lse = jnp.where(hm, m + jnp.log(den), lse)
                o_ref[pl.ds(r0, ATTN_W), cols] = o.astype(BF16)
                lse_ref[pl.ds(r0, ATTN_W), cols] = lse
            return carry

        lax.fori_loop(0, nsub, sub, 0)

    cur = lambda c: pl.BlockSpec((None, qb, PAD_LANES), lambda r, n: (r, n, c))
    prev = lambda c: pl.BlockSpec((None, ATTN_W, PAD_LANES), lambda r, n: (r, jnp.maximum(n * nsub - 1, 0), c))
    out_spec = pl.BlockSpec((None, qb, PAD_LANES), lambda r, n: (r, n, 0))
    return _pcall(
        body, name=name, grid=(dil, l_len // qb),
        in_specs=[cur(0), cur(1), prev(1), cur(2), prev(2)],
        out_specs=[out_spec, out_spec],
        out_shape=[_sds((dil, l_len, PAD_LANES), BF16), _sds((dil, l_len, PAD_LANES), F32)],
        scratch_shapes=[pltpu.VMEM((qb + ATTN_W, PAD_LANES), BF16), pltpu.VMEM((qb + ATTN_W, PAD_LANES), BF16)],
        semantics=("arbitrary", "arbitrary"),
    )(qkv_g, qkv_g, qkv_g, qkv_g, qkv_g)


def _group_stats(lse):
    lane = lax.broadcasted_iota(jnp.int32, (1, QKV_PAD), 1)
    gmask = [(lane >= g * PAD_LANES) & (lane < g * PAD_LANES + GROUP_REAL[g]) for g in range(3)]
    lses, glse = [], []
    for g in range(3):
        mx = jnp.max(jnp.where(gmask[g], lse, -jnp.inf), axis=-1, keepdims=True)
        sm = jnp.sum(jnp.where(gmask[g], jnp.exp(lse - mx), 0.0), axis=-1, keepdims=True) / HEAD_DIM
        full = mx + jnp.log(sm)
        lses.append(full)
        glse.append(full - math.log(GROUP_HEADS[g]))
    top = jnp.maximum(jnp.maximum(glse[0], glse[1]), glse[2])
    ex = [jnp.exp(v - top) for v in glse]
    tot = ex[0] + ex[1] + ex[2]
    alpha = [v / tot for v in ex]
    pick = lambda vals: jnp.where(lane < PAD_LANES, vals[0], jnp.where(lane < 2 * PAD_LANES, vals[1], vals[2]))
    return gmask, alpha, pick([3.0 * a for a in alpha]), lse - pick(lses), pick


def attn_out_fwd(x, o_parts, lse_parts, w_out_pad):
    t_len = x.shape[0]
    tm = _tile_rows(t_len)

    def body(x_ref, o1, o4, o16, l1, l4, l16, w_ref, xo_ref, mg_ref, o_ref, lse_ref, o_slab, l_slab):
        for gi, (dil, og, lg) in enumerate(zip(GROUP_DIL, (o1, o4, o16), (l1, l4, l16))):
            for j in range(3):
                _to_natural(og, j, o_slab, 3 * gi + j, dil, tm)
                _to_natural(lg, j, l_slab, 3 * gi + j, dil, tm)
        o = jnp.concatenate([o_slab[j] for j in range(N_SLABS)], axis=1)
        lse = jnp.concatenate([l_slab[j] for j in range(N_SLABS)], axis=1)
        o_ref[...] = o.astype(BF16)
        lse_ref[...] = lse
        _, _, scale, _, _ = _group_stats(lse)
        merged = (o * scale).astype(BF16)
        mg_ref[...] = merged
        xo_ref[...] = x_ref[...] + _dot(merged, w_ref[...])

    row_spec = pl.BlockSpec((tm, D_MODEL), lambda t: (t, 0))
    pad_spec = pl.BlockSpec((tm, QKV_PAD), lambda t: (t, 0))
    part_specs = [pl.BlockSpec((d, tm // d, PAD_LANES), lambda t: (0, t, 0)) for d in GROUP_DIL]
    return _pcall(
        body, name="attn_out_fwd", grid=(t_len // tm,),
        in_specs=[row_spec] + part_specs + part_specs + [pl.BlockSpec((QKV_PAD, D_MODEL), lambda t: (0, 0))],
        out_specs=[row_spec, pad_spec, pad_spec, pad_spec],
        out_shape=[_sds((t_len, D_MODEL), F32), _sds((t_len, QKV_PAD), BF16),
                   _sds((t_len, QKV_PAD), BF16), _sds((t_len, QKV_PAD), F32)],
        scratch_shapes=[pltpu.VMEM((N_SLABS, tm, LANE), F32), pltpu.VMEM((N_SLABS, tm, LANE), F32)],
        semantics=("arbitrary",),
    )(x, *o_parts, *lse_parts, w_out_pad)


def attn_out_bwd(dxo, w_out_pad, o, lse):
    t_len = dxo.shape[0]
    tm = _tile_rows(t_len)

    def body(dx_ref, w_ref, o_ref, lse_ref, d1, d4, d16, c1, c4, c16, slab):
        dmerged = _dot_nt(dx_ref[...].astype(BF16), w_ref[...])
        o_t = o_ref[...].astype(F32)
        gmask, alpha, scale, lse_rel, pick = _group_stats(lse_ref[...])
        e = dmerged * o_t
        dalpha = [3.0 * jnp.sum(jnp.where(gmask[g], e, 0.0), axis=-1, keepdims=True) for g in range(3)]
        mean_da = alpha[0] * dalpha[0] + alpha[1] * dalpha[1] + alpha[2] * dalpha[2]
        dglse = [alpha[g] * (dalpha[g] - mean_da) for g in range(3)]
        dlse = pick(dglse) * jnp.exp(lse_rel)
        do = dmerged * scale
        es = e * scale
        lane = lax.broadcasted_iota(jnp.int32, (1, LANE), 1)
        first = lane < HEAD_DIM
        for j in range(N_SLABS):
            slab[j] = do[:, _lane_cols(j)]
        for gi, (dil, dg) in enumerate(zip(GROUP_DIL, (d1, d4, d16))):
            for j in range(3):
                _to_residue_major(slab, 3 * gi + j, dg, j, dil, tm)
        for j in range(N_SLABS):
            blk = es[:, _lane_cols(j)]
            s0 = jnp.sum(jnp.where(first, blk, 0.0), axis=-1, keepdims=True)
            s1 = jnp.sum(jnp.where(first, 0.0, blk), axis=-1, keepdims=True)
            slab[j] = jnp.where(first, s0, s1) - dlse[:, _lane_cols(j)]
        for gi, (dil, cg) in enumerate(zip(GROUP_DIL, (c1, c4, c16))):
            for j in range(3):
                _to_residue_major(slab, 3 * gi + j, cg, j, dil, tm)

    row_spec = pl.BlockSpec((tm, D_MODEL), lambda t: (t, 0))
    pad_spec = pl.BlockSpec((tm, QKV_PAD), lambda t: (t, 0))
    part_specs = [pl.BlockSpec((d, tm // d, PAD_LANES), lambda t: (0, t, 0)) for d in GROUP_DIL]
    shapes = lambda dt: [_sds((d, t_len // d, PAD_LANES), dt) for d in GROUP_DIL]
    outs = _pcall(
        body, name="attn_out_bwd", grid=(t_len // tm,),
        in_specs=[row_spec, pl.BlockSpec((QKV_PAD, D_MODEL), lambda t: (0, 0)), pad_spec, pad_spec],
        out_specs=part_specs + part_specs,
        out_shape=shapes(BF16) + shapes(F32),
        scratch_shapes=[pltpu.VMEM((N_SLABS, tm, LANE), F32)],
        semantics=("arbitrary",),
    )(dxo, w_out_pad, o, lse)
    return outs[:3], outs[3:]


def attn_bwd(qkv_g, do_g, lse_g, c_g, gi, name):
    dil, l_len, _ = qkv_g.shape
    qb = min(Q_BLOCK, l_len)
    nsub = qb // ATTN_W
    nsb = l_len // qb

    def body(q_ref, kc_ref, kp_ref, vc_ref, vp_ref, do_ref, lse_ref, c_ref,
             qn_ref, don_ref, lsen_ref, cn_ref, o_ref, kbuf, vbuf, dkbuf, dvbuf):
        n = pl.program_id(1)
        kbuf[pl.ds(0, ATTN_W), :] = kp_ref[...]
        kbuf[pl.ds(ATTN_W, qb), :] = kc_ref[...]
        vbuf[pl.ds(0, ATTN_W), :] = vp_ref[...]
        vbuf[pl.ds(ATTN_W, qb), :] = vc_ref[...]
        dkbuf[...] = jnp.zeros_like(dkbuf)
        dvbuf[...] = jnp.zeros_like(dvbuf)

        def tile(q, do_t, lse_t, c_t, k, v, mask, hms):
            dq = jnp.zeros((ATTN_W, LANE), F32)
            dk = jnp.zeros((k.shape[0], LANE), F32)
            dv = jnp.zeros((k.shape[0], LANE), F32)
            for hm in hms:
                qh = jnp.where(hm, q, jnp.zeros_like(q))
                doh = jnp.where(hm, do_t, jnp.zeros_like(do_t))
                lse_h = jnp.max(jnp.where(hm, lse_t, -jnp.inf), axis=-1, keepdims=True)
                c_h = jnp.max(jnp.where(hm, c_t, -jnp.inf), axis=-1, keepdims=True)
                s = jnp.where(mask, _dot_nt(qh, k), NEG_INF)
                p = jnp.exp(s - lse_h)
                dp = _dot_nt(doh, v)
                ds = (p * (dp - c_h)).astype(BF16)
                dq = jnp.where(hm, _dot(ds, k), dq)
                dk = dk + _dot_tn(ds, qh)
                dv = dv + _dot_tn(p.astype(BF16), doh)
            return dq, dk, dv

        def sub(b, carry):
            r0 = pl.multiple_of(b * ATTN_W, ATTN_W)
            mask = _band_mask(n + b)
            rows = pl.ds(r0, ATTN_W)
            krows = pl.ds(r0, 2 * ATTN_W)
            for j in range(3):
                cols = _lane_cols(j)
                dq, dk, dv = tile(q_ref[rows, cols], do_ref[rows, cols], lse_ref[rows, cols], c_ref[rows, cols],
                                  kbuf[krows, cols], vbuf[krows, cols], mask, _live_halves(gi, j))
                o_ref[rows, cols] = dq.astype(BF16)
                dkbuf[krows, cols] += dk
                dvbuf[krows, cols] += dv
            return carry

        lax.fori_loop(0, nsub, sub, 0)

        qi = lax.broadcasted_iota(jnp.int32, (ATTN_W, ATTN_W), 0)
        kj = lax.broadcasted_iota(jnp.int32, (ATTN_W, ATTN_W), 1)
        nmask = (qi <= kj) & (n < nsb - 1)
        last = pl.ds(qb, ATTN_W)
        for j in range(3):
            cols = _lane_cols(j)
            _, dk, dv = tile(qn_ref[:, cols], don_ref[:, cols], lsen_ref[:, cols], cn_ref[:, cols],
                             kbuf[last, cols], vbuf[last, cols], nmask, _live_halves(gi, j))
            dkbuf[last, cols] += dk
            dvbuf[last, cols] += dv
        o_ref[:, pl.ds(PAD_LANES, PAD_LANES)] = dkbuf[pl.ds(ATTN_W, qb), :].astype(BF16)
        o_ref[:, pl.ds(2 * PAD_LANES, PAD_LANES)] = dvbuf[pl.ds(ATTN_W, qb), :].astype(BF16)

    cur = lambda c: pl.BlockSpec((None, qb, PAD_LANES), lambda r, n: (r, n, c))
    prev = lambda c: pl.BlockSpec((None, ATTN_W, PAD_LANES), lambda r, n: (r, jnp.maximum(n * nsub - 1, 0), c))
    nxt = pl.BlockSpec((None, ATTN_W, PAD_LANES), lambda r, n: (r, jnp.minimum((n + 1) * nsub, nsb * nsub - 1), 0))
    return _pcall(
        body, name=name, grid=(dil, nsb),
        in_specs=[cur(0), cur(1), prev(1), cur(2), prev(2), cur(0), cur(0), cur(0), nxt, nxt, nxt, nxt],
        out_specs=pl.BlockSpec((None, qb, QKV_PAD), lambda r, n: (r, n, 0)),
        out_shape=_sds((dil, l_len, QKV_PAD), BF16),
        scratch_shapes=[pltpu.VMEM((qb + ATTN_W, PAD_LANES), BF16), pltpu.VMEM((qb + ATTN_W, PAD_LANES), BF16),
                        pltpu.VMEM((qb + ATTN_W, PAD_LANES), F32), pltpu.VMEM((qb + ATTN_W, PAD_LANES), F32)],
        semantics=("arbitrary", "arbitrary"),
    )(qkv_g, qkv_g, qkv_g, qkv_g, qkv_g, do_g, lse_g, c_g, qkv_g, do_g, lse_g, c_g)


def qkv_bwd(dqkv_parts, w_pad, dxo, x, g_row, cos, sin):
    t_len = x.shape[0]
    tm = _tile_rows(t_len)

    def body(p1, p4, p16, w_ref, dxo_ref, x_ref, g_ref, cos_ref, sin_ref, dq_ref, dx_ref, dn_ref, dh, slab):
        t = pl.program_id(0)
        g = pl.program_id(1)

        @pl.when(g == 0)
        def _():
            dh[...] = jnp.zeros_like(dh)

        @pl.when(jnp.logical_and(g == 0, t == 0))
        def _():
            dn_ref[...] = jnp.zeros_like(dn_ref)

        for gi, (dil, part) in enumerate(zip(GROUP_DIL, (p1, p4, p16))):
            @pl.when(g == gi)
            def _(dil=dil, part=part):
                for j in range(N_SLABS):
                    _to_natural(part, j, slab, j, dil, tm)

        cos_t = cos_ref[...]
        sin_t = sin_ref[...]
        for j in range(N_SLABS):
            a = slab[j]
            if j < 6:
                a = a * cos_t - _rot_half(a * sin_t)
            if j < 3:
                a = a * (HEAD_DIM ** -0.5)
            dq_ref[:, _lane_cols(j)] = a.astype(BF16)
        dh[...] += _dot_nt(dq_ref[...], w_ref[...])

        @pl.when(g == 2)
        def _():
            dx, dn = _rms_bwd(dh[...], x_ref[...], g_ref[...])
            dx_ref[...] = dxo_ref[...] + dx
            dn_ref[...] += dn

    row_spec = pl.BlockSpec((tm, D_MODEL), lambda t, g: (t, 0))
    vec_spec = pl.BlockSpec((1, D_MODEL), lambda t, g: (0, 0))
    tab_spec = pl.BlockSpec((tm, LANE), lambda t, g: (t, 0))
    part_specs = [pl.BlockSpec((d, tm // d, QKV_PAD), lambda t, g: (0, t, 0)) for d in GROUP_DIL]
    return _pcall(
        body, name="qkv_bwd", grid=(t_len // tm, 3),
        in_specs=part_specs + [pl.BlockSpec((None, D_MODEL, QKV_PAD), lambda t, g: (g, 0, 0)),
                               row_spec, row_spec, vec_spec, tab_spec, tab_spec],
        out_specs=[pl.BlockSpec((None, tm, QKV_PAD), lambda t, g: (g, t, 0)), row_spec, vec_spec],
        out_shape=[_sds((3, t_len, QKV_PAD), BF16), _sds((t_len, D_MODEL), F32), _sds((1, D_MODEL), F32)],
        scratch_shapes=[pltpu.VMEM((tm, D_MODEL), F32), pltpu.VMEM((N_SLABS, tm, LANE), F32)],
        semantics=("arbitrary", "arbitrary"),
    )(*dqkv_parts, w_pad, dxo, x, g_row, cos, sin)


def final_fwd_bwd(x, g_row, target):
    t_len = x.shape[0]
    tm = _tile_rows(t_len)

    def body(x_ref, g_ref, tgt_ref, dx_ref, dn_ref, loss_ref):
        @pl.when(pl.program_id(0) == 0)
        def _():
            dn_ref[...] = jnp.zeros_like(dn_ref)
            loss_ref[...] = jnp.zeros_like(loss_ref)

        x_t = x_ref[...]
        g = g_ref[...]
        diff = _rms_fwd(x_t, g) - tgt_ref[...]
        loss_ref[...] += 0.5 * jnp.sum(jnp.mean(diff * diff, axis=-1, keepdims=True), axis=0, keepdims=True)
        dx, dn = _rms_bwd(diff * (1.0 / D_MODEL), x_t, g)
        dx_ref[...] = dx
        dn_ref[...] += dn

    row_spec = pl.BlockSpec((tm, D_MODEL), lambda t: (t, 0))
    vec_spec = pl.BlockSpec((1, D_MODEL), lambda t: (0, 0))
    return _pcall(
        body, name="final_fwd_bwd", grid=(t_len // tm,),
        in_specs=[row_spec, vec_spec, row_spec],
        out_specs=[row_spec, vec_spec, pl.BlockSpec((1, 1), lambda t: (0, 0))],
        out_shape=[_sds((t_len, D_MODEL), F32), _sds((1, D_MODEL), F32), _sds((1, 1), F32)],
        semantics=("arbitrary",),
    )(x, g_row, target)


def pool_bwd(dxo, x, g_row, w_in, w_grp, scale, w_out, zr):
    t_len = x.shape[0]
    tm = _tile_rows(t_len)
    nt = t_len // tm

    def body(dxo_ref, x_ref, g_ref, win_ref, wgrp_ref, scale_ref, wout_ref, zr_ref,
             dzs_ref, du_ref, dx_ref, dn_ref, dsc_ref, ebuf):
        i = pl.program_id(0)
        t = nt - 1 - i

        @pl.when(i == 0)
        def _():
            ebuf[pl.ds(tm, POOL_HALO), :] = jnp.zeros((POOL_HALO, D_MODEL), F32)
            dn_ref[...] = jnp.zeros_like(dn_ref)
            dsc_ref[...] = jnp.zeros_like(dsc_ref)

        dxo_t = dxo_ref[...]
        dz = _dot_nt(dxo_t.astype(BF16), wout_ref[...])
        dsc_ref[...] += jnp.sum(dz * zr_ref[...].astype(F32), axis=0, keepdims=True)
        dzs_ref[...] = (dz * scale_ref[...]).astype(BF16)
        row = t * tm + lax.broadcasted_iota(jnp.int32, (tm, 1), 0)
        for gi, w in enumerate(POOL_WINDOWS):
            cols = pl.ds(gi * POOL_GROUP_DIM, POOL_GROUP_DIM)
            dp_g = _dot_nt(dzs_ref[:, cols], wgrp_ref[gi])
            cnt = jnp.minimum(row + 1, w).astype(F32)
            ebuf[pl.ds(0, tm), cols] = dp_g / cnt
            acc = -dp_g
            for j in range(w):
                acc = acc + ebuf[pl.ds(j, tm), cols]
            du_ref[:, cols] = acc.astype(BF16)
        ebuf[pl.ds(tm, POOL_HALO), :] = ebuf[pl.ds(0, POOL_HALO), :]
        dh = _dot_nt(du_ref[...], win_ref[...])
        dx, dn = _rms_bwd(dh, x_ref[...], g_ref[...])
        dx_ref[...] = dxo_t + dx
        dn_ref[...] += dn

    row_spec = pl.BlockSpec((tm, D_MODEL), lambda i: (nt - 1 - i, 0))
    full = lambda shape: pl.BlockSpec(shape, lambda i: (0,) * len(shape))
    vec = full((1, D_MODEL))
    return _pcall(
        body, name="pool_bwd", grid=(nt,),
        in_specs=[row_spec, row_spec, vec, full((D_MODEL, D_MODEL)), full((4, POOL_GROUP_DIM, POOL_GROUP_DIM)),
                  vec, full((D_MODEL, D_MODEL)), row_spec],
        out_specs=[row_spec, row_spec, row_spec, vec, vec],
        out_shape=[_sds((t_len, D_MODEL), BF16), _sds((t_len, D_MODEL), BF16), _sds((t_len, D_MODEL), F32),
                   _sds((1, D_MODEL), F32), _sds((1, D_MODEL), F32)],
        scratch_shapes=[pltpu.VMEM((tm + POOL_HALO, D_MODEL), F32)],
        semantics=("arbitrary",),
    )(dxo, x, g_row, w_in, w_grp, scale, w_out, zr)


def _mesh_pos():
    return lax.axis_index("x"), lax.axis_index("y"), lax.axis_index("c")


def _other_chips(x, y):
    return [(1 - x, y), (x, 1 - y), (1 - x, 1 - y)]


def _remote(src, dst, send_sem, recv_sem, device):
    return pltpu.make_async_remote_copy(src_ref=src, dst_ref=dst, send_sem=send_sem, recv_sem=recv_sem,
                                        device_id=device, device_id_type=MESH)


_ANY = pl.BlockSpec(memory_space=pl.ANY)


def _comm_call(body, name, ins, out_shape, scratch_shapes):
    return pl.pallas_call(
        body, name=name, out_shape=out_shape, in_specs=[_ANY] * len(ins), out_specs=[_ANY] * len(out_shape),
        scratch_shapes=scratch_shapes,
        compiler_params=pltpu.CompilerParams(has_side_effects=True),
    )(*ins)


def gather_weights(shards):
    n = len(shards)
    halves = [s.shape[0] // 2 for s in shards]

    def body(*refs):
        ins, outs = refs[:n], refs[n:2 * n]
        send_sems, recv_sems, local_sems = refs[2 * n:]
        x, y, c = _mesh_pos()
        me = 2 * x + y
        chips = _other_chips(x, y)
        sibling = (x, y, 1 - c)
        started, locs = [], []
        for a in range(n):
            h = halves[a]
            mine = pltpu.make_async_copy(ins[a], outs[a].at[me], local_sems.at[a])
            mine.start()
            locs.append(mine)
            for j, chip in enumerate(chips):
                cp = _remote(ins[a].at[pl.ds(c * h, h)], outs[a].at[me, pl.ds(c * h, h)],
                             send_sems.at[a, j], recv_sems.at[a, j], (*chip, c))
                cp.start()
                started.append(cp)
        for a in range(n):
            h = halves[a]
            for j, chip in enumerate(chips):
                slot = outs[a].at[2 * chip[0] + chip[1], pl.ds(c * h, h)]
                _remote(slot, slot, send_sems.at[a, j], recv_sems.at[a, j], (*chip, c)).wait_recv()
                fw = _remote(slot, slot, send_sems.at[a, 3 + j], recv_sems.at[a, 3 + j], sibling)
                fw.start()
                started.append(fw)
        for a in range(n):
            h = halves[a]
            for j, chip in enumerate(chips):
                slot = outs[a].at[2 * chip[0] + chip[1], pl.ds((1 - c) * h, h)]
                _remote(slot, slot, send_sems.at[a, 3 + j], recv_sems.at[a, 3 + j], sibling).wait_recv()
        for cp in started:
            cp.wait_send()
        for lc in locs:
            lc.wait()

    out_shape = [_sds((N_SHARDS,) + s.shape, s.dtype) for s in shards]
    return _comm_call(body, "gather_weights", list(shards), out_shape,
                      [pltpu.SemaphoreType.DMA((n, 6)), pltpu.SemaphoreType.DMA((n, 6)),
                       pltpu.SemaphoreType.DMA((n,))])


def sibling_send_halves(grads):
    n = len(grads)
    halves = [g.shape[1] // 2 for g in grads]

    def body(*refs):
        ins, outs = refs[:n], refs[n:2 * n]
        send_sems, recv_sems = refs[2 * n:]
        x, y, c = _mesh_pos()
        sibling = (x, y, 1 - c)
        cps = []
        for a in range(n):
            h = halves[a]
            cp = _remote(ins[a].at[:, pl.ds((1 - c) * h, h)], outs[a], send_sems.at[a], recv_sems.at[a], sibling)
            cp.start()
            cps.append(cp)
        for cp in cps:
            cp.wait()

    out_shape = [_sds((N_SHARDS, h, g.shape[2]), g.dtype) for g, h in zip(grads, halves)]
    return _comm_call(body, "rs_sibling_halves", list(grads), out_shape,
                      [pltpu.SemaphoreType.DMA((n,)), pltpu.SemaphoreType.DMA((n,))])


def chip_exchange(parts):
    n = len(parts)

    def body(*refs):
        ins, outs = refs[:n], refs[n:2 * n]
        send_sems, recv_sems, local_sems = refs[2 * n:]
        x, y, c = _mesh_pos()
        me = 2 * x + y
        chips = _other_chips(x, y)
        locs, cps = [], []
        for a in range(n):
            lc = pltpu.make_async_copy(ins[a].at[me], outs[a].at[me], local_sems.at[a])
            lc.start()
            locs.append(lc)
            for j, chip in enumerate(chips):
                cp = _remote(ins[a].at[2 * chip[0] + chip[1]], outs[a].at[me],
                             send_sems.at[a, j], recv_sems.at[a, j], (*chip, c))
                cp.start()
                cps.append(cp)
        for a in range(n):
            for j, chip in enumerate(chips):
                slot = outs[a].at[2 * chip[0] + chip[1]]
                _remote(slot, slot, send_sems.at[a, j], recv_sems.at[a, j], (*chip, c)).wait_recv()
        for cp in cps:
            cp.wait_send()
        for lc in locs:
            lc.wait()

    out_shape = [_sds(p.shape, p.dtype) for p in parts]
    return _comm_call(body, "rs_chip_exchange", list(parts), out_shape,
                      [pltpu.SemaphoreType.DMA((n, 3)), pltpu.SemaphoreType.DMA((n, 3)),
                       pltpu.SemaphoreType.DMA((n,))])


def sibling_share(reduced):
    n = len(reduced)

    def body(*refs):
        ins, outs = refs[:n], refs[n:2 * n]
        send_sems, recv_sems, local_sems = refs[2 * n:]
        x, y, c = _mesh_pos()
        sibling = (x, y, 1 - c)
        for core in (0, 1):
            @pl.when(c == core)
            def _(core=core):
                locs, cps = [], []
                for a in range(n):
                    h = ins[a].shape[0]
                    mine = outs[a].at[pl.ds(core * h, h)]
                    lc = pltpu.make_async_copy(ins[a], mine, local_sems.at[a])
                    lc.start()
                    locs.append(lc)
                    cp = _remote(ins[a], mine, send_sems.at[a], recv_sems.at[a], sibling)
                    cp.start()
                    cps.append(cp)
                for a in range(n):
                    h = ins[a].shape[0]
                    theirs = outs[a].at[pl.ds((1 - core) * h, h)]
                    _remote(theirs, theirs, send_sems.at[a], recv_sems.at[a], sibling).wait_recv()
                for cp in cps:
                    cp.wait_send()
                for lc in locs:
                    lc.wait()

    out_shape = [_sds((2 * r.shape[0], r.shape[1]), r.dtype) for r in reduced]
    return _comm_call(body, "rs_sibling_share", list(reduced), out_shape,
                      [pltpu.SemaphoreType.DMA((n,)), pltpu.SemaphoreType.DMA((n,)),
                       pltpu.SemaphoreType.DMA((n,))])


def allreduce_small(v):
    def body(v_ref, o_ref, buf, send_sems, recv_sems):
        x, y, c = _mesh_pos()
        me = 4 * x + 2 * y + c
        buf[me] = v_ref[...]
        flip = lambda p, f: 1 - p if f else p
        peers = [(flip(x, k & 4), flip(y, k & 2), flip(c, k & 1)) for k in range(1, N_DEV)]
        cps = []
        for k, peer in enumerate(peers):
            cp = _remote(v_ref, buf.at[me], send_sems.at[k], recv_sems.at[k], peer)
            cp.start()
            cps.append(cp)
        for k, peer in enumerate(peers):
            slot = buf.at[4 * peer[0] + 2 * peer[1] + peer[2]]
            _remote(slot, slot, send_sems.at[k], recv_sems.at[k], peer).wait_recv()
        for cp in cps:
            cp.wait_send()
        acc = buf[0]
        for i in range(1, N_DEV):
            acc = acc + buf[i]
        o_ref[...] = acc

    vm = pl.BlockSpec(memory_space=pltpu.VMEM)
    return pl.pallas_call(
        body, name="allreduce_small", out_shape=_sds(v.shape, v.dtype), in_specs=[vm], out_specs=vm,
        scratch_shapes=[pltpu.VMEM((N_DEV,) + v.shape, v.dtype),
                        pltpu.SemaphoreType.DMA((N_DEV - 1,)), pltpu.SemaphoreType.DMA((N_DEV - 1,))],
        compiler_params=pltpu.CompilerParams(has_side_effects=True),
    )(v)


def add_my_half(grad, theirs, c_idx, name):
    _, r, cols = grad.shape
    h = r // 2

    def body(c_ref, g_ref, t_ref, o_ref):
        o_ref[...] = (g_ref[...] + t_ref[...]).astype(BF16)

    slot = pl.BlockSpec((None, h, cols), lambda s, c: (s, 0, 0))
    grid_spec = pltpu.PrefetchScalarGridSpec(
        num_scalar_prefetch=1, grid=(N_SHARDS,),
        in_specs=[pl.BlockSpec((None, h, cols), lambda s, c: (s, c[0], 0)), slot], out_specs=slot)
    return pl.pallas_call(
        body, name=name, grid_spec=grid_spec, out_shape=_sds((N_SHARDS, h, cols), BF16),
        compiler_params=pltpu.CompilerParams(dimension_semantics=("arbitrary",), vmem_limit_bytes=VMEM_LIMIT_BYTES),
    )(c_idx, grad, theirs)


def sum_slots(parts, name):
    _, h, cols = parts.shape

    def body(p_ref, o_ref):
        acc = p_ref[0].astype(F32)
        for k in range(1, N_SHARDS):
            acc = acc + p_ref[k].astype(F32)
        o_ref[...] = acc

    return _pcall(body, name=name, out_shape=_sds((h, cols), F32))(parts)


def adamw(name, grads, w, m, v):
    n_layers, r, cols = w.shape
    tr = r // 2 if r % 16 == 0 else r
    bias1 = 1.0 - ADAM_B1 ** ADAM_STEP
    bias2 = 1.0 - ADAM_B2 ** ADAM_STEP

    def body(*refs):
        g_refs = refs[:n_layers]
        w_ref, m_ref, v_ref, go_ref, d_ref, mo_ref, vo_ref = refs[n_layers:]
        g = g_refs[0][...]
        for layer in range(1, n_layers):
            g = jnp.where(pl.program_id(0) == layer, g_refs[layer][...], g)
        m_new = ADAM_B1 * m_ref[...] + (1.0 - ADAM_B1) * g
        v_new = ADAM_B2 * v_ref[...] + (1.0 - ADAM_B2) * (g * g)
        m_hat = m_new / bias1
        v_hat = v_new / bias2
        go_ref[...] = g
        d_ref[...] = -ADAM_LR * (m_hat / (jnp.sqrt(v_hat) + ADAM_EPS) + ADAM_WD * w_ref[...])
        mo_ref[...] = m_new
        vo_ref[...] = v_new

    g_spec = pl.BlockSpec((tr, cols), lambda l, i: (i, 0))
    lay_spec = pl.BlockSpec((None, tr, cols), lambda l, i: (l, i, 0))
    shape = _sds((n_layers, r, cols), F32)
    return _pcall(
        body, name=name, grid=(n_layers, r // tr),
        in_specs=[g_spec] * n_layers + [lay_spec] * 3, out_specs=[lay_spec] * 4,
        out_shape=[shape] * 4, semantics=("arbitrary", "arbitrary"),
    )(*grads, w, m, v)


def kernel(x, norm_mix, norm_ffn, norm_final, pool_w_in, pool_w_group, pool_scale, pool_w_out, attn_w_qkv, attn_w_out, ffn_w_gate, ffn_w_up, ffn_w_down, loss_target, m_norm_mix, m_norm_ffn, m_norm_final, m_pool_w_in, m_pool_w_group, m_pool_scale, m_pool_w_out, m_attn_w_qkv, m_attn_w_out, m_ffn_w_gate, m_ffn_w_up, m_ffn_w_down, v_norm_mix, v_norm_ffn, v_norm_final, v_pool_w_in, v_pool_w_group, v_pool_scale, v_pool_w_out, v_attn_w_qkv, v_attn_w_out, v_ffn_w_gate, v_ffn_w_up, v_ffn_w_down):
    t_len = x.shape[1]
    x0 = x.reshape(t_len, D_MODEL)
    target = loss_target.reshape(t_len, D_MODEL)
    row = lambda a: a.reshape(1, D_MODEL)

    grp_rows = POOL_GROUP_DIM // N_SHARDS
    shards = [
        pool_w_in[0], pool_w_group[0].reshape(4 * grp_rows, POOL_GROUP_DIM), pool_w_out[0],
        attn_w_qkv[0], attn_w_out[0],
        ffn_w_gate[0], ffn_w_gate[1], ffn_w_up[0], ffn_w_up[1], ffn_w_down[0], ffn_w_down[1],
    ]
    gathered = gather_weights([s.astype(BF16) for s in shards])
    w_in = gathered[0].reshape(D_MODEL, D_MODEL)
    w_grp = gathered[1].reshape(N_SHARDS, 4, grp_rows, POOL_GROUP_DIM).transpose(1, 0, 2, 3).reshape(
        4, POOL_GROUP_DIM, POOL_GROUP_DIM)
    w_out = gathered[2].reshape(D_MODEL, D_MODEL)
    w_qkv = pad_qkv_weight(gathered[3])
    w_ao = jnp.concatenate(pad_groups(gathered[4].reshape(D_MODEL, D_MODEL), 0), axis=0)
    w_gate, w_up, w_down = gathered[5:7], gathered[7:9], gathered[9:11]
    cos, sin = rope_tables(t_len)

    h0, p, zr, z, x1 = pool_fwd(x0, row(norm_mix[0]), w_in, w_grp, pool_scale, w_out)
    h1, gate0, up0, x2 = ffn_fwd(x1, row(norm_ffn[0]), w_gate[0], w_up[0], w_down[0], "ffn_fwd0")
    h2, *qkv_parts = qkv_fwd(x2, row(norm_mix[1]), w_qkv, cos, sin)
    o_parts, lse_parts = [], []
    for gi in range(3):
        o_g, lse_g = attn_fwd(qkv_parts[gi], gi, f"attn_fwd_g{gi}")
        o_parts.append(o_g)
        lse_parts.append(lse_g)
    x3, merged, o_nat, lse_nat = attn_out_fwd(x2, o_parts, lse_parts, w_ao)
    h3, gate1, up1, x4 = ffn_fwd(x3, row(norm_ffn[1]), w_gate[1], w_up[1], w_down[1], "ffn_fwd1")
    dx4, d_norm_final, loss_local = final_fwd_bwd(x4, row(norm_final), target)

    act1, dgate1, dup1, dx3, d_nf1 = ffn_bwd(dx4, x3, row(norm_ffn[1]), gate1, up1,
                                             w_gate[1], w_up[1], w_down[1], "ffn_bwd1")
    g_gate1 = wgrad_col_sharded("wgrad_gate1", h3, dgate1)
    g_up1 = wgrad_col_sharded("wgrad_up1", h3, dup1)
    g_down1 = wgrad_row_sharded("wgrad_down1", act1, dx4)

    do_parts, c_parts = attn_out_bwd(dx3, w_ao, o_nat, lse_nat)
    g_ao = unpad_groups(jnp.split(wgrad_full("wgrad_attn_out", merged, dx3), 3, axis=0), 0)
    dqkv_parts = [attn_bwd(qkv_parts[gi], do_parts[gi], lse_parts[gi], c_parts[gi], gi, f"attn_bwd_g{gi}")
                  for gi in range(3)]
    dqkv, dx2, d_nm1 = qkv_bwd(dqkv_parts, w_qkv, dx3, x2, row(norm_mix[1]), cos, sin)
    g_qkv = unpad_qkv_grad(wgrad_col_sharded("wgrad_qkv", h2, dqkv))

    act0, dgate0, dup0, dx1, d_nf0 = ffn_bwd(dx2, x1, row(norm_ffn[0]), gate0, up0,
                                             w_gate[0], w_up[0], w_down[0], "ffn_bwd0")
    g_gate0 = wgrad_col_sharded("wgrad_gate0", h1, dgate0)
    g_up0 = wgrad_col_sharded("wgrad_up0", h1, dup0)
    g_down0 = wgrad_row_sharded("wgrad_down0", act0, dx2)

    dzs, du, dx0, d_nm0, d_scale = pool_bwd(dx1, x0, row(norm_mix[0]), w_in, w_grp, pool_scale, w_out, zr)
    g_out = wgrad_full("wgrad_pool_out", z, dx1)
    g_grp = wgrad_pool_group("wgrad_pool_group", p, dzs)
    g_in = wgrad_full("wgrad_pool_in", h0, du)

    shard_grads = [
        g_in.reshape(N_SHARDS, D_MODEL // N_SHARDS, D_MODEL),
        g_grp.reshape(N_SHARDS, 4 * grp_rows, POOL_GROUP_DIM),
        g_out.reshape(N_SHARDS, D_MODEL // N_SHARDS, D_MODEL),
        g_qkv,
        g_ao.reshape(N_SHARDS, D_MODEL // N_SHARDS, D_MODEL),
        g_gate0, g_gate1, g_up0, g_up1, g_down0, g_down1,
    ]
    names = ["pool_in", "pool_group", "pool_out", "qkv", "attn_out",
             "gate0", "gate1", "up0", "up1", "down0", "down1"]
    c_idx = lax.axis_index("c").astype(jnp.int32).reshape(1)
    theirs = sibling_send_halves(shard_grads)
    partials = [add_my_half(g, t, c_idx, f"rs_add_{nm}") for g, t, nm in zip(shard_grads, theirs, names)]
    received = chip_exchange(partials)
    reduced = [sum_slots(r, f"rs_sum_{nm}") for r, nm in zip(received, names)]
    full = sibling_share(reduced)

    zero_row = jnp.zeros((1, D_MODEL), F32)
    small = jnp.concatenate([d_nm0, d_nm1, d_nf0, d_nf1, d_norm_final, d_scale,
                             jnp.broadcast_to(loss_local, (1, D_MODEL)), zero_row], axis=0)
    small = allreduce_small(small)
    loss = small[6, 0]

    pack = lambda a, b, c, d: jnp.concatenate([a, b, row(c), d, zero_row, zero_row], axis=0)[None]
    sg, sd, sm, sv = adamw("adamw_small", [small],
                           pack(norm_mix, norm_ffn, norm_final, pool_scale),
                           pack(m_norm_mix, m_norm_ffn, m_norm_final, m_pool_scale),
                           pack(v_norm_mix, v_norm_ffn, v_norm_final, v_pool_scale))
    unpack = lambda a: (a[0, 0:2], a[0, 2:4], a[0, 4], a[0, 5:6])

    def update(name, grads, w, m, v):
        n_layers = len(grads)
        shp = (n_layers,) + grads[0].shape
        outs = adamw(name, grads, w.reshape(shp), m.reshape(shp), v.reshape(shp))
        return [o.reshape(w.shape) for o in outs]

    big = [
        update("adamw_pool_in", [full[0]], pool_w_in, m_pool_w_in, v_pool_w_in),
        update("adamw_pool_group", [full[1]], pool_w_group, m_pool_w_group, v_pool_w_group),
        update("adamw_pool_out", [full[2]], pool_w_out, m_pool_w_out, v_pool_w_out),
        update("adamw_qkv", [full[3]], attn_w_qkv, m_attn_w_qkv, v_attn_w_qkv),
        update("adamw_attn_out", [full[4]], attn_w_out, m_attn_w_out, v_attn_w_out),
        update("adamw_gate", [full[5], full[6]], ffn_w_gate, m_ffn_w_gate, v_ffn_w_gate),
        update("adamw_up", [full[7], full[8]], ffn_w_up, m_ffn_w_up, v_ffn_w_up),
        update("adamw_down", [full[9], full[10]], ffn_w_down, m_ffn_w_down, v_ffn_w_down),
    ]

    def leaves(k, small_vals):
        nm, nf, nfin, psc = unpack(small_vals)
        return [nm, nf, nfin, big[0][k], big[1][k], psc, big[2][k], big[3][k], big[4][k],
                big[5][k], big[6][k], big[7][k]]

    grad_x = dx0.reshape(x.shape)
    return (loss, grad_x, *leaves(0, sg), *leaves(1, sd), *leaves(2, sm), *leaves(3, sv))
```

```python
import math

import jax
import jax.numpy as jnp
from jax import lax
from jax.experimental import pallas as pl
from jax.experimental.pallas import tpu as pltpu

F32 = jnp.float32
BF16 = jnp.bfloat16

D_MODEL = 1024
N_SHARDS = 4
N_DEV = 8
D_FF = 2816
FF_SHARD = D_FF // N_SHARDS
HEAD_DIM = 64
QKV_SHARD = 3 * D_MODEL // N_SHARDS
POOL_WINDOWS = (2, 4, 8, 16)
POOL_GROUP_DIM = 256
POOL_HALO = 16
ATTN_W = 128
GROUP_LANES = (0, 384, 704, 1024)
GROUP_HEADS = (6, 5, 5)
GROUP_DIL = (1, 4, 16)
ROPE_THETA = 10000.0
EPS = 1e-6
NEG_INF = -1e30
LANE = 128
VMEM_LIMIT_BYTES = 60 * 1024 * 1024

ADAM_LR = 0.001
ADAM_B1 = 0.9
ADAM_B2 = 0.999
ADAM_EPS = 1e-08
ADAM_WD = 0.01
ADAM_STEP = 10

NT_DIMS = (((1,), (1,)), ((), ()))
TN_DIMS = (((0,), (0,)), ((), ()))
MESH = pl.DeviceIdType.MESH


def _pcall(body, *, name, out_shape, grid=None, in_specs=None, out_specs=None, scratch_shapes=(),
           semantics=None):
    kw = {}
    if grid is not None:
        kw["grid"] = grid
    if in_specs is not None:
        kw["in_specs"] = in_specs
    if out_specs is not None:
        kw["out_specs"] = out_specs
    return pl.pallas_call(
        body, name=name, out_shape=out_shape, scratch_shapes=list(scratch_shapes),
        compiler_params=pltpu.CompilerParams(dimension_semantics=semantics, vmem_limit_bytes=VMEM_LIMIT_BYTES),
        **kw)


def _sds(shape, dtype):
    return jax.ShapeDtypeStruct(tuple(shape), dtype)


def _dot(a, b):
    return jnp.dot(a, b, preferred_element_type=F32)


def _dot_nt(a, b):
    return lax.dot_general(a, b, NT_DIMS, preferred_element_type=F32)


def _dot_tn(a, b):
    return lax.dot_general(a, b, TN_DIMS, preferred_element_type=F32)


def _rms_fwd(x, g):
    r = lax.rsqrt(jnp.mean(x * x, axis=-1, keepdims=True) + EPS)
    return x * r * g


def _rms_bwd(dh, x, g):
    r = lax.rsqrt(jnp.mean(x * x, axis=-1, keepdims=True) + EPS)
    xh = x * r
    dg = jnp.sum(dh * xh, axis=0, keepdims=True)
    dxh = dh * g
    dx = r * (dxh - xh * jnp.mean(dxh * xh, axis=-1, keepdims=True))
    return dx, dg


def _sigmoid(x):
    return 1.0 / (1.0 + jnp.exp(-x))


def _tile_rows(t):
    return min(512, t)


def _sub_tiles(tm, n_sub=2):
    rows = tm // n_sub
    return [pl.ds(i * rows, rows) for i in range(n_sub)]


def _wgrad_rows(t):
    return min(2048, t)


def pool_fwd(x, g_row, w_in, w_grp, scale, w_out):
    t_len = x.shape[0]
    tm = _tile_rows(t_len)

    def body(x_ref, g_ref, win_ref, wgrp_ref, scale_ref, wout_ref,
             h_ref, p_ref, zr_ref, z_ref, xo_ref, ubuf):
        t = pl.program_id(0)

        @pl.when(t == 0)
        def _():
            ubuf[pl.ds(0, POOL_HALO), :] = jnp.zeros((POOL_HALO, D_MODEL), F32)

        x_t = x_ref[...]
        h = _rms_fwd(x_t, g_ref[...]).astype(BF16)
        h_ref[...] = h
        ubuf[pl.ds(POOL_HALO, tm), :] = _dot(h, win_ref[...])
        row = t * tm + lax.broadcasted_iota(jnp.int32, (tm, 1), 0)
        for gi, w in enumerate(POOL_WINDOWS):
            cols = pl.ds(gi * POOL_GROUP_DIM, POOL_GROUP_DIM)
            u_g = ubuf[pl.ds(POOL_HALO, tm), cols]
            acc = u_g
            for j in range(1, w):
                acc = acc + ubuf[pl.ds(POOL_HALO - j, tm), cols]
            cnt = jnp.minimum(row + 1, w).astype(F32)
            p_g = (acc / cnt - u_g).astype(BF16)
            p_ref[:, cols] = p_g
            z_g = _dot(p_g, wgrp_ref[gi])
            zr_ref[:, cols] = z_g.astype(BF16)
            z_ref[:, cols] = (z_g * scale_ref[:, cols]).astype(BF16)
        ubuf[pl.ds(0, POOL_HALO), :] = ubuf[pl.ds(tm, POOL_HALO), :]
        xo_ref[...] = x_t + _dot(z_ref[...], wout_ref[...])

    row_spec = pl.BlockSpec((tm, D_MODEL), lambda t: (t, 0))
    full2 = lambda shape: pl.BlockSpec(shape, lambda t: (0,) * len(shape))
    return _pcall(
        body, name="pool_fwd", grid=(t_len // tm,),
        in_specs=[row_spec, full2((1, D_MODEL)), full2((D_MODEL, D_MODEL)),
                  full2((4, POOL_GROUP_DIM, POOL_GROUP_DIM)), full2((1, D_MODEL)), full2((D_MODEL, D_MODEL))],
        out_specs=[row_spec] * 5,
        out_shape=[_sds((t_len, D_MODEL), BF16)] * 4 + [_sds((t_len, D_MODEL), F32)],
        scratch_shapes=[pltpu.VMEM((tm + POOL_HALO, D_MODEL), F32)],
        semantics=("arbitrary",),
    )(x, g_row, w_in, w_grp, scale, w_out)


def ffn_fwd(x, g_row, w_gate, w_up, w_down, name):
    t_len = x.shape[0]
    tm = min(1024, t_len)

    def body(x_ref, g_ref, wg_ref, wu_ref, wd_ref, h_ref, go_ref, uo_ref, xo_ref, hbuf, acc):
        s = pl.program_id(1)

        @pl.when(s == 0)
        def _():
            h = _rms_fwd(x_ref[...], g_ref[...]).astype(BF16)
            hbuf[...] = h
            h_ref[...] = h
            acc[...] = jnp.zeros_like(acc)

        for rows in _sub_tiles(tm, 1):
            h = hbuf[rows, :]
            gate = _dot(h, wg_ref[...])
            up = _dot(h, wu_ref[...])
            go_ref[rows, :] = gate.astype(BF16)
            uo_ref[rows, :] = up.astype(BF16)
            act = (gate * _sigmoid(gate) * up).astype(BF16)
            acc[rows, :] += _dot(act, wd_ref[...])

        @pl.when(s == N_SHARDS - 1)
        def _():
            xo_ref[...] = x_ref[...] + acc[...]

    row_spec = pl.BlockSpec((tm, D_MODEL), lambda t, s: (t, 0))
    col_w = pl.BlockSpec((None, D_MODEL, FF_SHARD), lambda t, s: (s, 0, 0))
    row_w = pl.BlockSpec((None, FF_SHARD, D_MODEL), lambda t, s: (s, 0, 0))
    act_spec = pl.BlockSpec((None, tm, FF_SHARD), lambda t, s: (s, t, 0))
    return _pcall(
        body, name=name, grid=(t_len // tm, N_SHARDS),
        in_specs=[row_spec, pl.BlockSpec((1, D_MODEL), lambda t, s: (0, 0)), col_w, col_w, row_w],
        out_specs=[row_spec, act_spec, act_spec, row_spec],
        out_shape=[_sds((t_len, D_MODEL), BF16), _sds((N_SHARDS, t_len, FF_SHARD), BF16),
                   _sds((N_SHARDS, t_len, FF_SHARD), BF16), _sds((t_len, D_MODEL), F32)],
        scratch_shapes=[pltpu.VMEM((tm, D_MODEL), BF16), pltpu.VMEM((tm, D_MODEL), F32)],
        semantics=("arbitrary", "arbitrary"),
    )(x, g_row, w_gate, w_up, w_down)


def ffn_bwd(dxo, x, g_row, gate, up, w_gate, w_up, w_down, name):
    t_len = x.shape[0]
    tm = _tile_rows(t_len)

    def body(dxo_ref, x_ref, g_ref, gate_ref, up_ref, wg_ref, wu_ref, wd_ref,
             act_ref, dg_ref, du_ref, dx_ref, dn_ref, dxb, dh):
        t = pl.program_id(0)
        s = pl.program_id(1)

        @pl.when(s == 0)
        def _():
            dxb[...] = dxo_ref[...].astype(BF16)
            dh[...] = jnp.zeros_like(dh)

        @pl.when(jnp.logical_and(s == 0, t == 0))
        def _():
            dn_ref[...] = jnp.zeros_like(dn_ref)

        for rows in _sub_tiles(tm):
            dact = _dot_nt(dxb[rows, :], wd_ref[...])
            gv = gate_ref[rows, :].astype(F32)
            uv = up_ref[rows, :].astype(F32)
            sg = _sigmoid(gv)
            silu = gv * sg
            act_ref[rows, :] = (silu * uv).astype(BF16)
            dgv = (dact * uv * (sg * (1.0 + gv * (1.0 - sg)))).astype(BF16)
            duv = (dact * silu).astype(BF16)
            dg_ref[rows, :] = dgv
            du_ref[rows, :] = duv
            dh[rows, :] += _dot_nt(dgv, wg_ref[...]) + _dot_nt(duv, wu_ref[...])

        @pl.when(s == N_SHARDS - 1)
        def _():
            dx, dn = _rms_bwd(dh[...], x_ref[...], g_ref[...])
            dx_ref[...] = dxo_ref[...] + dx
            dn_ref[...] += dn

    row_spec = pl.BlockSpec((tm, D_MODEL), lambda t, s: (t, 0))
    vec_spec = pl.BlockSpec((1, D_MODEL), lambda t, s: (0, 0))
    col_w = pl.BlockSpec((None, D_MODEL, FF_SHARD), lambda t, s: (s, 0, 0))
    row_w = pl.BlockSpec((None, FF_SHARD, D_MODEL), lambda t, s: (s, 0, 0))
    act_spec = pl.BlockSpec((None, tm, FF_SHARD), lambda t, s: (s, t, 0))
    act_shape = _sds((N_SHARDS, t_len, FF_SHARD), BF16)
    return _pcall(
        body, name=name, grid=(t_len // tm, N_SHARDS),
        in_specs=[row_spec, row_spec, vec_spec, act_spec, act_spec, col_w, col_w, row_w],
        out_specs=[act_spec, act_spec, act_spec, row_spec, vec_spec],
        out_shape=[act_shape, act_shape, act_shape, _sds((t_len, D_MODEL), F32), _sds((1, D_MODEL), F32)],
        scratch_shapes=[pltpu.VMEM((tm, D_MODEL), BF16), pltpu.VMEM((tm, D_MODEL), F32)],
        semantics=("arbitrary", "arbitrary"),
    )(dxo, x, g_row, gate, up, w_gate, w_up, w_down)


def tn_matmul(name, a, b, a_spec, b_spec, out_shape, out_spec, grid):
    def body(a_ref, b_ref, o_ref):
        @pl.when(pl.program_id(len(grid) - 1) == 0)
        def _():
            o_ref[...] = jnp.zeros_like(o_ref)

        res = _dot_tn(a_ref[...].astype(BF16), b_ref[...].astype(BF16))
        o_ref[...] += res.reshape(o_ref.shape)

    return _pcall(body, name=name, grid=grid, in_specs=[a_spec, b_spec], out_specs=out_spec,
                  out_shape=out_shape, semantics=("arbitrary",) * len(grid))(a, b)


def wgrad_full(name, a, b):
    t_len, k = a.shape
    n = b.shape[1]
    tt = _wgrad_rows(t_len)
    return tn_matmul(name, a, b,
                     pl.BlockSpec((tt, k), lambda t: (t, 0)), pl.BlockSpec((tt, n), lambda t: (t, 0)),
                     _sds((k, n), F32), pl.BlockSpec((k, n), lambda t: (0, 0)), (t_len // tt,))


def wgrad_col_sharded(name, a, b_sh):
    t_len, k = a.shape
    n_sh, _, n = b_sh.shape
    tt = _wgrad_rows(t_len)
    return tn_matmul(name, a, b_sh,
                     pl.BlockSpec((tt, k), lambda s, t: (t, 0)), pl.BlockSpec((None, tt, n), lambda s, t: (s, t, 0)),
                     _sds((n_sh, k, n), F32), pl.BlockSpec((None, k, n), lambda s, t: (s, 0, 0)),
                     (n_sh, t_len // tt))


def wgrad_row_sharded(name, a_sh, b):
    t_len, n = b.shape
    n_sh, _, k = a_sh.shape
    tt = _wgrad_rows(t_len)
    return tn_matmul(name, a_sh, b,
                     pl.BlockSpec((None, tt, k), lambda s, t: (s, t, 0)), pl.BlockSpec((tt, n), lambda s, t: (t, 0)),
                     _sds((n_sh, k, n), F32), pl.BlockSpec((None, k, n), lambda s, t: (s, 0, 0)),
                     (n_sh, t_len // tt))


def wgrad_pool_group(name, p, dzs):
    t_len = p.shape[0]
    tt = _wgrad_rows(t_len)
    gd = POOL_GROUP_DIM
    rows = gd // N_SHARDS
    return tn_matmul(name, p, dzs,
                     pl.BlockSpec((tt, gd), lambda g, t: (t, g)), pl.BlockSpec((tt, gd), lambda g, t: (t, g)),
                     _sds((N_SHARDS, 4, rows, gd), F32),
                     pl.BlockSpec((N_SHARDS, None, rows, gd), lambda g, t: (0, g, 0, 0)),
                     (4, t_len // tt))


PAD_LANES = 384
QKV_PAD = 3 * PAD_LANES
N_SLABS = QKV_PAD // LANE
GROUP_REAL = tuple(GROUP_LANES[g + 1] - GROUP_LANES[g] for g in range(3))
Q_BLOCK = 512


def pad_groups(w, axis):
    parts = []
    for g in range(3):
        blk = lax.slice_in_dim(w, GROUP_LANES[g], GROUP_LANES[g + 1], axis=axis)
        pad = [(0, 0)] * w.ndim
        pad[axis] = (0, PAD_LANES - GROUP_REAL[g])
        parts.append(jnp.pad(blk, pad))
    return parts


def unpad_groups(parts, axis):
    return jnp.concatenate([lax.slice_in_dim(p, 0, GROUP_REAL[g], axis=axis) for g, p in enumerate(parts)],
                           axis=axis)


def pad_qkv_weight(w_qkv_sh):
    w = jnp.transpose(w_qkv_sh, (1, 0, 2)).reshape(D_MODEL, 3 * D_MODEL)
    q, k, v = (pad_groups(w[:, i * D_MODEL:(i + 1) * D_MODEL], 1) for i in range(3))
    return jnp.stack([jnp.concatenate([q[g], k[g], v[g]], axis=1) for g in range(3)])


def unpad_qkv_grad(g_pad):
    cols = [unpad_groups([g_pad[g][:, i * PAD_LANES:(i + 1) * PAD_LANES] for g in range(3)], 1) for i in range(3)]
    w = jnp.concatenate(cols, axis=1)
    return jnp.transpose(w.reshape(D_MODEL, N_SHARDS, QKV_SHARD), (1, 0, 2))


def rope_tables(t_len):
    inv_freq = 1.0 / (ROPE_THETA ** (jnp.arange(0, HEAD_DIM, 2, dtype=F32) / HEAD_DIM))
    ang = jnp.arange(t_len, dtype=F32)[:, None] * inv_freq[None, :]
    ang = jnp.concatenate([ang] * (2 * LANE // HEAD_DIM), axis=-1)
    return jnp.cos(ang), jnp.sin(ang)


def _rot_half(v):
    n = v.shape[1]
    lane = lax.broadcasted_iota(jnp.int32, v.shape, 1)
    return jnp.where(lane % HEAD_DIM < HEAD_DIM // 2,
                     -pltpu.roll(v, n - HEAD_DIM // 2, 1), pltpu.roll(v, HEAD_DIM // 2, 1))


def _lane_cols(j):
    return slice(j * LANE, (j + 1) * LANE)


def _to_residue_major(slab, j_src, dst_ref, j_dst, dil, rows):
    for r in range(dil):
        dst_ref[r, :, _lane_cols(j_dst)] = slab[j_src, pl.ds(r, rows // dil, stride=dil), :].astype(dst_ref.dtype)


def _to_natural(src_ref, j_src, slab, j_dst, dil, rows):
    for r in range(dil):
        slab[j_dst, pl.ds(r, rows // dil, stride=dil), :] = src_ref[r, :, _lane_cols(j_src)].astype(F32)


def qkv_fwd(x, g_row, w_pad, cos, sin):
    t_len = x.shape[0]
    tm = _tile_rows(t_len)

    def body(x_ref, g_ref, w_ref, cos_ref, sin_ref, h_ref, o1_ref, o4_ref, o16_ref, hbuf, slab):
        g = pl.program_id(1)

        @pl.when(g == 0)
        def _():
            h = _rms_fwd(x_ref[...], g_ref[...]).astype(BF16)
            hbuf[...] = h
            h_ref[...] = h

        acc = _dot(hbuf[...], w_ref[...])
        cos_t = cos_ref[...]
        sin_t = sin_ref[...]
        for j in range(N_SLABS):
            a = acc[:, _lane_cols(j)]
            if j < 6:
                a = a * cos_t + _rot_half(a) * sin_t
            if j < 3:
                a = a * (HEAD_DIM ** -0.5)
            slab[j] = a
        for gi, (dil, o_ref) in enumerate(zip(GROUP_DIL, (o1_ref, o4_ref, o16_ref))):
            @pl.when(g == gi)
            def _(dil=dil, o_ref=o_ref):
                for j in range(N_SLABS):
                    _to_residue_major(slab, j, o_ref, j, dil, tm)

    row_spec = pl.BlockSpec((tm, D_MODEL), lambda t, g: (t, 0))
    tab_spec = pl.BlockSpec((tm, LANE), lambda t, g: (t, 0))
    out_specs = [row_spec] + [pl.BlockSpec((d, tm // d, QKV_PAD), lambda t, g: (0, t, 0)) for d in GROUP_DIL]
    out_shape = [_sds((t_len, D_MODEL), BF16)] + [_sds((d, t_len // d, QKV_PAD), BF16) for d in GROUP_DIL]
    return _pcall(
        body, name="qkv_fwd", grid=(t_len // tm, 3),
        in_specs=[row_spec, pl.BlockSpec((1, D_MODEL), lambda t, g: (0, 0)),
                  pl.BlockSpec((None, D_MODEL, QKV_PAD), lambda t, g: (g, 0, 0)), tab_spec, tab_spec],
        out_specs=out_specs, out_shape=out_shape,
        scratch_shapes=[pltpu.VMEM((tm, D_MODEL), BF16), pltpu.VMEM((N_SLABS, tm, LANE), F32)],
        semantics=("arbitrary", "arbitrary"),
    )(x, g_row, w_pad, cos, sin)


def _band_mask(n):
    qi = lax.broadcasted_iota(jnp.int32, (ATTN_W, 2 * ATTN_W), 0)
    kj = lax.broadcasted_iota(jnp.int32, (ATTN_W, 2 * ATTN_W), 1)
    dist = ATTN_W + qi - kj
    return (dist >= 0) & (dist <= ATTN_W) & ((kj >= ATTN_W) | (n > 0))


def _half_masks():
    lane = lax.broadcasted_iota(jnp.int32, (1, LANE), 1)
    return [lane < HEAD_DIM, lane >= HEAD_DIM]


def _live_halves(gi, j):
    hms = _half_masks()
    return hms if (gi == 0 or j < 2) else hms[:1]


def attn_fwd(qkv_g, gi, name):
    dil, l_len, _ = qkv_g.shape
    qb = min(Q_BLOCK, l_len)
    nsub = qb // ATTN_W

    def body(q_ref, kc_ref, kp_ref, vc_ref, vp_ref, o_ref, lse_ref, kbuf, vbuf):
        n = pl.program_id(1)
        kbuf[pl.ds(0, ATTN_W), :] = kp_ref[...]
        kbuf[pl.ds(ATTN_W, qb), :] = kc_ref[...]
        vbuf[pl.ds(0, ATTN_W), :] = vp_ref[...]
        vbuf[pl.ds(ATTN_W, qb), :] = vc_ref[...]

        def sub(b, carry):
            r0 = pl.multiple_of(b * ATTN_W, ATTN_W)
            mask = _band_mask(n + b)
            for j in range(3):
                cols = _lane_cols(j)
                q = q_ref[pl.ds(r0, ATTN_W), cols]
                k = kbuf[pl.ds(r0, 2 * ATTN_W), cols]
                v = vbuf[pl.ds(r0, 2 * ATTN_W), cols]
                o = jnp.zeros((ATTN_W, LANE), F32)
                lse = jnp.zeros((ATTN_W, LANE), F32)
                for hm in _live_halves(gi, j):
                    qh = jnp.where(hm, q, jnp.zeros_like(q))
                    s = jnp.where(mask, _dot_nt(qh, k), NEG_INF)
                    m = jnp.max(s, axis=-1, keepdims=True)
                    e = jnp.exp(s - m)
                    den = jnp.sum(e, axis=-1, keepdims=True)
                    p = (e / den).astype(BF16)
                    o = jnp.where(hm, _dot(p, v), o)
                    lse = jnp.where(hm, m + jnp.log(den), lse)
                o_ref[pl.ds(r0, ATTN_W), cols] = o.astype(BF16)
                lse_ref[pl.ds(r0, ATTN_W), cols] = lse
            return carry

        lax.fori_loop(0, nsub, sub, 0)

    cur = lambda c: pl.BlockSpec((None, qb, PAD_LANES), lambda r, n: (r, n, c))
    prev = lambda c: pl.BlockSpec((None, ATTN_W, PAD_LANES), lambda r, n: (r, jnp.maximum(n * nsub - 1, 0), c))
    out_spec = pl.BlockSpec((None, qb, PAD_LANES), lambda r, n: (r, n, 0))
    return _pcall(
        body, name=name, grid=(dil, l_len // qb),
        in_specs=[cur(0), cur(1), prev(1), cur(2), prev(2)],
        out_specs=[out_spec, out_spec],
        out_shape=[_sds((dil, l_len, PAD_LANES), BF16), _sds((dil, l_len, PAD_LANES), F32)],
        scratch_shapes=[pltpu.VMEM((qb + ATTN_W, PAD_LANES), BF16), pltpu.VMEM((qb + ATTN_W, PAD_LANES), BF16)],
        semantics=("arbitrary", "arbitrary"),
    )(qkv_g, qkv_g, qkv_g, qkv_g, qkv_g)


def _group_stats(lse):
    lane = lax.broadcasted_iota(jnp.int32, (1, QKV_PAD), 1)
    gmask = [(lane >= g * PAD_LANES) & (lane < g * PAD_LANES + GROUP_REAL[g]) for g in range(3)]
    lses, glse = [], []
    for g in range(3):
        mx = jnp.max(jnp.where(gmask[g], lse, -jnp.inf), axis=-1, keepdims=True)
        sm = jnp.sum(jnp.where(gmask[g], jnp.exp(lse - mx), 0.0), axis=-1, keepdims=True) / HEAD_DIM
        full = mx + jnp.log(sm)
        lses.append(full)
        glse.append(full - math.log(GROUP_HEADS[g]))
    top = jnp.maximum(jnp.maximum(glse[0], glse[1]), glse[2])
    ex = [jnp.exp(v - top) for v in glse]
    tot = ex[0] + ex[1] + ex[2]
    alpha = [v / tot for v in ex]
    pick = lambda vals: jnp.where(lane < PAD_LANES, vals[0], jnp.where(lane < 2 * PAD_LANES, vals[1], vals[2]))
    return gmask, alpha, pick([3.0 * a for a in alpha]), lse - pick(lses), pick


def attn_out_fwd(x, o_parts, lse_parts, w_out_pad):
    t_len = x.shape[0]
    tm = _tile_rows(t_len)

    def body(x_ref, o1, o4, o16, l1, l4, l16, w_ref, xo_ref, mg_ref, o_ref, lse_ref, o_slab, l_slab):
        for gi, (dil, og, lg) in enumerate(zip(GROUP_DIL, (o1, o4, o16), (l1, l4, l16))):
            for j in range(3):
                _to_natural(og, j, o_slab, 3 * gi + j, dil, tm)
                _to_natural(lg, j, l_slab, 3 * gi + j, dil, tm)
        o = jnp.concatenate([o_slab[j] for j in range(N_SLABS)], axis=1)
        lse = jnp.concatenate([l_slab[j] for j in range(N_SLABS)], axis=1)
        o_ref[...] = o.astype(BF16)
        lse_ref[...] = lse
        _, _, scale, _, _ = _group_stats(lse)
        merged = (o * scale).astype(BF16)
        mg_ref[...] = merged
        xo_ref[...] = x_ref[...] + _dot(merged, w_ref[...])

    row_spec = pl.BlockSpec((tm, D_MODEL), lambda t: (t, 0))
    pad_spec = pl.BlockSpec((tm, QKV_PAD), lambda t: (t, 0))
    part_specs = [pl.BlockSpec((d, tm // d, PAD_LANES), lambda t: (0, t, 0)) for d in GROUP_DIL]
    return _pcall(
        body, name="attn_out_fwd", grid=(t_len // tm,),
        in_specs=[row_spec] + part_specs + part_specs + [pl.BlockSpec((QKV_PAD, D_MODEL), lambda t: (0, 0))],
        out_specs=[row_spec, pad_spec, pad_spec, pad_spec],
        out_shape=[_sds((t_len, D_MODEL), F32), _sds((t_len, QKV_PAD), BF16),
                   _sds((t_len, QKV_PAD), BF16), _sds((t_len, QKV_PAD), F32)],
        scratch_shapes=[pltpu.VMEM((N_SLABS, tm, LANE), F32), pltpu.VMEM((N_SLABS, tm, LANE), F32)],
        semantics=("arbitrary",),
    )(x, *o_parts, *lse_parts, w_out_pad)


def attn_out_bwd(dxo, w_out_pad, o, lse):
    t_len = dxo.shape[0]
    tm = _tile_rows(t_len)

    def body(dx_ref, w_ref, o_ref, lse_ref, d1, d4, d16, c1, c4, c16, slab):
        dmerged = _dot_nt(dx_ref[...].astype(BF16), w_ref[...])
        o_t = o_ref[...].astype(F32)
        gmask, alpha, scale, lse_rel, pick = _group_stats(lse_ref[...])
        e = dmerged * o_t
        dalpha = [3.0 * jnp.sum(jnp.where(gmask[g], e, 0.0), axis=-1, keepdims=True) for g in range(3)]
        mean_da = alpha[0] * dalpha[0] + alpha[1] * dalpha[1] + alpha[2] * dalpha[2]
        dglse = [alpha[g] * (dalpha[g] - mean_da) for g in range(3)]
        dlse = pick(dglse) * jnp.exp(lse_rel)
        do = dmerged * scale
        es = e * scale
        lane = lax.broadcasted_iota(jnp.int32, (1, LANE), 1)
        first = lane < HEAD_DIM
        for j in range(N_SLABS):
            slab[j] = do[:, _lane_cols(j)]
        for gi, (dil, dg) in enumerate(zip(GROUP_DIL, (d1, d4, d16))):
            for j in range(3):
                _to_residue_major(slab, 3 * gi + j, dg, j, dil, tm)
        for j in range(N_SLABS):
            blk = es[:, _lane_cols(j)]
            s0 = jnp.sum(jnp.where(first, blk, 0.0), axis=-1, keepdims=True)
            s1 = jnp.sum(jnp.where(first, 0.0, blk), axis=-1, keepdims=True)
            slab[j] = jnp.where(first, s0, s1) - dlse[:, _lane_cols(j)]
        for gi, (dil, cg) in enumerate(zip(GROUP_DIL, (c1, c4, c16))):
            for j in range(3):
                _to_residue_major(slab, 3 * gi + j, cg, j, dil, tm)

    row_spec = pl.BlockSpec((tm, D_MODEL), lambda t: (t, 0))
    pad_spec = pl.BlockSpec((tm, QKV_PAD), lambda t: (t, 0))
    part_specs = [pl.BlockSpec((d, tm // d, PAD_LANES), lambda t: (0, t, 0)) for d in GROUP_DIL]
    shapes = lambda dt: [_sds((d, t_len // d, PAD_LANES), dt) for d in GROUP_DIL]
    outs = _pcall(
        body, name="attn_out_bwd", grid=(t_len // tm,),
        in_specs=[row_spec, pl.BlockSpec((QKV_PAD, D_MODEL), lambda t: (0, 0)), pad_spec, pad_spec],
        out_specs=part_specs + part_specs,
        out_shape=shapes(BF16) + shapes(F32),
        scratch_shapes=[pltpu.VMEM((N_SLABS, tm, LANE), F32)],
        semantics=("arbitrary",),
    )(dxo, w_out_pad, o, lse)
    return outs[:3], outs[3:]


def attn_bwd(qkv_g, do_g, lse_g, c_g, gi, name):
    dil, l_len, _ = qkv_g.shape
    qb = min(Q_BLOCK, l_len)
    nsub = qb // ATTN_W
    nsb = l_len // qb

    def body(q_ref, kc_ref, kp_ref, vc_ref, vp_ref, do_ref, lse_ref, c_ref,
             qn_ref, don_ref, lsen_ref, cn_ref, o_ref, kbuf, vbuf, dkbuf, dvbuf):
        n = pl.program_id(1)
        kbuf[pl.ds(0, ATTN_W), :] = kp_ref[...]
        kbuf[pl.ds(ATTN_W, qb), :] = kc_ref[...]
        vbuf[pl.ds(0, ATTN_W), :] = vp_ref[...]
        vbuf[pl.ds(ATTN_W, qb), :] = vc_ref[...]
        dkbuf[...] = jnp.zeros_like(dkbuf)
        dvbuf[...] = jnp.zeros_like(dvbuf)

        def tile(q, do_t, lse_t, c_t, k, v, mask, hms):
            dq = jnp.zeros((ATTN_W, LANE), F32)
            dk = jnp.zeros((k.shape[0], LANE), F32)
            dv = jnp.zeros((k.shape[0], LANE), F32)
            for hm in hms:
                qh = jnp.where(hm, q, jnp.zeros_like(q))
                doh = jnp.where(hm, do_t, jnp.zeros_like(do_t))
                lse_h = jnp.max(jnp.where(hm, lse_t, -jnp.inf), axis=-1, keepdims=True)
                c_h = jnp.max(jnp.where(hm, c_t, -jnp.inf), axis=-1, keepdims=True)
                s = jnp.where(mask, _dot_nt(qh, k), NEG_INF)
                p = jnp.exp(s - lse_h)
                dp = _dot_nt(doh, v)
                ds = (p * (dp - c_h)).astype(BF16)
                dq = jnp.where(hm, _dot(ds, k), dq)
                dk = dk + _dot_tn(ds, qh)
                dv = dv + _dot_tn(p.astype(BF16), doh)
            return dq, dk, dv

        def sub(b, carry):
            r0 = pl.multiple_of(b * ATTN_W, ATTN_W)
            mask = _band_mask(n + b)
            rows = pl.ds(r0, ATTN_W)
            krows = pl.ds(r0, 2 * ATTN_W)
            for j in range(3):
                cols = _lane_cols(j)
                dq, dk, dv = tile(q_ref[rows, cols], do_ref[rows, cols], lse_ref[rows, cols], c_ref[rows, cols],
                                  kbuf[krows, cols], vbuf[krows, cols], mask, _live_halves(gi, j))
                o_ref[rows, cols] = dq.astype(BF16)
                dkbuf[krows, cols] += dk
                dvbuf[krows, cols] += dv
            return carry

        lax.fori_loop(0, nsub, sub, 0)

        qi = lax.broadcasted_iota(jnp.int32, (ATTN_W, ATTN_W), 0)
        kj = lax.broadcasted_iota(jnp.int32, (ATTN_W, ATTN_W), 1)
        nmask = (qi <= kj) & (n < nsb - 1)
        last = pl.ds(qb, ATTN_W)
        for j in range(3):
            cols = _lane_cols(j)
            _, dk, dv = tile(qn_ref[:, cols], don_ref[:, cols], lsen_ref[:, cols], cn_ref[:, cols],
                             kbuf[last, cols], vbuf[last, cols], nmask, _live_halves(gi, j))
            dkbuf[last, cols] += dk
            dvbuf[last, cols] += dv
        o_ref[:, pl.ds(PAD_LANES, PAD_LANES)] = dkbuf[pl.ds(ATTN_W, qb), :].astype(BF16)
        o_ref[:, pl.ds(2 * PAD_LANES, PAD_LANES)] = dvbuf[pl.ds(ATTN_W, qb), :].astype(BF16)

    cur = lambda c: pl.BlockSpec((None, qb, PAD_LANES), lambda r, n: (r, n, c))
    prev = lambda c: pl.BlockSpec((None, ATTN_W, PAD_LANES), lambda r, n: (r, jnp.maximum(n * nsub - 1, 0), c))
    nxt = pl.BlockSpec((None, ATTN_W, PAD_LANES), lambda r, n: (r, jnp.minimum((n + 1) * nsub, nsb * nsub - 1), 0))
    return _pcall(
        body, name=name, grid=(dil, nsb),
        in_specs=[cur(0), cur(1), prev(1), cur(2), prev(2), cur(0), cur(0), cur(0), nxt, nxt, nxt, nxt],
        out_specs=pl.BlockSpec((None, qb, QKV_PAD), lambda r, n: (r, n, 0)),
        out_shape=_sds((dil, l_len, QKV_PAD), BF16),
        scratch_shapes=[pltpu.VMEM((qb + ATTN_W, PAD_LANES), BF16), pltpu.VMEM((qb + ATTN_W, PAD_LANES), BF16),
                        pltpu.VMEM((qb + ATTN_W, PAD_LANES), F32), pltpu.VMEM((qb + ATTN_W, PAD_LANES), F32)],
        semantics=("arbitrary", "arbitrary"),
    )(qkv_g, qkv_g, qkv_g, qkv_g, qkv_g, do_g, lse_g, c_g, qkv_g, do_g, lse_g, c_g)


def qkv_bwd(dqkv_parts, w_pad, dxo, x, g_row, cos, sin):
    t_len = x.shape[0]
    tm = _tile_rows(t_len)

    def body(p1, p4, p16, w_ref, dxo_ref, x_ref, g_ref, cos_ref, sin_ref, dq_ref, dx_ref, dn_ref, dh, slab):
        t = pl.program_id(0)
        g = pl.program_id(1)

        @pl.when(g == 0)
        def _():
            dh[...] = jnp.zeros_like(dh)

        @pl.when(jnp.logical_and(g == 0, t == 0))
        def _():
            dn_ref[...] = jnp.zeros_like(dn_ref)

        for gi, (dil, part) in enumerate(zip(GROUP_DIL, (p1, p4, p16))):
            @pl.when(g == gi)
            def _(dil=dil, part=part):
                for j in range(N_SLABS):
                    _to_natural(part, j, slab, j, dil, tm)

        cos_t = cos_ref[...]
        sin_t = sin_ref[...]
        for j in range(N_SLABS):
            a = slab[j]
            if j < 6:
                a = a * cos_t - _rot_half(a * sin_t)
            if j < 3:
                a = a * (HEAD_DIM ** -0.5)
            dq_ref[:, _lane_cols(j)] = a.astype(BF16)
        dh[...] += _dot_nt(dq_ref[...], w_ref[...])

        @pl.when(g == 2)
        def _():
            dx, dn = _rms_bwd(dh[...], x_ref[...], g_ref[...])
            dx_ref[...] = dxo_ref[...] + dx
            dn_ref[...] += dn

    row_spec = pl.BlockSpec((tm, D_MODEL), lambda t, g: (t, 0))
    vec_spec = pl.BlockSpec((1, D_MODEL), lambda t, g: (0, 0))
    tab_spec = pl.BlockSpec((tm, LANE), lambda t, g: (t, 0))
    part_specs = [pl.BlockSpec((d, tm // d, QKV_PAD), lambda t, g: (0, t, 0)) for d in GROUP_DIL]
    return _pcall(
        body, name="qkv_bwd", grid=(t_len // tm, 3),
        in_specs=part_specs + [pl.BlockSpec((None, D_MODEL, QKV_PAD), lambda t, g: (g, 0, 0)),
                               row_spec, row_spec, vec_spec, tab_spec, tab_spec],
        out_specs=[pl.BlockSpec((None, tm, QKV_PAD), lambda t, g: (g, t, 0)), row_spec, vec_spec],
        out_shape=[_sds((3, t_len, QKV_PAD), BF16), _sds((t_len, D_MODEL), F32), _sds((1, D_MODEL), F32)],
        scratch_shapes=[pltpu.VMEM((tm, D_MODEL), F32), pltpu.VMEM((N_SLABS, tm, LANE), F32)],
        semantics=("arbitrary", "arbitrary"),
    )(*dqkv_parts, w_pad, dxo, x, g_row, cos, sin)


def final_fwd_bwd(x, g_row, target):
    t_len = x.shape[0]
    tm = _tile_rows(t_len)

    def body(x_ref, g_ref, tgt_ref, dx_ref, dn_ref, loss_ref):
        @pl.when(pl.program_id(0) == 0)
        def _():
            dn_ref[...] = jnp.zeros_like(dn_ref)
            loss_ref[...] = jnp.zeros_like(loss_ref)

        x_t = x_ref[...]
        g = g_ref[...]
        diff = _rms_fwd(x_t, g) - tgt_ref[...]
        loss_ref[...] += 0.5 * jnp.sum(jnp.mean(diff * diff, axis=-1, keepdims=True), axis=0, keepdims=True)
        dx, dn = _rms_bwd(diff * (1.0 / D_MODEL), x_t, g)
        dx_ref[...] = dx
        dn_ref[...] += dn

    row_spec = pl.BlockSpec((tm, D_MODEL), lambda t: (t, 0))
    vec_spec = pl.BlockSpec((1, D_MODEL), lambda t: (0, 0))
    return _pcall(
        body, name="final_fwd_bwd", grid=(t_len // tm,),
        in_specs=[row_spec, vec_spec, row_spec],
        out_specs=[row_spec, vec_spec, pl.BlockSpec((1, 1), lambda t: (0, 0))],
        out_shape=[_sds((t_len, D_MODEL), F32), _sds((1, D_MODEL), F32), _sds((1, 1), F32)],
        semantics=("arbitrary",),
    )(x, g_row, target)


def pool_bwd(dxo, x, g_row, w_in, w_grp, scale, w_out, zr):
    t_len = x.shape[0]
    tm = _tile_rows(t_len)
    nt = t_len // tm

    def body(dxo_ref, x_ref, g_ref, win_ref, wgrp_ref, scale_ref, wout_ref, zr_ref,
             dzs_ref, du_ref, dx_ref, dn_ref, dsc_ref, ebuf):
        i = pl.program_id(0)
        t = nt - 1 - i

        @pl.when(i == 0)
        def _():
            ebuf[pl.ds(tm, POOL_HALO), :] = jnp.zeros((POOL_HALO, D_MODEL), F32)
            dn_ref[...] = jnp.zeros_like(dn_ref)
            dsc_ref[...] = jnp.zeros_like(dsc_ref)

        dxo_t = dxo_ref[...]
        dz = _dot_nt(dxo_t.astype(BF16), wout_ref[...])
        dsc_ref[...] += jnp.sum(dz * zr_ref[...].astype(F32), axis=0, keepdims=True)
        dzs_ref[...] = (dz * scale_ref[...]).astype(BF16)
        row = t * tm + lax.broadcasted_iota(jnp.int32, (tm, 1), 0)
        for gi, w in enumerate(POOL_WINDOWS):
            cols = pl.ds(gi * POOL_GROUP_DIM, POOL_GROUP_DIM)
            dp_g = _dot_nt(dzs_ref[:, cols], wgrp_ref[gi])
            cnt = jnp.minimum(row + 1, w).astype(F32)
            ebuf[pl.ds(0, tm), cols] = dp_g / cnt
            acc = -dp_g
            for j in range(w):
                acc = acc + ebuf[pl.ds(j, tm), cols]
            du_ref[:, cols] = acc.astype(BF16)
        ebuf[pl.ds(tm, POOL_HALO), :] = ebuf[pl.ds(0, POOL_HALO), :]
        dh = _dot_nt(du_ref[...], win_ref[...])
        dx, dn = _rms_bwd(dh, x_ref[...], g_ref[...])
        dx_ref[...] = dxo_t + dx
        dn_ref[...] += dn

    row_spec = pl.BlockSpec((tm, D_MODEL), lambda i: (nt - 1 - i, 0))
    full = lambda shape: pl.BlockSpec(shape, lambda i: (0,) * len(shape))
    vec = full((1, D_MODEL))
    return _pcall(
        body, name="pool_bwd", grid=(nt,),
        in_specs=[row_spec, row_spec, vec, full((D_MODEL, D_MODEL)), full((4, POOL_GROUP_DIM, POOL_GROUP_DIM)),
                  vec, full((D_MODEL, D_MODEL)), row_spec],
        out_specs=[row_spec, row_spec, row_spec, vec, vec],
        out_shape=[_sds((t_len, D_MODEL), BF16), _sds((t_len, D_MODEL), BF16), _sds((t_len, D_MODEL), F32),
                   _sds((1, D_MODEL), F32), _sds((1, D_MODEL), F32)],
        scratch_shapes=[pltpu.VMEM((tm + POOL_HALO, D_MODEL), F32)],
        semantics=("arbitrary",),
    )(dxo, x, g_row, w_in, w_grp, scale, w_out, zr)


def _mesh_pos():
    return lax.axis_index("x"), lax.axis_index("y"), lax.axis_index("c")


def _other_chips(x, y):
    return [(1 - x, y), (x, 1 - y), (1 - x, 1 - y)]


def _remote(src, dst, send_sem, recv_sem, device):
    return pltpu.make_async_remote_copy(src_ref=src, dst_ref=dst, send_sem=send_sem, recv_sem=recv_sem,
                                        device_id=device, device_id_type=MESH)


_ANY = pl.BlockSpec(memory_space=pl.ANY)


def _comm_call(body, name, ins, out_shape, scratch_shapes, aliases=None):
    return pl.pallas_call(
        body, name=name, out_shape=out_shape, in_specs=[_ANY] * len(ins), out_specs=[_ANY] * len(out_shape),
        scratch_shapes=scratch_shapes, input_output_aliases=aliases or {},
        compiler_params=pltpu.CompilerParams(has_side_effects=True),
    )(*ins)


def gather_weights(shards):
    n = len(shards)
    halves = [s.shape[0] // 2 for s in shards]
    my_slot = 2 * lax.axis_index("x") + lax.axis_index("y")
    staged = [lax.dynamic_update_slice(lax.empty((N_SHARDS,) + s.shape, s.dtype), s[None], (my_slot, 0, 0))
              for s in shards]

    def body(*refs):
        ins, outs = refs[:n], refs[2 * n:3 * n]
        send_sems, recv_sems = refs[3 * n:]
        x, y, c = _mesh_pos()
        me = 2 * x + y
        chips = _other_chips(x, y)
        sibling = (x, y, 1 - c)
        started = []
        for a in range(n):
            h = halves[a]
            for j, chip in enumerate(chips):
                cp = _remote(ins[a].at[pl.ds(c * h, h)], outs[a].at[me, pl.ds(c * h, h)],
                             send_sems.at[a, j], recv_sems.at[a, j], (*chip, c))
                cp.start()
                started.append(cp)
        for a in range(n):
            h = halves[a]
            for j, chip in enumerate(chips):
                slot = outs[a].at[2 * chip[0] + chip[1], pl.ds(c * h, h)]
                _remote(slot, slot, send_sems.at[a, j], recv_sems.at[a, j], (*chip, c)).wait_recv()
                fw = _remote(slot, slot, send_sems.at[a, 3 + j], recv_sems.at[a, 3 + j], sibling)
                fw.start()
                started.append(fw)
        for a in range(n):
            h = halves[a]
            for j, chip in enumerate(chips):
                slot = outs[a].at[2 * chip[0] + chip[1], pl.ds((1 - c) * h, h)]
                _remote(slot, slot, send_sems.at[a, 3 + j], recv_sems.at[a, 3 + j], sibling).wait_recv()
        for cp in started:
            cp.wait_send()

    out_shape = [_sds((N_SHARDS,) + s.shape, s.dtype) for s in shards]
    return _comm_call(body, "gather_weights", list(shards) + staged, out_shape,
                      [pltpu.SemaphoreType.DMA((n, 6)), pltpu.SemaphoreType.DMA((n, 6))],
                      aliases={n + a: a for a in range(n)})


def sibling_send_halves(grads):
    n = len(grads)
    halves = [g.shape[1] // 2 for g in grads]

    def body(*refs):
        ins, outs = refs[:n], refs[n:2 * n]
        send_sems, recv_sems = refs[2 * n:]
        x, y, c = _mesh_pos()
        sibling = (x, y, 1 - c)
        cps = []
        for a in range(n):
            h = halves[a]
            cp = _remote(ins[a].at[:, pl.ds((1 - c) * h, h)], outs[a], send_sems.at[a], recv_sems.at[a], sibling)
            cp.start()
            cps.append(cp)
        for cp in cps:
            cp.wait()

    out_shape = [_sds((N_SHARDS, h, g.shape[2]), g.dtype) for g, h in zip(grads, halves)]
    return _comm_call(body, "rs_sibling_halves", list(grads), out_shape,
                      [pltpu.SemaphoreType.DMA((n,)), pltpu.SemaphoreType.DMA((n,))])


def chip_exchange(parts):
    n = len(parts)

    def body(*refs):
        ins, outs = refs[:n], refs[n:2 * n]
        send_sems, recv_sems = refs[2 * n:]
        x, y, c = _mesh_pos()
        me = 2 * x + y
        chips = _other_chips(x, y)
        cps = []
        for a in range(n):
            for j, chip in enumerate(chips):
                cp = _remote(ins[a].at[2 * chip[0] + chip[1]], outs[a].at[me],
                             send_sems.at[a, j], recv_sems.at[a, j], (*chip, c))
                cp.start()
                cps.append(cp)
        for a in range(n):
            for j, chip in enumerate(chips):
                slot = outs[a].at[2 * chip[0] + chip[1]]
                _remote(slot, slot, send_sems.at[a, j], recv_sems.at[a, j], (*chip, c)).wait_recv()
        for cp in cps:
            cp.wait_send()

    out_shape = [_sds(p.shape, p.dtype) for p in parts]
    return _comm_call(body, "rs_chip_exchange", list(parts), out_shape,
                      [pltpu.SemaphoreType.DMA((n, 3)), pltpu.SemaphoreType.DMA((n, 3))])


def sibling_share(reduced):
    n = len(reduced)

    def body(*refs):
        outs = refs[n:2 * n]
        send_sems, recv_sems = refs[2 * n:]
        x, y, c = _mesh_pos()
        sibling = (x, y, 1 - c)
        cps = []
        for a in range(n):
            h = outs[a].shape[0] // 2
            mine = outs[a].at[pl.ds(c * h, h)]
            cp = _remote(mine, mine, send_sems.at[a], recv_sems.at[a], sibling)
            cp.start()
            cps.append(cp)
        for a in range(n):
            h = outs[a].shape[0] // 2
            theirs = outs[a].at[pl.ds((1 - c) * h, h)]
            _remote(theirs, theirs, send_sems.at[a], recv_sems.at[a], sibling).wait_recv()
        for cp in cps:
            cp.wait_send()

    out_shape = [_sds(r.shape, r.dtype) for r in reduced]
    return _comm_call(body, "rs_sibling_share", list(reduced), out_shape,
                      [pltpu.SemaphoreType.DMA((n,)), pltpu.SemaphoreType.DMA((n,))],
                      aliases={a: a for a in range(n)})


def allreduce_small(v):
    def body(v_ref, o_ref, buf, send_sems, recv_sems):
        x, y, c = _mesh_pos()
        me = 4 * x + 2 * y + c
        buf[me] = v_ref[...]
        flip = lambda p, f: 1 - p if f else p
        peers = [(flip(x, k & 4), flip(y, k & 2), flip(c, k & 1)) for k in range(1, N_DEV)]
        cps = []
        for k, peer in enumerate(peers):
            cp = _remote(v_ref, buf.at[me], send_sems.at[k], recv_sems.at[k], peer)
            cp.start()
            cps.append(cp)
        for k, peer in enumerate(peers):
            slot = buf.at[4 * peer[0] + 2 * peer[1] + peer[2]]
            _remote(slot, slot, send_sems.at[k], recv_sems.at[k], peer).wait_recv()
        for cp in cps:
            cp.wait_send()
        acc = buf[0]
        for i in range(1, N_DEV):
            acc = acc + buf[i]
        o_ref[...] = acc

    vm = pl.BlockSpec(memory_space=pltpu.VMEM)
    return pl.pallas_call(
        body, name="allreduce_small", out_shape=_sds(v.shape, v.dtype), in_specs=[vm], out_specs=vm,
        scratch_shapes=[pltpu.VMEM((N_DEV,) + v.shape, v.dtype),
                        pltpu.SemaphoreType.DMA((N_DEV - 1,)), pltpu.SemaphoreType.DMA((N_DEV - 1,))],
        compiler_params=pltpu.CompilerParams(has_side_effects=True),
    )(v)


def add_my_half(grad, theirs, c_idx, name):
    _, r, cols = grad.shape
    h = r // 2

    def body(c_ref, g_ref, t_ref, o_ref):
        o_ref[...] = (g_ref[...] + t_ref[...]).astype(BF16)

    slot = pl.BlockSpec((None, h, cols), lambda s, c: (s, 0, 0))
    grid_spec = pltpu.PrefetchScalarGridSpec(
        num_scalar_prefetch=1, grid=(N_SHARDS,),
        in_specs=[pl.BlockSpec((None, h, cols), lambda s, c: (s, c[0], 0)), slot], out_specs=slot)
    return pl.pallas_call(
        body, name=name, grid_spec=grid_spec, out_shape=_sds((N_SHARDS, h, cols), BF16),
        compiler_params=pltpu.CompilerParams(dimension_semantics=("arbitrary",), vmem_limit_bytes=VMEM_LIMIT_BYTES),
    )(c_idx, grad, theirs)


def sum_slots(received, mine, pos_idx, name):
    _, h, cols = received.shape

    def body(pos_ref, r_ref, m_ref, o_ref):
        acc = None
        for k in range(N_SHARDS):
            term = jnp.where(pos_ref[0] == k, m_ref[k], r_ref[k]).astype(F32)
            acc = term if acc is None else acc + term
        o_ref[...] = acc

    whole = pl.BlockSpec((N_SHARDS, h, cols), lambda i, pos: (0, 0, 0))
    grid_spec = pltpu.PrefetchScalarGridSpec(
        num_scalar_prefetch=1, grid=(1,), in_specs=[whole, whole],
        out_specs=pl.BlockSpec((h, cols), lambda i, pos: (pos[1], 0)))
    return pl.pallas_call(
        body, name=name, grid_spec=grid_spec, out_shape=_sds((2 * h, cols), F32),
        compiler_params=pltpu.CompilerParams(dimension_semantics=("arbitrary",), vmem_limit_bytes=VMEM_LIMIT_BYTES),
    )(pos_idx, received, mine)


def adamw(name, grads, w, m, v):
    n_layers, r, cols = w.shape
    tr = r // 2 if r % 16 == 0 else r
    bias1 = 1.0 - ADAM_B1 ** ADAM_STEP
    bias2 = 1.0 - ADAM_B2 ** ADAM_STEP

    def body(*refs):
        g_refs = refs[:n_layers]
        w_ref, m_ref, v_ref, go_ref, d_ref, mo_ref, vo_ref = refs[n_layers:]
        g = g_refs[0][...]
        for layer in range(1, n_layers):
            g = jnp.where(pl.program_id(0) == layer, g_refs[layer][...], g)
        m_new = ADAM_B1 * m_ref[...] + (1.0 - ADAM_B1) * g
        v_new = ADAM_B2 * v_ref[...] + (1.0 - ADAM_B2) * (g * g)
        m_hat = m_new / bias1
        v_hat = v_new / bias2
        go_ref[...] = g
        d_ref[...] = -ADAM_LR * (m_hat / (jnp.sqrt(v_hat) + ADAM_EPS) + ADAM_WD * w_ref[...])
        mo_ref[...] = m_new
        vo_ref[...] = v_new

    g_spec = pl.BlockSpec((tr, cols), lambda l, i: (i, 0))
    lay_spec = pl.BlockSpec((None, tr, cols), lambda l, i: (l, i, 0))
    shape = _sds((n_layers, r, cols), F32)
    return _pcall(
        body, name=name, grid=(n_layers, r // tr),
        in_specs=[g_spec] * n_layers + [lay_spec] * 3, out_specs=[lay_spec] * 4,
        out_shape=[shape] * 4, semantics=("arbitrary", "arbitrary"),
    )(*grads, w, m, v)


def kernel(x, norm_mix, norm_ffn, norm_final, pool_w_in, pool_w_group, pool_scale, pool_w_out, attn_w_qkv, attn_w_out, ffn_w_gate, ffn_w_up, ffn_w_down, loss_target, m_norm_mix, m_norm_ffn, m_norm_final, m_pool_w_in, m_pool_w_group, m_pool_scale, m_pool_w_out, m_attn_w_qkv, m_attn_w_out, m_ffn_w_gate, m_ffn_w_up, m_ffn_w_down, v_norm_mix, v_norm_ffn, v_norm_final, v_pool_w_in, v_pool_w_group, v_pool_scale, v_pool_w_out, v_attn_w_qkv, v_attn_w_out, v_ffn_w_gate, v_ffn_w_up, v_ffn_w_down):
    t_len = x.shape[1]
    x0 = x.reshape(t_len, D_MODEL)
    target = loss_target.reshape(t_len, D_MODEL)
    row = lambda a: a.reshape(1, D_MODEL)

    grp_rows = POOL_GROUP_DIM // N_SHARDS
    shards = [
        pool_w_in[0], pool_w_group[0].reshape(4 * grp_rows, POOL_GROUP_DIM), pool_w_out[0],
        attn_w_qkv[0], attn_w_out[0],
        ffn_w_gate[0], ffn_w_gate[1], ffn_w_up[0], ffn_w_up[1], ffn_w_down[0], ffn_w_down[1],
    ]
    gathered = gather_weights([s.astype(BF16) for s in shards])
    w_in = gathered[0].reshape(D_MODEL, D_MODEL)
    w_grp = gathered[1].reshape(N_SHARDS, 4, grp_rows, POOL_GROUP_DIM).transpose(1, 0, 2, 3).reshape(
        4, POOL_GROUP_DIM, POOL_GROUP_DIM)
    w_out = gathered[2].reshape(D_MODEL, D_MODEL)
    w_qkv = pad_qkv_weight(gathered[3])
    w_ao = jnp.concatenate(pad_groups(gathered[4].reshape(D_MODEL, D_MODEL), 0), axis=0)
    w_gate, w_up, w_down = gathered[5:7], gathered[7:9], gathered[9:11]
    cos, sin = rope_tables(t_len)

    h0, p, zr, z, x1 = pool_fwd(x0, row(norm_mix[0]), w_in, w_grp, pool_scale, w_out)
    h1, gate0, up0, x2 = ffn_fwd(x1, row(norm_ffn[0]), w_gate[0], w_up[0], w_down[0], "ffn_fwd0")
    h2, *qkv_parts = qkv_fwd(x2, row(norm_mix[1]), w_qkv, cos, sin)
    o_parts, lse_parts = [], []
    for gi in range(3):
        o_g, lse_g = attn_fwd(qkv_parts[gi], gi, f"attn_fwd_g{gi}")
        o_parts.append(o_g)
        lse_parts.append(lse_g)
    x3, merged, o_nat, lse_nat = attn_out_fwd(x2, o_parts, lse_parts, w_ao)
    h3, gate1, up1, x4 = ffn_fwd(x3, row(norm_ffn[1]), w_gate[1], w_up[1], w_down[1], "ffn_fwd1")
    dx4, d_norm_final, loss_local = final_fwd_bwd(x4, row(norm_final), target)

    act1, dgate1, dup1, dx3, d_nf1 = ffn_bwd(dx4, x3, row(norm_ffn[1]), gate1, up1,
                                             w_gate[1], w_up[1], w_down[1], "ffn_bwd1")
    g_gate1 = wgrad_col_sharded("wgrad_gate1", h3, dgate1)
    g_up1 = wgrad_col_sharded("wgrad_up1", h3, dup1)
    g_down1 = wgrad_row_sharded("wgrad_down1", act1, dx4)

    do_parts, c_parts = attn_out_bwd(dx3, w_ao, o_nat, lse_nat)
    g_ao = unpad_groups(jnp.split(wgrad_full("wgrad_attn_out", merged, dx3), 3, axis=0), 0)
    dqkv_parts = [attn_bwd(qkv_parts[gi], do_parts[gi], lse_parts[gi], c_parts[gi], gi, f"attn_bwd_g{gi}")
                  for gi in range(3)]
    dqkv, dx2, d_nm1 = qkv_bwd(dqkv_parts, w_qkv, dx3, x2, row(norm_mix[1]), cos, sin)
    g_qkv = unpad_qkv_grad(wgrad_col_sharded("wgrad_qkv", h2, dqkv))

    act0, dgate0, dup0, dx1, d_nf0 = ffn_bwd(dx2, x1, row(norm_ffn[0]), gate0, up0,
                                             w_gate[0], w_up[0], w_down[0], "ffn_bwd0")
    g_gate0 = wgrad_col_sharded("wgrad_gate0", h1, dgate0)
    g_up0 = wgrad_col_sharded("wgrad_up0", h1, dup0)
    g_down0 = wgrad_row_sharded("wgrad_down0", act0, dx2)

    dzs, du, dx0, d_nm0, d_scale = pool_bwd(dx1, x0, row(norm_mix[0]), w_in, w_grp, pool_scale, w_out, zr)
    g_out = wgrad_full("wgrad_pool_out", z, dx1)
    g_grp = wgrad_pool_group("wgrad_pool_group", p, dzs)
    g_in = wgrad_full("wgrad_pool_in", h0, du)

    shard_grads = [
        g_in.reshape(N_SHARDS, D_MODEL // N_SHARDS, D_MODEL),
        g_grp.reshape(N_SHARDS, 4 * grp_rows, POOL_GROUP_DIM),
        g_out.reshape(N_SHARDS, D_MODEL // N_SHARDS, D_MODEL),
        g_qkv,
        g_ao.reshape(N_SHARDS, D_MODEL // N_SHARDS, D_MODEL),
        g_gate0, g_gate1, g_up0, g_up1, g_down0, g_down1,
    ]
    names = ["pool_in", "pool_group", "pool_out", "qkv", "attn_out",
             "gate0", "gate1", "up0", "up1", "down0", "down1"]
    c_idx = lax.axis_index("c").astype(jnp.int32).reshape(1)
    pos_idx = jnp.stack([2 * lax.axis_index("x") + lax.axis_index("y"), lax.axis_index("c")]).astype(jnp.int32)
    theirs = sibling_send_halves(shard_grads)
    partials = [add_my_half(g, t, c_idx, f"rs_add_{nm}") for g, t, nm in zip(shard_grads, theirs, names)]
    received = chip_exchange(partials)
    reduced = [sum_slots(r, p_, pos_idx, f"rs_sum_{nm}") for r, p_, nm in zip(received, partials, names)]
    full = sibling_share(reduced)

    zero_row = jnp.zeros((1, D_MODEL), F32)
    small = jnp.concatenate([d_nm0, d_nm1, d_nf0, d_nf1, d_norm_final, d_scale,
                             jnp.broadcast_to(loss_local, (1, D_MODEL)), zero_row], axis=0)
    small = allreduce_small(small)
    loss = small[6, 0]

    pack = lambda a, b, c, d: jnp.concatenate([a, b, row(c), d, zero_row, zero_row], axis=0)[None]
    sg, sd, sm, sv = adamw("adamw_small", [small],
                           pack(norm_mix, norm_ffn, norm_final, pool_scale),
                           pack(m_norm_mix, m_norm_ffn, m_norm_final, m_pool_scale),
                           pack(v_norm_mix, v_norm_ffn, v_norm_final, v_pool_scale))
    unpack = lambda a: (a[0, 0:2], a[0, 2:4], a[0, 4], a[0, 5:6])

    def update(name, grads, w, m, v):
        n_layers = len(grads)
        shp = (n_layers,) + grads[0].shape
        outs = adamw(name, grads, w.reshape(shp), m.reshape(shp), v.reshape(shp))
        return [o.reshape(w.shape) for o in outs]

    big = [
        update("adamw_pool_in", [full[0]], pool_w_in, m_pool_w_in, v_pool_w_in),
        update("adamw_pool_group", [full[1]], pool_w_group, m_pool_w_group, v_pool_w_group),
        update("adamw_pool_out", [full[2]], pool_w_out, m_pool_w_out, v_pool_w_out),
        update("adamw_qkv", [full[3]], attn_w_qkv, m_attn_w_qkv, v_attn_w_qkv),
        update("adamw_attn_out", [full[4]], attn_w_out, m_attn_w_out, v_attn_w_out),
        update("adamw_gate", [full[5], full[6]], ffn_w_gate, m_ffn_w_gate, v_ffn_w_gate),
        update("adamw_up", [full[7], full[8]], ffn_w_up, m_ffn_w_up, v_ffn_w_up),
        update("adamw_down", [full[9], full[10]], ffn_w_down, m_ffn_w_down, v_ffn_w_down),
    ]

    def leaves(k, small_vals):
        nm, nf, nfin, psc = unpack(small_vals)
        return [nm, nf, nfin, big[0][k], big[1][k], psc, big[2][k], big[3][k], big[4][k],
                big[5][k], big[6][k], big[7][k]]

    grad_x = dx0.reshape(x.shape)
    return (loss, grad_x, *leaves(0, sg), *leaves(1, sd), *leaves(2, sm), *leaves(3, sv))
```

```python
import math

import jax
import jax.numpy as jnp
from jax import lax
from jax.experimental import pallas as pl
from jax.experimental.pallas import tpu as pltpu

F32 = jnp.float32
BF16 = jnp.bfloat16

D_MODEL = 1024
N_SHARDS = 4
N_DEV = 8
D_FF = 2816
FF_SHARD = D_FF // N_SHARDS
HEAD_DIM = 64
QKV_SHARD = 3 * D_MODEL // N_SHARDS
POOL_WINDOWS = (2, 4, 8, 16)
POOL_GROUP_DIM = 256
POOL_HALO = 16
ATTN_W = 128
GROUP_LANES = (0, 384, 704, 1024)
GROUP_HEADS = (6, 5, 5)
GROUP_DIL = (1, 4, 16)
ROPE_THETA = 10000.0
EPS = 1e-6
NEG_INF = -1e30
LANE = 128
VMEM_LIMIT_BYTES = 60 * 1024 * 1024

ADAM_LR = 0.001
ADAM_B1 = 0.9
ADAM_B2 = 0.999
ADAM_EPS = 1e-08
ADAM_WD = 0.01
ADAM_STEP = 10

NT_DIMS = (((1,), (1,)), ((), ()))
TN_DIMS = (((0,), (0,)), ((), ()))
MESH = pl.DeviceIdType.MESH


_ANY = pl.BlockSpec(memory_space=pl.ANY)


def _pcall(body, *, name, out_shape, grid=None, in_specs=None, out_specs=None, scratch_shapes=(),
           semantics=None, duty=None):
    kw = {}
    if in_specs is not None and duty is None:
        kw["in_specs"] = in_specs
    if out_specs is not None and duty is None:
        kw["out_specs"] = out_specs
    if grid is not None:
        kw["grid"] = grid
    params = dict(dimension_semantics=semantics, vmem_limit_bytes=VMEM_LIMIT_BYTES)
    if duty is None:
        return pl.pallas_call(body, name=name, out_shape=out_shape, scratch_shapes=list(scratch_shapes),
                              compiler_params=pltpu.CompilerParams(**params), **kw)

    single = not isinstance(out_shape, (list, tuple))
    c_out_shape = [out_shape] if single else list(out_shape)
    c_out_specs = [out_specs] if single else list(out_specs)
    n_in, n_out, n_scr = len(in_specs), len(c_out_shape), len(scratch_shapes)
    d_in, d_out = len(duty.ins), len(duty.out_shape)
    total = math.prod(grid)
    mid_step = (2 * total) // 3

    def wrapped(*refs):
        c_in, d_ins = refs[:n_in], refs[n_in:n_in + d_in]
        o0 = n_in + d_in
        c_outs, d_outs = refs[o0:o0 + n_out], refs[o0 + n_out:o0 + n_out + d_out]
        s0 = o0 + n_out + d_out
        c_scr, d_sems = refs[s0:s0 + n_scr], refs[s0 + n_scr:]
        step = pl.program_id(0)
        for ax in range(1, len(grid)):
            step = step * grid[ax] + pl.program_id(ax)

        @pl.when(step == 0)
        def _():
            duty.start(d_ins, d_outs, d_sems)

        body(*c_in, *c_outs, *c_scr)

        @pl.when(step == mid_step)
        def _():
            duty.mid(d_ins, d_outs, d_sems)

        @pl.when(step == total - 1)
        def _():
            duty.finish(d_ins, d_outs, d_sems)

    call = pl.pallas_call(
        wrapped, name=name, grid=grid,
        in_specs=list(in_specs) + [_ANY] * d_in, out_specs=c_out_specs + [_ANY] * d_out,
        out_shape=c_out_shape + list(duty.out_shape),
        scratch_shapes=list(scratch_shapes) + list(duty.scratch),
        input_output_aliases={n_in + i: n_out + o for i, o in duty.aliases.items()},
        compiler_params=pltpu.CompilerParams(has_side_effects=True, **params))

    def run(*args):
        outs = call(*args, *duty.ins)
        c = outs[:n_out]
        return (c[0] if single else list(c)), list(outs[n_out:])

    return run


def _sds(shape, dtype):
    return jax.ShapeDtypeStruct(tuple(shape), dtype)


def _dot(a, b):
    return jnp.dot(a, b, preferred_element_type=F32)


def _dot_nt(a, b):
    return lax.dot_general(a, b, NT_DIMS, preferred_element_type=F32)


def _dot_tn(a, b):
    return lax.dot_general(a, b, TN_DIMS, preferred_element_type=F32)


def _rms_fwd(x, g):
    r = lax.rsqrt(jnp.mean(x * x, axis=-1, keepdims=True) + EPS)
    return x * r * g


def _rms_bwd(dh, x, g):
    r = lax.rsqrt(jnp.mean(x * x, axis=-1, keepdims=True) + EPS)
    xh = x * r
    dg = jnp.sum(dh * xh, axis=0, keepdims=True)
    dxh = dh * g
    dx = r * (dxh - xh * jnp.mean(dxh * xh, axis=-1, keepdims=True))
    return dx, dg


def _sigmoid(x):
    return 1.0 / (1.0 + jnp.exp(-x))


def _tile_rows(t):
    return min(512, t)


def _sub_tiles(tm, n_sub=2):
    rows = tm // n_sub
    return [pl.ds(i * rows, rows) for i in range(n_sub)]


def _wgrad_rows(t):
    return min(2048, t)


def pool_fwd(x, g_row, w_in, w_grp, scale, w_out, duty=None):
    t_len = x.shape[0]
    tm = _tile_rows(t_len)

    def body(x_ref, g_ref, win_ref, wgrp_ref, scale_ref, wout_ref,
             h_ref, p_ref, zr_ref, z_ref, xo_ref, ubuf):
        t = pl.program_id(0)

        @pl.when(t == 0)
        def _():
            ubuf[pl.ds(0, POOL_HALO), :] = jnp.zeros((POOL_HALO, D_MODEL), F32)

        x_t = x_ref[...]
        h = _rms_fwd(x_t, g_ref[...]).astype(BF16)
        h_ref[...] = h
        ubuf[pl.ds(POOL_HALO, tm), :] = _dot(h, win_ref[...])
        row = t * tm + lax.broadcasted_iota(jnp.int32, (tm, 1), 0)
        for gi, w in enumerate(POOL_WINDOWS):
            cols = pl.ds(gi * POOL_GROUP_DIM, POOL_GROUP_DIM)
            u_g = ubuf[pl.ds(POOL_HALO, tm), cols]
            acc = u_g
            for j in range(1, w):
                acc = acc + ubuf[pl.ds(POOL_HALO - j, tm), cols]
            cnt = jnp.minimum(row + 1, w).astype(F32)
            p_g = (acc / cnt - u_g).astype(BF16)
            p_ref[:, cols] = p_g
            z_g = _dot(p_g, wgrp_ref[gi])
            zr_ref[:, cols] = z_g.astype(BF16)
            z_ref[:, cols] = (z_g * scale_ref[:, cols]).astype(BF16)
        ubuf[pl.ds(0, POOL_HALO), :] = ubuf[pl.ds(tm, POOL_HALO), :]
        xo_ref[...] = x_t + _dot(z_ref[...], wout_ref[...])

    row_spec = pl.BlockSpec((tm, D_MODEL), lambda t: (t, 0))
    full2 = lambda shape: pl.BlockSpec(shape, lambda t: (0,) * len(shape))
    return _pcall(
        body, name="pool_fwd", grid=(t_len // tm,),
        in_specs=[row_spec, full2((1, D_MODEL)), full2((D_MODEL, D_MODEL)),
                  full2((4, POOL_GROUP_DIM, POOL_GROUP_DIM)), full2((1, D_MODEL)), full2((D_MODEL, D_MODEL))],
        out_specs=[row_spec] * 5,
        out_shape=[_sds((t_len, D_MODEL), BF16)] * 4 + [_sds((t_len, D_MODEL), F32)],
        scratch_shapes=[pltpu.VMEM((tm + POOL_HALO, D_MODEL), F32)],
        semantics=("arbitrary",), duty=duty,
    )(x, g_row, w_in, w_grp, scale, w_out)


def ffn_fwd(x, g_row, w_gate_t, w_up_t, w_down, name, duty=None):
    t_len = x.shape[0]
    tm = min(1024, t_len)

    def body(x_ref, g_ref, wg_ref, wu_ref, wd_ref, h_ref, go_ref, uo_ref, xo_ref, hbuf, acc):
        s = pl.program_id(1)

        @pl.when(s == 0)
        def _():
            h = _rms_fwd(x_ref[...], g_ref[...]).astype(BF16)
            hbuf[...] = h
            h_ref[...] = h
            acc[...] = jnp.zeros_like(acc)

        for rows in _sub_tiles(tm, 1):
            h = hbuf[rows, :]
            gate = _dot_nt(h, wg_ref[...])
            up = _dot_nt(h, wu_ref[...])
            go_ref[rows, :] = gate.astype(BF16)
            uo_ref[rows, :] = up.astype(BF16)
            act = (gate * _sigmoid(gate) * up).astype(BF16)
            acc[rows, :] += _dot(act, wd_ref[...])

        @pl.when(s == N_SHARDS - 1)
        def _():
            xo_ref[...] = x_ref[...] + acc[...]

    row_spec = pl.BlockSpec((tm, D_MODEL), lambda t, s: (t, 0))
    row_w = pl.BlockSpec((None, FF_SHARD, D_MODEL), lambda t, s: (s, 0, 0))
    act_spec = pl.BlockSpec((None, tm, FF_SHARD), lambda t, s: (s, t, 0))
    return _pcall(
        body, name=name, grid=(t_len // tm, N_SHARDS),
        in_specs=[row_spec, pl.BlockSpec((1, D_MODEL), lambda t, s: (0, 0)), row_w, row_w, row_w],
        out_specs=[row_spec, act_spec, act_spec, row_spec],
        out_shape=[_sds((t_len, D_MODEL), BF16), _sds((N_SHARDS, t_len, FF_SHARD), BF16),
                   _sds((N_SHARDS, t_len, FF_SHARD), BF16), _sds((t_len, D_MODEL), F32)],
        scratch_shapes=[pltpu.VMEM((tm, D_MODEL), BF16), pltpu.VMEM((tm, D_MODEL), F32)],
        semantics=("arbitrary", "arbitrary"), duty=duty,
    )(x, g_row, w_gate_t, w_up_t, w_down)


def ffn_bwd(dxo, x, g_row, gate, up, w_gate_t, w_up_t, w_down, name, duty=None):
    t_len = x.shape[0]
    tm = _tile_rows(t_len)

    def body(dxo_ref, x_ref, g_ref, gate_ref, up_ref, wg_ref, wu_ref, wd_ref,
             act_ref, dg_ref, du_ref, dx_ref, dn_ref, dxb, dh):
        t = pl.program_id(0)
        s = pl.program_id(1)

        @pl.when(s == 0)
        def _():
            dxb[...] = dxo_ref[...].astype(BF16)
            dh[...] = jnp.zeros_like(dh)

        @pl.when(jnp.logical_and(s == 0, t == 0))
        def _():
            dn_ref[...] = jnp.zeros_like(dn_ref)

        for rows in _sub_tiles(tm):
            dact = _dot_nt(dxb[rows, :], wd_ref[...])
            gv = gate_ref[rows, :].astype(F32)
            uv = up_ref[rows, :].astype(F32)
            sg = _sigmoid(gv)
            silu = gv * sg
            act_ref[rows, :] = (silu * uv).astype(BF16)
            dgv = (dact * uv * (sg * (1.0 + gv * (1.0 - sg)))).astype(BF16)
            duv = (dact * silu).astype(BF16)
            dg_ref[rows, :] = dgv
            du_ref[rows, :] = duv
            dh[rows, :] += _dot(dgv, wg_ref[...]) + _dot(duv, wu_ref[...])

        @pl.when(s == N_SHARDS - 1)
        def _():
            dx, dn = _rms_bwd(dh[...], x_ref[...], g_ref[...])
            dx_ref[...] = dxo_ref[...] + dx
            dn_ref[...] += dn

    row_spec = pl.BlockSpec((tm, D_MODEL), lambda t, s: (t, 0))
    vec_spec = pl.BlockSpec((1, D_MODEL), lambda t, s: (0, 0))
    row_w = pl.BlockSpec((None, FF_SHARD, D_MODEL), lambda t, s: (s, 0, 0))
    act_spec = pl.BlockSpec((None, tm, FF_SHARD), lambda t, s: (s, t, 0))
    act_shape = _sds((N_SHARDS, t_len, FF_SHARD), BF16)
    return _pcall(
        body, name=name, grid=(t_len // tm, N_SHARDS),
        in_specs=[row_spec, row_spec, vec_spec, act_spec, act_spec, row_w, row_w, row_w],
        out_specs=[act_spec, act_spec, act_spec, row_spec, vec_spec],
        out_shape=[act_shape, act_shape, act_shape, _sds((t_len, D_MODEL), F32), _sds((1, D_MODEL), F32)],
        scratch_shapes=[pltpu.VMEM((tm, D_MODEL), BF16), pltpu.VMEM((tm, D_MODEL), F32)],
        semantics=("arbitrary", "arbitrary"), duty=duty,
    )(dxo, x, g_row, gate, up, w_gate_t, w_up_t, w_down)


def tn_matmul(name, a, b, a_spec, b_spec, out_shape, out_spec, grid, duty=None):
    def body(a_ref, b_ref, o_ref):
        @pl.when(pl.program_id(len(grid) - 1) == 0)
        def _():
            o_ref[...] = jnp.zeros_like(o_ref)

        res = _dot_tn(a_ref[...].astype(BF16), b_ref[...].astype(BF16))
        o_ref[...] += res.reshape(o_ref.shape)

    return _pcall(body, name=name, grid=grid, in_specs=[a_spec, b_spec], out_specs=out_spec,
                  out_shape=out_shape, semantics=("arbitrary",) * len(grid), duty=duty)(a, b)


def wgrad_full(name, a, b, duty=None):
    t_len, k = a.shape
    n = b.shape[1]
    tt = _wgrad_rows(t_len)
    return tn_matmul(name, a, b,
                     pl.BlockSpec((tt, k), lambda t: (t, 0)), pl.BlockSpec((tt, n), lambda t: (t, 0)),
                     _sds((k, n), F32), pl.BlockSpec((k, n), lambda t: (0, 0)), (t_len // tt,), duty)


def wgrad_col_sharded(name, a, b_sh, duty=None):
    t_len, k = a.shape
    n_sh, _, n = b_sh.shape
    tt = _wgrad_rows(t_len)
    return tn_matmul(name, a, b_sh,
                     pl.BlockSpec((tt, k), lambda s, t: (t, 0)), pl.BlockSpec((None, tt, n), lambda s, t: (s, t, 0)),
                     _sds((n_sh, k, n), F32), pl.BlockSpec((None, k, n), lambda s, t: (s, 0, 0)),
                     (n_sh, t_len // tt), duty)


def wgrad_row_sharded(name, a_sh, b, duty=None):
    t_len, n = b.shape
    n_sh, _, k = a_sh.shape
    tt = _wgrad_rows(t_len)
    return tn_matmul(name, a_sh, b,
                     pl.BlockSpec((None, tt, k), lambda s, t: (s, t, 0)), pl.BlockSpec((tt, n), lambda s, t: (t, 0)),
                     _sds((n_sh, k, n), F32), pl.BlockSpec((None, k, n), lambda s, t: (s, 0, 0)),
                     (n_sh, t_len // tt), duty)


def wgrad_pool_group(name, p, dzs, duty=None):
    t_len = p.shape[0]
    tt = _wgrad_rows(t_len)
    gd = POOL_GROUP_DIM
    rows = gd // N_SHARDS
    return tn_matmul(name, p, dzs,
                     pl.BlockSpec((tt, gd), lambda g, t: (t, g)), pl.BlockSpec((tt, gd), lambda g, t: (t, g)),
                     _sds((N_SHARDS, 4, rows, gd), F32),
                     pl.BlockSpec((N_SHARDS, None, rows, gd), lambda g, t: (0, g, 0, 0)),
                     (4, t_len // tt), duty)


PAD_LANES = 384
QKV_PAD = 3 * PAD_LANES
N_SLABS = QKV_PAD // LANE
GROUP_REAL = tuple(GROUP_LANES[g + 1] - GROUP_LANES[g] for g in range(3))
Q_BLOCK = 512


def pad_groups(w, axis):
    parts = []
    for g in range(3):
        blk = lax.slice_in_dim(w, GROUP_LANES[g], GROUP_LANES[g + 1], axis=axis)
        pad = [(0, 0)] * w.ndim
        pad[axis] = (0, PAD_LANES - GROUP_REAL[g])
        parts.append(jnp.pad(blk, pad))
    return parts


def unpad_groups(parts, axis):
    return jnp.concatenate([lax.slice_in_dim(p, 0, GROUP_REAL[g], axis=axis) for g, p in enumerate(parts)],
                           axis=axis)


def pad_qkv_weight(w_qkv_sh):
    w = jnp.transpose(w_qkv_sh, (1, 0, 2)).reshape(D_MODEL, 3 * D_MODEL)
    q, k, v = (pad_groups(w[:, i * D_MODEL:(i + 1) * D_MODEL], 1) for i in range(3))
    return jnp.stack([jnp.concatenate([q[g], k[g], v[g]], axis=1) for g in range(3)])


def unpad_qkv_grad(g_pad):
    cols = [unpad_groups([g_pad[g][:, i * PAD_LANES:(i + 1) * PAD_LANES] for g in range(3)], 1) for i in range(3)]
    w = jnp.concatenate(cols, axis=1)
    return jnp.transpose(w.reshape(D_MODEL, N_SHARDS, QKV_SHARD), (1, 0, 2))


def rope_tables(t_len):
    inv_freq = 1.0 / (ROPE_THETA ** (jnp.arange(0, HEAD_DIM, 2, dtype=F32) / HEAD_DIM))
    ang = jnp.arange(t_len, dtype=F32)[:, None] * inv_freq[None, :]
    ang = jnp.concatenate([ang] * (2 * LANE // HEAD_DIM), axis=-1)
    return jnp.cos(ang), jnp.sin(ang)


def _rot_half(v):
    n = v.shape[1]
    lane = lax.broadcasted_iota(jnp.int32, v.shape, 1)
    return jnp.where(lane % HEAD_DIM < HEAD_DIM // 2,
                     -pltpu.roll(v, n - HEAD_DIM // 2, 1), pltpu.roll(v, HEAD_DIM // 2, 1))


def _lane_cols(j):
    return slice(j * LANE, (j + 1) * LANE)


def _to_residue_major(slab, j_src, dst_ref, j_dst, dil, rows):
    for r in range(dil):
        dst_ref[r, :, _lane_cols(j_dst)] = slab[j_src, pl.ds(r, rows // dil, stride=dil), :].astype(dst_ref.dtype)


def _to_natural(src_ref, j_src, slab, j_dst, dil, rows):
    for r in range(dil):
        slab[j_dst, pl.ds(r, rows // dil, stride=dil), :] = src_ref[r, :, _lane_cols(j_src)].astype(F32)


def qkv_fwd(x, g_row, w_pad, cos, sin):
    t_len = x.shape[0]
    tm = _tile_rows(t_len)

    def body(x_ref, g_ref, w_ref, cos_ref, sin_ref, h_ref, o1_ref, o4_ref, o16_ref, hbuf, slab):
        g = pl.program_id(1)

        @pl.when(g == 0)
        def _():
            h = _rms_fwd(x_ref[...], g_ref[...]).astype(BF16)
            hbuf[...] = h
            h_ref[...] = h

        acc = _dot(hbuf[...], w_ref[...])
        cos_t = cos_ref[...]
        sin_t = sin_ref[...]
        for j in range(N_SLABS):
            a = acc[:, _lane_cols(j)]
            if j < 6:
                a = a * cos_t + _rot_half(a) * sin_t
            if j < 3:
                a = a * (HEAD_DIM ** -0.5)
            slab[j] = a
        for gi, (dil, o_ref) in enumerate(zip(GROUP_DIL, (o1_ref, o4_ref, o16_ref))):
            @pl.when(g == gi)
            def _(dil=dil, o_ref=o_ref):
                for j in range(N_SLABS):
                    _to_residue_major(slab, j, o_ref, j, dil, tm)

    row_spec = pl.BlockSpec((tm, D_MODEL), lambda t, g: (t, 0))
    tab_spec = pl.BlockSpec((tm, LANE), lambda t, g: (t, 0))
    out_specs = [row_spec] + [pl.BlockSpec((d, tm // d, QKV_PAD), lambda t, g: (0, t, 0)) for d in GROUP_DIL]
    out_shape = [_sds((t_len, D_MODEL), BF16)] + [_sds((d, t_len // d, QKV_PAD), BF16) for d in GROUP_DIL]
    return _pcall(
        body, name="qkv_fwd", grid=(t_len // tm, 3),
        in_specs=[row_spec, pl.BlockSpec((1, D_MODEL), lambda t, g: (0, 0)),
                  pl.BlockSpec((None, D_MODEL, QKV_PAD), lambda t, g: (g, 0, 0)), tab_spec, tab_spec],
        out_specs=out_specs, out_shape=out_shape,
        scratch_shapes=[pltpu.VMEM((tm, D_MODEL), BF16), pltpu.VMEM((N_SLABS, tm, LANE), F32)],
        semantics=("arbitrary", "arbitrary"),
    )(x, g_row, w_pad, cos, sin)


def _band_mask(n):
    qi = lax.broadcasted_iota(jnp.int32, (ATTN_W, 2 * ATTN_W), 0)
    kj = lax.broadcasted_iota(jnp.int32, (ATTN_W, 2 * ATTN_W), 1)
    dist = ATTN_W + qi - kj
    return (dist >= 0) & (dist <= ATTN_W) & ((kj >= ATTN_W) | (n > 0))


def _half_masks():
    lane = lax.broadcasted_iota(jnp.int32, (1, LANE), 1)
    return [lane < HEAD_DIM, lane >= HEAD_DIM]


def _live_halves(gi, j):
    hms = _half_masks()
    return hms if (gi == 0 or j < 2) else hms[:1]


def attn_fwd(qkv_g, gi, name):
    dil, l_len, _ = qkv_g.shape
    qb = min(Q_BLOCK, l_len)
    nsub = qb // ATTN_W

    def body(q_ref, kc_ref, kp_ref, vc_ref, vp_ref, o_ref, lse_ref, kbuf, vbuf):
        n = pl.program_id(1)
        kbuf[pl.ds(0, ATTN_W), :] = kp_ref[...]
        kbuf[pl.ds(ATTN_W, qb), :] = kc_ref[...]
        vbuf[pl.ds(0, ATTN_W), :] = vp_ref[...]
        vbuf[pl.ds(ATTN_W, qb), :] = vc_ref[...]

        def sub(b, carry):
            r0 = pl.multiple_of(b * ATTN_W, ATTN_W)
            mask = _band_mask(n + b)
            for j in range(3):
                cols = _lane_cols(j)
                q = q_ref[pl.ds(r0, ATTN_W), cols]
                k = kbuf[pl.ds(r0, 2 * ATTN_W), cols]
                v = vbuf[pl.ds(r0, 2 * ATTN_W), cols]
                o = jnp.zeros((ATTN_W, LANE), F32)
                lse = jnp.zeros((ATTN_W, LANE), F32)
                for hm in _live_halves(gi, j):
                    qh = jnp.where(hm, q, jnp.zeros_like(q))
                    s = jnp.where(mask, _dot_nt(qh, k), NEG_INF)
                    m = jnp.max(s, axis=-1, keepdims=True)
                    e = jnp.exp(s - m)
                    den = jnp.sum(e, axis=-1, keepdims=True)
                    p = (e / den).astype(BF16)
                    o = jnp.where(hm, _dot(p, v), o)
                    lse = jnp.where(hm, m + jnp.log(den), lse)
                o_ref[pl.ds(r0, ATTN_W), cols] = o.astype(BF16)
                lse_ref[pl.ds(r0, ATTN_W), cols] = lse
            return carry

        lax.fori_loop(0, nsub, sub, 0)

    cur = lambda c: pl.BlockSpec((None, qb, PAD_LANES), lambda r, n: (r, n, c))
    prev = lambda c: pl.BlockSpec((None, ATTN_W, PAD_LANES), lambda r, n: (r, jnp.maximum(n * nsub - 1, 0), c))
    out_spec = pl.BlockSpec((None, qb, PAD_LANES), lambda r, n: (r, n, 0))
    return _pcall(
        body, name=name, grid=(dil, l_len // qb),
        in_specs=[cur(0), cur(1), prev(1), cur(2), prev(2)],
        out_specs=[out_spec, out_spec],
        out_shape=[_sds((dil, l_len, PAD_LANES), BF16), _sds((dil, l_len, PAD_LANES), F32)],
        scratch_shapes=[pltpu.VMEM((qb + ATTN_W, PAD_LANES), BF16), pltpu.VMEM((qb + ATTN_W, PAD_LANES), BF16)],
        semantics=("arbitrary", "arbitrary"),
    )(qkv_g, qkv_g, qkv_g, qkv_g, qkv_g)


def _group_stats(lse):
    lane = lax.broadcasted_iota(jnp.int32, (1, QKV_PAD), 1)
    gmask = [(lane >= g * PAD_LANES) & (lane < g * PAD_LANES + GROUP_REAL[g]) for g in range(3)]
    lses, glse = [], []
    for g in range(3):
        mx = jnp.max(jnp.where(gmask[g], lse, -jnp.inf), axis=-1, keepdims=True)
        sm = jnp.sum(jnp.where(gmask[g], jnp.exp(lse - mx), 0.0), axis=-1, keepdims=True) / HEAD_DIM
        full = mx + jnp.log(sm)
        lses.append(full)
        glse.append(full - math.log(GROUP_HEADS[g]))
    top = jnp.maximum(jnp.maximum(glse[0], glse[1]), glse[2])
    ex = [jnp.exp(v - top) for v in glse]
    tot = ex[0] + ex[1] + ex[2]
    alpha = [v / tot for v in ex]
    pick = lambda vals: jnp.where(lane < PAD_LANES, vals[0], jnp.where(lane < 2 * PAD_LANES, vals[1], vals[2]))
    return gmask, alpha, pick([3.0 * a for a in alpha]), lse - pick(lses), pick


def attn_out_fwd(x, o_parts, lse_parts, w_out_pad):
    t_len = x.shape[0]
    tm = _tile_rows(t_len)

    def body(x_ref, o1, o4, o16, l1, l4, l16, w_ref, xo_ref, mg_ref, o_ref, lse_ref, o_slab, l_slab):
        for gi, (dil, og, lg) in enumerate(zip(GROUP_DIL, (o1, o4, o16), (l1, l4, l16))):
            for j in range(3):
                _to_natural(og, j, o_slab, 3 * gi + j, dil, tm)
                _to_natural(lg, j, l_slab, 3 * gi + j, dil, tm)
        o = jnp.concatenate([o_slab[j] for j in range(N_SLABS)], axis=1)
        lse = jnp.concatenate([l_slab[j] for j in range(N_SLABS)], axis=1)
        o_ref[...] = o.astype(BF16)
        lse_ref[...] = lse
        _, _, scale, _, _ = _group_stats(lse)
        merged = (o * scale).astype(BF16)
        mg_ref[...] = merged
        xo_ref[...] = x_ref[...] + _dot(merged, w_ref[...])

    row_spec = pl.BlockSpec((tm, D_MODEL), lambda t: (t, 0))
    pad_spec = pl.BlockSpec((tm, QKV_PAD), lambda t: (t, 0))
    part_specs = [pl.BlockSpec((d, tm // d, PAD_LANES), lambda t: (0, t, 0)) for d in GROUP_DIL]
    return _pcall(
        body, name="attn_out_fwd", grid=(t_len // tm,),
        in_specs=[row_spec] + part_specs + part_specs + [pl.BlockSpec((QKV_PAD, D_MODEL), lambda t: (0, 0))],
        out_specs=[row_spec, pad_spec, pad_spec, pad_spec],
        out_shape=[_sds((t_len, D_MODEL), F32), _sds((t_len, QKV_PAD), BF16),
                   _sds((t_len, QKV_PAD), BF16), _sds((t_len, QKV_PAD), F32)],
        scratch_shapes=[pltpu.VMEM((N_SLABS, tm, LANE), F32), pltpu.VMEM((N_SLABS, tm, LANE), F32)],
        semantics=("arbitrary",),
    )(x, *o_parts, *lse_parts, w_out_pad)


def attn_out_bwd(dxo, w_out_pad, o, lse):
    t_len = dxo.shape[0]
    tm = _tile_rows(t_len)

    def body(dx_ref, w_ref, o_ref, lse_ref, d1, d4, d16, c1, c4, c16, slab):
        dmerged = _dot_nt(dx_ref[...].astype(BF16), w_ref[...])
        o_t = o_ref[...].astype(F32)
        gmask, alpha, scale, lse_rel, pick = _group_stats(lse_ref[...])
        e = dmerged * o_t
        dalpha = [3.0 * jnp.sum(jnp.where(gmask[g], e, 0.0), axis=-1, keepdims=True) for g in range(3)]
        mean_da = alpha[0] * dalpha[0] + alpha[1] * dalpha[1] + alpha[2] * dalpha[2]
        dglse = [alpha[g] * (dalpha[g] - mean_da) for g in range(3)]
        dlse = pick(dglse) * jnp.exp(lse_rel)
        do = dmerged * scale
        es = e * scale
        lane = lax.broadcasted_iota(jnp.int32, (1, LANE), 1)
        first = lane < HEAD_DIM
        for j in range(N_SLABS):
            slab[j] = do[:, _lane_cols(j)]
        for gi, (dil, dg) in enumerate(zip(GROUP_DIL, (d1, d4, d16))):
            for j in range(3):
                _to_residue_major(slab, 3 * gi + j, dg, j, dil, tm)
        for j in range(N_SLABS):
            blk = es[:, _lane_cols(j)]
            s0 = jnp.sum(jnp.where(first, blk, 0.0), axis=-1, keepdims=True)
            s1 = jnp.sum(jnp.where(first, 0.0, blk), axis=-1, keepdims=True)
            slab[j] = jnp.where(first, s0, s1) - dlse[:, _lane_cols(j)]
        for gi, (dil, cg) in enumerate(zip(GROUP_DIL, (c1, c4, c16))):
            for j in range(3):
                _to_residue_major(slab, 3 * gi + j, cg, j, dil, tm)

    row_spec = pl.BlockSpec((tm, D_MODEL), lambda t: (t, 0))
    pad_spec = pl.BlockSpec((tm, QKV_PAD), lambda t: (t, 0))
    part_specs = [pl.BlockSpec((d, tm // d, PAD_LANES), lambda t: (0, t, 0)) for d in GROUP_DIL]
    shapes = lambda dt: [_sds((d, t_len // d, PAD_LANES), dt) for d in GROUP_DIL]
    outs = _pcall(
        body, name="attn_out_bwd", grid=(t_len // tm,),
        in_specs=[row_spec, pl.BlockSpec((QKV_PAD, D_MODEL), lambda t: (0, 0)), pad_spec, pad_spec],
        out_specs=part_specs + part_specs,
        out_shape=shapes(BF16) + shapes(F32),
        scratch_shapes=[pltpu.VMEM((N_SLABS, tm, LANE), F32)],
        semantics=("arbitrary",),
    )(dxo, w_out_pad, o, lse)
    return outs[:3], outs[3:]


def attn_bwd(qkv_g, do_g, lse_g, c_g, gi, name, duty=None):
    dil, l_len, _ = qkv_g.shape
    qb = min(Q_BLOCK, l_len)
    nsub = qb // ATTN_W
    nsb = l_len // qb

    def body(q_ref, kc_ref, kp_ref, vc_ref, vp_ref, do_ref, lse_ref, c_ref,
             qn_ref, don_ref, lsen_ref, cn_ref, o_ref, kbuf, vbuf, dkbuf, dvbuf):
        n = pl.program_id(1)
        kbuf[pl.ds(0, ATTN_W), :] = kp_ref[...]
        kbuf[pl.ds(ATTN_W, qb), :] = kc_ref[...]
        vbuf[pl.ds(0, ATTN_W), :] = vp_ref[...]
        vbuf[pl.ds(ATTN_W, qb), :] = vc_ref[...]
        dkbuf[...] = jnp.zeros_like(dkbuf)
        dvbuf[...] = jnp.zeros_like(dvbuf)

        def tile(q, do_t, lse_t, c_t, k, v, mask, hms):
            dq = jnp.zeros((ATTN_W, LANE), F32)
            dk = jnp.zeros((k.shape[0], LANE), F32)
            dv = jnp.zeros((k.shape[0], LANE), F32)
            for hm in hms:
                qh = jnp.where(hm, q, jnp.zeros_like(q))
                doh = jnp.where(hm, do_t, jnp.zeros_like(do_t))
                lse_h = jnp.max(jnp.where(hm, lse_t, -jnp.inf), axis=-1, keepdims=True)
                c_h = jnp.max(jnp.where(hm, c_t, -jnp.inf), axis=-1, keepdims=True)
                s = jnp.where(mask, _dot_nt(qh, k), NEG_INF)
                p = jnp.exp(s - lse_h)
                dp = _dot_nt(doh, v)
                ds = (p * (dp - c_h)).astype(BF16)
                dq = jnp.where(hm, _dot(ds, k), dq)
                dk = dk + _dot_tn(ds, qh)
                dv = dv + _dot_tn(p.astype(BF16), doh)
            return dq, dk, dv

        def sub(b, carry):
            r0 = pl.multiple_of(b * ATTN_W, ATTN_W)
            mask = _band_mask(n + b)
            rows = pl.ds(r0, ATTN_W)
            krows = pl.ds(r0, 2 * ATTN_W)
            for j in range(3):
                cols = _lane_cols(j)
                dq, dk, dv = tile(q_ref[rows, cols], do_ref[rows, cols], lse_ref[rows, cols], c_ref[rows, cols],
                                  kbuf[krows, cols], vbuf[krows, cols], mask, _live_halves(gi, j))
                o_ref[rows, cols] = dq.astype(BF16)
                dkbuf[krows, cols] += dk
                dvbuf[krows, cols] += dv
            return carry

        lax.fori_loop(0, nsub, sub, 0)

        qi = lax.broadcasted_iota(jnp.int32, (ATTN_W, ATTN_W), 0)
        kj = lax.broadcasted_iota(jnp.int32, (ATTN_W, ATTN_W), 1)
        nmask = (qi <= kj) & (n < nsb - 1)
        last = pl.ds(qb, ATTN_W)
        for j in range(3):
            cols = _lane_cols(j)
            _, dk, dv = tile(qn_ref[:, cols], don_ref[:, cols], lsen_ref[:, cols], cn_ref[:, cols],
                             kbuf[last, cols], vbuf[last, cols], nmask, _live_halves(gi, j))
            dkbuf[last, cols] += dk
            dvbuf[last, cols] += dv
        o_ref[:, pl.ds(PAD_LANES, PAD_LANES)] = dkbuf[pl.ds(ATTN_W, qb), :].astype(BF16)
        o_ref[:, pl.ds(2 * PAD_LANES, PAD_LANES)] = dvbuf[pl.ds(ATTN_W, qb), :].astype(BF16)

    cur = lambda c: pl.BlockSpec((None, qb, PAD_LANES), lambda r, n: (r, n, c))
    prev = lambda c: pl.BlockSpec((None, ATTN_W, PAD_LANES), lambda r, n: (r, jnp.maximum(n * nsub - 1, 0), c))
    nxt = pl.BlockSpec((None, ATTN_W, PAD_LANES), lambda r, n: (r, jnp.minimum((n + 1) * nsub, nsb * nsub - 1), 0))
    return _pcall(
        body, name=name, grid=(dil, nsb),
        in_specs=[cur(0), cur(1), prev(1), cur(2), prev(2), cur(0), cur(0), cur(0), nxt, nxt, nxt, nxt],
        out_specs=pl.BlockSpec((None, qb, QKV_PAD), lambda r, n: (r, n, 0)),
        out_shape=_sds((dil, l_len, QKV_PAD), BF16),
        scratch_shapes=[pltpu.VMEM((qb + ATTN_W, PAD_LANES), BF16), pltpu.VMEM((qb + ATTN_W, PAD_LANES), BF16),
                        pltpu.VMEM((qb + ATTN_W, PAD_LANES), F32), pltpu.VMEM((qb + ATTN_W, PAD_LANES), F32)],
        semantics=("arbitrary", "arbitrary"), duty=duty,
    )(qkv_g, qkv_g, qkv_g, qkv_g, qkv_g, do_g, lse_g, c_g, qkv_g, do_g, lse_g, c_g)


def qkv_bwd(dqkv_parts, w_pad, dxo, x, g_row, cos, sin):
    t_len = x.shape[0]
    tm = _tile_rows(t_len)

    def body(p1, p4, p16, w_ref, dxo_ref, x_ref, g_ref, cos_ref, sin_ref, dq_ref, dx_ref, dn_ref, dh, slab):
        t = pl.program_id(0)
        g = pl.program_id(1)

        @pl.when(g == 0)
        def _():
            dh[...] = jnp.zeros_like(dh)

        @pl.when(jnp.logical_and(g == 0, t == 0))
        def _():
            dn_ref[...] = jnp.zeros_like(dn_ref)

        for gi, (dil, part) in enumerate(zip(GROUP_DIL, (p1, p4, p16))):
            @pl.when(g == gi)
            def _(dil=dil, part=part):
                for j in range(N_SLABS):
                    _to_natural(part, j, slab, j, dil, tm)

        cos_t = cos_ref[...]
        sin_t = sin_ref[...]
        for j in range(N_SLABS):
            a = slab[j]
            if j < 6:
                a = a * cos_t - _rot_half(a * sin_t)
            if j < 3:
                a = a * (HEAD_DIM ** -0.5)
            dq_ref[:, _lane_cols(j)] = a.astype(BF16)
        dh[...] += _dot_nt(dq_ref[...], w_ref[...])

        @pl.when(g == 2)
        def _():
            dx, dn = _rms_bwd(dh[...], x_ref[...], g_ref[...])
            dx_ref[...] = dxo_ref[...] + dx
            dn_ref[...] += dn

    row_spec = pl.BlockSpec((tm, D_MODEL), lambda t, g: (t, 0))
    vec_spec = pl.BlockSpec((1, D_MODEL), lambda t, g: (0, 0))
    tab_spec = pl.BlockSpec((tm, LANE), lambda t, g: (t, 0))
    part_specs = [pl.BlockSpec((d, tm // d, QKV_PAD), lambda t, g: (0, t, 0)) for d in GROUP_DIL]
    return _pcall(
        body, name="qkv_bwd", grid=(t_len // tm, 3),
        in_specs=part_specs + [pl.BlockSpec((None, D_MODEL, QKV_PAD), lambda t, g: (g, 0, 0)),
                               row_spec, row_spec, vec_spec, tab_spec, tab_spec],
        out_specs=[pl.BlockSpec((None, tm, QKV_PAD), lambda t, g: (g, t, 0)), row_spec, vec_spec],
        out_shape=[_sds((3, t_len, QKV_PAD), BF16), _sds((t_len, D_MODEL), F32), _sds((1, D_MODEL), F32)],
        scratch_shapes=[pltpu.VMEM((tm, D_MODEL), F32), pltpu.VMEM((N_SLABS, tm, LANE), F32)],
        semantics=("arbitrary", "arbitrary"),
    )(*dqkv_parts, w_pad, dxo, x, g_row, cos, sin)


def final_fwd_bwd(x, g_row, target):
    t_len = x.shape[0]
    tm = _tile_rows(t_len)

    def body(x_ref, g_ref, tgt_ref, dx_ref, dn_ref, loss_ref):
        @pl.when(pl.program_id(0) == 0)
        def _():
            dn_ref[...] = jnp.zeros_like(dn_ref)
            loss_ref[...] = jnp.zeros_like(loss_ref)

        x_t = x_ref[...]
        g = g_ref[...]
        diff = _rms_fwd(x_t, g) - tgt_ref[...]
        loss_ref[...] += 0.5 * jnp.sum(jnp.mean(diff * diff, axis=-1, keepdims=True), axis=0, keepdims=True)
        dx, dn = _rms_bwd(diff * (1.0 / D_MODEL), x_t, g)
        dx_ref[...] = dx
        dn_ref[...] += dn

    row_spec = pl.BlockSpec((tm, D_MODEL), lambda t: (t, 0))
    vec_spec = pl.BlockSpec((1, D_MODEL), lambda t: (0, 0))
    return _pcall(
        body, name="final_fwd_bwd", grid=(t_len // tm,),
        in_specs=[row_spec, vec_spec, row_spec],
        out_specs=[row_spec, vec_spec, pl.BlockSpec((1, 1), lambda t: (0, 0))],
        out_shape=[_sds((t_len, D_MODEL), F32), _sds((1, D_MODEL), F32), _sds((1, 1), F32)],
        semantics=("arbitrary",),
    )(x, g_row, target)


def pool_bwd(dxo, x, g_row, w_in, w_grp, scale, w_out, zr, duty=None):
    t_len = x.shape[0]
    tm = _tile_rows(t_len)
    nt = t_len // tm

    def body(dxo_ref, x_ref, g_ref, win_ref, wgrp_ref, scale_ref, wout_ref, zr_ref,
             dzs_ref, du_ref, dx_ref, dn_ref, dsc_ref, ebuf):
        i = pl.program_id(0)
        t = nt - 1 - i

        @pl.when(i == 0)
        def _():
            ebuf[pl.ds(tm, POOL_HALO), :] = jnp.zeros((POOL_HALO, D_MODEL), F32)
            dn_ref[...] = jnp.zeros_like(dn_ref)
            dsc_ref[...] = jnp.zeros_like(dsc_ref)

        dxo_t = dxo_ref[...]
        dz = _dot_nt(dxo_t.astype(BF16), wout_ref[...])
        dsc_ref[...] += jnp.sum(dz * zr_ref[...].astype(F32), axis=0, keepdims=True)
        dzs_ref[...] = (dz * scale_ref[...]).astype(BF16)
        row = t * tm + lax.broadcasted_iota(jnp.int32, (tm, 1), 0)
        for gi, w in enumerate(POOL_WINDOWS):
            cols = pl.ds(gi * POOL_GROUP_DIM, POOL_GROUP_DIM)
            dp_g = _dot_nt(dzs_ref[:, cols], wgrp_ref[gi])
            cnt = jnp.minimum(row + 1, w).astype(F32)
            ebuf[pl.ds(0, tm), cols] = dp_g / cnt
            acc = -dp_g
            for j in range(w):
                acc = acc + ebuf[pl.ds(j, tm), cols]
            du_ref[:, cols] = acc.astype(BF16)
        ebuf[pl.ds(tm, POOL_HALO), :] = ebuf[pl.ds(0, POOL_HALO), :]
        dh = _dot_nt(du_ref[...], win_ref[...])
        dx, dn = _rms_bwd(dh, x_ref[...], g_ref[...])
        dx_ref[...] = dxo_t + dx
        dn_ref[...] += dn

    row_spec = pl.BlockSpec((tm, D_MODEL), lambda i: (nt - 1 - i, 0))
    full = lambda shape: pl.BlockSpec(shape, lambda i: (0,) * len(shape))
    vec = full((1, D_MODEL))
    return _pcall(
        body, name="pool_bwd", grid=(nt,),
        in_specs=[row_spec, row_spec, vec, full((D_MODEL, D_MODEL)), full((4, POOL_GROUP_DIM, POOL_GROUP_DIM)),
                  vec, full((D_MODEL, D_MODEL)), row_spec],
        out_specs=[row_spec, row_spec, row_spec, vec, vec],
        out_shape=[_sds((t_len, D_MODEL), BF16), _sds((t_len, D_MODEL), BF16), _sds((t_len, D_MODEL), F32),
                   _sds((1, D_MODEL), F32), _sds((1, D_MODEL), F32)],
        scratch_shapes=[pltpu.VMEM((tm + POOL_HALO, D_MODEL), F32)],
        semantics=("arbitrary",), duty=duty,
    )(dxo, x, g_row, w_in, w_grp, scale, w_out, zr)


def _mesh_pos():
    return lax.axis_index("x"), lax.axis_index("y"), lax.axis_index("c")


def _other_chips(x, y):
    return [(1 - x, y), (x, 1 - y), (1 - x, 1 - y)]


def _remote(src, dst, send_sem, recv_sem, device):
    return pltpu.make_async_remote_copy(src_ref=src, dst_ref=dst, send_sem=send_sem, recv_sem=recv_sem,
                                        device_id=device, device_id_type=MESH)


class _Duty:
    aliases = {}

    def mid(self, ins, outs, sems):
        pass


def run_duty(duty, name):
    d_in, d_out = len(duty.ins), len(duty.out_shape)

    def body(*refs):
        ins, outs, sems = refs[:d_in], refs[d_in:d_in + d_out], refs[d_in + d_out:]
        duty.start(ins, outs, sems)
        duty.mid(ins, outs, sems)
        duty.finish(ins, outs, sems)

    return pl.pallas_call(
        body, name=name, out_shape=list(duty.out_shape), in_specs=[_ANY] * d_in, out_specs=[_ANY] * d_out,
        scratch_shapes=list(duty.scratch), input_output_aliases=dict(duty.aliases),
        compiler_params=pltpu.CompilerParams(has_side_effects=True),
    )(*duty.ins)


class GatherWeights(_Duty):
    def __init__(self, shards):
        n = self.n = len(shards)
        self.halves = [s.shape[0] // 2 for s in shards]
        my_slot = 2 * lax.axis_index("x") + lax.axis_index("y")
        staged = [lax.dynamic_update_slice(lax.empty((N_SHARDS,) + s.shape, s.dtype), s[None], (my_slot, 0, 0))
                  for s in shards]
        self.ins = list(shards) + staged
        self.out_shape = [_sds((N_SHARDS,) + s.shape, s.dtype) for s in shards]
        self.aliases = {n + a: a for a in range(n)}
        self.scratch = [pltpu.SemaphoreType.DMA((n, 6)), pltpu.SemaphoreType.DMA((n, 6))]

    def _over_ici(self, ins, outs, sems):
        x, y, c = _mesh_pos()
        return [_remote(ins[a].at[pl.ds(c * h, h)], outs[a].at[2 * x + y, pl.ds(c * h, h)],
                        sems[0].at[a, j], sems[1].at[a, j], (*chip, c))
                for a, h in enumerate(self.halves) for j, chip in enumerate(_other_chips(x, y))]

    def _forwards(self, outs, sems, half_of):
        x, y, c = _mesh_pos()
        cps = []
        for a, h in enumerate(self.halves):
            for j, chip in enumerate(_other_chips(x, y)):
                slot = outs[a].at[2 * chip[0] + chip[1], pl.ds(half_of(c) * h, h)]
                cps.append(_remote(slot, slot, sems[0].at[a, 3 + j], sems[1].at[a, 3 + j], (x, y, 1 - c)))
        return cps

    def start(self, ins, outs, sems):
        for cp in self._over_ici(ins, outs, sems):
            cp.start()

    def mid(self, ins, outs, sems):
        x, y, c = _mesh_pos()
        forwards = self._forwards(outs, sems, lambda core: core)
        k = 0
        for a, h in enumerate(self.halves):
            for j, chip in enumerate(_other_chips(x, y)):
                slot = outs[a].at[2 * chip[0] + chip[1], pl.ds(c * h, h)]
                _remote(slot, slot, sems[0].at[a, j], sems[1].at[a, j], (*chip, c)).wait_recv()
                forwards[k].start()
                k += 1

    def finish(self, ins, outs, sems):
        for cp in self._forwards(outs, sems, lambda core: 1 - core):
            cp.wait_recv()
        for cp in self._over_ici(ins, outs, sems) + self._forwards(outs, sems, lambda core: core):
            cp.wait_send()


class SiblingHalves(_Duty):
    def __init__(self, grads):
        n = len(grads)
        self.halves = [g.shape[1] // 2 for g in grads]
        self.ins = list(grads)
        self.out_shape = [_sds((N_SHARDS, h, g.shape[2]), g.dtype) for g, h in zip(grads, self.halves)]
        self.scratch = [pltpu.SemaphoreType.DMA((n,)), pltpu.SemaphoreType.DMA((n,))]

    def _copies(self, ins, outs, sems):
        x, y, c = _mesh_pos()
        return [_remote(ins[a].at[:, pl.ds((1 - c) * h, h)], outs[a], sems[0].at[a], sems[1].at[a], (x, y, 1 - c))
                for a, h in enumerate(self.halves)]

    def start(self, ins, outs, sems):
        for cp in self._copies(ins, outs, sems):
            cp.start()

    def finish(self, ins, outs, sems):
        for cp in self._copies(ins, outs, sems):
            cp.wait()


class ChipExchange(_Duty):
    def __init__(self, parts):
        n = self.n = len(parts)
        self.ins = list(parts)
        self.out_shape = [_sds(p.shape, p.dtype) for p in parts]
        self.scratch = [pltpu.SemaphoreType.DMA((n, 3)), pltpu.SemaphoreType.DMA((n, 3))]

    def _copies(self, ins, outs, sems, arriving):
        x, y, c = _mesh_pos()
        cps = []
        for a in range(self.n):
            for j, chip in enumerate(_other_chips(x, y)):
                theirs = 2 * chip[0] + chip[1]
                src = outs[a].at[theirs] if arriving else ins[a].at[theirs]
                dst = outs[a].at[theirs] if arriving else outs[a].at[2 * x + y]
                cps.append(_remote(src, dst, sems[0].at[a, j], sems[1].at[a, j], (*chip, c)))
        return cps

    def start(self, ins, outs, sems):
        for cp in self._copies(ins, outs, sems, False):
            cp.start()

    def finish(self, ins, outs, sems):
        for cp in self._copies(ins, outs, sems, True):
            cp.wait_recv()
        for cp in self._copies(ins, outs, sems, False):
            cp.wait_send()


class SiblingShare(_Duty):
    def __init__(self, reduced):
        n = self.n = len(reduced)
        self.ins = list(reduced)
        self.out_shape = [_sds(r.shape, r.dtype) for r in reduced]
        self.aliases = {a: a for a in range(n)}
        self.scratch = [pltpu.SemaphoreType.DMA((n,)), pltpu.SemaphoreType.DMA((n,))]

    def _copies(self, outs, sems, half_of):
        x, y, c = _mesh_pos()
        cps = []
        for a in range(self.n):
            h = outs[a].shape[0] // 2
            rows = outs[a].at[pl.ds(half_of(c) * h, h)]
            cps.append(_remote(rows, rows, sems[0].at[a], sems[1].at[a], (x, y, 1 - c)))
        return cps

    def start(self, ins, outs, sems):
        for cp in self._copies(outs, sems, lambda core: core):
            cp.start()

    def finish(self, ins, outs, sems):
        for cp in self._copies(outs, sems, lambda core: 1 - core):
            cp.wait_recv()
        for cp in self._copies(outs, sems, lambda core: core):
            cp.wait_send()


def allreduce_small(v):
    def body(v_ref, o_ref, buf, send_sems, recv_sems):
        x, y, c = _mesh_pos()
        me = 4 * x + 2 * y + c
        buf[me] = v_ref[...]
        flip = lambda p, f: 1 - p if f else p
        peers = [(flip(x, k & 4), flip(y, k & 2), flip(c, k & 1)) for k in range(1, N_DEV)]
        cps = []
        for k, peer in enumerate(peers):
            cp = _remote(v_ref, buf.at[me], send_sems.at[k], recv_sems.at[k], peer)
            cp.start()
            cps.append(cp)
        for k, peer in enumerate(peers):
            slot = buf.at[4 * peer[0] + 2 * peer[1] + peer[2]]
            _remote(slot, slot, send_sems.at[k], recv_sems.at[k], peer).wait_recv()
        for cp in cps:
            cp.wait_send()
        acc = buf[0]
        for i in range(1, N_DEV):
            acc = acc + buf[i]
        o_ref[...] = acc

    vm = pl.BlockSpec(memory_space=pltpu.VMEM)
    return pl.pallas_call(
        body, name="allreduce_small", out_shape=_sds(v.shape, v.dtype), in_specs=[vm], out_specs=vm,
        scratch_shapes=[pltpu.VMEM((N_DEV,) + v.shape, v.dtype),
                        pltpu.SemaphoreType.DMA((N_DEV - 1,)), pltpu.SemaphoreType.DMA((N_DEV - 1,))],
        compiler_params=pltpu.CompilerParams(has_side_effects=True),
    )(v)


def add_my_half(grad, theirs, c_idx, name):
    _, r, cols = grad.shape
    h = r // 2

    def body(c_ref, g_ref, t_ref, o_ref):
        o_ref[...] = (g_ref[...] + t_ref[...]).astype(BF16)

    slot = pl.BlockSpec((None, h, cols), lambda s, c: (s, 0, 0))
    grid_spec = pltpu.PrefetchScalarGridSpec(
        num_scalar_prefetch=1, grid=(N_SHARDS,),
        in_specs=[pl.BlockSpec((None, h, cols), lambda s, c: (s, c[0], 0)), slot], out_specs=slot)
    return pl.pallas_call(
        body, name=name, grid_spec=grid_spec, out_shape=_sds((N_SHARDS, h, cols), BF16),
        compiler_params=pltpu.CompilerParams(dimension_semantics=("arbitrary",), vmem_limit_bytes=VMEM_LIMIT_BYTES),
    )(c_idx, grad, theirs)


def sum_slots(received, mine, pos_idx, name):
    _, h, cols = received.shape

    def body(pos_ref, r_ref, m_ref, o_ref):
        acc = None
        for k in range(N_SHARDS):
            term = jnp.where(pos_ref[0] == k, m_ref[k], r_ref[k]).astype(F32)
            acc = term if acc is None else acc + term
        o_ref[...] = acc

    whole = pl.BlockSpec((N_SHARDS, h, cols), lambda i, pos: (0, 0, 0))
    grid_spec = pltpu.PrefetchScalarGridSpec(
        num_scalar_prefetch=1, grid=(1,), in_specs=[whole, whole],
        out_specs=pl.BlockSpec((h, cols), lambda i, pos: (pos[1], 0)))
    return pl.pallas_call(
        body, name=name, grid_spec=grid_spec, out_shape=_sds((2 * h, cols), F32),
        compiler_params=pltpu.CompilerParams(dimension_semantics=("arbitrary",), vmem_limit_bytes=VMEM_LIMIT_BYTES),
    )(pos_idx, received, mine)


def adamw(name, grads, w, m, v):
    n_layers, r, cols = w.shape
    tr = r // 2 if r % 16 == 0 else r
    bias1 = 1.0 - ADAM_B1 ** ADAM_STEP
    bias2 = 1.0 - ADAM_B2 ** ADAM_STEP

    def body(*refs):
        g_refs = refs[:n_layers]
        w_ref, m_ref, v_ref, go_ref, d_ref, mo_ref, vo_ref = refs[n_layers:]
        g = g_refs[0][...]
        for layer in range(1, n_layers):
            g = jnp.where(pl.program_id(0) == layer, g_refs[layer][...], g)
        m_new = ADAM_B1 * m_ref[...] + (1.0 - ADAM_B1) * g
        v_new = ADAM_B2 * v_ref[...] + (1.0 - ADAM_B2) * (g * g)
        m_hat = m_new / bias1
        v_hat = v_new / bias2
        go_ref[...] = g
        d_ref[...] = -ADAM_LR * (m_hat / (jnp.sqrt(v_hat) + ADAM_EPS) + ADAM_WD * w_ref[...])
        mo_ref[...] = m_new
        vo_ref[...] = v_new

    g_spec = pl.BlockSpec((tr, cols), lambda l, i: (i, 0))
    lay_spec = pl.BlockSpec((None, tr, cols), lambda l, i: (l, i, 0))
    shape = _sds((n_layers, r, cols), F32)
    return _pcall(
        body, name=name, grid=(n_layers, r // tr),
        in_specs=[g_spec] * n_layers + [lay_spec] * 3, out_specs=[lay_spec] * 4,
        out_shape=[shape] * 4, semantics=("arbitrary", "arbitrary"),
    )(*grads, w, m, v)


def kernel(x, norm_mix, norm_ffn, norm_final, pool_w_in, pool_w_group, pool_scale, pool_w_out, attn_w_qkv, attn_w_out, ffn_w_gate, ffn_w_up, ffn_w_down, loss_target, m_norm_mix, m_norm_ffn, m_norm_final, m_pool_w_in, m_pool_w_group, m_pool_scale, m_pool_w_out, m_attn_w_qkv, m_attn_w_out, m_ffn_w_gate, m_ffn_w_up, m_ffn_w_down, v_norm_mix, v_norm_ffn, v_norm_final, v_pool_w_in, v_pool_w_group, v_pool_scale, v_pool_w_out, v_attn_w_qkv, v_attn_w_out, v_ffn_w_gate, v_ffn_w_up, v_ffn_w_down):
    t_len = x.shape[1]
    x0 = x.reshape(t_len, D_MODEL)
    target = loss_target.reshape(t_len, D_MODEL)
    row = lambda a: a.reshape(1, D_MODEL)

    grp_rows = POOL_GROUP_DIM // N_SHARDS
    bf = lambda a: a.astype(BF16)
    gate_t, up_t = jnp.swapaxes(ffn_w_gate, 1, 2), jnp.swapaxes(ffn_w_up, 1, 2)
    pool_shards = [bf(pool_w_in[0]), bf(pool_w_group[0].reshape(4 * grp_rows, POOL_GROUP_DIM)), bf(pool_w_out[0])]
    ffn0_shards = [bf(gate_t[0]), bf(up_t[0]), bf(ffn_w_down[0])]
    late_shards = [bf(attn_w_qkv[0]), bf(attn_w_out[0]), bf(gate_t[1]), bf(up_t[1]), bf(ffn_w_down[1])]
    cos, sin = rope_tables(t_len)
    c_idx = lax.axis_index("c").astype(jnp.int32).reshape(1)
    pos_idx = jnp.stack([2 * lax.axis_index("x") + lax.axis_index("y"), lax.axis_index("c")]).astype(jnp.int32)
    chip_rows = lambda g: g.reshape(N_SHARDS, D_MODEL // N_SHARDS, D_MODEL)

    def add_halves(grads, theirs, names):
        return [add_my_half(g, t, c_idx, f"rs_add_{nm}") for g, t, nm in zip(grads, theirs, names)]

    def sum_chips(received, partials, names):
        return [sum_slots(r, p_, pos_idx, f"rs_sum_{nm}") for r, p_, nm in zip(received, partials, names)]

    g_pool = run_duty(GatherWeights(pool_shards), "gather_pool")
    w_in = g_pool[0].reshape(D_MODEL, D_MODEL)
    w_grp = g_pool[1].reshape(N_SHARDS, 4, grp_rows, POOL_GROUP_DIM).transpose(1, 0, 2, 3).reshape(
        4, POOL_GROUP_DIM, POOL_GROUP_DIM)
    w_out = g_pool[2].reshape(D_MODEL, D_MODEL)
    (h0, p, zr, z, x1), ffn0 = pool_fwd(x0, row(norm_mix[0]), w_in, w_grp, pool_scale, w_out,
                                        duty=GatherWeights(ffn0_shards))
    (h1, gate0, up0, x2), late = ffn_fwd(x1, row(norm_ffn[0]), *ffn0, "ffn_fwd0", duty=GatherWeights(late_shards))
    w_qkv = pad_qkv_weight(late[0])
    w_ao = jnp.concatenate(pad_groups(late[1].reshape(D_MODEL, D_MODEL), 0), axis=0)
    ffn1 = late[2:5]
    h2, *qkv_parts = qkv_fwd(x2, row(norm_mix[1]), w_qkv, cos, sin)
    o_parts, lse_parts = [], []
    for gi in range(3):
        o_g, lse_g = attn_fwd(qkv_parts[gi], gi, f"attn_fwd_g{gi}")
        o_parts.append(o_g)
        lse_parts.append(lse_g)
    x3, merged, o_nat, lse_nat = attn_out_fwd(x2, o_parts, lse_parts, w_ao)
    h3, gate1, up1, x4 = ffn_fwd(x3, row(norm_ffn[1]), *ffn1, "ffn_fwd1")
    dx4, d_norm_final, loss_local = final_fwd_bwd(x4, row(norm_final), target)

    act1, dgate1, dup1, dx3, d_nf1 = ffn_bwd(dx4, x3, row(norm_ffn[1]), gate1, up1, *ffn1, "ffn_bwd1")
    g_gate1 = wgrad_row_sharded("wgrad_gate1", dgate1, h3)
    g_up1 = wgrad_row_sharded("wgrad_up1", dup1, h3)
    g_down1 = wgrad_row_sharded("wgrad_down1", act1, dx4)
    do_parts, c_parts = attn_out_bwd(dx3, w_ao, o_nat, lse_nat)
    g_ao = unpad_groups(jnp.split(wgrad_full("wgrad_attn_out", merged, dx3), 3, axis=0), 0)

    set1, names1 = [g_gate1, g_up1, g_down1, chip_rows(g_ao)], ["gate1", "up1", "down1", "attn_out"]
    attn_b = lambda gi, duty: attn_bwd(qkv_parts[gi], do_parts[gi], lse_parts[gi], c_parts[gi], gi,
                                       f"attn_bwd_g{gi}", duty=duty)
    dqkv0, theirs1 = attn_b(0, SiblingHalves(set1))
    partials1 = add_halves(set1, theirs1, names1)
    dqkv1, received1 = attn_b(1, ChipExchange(partials1))
    dqkv2, full1 = attn_b(2, SiblingShare(sum_chips(received1, partials1, names1)))
    dqkv, dx2, d_nm1 = qkv_bwd([dqkv0, dqkv1, dqkv2], w_qkv, dx3, x2, row(norm_mix[1]), cos, sin)
    g_qkv = unpad_qkv_grad(wgrad_col_sharded("wgrad_qkv", h2, dqkv))

    set2, names2 = [g_qkv], ["qkv"]
    (act0, dgate0, dup0, dx1, d_nf0), theirs2 = ffn_bwd(dx2, x1, row(norm_ffn[0]), gate0, up0, *ffn0, "ffn_bwd0",
                                                        duty=SiblingHalves(set2))
    partials2 = add_halves(set2, theirs2, names2)
    g_gate0, received2 = wgrad_row_sharded("wgrad_gate0", dgate0, h1, duty=ChipExchange(partials2))
    g_up0, full2 = wgrad_row_sharded("wgrad_up0", dup0, h1,
                                     duty=SiblingShare(sum_chips(received2, partials2, names2)))
    g_down0 = wgrad_row_sharded("wgrad_down0", act0, dx2)

    set3, names3 = [g_gate0, g_up0, g_down0], ["gate0", "up0", "down0"]
    (dzs, du, dx0, d_nm0, d_scale), theirs3 = pool_bwd(dx1, x0, row(norm_mix[0]), w_in, w_grp, pool_scale, w_out, zr,
                                                       duty=SiblingHalves(set3))
    partials3 = add_halves(set3, theirs3, names3)
    g_out, received3 = wgrad_full("wgrad_pool_out", z, dx1, duty=ChipExchange(partials3))
    g_grp, full3 = wgrad_pool_group("wgrad_pool_group", p, dzs,
                                    duty=SiblingShare(sum_chips(received3, partials3, names3)))
    g_in = wgrad_full("wgrad_pool_in", h0, du)

    set4 = [chip_rows(g_in), g_grp.reshape(N_SHARDS, 4 * grp_rows, POOL_GROUP_DIM), chip_rows(g_out)]
    names4 = ["pool_in", "pool_group", "pool_out"]
    partials4 = add_halves(set4, run_duty(SiblingHalves(set4), "rs_halves_pool"), names4)
    received4 = run_duty(ChipExchange(partials4), "rs_exchange_pool")
    full4 = run_duty(SiblingShare(sum_chips(received4, partials4, names4)), "rs_share_pool")
    full = [full4[0], full4[1], full4[2], full2[0], full1[3],
            full3[0], full1[0], full3[1], full1[1], full3[2], full1[2]]

    zero_row = jnp.zeros((1, D_MODEL), F32)
    small = jnp.concatenate([d_nm0, d_nm1, d_nf0, d_nf1, d_norm_final, d_scale,
                             jnp.broadcast_to(loss_local, (1, D_MODEL)), zero_row], axis=0)
    small = allreduce_small(small)
    loss = small[6, 0]

    pack = lambda a, b, c, d: jnp.concatenate([a, b, row(c), d, zero_row, zero_row], axis=0)[None]
    sg, sd, sm, sv = adamw("adamw_small", [small],
                           pack(norm_mix, norm_ffn, norm_final, pool_scale),
                           pack(m_norm_mix, m_norm_ffn, m_norm_final, m_pool_scale),
                           pack(v_norm_mix, v_norm_ffn, v_norm_final, v_pool_scale))
    unpack = lambda a: (a[0, 0:2], a[0, 2:4], a[0, 4], a[0, 5:6])

    def update(name, grads, w, m, v, transposed=False):
        if transposed:
            w, m, v = (jnp.swapaxes(a, 1, 2) for a in (w, m, v))
        n_layers = len(grads)
        shp = (n_layers,) + grads[0].shape
        outs = [o.reshape(w.shape) for o in adamw(name, grads, w.reshape(shp), m.reshape(shp), v.reshape(shp))]
        return [jnp.swapaxes(o, 1, 2) for o in outs] if transposed else outs

    big = [
        update("adamw_pool_in", [full[0]], pool_w_in, m_pool_w_in, v_pool_w_in),
        update("adamw_pool_group", [full[1]], pool_w_group, m_pool_w_group, v_pool_w_group),
        update("adamw_pool_out", [full[2]], pool_w_out, m_pool_w_out, v_pool_w_out),
        update("adamw_qkv", [full[3]], attn_w_qkv, m_attn_w_qkv, v_attn_w_qkv),
        update("adamw_attn_out", [full[4]], attn_w_out, m_attn_w_out, v_attn_w_out),
        update("adamw_gate", [full[5], full[6]], ffn_w_gate, m_ffn_w_gate, v_ffn_w_gate, transposed=True),
        update("adamw_up", [full[7], full[8]], ffn_w_up, m_ffn_w_up, v_ffn_w_up, transposed=True),
        update("adamw_down", [full[9], full[10]], ffn_w_down, m_ffn_w_down, v_ffn_w_down),
    ]

    def leaves(k, small_vals):
        nm, nf, nfin, psc = unpack(small_vals)
        return [nm, nf, nfin, big[0][k], big[1][k], psc, big[2][k], big[3][k], big[4][k],
                big[5][k], big[6][k], big[7][k]]

    grad_x = dx0.reshape(x.shape)
    return (loss, grad_x, *leaves(0, sg), *leaves(1, sd), *leaves(2, sm), *leaves(3, sv))
```

```python
import math

import jax
import jax.numpy as jnp
from jax import lax
from jax.experimental import pallas as pl
from jax.experimental.pallas import tpu as pltpu

F32 = jnp.float32
BF16 = jnp.bfloat16

D_MODEL = 1024
N_SHARDS = 4
N_DEV = 8
D_FF = 2816
FF_SHARD = D_FF // N_SHARDS
HEAD_DIM = 64
QKV_SHARD = 3 * D_MODEL // N_SHARDS
POOL_WINDOWS = (2, 4, 8, 16)
POOL_GROUP_DIM = 256
POOL_HALO = 16
ATTN_W = 128
GROUP_LANES = (0, 384, 704, 1024)
GROUP_HEADS = (6, 5, 5)
GROUP_DIL = (1, 4, 16)
ROPE_THETA = 10000.0
EPS = 1e-6
NEG_INF = -1e30
LANE = 128
VMEM_LIMIT_BYTES = 60 * 1024 * 1024

ADAM_LR = 0.001
ADAM_B1 = 0.9
ADAM_B2 = 0.999
ADAM_EPS = 1e-08
ADAM_WD = 0.01
ADAM_STEP = 10

NT_DIMS = (((1,), (1,)), ((), ()))
TN_DIMS = (((0,), (0,)), ((), ()))
MESH = pl.DeviceIdType.MESH


_ANY = pl.BlockSpec(memory_space=pl.ANY)


def _pcall(body, *, name, out_shape, grid=None, in_specs=None, out_specs=None, scratch_shapes=(),
           semantics=None, duty=None):
    kw = {}
    if in_specs is not None and duty is None:
        kw["in_specs"] = in_specs
    if out_specs is not None and duty is None:
        kw["out_specs"] = out_specs
    if grid is not None:
        kw["grid"] = grid
    params = dict(dimension_semantics=semantics, vmem_limit_bytes=VMEM_LIMIT_BYTES)
    if duty is None:
        return pl.pallas_call(body, name=name, out_shape=out_shape, scratch_shapes=list(scratch_shapes),
                              compiler_params=pltpu.CompilerParams(**params), **kw)

    single = not isinstance(out_shape, (list, tuple))
    c_out_shape = [out_shape] if single else list(out_shape)
    c_out_specs = [out_specs] if single else list(out_specs)
    n_in, n_out, n_scr = len(in_specs), len(c_out_shape), len(scratch_shapes)
    d_in, d_out = len(duty.ins), len(duty.out_shape)
    total = math.prod(grid)
    mid_step = (5 * total) // 6

    def wrapped(*refs):
        c_in, d_ins = refs[:n_in], refs[n_in:n_in + d_in]
        o0 = n_in + d_in
        c_outs, d_outs = refs[o0:o0 + n_out], refs[o0 + n_out:o0 + n_out + d_out]
        s0 = o0 + n_out + d_out
        c_scr, d_sems = refs[s0:s0 + n_scr], refs[s0 + n_scr:]
        step = pl.program_id(0)
        for ax in range(1, len(grid)):
            step = step * grid[ax] + pl.program_id(ax)

        @pl.when(step == 0)
        def _():
            duty.start(d_ins, d_outs, d_sems)

        body(*c_in, *c_outs, *c_scr)

        @pl.when(step == mid_step)
        def _():
            duty.mid(d_ins, d_outs, d_sems)

        @pl.when(step == total - 1)
        def _():
            duty.finish(d_ins, d_outs, d_sems)

    call = pl.pallas_call(
        wrapped, name=name, grid=grid,
        in_specs=list(in_specs) + [_ANY] * d_in, out_specs=c_out_specs + [_ANY] * d_out,
        out_shape=c_out_shape + list(duty.out_shape),
        scratch_shapes=list(scratch_shapes) + list(duty.scratch),
        input_output_aliases={n_in + i: n_out + o for i, o in duty.aliases.items()},
        compiler_params=pltpu.CompilerParams(has_side_effects=True, **params))

    def run(*args):
        outs = call(*args, *duty.ins)
        c = outs[:n_out]
        return (c[0] if single else list(c)), list(outs[n_out:])

    return run


def _sds(shape, dtype):
    return jax.ShapeDtypeStruct(tuple(shape), dtype)


def _dot(a, b):
    return jnp.dot(a, b, preferred_element_type=F32)


def _dot_nt(a, b):
    return lax.dot_general(a, b, NT_DIMS, preferred_element_type=F32)


def _dot_tn(a, b):
    return lax.dot_general(a, b, TN_DIMS, preferred_element_type=F32)


def _rms_fwd(x, g):
    r = lax.rsqrt(jnp.mean(x * x, axis=-1, keepdims=True) + EPS)
    return x * r * g


def _rms_bwd(dh, x, g):
    r = lax.rsqrt(jnp.mean(x * x, axis=-1, keepdims=True) + EPS)
    xh = x * r
    dg = jnp.sum(dh * xh, axis=0, keepdims=True)
    dxh = dh * g
    dx = r * (dxh - xh * jnp.mean(dxh * xh, axis=-1, keepdims=True))
    return dx, dg


def _sigmoid(x):
    return 1.0 / (1.0 + jnp.exp(-x))


def _tile_rows(t):
    return min(512, t)


def _sub_tiles(tm, n_sub=2):
    rows = tm // n_sub
    return [pl.ds(i * rows, rows) for i in range(n_sub)]


def _wgrad_rows(t):
    return min(2048, t)


def pool_fwd(x, g_row, w_in, w_grp, scale, w_out, duty=None):
    t_len = x.shape[0]
    tm = _tile_rows(t_len)

    def body(x_ref, g_ref, win_ref, wgrp_ref, scale_ref, wout_ref,
             h_ref, p_ref, zr_ref, z_ref, xo_ref, ubuf):
        t = pl.program_id(0)

        @pl.when(t == 0)
        def _():
            ubuf[pl.ds(0, POOL_HALO), :] = jnp.zeros((POOL_HALO, D_MODEL), F32)

        x_t = x_ref[...]
        h = _rms_fwd(x_t, g_ref[...]).astype(BF16)
        h_ref[...] = h
        ubuf[pl.ds(POOL_HALO, tm), :] = _dot(h, win_ref[...])
        row = t * tm + lax.broadcasted_iota(jnp.int32, (tm, 1), 0)
        for gi, w in enumerate(POOL_WINDOWS):
            cols = pl.ds(gi * POOL_GROUP_DIM, POOL_GROUP_DIM)
            u_g = ubuf[pl.ds(POOL_HALO, tm), cols]
            acc = u_g
            for j in range(1, w):
                acc = acc + ubuf[pl.ds(POOL_HALO - j, tm), cols]
            cnt = jnp.minimum(row + 1, w).astype(F32)
            p_g = (acc / cnt - u_g).astype(BF16)
            p_ref[:, cols] = p_g
            z_g = _dot(p_g, wgrp_ref[gi])
            zr_ref[:, cols] = z_g.astype(BF16)
            z_ref[:, cols] = (z_g * scale_ref[:, cols]).astype(BF16)
        ubuf[pl.ds(0, POOL_HALO), :] = ubuf[pl.ds(tm, POOL_HALO), :]
        xo_ref[...] = x_t + _dot(z_ref[...], wout_ref[...])

    row_spec = pl.BlockSpec((tm, D_MODEL), lambda t: (t, 0))
    full2 = lambda shape: pl.BlockSpec(shape, lambda t: (0,) * len(shape))
    return _pcall(
        body, name="pool_fwd", grid=(t_len // tm,),
        in_specs=[row_spec, full2((1, D_MODEL)), full2((D_MODEL, D_MODEL)),
                  full2((4, POOL_GROUP_DIM, POOL_GROUP_DIM)), full2((1, D_MODEL)), full2((D_MODEL, D_MODEL))],
        out_specs=[row_spec] * 5,
        out_shape=[_sds((t_len, D_MODEL), BF16)] * 4 + [_sds((t_len, D_MODEL), F32)],
        scratch_shapes=[pltpu.VMEM((tm + POOL_HALO, D_MODEL), F32)],
        semantics=("arbitrary",), duty=duty,
    )(x, g_row, w_in, w_grp, scale, w_out)


def ffn_fwd(x, g_row, w_gate_t, w_up_t, w_down, name, duty=None):
    t_len = x.shape[0]
    tm = min(1024, t_len)

    def body(x_ref, g_ref, wg_ref, wu_ref, wd_ref, h_ref, go_ref, uo_ref, xo_ref, hbuf, acc):
        s = pl.program_id(1)

        @pl.when(s == 0)
        def _():
            h = _rms_fwd(x_ref[...], g_ref[...]).astype(BF16)
            hbuf[...] = h
            h_ref[...] = h
            acc[...] = jnp.zeros_like(acc)

        for rows in _sub_tiles(tm, 1):
            h = hbuf[rows, :]
            gate = _dot_nt(h, wg_ref[...])
            up = _dot_nt(h, wu_ref[...])
            go_ref[rows, :] = gate.astype(BF16)
            uo_ref[rows, :] = up.astype(BF16)
            act = (gate * _sigmoid(gate) * up).astype(BF16)
            acc[rows, :] += _dot(act, wd_ref[...])

        @pl.when(s == N_SHARDS - 1)
        def _():
            xo_ref[...] = x_ref[...] + acc[...]

    row_spec = pl.BlockSpec((tm, D_MODEL), lambda t, s: (t, 0))
    row_w = pl.BlockSpec((None, FF_SHARD, D_MODEL), lambda t, s: (s, 0, 0))
    act_spec = pl.BlockSpec((None, tm, FF_SHARD), lambda t, s: (s, t, 0))
    return _pcall(
        body, name=name, grid=(t_len // tm, N_SHARDS),
        in_specs=[row_spec, pl.BlockSpec((1, D_MODEL), lambda t, s: (0, 0)), row_w, row_w, row_w],
        out_specs=[row_spec, act_spec, act_spec, row_spec],
        out_shape=[_sds((t_len, D_MODEL), BF16), _sds((N_SHARDS, t_len, FF_SHARD), BF16),
                   _sds((N_SHARDS, t_len, FF_SHARD), BF16), _sds((t_len, D_MODEL), F32)],
        scratch_shapes=[pltpu.VMEM((tm, D_MODEL), BF16), pltpu.VMEM((tm, D_MODEL), F32)],
        semantics=("arbitrary", "arbitrary"), duty=duty,
    )(x, g_row, w_gate_t, w_up_t, w_down)


def ffn_bwd(dxo, x, g_row, gate, up, w_gate_t, w_up_t, w_down, name, duty=None):
    t_len = x.shape[0]
    tm = _tile_rows(t_len)

    def body(dxo_ref, x_ref, g_ref, gate_ref, up_ref, wg_ref, wu_ref, wd_ref,
             act_ref, dg_ref, du_ref, dx_ref, dn_ref, dxb, dh):
        t = pl.program_id(0)
        s = pl.program_id(1)

        @pl.when(s == 0)
        def _():
            dxb[...] = dxo_ref[...].astype(BF16)
            dh[...] = jnp.zeros_like(dh)

        @pl.when(jnp.logical_and(s == 0, t == 0))
        def _():
            dn_ref[...] = jnp.zeros_like(dn_ref)

        for rows in _sub_tiles(tm):
            dact = _dot_nt(dxb[rows, :], wd_ref[...])
            gv = gate_ref[rows, :].astype(F32)
            uv = up_ref[rows, :].astype(F32)
            sg = _sigmoid(gv)
            silu = gv * sg
            act_ref[rows, :] = (silu * uv).astype(BF16)
            dgv = (dact * uv * (sg * (1.0 + gv * (1.0 - sg)))).astype(BF16)
            duv = (dact * silu).astype(BF16)
            dg_ref[rows, :] = dgv
            du_ref[rows, :] = duv
            dh[rows, :] += _dot(dgv, wg_ref[...]) + _dot(duv, wu_ref[...])

        @pl.when(s == N_SHARDS - 1)
        def _():
            dx, dn = _rms_bwd(dh[...], x_ref[...], g_ref[...])
            dx_ref[...] = dxo_ref[...] + dx
            dn_ref[...] += dn

    row_spec = pl.BlockSpec((tm, D_MODEL), lambda t, s: (t, 0))
    vec_spec = pl.BlockSpec((1, D_MODEL), lambda t, s: (0, 0))
    row_w = pl.BlockSpec((None, FF_SHARD, D_MODEL), lambda t, s: (s, 0, 0))
    act_spec = pl.BlockSpec((None, tm, FF_SHARD), lambda t, s: (s, t, 0))
    act_shape = _sds((N_SHARDS, t_len, FF_SHARD), BF16)
    return _pcall(
        body, name=name, grid=(t_len // tm, N_SHARDS),
        in_specs=[row_spec, row_spec, vec_spec, act_spec, act_spec, row_w, row_w, row_w],
        out_specs=[act_spec, act_spec, act_spec, row_spec, vec_spec],
        out_shape=[act_shape, act_shape, act_shape, _sds((t_len, D_MODEL), F32), _sds((1, D_MODEL), F32)],
        scratch_shapes=[pltpu.VMEM((tm, D_MODEL), BF16), pltpu.VMEM((tm, D_MODEL), F32)],
        semantics=("arbitrary", "arbitrary"), duty=duty,
    )(dxo, x, g_row, gate, up, w_gate_t, w_up_t, w_down)


def tn_matmul(name, a, b, a_spec, b_spec, out_shape, out_spec, grid, duty=None):
    def body(a_ref, b_ref, o_ref):
        @pl.when(pl.program_id(len(grid) - 1) == 0)
        def _():
            o_ref[...] = jnp.zeros_like(o_ref)

        res = _dot_tn(a_ref[...].astype(BF16), b_ref[...].astype(BF16))
        o_ref[...] += res.reshape(o_ref.shape)

    return _pcall(body, name=name, grid=grid, in_specs=[a_spec, b_spec], out_specs=out_spec,
                  out_shape=out_shape, semantics=("arbitrary",) * len(grid), duty=duty)(a, b)


def wgrad_full(name, a, b, duty=None):
    t_len, k = a.shape
    n = b.shape[1]
    tt = _wgrad_rows(t_len)
    return tn_matmul(name, a, b,
                     pl.BlockSpec((tt, k), lambda t: (t, 0)), pl.BlockSpec((tt, n), lambda t: (t, 0)),
                     _sds((k, n), F32), pl.BlockSpec((k, n), lambda t: (0, 0)), (t_len // tt,), duty)


def wgrad_col_sharded(name, a, b_sh, duty=None):
    t_len, k = a.shape
    n_sh, _, n = b_sh.shape
    tt = _wgrad_rows(t_len)
    return tn_matmul(name, a, b_sh,
                     pl.BlockSpec((tt, k), lambda s, t: (t, 0)), pl.BlockSpec((None, tt, n), lambda s, t: (s, t, 0)),
                     _sds((n_sh, k, n), F32), pl.BlockSpec((None, k, n), lambda s, t: (s, 0, 0)),
                     (n_sh, t_len // tt), duty)


def wgrad_row_sharded(name, a_sh, b, duty=None):
    t_len, n = b.shape
    n_sh, _, k = a_sh.shape
    tt = _wgrad_rows(t_len)
    return tn_matmul(name, a_sh, b,
                     pl.BlockSpec((None, tt, k), lambda s, t: (s, t, 0)), pl.BlockSpec((tt, n), lambda s, t: (t, 0)),
                     _sds((n_sh, k, n), F32), pl.BlockSpec((None, k, n), lambda s, t: (s, 0, 0)),
                     (n_sh, t_len // tt), duty)


def wgrad_pool_group(name, p, dzs, duty=None):
    t_len = p.shape[0]
    tt = _wgrad_rows(t_len)
    gd = POOL_GROUP_DIM
    rows = gd // N_SHARDS
    return tn_matmul(name, p, dzs,
                     pl.BlockSpec((tt, gd), lambda g, t: (t, g)), pl.BlockSpec((tt, gd), lambda g, t: (t, g)),
                     _sds((N_SHARDS, 4, rows, gd), F32),
                     pl.BlockSpec((N_SHARDS, None, rows, gd), lambda g, t: (0, g, 0, 0)),
                     (4, t_len // tt), duty)


PAD_LANES = 384
QKV_PAD = 3 * PAD_LANES
N_SLABS = QKV_PAD // LANE
GROUP_REAL = tuple(GROUP_LANES[g + 1] - GROUP_LANES[g] for g in range(3))
Q_BLOCK = 512


def pad_groups(w, axis):
    parts = []
    for g in range(3):
        blk = lax.slice_in_dim(w, GROUP_LANES[g], GROUP_LANES[g + 1], axis=axis)
        pad = [(0, 0)] * w.ndim
        pad[axis] = (0, PAD_LANES - GROUP_REAL[g])
        parts.append(jnp.pad(blk, pad))
    return parts


def unpad_groups(parts, axis):
    return jnp.concatenate([lax.slice_in_dim(p, 0, GROUP_REAL[g], axis=axis) for g, p in enumerate(parts)],
                           axis=axis)


def pad_qkv_weight(w_qkv_sh):
    w = jnp.transpose(w_qkv_sh, (1, 0, 2)).reshape(D_MODEL, 3 * D_MODEL)
    q, k, v = (pad_groups(w[:, i * D_MODEL:(i + 1) * D_MODEL], 1) for i in range(3))
    return jnp.stack([jnp.concatenate([q[g], k[g], v[g]], axis=1) for g in range(3)])


def unpad_qkv_grad(g_pad):
    cols = [unpad_groups([g_pad[g][:, i * PAD_LANES:(i + 1) * PAD_LANES] for g in range(3)], 1) for i in range(3)]
    w = jnp.concatenate(cols, axis=1)
    return jnp.transpose(w.reshape(D_MODEL, N_SHARDS, QKV_SHARD), (1, 0, 2))


def rope_tables(t_len):
    inv_freq = 1.0 / (ROPE_THETA ** (jnp.arange(0, HEAD_DIM, 2, dtype=F32) / HEAD_DIM))
    ang = jnp.arange(t_len, dtype=F32)[:, None] * inv_freq[None, :]
    ang = jnp.concatenate([ang] * (2 * LANE // HEAD_DIM), axis=-1)
    return jnp.cos(ang), jnp.sin(ang)


def _rot_half(v):
    n = v.shape[1]
    lane = lax.broadcasted_iota(jnp.int32, v.shape, 1)
    return jnp.where(lane % HEAD_DIM < HEAD_DIM // 2,
                     -pltpu.roll(v, n - HEAD_DIM // 2, 1), pltpu.roll(v, HEAD_DIM // 2, 1))


def _lane_cols(j):
    return slice(j * LANE, (j + 1) * LANE)


def _to_residue_major(slab, j_src, dst_ref, j_dst, dil, rows):
    for r in range(dil):
        dst_ref[r, :, _lane_cols(j_dst)] = slab[j_src, pl.ds(r, rows // dil, stride=dil), :].astype(dst_ref.dtype)


def _to_natural(src_ref, j_src, slab, j_dst, dil, rows):
    for r in range(dil):
        slab[j_dst, pl.ds(r, rows // dil, stride=dil), :] = src_ref[r, :, _lane_cols(j_src)].astype(F32)


def qkv_fwd(x, g_row, w_pad, cos, sin):
    t_len = x.shape[0]
    tm = _tile_rows(t_len)

    def body(x_ref, g_ref, w_ref, cos_ref, sin_ref, h_ref, o1_ref, o4_ref, o16_ref, hbuf, slab):
        g = pl.program_id(1)

        @pl.when(g == 0)
        def _():
            h = _rms_fwd(x_ref[...], g_ref[...]).astype(BF16)
            hbuf[...] = h
            h_ref[...] = h

        acc = _dot(hbuf[...], w_ref[...])
        cos_t = cos_ref[...]
        sin_t = sin_ref[...]
        for j in range(N_SLABS):
            a = acc[:, _lane_cols(j)]
            if j < 6:
                a = a * cos_t + _rot_half(a) * sin_t
            if j < 3:
                a = a * (HEAD_DIM ** -0.5)
            slab[j] = a
        for gi, (dil, o_ref) in enumerate(zip(GROUP_DIL, (o1_ref, o4_ref, o16_ref))):
            @pl.when(g == gi)
            def _(dil=dil, o_ref=o_ref):
                for j in range(N_SLABS):
                    _to_residue_major(slab, j, o_ref, j, dil, tm)

    row_spec = pl.BlockSpec((tm, D_MODEL), lambda t, g: (t, 0))
    tab_spec = pl.BlockSpec((tm, LANE), lambda t, g: (t, 0))
    out_specs = [row_spec] + [pl.BlockSpec((d, tm // d, QKV_PAD), lambda t, g: (0, t, 0)) for d in GROUP_DIL]
    out_shape = [_sds((t_len, D_MODEL), BF16)] + [_sds((d, t_len // d, QKV_PAD), BF16) for d in GROUP_DIL]
    return _pcall(
        body, name="qkv_fwd", grid=(t_len // tm, 3),
        in_specs=[row_spec, pl.BlockSpec((1, D_MODEL), lambda t, g: (0, 0)),
                  pl.BlockSpec((None, D_MODEL, QKV_PAD), lambda t, g: (g, 0, 0)), tab_spec, tab_spec],
        out_specs=out_specs, out_shape=out_shape,
        scratch_shapes=[pltpu.VMEM((tm, D_MODEL), BF16), pltpu.VMEM((N_SLABS, tm, LANE), F32)],
        semantics=("arbitrary", "arbitrary"),
    )(x, g_row, w_pad, cos, sin)


def _band_mask(n):
    qi = lax.broadcasted_iota(jnp.int32, (ATTN_W, 2 * ATTN_W), 0)
    kj = lax.broadcasted_iota(jnp.int32, (ATTN_W, 2 * ATTN_W), 1)
    dist = ATTN_W + qi - kj
    return (dist >= 0) & (dist <= ATTN_W) & ((kj >= ATTN_W) | (n > 0))


def _half_masks():
    lane = lax.broadcasted_iota(jnp.int32, (1, LANE), 1)
    return [lane < HEAD_DIM, lane >= HEAD_DIM]


def _live_halves(gi, j):
    hms = _half_masks()
    return hms if (gi == 0 or j < 2) else hms[:1]


def attn_fwd(qkv_g, gi, name):
    dil, l_len, _ = qkv_g.shape
    qb = min(Q_BLOCK, l_len)
    nsub = qb // ATTN_W

    def body(q_ref, kc_ref, kp_ref, vc_ref, vp_ref, o_ref, lse_ref, kbuf, vbuf):
        n = pl.program_id(1)
        kbuf[pl.ds(0, ATTN_W), :] = kp_ref[...]
        kbuf[pl.ds(ATTN_W, qb), :] = kc_ref[...]
        vbuf[pl.ds(0, ATTN_W), :] = vp_ref[...]
        vbuf[pl.ds(ATTN_W, qb), :] = vc_ref[...]

        def sub(b, carry):
            r0 = pl.multiple_of(b * ATTN_W, ATTN_W)
            mask = _band_mask(n + b)
            for j in range(3):
                cols = _lane_cols(j)
                q = q_ref[pl.ds(r0, ATTN_W), cols]
                k = kbuf[pl.ds(r0, 2 * ATTN_W), cols]
                v = vbuf[pl.ds(r0, 2 * ATTN_W), cols]
                o = jnp.zeros((ATTN_W, LANE), F32)
                lse = jnp.zeros((ATTN_W, LANE), F32)
                for hm in _live_halves(gi, j):
                    qh = jnp.where(hm, q, jnp.zeros_like(q))
                    s = jnp.where(mask, _dot_nt(qh, k), NEG_INF)
                    m = jnp.max(s, axis=-1, keepdims=True)
                    e = jnp.exp(s - m)
                    den = jnp.sum(e, axis=-1, keepdims=True)
                    p = (e / den).astype(BF16)
                    o = jnp.where(hm, _dot(p, v), o)
                    lse = jnp.where(hm, m + jnp.log(den), lse)
                o_ref[pl.ds(r0, ATTN_W), cols] = o.astype(BF16)
                lse_ref[pl.ds(r0, ATTN_W), cols] = lse
            return carry

        lax.fori_loop(0, nsub, sub, 0)

    cur = lambda c: pl.BlockSpec((None, qb, PAD_LANES), lambda r, n: (r, n, c))
    prev = lambda c: pl.BlockSpec((None, ATTN_W, PAD_LANES), lambda r, n: (r, jnp.maximum(n * nsub - 1, 0), c))
    out_spec = pl.BlockSpec((None, qb, PAD_LANES), lambda r, n: (r, n, 0))
    return _pcall(
        body, name=name, grid=(dil, l_len // qb),
        in_specs=[cur(0), cur(1), prev(1), cur(2), prev(2)],
        out_specs=[out_spec, out_spec],
        out_shape=[_sds((dil, l_len, PAD_LANES), BF16), _sds((dil, l_len, PAD_LANES), F32)],
        scratch_shapes=[pltpu.VMEM((qb + ATTN_W, PAD_LANES), BF16), pltpu.VMEM((qb + ATTN_W, PAD_LANES), BF16)],
        semantics=("arbitrary", "arbitrary"),
    )(qkv_g, qkv_g, qkv_g, qkv_g, qkv_g)


def _group_stats(lse):
    lane = lax.broadcasted_iota(jnp.int32, (1, QKV_PAD), 1)
    gmask = [(lane >= g * PAD_LANES) & (lane < g * PAD_LANES + GROUP_REAL[g]) for g in range(3)]
    lses, glse = [], []
    for g in range(3):
        mx = jnp.max(jnp.where(gmask[g], lse, -jnp.inf), axis=-1, keepdims=True)
        sm = jnp.sum(jnp.where(gmask[g], jnp.exp(lse - mx), 0.0), axis=-1, keepdims=True) / HEAD_DIM
        full = mx + jnp.log(sm)
        lses.append(full)
        glse.append(full - math.log(GROUP_HEADS[g]))
    top = jnp.maximum(jnp.maximum(glse[0], glse[1]), glse[2])
    ex = [jnp.exp(v - top) for v in glse]
    tot = ex[0] + ex[1] + ex[2]
    alpha = [v / tot for v in ex]
    pick = lambda vals: jnp.where(lane < PAD_LANES, vals[0], jnp.where(lane < 2 * PAD_LANES, vals[1], vals[2]))
    return gmask, alpha, pick([3.0 * a for a in alpha]), lse - pick(lses), pick


def attn_out_fwd(x, o_parts, lse_parts, w_out_pad):
    t_len = x.shape[0]
    tm = _tile_rows(t_len)

    def body(x_ref, o1, o4, o16, l1, l4, l16, w_ref, xo_ref, mg_ref, o_ref, lse_ref, o_slab, l_slab):
        for gi, (dil, og, lg) in enumerate(zip(GROUP_DIL, (o1, o4, o16), (l1, l4, l16))):
            for j in range(3):
                _to_natural(og, j, o_slab, 3 * gi + j, dil, tm)
                _to_natural(lg, j, l_slab, 3 * gi + j, dil, tm)
        o = jnp.concatenate([o_slab[j] for j in range(N_SLABS)], axis=1)
        lse = jnp.concatenate([l_slab[j] for j in range(N_SLABS)], axis=1)
        o_ref[...] = o.astype(BF16)
        lse_ref[...] = lse
        _, _, scale, _, _ = _group_stats(lse)
        merged = (o * scale).astype(BF16)
        mg_ref[...] = merged
        xo_ref[...] = x_ref[...] + _dot(merged, w_ref[...])

    row_spec = pl.BlockSpec((tm, D_MODEL), lambda t: (t, 0))
    pad_spec = pl.BlockSpec((tm, QKV_PAD), lambda t: (t, 0))
    part_specs = [pl.BlockSpec((d, tm // d, PAD_LANES), lambda t: (0, t, 0)) for d in GROUP_DIL]
    return _pcall(
        body, name="attn_out_fwd", grid=(t_len // tm,),
        in_specs=[row_spec] + part_specs + part_specs + [pl.BlockSpec((QKV_PAD, D_MODEL), lambda t: (0, 0))],
        out_specs=[row_spec, pad_spec, pad_spec, pad_spec],
        out_shape=[_sds((t_len, D_MODEL), F32), _sds((t_len, QKV_PAD), BF16),
                   _sds((t_len, QKV_PAD), BF16), _sds((t_len, QKV_PAD), F32)],
        scratch_shapes=[pltpu.VMEM((N_SLABS, tm, LANE), F32), pltpu.VMEM((N_SLABS, tm, LANE), F32)],
        semantics=("arbitrary",),
    )(x, *o_parts, *lse_parts, w_out_pad)


def attn_out_bwd(dxo, w_out_pad, o, lse):
    t_len = dxo.shape[0]
    tm = _tile_rows(t_len)

    def body(dx_ref, w_ref, o_ref, lse_ref, d1, d4, d16, c1, c4, c16, slab):
        dmerged = _dot_nt(dx_ref[...].astype(BF16), w_ref[...])
        o_t = o_ref[...].astype(F32)
        gmask, alpha, scale, lse_rel, pick = _group_stats(lse_ref[...])
        e = dmerged * o_t
        dalpha = [3.0 * jnp.sum(jnp.where(gmask[g], e, 0.0), axis=-1, keepdims=True) for g in range(3)]
        mean_da = alpha[0] * dalpha[0] + alpha[1] * dalpha[1] + alpha[2] * dalpha[2]
        dglse = [alpha[g] * (dalpha[g] - mean_da) for g in range(3)]
        dlse = pick(dglse) * jnp.exp(lse_rel)
        do = dmerged * scale
        es = e * scale
        lane = lax.broadcasted_iota(jnp.int32, (1, LANE), 1)
        first = lane < HEAD_DIM
        for j in range(N_SLABS):
            slab[j] = do[:, _lane_cols(j)]
        for gi, (dil, dg) in enumerate(zip(GROUP_DIL, (d1, d4, d16))):
            for j in range(3):
                _to_residue_major(slab, 3 * gi + j, dg, j, dil, tm)
        for j in range(N_SLABS):
            blk = es[:, _lane_cols(j)]
            s0 = jnp.sum(jnp.where(first, blk, 0.0), axis=-1, keepdims=True)
            s1 = jnp.sum(jnp.where(first, 0.0, blk), axis=-1, keepdims=True)
            slab[j] = jnp.where(first, s0, s1) - dlse[:, _lane_cols(j)]
        for gi, (dil, cg) in enumerate(zip(GROUP_DIL, (c1, c4, c16))):
            for j in range(3):
                _to_residue_major(slab, 3 * gi + j, cg, j, dil, tm)

    row_spec = pl.BlockSpec((tm, D_MODEL), lambda t: (t, 0))
    pad_spec = pl.BlockSpec((tm, QKV_PAD), lambda t: (t, 0))
    part_specs = [pl.BlockSpec((d, tm // d, PAD_LANES), lambda t: (0, t, 0)) for d in GROUP_DIL]
    shapes = lambda dt: [_sds((d, t_len // d, PAD_LANES), dt) for d in GROUP_DIL]
    outs = _pcall(
        body, name="attn_out_bwd", grid=(t_len // tm,),
        in_specs=[row_spec, pl.BlockSpec((QKV_PAD, D_MODEL), lambda t: (0, 0)), pad_spec, pad_spec],
        out_specs=part_specs + part_specs,
        out_shape=shapes(BF16) + shapes(F32),
        scratch_shapes=[pltpu.VMEM((N_SLABS, tm, LANE), F32)],
        semantics=("arbitrary",),
    )(dxo, w_out_pad, o, lse)
    return outs[:3], outs[3:]


def attn_bwd(qkv_g, do_g, lse_g, c_g, gi, name, duty=None):
    dil, l_len, _ = qkv_g.shape
    qb = min(Q_BLOCK, l_len)
    nsub = qb // ATTN_W
    nsb = l_len // qb

    def body(q_ref, kc_ref, kp_ref, vc_ref, vp_ref, do_ref, lse_ref, c_ref,
             qn_ref, don_ref, lsen_ref, cn_ref, o_ref, kbuf, vbuf, dkbuf, dvbuf):
        n = pl.program_id(1)
        kbuf[pl.ds(0, ATTN_W), :] = kp_ref[...]
        kbuf[pl.ds(ATTN_W, qb), :] = kc_ref[...]
        vbuf[pl.ds(0, ATTN_W), :] = vp_ref[...]
        vbuf[pl.ds(ATTN_W, qb), :] = vc_ref[...]
        dkbuf[...] = jnp.zeros_like(dkbuf)
        dvbuf[...] = jnp.zeros_like(dvbuf)

        def tile(q, do_t, lse_t, c_t, k, v, mask, hms):
            dq = jnp.zeros((ATTN_W, LANE), F32)
            dk = jnp.zeros((k.shape[0], LANE), F32)
            dv = jnp.zeros((k.shape[0], LANE), F32)
            for hm in hms:
                qh = jnp.where(hm, q, jnp.zeros_like(q))
                doh = jnp.where(hm, do_t, jnp.zeros_like(do_t))
                lse_h = jnp.max(jnp.where(hm, lse_t, -jnp.inf), axis=-1, keepdims=True)
                c_h = jnp.max(jnp.where(hm, c_t, -jnp.inf), axis=-1, keepdims=True)
                s = jnp.where(mask, _dot_nt(qh, k), NEG_INF)
                p = jnp.exp(s - lse_h)
                dp = _dot_nt(doh, v)
                ds = (p * (dp - c_h)).astype(BF16)
                dq = jnp.where(hm, _dot(ds, k), dq)
                dk = dk + _dot_tn(ds, qh)
                dv = dv + _dot_tn(p.astype(BF16), doh)
            return dq, dk, dv

        def sub(b, carry):
            r0 = pl.multiple_of(b * ATTN_W, ATTN_W)
            mask = _band_mask(n + b)
            rows = pl.ds(r0, ATTN_W)
            krows = pl.ds(r0, 2 * ATTN_W)
            for j in range(3):
                cols = _lane_cols(j)
                dq, dk, dv = tile(q_ref[rows, cols], do_ref[rows, cols], lse_ref[rows, cols], c_ref[rows, cols],
                                  kbuf[krows, cols], vbuf[krows, cols], mask, _live_halves(gi, j))
                o_ref[rows, cols] = dq.astype(BF16)
                dkbuf[krows, cols] += dk
                dvbuf[krows, cols] += dv
            return carry

        lax.fori_loop(0, nsub, sub, 0)

        qi = lax.broadcasted_iota(jnp.int32, (ATTN_W, ATTN_W), 0)
        kj = lax.broadcasted_iota(jnp.int32, (ATTN_W, ATTN_W), 1)
        nmask = (qi <= kj) & (n < nsb - 1)
        last = pl.ds(qb, ATTN_W)
        for j in range(3):
            cols = _lane_cols(j)
            _, dk, dv = tile(qn_ref[:, cols], don_ref[:, cols], lsen_ref[:, cols], cn_ref[:, cols],
                             kbuf[last, cols], vbuf[last, cols], nmask, _live_halves(gi, j))
            dkbuf[last, cols] += dk
            dvbuf[last, cols] += dv
        o_ref[:, pl.ds(PAD_LANES, PAD_LANES)] = dkbuf[pl.ds(ATTN_W, qb), :].astype(BF16)
        o_ref[:, pl.ds(2 * PAD_LANES, PAD_LANES)] = dvbuf[pl.ds(ATTN_W, qb), :].astype(BF16)

    cur = lambda c: pl.BlockSpec((None, qb, PAD_LANES), lambda r, n: (r, n, c))
    prev = lambda c: pl.BlockSpec((None, ATTN_W, PAD_LANES), lambda r, n: (r, jnp.maximum(n * nsub - 1, 0), c))
    nxt = pl.BlockSpec((None, ATTN_W, PAD_LANES), lambda r, n: (r, jnp.minimum((n + 1) * nsub, nsb * nsub - 1), 0))
    return _pcall(
        body, name=name, grid=(dil, nsb),
        in_specs=[cur(0), cur(1), prev(1), cur(2), prev(2), cur(0), cur(0), cur(0), nxt, nxt, nxt, nxt],
        out_specs=pl.BlockSpec((None, qb, QKV_PAD), lambda r, n: (r, n, 0)),
        out_shape=_sds((dil, l_len, QKV_PAD), BF16),
        scratch_shapes=[pltpu.VMEM((qb + ATTN_W, PAD_LANES), BF16), pltpu.VMEM((qb + ATTN_W, PAD_LANES), BF16),
                        pltpu.VMEM((qb + ATTN_W, PAD_LANES), F32), pltpu.VMEM((qb + ATTN_W, PAD_LANES), F32)],
        semantics=("arbitrary", "arbitrary"), duty=duty,
    )(qkv_g, qkv_g, qkv_g, qkv_g, qkv_g, do_g, lse_g, c_g, qkv_g, do_g, lse_g, c_g)


def qkv_bwd(dqkv_parts, w_pad, dxo, x, g_row, cos, sin):
    t_len = x.shape[0]
    tm = _tile_rows(t_len)

    def body(p1, p4, p16, w_ref, dxo_ref, x_ref, g_ref, cos_ref, sin_ref, dq_ref, dx_ref, dn_ref, dh, slab):
        t = pl.program_id(0)
        g = pl.program_id(1)

        @pl.when(g == 0)
        def _():
            dh[...] = jnp.zeros_like(dh)

        @pl.when(jnp.logical_and(g == 0, t == 0))
        def _():
            dn_ref[...] = jnp.zeros_like(dn_ref)

        for gi, (dil, part) in enumerate(zip(GROUP_DIL, (p1, p4, p16))):
            @pl.when(g == gi)
            def _(dil=dil, part=part):
                for j in range(N_SLABS):
                    _to_natural(part, j, slab, j, dil, tm)

        cos_t = cos_ref[...]
        sin_t = sin_ref[...]
        for j in range(N_SLABS):
            a = slab[j]
            if j < 6:
                a = a * cos_t - _rot_half(a * sin_t)
            if j < 3:
                a = a * (HEAD_DIM ** -0.5)
            dq_ref[:, _lane_cols(j)] = a.astype(BF16)
        dh[...] += _dot_nt(dq_ref[...], w_ref[...])

        @pl.when(g == 2)
        def _():
            dx, dn = _rms_bwd(dh[...], x_ref[...], g_ref[...])
            dx_ref[...] = dxo_ref[...] + dx
            dn_ref[...] += dn

    row_spec = pl.BlockSpec((tm, D_MODEL), lambda t, g: (t, 0))
    vec_spec = pl.BlockSpec((1, D_MODEL), lambda t, g: (0, 0))
    tab_spec = pl.BlockSpec((tm, LANE), lambda t, g: (t, 0))
    part_specs = [pl.BlockSpec((d, tm // d, QKV_PAD), lambda t, g: (0, t, 0)) for d in GROUP_DIL]
    return _pcall(
        body, name="qkv_bwd", grid=(t_len // tm, 3),
        in_specs=part_specs + [pl.BlockSpec((None, D_MODEL, QKV_PAD), lambda t, g: (g, 0, 0)),
                               row_spec, row_spec, vec_spec, tab_spec, tab_spec],
        out_specs=[pl.BlockSpec((None, tm, QKV_PAD), lambda t, g: (g, t, 0)), row_spec, vec_spec],
        out_shape=[_sds((3, t_len, QKV_PAD), BF16), _sds((t_len, D_MODEL), F32), _sds((1, D_MODEL), F32)],
        scratch_shapes=[pltpu.VMEM((tm, D_MODEL), F32), pltpu.VMEM((N_SLABS, tm, LANE), F32)],
        semantics=("arbitrary", "arbitrary"),
    )(*dqkv_parts, w_pad, dxo, x, g_row, cos, sin)


def final_fwd_bwd(x, g_row, target):
    t_len = x.shape[0]
    tm = _tile_rows(t_len)

    def body(x_ref, g_ref, tgt_ref, dx_ref, dn_ref, loss_ref):
        @pl.when(pl.program_id(0) == 0)
        def _():
            dn_ref[...] = jnp.zeros_like(dn_ref)
            loss_ref[...] = jnp.zeros_like(loss_ref)

        x_t = x_ref[...]
        g = g_ref[...]
        diff = _rms_fwd(x_t, g) - tgt_ref[...]
        loss_ref[...] += 0.5 * jnp.sum(jnp.mean(diff * diff, axis=-1, keepdims=True), axis=0, keepdims=True)
        dx, dn = _rms_bwd(diff * (1.0 / D_MODEL), x_t, g)
        dx_ref[...] = dx
        dn_ref[...] += dn

    row_spec = pl.BlockSpec((tm, D_MODEL), lambda t: (t, 0))
    vec_spec = pl.BlockSpec((1, D_MODEL), lambda t: (0, 0))
    return _pcall(
        body, name="final_fwd_bwd", grid=(t_len // tm,),
        in_specs=[row_spec, vec_spec, row_spec],
        out_specs=[row_spec, vec_spec, pl.BlockSpec((1, 1), lambda t: (0, 0))],
        out_shape=[_sds((t_len, D_MODEL), F32), _sds((1, D_MODEL), F32), _sds((1, 1), F32)],
        semantics=("arbitrary",),
    )(x, g_row, target)


def pool_bwd(dxo, x, g_row, w_in, w_grp, scale, w_out, zr, duty=None):
    t_len = x.shape[0]
    tm = _tile_rows(t_len)
    nt = t_len // tm

    def body(dxo_ref, x_ref, g_ref, win_ref, wgrp_ref, scale_ref, wout_ref, zr_ref,
             dzs_ref, du_ref, dx_ref, dn_ref, dsc_ref, ebuf):
        i = pl.program_id(0)
        t = nt - 1 - i

        @pl.when(i == 0)
        def _():
            ebuf[pl.ds(tm, POOL_HALO), :] = jnp.zeros((POOL_HALO, D_MODEL), F32)
            dn_ref[...] = jnp.zeros_like(dn_ref)
            dsc_ref[...] = jnp.zeros_like(dsc_ref)

        dxo_t = dxo_ref[...]
        dz = _dot_nt(dxo_t.astype(BF16), wout_ref[...])
        dsc_ref[...] += jnp.sum(dz * zr_ref[...].astype(F32), axis=0, keepdims=True)
        dzs_ref[...] = (dz * scale_ref[...]).astype(BF16)
        row = t * tm + lax.broadcasted_iota(jnp.int32, (tm, 1), 0)
        for gi, w in enumerate(POOL_WINDOWS):
            cols = pl.ds(gi * POOL_GROUP_DIM, POOL_GROUP_DIM)
            dp_g = _dot_nt(dzs_ref[:, cols], wgrp_ref[gi])
            cnt = jnp.minimum(row + 1, w).astype(F32)
            ebuf[pl.ds(0, tm), cols] = dp_g / cnt
            acc = -dp_g
            for j in range(w):
                acc = acc + ebuf[pl.ds(j, tm), cols]
            du_ref[:, cols] = acc.astype(BF16)
        ebuf[pl.ds(tm, POOL_HALO), :] = ebuf[pl.ds(0, POOL_HALO), :]
        dh = _dot_nt(du_ref[...], win_ref[...])
        dx, dn = _rms_bwd(dh, x_ref[...], g_ref[...])
        dx_ref[...] = dxo_t + dx
        dn_ref[...] += dn

    row_spec = pl.BlockSpec((tm, D_MODEL), lambda i: (nt - 1 - i, 0))
    full = lambda shape: pl.BlockSpec(shape, lambda i: (0,) * len(shape))
    vec = full((1, D_MODEL))
    return _pcall(
        body, name="pool_bwd", grid=(nt,),
        in_specs=[row_spec, row_spec, vec, full((D_MODEL, D_MODEL)), full((4, POOL_GROUP_DIM, POOL_GROUP_DIM)),
                  vec, full((D_MODEL, D_MODEL)), row_spec],
        out_specs=[row_spec, row_spec, row_spec, vec, vec],
        out_shape=[_sds((t_len, D_MODEL), BF16), _sds((t_len, D_MODEL), BF16), _sds((t_len, D_MODEL), F32),
                   _sds((1, D_MODEL), F32), _sds((1, D_MODEL), F32)],
        scratch_shapes=[pltpu.VMEM((tm + POOL_HALO, D_MODEL), F32)],
        semantics=("arbitrary",), duty=duty,
    )(dxo, x, g_row, w_in, w_grp, scale, w_out, zr)


def _mesh_pos():
    return lax.axis_index("x"), lax.axis_index("y"), lax.axis_index("c")


def _other_chips(x, y):
    return [(1 - x, y), (x, 1 - y), (1 - x, 1 - y)]


def _remote(src, dst, send_sem, recv_sem, device):
    return pltpu.make_async_remote_copy(src_ref=src, dst_ref=dst, send_sem=send_sem, recv_sem=recv_sem,
                                        device_id=device, device_id_type=MESH)


class _Duty:
    aliases = {}

    def mid(self, ins, outs, sems):
        pass


class Together(_Duty):
    def __init__(self, duties):
        self.duties = duties
        self.ins = [a for d in duties for a in d.ins]
        self.out_shape = [s for d in duties for s in d.out_shape]
        self.scratch = [s for d in duties for s in d.scratch]
        self.aliases = {}
        i0 = o0 = 0
        for d in duties:
            self.aliases.update({i0 + i: o0 + o for i, o in d.aliases.items()})
            i0 += len(d.ins)
            o0 += len(d.out_shape)

    def _each(self, ins, outs, sems):
        i0 = o0 = s0 = 0
        for d in self.duties:
            ni, no, ns = len(d.ins), len(d.out_shape), len(d.scratch)
            yield d, ins[i0:i0 + ni], outs[o0:o0 + no], sems[s0:s0 + ns]
            i0, o0, s0 = i0 + ni, o0 + no, s0 + ns

    def split(self, outs):
        return [list(o) for _, _, o, _ in self._each(self.ins, outs, self.scratch)]

    def start(self, ins, outs, sems):
        for d, i, o, s in self._each(ins, outs, sems):
            d.start(i, o, s)

    def mid(self, ins, outs, sems):
        for d, i, o, s in self._each(ins, outs, sems):
            d.mid(i, o, s)

    def finish(self, ins, outs, sems):
        for d, i, o, s in self._each(ins, outs, sems):
            d.finish(i, o, s)


def run_duty(duty, name):
    d_in, d_out = len(duty.ins), len(duty.out_shape)

    def body(*refs):
        ins, outs, sems = refs[:d_in], refs[d_in:d_in + d_out], refs[d_in + d_out:]
        duty.start(ins, outs, sems)
        duty.mid(ins, outs, sems)
        duty.finish(ins, outs, sems)

    return pl.pallas_call(
        body, name=name, out_shape=list(duty.out_shape), in_specs=[_ANY] * d_in, out_specs=[_ANY] * d_out,
        scratch_shapes=list(duty.scratch), input_output_aliases=dict(duty.aliases),
        compiler_params=pltpu.CompilerParams(has_side_effects=True),
    )(*duty.ins)


class GatherWeights(_Duty):
    def __init__(self, shards):
        n = self.n = len(shards)
        self.halves = [s.shape[0] // 2 for s in shards]
        my_slot = 2 * lax.axis_index("x") + lax.axis_index("y")
        staged = [lax.dynamic_update_slice(lax.empty((N_SHARDS,) + s.shape, s.dtype), s[None], (my_slot, 0, 0))
                  for s in shards]
        self.ins = list(shards) + staged
        self.out_shape = [_sds((N_SHARDS,) + s.shape, s.dtype) for s in shards]
        self.aliases = {n + a: a for a in range(n)}
        self.scratch = [pltpu.SemaphoreType.DMA((n, 6)), pltpu.SemaphoreType.DMA((n, 6))]

    def _over_ici(self, ins, outs, sems):
        x, y, c = _mesh_pos()
        return [_remote(ins[a].at[pl.ds(c * h, h)], outs[a].at[2 * x + y, pl.ds(c * h, h)],
                        sems[0].at[a, j], sems[1].at[a, j], (*chip, c))
                for a, h in enumerate(self.halves) for j, chip in enumerate(_other_chips(x, y))]

    def _forwards(self, outs, sems, half_of):
        x, y, c = _mesh_pos()
        cps = []
        for a, h in enumerate(self.halves):
            for j, chip in enumerate(_other_chips(x, y)):
                slot = outs[a].at[2 * chip[0] + chip[1], pl.ds(half_of(c) * h, h)]
                cps.append(_remote(slot, slot, sems[0].at[a, 3 + j], sems[1].at[a, 3 + j], (x, y, 1 - c)))
        return cps

    def start(self, ins, outs, sems):
        for cp in self._over_ici(ins, outs, sems):
            cp.start()

    def mid(self, ins, outs, sems):
        x, y, c = _mesh_pos()
        forwards = self._forwards(outs, sems, lambda core: core)
        k = 0
        for a, h in enumerate(self.halves):
            for j, chip in enumerate(_other_chips(x, y)):
                slot = outs[a].at[2 * chip[0] + chip[1], pl.ds(c * h, h)]
                _remote(slot, slot, sems[0].at[a, j], sems[1].at[a, j], (*chip, c)).wait_recv()
                forwards[k].start()
                k += 1

    def finish(self, ins, outs, sems):
        for cp in self._forwards(outs, sems, lambda core: 1 - core):
            cp.wait_recv()
        for cp in self._over_ici(ins, outs, sems) + self._forwards(outs, sems, lambda core: core):
            cp.wait_send()


class SiblingHalves(_Duty):
    def __init__(self, grads):
        n = len(grads)
        self.halves = [g.shape[1] // 2 for g in grads]
        self.ins = list(grads)
        self.out_shape = [_sds((N_SHARDS, h, g.shape[2]), g.dtype) for g, h in zip(grads, self.halves)]
        self.scratch = [pltpu.SemaphoreType.DMA((n,)), pltpu.SemaphoreType.DMA((n,))]

    def _copies(self, ins, outs, sems):
        x, y, c = _mesh_pos()
        return [_remote(ins[a].at[:, pl.ds((1 - c) * h, h)], outs[a], sems[0].at[a], sems[1].at[a], (x, y, 1 - c))
                for a, h in enumerate(self.halves)]

    def start(self, ins, outs, sems):
        for cp in self._copies(ins, outs, sems):
            cp.start()

    def finish(self, ins, outs, sems):
        for cp in self._copies(ins, outs, sems):
            cp.wait()


class ChipExchange(_Duty):
    def __init__(self, parts):
        n = self.n = len(parts)
        self.ins = list(parts)
        self.out_shape = [_sds(p.shape, p.dtype) for p in parts]
        self.scratch = [pltpu.SemaphoreType.DMA((n, 3)), pltpu.SemaphoreType.DMA((n, 3))]

    def _copies(self, ins, outs, sems, arriving):
        x, y, c = _mesh_pos()
        cps = []
        for a in range(self.n):
            for j, chip in enumerate(_other_chips(x, y)):
                theirs = 2 * chip[0] + chip[1]
                src = outs[a].at[theirs] if arriving else ins[a].at[theirs]
                dst = outs[a].at[theirs] if arriving else outs[a].at[2 * x + y]
                cps.append(_remote(src, dst, sems[0].at[a, j], sems[1].at[a, j], (*chip, c)))
        return cps

    def start(self, ins, outs, sems):
        for cp in self._copies(ins, outs, sems, False):
            cp.start()

    def finish(self, ins, outs, sems):
        for cp in self._copies(ins, outs, sems, True):
            cp.wait_recv()
        for cp in self._copies(ins, outs, sems, False):
            cp.wait_send()


class SiblingShare(_Duty):
    def __init__(self, reduced):
        n = self.n = len(reduced)
        self.ins = list(reduced)
        self.out_shape = [_sds(r.shape, r.dtype) for r in reduced]
        self.aliases = {a: a for a in range(n)}
        self.scratch = [pltpu.SemaphoreType.DMA((n,)), pltpu.SemaphoreType.DMA((n,))]

    def _copies(self, outs, sems, half_of):
        x, y, c = _mesh_pos()
        cps = []
        for a in range(self.n):
            h = outs[a].shape[0] // 2
            rows = outs[a].at[pl.ds(half_of(c) * h, h)]
            cps.append(_remote(rows, rows, sems[0].at[a], sems[1].at[a], (x, y, 1 - c)))
        return cps

    def start(self, ins, outs, sems):
        for cp in self._copies(outs, sems, lambda core: core):
            cp.start()

    def finish(self, ins, outs, sems):
        for cp in self._copies(outs, sems, lambda core: 1 - core):
            cp.wait_recv()
        for cp in self._copies(outs, sems, lambda core: core):
            cp.wait_send()


def allreduce_small(v):
    def body(v_ref, o_ref, buf, send_sems, recv_sems):
        x, y, c = _mesh_pos()
        me = 4 * x + 2 * y + c
        buf[me] = v_ref[...]
        flip = lambda p, f: 1 - p if f else p
        peers = [(flip(x, k & 4), flip(y, k & 2), flip(c, k & 1)) for k in range(1, N_DEV)]
        cps = []
        for k, peer in enumerate(peers):
            cp = _remote(v_ref, buf.at[me], send_sems.at[k], recv_sems.at[k], peer)
            cp.start()
            cps.append(cp)
        for k, peer in enumerate(peers):
            slot = buf.at[4 * peer[0] + 2 * peer[1] + peer[2]]
            _remote(slot, slot, send_sems.at[k], recv_sems.at[k], peer).wait_recv()
        for cp in cps:
            cp.wait_send()
        acc = buf[0]
        for i in range(1, N_DEV):
            acc = acc + buf[i]
        o_ref[...] = acc

    vm = pl.BlockSpec(memory_space=pltpu.VMEM)
    return pl.pallas_call(
        body, name="allreduce_small", out_shape=_sds(v.shape, v.dtype), in_specs=[vm], out_specs=vm,
        scratch_shapes=[pltpu.VMEM((N_DEV,) + v.shape, v.dtype),
                        pltpu.SemaphoreType.DMA((N_DEV - 1,)), pltpu.SemaphoreType.DMA((N_DEV - 1,))],
        compiler_params=pltpu.CompilerParams(has_side_effects=True),
    )(v)


def add_my_half(grad, theirs, c_idx, name):
    _, r, cols = grad.shape
    h = r // 2

    def body(c_ref, g_ref, t_ref, o_ref):
        o_ref[...] = (g_ref[...] + t_ref[...]).astype(BF16)

    slot = pl.BlockSpec((None, h, cols), lambda s, c: (s, 0, 0))
    grid_spec = pltpu.PrefetchScalarGridSpec(
        num_scalar_prefetch=1, grid=(N_SHARDS,),
        in_specs=[pl.BlockSpec((None, h, cols), lambda s, c: (s, c[0], 0)), slot], out_specs=slot)
    return pl.pallas_call(
        body, name=name, grid_spec=grid_spec, out_shape=_sds((N_SHARDS, h, cols), BF16),
        compiler_params=pltpu.CompilerParams(dimension_semantics=("arbitrary",), vmem_limit_bytes=VMEM_LIMIT_BYTES),
    )(c_idx, grad, theirs)


def sum_slots(received, mine, pos_idx, name):
    _, h, cols = received.shape

    def body(pos_ref, r_ref, m_ref, o_ref):
        acc = None
        for k in range(N_SHARDS):
            term = jnp.where(pos_ref[0] == k, m_ref[k], r_ref[k]).astype(F32)
            acc = term if acc is None else acc + term
        o_ref[...] = acc

    whole = pl.BlockSpec((N_SHARDS, h, cols), lambda i, pos: (0, 0, 0))
    grid_spec = pltpu.PrefetchScalarGridSpec(
        num_scalar_prefetch=1, grid=(1,), in_specs=[whole, whole],
        out_specs=pl.BlockSpec((h, cols), lambda i, pos: (pos[1], 0)))
    return pl.pallas_call(
        body, name=name, grid_spec=grid_spec, out_shape=_sds((2 * h, cols), F32),
        compiler_params=pltpu.CompilerParams(dimension_semantics=("arbitrary",), vmem_limit_bytes=VMEM_LIMIT_BYTES),
    )(pos_idx, received, mine)


def adamw(name, grads, w, m, v):
    n_layers, r, cols = w.shape
    tr = r // 2 if r % 16 == 0 else r
    bias1 = 1.0 - ADAM_B1 ** ADAM_STEP
    bias2 = 1.0 - ADAM_B2 ** ADAM_STEP

    def body(*refs):
        g_refs = refs[:n_layers]
        w_ref, m_ref, v_ref, go_ref, d_ref, mo_ref, vo_ref = refs[n_layers:]
        g = g_refs[0][...]
        for layer in range(1, n_layers):
            g = jnp.where(pl.program_id(0) == layer, g_refs[layer][...], g)
        m_new = ADAM_B1 * m_ref[...] + (1.0 - ADAM_B1) * g
        v_new = ADAM_B2 * v_ref[...] + (1.0 - ADAM_B2) * (g * g)
        m_hat = m_new / bias1
        v_hat = v_new / bias2
        go_ref[...] = g
        d_ref[...] = -ADAM_LR * (m_hat / (jnp.sqrt(v_hat) + ADAM_EPS) + ADAM_WD * w_ref[...])
        mo_ref[...] = m_new
        vo_ref[...] = v_new

    g_spec = pl.BlockSpec((tr, cols), lambda l, i: (i, 0))
    lay_spec = pl.BlockSpec((None, tr, cols), lambda l, i: (l, i, 0))
    shape = _sds((n_layers, r, cols), F32)
    return _pcall(
        body, name=name, grid=(n_layers, r // tr),
        in_specs=[g_spec] * n_layers + [lay_spec] * 3, out_specs=[lay_spec] * 4,
        out_shape=[shape] * 4, semantics=("arbitrary", "arbitrary"),
    )(*grads, w, m, v)


def kernel(x, norm_mix, norm_ffn, norm_final, pool_w_in, pool_w_group, pool_scale, pool_w_out, attn_w_qkv, attn_w_out, ffn_w_gate, ffn_w_up, ffn_w_down, loss_target, m_norm_mix, m_norm_ffn, m_norm_final, m_pool_w_in, m_pool_w_group, m_pool_scale, m_pool_w_out, m_attn_w_qkv, m_attn_w_out, m_ffn_w_gate, m_ffn_w_up, m_ffn_w_down, v_norm_mix, v_norm_ffn, v_norm_final, v_pool_w_in, v_pool_w_group, v_pool_scale, v_pool_w_out, v_attn_w_qkv, v_attn_w_out, v_ffn_w_gate, v_ffn_w_up, v_ffn_w_down):
    t_len = x.shape[1]
    x0 = x.reshape(t_len, D_MODEL)
    target = loss_target.reshape(t_len, D_MODEL)
    row = lambda a: a.reshape(1, D_MODEL)

    grp_rows = POOL_GROUP_DIM // N_SHARDS
    bf = lambda a: a.astype(BF16)
    gate_t, up_t = jnp.swapaxes(ffn_w_gate, 1, 2), jnp.swapaxes(ffn_w_up, 1, 2)
    pool_shards = [bf(pool_w_in[0]), bf(pool_w_group[0].reshape(4 * grp_rows, POOL_GROUP_DIM)), bf(pool_w_out[0])]
    ffn0_shards = [bf(gate_t[0]), bf(up_t[0]), bf(ffn_w_down[0])]
    late_shards = [bf(attn_w_qkv[0]), bf(attn_w_out[0]), bf(gate_t[1]), bf(up_t[1]), bf(ffn_w_down[1])]
    cos, sin = rope_tables(t_len)
    c_idx = lax.axis_index("c").astype(jnp.int32).reshape(1)
    pos_idx = jnp.stack([2 * lax.axis_index("x") + lax.axis_index("y"), lax.axis_index("c")]).astype(jnp.int32)
    chip_rows = lambda g: g.reshape(N_SHARDS, D_MODEL // N_SHARDS, D_MODEL)

    def add_halves(grads, theirs, names):
        return [add_my_half(g, t, c_idx, f"rs_add_{nm}") for g, t, nm in zip(grads, theirs, names)]

    def sum_chips(received, partials, names):
        return [sum_slots(r, p_, pos_idx, f"rs_sum_{nm}") for r, p_, nm in zip(received, partials, names)]

    g_pool = run_duty(GatherWeights(pool_shards), "gather_pool")
    w_in = g_pool[0].reshape(D_MODEL, D_MODEL)
    w_grp = g_pool[1].reshape(N_SHARDS, 4, grp_rows, POOL_GROUP_DIM).transpose(1, 0, 2, 3).reshape(
        4, POOL_GROUP_DIM, POOL_GROUP_DIM)
    w_out = g_pool[2].reshape(D_MODEL, D_MODEL)
    (h0, p, zr, z, x1), ffn0 = pool_fwd(x0, row(norm_mix[0]), w_in, w_grp, pool_scale, w_out,
                                        duty=GatherWeights(ffn0_shards))
    (h1, gate0, up0, x2), late = ffn_fwd(x1, row(norm_ffn[0]), *ffn0, "ffn_fwd0", duty=GatherWeights(late_shards))
    w_qkv = pad_qkv_weight(late[0])
    w_ao = jnp.concatenate(pad_groups(late[1].reshape(D_MODEL, D_MODEL), 0), axis=0)
    ffn1 = late[2:5]
    h2, *qkv_parts = qkv_fwd(x2, row(norm_mix[1]), w_qkv, cos, sin)
    o_parts, lse_parts = [], []
    for gi in range(3):
        o_g, lse_g = attn_fwd(qkv_parts[gi], gi, f"attn_fwd_g{gi}")
        o_parts.append(o_g)
        lse_parts.append(lse_g)
    x3, merged, o_nat, lse_nat = attn_out_fwd(x2, o_parts, lse_parts, w_ao)
    h3, gate1, up1, x4 = ffn_fwd(x3, row(norm_ffn[1]), *ffn1, "ffn_fwd1")
    dx4, d_norm_final, loss_local = final_fwd_bwd(x4, row(norm_final), target)

    act1, dgate1, dup1, dx3, d_nf1 = ffn_bwd(dx4, x3, row(norm_ffn[1]), gate1, up1, *ffn1, "ffn_bwd1")
    g_gate1 = wgrad_row_sharded("wgrad_gate1", dgate1, h3)
    g_up1 = wgrad_row_sharded("wgrad_up1", dup1, h3)
    g_down1 = wgrad_row_sharded("wgrad_down1", act1, dx4)
    do_parts, c_parts = attn_out_bwd(dx3, w_ao, o_nat, lse_nat)
    g_ao = unpad_groups(jnp.split(wgrad_full("wgrad_attn_out", merged, dx3), 3, axis=0), 0)

    set1, names1 = [g_gate1, g_up1, g_down1, chip_rows(g_ao)], ["gate1", "up1", "down1", "attn_out"]
    attn_b = lambda gi, duty: attn_bwd(qkv_parts[gi], do_parts[gi], lse_parts[gi], c_parts[gi], gi,
                                       f"attn_bwd_g{gi}", duty=duty)
    dqkv0, theirs1 = attn_b(0, SiblingHalves(set1))
    partials1 = add_halves(set1, theirs1, names1)
    dqkv1, received1 = attn_b(1, ChipExchange(partials1))
    dqkv2, full1 = attn_b(2, SiblingShare(sum_chips(received1, partials1, names1)))
    dqkv, dx2, d_nm1 = qkv_bwd([dqkv0, dqkv1, dqkv2], w_qkv, dx3, x2, row(norm_mix[1]), cos, sin)
    g_qkv = unpad_qkv_grad(wgrad_col_sharded("wgrad_qkv", h2, dqkv))

    set2, names2 = [g_qkv], ["qkv"]
    (act0, dgate0, dup0, dx1, d_nf0), theirs2 = ffn_bwd(dx2, x1, row(norm_ffn[0]), gate0, up0, *ffn0, "ffn_bwd0",
                                                        duty=SiblingHalves(set2))
    partials2 = add_halves(set2, theirs2, names2)
    g_gate0, received2 = wgrad_row_sharded("wgrad_gate0", dgate0, h1, duty=ChipExchange(partials2))

    duties = Together([SiblingShare(sum_chips(received2, partials2, names2)), SiblingHalves([g_gate0])])
    g_up0, outs = wgrad_row_sharded("wgrad_up0", dup0, h1, duty=duties)
    full2, theirs_g = duties.split(outs)
    partials_g = add_halves([g_gate0], theirs_g, ["gate0"])
    duties = Together([ChipExchange(partials_g), SiblingHalves([g_up0])])
    g_down0, outs = wgrad_row_sharded("wgrad_down0", act0, dx2, duty=duties)
    received_g, theirs_u = duties.split(outs)
    partials_u = add_halves([g_up0], theirs_u, ["up0"])
    duties = Together([ChipExchange(partials_u), SiblingHalves([g_down0]),
                       SiblingShare(sum_chips(received_g, partials_g, ["gate0"]))])
    g_out, outs = wgrad_full("wgrad_pool_out", z, dx1, duty=duties)
    received_u, theirs_d, full_g = duties.split(outs)
    partials_d = add_halves([g_down0], theirs_d, ["down0"])
    dzs, du, dx0, d_nm0, d_scale = pool_bwd(dx1, x0, row(norm_mix[0]), w_in, w_grp, pool_scale, w_out, zr)
    duties = Together([ChipExchange(partials_d), SiblingShare(sum_chips(received_u, partials_u, ["up0"]))])
    g_grp, outs = wgrad_pool_group("wgrad_pool_group", p, dzs, duty=duties)
    received_d, full_u = duties.split(outs)
    g_in, full_d = wgrad_full("wgrad_pool_in", h0, du,
                              duty=SiblingShare(sum_chips(received_d, partials_d, ["down0"])))
    full3 = [full_g[0], full_u[0], full_d[0]]

    set4 = [chip_rows(g_in), g_grp.reshape(N_SHARDS, 4 * grp_rows, POOL_GROUP_DIM), chip_rows(g_out)]
    names4 = ["pool_in", "pool_group", "pool_out"]
    partials4 = add_halves(set4, run_duty(SiblingHalves(set4), "rs_halves_pool"), names4)
    received4 = run_duty(ChipExchange(partials4), "rs_exchange_pool")
    full4 = run_duty(SiblingShare(sum_chips(received4, partials4, names4)), "rs_share_pool")
    full = [full4[0], full4[1], full4[2], full2[0], full1[3],
            full3[0], full1[0], full3[1], full1[1], full3[2], full1[2]]

    zero_row = jnp.zeros((1, D_MODEL), F32)
    small = jnp.concatenate([d_nm0, d_nm1, d_nf0, d_nf1, d_norm_final, d_scale,
                             jnp.broadcast_to(loss_local, (1, D_MODEL)), zero_row], axis=0)
    small = allreduce_small(small)
    loss = small[6, 0]

    pack = lambda a, b, c, d: jnp.concatenate([a, b, row(c), d, zero_row, zero_row], axis=0)[None]
    sg, sd, sm, sv = adamw("adamw_small", [small],
                           pack(norm_mix, norm_ffn, norm_final, pool_scale),
                           pack(m_norm_mix, m_norm_ffn, m_norm_final, m_pool_scale),
                           pack(v_norm_mix, v_norm_ffn, v_norm_final, v_pool_scale))
    unpack = lambda a: (a[0, 0:2], a[0, 2:4], a[0, 4], a[0, 5:6])

    def update(name, grads, w, m, v, transposed=False):
        if transposed:
            w, m, v = (jnp.swapaxes(a, 1, 2) for a in (w, m, v))
        n_layers = len(grads)
        shp = (n_layers,) + grads[0].shape
        outs = [o.reshape(w.shape) for o in adamw(name, grads, w.reshape(shp), m.reshape(shp), v.reshape(shp))]
        return [jnp.swapaxes(o, 1, 2) for o in outs] if transposed else outs

    big = [
        update("adamw_pool_in", [full[0]], pool_w_in, m_pool_w_in, v_pool_w_in),
        update("adamw_pool_group", [full[1]], pool_w_group, m_pool_w_group, v_pool_w_group),
        update("adamw_pool_out", [full[2]], pool_w_out, m_pool_w_out, v_pool_w_out),
        update("adamw_qkv", [full[3]], attn_w_qkv, m_attn_w_qkv, v_attn_w_qkv),
        update("adamw_attn_out", [full[4]], attn_w_out, m_attn_w_out, v_attn_w_out),
        update("adamw_gate", [full[5], full[6]], ffn_w_gate, m_ffn_w_gate, v_ffn_w_gate, transposed=True),
        update("adamw_up", [full[7], full[8]], ffn_w_up, m_ffn_w_up, v_ffn_w_up, transposed=True),
        update("adamw_down", [full[9], full[10]], ffn_w_down, m_ffn_w_down, v_ffn_w_down),
    ]

    def leaves(k, small_vals):
        nm, nf, nfin, psc = unpack(small_vals)
        return [nm, nf, nfin, big[0][k], big[1][k], psc, big[2][k], big[3][k], big[4][k],
                big[5][k], big[6][k], big[7][k]]

    grad_x = dx0.reshape(x.shape)
    return (loss, grad_x, *leaves(0, sg), *leaves(1, sd), *leaves(2, sm), *leaves(3, sv))
```

```python
import math

import jax
import jax.numpy as jnp
from jax import lax
from jax.experimental import pallas as pl
from jax.experimental.pallas import tpu as pltpu

F32 = jnp.float32
BF16 = jnp.bfloat16

D_MODEL = 1024
N_SHARDS = 4
N_DEV = 8
D_FF = 2816
FF_SHARD = D_FF // N_SHARDS
HEAD_DIM = 64
QKV_SHARD = 3 * D_MODEL // N_SHARDS
POOL_WINDOWS = (2, 4, 8, 16)
POOL_GROUP_DIM = 256
POOL_HALO = 16
ATTN_W = 128
GROUP_LANES = (0, 384, 704, 1024)
GROUP_HEADS = (6, 5, 5)
GROUP_DIL = (1, 4, 16)
ROPE_THETA = 10000.0
EPS = 1e-6
NEG_INF = -1e30
LANE = 128
VMEM_LIMIT_BYTES = 60 * 1024 * 1024

ADAM_LR = 0.001
ADAM_B1 = 0.9
ADAM_B2 = 0.999
ADAM_EPS = 1e-08
ADAM_WD = 0.01
ADAM_STEP = 10

NT_DIMS = (((1,), (1,)), ((), ()))
TN_DIMS = (((0,), (0,)), ((), ()))
MESH = pl.DeviceIdType.MESH


_ANY = pl.BlockSpec(memory_space=pl.ANY)


def _pcall(body, *, name, out_shape, grid=None, in_specs=None, out_specs=None, scratch_shapes=(),
           semantics=None, duty=None):
    kw = {}
    if in_specs is not None and duty is None:
        kw["in_specs"] = in_specs
    if out_specs is not None and duty is None:
        kw["out_specs"] = out_specs
    if grid is not None:
        kw["grid"] = grid
    params = dict(dimension_semantics=semantics, vmem_limit_bytes=VMEM_LIMIT_BYTES)
    if duty is None:
        return pl.pallas_call(body, name=name, out_shape=out_shape, scratch_shapes=list(scratch_shapes),
                              compiler_params=pltpu.CompilerParams(**params), **kw)

    single = not isinstance(out_shape, (list, tuple))
    c_out_shape = [out_shape] if single else list(out_shape)
    c_out_specs = [out_specs] if single else list(out_specs)
    n_in, n_out, n_scr = len(in_specs), len(c_out_shape), len(scratch_shapes)
    d_in, d_out = len(duty.ins), len(duty.out_shape)
    total = math.prod(grid)
    mid_step = (5 * total) // 6

    def wrapped(*refs):
        c_in, d_ins = refs[:n_in], refs[n_in:n_in + d_in]
        o0 = n_in + d_in
        c_outs, d_outs = refs[o0:o0 + n_out], refs[o0 + n_out:o0 + n_out + d_out]
        s0 = o0 + n_out + d_out
        c_scr, d_sems = refs[s0:s0 + n_scr], refs[s0 + n_scr:]
        step = pl.program_id(0)
        for ax in range(1, len(grid)):
            step = step * grid[ax] + pl.program_id(ax)

        @pl.when(step == 0)
        def _():
            duty.start(d_ins, d_outs, d_sems)

        body(*c_in, *c_outs, *c_scr)

        @pl.when(step == mid_step)
        def _():
            duty.mid(d_ins, d_outs, d_sems)

        @pl.when(step == total - 1)
        def _():
            duty.finish(d_ins, d_outs, d_sems)

    call = pl.pallas_call(
        wrapped, name=name, grid=grid,
        in_specs=list(in_specs) + [_ANY] * d_in, out_specs=c_out_specs + [_ANY] * d_out,
        out_shape=c_out_shape + list(duty.out_shape),
        scratch_shapes=list(scratch_shapes) + list(duty.scratch),
        input_output_aliases={n_in + i: n_out + o for i, o in duty.aliases.items()},
        compiler_params=pltpu.CompilerParams(has_side_effects=True, **params))

    def run(*args):
        outs = call(*args, *duty.ins)
        c = outs[:n_out]
        return (c[0] if single else list(c)), list(outs[n_out:])

    return run


def _sds(shape, dtype):
    return jax.ShapeDtypeStruct(tuple(shape), dtype)


def _dot(a, b):
    return jnp.dot(a, b, preferred_element_type=F32)


def _dot_nt(a, b):
    return lax.dot_general(a, b, NT_DIMS, preferred_element_type=F32)


def _dot_tn(a, b):
    return lax.dot_general(a, b, TN_DIMS, preferred_element_type=F32)


def _rms_fwd(x, g):
    r = lax.rsqrt(jnp.mean(x * x, axis=-1, keepdims=True) + EPS)
    return x * r * g


def _rms_bwd(dh, x, g):
    r = lax.rsqrt(jnp.mean(x * x, axis=-1, keepdims=True) + EPS)
    xh = x * r
    dg = jnp.sum(dh * xh, axis=0, keepdims=True)
    dxh = dh * g
    dx = r * (dxh - xh * jnp.mean(dxh * xh, axis=-1, keepdims=True))
    return dx, dg


def _sigmoid(x):
    return 0.5 * jnp.tanh(0.5 * x) + 0.5


def _tile_rows(t):
    return min(512, t)


def _sub_tiles(tm, n_sub=2):
    rows = tm // n_sub
    return [pl.ds(i * rows, rows) for i in range(n_sub)]


def _wgrad_rows(t):
    return min(2048, t)


def pool_fwd(x, g_row, w_in, w_grp, scale, w_out, duty=None):
    t_len = x.shape[0]
    tm = _tile_rows(t_len)

    def body(x_ref, g_ref, win_ref, wgrp_ref, scale_ref, wout_ref,
             h_ref, p_ref, zr_ref, z_ref, xo_ref, ubuf):
        t = pl.program_id(0)

        @pl.when(t == 0)
        def _():
            ubuf[pl.ds(0, POOL_HALO), :] = jnp.zeros((POOL_HALO, D_MODEL), F32)

        x_t = x_ref[...]
        h = _rms_fwd(x_t, g_ref[...]).astype(BF16)
        h_ref[...] = h
        ubuf[pl.ds(POOL_HALO, tm), :] = _dot(h, win_ref[...])
        row = t * tm + lax.broadcasted_iota(jnp.int32, (tm, 1), 0)
        for gi, w in enumerate(POOL_WINDOWS):
            cols = pl.ds(gi * POOL_GROUP_DIM, POOL_GROUP_DIM)
            u_g = ubuf[pl.ds(POOL_HALO, tm), cols]
            acc = u_g
            for j in range(1, w):
                acc = acc + ubuf[pl.ds(POOL_HALO - j, tm), cols]
            inv_cnt = 1.0 / jnp.minimum(row + 1, w).astype(F32)
            p_g = (acc * inv_cnt - u_g).astype(BF16)
            p_ref[:, cols] = p_g
            z_g = _dot(p_g, wgrp_ref[gi])
            zr_ref[:, cols] = z_g.astype(BF16)
            z_ref[:, cols] = (z_g * scale_ref[:, cols]).astype(BF16)
        ubuf[pl.ds(0, POOL_HALO), :] = ubuf[pl.ds(tm, POOL_HALO), :]
        xo_ref[...] = x_t + _dot(z_ref[...], wout_ref[...])

    row_spec = pl.BlockSpec((tm, D_MODEL), lambda t: (t, 0))
    full2 = lambda shape: pl.BlockSpec(shape, lambda t: (0,) * len(shape))
    return _pcall(
        body, name="pool_fwd", grid=(t_len // tm,),
        in_specs=[row_spec, full2((1, D_MODEL)), full2((D_MODEL, D_MODEL)),
                  full2((4, POOL_GROUP_DIM, POOL_GROUP_DIM)), full2((1, D_MODEL)), full2((D_MODEL, D_MODEL))],
        out_specs=[row_spec] * 5,
        out_shape=[_sds((t_len, D_MODEL), BF16)] * 4 + [_sds((t_len, D_MODEL), F32)],
        scratch_shapes=[pltpu.VMEM((tm + POOL_HALO, D_MODEL), F32)],
        semantics=("arbitrary",), duty=duty,
    )(x, g_row, w_in, w_grp, scale, w_out)


def ffn_fwd(x, g_row, w_gate_t, w_up_t, w_down, name, duty=None):
    t_len = x.shape[0]
    tm = min(1024, t_len)

    def body(x_ref, g_ref, wg_ref, wu_ref, wd_ref, h_ref, go_ref, uo_ref, ao_ref, xo_ref, hbuf, acc):
        s = pl.program_id(1)

        @pl.when(s == 0)
        def _():
            h = _rms_fwd(x_ref[...], g_ref[...]).astype(BF16)
            hbuf[...] = h
            h_ref[...] = h
            acc[...] = jnp.zeros_like(acc)

        for rows in _sub_tiles(tm, 1):
            h = hbuf[rows, :]
            gate = _dot_nt(h, wg_ref[...])
            up = _dot_nt(h, wu_ref[...])
            go_ref[rows, :] = gate.astype(BF16)
            uo_ref[rows, :] = up.astype(BF16)
            act = (gate * _sigmoid(gate) * up).astype(BF16)
            ao_ref[rows, :] = act
            acc[rows, :] += _dot(act, wd_ref[...])

        @pl.when(s == N_SHARDS - 1)
        def _():
            xo_ref[...] = x_ref[...] + acc[...]

    row_spec = pl.BlockSpec((tm, D_MODEL), lambda t, s: (t, 0))
    row_w = pl.BlockSpec((None, FF_SHARD, D_MODEL), lambda t, s: (s, 0, 0))
    act_spec = pl.BlockSpec((None, tm, FF_SHARD), lambda t, s: (s, t, 0))
    return _pcall(
        body, name=name, grid=(t_len // tm, N_SHARDS),
        in_specs=[row_spec, pl.BlockSpec((1, D_MODEL), lambda t, s: (0, 0)), row_w, row_w, row_w],
        out_specs=[row_spec, act_spec, act_spec, act_spec, row_spec],
        out_shape=[_sds((t_len, D_MODEL), BF16)] + [_sds((N_SHARDS, t_len, FF_SHARD), BF16)] * 3
                  + [_sds((t_len, D_MODEL), F32)],
        scratch_shapes=[pltpu.VMEM((tm, D_MODEL), BF16), pltpu.VMEM((tm, D_MODEL), F32)],
        semantics=("arbitrary", "arbitrary"), duty=duty,
    )(x, g_row, w_gate_t, w_up_t, w_down)


def ffn_bwd(dxo, x, g_row, gate, up, w_gate_t, w_up_t, w_down, name, duty=None):
    t_len = x.shape[0]
    tm = _tile_rows(t_len)

    def body(dxo_ref, x_ref, g_ref, gate_ref, up_ref, wg_ref, wu_ref, wd_ref,
             dg_ref, du_ref, dx_ref, dn_ref, dxb, dh):
        t = pl.program_id(0)
        s = pl.program_id(1)

        @pl.when(s == 0)
        def _():
            dxb[...] = dxo_ref[...].astype(BF16)
            dh[...] = jnp.zeros_like(dh)

        @pl.when(jnp.logical_and(s == 0, t == 0))
        def _():
            dn_ref[...] = jnp.zeros_like(dn_ref)

        sub_tiles = _sub_tiles(tm)
        dacts = [_dot_nt(dxb[rows, :], wd_ref[...]) for rows in sub_tiles]
        for rows, dact in zip(sub_tiles, dacts):
            gv = gate_ref[rows, :].astype(F32)
            uv = up_ref[rows, :].astype(F32)
            sg = _sigmoid(gv)
            dgv = (dact * uv * (sg * (1.0 + gv * (1.0 - sg)))).astype(BF16)
            duv = (dact * (gv * sg)).astype(BF16)
            dg_ref[rows, :] = dgv
            du_ref[rows, :] = duv
            dh[rows, :] += _dot(dgv, wg_ref[...]) + _dot(duv, wu_ref[...])

        @pl.when(s == N_SHARDS - 1)
        def _():
            dx, dn = _rms_bwd(dh[...], x_ref[...], g_ref[...])
            dx_ref[...] = dxo_ref[...] + dx
            dn_ref[...] += dn

    row_spec = pl.BlockSpec((tm, D_MODEL), lambda t, s: (t, 0))
    vec_spec = pl.BlockSpec((1, D_MODEL), lambda t, s: (0, 0))
    row_w = pl.BlockSpec((None, FF_SHARD, D_MODEL), lambda t, s: (s, 0, 0))
    act_spec = pl.BlockSpec((None, tm, FF_SHARD), lambda t, s: (s, t, 0))
    act_shape = _sds((N_SHARDS, t_len, FF_SHARD), BF16)
    return _pcall(
        body, name=name, grid=(t_len // tm, N_SHARDS),
        in_specs=[row_spec, row_spec, vec_spec, act_spec, act_spec, row_w, row_w, row_w],
        out_specs=[act_spec, act_spec, row_spec, vec_spec],
        out_shape=[act_shape, act_shape, _sds((t_len, D_MODEL), F32), _sds((1, D_MODEL), F32)],
        scratch_shapes=[pltpu.VMEM((tm, D_MODEL), BF16), pltpu.VMEM((tm, D_MODEL), F32)],
        semantics=("arbitrary", "arbitrary"), duty=duty,
    )(dxo, x, g_row, gate, up, w_gate_t, w_up_t, w_down)


def tn_matmul(name, a, b, a_spec, b_spec, out_shape, out_spec, grid, duty=None):
    def body(a_ref, b_ref, o_ref):
        @pl.when(pl.program_id(len(grid) - 1) == 0)
        def _():
            o_ref[...] = jnp.zeros_like(o_ref)

        res = _dot_tn(a_ref[...].astype(BF16), b_ref[...].astype(BF16))
        o_ref[...] += res.reshape(o_ref.shape)

    return _pcall(body, name=name, grid=grid, in_specs=[a_spec, b_spec], out_specs=out_spec,
                  out_shape=out_shape, semantics=("arbitrary",) * len(grid), duty=duty)(a, b)


def wgrad_full(name, a, b, duty=None):
    t_len, k = a.shape
    n = b.shape[1]
    tt = _wgrad_rows(t_len)
    return tn_matmul(name, a, b,
                     pl.BlockSpec((tt, k), lambda t: (t, 0)), pl.BlockSpec((tt, n), lambda t: (t, 0)),
                     _sds((k, n), F32), pl.BlockSpec((k, n), lambda t: (0, 0)), (t_len // tt,), duty)


def wgrad_col_sharded(name, a, b_sh, duty=None):
    t_len, k = a.shape
    n_sh, _, n = b_sh.shape
    tt = _wgrad_rows(t_len)
    return tn_matmul(name, a, b_sh,
                     pl.BlockSpec((tt, k), lambda s, t: (t, 0)), pl.BlockSpec((None, tt, n), lambda s, t: (s, t, 0)),
                     _sds((n_sh, k, n), F32), pl.BlockSpec((None, k, n), lambda s, t: (s, 0, 0)),
                     (n_sh, t_len // tt), duty)


def wgrad_row_sharded(name, a_sh, b, duty=None):
    t_len, n = b.shape
    n_sh, _, k = a_sh.shape
    tt = _wgrad_rows(t_len)
    return tn_matmul(name, a_sh, b,
                     pl.BlockSpec((None, tt, k), lambda s, t: (s, t, 0)), pl.BlockSpec((tt, n), lambda s, t: (t, 0)),
                     _sds((n_sh, k, n), F32), pl.BlockSpec((None, k, n), lambda s, t: (s, 0, 0)),
                     (n_sh, t_len // tt), duty)


def wgrad_pool_group(name, p, dzs, duty=None):
    t_len = p.shape[0]
    tt = _wgrad_rows(t_len)
    gd = POOL_GROUP_DIM
    rows = gd // N_SHARDS
    return tn_matmul(name, p, dzs,
                     pl.BlockSpec((tt, gd), lambda g, t: (t, g)), pl.BlockSpec((tt, gd), lambda g, t: (t, g)),
                     _sds((N_SHARDS, 4, rows, gd), F32),
                     pl.BlockSpec((N_SHARDS, None, rows, gd), lambda g, t: (0, g, 0, 0)),
                     (4, t_len // tt), duty)


PAD_LANES = 384
QKV_PAD = 3 * PAD_LANES
N_SLABS = QKV_PAD // LANE
GROUP_REAL = tuple(GROUP_LANES[g + 1] - GROUP_LANES[g] for g in range(3))
Q_BLOCK = 512


def pad_groups(w, axis):
    parts = []
    for g in range(3):
        blk = lax.slice_in_dim(w, GROUP_LANES[g], GROUP_LANES[g + 1], axis=axis)
        pad = [(0, 0)] * w.ndim
        pad[axis] = (0, PAD_LANES - GROUP_REAL[g])
        parts.append(jnp.pad(blk, pad))
    return parts


def unpad_groups(parts, axis):
    return jnp.concatenate([lax.slice_in_dim(p, 0, GROUP_REAL[g], axis=axis) for g, p in enumerate(parts)],
                           axis=axis)


def pad_qkv_weight(w_qkv_sh):
    w = jnp.transpose(w_qkv_sh, (1, 0, 2)).reshape(D_MODEL, 3 * D_MODEL)
    q, k, v = (pad_groups(w[:, i * D_MODEL:(i + 1) * D_MODEL], 1) for i in range(3))
    return jnp.stack([jnp.concatenate([q[g], k[g], v[g]], axis=1) for g in range(3)])


def unpad_qkv_grad(g_pad):
    cols = [unpad_groups([g_pad[g][:, i * PAD_LANES:(i + 1) * PAD_LANES] for g in range(3)], 1) for i in range(3)]
    w = jnp.concatenate(cols, axis=1)
    return jnp.transpose(w.reshape(D_MODEL, N_SHARDS, QKV_SHARD), (1, 0, 2))


def rope_tables(t_len):
    inv_freq = 1.0 / (ROPE_THETA ** (jnp.arange(0, HEAD_DIM, 2, dtype=F32) / HEAD_DIM))
    ang = jnp.arange(t_len, dtype=F32)[:, None] * inv_freq[None, :]
    cos_h, sin_h = lax.optimization_barrier((jnp.cos(ang), jnp.sin(ang)))
    reps = (1, 2 * LANE // HEAD_DIM)
    return jnp.tile(cos_h, reps), jnp.tile(sin_h, reps)


def _rot_half(v):
    n = v.shape[1]
    lane = lax.broadcasted_iota(jnp.int32, v.shape, 1)
    return jnp.where(lane % HEAD_DIM < HEAD_DIM // 2,
                     -pltpu.roll(v, n - HEAD_DIM // 2, 1), pltpu.roll(v, HEAD_DIM // 2, 1))


def _lane_cols(j):
    return slice(j * LANE, (j + 1) * LANE)


def _to_residue_major(slab, j_src, dst_ref, j_dst, dil, rows):
    for r in range(dil):
        dst_ref[r, :, _lane_cols(j_dst)] = slab[j_src, pl.ds(r, rows // dil, stride=dil), :].astype(dst_ref.dtype)


def _to_natural(src_ref, j_src, slab, j_dst, dil, rows):
    for r in range(dil):
        slab[j_dst, pl.ds(r, rows // dil, stride=dil), :] = src_ref[r, :, _lane_cols(j_src)].astype(F32)


def qkv_fwd(x, g_row, w_pad, cos, sin):
    t_len = x.shape[0]
    tm = _tile_rows(t_len)

    def body(x_ref, g_ref, w_ref, cos_ref, sin_ref, h_ref, o1_ref, o4_ref, o16_ref, slabs):
        h = _rms_fwd(x_ref[...], g_ref[...]).astype(BF16)
        h_ref[...] = h
        accs = [_dot(h, w_ref[gi]) for gi in range(3)]
        cos_t = cos_ref[...]
        sin_t = sin_ref[...]
        for gi, (dil, o_ref) in enumerate(zip(GROUP_DIL, (o1_ref, o4_ref, o16_ref))):
            slab = slabs.at[gi]
            for j in range(N_SLABS):
                a = accs[gi][:, _lane_cols(j)]
                if j < 6:
                    a = a * cos_t + _rot_half(a) * sin_t
                if j < 3:
                    a = a * (HEAD_DIM ** -0.5)
                if dil == 1:
                    o_ref[0, :, _lane_cols(j)] = a.astype(BF16)
                else:
                    slab[j] = a
                    _to_residue_major(slab, j, o_ref, j, dil, tm)

    row_spec = pl.BlockSpec((tm, D_MODEL), lambda t: (t, 0))
    tab_spec = pl.BlockSpec((tm, LANE), lambda t: (t, 0))
    out_specs = [row_spec] + [pl.BlockSpec((d, tm // d, QKV_PAD), lambda t: (0, t, 0)) for d in GROUP_DIL]
    out_shape = [_sds((t_len, D_MODEL), BF16)] + [_sds((d, t_len // d, QKV_PAD), BF16) for d in GROUP_DIL]
    return _pcall(
        body, name="qkv_fwd", grid=(t_len // tm,),
        in_specs=[row_spec, pl.BlockSpec((1, D_MODEL), lambda t: (0, 0)),
                  pl.BlockSpec((3, D_MODEL, QKV_PAD), lambda t: (0, 0, 0)), tab_spec, tab_spec],
        out_specs=out_specs, out_shape=out_shape,
        scratch_shapes=[pltpu.VMEM((3, N_SLABS, tm, LANE), F32)],
        semantics=("arbitrary",),
    )(x, g_row, w_pad, cos, sin)


def _band_mask(n):
    qi = lax.broadcasted_iota(jnp.int32, (ATTN_W, 2 * ATTN_W), 0)
    kj = lax.broadcasted_iota(jnp.int32, (ATTN_W, 2 * ATTN_W), 1)
    dist = ATTN_W + qi - kj
    return (dist >= 0) & (dist <= ATTN_W) & ((kj >= ATTN_W) | (n > 0))


def _half_masks():
    lane = lax.broadcasted_iota(jnp.int32, (1, LANE), 1)
    return [lane < HEAD_DIM, lane >= HEAD_DIM]


def _live_halves(gi, j):
    hms = _half_masks()
    return hms if (gi == 0 or j < 2) else hms[:1]


def attn_fwd(qkv_g, gi, name):
    dil, l_len, _ = qkv_g.shape
    qb = min(Q_BLOCK, l_len)
    nsub = qb // ATTN_W

    def body(q_ref, kc_ref, kp_ref, vc_ref, vp_ref, o_ref, lse_ref, kbuf, vbuf):
        n = pl.program_id(1)
        kbuf[pl.ds(0, ATTN_W), :] = kp_ref[...]
        kbuf[pl.ds(ATTN_W, qb), :] = kc_ref[...]
        vbuf[pl.ds(0, ATTN_W), :] = vp_ref[...]
        vbuf[pl.ds(ATTN_W, qb), :] = vc_ref[...]

        def sub(b, carry):
            r0 = pl.multiple_of(b * ATTN_W, ATTN_W)
            mask = _band_mask(n + b)
            krows = pl.ds(r0, 2 * ATTN_W)
            scores = []
            for j in range(3):
                q = q_ref[pl.ds(r0, ATTN_W), _lane_cols(j)]
                for hm in _live_halves(gi, j):
                    scores.append(_dot_nt(jnp.where(hm, q, jnp.zeros_like(q)), kbuf[krows, _lane_cols(j)]))
            scores = iter(scores)
            for j in range(3):
                cols = _lane_cols(j)
                v = vbuf[krows, cols]
                o = jnp.zeros((ATTN_W, LANE), F32)
                lse = jnp.zeros((ATTN_W, LANE), F32)
                for hm in _live_halves(gi, j):
                    s = jnp.where(mask, next(scores), NEG_INF)
                    m = jnp.max(s, axis=-1, keepdims=True)
                    e = jnp.exp(s - m)
                    den = jnp.sum(e, axis=-1, keepdims=True)
                    p = (e * (1.0 / den)).astype(BF16)
                    o = jnp.where(hm, _dot(p, v), o)
                    lse = jnp.where(hm, m + jnp.log(den), lse)
                o_ref[pl.ds(r0, ATTN_W), cols] = o.astype(BF16)
                lse_ref[pl.ds(r0, ATTN_W), cols] = lse
            return carry

        lax.fori_loop(0, nsub, sub, 0)

    cur = lambda c: pl.BlockSpec((None, qb, PAD_LANES), lambda r, n: (r, n, c))
    prev = lambda c: pl.BlockSpec((None, ATTN_W, PAD_LANES), lambda r, n: (r, jnp.maximum(n * nsub - 1, 0), c))
    out_spec = pl.BlockSpec((None, qb, PAD_LANES), lambda r, n: (r, n, 0))
    return _pcall(
        body, name=name, grid=(dil, l_len // qb),
        in_specs=[cur(0), cur(1), prev(1), cur(2), prev(2)],
        out_specs=[out_spec, out_spec],
        out_shape=[_sds((dil, l_len, PAD_LANES), BF16), _sds((dil, l_len, PAD_LANES), F32)],
        scratch_shapes=[pltpu.VMEM((qb + ATTN_W, PAD_LANES), BF16), pltpu.VMEM((qb + ATTN_W, PAD_LANES), BF16)],
        semantics=("arbitrary", "arbitrary"),
    )(qkv_g, qkv_g, qkv_g, qkv_g, qkv_g)


def _group_stats(lse):
    lane = lax.broadcasted_iota(jnp.int32, (1, QKV_PAD), 1)
    gmask = [(lane >= g * PAD_LANES) & (lane < g * PAD_LANES + GROUP_REAL[g]) for g in range(3)]
    lses, glse = [], []
    for g in range(3):
        mx = jnp.max(jnp.where(gmask[g], lse, -jnp.inf), axis=-1, keepdims=True)
        sm = jnp.sum(jnp.where(gmask[g], jnp.exp(lse - mx), 0.0), axis=-1, keepdims=True) / HEAD_DIM
        full = mx + jnp.log(sm)
        lses.append(full)
        glse.append(full - math.log(GROUP_HEADS[g]))
    top = jnp.maximum(jnp.maximum(glse[0], glse[1]), glse[2])
    ex = [jnp.exp(v - top) for v in glse]
    tot = ex[0] + ex[1] + ex[2]
    alpha = [v / tot for v in ex]
    pick = lambda vals: jnp.where(lane < PAD_LANES, vals[0], jnp.where(lane < 2 * PAD_LANES, vals[1], vals[2]))
    return gmask, alpha, pick([3.0 * a for a in alpha]), lse - pick(lses), pick


def attn_out_fwd(x, o_parts, lse_parts, w_out_pad):
    t_len = x.shape[0]
    tm = _tile_rows(t_len)

    def body(x_ref, o1, o4, o16, l1, l4, l16, w_ref, xo_ref, mg_ref, o_ref, lse_ref, o_slab, l_slab):
        for gi, (dil, og, lg) in enumerate(zip(GROUP_DIL, (o1, o4, o16), (l1, l4, l16))):
            for j in range(3):
                _to_natural(og, j, o_slab, 3 * gi + j, dil, tm)
                _to_natural(lg, j, l_slab, 3 * gi + j, dil, tm)
        o = jnp.concatenate([o_slab[j] for j in range(N_SLABS)], axis=1)
        lse = jnp.concatenate([l_slab[j] for j in range(N_SLABS)], axis=1)
        o_ref[...] = o.astype(BF16)
        lse_ref[...] = lse
        _, _, scale, _, _ = _group_stats(lse)
        merged = (o * scale).astype(BF16)
        mg_ref[...] = merged
        xo_ref[...] = x_ref[...] + _dot(merged, w_ref[...])

    row_spec = pl.BlockSpec((tm, D_MODEL), lambda t: (t, 0))
    pad_spec = pl.BlockSpec((tm, QKV_PAD), lambda t: (t, 0))
    part_specs = [pl.BlockSpec((d, tm // d, PAD_LANES), lambda t: (0, t, 0)) for d in GROUP_DIL]
    return _pcall(
        body, name="attn_out_fwd", grid=(t_len // tm,),
        in_specs=[row_spec] + part_specs + part_specs + [pl.BlockSpec((QKV_PAD, D_MODEL), lambda t: (0, 0))],
        out_specs=[row_spec, pad_spec, pad_spec, pad_spec],
        out_shape=[_sds((t_len, D_MODEL), F32), _sds((t_len, QKV_PAD), BF16),
                   _sds((t_len, QKV_PAD), BF16), _sds((t_len, QKV_PAD), F32)],
        scratch_shapes=[pltpu.VMEM((N_SLABS, tm, LANE), F32), pltpu.VMEM((N_SLABS, tm, LANE), F32)],
        semantics=("arbitrary",),
    )(x, *o_parts, *lse_parts, w_out_pad)


def attn_out_bwd(dxo, w_out_pad, o, lse):
    t_len = dxo.shape[0]
    tm = _tile_rows(t_len)

    def body(dx_ref, w_ref, o_ref, lse_ref, d1, d4, d16, c1, c4, c16, slab):
        dmerged = _dot_nt(dx_ref[...].astype(BF16), w_ref[...])
        o_t = o_ref[...].astype(F32)
        gmask, alpha, scale, lse_rel, pick = _group_stats(lse_ref[...])
        e = dmerged * o_t
        dalpha = [3.0 * jnp.sum(jnp.where(gmask[g], e, 0.0), axis=-1, keepdims=True) for g in range(3)]
        mean_da = alpha[0] * dalpha[0] + alpha[1] * dalpha[1] + alpha[2] * dalpha[2]
        dglse = [alpha[g] * (dalpha[g] - mean_da) for g in range(3)]
        dlse = pick(dglse) * jnp.exp(lse_rel)
        do = dmerged * scale
        es = e * scale
        lane = lax.broadcasted_iota(jnp.int32, (1, LANE), 1)
        first = lane < HEAD_DIM
        for j in range(N_SLABS):
            slab[j] = do[:, _lane_cols(j)]
        for gi, (dil, dg) in enumerate(zip(GROUP_DIL, (d1, d4, d16))):
            for j in range(3):
                _to_residue_major(slab, 3 * gi + j, dg, j, dil, tm)
        for j in range(N_SLABS):
            blk = es[:, _lane_cols(j)]
            s0 = jnp.sum(jnp.where(first, blk, 0.0), axis=-1, keepdims=True)
            s1 = jnp.sum(jnp.where(first, 0.0, blk), axis=-1, keepdims=True)
            slab[j] = jnp.where(first, s0, s1) - dlse[:, _lane_cols(j)]
        for gi, (dil, cg) in enumerate(zip(GROUP_DIL, (c1, c4, c16))):
            for j in range(3):
                _to_residue_major(slab, 3 * gi + j, cg, j, dil, tm)

    row_spec = pl.BlockSpec((tm, D_MODEL), lambda t: (t, 0))
    pad_spec = pl.BlockSpec((tm, QKV_PAD), lambda t: (t, 0))
    part_specs = [pl.BlockSpec((d, tm // d, PAD_LANES), lambda t: (0, t, 0)) for d in GROUP_DIL]
    shapes = lambda dt: [_sds((d, t_len // d, PAD_LANES), dt) for d in GROUP_DIL]
    outs = _pcall(
        body, name="attn_out_bwd", grid=(t_len // tm,),
        in_specs=[row_spec, pl.BlockSpec((QKV_PAD, D_MODEL), lambda t: (0, 0)), pad_spec, pad_spec],
        out_specs=part_specs + part_specs,
        out_shape=shapes(BF16) + shapes(F32),
        scratch_shapes=[pltpu.VMEM((N_SLABS, tm, LANE), F32)],
        semantics=("arbitrary",),
    )(dxo, w_out_pad, o, lse)
    return outs[:3], outs[3:]


def attn_bwd(qkv_g, do_g, lse_g, c_g, gi, name, duty=None):
    dil, l_len, _ = qkv_g.shape
    qb = min(Q_BLOCK, l_len)
    nsub = qb // ATTN_W
    nsb = l_len // qb

    def body(q_ref, kc_ref, kp_ref, vc_ref, vp_ref, do_ref, lse_ref, c_ref,
             qn_ref, don_ref, lsen_ref, cn_ref, o_ref, kbuf, vbuf, dkbuf, dvbuf):
        n = pl.program_id(1)
        kbuf[pl.ds(0, ATTN_W), :] = kp_ref[...]
        kbuf[pl.ds(ATTN_W, qb), :] = kc_ref[...]
        vbuf[pl.ds(0, ATTN_W), :] = vp_ref[...]
        vbuf[pl.ds(ATTN_W, qb), :] = vc_ref[...]
        dkbuf[...] = jnp.zeros_like(dkbuf)
        dvbuf[...] = jnp.zeros_like(dvbuf)

        def block(q_of, do_of, lse_of, c_of, krows, mask, dq_rows):
            heads = []
            for j in range(3):
                cols = _lane_cols(j)
                q, do_t, k, v = q_of(cols), do_of(cols), kbuf[krows, cols], vbuf[krows, cols]
                for hm in _live_halves(gi, j):
                    qh = jnp.where(hm, q, jnp.zeros_like(q))
                    doh = jnp.where(hm, do_t, jnp.zeros_like(do_t))
                    heads.append((j, hm, qh, doh, _dot_nt(qh, k), _dot_nt(doh, v)))
            for j in range(3):
                cols = _lane_cols(j)
                k = kbuf[krows, cols]
                dq = jnp.zeros((ATTN_W, LANE), F32)
                dk = jnp.zeros((k.shape[0], LANE), F32)
                dv = jnp.zeros((k.shape[0], LANE), F32)
                for hj, hm, qh, doh, s, dp in heads:
                    if hj != j:
                        continue
                    lse_h = jnp.max(jnp.where(hm, lse_of(cols), -jnp.inf), axis=-1, keepdims=True)
                    c_h = jnp.max(jnp.where(hm, c_of(cols), -jnp.inf), axis=-1, keepdims=True)
                    p = jnp.exp(jnp.where(mask, s, NEG_INF) - lse_h)
                    ds = (p * (dp - c_h)).astype(BF16)
                    if dq_rows is not None:
                        dq = jnp.where(hm, _dot(ds, k), dq)
                    dk = dk + _dot_tn(ds, qh)
                    dv = dv + _dot_tn(p.astype(BF16), doh)
                if dq_rows is not None:
                    o_ref[dq_rows, cols] = dq.astype(BF16)
                dkbuf[krows, cols] += dk
                dvbuf[krows, cols] += dv

        def sub(b, carry):
            rows = pl.ds(pl.multiple_of(b * ATTN_W, ATTN_W), ATTN_W)
            krows = pl.ds(pl.multiple_of(b * ATTN_W, ATTN_W), 2 * ATTN_W)
            block(lambda c: q_ref[rows, c], lambda c: do_ref[rows, c], lambda c: lse_ref[rows, c],
                  lambda c: c_ref[rows, c], krows, _band_mask(n + b), rows)
            return carry

        lax.fori_loop(0, nsub, sub, 0)

        qi = lax.broadcasted_iota(jnp.int32, (ATTN_W, ATTN_W), 0)
        kj = lax.broadcasted_iota(jnp.int32, (ATTN_W, ATTN_W), 1)
        nmask = (qi <= kj) & (n < nsb - 1)
        block(lambda c: qn_ref[:, c], lambda c: don_ref[:, c], lambda c: lsen_ref[:, c],
              lambda c: cn_ref[:, c], pl.ds(qb, ATTN_W), nmask, None)
        o_ref[:, pl.ds(PAD_LANES, PAD_LANES)] = dkbuf[pl.ds(ATTN_W, qb), :].astype(BF16)
        o_ref[:, pl.ds(2 * PAD_LANES, PAD_LANES)] = dvbuf[pl.ds(ATTN_W, qb), :].astype(BF16)

    cur = lambda c: pl.BlockSpec((None, qb, PAD_LANES), lambda r, n: (r, n, c))
    prev = lambda c: pl.BlockSpec((None, ATTN_W, PAD_LANES), lambda r, n: (r, jnp.maximum(n * nsub - 1, 0), c))
    nxt = pl.BlockSpec((None, ATTN_W, PAD_LANES), lambda r, n: (r, jnp.minimum((n + 1) * nsub, nsb * nsub - 1), 0))
    return _pcall(
        body, name=name, grid=(dil, nsb),
        in_specs=[cur(0), cur(1), prev(1), cur(2), prev(2), cur(0), cur(0), cur(0), nxt, nxt, nxt, nxt],
        out_specs=pl.BlockSpec((None, qb, QKV_PAD), lambda r, n: (r, n, 0)),
        out_shape=_sds((dil, l_len, QKV_PAD), BF16),
        scratch_shapes=[pltpu.VMEM((qb + ATTN_W, PAD_LANES), BF16), pltpu.VMEM((qb + ATTN_W, PAD_LANES), BF16),
                        pltpu.VMEM((qb + ATTN_W, PAD_LANES), F32), pltpu.VMEM((qb + ATTN_W, PAD_LANES), F32)],
        semantics=("arbitrary", "arbitrary"), duty=duty,
    )(qkv_g, qkv_g, qkv_g, qkv_g, qkv_g, do_g, lse_g, c_g, qkv_g, do_g, lse_g, c_g)


def qkv_bwd(dqkv_parts, w_pad, dxo, x, g_row, cos, sin):
    t_len = x.shape[0]
    tm = _tile_rows(t_len)

    def body(p1, p4, p16, w_ref, dxo_ref, x_ref, g_ref, cos_ref, sin_ref, dq_ref, dx_ref, dn_ref, slabs):
        @pl.when(pl.program_id(0) == 0)
        def _():
            dn_ref[...] = jnp.zeros_like(dn_ref)

        cos_t = cos_ref[...]
        sin_t = sin_ref[...]
        dh = None
        for gi, (dil, part) in enumerate(zip(GROUP_DIL, (p1, p4, p16))):
            slab = slabs.at[gi]
            for j in range(N_SLABS):
                if dil == 1:
                    a = part[0, :, _lane_cols(j)].astype(F32)
                else:
                    _to_natural(part, j, slab, j, dil, tm)
                    a = slab[j]
                if j < 6:
                    a = a * cos_t - _rot_half(a * sin_t)
                if j < 3:
                    a = a * (HEAD_DIM ** -0.5)
                dq_ref[gi, :, _lane_cols(j)] = a.astype(BF16)
            contrib = _dot_nt(dq_ref[gi], w_ref[gi])
            dh = contrib if dh is None else dh + contrib
        dx, dn = _rms_bwd(dh, x_ref[...], g_ref[...])
        dx_ref[...] = dxo_ref[...] + dx
        dn_ref[...] += dn

    row_spec = pl.BlockSpec((tm, D_MODEL), lambda t: (t, 0))
    vec_spec = pl.BlockSpec((1, D_MODEL), lambda t: (0, 0))
    tab_spec = pl.BlockSpec((tm, LANE), lambda t: (t, 0))
    part_specs = [pl.BlockSpec((d, tm // d, QKV_PAD), lambda t: (0, t, 0)) for d in GROUP_DIL]
    return _pcall(
        body, name="qkv_bwd", grid=(t_len // tm,),
        in_specs=part_specs + [pl.BlockSpec((3, D_MODEL, QKV_PAD), lambda t: (0, 0, 0)),
                               row_spec, row_spec, vec_spec, tab_spec, tab_spec],
        out_specs=[pl.BlockSpec((3, tm, QKV_PAD), lambda t: (0, t, 0)), row_spec, vec_spec],
        out_shape=[_sds((3, t_len, QKV_PAD), BF16), _sds((t_len, D_MODEL), F32), _sds((1, D_MODEL), F32)],
        scratch_shapes=[pltpu.VMEM((3, N_SLABS, tm, LANE), F32)],
        semantics=("arbitrary",),
    )(*dqkv_parts, w_pad, dxo, x, g_row, cos, sin)


def final_fwd_bwd(x, g_row, target):
    t_len = x.shape[0]
    tm = _tile_rows(t_len)

    def body(x_ref, g_ref, tgt_ref, dx_ref, dn_ref, loss_ref):
        @pl.when(pl.program_id(0) == 0)
        def _():
            dn_ref[...] = jnp.zeros_like(dn_ref)
            loss_ref[...] = jnp.zeros_like(loss_ref)

        x_t = x_ref[...]
        g = g_ref[...]
        diff = _rms_fwd(x_t, g) - tgt_ref[...]
        loss_ref[...] += 0.5 * jnp.sum(jnp.mean(diff * diff, axis=-1, keepdims=True), axis=0, keepdims=True)
        dx, dn = _rms_bwd(diff * (1.0 / D_MODEL), x_t, g)
        dx_ref[...] = dx
        dn_ref[...] += dn

    row_spec = pl.BlockSpec((tm, D_MODEL), lambda t: (t, 0))
    vec_spec = pl.BlockSpec((1, D_MODEL), lambda t: (0, 0))
    return _pcall(
        body, name="final_fwd_bwd", grid=(t_len // tm,),
        in_specs=[row_spec, vec_spec, row_spec],
        out_specs=[row_spec, vec_spec, pl.BlockSpec((1, 1), lambda t: (0, 0))],
        out_shape=[_sds((t_len, D_MODEL), F32), _sds((1, D_MODEL), F32), _sds((1, 1), F32)],
        semantics=("arbitrary",),
    )(x, g_row, target)


def pool_bwd(dxo, x, g_row, w_in, w_grp, scale, w_out, zr, duty=None):
    t_len = x.shape[0]
    tm = _tile_rows(t_len)
    nt = t_len // tm

    def body(dxo_ref, x_ref, g_ref, win_ref, wgrp_ref, scale_ref, wout_ref, zr_ref,
             dzs_ref, du_ref, dx_ref, dn_ref, dsc_ref, ebuf):
        i = pl.program_id(0)
        t = nt - 1 - i

        @pl.when(i == 0)
        def _():
            ebuf[pl.ds(tm, POOL_HALO), :] = jnp.zeros((POOL_HALO, D_MODEL), F32)
            dn_ref[...] = jnp.zeros_like(dn_ref)
            dsc_ref[...] = jnp.zeros_like(dsc_ref)

        dxo_t = dxo_ref[...]
        dz = _dot_nt(dxo_t.astype(BF16), wout_ref[...])
        dsc_ref[...] += jnp.sum(dz * zr_ref[...].astype(F32), axis=0, keepdims=True)
        dzs_ref[...] = (dz * scale_ref[...]).astype(BF16)
        row = t * tm + lax.broadcasted_iota(jnp.int32, (tm, 1), 0)
        for gi, w in enumerate(POOL_WINDOWS):
            cols = pl.ds(gi * POOL_GROUP_DIM, POOL_GROUP_DIM)
            dp_g = _dot_nt(dzs_ref[:, cols], wgrp_ref[gi])
            inv_cnt = 1.0 / jnp.minimum(row + 1, w).astype(F32)
            ebuf[pl.ds(0, tm), cols] = dp_g * inv_cnt
            acc = -dp_g
            for j in range(w):
                acc = acc + ebuf[pl.ds(j, tm), cols]
            du_ref[:, cols] = acc.astype(BF16)
        ebuf[pl.ds(tm, POOL_HALO), :] = ebuf[pl.ds(0, POOL_HALO), :]
        dh = _dot_nt(du_ref[...], win_ref[...])
        dx, dn = _rms_bwd(dh, x_ref[...], g_ref[...])
        dx_ref[...] = dxo_t + dx
        dn_ref[...] += dn

    row_spec = pl.BlockSpec((tm, D_MODEL), lambda i: (nt - 1 - i, 0))
    full = lambda shape: pl.BlockSpec(shape, lambda i: (0,) * len(shape))
    vec = full((1, D_MODEL))
    return _pcall(
        body, name="pool_bwd", grid=(nt,),
        in_specs=[row_spec, row_spec, vec, full((D_MODEL, D_MODEL)), full((4, POOL_GROUP_DIM, POOL_GROUP_DIM)),
                  vec, full((D_MODEL, D_MODEL)), row_spec],
        out_specs=[row_spec, row_spec, row_spec, vec, vec],
        out_shape=[_sds((t_len, D_MODEL), BF16), _sds((t_len, D_MODEL), BF16), _sds((t_len, D_MODEL), F32),
                   _sds((1, D_MODEL), F32), _sds((1, D_MODEL), F32)],
        scratch_shapes=[pltpu.VMEM((tm + POOL_HALO, D_MODEL), F32)],
        semantics=("arbitrary",), duty=duty,
    )(dxo, x, g_row, w_in, w_grp, scale, w_out, zr)


def _mesh_pos():
    return lax.axis_index("x"), lax.axis_index("y"), lax.axis_index("c")


def _other_chips(x, y):
    return [(1 - x, y), (x, 1 - y), (1 - x, 1 - y)]


def _remote(src, dst, send_sem, recv_sem, device):
    return pltpu.make_async_remote_copy(src_ref=src, dst_ref=dst, send_sem=send_sem, recv_sem=recv_sem,
                                        device_id=device, device_id_type=MESH)


class _Duty:
    aliases = {}

    def mid(self, ins, outs, sems):
        pass


class Together(_Duty):
    def __init__(self, duties):
        self.duties = duties
        self.ins = [a for d in duties for a in d.ins]
        self.out_shape = [s for d in duties for s in d.out_shape]
        self.scratch = [s for d in duties for s in d.scratch]
        self.aliases = {}
        i0 = o0 = 0
        for d in duties:
            self.aliases.update({i0 + i: o0 + o for i, o in d.aliases.items()})
            i0 += len(d.ins)
            o0 += len(d.out_shape)

    def _each(self, ins, outs, sems):
        i0 = o0 = s0 = 0
        for d in self.duties:
            ni, no, ns = len(d.ins), len(d.out_shape), len(d.scratch)
            yield d, ins[i0:i0 + ni], outs[o0:o0 + no], sems[s0:s0 + ns]
            i0, o0, s0 = i0 + ni, o0 + no, s0 + ns

    def split(self, outs):
        return [list(o) for _, _, o, _ in self._each(self.ins, outs, self.scratch)]

    def start(self, ins, outs, sems):
        for d, i, o, s in self._each(ins, outs, sems):
            d.start(i, o, s)

    def mid(self, ins, outs, sems):
        for d, i, o, s in self._each(ins, outs, sems):
            d.mid(i, o, s)

    def finish(self, ins, outs, sems):
        for d, i, o, s in self._each(ins, outs, sems):
            d.finish(i, o, s)


def run_duty(duty, name):
    d_in, d_out = len(duty.ins), len(duty.out_shape)

    def body(*refs):
        ins, outs, sems = refs[:d_in], refs[d_in:d_in + d_out], refs[d_in + d_out:]
        duty.start(ins, outs, sems)
        duty.mid(ins, outs, sems)
        duty.finish(ins, outs, sems)

    return pl.pallas_call(
        body, name=name, out_shape=list(duty.out_shape), in_specs=[_ANY] * d_in, out_specs=[_ANY] * d_out,
        scratch_shapes=list(duty.scratch), input_output_aliases=dict(duty.aliases),
        compiler_params=pltpu.CompilerParams(has_side_effects=True),
    )(*duty.ins)


class GatherWeights(_Duty):
    def __init__(self, shards):
        n = self.n = len(shards)
        self.halves = [s.shape[0] // 2 for s in shards]
        my_slot = 2 * lax.axis_index("x") + lax.axis_index("y")
        staged = [lax.dynamic_update_slice(lax.empty((N_SHARDS,) + s.shape, s.dtype), s[None], (my_slot, 0, 0))
                  for s in shards]
        self.ins = list(shards) + staged
        self.out_shape = [_sds((N_SHARDS,) + s.shape, s.dtype) for s in shards]
        self.aliases = {n + a: a for a in range(n)}
        self.scratch = [pltpu.SemaphoreType.DMA((n, 6)), pltpu.SemaphoreType.DMA((n, 6))]

    def _over_ici(self, ins, outs, sems):
        x, y, c = _mesh_pos()
        return [_remote(ins[a].at[pl.ds(c * h, h)], outs[a].at[2 * x + y, pl.ds(c * h, h)],
                        sems[0].at[a, j], sems[1].at[a, j], (*chip, c))
                for a, h in enumerate(self.halves) for j, chip in enumerate(_other_chips(x, y))]

    def _forwards(self, outs, sems, half_of):
        x, y, c = _mesh_pos()
        cps = []
        for a, h in enumerate(self.halves):
            for j, chip in enumerate(_other_chips(x, y)):
                slot = outs[a].at[2 * chip[0] + chip[1], pl.ds(half_of(c) * h, h)]
                cps.append(_remote(slot, slot, sems[0].at[a, 3 + j], sems[1].at[a, 3 + j], (x, y, 1 - c)))
        return cps

    def start(self, ins, outs, sems):
        for cp in self._over_ici(ins, outs, sems):
            cp.start()

    def mid(self, ins, outs, sems):
        x, y, c = _mesh_pos()
        forwards = self._forwards(outs, sems, lambda core: core)
        k = 0
        for a, h in enumerate(self.halves):
            for j, chip in enumerate(_other_chips(x, y)):
                slot = outs[a].at[2 * chip[0] + chip[1], pl.ds(c * h, h)]
                _remote(slot, slot, sems[0].at[a, j], sems[1].at[a, j], (*chip, c)).wait_recv()
                forwards[k].start()
                k += 1

    def finish(self, ins, outs, sems):
        for cp in self._forwards(outs, sems, lambda core: 1 - core):
            cp.wait_recv()
        for cp in self._over_ici(ins, outs, sems) + self._forwards(outs, sems, lambda core: core):
            cp.wait_send()


class SiblingHalves(_Duty):
    def __init__(self, grads):
        n = len(grads)
        self.halves = [g.shape[1] // 2 for g in grads]
        self.ins = list(grads)
        self.out_shape = [_sds((N_SHARDS, h, g.shape[2]), g.dtype) for g, h in zip(grads, self.halves)]
        self.scratch = [pltpu.SemaphoreType.DMA((n,)), pltpu.SemaphoreType.DMA((n,))]

    def _copies(self, ins, outs, sems):
        x, y, c = _mesh_pos()
        return [_remote(ins[a].at[:, pl.ds((1 - c) * h, h)], outs[a], sems[0].at[a], sems[1].at[a], (x, y, 1 - c))
                for a, h in enumerate(self.halves)]

    def start(self, ins, outs, sems):
        for cp in self._copies(ins, outs, sems):
            cp.start()

    def finish(self, ins, outs, sems):
        for cp in self._copies(ins, outs, sems):
            cp.wait()


class ChipExchange(_Duty):
    def __init__(self, parts):
        n = self.n = len(parts)
        self.ins = list(parts)
        self.out_shape = [_sds(p.shape, p.dtype) for p in parts]
        self.scratch = [pltpu.SemaphoreType.DMA((n, 3)), pltpu.SemaphoreType.DMA((n, 3))]

    def _copies(self, ins, outs, sems, arriving):
        x, y, c = _mesh_pos()
        cps = []
        for a in range(self.n):
            for j, chip in enumerate(_other_chips(x, y)):
                theirs = 2 * chip[0] + chip[1]
                src = outs[a].at[theirs] if arriving else ins[a].at[theirs]
                dst = outs[a].at[theirs] if arriving else outs[a].at[2 * x + y]
                cps.append(_remote(src, dst, sems[0].at[a, j], sems[1].at[a, j], (*chip, c)))
        return cps

    def start(self, ins, outs, sems):
        for cp in self._copies(ins, outs, sems, False):
            cp.start()

    def finish(self, ins, outs, sems):
        for cp in self._copies(ins, outs, sems, True):
            cp.wait_recv()
        for cp in self._copies(ins, outs, sems, False):
            cp.wait_send()


class SiblingShare(_Duty):
    def __init__(self, reduced):
        n = self.n = len(reduced)
        self.ins = list(reduced)
        self.out_shape = [_sds(r.shape, r.dtype) for r in reduced]
        self.aliases = {a: a for a in range(n)}
        self.scratch = [pltpu.SemaphoreType.DMA((n,)), pltpu.SemaphoreType.DMA((n,))]

    def _copies(self, outs, sems, half_of):
        x, y, c = _mesh_pos()
        cps = []
        for a in range(self.n):
            h = outs[a].shape[0] // 2
            rows = outs[a].at[pl.ds(half_of(c) * h, h)]
            cps.append(_remote(rows, rows, sems[0].at[a], sems[1].at[a], (x, y, 1 - c)))
        return cps

    def start(self, ins, outs, sems):
        for cp in self._copies(outs, sems, lambda core: core):
            cp.start()

    def finish(self, ins, outs, sems):
        for cp in self._copies(outs, sems, lambda core: 1 - core):
            cp.wait_recv()
        for cp in self._copies(outs, sems, lambda core: core):
            cp.wait_send()


def allreduce_small(v):
    def body(v_ref, o_ref, buf, send_sems, recv_sems):
        x, y, c = _mesh_pos()
        me = 4 * x + 2 * y + c
        buf[me] = v_ref[...]
        flip = lambda p, f: 1 - p if f else p
        peers = [(flip(x, k & 4), flip(y, k & 2), flip(c, k & 1)) for k in range(1, N_DEV)]
        cps = []
        for k, peer in enumerate(peers):
            cp = _remote(v_ref, buf.at[me], send_sems.at[k], recv_sems.at[k], peer)
            cp.start()
            cps.append(cp)
        for k, peer in enumerate(peers):
            slot = buf.at[4 * peer[0] + 2 * peer[1] + peer[2]]
            _remote(slot, slot, send_sems.at[k], recv_sems.at[k], peer).wait_recv()
        for cp in cps:
            cp.wait_send()
        acc = buf[0]
        for i in range(1, N_DEV):
            acc = acc + buf[i]
        o_ref[...] = acc

    vm = pl.BlockSpec(memory_space=pltpu.VMEM)
    return pl.pallas_call(
        body, name="allreduce_small", out_shape=_sds(v.shape, v.dtype), in_specs=[vm], out_specs=vm,
        scratch_shapes=[pltpu.VMEM((N_DEV,) + v.shape, v.dtype),
                        pltpu.SemaphoreType.DMA((N_DEV - 1,)), pltpu.SemaphoreType.DMA((N_DEV - 1,))],
        compiler_params=pltpu.CompilerParams(has_side_effects=True),
    )(v)


def add_my_half(grad, theirs, c_idx, name):
    _, r, cols = grad.shape
    h = r // 2

    def body(c_ref, g_ref, t_ref, o_ref):
        o_ref[...] = (g_ref[...] + t_ref[...]).astype(BF16)

    slot = pl.BlockSpec((None, h, cols), lambda s, c: (s, 0, 0))
    grid_spec = pltpu.PrefetchScalarGridSpec(
        num_scalar_prefetch=1, grid=(N_SHARDS,),
        in_specs=[pl.BlockSpec((None, h, cols), lambda s, c: (s, c[0], 0)), slot], out_specs=slot)
    return pl.pallas_call(
        body, name=name, grid_spec=grid_spec, out_shape=_sds((N_SHARDS, h, cols), BF16),
        compiler_params=pltpu.CompilerParams(dimension_semantics=("arbitrary",), vmem_limit_bytes=VMEM_LIMIT_BYTES),
    )(c_idx, grad, theirs)


def sum_slots(received, mine, pos_idx, name):
    _, h, cols = received.shape

    def body(pos_ref, r_ref, m_ref, o_ref):
        acc = None
        for k in range(N_SHARDS):
            term = jnp.where(pos_ref[0] == k, m_ref[k], r_ref[k]).astype(F32)
            acc = term if acc is None else acc + term
        o_ref[...] = acc

    whole = pl.BlockSpec((N_SHARDS, h, cols), lambda i, pos: (0, 0, 0))
    grid_spec = pltpu.PrefetchScalarGridSpec(
        num_scalar_prefetch=1, grid=(1,), in_specs=[whole, whole],
        out_specs=pl.BlockSpec((h, cols), lambda i, pos: (pos[1], 0)))
    return pl.pallas_call(
        body, name=name, grid_spec=grid_spec, out_shape=_sds((2 * h, cols), F32),
        compiler_params=pltpu.CompilerParams(dimension_semantics=("arbitrary",), vmem_limit_bytes=VMEM_LIMIT_BYTES),
    )(pos_idx, received, mine)


def adamw(name, grads, w, m, v):
    n_layers, r, cols = w.shape
    tr = r // 2 if r % 16 == 0 else r
    bias1 = 1.0 - ADAM_B1 ** ADAM_STEP
    bias2 = 1.0 - ADAM_B2 ** ADAM_STEP

    def body(*refs):
        g_refs = refs[:n_layers]
        w_ref, m_ref, v_ref, go_ref, d_ref, mo_ref, vo_ref = refs[n_layers:]
        g = g_refs[0][...]
        for layer in range(1, n_layers):
            g = jnp.where(pl.program_id(0) == layer, g_refs[layer][...], g)
        m_new = ADAM_B1 * m_ref[...] + (1.0 - ADAM_B1) * g
        v_new = ADAM_B2 * v_ref[...] + (1.0 - ADAM_B2) * (g * g)
        m_hat = m_new / bias1
        v_hat = v_new / bias2
        go_ref[...] = g
        d_ref[...] = -ADAM_LR * (m_hat / (jnp.sqrt(v_hat) + ADAM_EPS) + ADAM_WD * w_ref[...])
        mo_ref[...] = m_new
        vo_ref[...] = v_new

    g_spec = pl.BlockSpec((tr, cols), lambda l, i: (i, 0))
    lay_spec = pl.BlockSpec((None, tr, cols), lambda l, i: (l, i, 0))
    shape = _sds((n_layers, r, cols), F32)
    return _pcall(
        body, name=name, grid=(n_layers, r // tr),
        in_specs=[g_spec] * n_layers + [lay_spec] * 3, out_specs=[lay_spec] * 4,
        out_shape=[shape] * 4, semantics=("arbitrary", "arbitrary"),
    )(*grads, w, m, v)


def kernel(x, norm_mix, norm_ffn, norm_final, pool_w_in, pool_w_group, pool_scale, pool_w_out, attn_w_qkv, attn_w_out, ffn_w_gate, ffn_w_up, ffn_w_down, loss_target, m_norm_mix, m_norm_ffn, m_norm_final, m_pool_w_in, m_pool_w_group, m_pool_scale, m_pool_w_out, m_attn_w_qkv, m_attn_w_out, m_ffn_w_gate, m_ffn_w_up, m_ffn_w_down, v_norm_mix, v_norm_ffn, v_norm_final, v_pool_w_in, v_pool_w_group, v_pool_scale, v_pool_w_out, v_attn_w_qkv, v_attn_w_out, v_ffn_w_gate, v_ffn_w_up, v_ffn_w_down):
    t_len = x.shape[1]
    x0 = x.reshape(t_len, D_MODEL)
    target = loss_target.reshape(t_len, D_MODEL)
    row = lambda a: a.reshape(1, D_MODEL)

    grp_rows = POOL_GROUP_DIM // N_SHARDS
    bf = lambda a: a.astype(BF16)
    gate_t, up_t = jnp.swapaxes(ffn_w_gate, 1, 2), jnp.swapaxes(ffn_w_up, 1, 2)
    pool_shards = [bf(pool_w_in[0]), bf(pool_w_group[0].reshape(4 * grp_rows, POOL_GROUP_DIM)), bf(pool_w_out[0])]
    ffn0_shards = [bf(gate_t[0]), bf(up_t[0]), bf(ffn_w_down[0])]
    late_shards = [bf(attn_w_qkv[0]), bf(attn_w_out[0]), bf(gate_t[1]), bf(up_t[1]), bf(ffn_w_down[1])]
    cos, sin = rope_tables(t_len)
    c_idx = lax.axis_index("c").astype(jnp.int32).reshape(1)
    pos_idx = jnp.stack([2 * lax.axis_index("x") + lax.axis_index("y"), lax.axis_index("c")]).astype(jnp.int32)
    chip_rows = lambda g: g.reshape(N_SHARDS, D_MODEL // N_SHARDS, D_MODEL)

    def add_halves(grads, theirs, names):
        return [add_my_half(g, t, c_idx, f"rs_add_{nm}") for g, t, nm in zip(grads, theirs, names)]

    def sum_chips(received, partials, names):
        return [sum_slots(r, p_, pos_idx, f"rs_sum_{nm}") for r, p_, nm in zip(received, partials, names)]

    g_pool = run_duty(GatherWeights(pool_shards), "gather_pool")
    w_in = g_pool[0].reshape(D_MODEL, D_MODEL)
    w_grp = g_pool[1].reshape(N_SHARDS, 4, grp_rows, POOL_GROUP_DIM).transpose(1, 0, 2, 3).reshape(
        4, POOL_GROUP_DIM, POOL_GROUP_DIM)
    w_out = g_pool[2].reshape(D_MODEL, D_MODEL)
    (h0, p, zr, z, x1), ffn0 = pool_fwd(x0, row(norm_mix[0]), w_in, w_grp, pool_scale, w_out,
                                        duty=GatherWeights(ffn0_shards))
    (h1, gate0, up0, act0, x2), late = ffn_fwd(x1, row(norm_ffn[0]), *ffn0, "ffn_fwd0",
                                               duty=GatherWeights(late_shards))
    w_qkv = pad_qkv_weight(late[0])
    w_ao = jnp.concatenate(pad_groups(late[1].reshape(D_MODEL, D_MODEL), 0), axis=0)
    ffn1 = late[2:5]
    h2, *qkv_parts = qkv_fwd(x2, row(norm_mix[1]), w_qkv, cos, sin)
    o_parts, lse_parts = [], []
    for gi in range(3):
        o_g, lse_g = attn_fwd(qkv_parts[gi], gi, f"attn_fwd_g{gi}")
        o_parts.append(o_g)
        lse_parts.append(lse_g)
    x3, merged, o_nat, lse_nat = attn_out_fwd(x2, o_parts, lse_parts, w_ao)
    h3, gate1, up1, act1, x4 = ffn_fwd(x3, row(norm_ffn[1]), *ffn1, "ffn_fwd1")
    dx4, d_norm_final, loss_local = final_fwd_bwd(x4, row(norm_final), target)

    dgate1, dup1, dx3, d_nf1 = ffn_bwd(dx4, x3, row(norm_ffn[1]), gate1, up1, *ffn1, "ffn_bwd1")
    g_gate1 = wgrad_row_sharded("wgrad_gate1", dgate1, h3)
    g_up1 = wgrad_row_sharded("wgrad_up1", dup1, h3)
    g_down1 = wgrad_row_sharded("wgrad_down1", act1, dx4)
    do_parts, c_parts = attn_out_bwd(dx3, w_ao, o_nat, lse_nat)
    g_ao = unpad_groups(jnp.split(wgrad_full("wgrad_attn_out", merged, dx3), 3, axis=0), 0)

    set1, names1 = [g_gate1, g_up1, g_down1, chip_rows(g_ao)], ["gate1", "up1", "down1", "attn_out"]
    attn_b = lambda gi, duty: attn_bwd(qkv_parts[gi], do_parts[gi], lse_parts[gi], c_parts[gi], gi,
                                       f"attn_bwd_g{gi}", duty=duty)
    dqkv0, theirs1 = attn_b(0, SiblingHalves(set1))
    partials1 = add_halves(set1, theirs1, names1)
    dqkv1, received1 = attn_b(1, ChipExchange(partials1))
    dqkv2, full1 = attn_b(2, SiblingShare(sum_chips(received1, partials1, names1)))
    dqkv, dx2, d_nm1 = qkv_bwd([dqkv0, dqkv1, dqkv2], w_qkv, dx3, x2, row(norm_mix[1]), cos, sin)
    g_qkv = unpad_qkv_grad(wgrad_col_sharded("wgrad_qkv", h2, dqkv))

    set2, names2 = [g_qkv], ["qkv"]
    (dgate0, dup0, dx1, d_nf0), theirs2 = ffn_bwd(dx2, x1, row(norm_ffn[0]), gate0, up0, *ffn0, "ffn_bwd0",
                                                        duty=SiblingHalves(set2))
    partials2 = add_halves(set2, theirs2, names2)
    g_gate0, received2 = wgrad_row_sharded("wgrad_gate0", dgate0, h1, duty=ChipExchange(partials2))

    duties = Together([SiblingShare(sum_chips(received2, partials2, names2)), SiblingHalves([g_gate0])])
    g_up0, outs = wgrad_row_sharded("wgrad_up0", dup0, h1, duty=duties)
    full2, theirs_g = duties.split(outs)
    partials_g = add_halves([g_gate0], theirs_g, ["gate0"])
    duties = Together([ChipExchange(partials_g), SiblingHalves([g_up0])])
    g_down0, outs = wgrad_row_sharded("wgrad_down0", act0, dx2, duty=duties)
    received_g, theirs_u = duties.split(outs)
    partials_u = add_halves([g_up0], theirs_u, ["up0"])
    duties = Together([ChipExchange(partials_u), SiblingHalves([g_down0]),
                       SiblingShare(sum_chips(received_g, partials_g, ["gate0"]))])
    g_out, outs = wgrad_full("wgrad_pool_out", z, dx1, duty=duties)
    received_u, theirs_d, full_g = duties.split(outs)
    partials_d = add_halves([g_down0], theirs_d, ["down0"])
    dzs, du, dx0, d_nm0, d_scale = pool_bwd(dx1, x0, row(norm_mix[0]), w_in, w_grp, pool_scale, w_out, zr)
    duties = Together([ChipExchange(partials_d), SiblingShare(sum_chips(received_u, partials_u, ["up0"]))])
    g_grp, outs = wgrad_pool_group("wgrad_pool_group", p, dzs, duty=duties)
    received_d, full_u = duties.split(outs)
    g_in, full_d = wgrad_full("wgrad_pool_in", h0, du,
                              duty=SiblingShare(sum_chips(received_d, partials_d, ["down0"])))
    full3 = [full_g[0], full_u[0], full_d[0]]

    set4 = [chip_rows(g_in), g_grp.reshape(N_SHARDS, 4 * grp_rows, POOL_GROUP_DIM), chip_rows(g_out)]
    names4 = ["pool_in", "pool_group", "pool_out"]
    partials4 = add_halves(set4, run_duty(SiblingHalves(set4), "rs_halves_pool"), names4)
    received4 = run_duty(ChipExchange(partials4), "rs_exchange_pool")
    full4 = run_duty(SiblingShare(sum_chips(received4, partials4, names4)), "rs_share_pool")
    full = [full4[0], full4[1], full4[2], full2[0], full1[3],
            full3[0], full1[0], full3[1], full1[1], full3[2], full1[2]]

    zero_row = jnp.zeros((1, D_MODEL), F32)
    small = jnp.concatenate([d_nm0, d_nm1, d_nf0, d_nf1, d_norm_final, d_scale,
                             jnp.broadcast_to(loss_local, (1, D_MODEL)), zero_row], axis=0)
    small = allreduce_small(small)
    loss = small[6, 0]

    pack = lambda a, b, c, d: jnp.concatenate([a, b, row(c), d, zero_row, zero_row], axis=0)[None]
    sg, sd, sm, sv = adamw("adamw_small", [small],
                           pack(norm_mix, norm_ffn, norm_final, pool_scale),
                           pack(m_norm_mix, m_norm_ffn, m_norm_final, m_pool_scale),
                           pack(v_norm_mix, v_norm_ffn, v_norm_final, v_pool_scale))
    unpack = lambda a: (a[0, 0:2], a[0, 2:4], a[0, 4], a[0, 5:6])

    def update(name, grads, w, m, v, transposed=False):
        if transposed:
            w, m, v = (jnp.swapaxes(a, 1, 2) for a in (w, m, v))
        n_layers = len(grads)
        shp = (n_layers,) + grads[0].shape
        outs = [o.reshape(w.shape) for o in adamw(name, grads, w.reshape(shp), m.reshape(shp), v.reshape(shp))]
        return [jnp.swapaxes(o, 1, 2) for o in outs] if transposed else outs

    big = [
        update("adamw_pool_in", [full[0]], pool_w_in, m_pool_w_in, v_pool_w_in),
        update("adamw_pool_group", [full[1]], pool_w_group, m_pool_w_group, v_pool_w_group),
        update("adamw_pool_out", [full[2]], pool_w_out, m_pool_w_out, v_pool_w_out),
        update("adamw_qkv", [full[3]], attn_w_qkv, m_attn_w_qkv, v_attn_w_qkv),
        update("adamw_attn_out", [full[4]], attn_w_out, m_attn_w_out, v_attn_w_out),
        update("adamw_gate", [full[5], full[6]], ffn_w_gate, m_ffn_w_gate, v_ffn_w_gate, transposed=True),
        update("adamw_up", [full[7], full[8]], ffn_w_up, m_ffn_w_up, v_ffn_w_up, transposed=True),
        update("adamw_down", [full[9], full[10]], ffn_w_down, m_ffn_w_down, v_ffn_w_down),
    ]

    def leaves(k, small_vals):
        nm, nf, nfin, psc = unpack(small_vals)
        return [nm, nf, nfin, big[0][k], big[1][k], psc, big[2][k], big[3][k], big[4][k],
                big[5][k], big[6][k], big[7][k]]

    grad_x = dx0.reshape(x.shape)
    return (loss, grad_x, *leaves(0, sg), *leaves(1, sd), *leaves(2, sm), *leaves(3, sv))
```

```python
import math

import jax
import jax.numpy as jnp
from jax import lax
from jax.experimental import pallas as pl
from jax.experimental.pallas import tpu as pltpu

F32 = jnp.float32
BF16 = jnp.bfloat16

D_MODEL = 1024
N_SHARDS = 4
N_DEV = 8
D_FF = 2816
FF_SHARD = D_FF // N_SHARDS
HEAD_DIM = 64
QKV_SHARD = 3 * D_MODEL // N_SHARDS
POOL_WINDOWS = (2, 4, 8, 16)
POOL_GROUP_DIM = 256
POOL_HALO = 16
ATTN_W = 128
GROUP_LANES = (0, 384, 704, 1024)
GROUP_HEADS = (6, 5, 5)
GROUP_DIL = (1, 4, 16)
ROPE_THETA = 10000.0
EPS = 1e-6
NEG_INF = -1e30
LANE = 128
VMEM_LIMIT_BYTES = 60 * 1024 * 1024

ADAM_LR = 0.001
ADAM_B1 = 0.9
ADAM_B2 = 0.999
ADAM_EPS = 1e-08
ADAM_WD = 0.01
ADAM_STEP = 10

NT_DIMS = (((1,), (1,)), ((), ()))
TN_DIMS = (((0,), (0,)), ((), ()))
MESH = pl.DeviceIdType.MESH


_ANY = pl.BlockSpec(memory_space=pl.ANY)


def _pcall(body, *, name, out_shape, grid=None, in_specs=None, out_specs=None, scratch_shapes=(),
           semantics=None, duty=None):
    kw = {}
    if in_specs is not None and duty is None:
        kw["in_specs"] = in_specs
    if out_specs is not None and duty is None:
        kw["out_specs"] = out_specs
    if grid is not None:
        kw["grid"] = grid
    params = dict(dimension_semantics=semantics, vmem_limit_bytes=VMEM_LIMIT_BYTES)
    if duty is None:
        return pl.pallas_call(body, name=name, out_shape=out_shape, scratch_shapes=list(scratch_shapes),
                              compiler_params=pltpu.CompilerParams(**params), **kw)

    single = not isinstance(out_shape, (list, tuple))
    c_out_shape = [out_shape] if single else list(out_shape)
    c_out_specs = [out_specs] if single else list(out_specs)
    n_in, n_out, n_scr = len(in_specs), len(c_out_shape), len(scratch_shapes)
    d_in, d_out = len(duty.ins), len(duty.out_shape)
    total = math.prod(grid)
    mid_step = (5 * total) // 6

    def wrapped(*refs):
        c_in, d_ins = refs[:n_in], refs[n_in:n_in + d_in]
        o0 = n_in + d_in
        c_outs, d_outs = refs[o0:o0 + n_out], refs[o0 + n_out:o0 + n_out + d_out]
        s0 = o0 + n_out + d_out
        c_scr, d_sems = refs[s0:s0 + n_scr], refs[s0 + n_scr:]
        step = pl.program_id(0)
        for ax in range(1, len(grid)):
            step = step * grid[ax] + pl.program_id(ax)

        @pl.when(step == 0)
        def _():
            duty.start(d_ins, d_outs, d_sems)

        body(*c_in, *c_outs, *c_scr)

        @pl.when(step == mid_step)
        def _():
            duty.mid(d_ins, d_outs, d_sems)

        @pl.when(step == total - 1)
        def _():
            duty.finish(d_ins, d_outs, d_sems)

    call = pl.pallas_call(
        wrapped, name=name, grid=grid,
        in_specs=list(in_specs) + [_ANY] * d_in, out_specs=c_out_specs + [_ANY] * d_out,
        out_shape=c_out_shape + list(duty.out_shape),
        scratch_shapes=list(scratch_shapes) + list(duty.scratch),
        input_output_aliases={n_in + i: n_out + o for i, o in duty.aliases.items()},
        compiler_params=pltpu.CompilerParams(has_side_effects=True, **params))

    def run(*args):
        outs = call(*args, *duty.ins)
        c = outs[:n_out]
        return (c[0] if single else list(c)), list(outs[n_out:])

    return run


def _sds(shape, dtype):
    return jax.ShapeDtypeStruct(tuple(shape), dtype)


def _dot(a, b):
    return jnp.dot(a, b, preferred_element_type=F32)


def _dot_nt(a, b):
    return lax.dot_general(a, b, NT_DIMS, preferred_element_type=F32)


def _dot_tn(a, b):
    return lax.dot_general(a, b, TN_DIMS, preferred_element_type=F32)


def _rms_fwd(x, g):
    r = lax.rsqrt(jnp.mean(x * x, axis=-1, keepdims=True) + EPS)
    return x * r * g


def _rms_bwd(dh, x, g):
    r = lax.rsqrt(jnp.mean(x * x, axis=-1, keepdims=True) + EPS)
    xh = x * r
    dg = jnp.sum(dh * xh, axis=0, keepdims=True)
    dxh = dh * g
    dx = r * (dxh - xh * jnp.mean(dxh * xh, axis=-1, keepdims=True))
    return dx, dg


def _sigmoid(x):
    return 0.5 * jnp.tanh(0.5 * x) + 0.5


def _tile_rows(t):
    return min(512, t)


def _sub_tiles(tm, n_sub=2):
    rows = tm // n_sub
    return [pl.ds(i * rows, rows) for i in range(n_sub)]


def _wgrad_rows(t):
    return min(2048, t)


def pool_fwd(x, g_row, w_in, w_grp, scale, w_out, duty=None):
    t_len = x.shape[0]
    tm = _tile_rows(t_len)

    def body(x_ref, g_ref, win_ref, wgrp_ref, scale_ref, wout_ref,
             h_ref, p_ref, zr_ref, z_ref, xo_ref, ubuf):
        t = pl.program_id(0)

        @pl.when(t == 0)
        def _():
            ubuf[pl.ds(0, POOL_HALO), :] = jnp.zeros((POOL_HALO, D_MODEL), F32)

        x_t = x_ref[...]
        h = _rms_fwd(x_t, g_ref[...]).astype(BF16)
        h_ref[...] = h
        ubuf[pl.ds(POOL_HALO, tm), :] = _dot(h, win_ref[...])
        row = t * tm + lax.broadcasted_iota(jnp.int32, (tm, 1), 0)
        for gi, w in enumerate(POOL_WINDOWS):
            cols = pl.ds(gi * POOL_GROUP_DIM, POOL_GROUP_DIM)
            u_g = ubuf[pl.ds(POOL_HALO, tm), cols]
            acc = u_g
            for j in range(1, w):
                acc = acc + ubuf[pl.ds(POOL_HALO - j, tm), cols]
            inv_cnt = 1.0 / jnp.minimum(row + 1, w).astype(F32)
            p_g = (acc * inv_cnt - u_g).astype(BF16)
            p_ref[:, cols] = p_g
            z_g = _dot(p_g, wgrp_ref[gi])
            zr_ref[:, cols] = z_g.astype(BF16)
            z_ref[:, cols] = (z_g * scale_ref[:, cols]).astype(BF16)
        ubuf[pl.ds(0, POOL_HALO), :] = ubuf[pl.ds(tm, POOL_HALO), :]
        xo_ref[...] = x_t + _dot(z_ref[...], wout_ref[...])

    row_spec = pl.BlockSpec((tm, D_MODEL), lambda t: (t, 0))
    full2 = lambda shape: pl.BlockSpec(shape, lambda t: (0,) * len(shape))
    return _pcall(
        body, name="pool_fwd", grid=(t_len // tm,),
        in_specs=[row_spec, full2((1, D_MODEL)), full2((D_MODEL, D_MODEL)),
                  full2((4, POOL_GROUP_DIM, POOL_GROUP_DIM)), full2((1, D_MODEL)), full2((D_MODEL, D_MODEL))],
        out_specs=[row_spec] * 5,
        out_shape=[_sds((t_len, D_MODEL), BF16)] * 4 + [_sds((t_len, D_MODEL), F32)],
        scratch_shapes=[pltpu.VMEM((tm + POOL_HALO, D_MODEL), F32)],
        semantics=("arbitrary",), duty=duty,
    )(x, g_row, w_in, w_grp, scale, w_out)


def ffn_fwd(x, g_row, w_gate_t, w_up_t, w_down, name, duty=None):
    t_len = x.shape[0]
    tm = min(1024, t_len)

    def body(x_ref, g_ref, wg_ref, wu_ref, wd_ref, h_ref, go_ref, uo_ref, ao_ref, xo_ref, hbuf, acc):
        s = pl.program_id(1)

        @pl.when(s == 0)
        def _():
            h = _rms_fwd(x_ref[...], g_ref[...]).astype(BF16)
            hbuf[...] = h
            h_ref[...] = h
            acc[...] = jnp.zeros_like(acc)

        for rows in _sub_tiles(tm, 1):
            h = hbuf[rows, :]
            gate = _dot_nt(h, wg_ref[...])
            up = _dot_nt(h, wu_ref[...])
            go_ref[rows, :] = gate.astype(BF16)
            uo_ref[rows, :] = up.astype(BF16)
            act = (gate * _sigmoid(gate) * up).astype(BF16)
            ao_ref[rows, :] = act
            acc[rows, :] += _dot(act, wd_ref[...])

        @pl.when(s == N_SHARDS - 1)
        def _():
            xo_ref[...] = x_ref[...] + acc[...]

    row_spec = pl.BlockSpec((tm, D_MODEL), lambda t, s: (t, 0))
    row_w = pl.BlockSpec((None, FF_SHARD, D_MODEL), lambda t, s: (s, 0, 0))
    act_spec = pl.BlockSpec((None, tm, FF_SHARD), lambda t, s: (s, t, 0))
    return _pcall(
        body, name=name, grid=(t_len // tm, N_SHARDS),
        in_specs=[row_spec, pl.BlockSpec((1, D_MODEL), lambda t, s: (0, 0)), row_w, row_w, row_w],
        out_specs=[row_spec, act_spec, act_spec, act_spec, row_spec],
        out_shape=[_sds((t_len, D_MODEL), BF16)] + [_sds((N_SHARDS, t_len, FF_SHARD), BF16)] * 3
                  + [_sds((t_len, D_MODEL), F32)],
        scratch_shapes=[pltpu.VMEM((tm, D_MODEL), BF16), pltpu.VMEM((tm, D_MODEL), F32)],
        semantics=("arbitrary", "arbitrary"), duty=duty,
    )(x, g_row, w_gate_t, w_up_t, w_down)


def ffn_bwd(dxo, x, g_row, gate, up, w_gate_t, w_up_t, w_down, name, duty=None):
    t_len = x.shape[0]
    tm = _tile_rows(t_len)

    def body(dxo_ref, x_ref, g_ref, gate_ref, up_ref, wg_ref, wu_ref, wd_ref,
             dg_ref, du_ref, dx_ref, dn_ref, dxb, dh):
        t = pl.program_id(0)
        s = pl.program_id(1)

        @pl.when(s == 0)
        def _():
            dxb[...] = dxo_ref[...].astype(BF16)
            dh[...] = jnp.zeros_like(dh)

        @pl.when(jnp.logical_and(s == 0, t == 0))
        def _():
            dn_ref[...] = jnp.zeros_like(dn_ref)

        sub_tiles = _sub_tiles(tm)
        dacts = [_dot_nt(dxb[rows, :], wd_ref[...]) for rows in sub_tiles]
        for rows, dact in zip(sub_tiles, dacts):
            gv = gate_ref[rows, :].astype(F32)
            uv = up_ref[rows, :].astype(F32)
            sg = _sigmoid(gv)
            dgv = (dact * uv * (sg * (1.0 + gv * (1.0 - sg)))).astype(BF16)
            duv = (dact * (gv * sg)).astype(BF16)
            dg_ref[rows, :] = dgv
            du_ref[rows, :] = duv
            dh[rows, :] += _dot(dgv, wg_ref[...]) + _dot(duv, wu_ref[...])

        @pl.when(s == N_SHARDS - 1)
        def _():
            dx, dn = _rms_bwd(dh[...], x_ref[...], g_ref[...])
            dx_ref[...] = dxo_ref[...] + dx
            dn_ref[...] += dn

    row_spec = pl.BlockSpec((tm, D_MODEL), lambda t, s: (t, 0))
    vec_spec = pl.BlockSpec((1, D_MODEL), lambda t, s: (0, 0))
    row_w = pl.BlockSpec((None, FF_SHARD, D_MODEL), lambda t, s: (s, 0, 0))
    act_spec = pl.BlockSpec((None, tm, FF_SHARD), lambda t, s: (s, t, 0))
    act_shape = _sds((N_SHARDS, t_len, FF_SHARD), BF16)
    return _pcall(
        body, name=name, grid=(t_len // tm, N_SHARDS),
        in_specs=[row_spec, row_spec, vec_spec, act_spec, act_spec, row_w, row_w, row_w],
        out_specs=[act_spec, act_spec, row_spec, vec_spec],
        out_shape=[act_shape, act_shape, _sds((t_len, D_MODEL), F32), _sds((1, D_MODEL), F32)],
        scratch_shapes=[pltpu.VMEM((tm, D_MODEL), BF16), pltpu.VMEM((tm, D_MODEL), F32)],
        semantics=("arbitrary", "arbitrary"), duty=duty,
    )(dxo, x, g_row, gate, up, w_gate_t, w_up_t, w_down)


def tn_matmul(name, a, b, a_spec, b_spec, out_shape, out_spec, grid, duty=None):
    def body(a_ref, b_ref, o_ref):
        @pl.when(pl.program_id(len(grid) - 1) == 0)
        def _():
            o_ref[...] = jnp.zeros_like(o_ref)

        res = _dot_tn(a_ref[...].astype(BF16), b_ref[...].astype(BF16))
        o_ref[...] += res.reshape(o_ref.shape)

    return _pcall(body, name=name, grid=grid, in_specs=[a_spec, b_spec], out_specs=out_spec,
                  out_shape=out_shape, semantics=("arbitrary",) * len(grid), duty=duty)(a, b)


def wgrad_full(name, a, b, duty=None):
    t_len, k = a.shape
    n = b.shape[1]
    tt = _wgrad_rows(t_len)
    return tn_matmul(name, a, b,
                     pl.BlockSpec((tt, k), lambda t: (t, 0)), pl.BlockSpec((tt, n), lambda t: (t, 0)),
                     _sds((k, n), F32), pl.BlockSpec((k, n), lambda t: (0, 0)), (t_len // tt,), duty)


def wgrad_col_sharded(name, a, b_sh, duty=None):
    t_len, k = a.shape
    n_sh, _, n = b_sh.shape
    tt = _wgrad_rows(t_len)
    return tn_matmul(name, a, b_sh,
                     pl.BlockSpec((tt, k), lambda s, t: (t, 0)), pl.BlockSpec((None, tt, n), lambda s, t: (s, t, 0)),
                     _sds((n_sh, k, n), F32), pl.BlockSpec((None, k, n), lambda s, t: (s, 0, 0)),
                     (n_sh, t_len // tt), duty)


def wgrad_row_sharded(name, a_sh, b, duty=None):
    t_len, n = b.shape
    n_sh, _, k = a_sh.shape
    tt = _wgrad_rows(t_len)
    return tn_matmul(name, a_sh, b,
                     pl.BlockSpec((None, tt, k), lambda s, t: (s, t, 0)), pl.BlockSpec((tt, n), lambda s, t: (t, 0)),
                     _sds((n_sh, k, n), F32), pl.BlockSpec((None, k, n), lambda s, t: (s, 0, 0)),
                     (n_sh, t_len // tt), duty)


def wgrad_pool_group(name, p, dzs, duty=None):
    t_len = p.shape[0]
    tt = _wgrad_rows(t_len)
    gd = POOL_GROUP_DIM
    rows = gd // N_SHARDS
    return tn_matmul(name, p, dzs,
                     pl.BlockSpec((tt, gd), lambda g, t: (t, g)), pl.BlockSpec((tt, gd), lambda g, t: (t, g)),
                     _sds((N_SHARDS, 4, rows, gd), F32),
                     pl.BlockSpec((N_SHARDS, None, rows, gd), lambda g, t: (0, g, 0, 0)),
                     (4, t_len // tt), duty)


PAD_LANES = 384
QKV_PAD = 3 * PAD_LANES
N_SLABS = QKV_PAD // LANE
GROUP_REAL = tuple(GROUP_LANES[g + 1] - GROUP_LANES[g] for g in range(3))
Q_BLOCK = 512


def pad_groups(w, axis):
    parts = []
    for g in range(3):
        blk = lax.slice_in_dim(w, GROUP_LANES[g], GROUP_LANES[g + 1], axis=axis)
        pad = [(0, 0)] * w.ndim
        pad[axis] = (0, PAD_LANES - GROUP_REAL[g])
        parts.append(jnp.pad(blk, pad))
    return parts


def unpad_groups(parts, axis):
    return jnp.concatenate([lax.slice_in_dim(p, 0, GROUP_REAL[g], axis=axis) for g, p in enumerate(parts)],
                           axis=axis)


def pad_qkv_weight(w_qkv_sh):
    w = jnp.transpose(w_qkv_sh, (1, 0, 2)).reshape(D_MODEL, 3 * D_MODEL)
    q, k, v = (pad_groups(w[:, i * D_MODEL:(i + 1) * D_MODEL], 1) for i in range(3))
    return jnp.stack([jnp.concatenate([q[g], k[g], v[g]], axis=1) for g in range(3)])


def unpad_qkv_grad(g_pad):
    cols = [unpad_groups([g_pad[g][:, i * PAD_LANES:(i + 1) * PAD_LANES] for g in range(3)], 1) for i in range(3)]
    w = jnp.concatenate(cols, axis=1)
    return jnp.transpose(w.reshape(D_MODEL, N_SHARDS, QKV_SHARD), (1, 0, 2))


def rope_tables(t_len):
    inv_freq = 1.0 / (ROPE_THETA ** (jnp.arange(0, HEAD_DIM, 2, dtype=F32) / HEAD_DIM))
    ang = jnp.arange(t_len, dtype=F32)[:, None] * inv_freq[None, :]
    cos_h, sin_h = lax.optimization_barrier((jnp.cos(ang), jnp.sin(ang)))
    reps = (1, 2 * LANE // HEAD_DIM)
    return jnp.tile(cos_h, reps), jnp.tile(sin_h, reps)


def _rot_half(v):
    n = v.shape[1]
    lane = lax.broadcasted_iota(jnp.int32, v.shape, 1)
    return jnp.where(lane % HEAD_DIM < HEAD_DIM // 2,
                     -pltpu.roll(v, n - HEAD_DIM // 2, 1), pltpu.roll(v, HEAD_DIM // 2, 1))


def _lane_cols(j):
    return slice(j * LANE, (j + 1) * LANE)


def _to_residue_major(slab, j_src, dst_ref, j_dst, dil, rows):
    for r in range(dil):
        dst_ref[r, :, _lane_cols(j_dst)] = slab[j_src, pl.ds(r, rows // dil, stride=dil), :].astype(dst_ref.dtype)


def _to_natural(src_ref, j_src, slab, j_dst, dil, rows):
    for r in range(dil):
        slab[j_dst, pl.ds(r, rows // dil, stride=dil), :] = src_ref[r, :, _lane_cols(j_src)].astype(F32)


def qkv_fwd(x, g_row, w_pad, cos, sin):
    t_len = x.shape[0]
    tm = _tile_rows(t_len)

    def body(x_ref, g_ref, w_ref, cos_ref, sin_ref, h_ref, o1_ref, o4_ref, o16_ref, slabs):
        h = _rms_fwd(x_ref[...], g_ref[...]).astype(BF16)
        h_ref[...] = h
        accs = [_dot(h, w_ref[gi]) for gi in range(3)]
        cos_t = cos_ref[...]
        sin_t = sin_ref[...]
        for gi, (dil, o_ref) in enumerate(zip(GROUP_DIL, (o1_ref, o4_ref, o16_ref))):
            slab = slabs.at[gi]
            for j in range(N_SLABS):
                a = accs[gi][:, _lane_cols(j)]
                if j < 6:
                    a = a * cos_t + _rot_half(a) * sin_t
                if j < 3:
                    a = a * (HEAD_DIM ** -0.5)
                if dil == 1:
                    o_ref[0, :, _lane_cols(j)] = a.astype(BF16)
                else:
                    slab[j] = a
                    _to_residue_major(slab, j, o_ref, j, dil, tm)

    row_spec = pl.BlockSpec((tm, D_MODEL), lambda t: (t, 0))
    tab_spec = pl.BlockSpec((tm, LANE), lambda t: (t, 0))
    out_specs = [row_spec] + [pl.BlockSpec((d, tm // d, QKV_PAD), lambda t: (0, t, 0)) for d in GROUP_DIL]
    out_shape = [_sds((t_len, D_MODEL), BF16)] + [_sds((d, t_len // d, QKV_PAD), BF16) for d in GROUP_DIL]
    return _pcall(
        body, name="qkv_fwd", grid=(t_len // tm,),
        in_specs=[row_spec, pl.BlockSpec((1, D_MODEL), lambda t: (0, 0)),
                  pl.BlockSpec((3, D_MODEL, QKV_PAD), lambda t: (0, 0, 0)), tab_spec, tab_spec],
        out_specs=out_specs, out_shape=out_shape,
        scratch_shapes=[pltpu.VMEM((3, N_SLABS, tm, LANE), F32)],
        semantics=("arbitrary",),
    )(x, g_row, w_pad, cos, sin)


def _band_mask(n):
    qi = lax.broadcasted_iota(jnp.int32, (ATTN_W, 2 * ATTN_W), 0)
    kj = lax.broadcasted_iota(jnp.int32, (ATTN_W, 2 * ATTN_W), 1)
    dist = ATTN_W + qi - kj
    return (dist >= 0) & (dist <= ATTN_W) & ((kj >= ATTN_W) | (n > 0))


def _half_masks():
    lane = lax.broadcasted_iota(jnp.int32, (1, LANE), 1)
    return [lane < HEAD_DIM, lane >= HEAD_DIM]


def _live_halves(gi, j):
    hms = _half_masks()
    return hms if (gi == 0 or j < 2) else hms[:1]


def attn_fwd(qkv_g, gi, name):
    dil, l_len, _ = qkv_g.shape
    qb = min(Q_BLOCK, l_len)
    nsub = qb // ATTN_W

    def body(q_ref, kc_ref, kp_ref, vc_ref, vp_ref, o_ref, lse_ref, kbuf, vbuf):
        n = pl.program_id(1)
        kbuf[pl.ds(0, ATTN_W), :] = kp_ref[...]
        kbuf[pl.ds(ATTN_W, qb), :] = kc_ref[...]
        vbuf[pl.ds(0, ATTN_W), :] = vp_ref[...]
        vbuf[pl.ds(ATTN_W, qb), :] = vc_ref[...]

        def sub(b, carry):
            r0 = pl.multiple_of(b * ATTN_W, ATTN_W)
            mask = _band_mask(n + b)
            krows = pl.ds(r0, 2 * ATTN_W)
            scores = []
            for j in range(3):
                q = q_ref[pl.ds(r0, ATTN_W), _lane_cols(j)]
                for hm in _live_halves(gi, j):
                    scores.append(_dot_nt(jnp.where(hm, q, jnp.zeros_like(q)), kbuf[krows, _lane_cols(j)]))
            scores = iter(scores)
            for j in range(3):
                cols = _lane_cols(j)
                v = vbuf[krows, cols]
                o = jnp.zeros((ATTN_W, LANE), F32)
                lse = jnp.zeros((ATTN_W, LANE), F32)
                for hm in _live_halves(gi, j):
                    s = jnp.where(mask, next(scores), NEG_INF)
                    m = jnp.max(s, axis=-1, keepdims=True)
                    e = jnp.exp(s - m)
                    den = jnp.sum(e, axis=-1, keepdims=True)
                    p = (e * (1.0 / den)).astype(BF16)
                    o = jnp.where(hm, _dot(p, v), o)
                    lse = jnp.where(hm, m + jnp.log(den), lse)
                o_ref[pl.ds(r0, ATTN_W), cols] = o.astype(BF16)
                lse_ref[pl.ds(r0, ATTN_W), cols] = lse
            return carry

        lax.fori_loop(0, nsub, sub, 0)

    cur = lambda c: pl.BlockSpec((None, qb, PAD_LANES), lambda r, n: (r, n, c))
    prev = lambda c: pl.BlockSpec((None, ATTN_W, PAD_LANES), lambda r, n: (r, jnp.maximum(n * nsub - 1, 0), c))
    out_spec = pl.BlockSpec((None, qb, PAD_LANES), lambda r, n: (r, n, 0))
    return _pcall(
        body, name=name, grid=(dil, l_len // qb),
        in_specs=[cur(0), cur(1), prev(1), cur(2), prev(2)],
        out_specs=[out_spec, out_spec],
        out_shape=[_sds((dil, l_len, PAD_LANES), BF16), _sds((dil, l_len, PAD_LANES), F32)],
        scratch_shapes=[pltpu.VMEM((qb + ATTN_W, PAD_LANES), BF16), pltpu.VMEM((qb + ATTN_W, PAD_LANES), BF16)],
        semantics=("arbitrary", "arbitrary"),
    )(qkv_g, qkv_g, qkv_g, qkv_g, qkv_g)


def _group_stats(lse):
    lane = lax.broadcasted_iota(jnp.int32, (1, QKV_PAD), 1)
    gmask = [(lane >= g * PAD_LANES) & (lane < g * PAD_LANES + GROUP_REAL[g]) for g in range(3)]
    lses, glse = [], []
    for g in range(3):
        mx = jnp.max(jnp.where(gmask[g], lse, -jnp.inf), axis=-1, keepdims=True)
        sm = jnp.sum(jnp.where(gmask[g], jnp.exp(lse - mx), 0.0), axis=-1, keepdims=True) / HEAD_DIM
        full = mx + jnp.log(sm)
        lses.append(full)
        glse.append(full - math.log(GROUP_HEADS[g]))
    top = jnp.maximum(jnp.maximum(glse[0], glse[1]), glse[2])
    ex = [jnp.exp(v - top) for v in glse]
    tot = ex[0] + ex[1] + ex[2]
    alpha = [v / tot for v in ex]
    pick = lambda vals: jnp.where(lane < PAD_LANES, vals[0], jnp.where(lane < 2 * PAD_LANES, vals[1], vals[2]))
    return gmask, alpha, pick([3.0 * a for a in alpha]), lse - pick(lses), pick


def attn_out_fwd(x, o_parts, lse_parts, w_out_pad):
    t_len = x.shape[0]
    tm = _tile_rows(t_len)

    def body(x_ref, o1, o4, o16, l1, l4, l16, w_ref, xo_ref, mg_ref, o_ref, lse_ref, o_slab, l_slab):
        for gi, (dil, og, lg) in enumerate(zip(GROUP_DIL, (o1, o4, o16), (l1, l4, l16))):
            for j in range(3):
                _to_natural(og, j, o_slab, 3 * gi + j, dil, tm)
                _to_natural(lg, j, l_slab, 3 * gi + j, dil, tm)
        o = jnp.concatenate([o_slab[j] for j in range(N_SLABS)], axis=1)
        lse = jnp.concatenate([l_slab[j] for j in range(N_SLABS)], axis=1)
        o_ref[...] = o.astype(BF16)
        lse_ref[...] = lse
        _, _, scale, _, _ = _group_stats(lse)
        merged = (o * scale).astype(BF16)
        mg_ref[...] = merged
        xo_ref[...] = x_ref[...] + _dot(merged, w_ref[...])

    row_spec = pl.BlockSpec((tm, D_MODEL), lambda t: (t, 0))
    pad_spec = pl.BlockSpec((tm, QKV_PAD), lambda t: (t, 0))
    part_specs = [pl.BlockSpec((d, tm // d, PAD_LANES), lambda t: (0, t, 0)) for d in GROUP_DIL]
    return _pcall(
        body, name="attn_out_fwd", grid=(t_len // tm,),
        in_specs=[row_spec] + part_specs + part_specs + [pl.BlockSpec((QKV_PAD, D_MODEL), lambda t: (0, 0))],
        out_specs=[row_spec, pad_spec, pad_spec, pad_spec],
        out_shape=[_sds((t_len, D_MODEL), F32), _sds((t_len, QKV_PAD), BF16),
                   _sds((t_len, QKV_PAD), BF16), _sds((t_len, QKV_PAD), F32)],
        scratch_shapes=[pltpu.VMEM((N_SLABS, tm, LANE), F32), pltpu.VMEM((N_SLABS, tm, LANE), F32)],
        semantics=("arbitrary",),
    )(x, *o_parts, *lse_parts, w_out_pad)


def attn_out_bwd(dxo, w_out_pad, o, lse, duty=None):
    t_len = dxo.shape[0]
    tm = _tile_rows(t_len)

    def body(dx_ref, w_ref, o_ref, lse_ref, d1, d4, d16, c1, c4, c16, slab):
        dmerged = _dot_nt(dx_ref[...].astype(BF16), w_ref[...])
        o_t = o_ref[...].astype(F32)
        gmask, alpha, scale, lse_rel, pick = _group_stats(lse_ref[...])
        e = dmerged * o_t
        dalpha = [3.0 * jnp.sum(jnp.where(gmask[g], e, 0.0), axis=-1, keepdims=True) for g in range(3)]
        mean_da = alpha[0] * dalpha[0] + alpha[1] * dalpha[1] + alpha[2] * dalpha[2]
        dglse = [alpha[g] * (dalpha[g] - mean_da) for g in range(3)]
        dlse = pick(dglse) * jnp.exp(lse_rel)
        do = dmerged * scale
        es = e * scale
        lane = lax.broadcasted_iota(jnp.int32, (1, LANE), 1)
        first = lane < HEAD_DIM
        for j in range(N_SLABS):
            slab[j] = do[:, _lane_cols(j)]
        for gi, (dil, dg) in enumerate(zip(GROUP_DIL, (d1, d4, d16))):
            for j in range(3):
                _to_residue_major(slab, 3 * gi + j, dg, j, dil, tm)
        for j in range(N_SLABS):
            blk = es[:, _lane_cols(j)]
            s0 = jnp.sum(jnp.where(first, blk, 0.0), axis=-1, keepdims=True)
            s1 = jnp.sum(jnp.where(first, 0.0, blk), axis=-1, keepdims=True)
            slab[j] = jnp.where(first, s0, s1) - dlse[:, _lane_cols(j)]
        for gi, (dil, cg) in enumerate(zip(GROUP_DIL, (c1, c4, c16))):
            for j in range(3):
                _to_residue_major(slab, 3 * gi + j, cg, j, dil, tm)

    row_spec = pl.BlockSpec((tm, D_MODEL), lambda t: (t, 0))
    pad_spec = pl.BlockSpec((tm, QKV_PAD), lambda t: (t, 0))
    part_specs = [pl.BlockSpec((d, tm // d, PAD_LANES), lambda t: (0, t, 0)) for d in GROUP_DIL]
    shapes = lambda dt: [_sds((d, t_len // d, PAD_LANES), dt) for d in GROUP_DIL]
    outs = _pcall(
        body, name="attn_out_bwd", grid=(t_len // tm,),
        in_specs=[row_spec, pl.BlockSpec((QKV_PAD, D_MODEL), lambda t: (0, 0)), pad_spec, pad_spec],
        out_specs=part_specs + part_specs,
        out_shape=shapes(BF16) + shapes(F32),
        scratch_shapes=[pltpu.VMEM((N_SLABS, tm, LANE), F32)],
        semantics=("arbitrary",), duty=duty,
    )(dxo, w_out_pad, o, lse)
    if duty is None:
        return outs[:3], outs[3:]
    return (outs[0][:3], outs[0][3:]), outs[1]


def attn_bwd(qkv_g, do_g, lse_g, c_g, gi, name, duty=None):
    dil, l_len, _ = qkv_g.shape
    qb = min(Q_BLOCK, l_len)
    nsub = qb // ATTN_W
    nsb = l_len // qb

    def body(q_ref, kc_ref, kp_ref, vc_ref, vp_ref, do_ref, lse_ref, c_ref,
             qn_ref, don_ref, lsen_ref, cn_ref, o_ref, kbuf, vbuf, dkbuf, dvbuf):
        n = pl.program_id(1)
        kbuf[pl.ds(0, ATTN_W), :] = kp_ref[...]
        kbuf[pl.ds(ATTN_W, qb), :] = kc_ref[...]
        vbuf[pl.ds(0, ATTN_W), :] = vp_ref[...]
        vbuf[pl.ds(ATTN_W, qb), :] = vc_ref[...]
        dkbuf[...] = jnp.zeros_like(dkbuf)
        dvbuf[...] = jnp.zeros_like(dvbuf)

        def block(q_of, do_of, lse_of, c_of, krows, mask, dq_rows):
            heads = []
            for j in range(3):
                cols = _lane_cols(j)
                q, do_t, k, v = q_of(cols), do_of(cols), kbuf[krows, cols], vbuf[krows, cols]
                for hm in _live_halves(gi, j):
                    qh = jnp.where(hm, q, jnp.zeros_like(q))
                    doh = jnp.where(hm, do_t, jnp.zeros_like(do_t))
                    heads.append((j, hm, qh, doh, _dot_nt(qh, k), _dot_nt(doh, v)))
            for j in range(3):
                cols = _lane_cols(j)
                k = kbuf[krows, cols]
                dq = jnp.zeros((ATTN_W, LANE), F32)
                dk = jnp.zeros((k.shape[0], LANE), F32)
                dv = jnp.zeros((k.shape[0], LANE), F32)
                for hj, hm, qh, doh, s, dp in heads:
                    if hj != j:
                        continue
                    lse_h = jnp.max(jnp.where(hm, lse_of(cols), -jnp.inf), axis=-1, keepdims=True)
                    c_h = jnp.max(jnp.where(hm, c_of(cols), -jnp.inf), axis=-1, keepdims=True)
                    p = jnp.exp(jnp.where(mask, s, NEG_INF) - lse_h)
                    ds = (p * (dp - c_h)).astype(BF16)
                    if dq_rows is not None:
                        dq = jnp.where(hm, _dot(ds, k), dq)
                    dk = dk + _dot_tn(ds, qh)
                    dv = dv + _dot_tn(p.astype(BF16), doh)
                if dq_rows is not None:
                    o_ref[dq_rows, cols] = dq.astype(BF16)
                dkbuf[krows, cols] += dk
                dvbuf[krows, cols] += dv

        def sub(b, carry):
            rows = pl.ds(pl.multiple_of(b * ATTN_W, ATTN_W), ATTN_W)
            krows = pl.ds(pl.multiple_of(b * ATTN_W, ATTN_W), 2 * ATTN_W)
            block(lambda c: q_ref[rows, c], lambda c: do_ref[rows, c], lambda c: lse_ref[rows, c],
                  lambda c: c_ref[rows, c], krows, _band_mask(n + b), rows)
            return carry

        lax.fori_loop(0, nsub, sub, 0)

        qi = lax.broadcasted_iota(jnp.int32, (ATTN_W, ATTN_W), 0)
        kj = lax.broadcasted_iota(jnp.int32, (ATTN_W, ATTN_W), 1)
        nmask = (qi <= kj) & (n < nsb - 1)
        block(lambda c: qn_ref[:, c], lambda c: don_ref[:, c], lambda c: lsen_ref[:, c],
              lambda c: cn_ref[:, c], pl.ds(qb, ATTN_W), nmask, None)
        o_ref[:, pl.ds(PAD_LANES, PAD_LANES)] = dkbuf[pl.ds(ATTN_W, qb), :].astype(BF16)
        o_ref[:, pl.ds(2 * PAD_LANES, PAD_LANES)] = dvbuf[pl.ds(ATTN_W, qb), :].astype(BF16)

    cur = lambda c: pl.BlockSpec((None, qb, PAD_LANES), lambda r, n: (r, n, c))
    prev = lambda c: pl.BlockSpec((None, ATTN_W, PAD_LANES), lambda r, n: (r, jnp.maximum(n * nsub - 1, 0), c))
    nxt = pl.BlockSpec((None, ATTN_W, PAD_LANES), lambda r, n: (r, jnp.minimum((n + 1) * nsub, nsb * nsub - 1), 0))
    return _pcall(
        body, name=name, grid=(dil, nsb),
        in_specs=[cur(0), cur(1), prev(1), cur(2), prev(2), cur(0), cur(0), cur(0), nxt, nxt, nxt, nxt],
        out_specs=pl.BlockSpec((None, qb, QKV_PAD), lambda r, n: (r, n, 0)),
        out_shape=_sds((dil, l_len, QKV_PAD), BF16),
        scratch_shapes=[pltpu.VMEM((qb + ATTN_W, PAD_LANES), BF16), pltpu.VMEM((qb + ATTN_W, PAD_LANES), BF16),
                        pltpu.VMEM((qb + ATTN_W, PAD_LANES), F32), pltpu.VMEM((qb + ATTN_W, PAD_LANES), F32)],
        semantics=("arbitrary", "arbitrary"), duty=duty,
    )(qkv_g, qkv_g, qkv_g, qkv_g, qkv_g, do_g, lse_g, c_g, qkv_g, do_g, lse_g, c_g)


def qkv_bwd(dqkv_parts, w_pad, dxo, x, g_row, cos, sin):
    t_len = x.shape[0]
    tm = _tile_rows(t_len)

    def body(p1, p4, p16, w_ref, dxo_ref, x_ref, g_ref, cos_ref, sin_ref, dq_ref, dx_ref, dn_ref, slabs):
        @pl.when(pl.program_id(0) == 0)
        def _():
            dn_ref[...] = jnp.zeros_like(dn_ref)

        cos_t = cos_ref[...]
        sin_t = sin_ref[...]
        dh = None
        for gi, (dil, part) in enumerate(zip(GROUP_DIL, (p1, p4, p16))):
            slab = slabs.at[gi]
            for j in range(N_SLABS):
                if dil == 1:
                    a = part[0, :, _lane_cols(j)].astype(F32)
                else:
                    _to_natural(part, j, slab, j, dil, tm)
                    a = slab[j]
                if j < 6:
                    a = a * cos_t - _rot_half(a * sin_t)
                if j < 3:
                    a = a * (HEAD_DIM ** -0.5)
                dq_ref[gi, :, _lane_cols(j)] = a.astype(BF16)
            contrib = _dot_nt(dq_ref[gi], w_ref[gi])
            dh = contrib if dh is None else dh + contrib
        dx, dn = _rms_bwd(dh, x_ref[...], g_ref[...])
        dx_ref[...] = dxo_ref[...] + dx
        dn_ref[...] += dn

    row_spec = pl.BlockSpec((tm, D_MODEL), lambda t: (t, 0))
    vec_spec = pl.BlockSpec((1, D_MODEL), lambda t: (0, 0))
    tab_spec = pl.BlockSpec((tm, LANE), lambda t: (t, 0))
    part_specs = [pl.BlockSpec((d, tm // d, QKV_PAD), lambda t: (0, t, 0)) for d in GROUP_DIL]
    return _pcall(
        body, name="qkv_bwd", grid=(t_len // tm,),
        in_specs=part_specs + [pl.BlockSpec((3, D_MODEL, QKV_PAD), lambda t: (0, 0, 0)),
                               row_spec, row_spec, vec_spec, tab_spec, tab_spec],
        out_specs=[pl.BlockSpec((3, tm, QKV_PAD), lambda t: (0, t, 0)), row_spec, vec_spec],
        out_shape=[_sds((3, t_len, QKV_PAD), BF16), _sds((t_len, D_MODEL), F32), _sds((1, D_MODEL), F32)],
        scratch_shapes=[pltpu.VMEM((3, N_SLABS, tm, LANE), F32)],
        semantics=("arbitrary",),
    )(*dqkv_parts, w_pad, dxo, x, g_row, cos, sin)


def final_fwd_bwd(x, g_row, target):
    t_len = x.shape[0]
    tm = _tile_rows(t_len)

    def body(x_ref, g_ref, tgt_ref, dx_ref, dn_ref, loss_ref):
        @pl.when(pl.program_id(0) == 0)
        def _():
            dn_ref[...] = jnp.zeros_like(dn_ref)
            loss_ref[...] = jnp.zeros_like(loss_ref)

        x_t = x_ref[...]
        g = g_ref[...]
        diff = _rms_fwd(x_t, g) - tgt_ref[...]
        loss_ref[...] += 0.5 * jnp.sum(jnp.mean(diff * diff, axis=-1, keepdims=True), axis=0, keepdims=True)
        dx, dn = _rms_bwd(diff * (1.0 / D_MODEL), x_t, g)
        dx_ref[...] = dx
        dn_ref[...] += dn

    row_spec = pl.BlockSpec((tm, D_MODEL), lambda t: (t, 0))
    vec_spec = pl.BlockSpec((1, D_MODEL), lambda t: (0, 0))
    return _pcall(
        body, name="final_fwd_bwd", grid=(t_len // tm,),
        in_specs=[row_spec, vec_spec, row_spec],
        out_specs=[row_spec, vec_spec, pl.BlockSpec((1, 1), lambda t: (0, 0))],
        out_shape=[_sds((t_len, D_MODEL), F32), _sds((1, D_MODEL), F32), _sds((1, 1), F32)],
        semantics=("arbitrary",),
    )(x, g_row, target)


def pool_bwd(dxo, x, g_row, w_in, w_grp, scale, w_out, zr, duty=None):
    t_len = x.shape[0]
    tm = _tile_rows(t_len)
    nt = t_len // tm

    def body(dxo_ref, x_ref, g_ref, win_ref, wgrp_ref, scale_ref, wout_ref, zr_ref,
             dzs_ref, du_ref, dx_ref, dn_ref, dsc_ref, ebuf):
        i = pl.program_id(0)
        t = nt - 1 - i

        @pl.when(i == 0)
        def _():
            ebuf[pl.ds(tm, POOL_HALO), :] = jnp.zeros((POOL_HALO, D_MODEL), F32)
            dn_ref[...] = jnp.zeros_like(dn_ref)
            dsc_ref[...] = jnp.zeros_like(dsc_ref)

        dxo_t = dxo_ref[...]
        dz = _dot_nt(dxo_t.astype(BF16), wout_ref[...])
        dsc_ref[...] += jnp.sum(dz * zr_ref[...].astype(F32), axis=0, keepdims=True)
        dzs_ref[...] = (dz * scale_ref[...]).astype(BF16)
        row = t * tm + lax.broadcasted_iota(jnp.int32, (tm, 1), 0)
        for gi, w in enumerate(POOL_WINDOWS):
            cols = pl.ds(gi * POOL_GROUP_DIM, POOL_GROUP_DIM)
            dp_g = _dot_nt(dzs_ref[:, cols], wgrp_ref[gi])
            inv_cnt = 1.0 / jnp.minimum(row + 1, w).astype(F32)
            ebuf[pl.ds(0, tm), cols] = dp_g * inv_cnt
            acc = -dp_g
            for j in range(w):
                acc = acc + ebuf[pl.ds(j, tm), cols]
            du_ref[:, cols] = acc.astype(BF16)
        ebuf[pl.ds(tm, POOL_HALO), :] = ebuf[pl.ds(0, POOL_HALO), :]
        dh = _dot_nt(du_ref[...], win_ref[...])
        dx, dn = _rms_bwd(dh, x_ref[...], g_ref[...])
        dx_ref[...] = dxo_t + dx
        dn_ref[...] += dn

    row_spec = pl.BlockSpec((tm, D_MODEL), lambda i: (nt - 1 - i, 0))
    full = lambda shape: pl.BlockSpec(shape, lambda i: (0,) * len(shape))
    vec = full((1, D_MODEL))
    return _pcall(
        body, name="pool_bwd", grid=(nt,),
        in_specs=[row_spec, row_spec, vec, full((D_MODEL, D_MODEL)), full((4, POOL_GROUP_DIM, POOL_GROUP_DIM)),
                  vec, full((D_MODEL, D_MODEL)), row_spec],
        out_specs=[row_spec, row_spec, row_spec, vec, vec],
        out_shape=[_sds((t_len, D_MODEL), BF16), _sds((t_len, D_MODEL), BF16), _sds((t_len, D_MODEL), F32),
                   _sds((1, D_MODEL), F32), _sds((1, D_MODEL), F32)],
        scratch_shapes=[pltpu.VMEM((tm + POOL_HALO, D_MODEL), F32)],
        semantics=("arbitrary",), duty=duty,
    )(dxo, x, g_row, w_in, w_grp, scale, w_out, zr)


def _mesh_pos():
    return lax.axis_index("x"), lax.axis_index("y"), lax.axis_index("c")


def _other_chips(x, y):
    return [(1 - x, y), (x, 1 - y), (1 - x, 1 - y)]


def _remote(src, dst, send_sem, recv_sem, device):
    return pltpu.make_async_remote_copy(src_ref=src, dst_ref=dst, send_sem=send_sem, recv_sem=recv_sem,
                                        device_id=device, device_id_type=MESH)


class _Duty:
    aliases = {}

    def mid(self, ins, outs, sems):
        pass


class Together(_Duty):
    def __init__(self, duties):
        self.duties = duties
        self.ins = [a for d in duties for a in d.ins]
        self.out_shape = [s for d in duties for s in d.out_shape]
        self.scratch = [s for d in duties for s in d.scratch]
        self.aliases = {}
        i0 = o0 = 0
        for d in duties:
            self.aliases.update({i0 + i: o0 + o for i, o in d.aliases.items()})
            i0 += len(d.ins)
            o0 += len(d.out_shape)

    def _each(self, ins, outs, sems):
        i0 = o0 = s0 = 0
        for d in self.duties:
            ni, no, ns = len(d.ins), len(d.out_shape), len(d.scratch)
            yield d, ins[i0:i0 + ni], outs[o0:o0 + no], sems[s0:s0 + ns]
            i0, o0, s0 = i0 + ni, o0 + no, s0 + ns

    def split(self, outs):
        return [list(o) for _, _, o, _ in self._each(self.ins, outs, self.scratch)]

    def start(self, ins, outs, sems):
        for d, i, o, s in self._each(ins, outs, sems):
            d.start(i, o, s)

    def mid(self, ins, outs, sems):
        for d, i, o, s in self._each(ins, outs, sems):
            d.mid(i, o, s)

    def finish(self, ins, outs, sems):
        for d, i, o, s in self._each(ins, outs, sems):
            d.finish(i, o, s)


def run_duty(duty, name):
    d_in, d_out = len(duty.ins), len(duty.out_shape)

    def body(*refs):
        ins, outs, sems = refs[:d_in], refs[d_in:d_in + d_out], refs[d_in + d_out:]
        duty.start(ins, outs, sems)
        duty.mid(ins, outs, sems)
        duty.finish(ins, outs, sems)

    return pl.pallas_call(
        body, name=name, out_shape=list(duty.out_shape), in_specs=[_ANY] * d_in, out_specs=[_ANY] * d_out,
        scratch_shapes=list(duty.scratch), input_output_aliases=dict(duty.aliases),
        compiler_params=pltpu.CompilerParams(has_side_effects=True),
    )(*duty.ins)


class GatherWeights(_Duty):
    def __init__(self, shards):
        n = self.n = len(shards)
        self.halves = [s.shape[0] // 2 for s in shards]
        my_slot = 2 * lax.axis_index("x") + lax.axis_index("y")
        staged = [lax.dynamic_update_slice(lax.empty((N_SHARDS,) + s.shape, s.dtype), s[None], (my_slot, 0, 0))
                  for s in shards]
        self.ins = list(shards) + staged
        self.out_shape = [_sds((N_SHARDS,) + s.shape, s.dtype) for s in shards]
        self.aliases = {n + a: a for a in range(n)}
        self.scratch = [pltpu.SemaphoreType.DMA((n, 6)), pltpu.SemaphoreType.DMA((n, 6))]

    def _over_ici(self, ins, outs, sems):
        x, y, c = _mesh_pos()
        return [_remote(ins[a].at[pl.ds(c * h, h)], outs[a].at[2 * x + y, pl.ds(c * h, h)],
                        sems[0].at[a, j], sems[1].at[a, j], (*chip, c))
                for a, h in enumerate(self.halves) for j, chip in enumerate(_other_chips(x, y))]

    def _forwards(self, outs, sems, half_of):
        x, y, c = _mesh_pos()
        cps = []
        for a, h in enumerate(self.halves):
            for j, chip in enumerate(_other_chips(x, y)):
                slot = outs[a].at[2 * chip[0] + chip[1], pl.ds(half_of(c) * h, h)]
                cps.append(_remote(slot, slot, sems[0].at[a, 3 + j], sems[1].at[a, 3 + j], (x, y, 1 - c)))
        return cps

    def start(self, ins, outs, sems):
        for cp in self._over_ici(ins, outs, sems):
            cp.start()

    def mid(self, ins, outs, sems):
        x, y, c = _mesh_pos()
        forwards = self._forwards(outs, sems, lambda core: core)
        k = 0
        for a, h in enumerate(self.halves):
            for j, chip in enumerate(_other_chips(x, y)):
                slot = outs[a].at[2 * chip[0] + chip[1], pl.ds(c * h, h)]
                _remote(slot, slot, sems[0].at[a, j], sems[1].at[a, j], (*chip, c)).wait_recv()
                forwards[k].start()
                k += 1

    def finish(self, ins, outs, sems):
        for cp in self._forwards(outs, sems, lambda core: 1 - core):
            cp.wait_recv()
        for cp in self._over_ici(ins, outs, sems) + self._forwards(outs, sems, lambda core: core):
            cp.wait_send()


class GradReducer:
    def __init__(self, c_idx, pos_idx):
        self.c_idx, self.pos_idx = c_idx, pos_idx
        self.in_flight = []
        self.done = {}

    def push(self, name, grad):
        self.in_flight.append(dict(name=name, stage="halves", data=grad))

    def _duties(self):
        make = {"halves": SiblingHalves, "exchange": ChipExchange, "share": SiblingShare}
        return Together([make[w["stage"]]([w["data"]]) for w in self.in_flight])

    def _advance(self, duties, outs):
        still = []
        for w, (res,) in zip(self.in_flight, duties.split(outs)):
            if w["stage"] == "halves":
                partial = add_my_half(w["data"], res, self.c_idx, f"rs_add_{w['name']}")
                still.append(dict(name=w["name"], stage="exchange", data=partial))
            elif w["stage"] == "exchange":
                reduced = sum_slots(res, w["data"], self.pos_idx, f"rs_sum_{w['name']}")
                still.append(dict(name=w["name"], stage="share", data=reduced))
            else:
                self.done[w["name"]] = res
        self.in_flight = still

    def carried_by(self, fn, *args, **kw):
        if not self.in_flight:
            return fn(*args, **kw)
        duties = self._duties()
        out, duty_outs = fn(*args, duty=duties, **kw)
        self._advance(duties, duty_outs)
        return out

    def drain(self, name):
        step = 0
        while self.in_flight:
            duties = self._duties()
            self._advance(duties, run_duty(duties, f"{name}{step}"))
            step += 1


class SiblingHalves(_Duty):
    def __init__(self, grads):
        n = len(grads)
        self.halves = [g.shape[1] // 2 for g in grads]
        self.ins = list(grads)
        self.out_shape = [_sds((N_SHARDS, h, g.shape[2]), g.dtype) for g, h in zip(grads, self.halves)]
        self.scratch = [pltpu.SemaphoreType.DMA((n,)), pltpu.SemaphoreType.DMA((n,))]

    def _copies(self, ins, outs, sems):
        x, y, c = _mesh_pos()
        return [_remote(ins[a].at[:, pl.ds((1 - c) * h, h)], outs[a], sems[0].at[a], sems[1].at[a], (x, y, 1 - c))
                for a, h in enumerate(self.halves)]

    def start(self, ins, outs, sems):
        for cp in self._copies(ins, outs, sems):
            cp.start()

    def finish(self, ins, outs, sems):
        for cp in self._copies(ins, outs, sems):
            cp.wait()


class ChipExchange(_Duty):
    def __init__(self, parts):
        n = self.n = len(parts)
        self.ins = list(parts)
        self.out_shape = [_sds(p.shape, p.dtype) for p in parts]
        self.scratch = [pltpu.SemaphoreType.DMA((n, 3)), pltpu.SemaphoreType.DMA((n, 3))]

    def _copies(self, ins, outs, sems, arriving):
        x, y, c = _mesh_pos()
        cps = []
        for a in range(self.n):
            for j, chip in enumerate(_other_chips(x, y)):
                theirs = 2 * chip[0] + chip[1]
                src = outs[a].at[theirs] if arriving else ins[a].at[theirs]
                dst = outs[a].at[theirs] if arriving else outs[a].at[2 * x + y]
                cps.append(_remote(src, dst, sems[0].at[a, j], sems[1].at[a, j], (*chip, c)))
        return cps

    def start(self, ins, outs, sems):
        for cp in self._copies(ins, outs, sems, False):
            cp.start()

    def finish(self, ins, outs, sems):
        for cp in self._copies(ins, outs, sems, True):
            cp.wait_recv()
        for cp in self._copies(ins, outs, sems, False):
            cp.wait_send()


class SiblingShare(_Duty):
    def __init__(self, reduced):
        n = self.n = len(reduced)
        self.ins = list(reduced)
        self.out_shape = [_sds(r.shape, r.dtype) for r in reduced]
        self.aliases = {a: a for a in range(n)}
        self.scratch = [pltpu.SemaphoreType.DMA((n,)), pltpu.SemaphoreType.DMA((n,))]

    def _copies(self, outs, sems, half_of):
        x, y, c = _mesh_pos()
        cps = []
        for a in range(self.n):
            h = outs[a].shape[0] // 2
            rows = outs[a].at[pl.ds(half_of(c) * h, h)]
            cps.append(_remote(rows, rows, sems[0].at[a], sems[1].at[a], (x, y, 1 - c)))
        return cps

    def start(self, ins, outs, sems):
        for cp in self._copies(outs, sems, lambda core: core):
            cp.start()

    def finish(self, ins, outs, sems):
        for cp in self._copies(outs, sems, lambda core: 1 - core):
            cp.wait_recv()
        for cp in self._copies(outs, sems, lambda core: core):
            cp.wait_send()


def allreduce_small(v):
    def body(v_ref, o_ref, buf, send_sems, recv_sems):
        x, y, c = _mesh_pos()
        me = 4 * x + 2 * y + c
        buf[me] = v_ref[...]
        flip = lambda p, f: 1 - p if f else p
        peers = [(flip(x, k & 4), flip(y, k & 2), flip(c, k & 1)) for k in range(1, N_DEV)]
        cps = []
        for k, peer in enumerate(peers):
            cp = _remote(v_ref, buf.at[me], send_sems.at[k], recv_sems.at[k], peer)
            cp.start()
            cps.append(cp)
        for k, peer in enumerate(peers):
            slot = buf.at[4 * peer[0] + 2 * peer[1] + peer[2]]
            _remote(slot, slot, send_sems.at[k], recv_sems.at[k], peer).wait_recv()
        for cp in cps:
            cp.wait_send()
        acc = buf[0]
        for i in range(1, N_DEV):
            acc = acc + buf[i]
        o_ref[...] = acc

    vm = pl.BlockSpec(memory_space=pltpu.VMEM)
    return pl.pallas_call(
        body, name="allreduce_small", out_shape=_sds(v.shape, v.dtype), in_specs=[vm], out_specs=vm,
        scratch_shapes=[pltpu.VMEM((N_DEV,) + v.shape, v.dtype),
                        pltpu.SemaphoreType.DMA((N_DEV - 1,)), pltpu.SemaphoreType.DMA((N_DEV - 1,))],
        compiler_params=pltpu.CompilerParams(has_side_effects=True),
    )(v)


def add_my_half(grad, theirs, c_idx, name):
    _, r, cols = grad.shape
    h = r // 2

    def body(c_ref, g_ref, t_ref, o_ref):
        o_ref[...] = (g_ref[...] + t_ref[...]).astype(BF16)

    slot = pl.BlockSpec((None, h, cols), lambda s, c: (s, 0, 0))
    grid_spec = pltpu.PrefetchScalarGridSpec(
        num_scalar_prefetch=1, grid=(N_SHARDS,),
        in_specs=[pl.BlockSpec((None, h, cols), lambda s, c: (s, c[0], 0)), slot], out_specs=slot)
    return pl.pallas_call(
        body, name=name, grid_spec=grid_spec, out_shape=_sds((N_SHARDS, h, cols), BF16),
        compiler_params=pltpu.CompilerParams(dimension_semantics=("arbitrary",), vmem_limit_bytes=VMEM_LIMIT_BYTES),
    )(c_idx, grad, theirs)


def sum_slots(received, mine, pos_idx, name):
    _, h, cols = received.shape

    def body(pos_ref, r_ref, m_ref, o_ref):
        acc = None
        for k in range(N_SHARDS):
            term = jnp.where(pos_ref[0] == k, m_ref[k], r_ref[k]).astype(F32)
            acc = term if acc is None else acc + term
        o_ref[...] = acc

    whole = pl.BlockSpec((N_SHARDS, h, cols), lambda i, pos: (0, 0, 0))
    grid_spec = pltpu.PrefetchScalarGridSpec(
        num_scalar_prefetch=1, grid=(1,), in_specs=[whole, whole],
        out_specs=pl.BlockSpec((h, cols), lambda i, pos: (pos[1], 0)))
    return pl.pallas_call(
        body, name=name, grid_spec=grid_spec, out_shape=_sds((2 * h, cols), F32),
        compiler_params=pltpu.CompilerParams(dimension_semantics=("arbitrary",), vmem_limit_bytes=VMEM_LIMIT_BYTES),
    )(pos_idx, received, mine)


def adamw(name, grads, w, m, v):
    n_layers, r, cols = w.shape
    tr = r // 2 if r % 16 == 0 else r
    bias1 = 1.0 - ADAM_B1 ** ADAM_STEP
    bias2 = 1.0 - ADAM_B2 ** ADAM_STEP

    def body(*refs):
        g_refs = refs[:n_layers]
        w_ref, m_ref, v_ref, go_ref, d_ref, mo_ref, vo_ref = refs[n_layers:]
        g = g_refs[0][...]
        for layer in range(1, n_layers):
            g = jnp.where(pl.program_id(0) == layer, g_refs[layer][...], g)
        m_new = ADAM_B1 * m_ref[...] + (1.0 - ADAM_B1) * g
        v_new = ADAM_B2 * v_ref[...] + (1.0 - ADAM_B2) * (g * g)
        m_hat = m_new / bias1
        v_hat = v_new / bias2
        go_ref[...] = g
        d_ref[...] = -ADAM_LR * (m_hat / (jnp.sqrt(v_hat) + ADAM_EPS) + ADAM_WD * w_ref[...])
        mo_ref[...] = m_new
        vo_ref[...] = v_new

    g_spec = pl.BlockSpec((tr, cols), lambda l, i: (i, 0))
    lay_spec = pl.BlockSpec((None, tr, cols), lambda l, i: (l, i, 0))
    shape = _sds((n_layers, r, cols), F32)
    return _pcall(
        body, name=name, grid=(n_layers, r // tr),
        in_specs=[g_spec] * n_layers + [lay_spec] * 3, out_specs=[lay_spec] * 4,
        out_shape=[shape] * 4, semantics=("arbitrary", "arbitrary"),
    )(*grads, w, m, v)


def kernel(x, norm_mix, norm_ffn, norm_final, pool_w_in, pool_w_group, pool_scale, pool_w_out, attn_w_qkv, attn_w_out, ffn_w_gate, ffn_w_up, ffn_w_down, loss_target, m_norm_mix, m_norm_ffn, m_norm_final, m_pool_w_in, m_pool_w_group, m_pool_scale, m_pool_w_out, m_attn_w_qkv, m_attn_w_out, m_ffn_w_gate, m_ffn_w_up, m_ffn_w_down, v_norm_mix, v_norm_ffn, v_norm_final, v_pool_w_in, v_pool_w_group, v_pool_scale, v_pool_w_out, v_attn_w_qkv, v_attn_w_out, v_ffn_w_gate, v_ffn_w_up, v_ffn_w_down):
    t_len = x.shape[1]
    x0 = x.reshape(t_len, D_MODEL)
    target = loss_target.reshape(t_len, D_MODEL)
    row = lambda a: a.reshape(1, D_MODEL)

    grp_rows = POOL_GROUP_DIM // N_SHARDS
    bf = lambda a: a.astype(BF16)
    gate_t, up_t = jnp.swapaxes(ffn_w_gate, 1, 2), jnp.swapaxes(ffn_w_up, 1, 2)
    pool_shards = [bf(pool_w_in[0]), bf(pool_w_group[0].reshape(4 * grp_rows, POOL_GROUP_DIM)), bf(pool_w_out[0])]
    ffn0_shards = [bf(gate_t[0]), bf(up_t[0]), bf(ffn_w_down[0])]
    late_shards = [bf(attn_w_qkv[0]), bf(attn_w_out[0]), bf(gate_t[1]), bf(up_t[1]), bf(ffn_w_down[1])]
    cos, sin = rope_tables(t_len)
    c_idx = lax.axis_index("c").astype(jnp.int32).reshape(1)
    pos_idx = jnp.stack([2 * lax.axis_index("x") + lax.axis_index("y"), lax.axis_index("c")]).astype(jnp.int32)
    chip_rows = lambda g: g.reshape(N_SHARDS, D_MODEL // N_SHARDS, D_MODEL)

    g_pool = run_duty(GatherWeights(pool_shards), "gather_pool")
    w_in = g_pool[0].reshape(D_MODEL, D_MODEL)
    w_grp = g_pool[1].reshape(N_SHARDS, 4, grp_rows, POOL_GROUP_DIM).transpose(1, 0, 2, 3).reshape(
        4, POOL_GROUP_DIM, POOL_GROUP_DIM)
    w_out = g_pool[2].reshape(D_MODEL, D_MODEL)
    (h0, p, zr, z, x1), ffn0 = pool_fwd(x0, row(norm_mix[0]), w_in, w_grp, pool_scale, w_out,
                                        duty=GatherWeights(ffn0_shards))
    (h1, gate0, up0, act0, x2), late = ffn_fwd(x1, row(norm_ffn[0]), *ffn0, "ffn_fwd0",
                                               duty=GatherWeights(late_shards))
    w_qkv = pad_qkv_weight(late[0])
    w_ao = jnp.concatenate(pad_groups(late[1].reshape(D_MODEL, D_MODEL), 0), axis=0)
    ffn1 = late[2:5]
    h2, *qkv_parts = qkv_fwd(x2, row(norm_mix[1]), w_qkv, cos, sin)
    o_parts, lse_parts = [], []
    for gi in range(3):
        o_g, lse_g = attn_fwd(qkv_parts[gi], gi, f"attn_fwd_g{gi}")
        o_parts.append(o_g)
        lse_parts.append(lse_g)
    x3, merged, o_nat, lse_nat = attn_out_fwd(x2, o_parts, lse_parts, w_ao)
    h3, gate1, up1, act1, x4 = ffn_fwd(x3, row(norm_ffn[1]), *ffn1, "ffn_fwd1")
    dx4, d_norm_final, loss_local = final_fwd_bwd(x4, row(norm_final), target)

    red = GradReducer(c_idx, pos_idx)
    dgate1, dup1, dx3, d_nf1 = ffn_bwd(dx4, x3, row(norm_ffn[1]), gate1, up1, *ffn1, "ffn_bwd1")
    red.push("gate1", wgrad_row_sharded("wgrad_gate1", dgate1, h3))
    red.push("up1", red.carried_by(wgrad_row_sharded, "wgrad_up1", dup1, h3))
    red.push("down1", red.carried_by(wgrad_row_sharded, "wgrad_down1", act1, dx4))
    do_parts, c_parts = red.carried_by(attn_out_bwd, dx3, w_ao, o_nat, lse_nat)
    g_ao = red.carried_by(wgrad_full, "wgrad_attn_out", merged, dx3)
    red.push("attn_out", chip_rows(unpad_groups(jnp.split(g_ao, 3, axis=0), 0)))
    dqkv_parts = [red.carried_by(attn_bwd, qkv_parts[gi], do_parts[gi], lse_parts[gi], c_parts[gi], gi,
                                 f"attn_bwd_g{gi}") for gi in range(3)]
    dqkv, dx2, d_nm1 = qkv_bwd(dqkv_parts, w_qkv, dx3, x2, row(norm_mix[1]), cos, sin)
    red.push("qkv", unpad_qkv_grad(wgrad_col_sharded("wgrad_qkv", h2, dqkv)))

    dgate0, dup0, dx1, d_nf0 = red.carried_by(ffn_bwd, dx2, x1, row(norm_ffn[0]), gate0, up0, *ffn0, "ffn_bwd0")
    red.push("gate0", red.carried_by(wgrad_row_sharded, "wgrad_gate0", dgate0, h1))
    red.push("up0", red.carried_by(wgrad_row_sharded, "wgrad_up0", dup0, h1))
    red.push("down0", red.carried_by(wgrad_row_sharded, "wgrad_down0", act0, dx2))
    red.push("pool_out", chip_rows(red.carried_by(wgrad_full, "wgrad_pool_out", z, dx1)))
    dzs, du, dx0, d_nm0, d_scale = pool_bwd(dx1, x0, row(norm_mix[0]), w_in, w_grp, pool_scale, w_out, zr)
    g_grp = red.carried_by(wgrad_pool_group, "wgrad_pool_group", p, dzs)
    red.push("pool_group", g_grp.reshape(N_SHARDS, 4 * grp_rows, POOL_GROUP_DIM))
    red.push("pool_in", chip_rows(red.carried_by(wgrad_full, "wgrad_pool_in", h0, du)))
    red.drain("rs_tail")
    full = [red.done[nm] for nm in ("pool_in", "pool_group", "pool_out", "qkv", "attn_out",
                                    "gate0", "gate1", "up0", "up1", "down0", "down1")]

    zero_row = jnp.zeros((1, D_MODEL), F32)
    small = jnp.concatenate([d_nm0, d_nm1, d_nf0, d_nf1, d_norm_final, d_scale,
                             jnp.broadcast_to(loss_local, (1, D_MODEL)), zero_row], axis=0)
    small = allreduce_small(small)
    loss = small[6, 0]

    pack = lambda a, b, c, d: jnp.concatenate([a, b, row(c), d, zero_row, zero_row], axis=0)[None]
    sg, sd, sm, sv = adamw("adamw_small", [small],
                           pack(norm_mix, norm_ffn, norm_final, pool_scale),
                           pack(m_norm_mix, m_norm_ffn, m_norm_final, m_pool_scale),
                           pack(v_norm_mix, v_norm_ffn, v_norm_final, v_pool_scale))
    unpack = lambda a: (a[0, 0:2], a[0, 2:4], a[0, 4], a[0, 5:6])

    def update(name, grads, w, m, v, transposed=False):
        if transposed:
            w, m, v = (jnp.swapaxes(a, 1, 2) for a in (w, m, v))
        n_layers = len(grads)
        shp = (n_layers,) + grads[0].shape
        outs = [o.reshape(w.shape) for o in adamw(name, grads, w.reshape(shp), m.reshape(shp), v.reshape(shp))]
        return [jnp.swapaxes(o, 1, 2) for o in outs] if transposed else outs

    big = [
        update("adamw_pool_in", [full[0]], pool_w_in, m_pool_w_in, v_pool_w_in),
        update("adamw_pool_group", [full[1]], pool_w_group, m_pool_w_group, v_pool_w_group),
        update("adamw_pool_out", [full[2]], pool_w_out, m_pool_w_out, v_pool_w_out),
        update("adamw_qkv", [full[3]], attn_w_qkv, m_attn_w_qkv, v_attn_w_qkv),
        update("adamw_attn_out", [full[4]], attn_w_out, m_attn_w_out, v_attn_w_out),
        update("adamw_gate", [full[5], full[6]], ffn_w_gate, m_ffn_w_gate, v_ffn_w_gate, transposed=True),
        update("adamw_up", [full[7], full[8]], ffn_w_up, m_ffn_w_up, v_ffn_w_up, transposed=True),
        update("adamw_down", [full[9], full[10]], ffn_w_down, m_ffn_w_down, v_ffn_w_down),
    ]

    def leaves(k, small_vals):
        nm, nf, nfin, psc = unpack(small_vals)
        return [nm, nf, nfin, big[0][k], big[1][k], psc, big[2][k], big[3][k], big[4][k],
                big[5][k], big[6][k], big[7][k]]

    grad_x = dx0.reshape(x.shape)
    return (loss, grad_x, *leaves(0, sg), *leaves(1, sd), *leaves(2, sm), *leaves(3, sv))
```

```python
import math

import jax
import jax.numpy as jnp
from jax import lax
from jax.experimental import pallas as pl
from jax.experimental.pallas import tpu as pltpu

F32 = jnp.float32
BF16 = jnp.bfloat16

D_MODEL = 1024
N_SHARDS = 4
N_DEV = 8
D_FF = 2816
FF_SHARD = D_FF // N_SHARDS
HEAD_DIM = 64
QKV_SHARD = 3 * D_MODEL // N_SHARDS
POOL_WINDOWS = (2, 4, 8, 16)
POOL_GROUP_DIM = 256
POOL_HALO = 16
ATTN_W = 128
GROUP_LANES = (0, 384, 704, 1024)
GROUP_HEADS = (6, 5, 5)
GROUP_DIL = (1, 4, 16)
ROPE_THETA = 10000.0
EPS = 1e-6
NEG_INF = -1e30
LANE = 128
VMEM_LIMIT_BYTES = 60 * 1024 * 1024

ADAM_LR = 0.001
ADAM_B1 = 0.9
ADAM_B2 = 0.999
ADAM_EPS = 1e-08
ADAM_WD = 0.01
ADAM_STEP = 10

NT_DIMS = (((1,), (1,)), ((), ()))
TN_DIMS = (((0,), (0,)), ((), ()))
MESH = pl.DeviceIdType.MESH


_ANY = pl.BlockSpec(memory_space=pl.ANY)


def _pcall(body, *, name, out_shape, grid=None, in_specs=None, out_specs=None, scratch_shapes=(),
           semantics=None, duty=None):
    kw = {}
    if in_specs is not None and duty is None:
        kw["in_specs"] = in_specs
    if out_specs is not None and duty is None:
        kw["out_specs"] = out_specs
    if grid is not None:
        kw["grid"] = grid
    params = dict(dimension_semantics=semantics, vmem_limit_bytes=VMEM_LIMIT_BYTES)
    if duty is None:
        return pl.pallas_call(body, name=name, out_shape=out_shape, scratch_shapes=list(scratch_shapes),
                              compiler_params=pltpu.CompilerParams(**params), **kw)

    single = not isinstance(out_shape, (list, tuple))
    c_out_shape = [out_shape] if single else list(out_shape)
    c_out_specs = [out_specs] if single else list(out_specs)
    n_in, n_out, n_scr = len(in_specs), len(c_out_shape), len(scratch_shapes)
    d_in, d_out = len(duty.ins), len(duty.out_shape)
    total = math.prod(grid)
    mid_step = (5 * total) // 6

    def wrapped(*refs):
        c_in, d_ins = refs[:n_in], refs[n_in:n_in + d_in]
        o0 = n_in + d_in
        c_outs, d_outs = refs[o0:o0 + n_out], refs[o0 + n_out:o0 + n_out + d_out]
        s0 = o0 + n_out + d_out
        c_scr, d_sems = refs[s0:s0 + n_scr], refs[s0 + n_scr:]
        step = pl.program_id(0)
        for ax in range(1, len(grid)):
            step = step * grid[ax] + pl.program_id(ax)

        @pl.when(step == 0)
        def _():
            duty.start(d_ins, d_outs, d_sems)

        body(*c_in, *c_outs, *c_scr)

        @pl.when(step == mid_step)
        def _():
            duty.mid(d_ins, d_outs, d_sems)

        @pl.when(step == total - 1)
        def _():
            duty.finish(d_ins, d_outs, d_sems)

    call = pl.pallas_call(
        wrapped, name=name, grid=grid,
        in_specs=list(in_specs) + [_ANY] * d_in, out_specs=c_out_specs + [_ANY] * d_out,
        out_shape=c_out_shape + list(duty.out_shape),
        scratch_shapes=list(scratch_shapes) + list(duty.scratch),
        input_output_aliases={n_in + i: n_out + o for i, o in duty.aliases.items()},
        compiler_params=pltpu.CompilerParams(has_side_effects=True, **params))

    def run(*args):
        outs = call(*args, *duty.ins)
        c = outs[:n_out]
        return (c[0] if single else list(c)), list(outs[n_out:])

    return run


def _sds(shape, dtype):
    return jax.ShapeDtypeStruct(tuple(shape), dtype)


def _dot(a, b):
    return jnp.dot(a, b, preferred_element_type=F32)


def _dot_nt(a, b):
    return lax.dot_general(a, b, NT_DIMS, preferred_element_type=F32)


def _dot_tn(a, b):
    return lax.dot_general(a, b, TN_DIMS, preferred_element_type=F32)


def _rms_fwd(x, g):
    r = lax.rsqrt(jnp.mean(x * x, axis=-1, keepdims=True) + EPS)
    return x * r * g


def _rms_bwd(dh, x, g):
    r = lax.rsqrt(jnp.mean(x * x, axis=-1, keepdims=True) + EPS)
    xh = x * r
    dg = jnp.sum(dh * xh, axis=0, keepdims=True)
    dxh = dh * g
    dx = r * (dxh - xh * jnp.mean(dxh * xh, axis=-1, keepdims=True))
    return dx, dg


def _sigmoid(x):
    return 0.5 * jnp.tanh(0.5 * x) + 0.5


def _tile_rows(t):
    return min(512, t)


def _sub_tiles(tm, n_sub=2):
    rows = tm // n_sub
    return [pl.ds(i * rows, rows) for i in range(n_sub)]


def _wgrad_rows(t):
    return min(2048, t)


def pool_fwd(x, g_row, w_in, w_grp, scale, w_out, duty=None):
    t_len = x.shape[0]
    tm = _tile_rows(t_len)

    def body(x_ref, g_ref, win_ref, wgrp_ref, scale_ref, wout_ref,
             h_ref, p_ref, zr_ref, z_ref, xo_ref, ubuf):
        t = pl.program_id(0)

        @pl.when(t == 0)
        def _():
            ubuf[pl.ds(0, POOL_HALO), :] = jnp.zeros((POOL_HALO, D_MODEL), F32)

        x_t = x_ref[...]
        h = _rms_fwd(x_t, g_ref[...]).astype(BF16)
        h_ref[...] = h
        ubuf[pl.ds(POOL_HALO, tm), :] = _dot(h, win_ref[...])
        row = t * tm + lax.broadcasted_iota(jnp.int32, (tm, 1), 0)
        for gi, w in enumerate(POOL_WINDOWS):
            cols = pl.ds(gi * POOL_GROUP_DIM, POOL_GROUP_DIM)
            u_g = ubuf[pl.ds(POOL_HALO, tm), cols]
            acc = u_g
            for j in range(1, w):
                acc = acc + ubuf[pl.ds(POOL_HALO - j, tm), cols]
            inv_cnt = 1.0 / jnp.minimum(row + 1, w).astype(F32)
            p_g = (acc * inv_cnt - u_g).astype(BF16)
            p_ref[:, cols] = p_g
            z_g = _dot(p_g, wgrp_ref[gi])
            zr_ref[:, cols] = z_g.astype(BF16)
            z_ref[:, cols] = (z_g * scale_ref[:, cols]).astype(BF16)
        ubuf[pl.ds(0, POOL_HALO), :] = ubuf[pl.ds(tm, POOL_HALO), :]
        xo_ref[...] = x_t + _dot(z_ref[...], wout_ref[...])

    row_spec = pl.BlockSpec((tm, D_MODEL), lambda t: (t, 0))
    full2 = lambda shape: pl.BlockSpec(shape, lambda t: (0,) * len(shape))
    return _pcall(
        body, name="pool_fwd", grid=(t_len // tm,),
        in_specs=[row_spec, full2((1, D_MODEL)), full2((D_MODEL, D_MODEL)),
                  full2((4, POOL_GROUP_DIM, POOL_GROUP_DIM)), full2((1, D_MODEL)), full2((D_MODEL, D_MODEL))],
        out_specs=[row_spec] * 5,
        out_shape=[_sds((t_len, D_MODEL), BF16)] * 4 + [_sds((t_len, D_MODEL), F32)],
        scratch_shapes=[pltpu.VMEM((tm + POOL_HALO, D_MODEL), F32)],
        semantics=("arbitrary",), duty=duty,
    )(x, g_row, w_in, w_grp, scale, w_out)


def ffn_fwd(x, g_row, w_gate_t, w_up_t, w_down, name, duty=None, loss_head=None):
    t_len = x.shape[0]
    tm = min(1024 if loss_head is None else 512, t_len)
    n_in = 5 if loss_head is None else 7

    def body(*refs):
        x_ref, g_ref, wg_ref, wu_ref, wd_ref = refs[:5]
        h_ref, go_ref, uo_ref, ao_ref = refs[n_in:n_in + 4]
        hbuf, acc = refs[-2:]
        t = pl.program_id(0)
        s = pl.program_id(1)

        @pl.when(s == 0)
        def _():
            h = _rms_fwd(x_ref[...], g_ref[...]).astype(BF16)
            hbuf[...] = h
            h_ref[...] = h
            acc[...] = jnp.zeros_like(acc)

        h = hbuf[...]
        gate = _dot_nt(h, wg_ref[...])
        up = _dot_nt(h, wu_ref[...])
        go_ref[...] = gate.astype(BF16)
        uo_ref[...] = up.astype(BF16)
        act = (gate * _sigmoid(gate) * up).astype(BF16)
        ao_ref[...] = act
        acc[...] += _dot(act, wd_ref[...])

        if loss_head is None:
            @pl.when(s == N_SHARDS - 1)
            def _():
                refs[n_in + 4][...] = x_ref[...] + acc[...]
        else:
            gf_ref, tgt_ref = refs[5:7]
            dx_ref, dn_ref, loss_ref = refs[n_in + 4:n_in + 7]

            @pl.when(jnp.logical_and(s == 0, t == 0))
            def _():
                dn_ref[...] = jnp.zeros_like(dn_ref)
                loss_ref[...] = jnp.zeros_like(loss_ref)

            @pl.when(s == N_SHARDS - 1)
            def _():
                x_out = x_ref[...] + acc[...]
                gf = gf_ref[...]
                diff = _rms_fwd(x_out, gf) - tgt_ref[...]
                loss_ref[...] += 0.5 * jnp.sum(jnp.mean(diff * diff, axis=-1, keepdims=True), axis=0, keepdims=True)
                dx, dn = _rms_bwd(diff * (1.0 / D_MODEL), x_out, gf)
                dx_ref[...] = dx
                dn_ref[...] += dn

    row_spec = pl.BlockSpec((tm, D_MODEL), lambda t, s: (t, 0))
    vec_spec = pl.BlockSpec((1, D_MODEL), lambda t, s: (0, 0))
    row_w = pl.BlockSpec((None, FF_SHARD, D_MODEL), lambda t, s: (s, 0, 0))
    act_spec = pl.BlockSpec((None, tm, FF_SHARD), lambda t, s: (s, t, 0))
    in_specs = [row_spec, vec_spec, row_w, row_w, row_w]
    out_specs = [row_spec, act_spec, act_spec, act_spec, row_spec]
    out_shape = [_sds((t_len, D_MODEL), BF16)] + [_sds((N_SHARDS, t_len, FF_SHARD), BF16)] * 3 + [
        _sds((t_len, D_MODEL), F32)]
    args = [x, g_row, w_gate_t, w_up_t, w_down]
    if loss_head is not None:
        in_specs += [vec_spec, row_spec]
        out_specs += [vec_spec, pl.BlockSpec((1, 1), lambda t, s: (0, 0))]
        out_shape += [_sds((1, D_MODEL), F32), _sds((1, 1), F32)]
        args += list(loss_head)
    return _pcall(
        body, name=name, grid=(t_len // tm, N_SHARDS), in_specs=in_specs, out_specs=out_specs, out_shape=out_shape,
        scratch_shapes=[pltpu.VMEM((tm, D_MODEL), BF16), pltpu.VMEM((tm, D_MODEL), F32)],
        semantics=("arbitrary", "arbitrary"), duty=duty,
    )(*args)


def ffn_bwd(dxo, x, g_row, gate, up, w_gate_t, w_up_t, w_down, name, duty=None):
    t_len = x.shape[0]
    tm = _tile_rows(t_len)

    def body(dxo_ref, x_ref, g_ref, gate_ref, up_ref, wg_ref, wu_ref, wd_ref,
             dg_ref, du_ref, dx_ref, dn_ref, dxb, dh):
        t = pl.program_id(0)
        s = pl.program_id(1)

        @pl.when(s == 0)
        def _():
            dxb[...] = dxo_ref[...].astype(BF16)
            dh[...] = jnp.zeros_like(dh)

        @pl.when(jnp.logical_and(s == 0, t == 0))
        def _():
            dn_ref[...] = jnp.zeros_like(dn_ref)

        sub_tiles = _sub_tiles(tm)
        dacts = [_dot_nt(dxb[rows, :], wd_ref[...]) for rows in sub_tiles]
        for rows, dact in zip(sub_tiles, dacts):
            gv = gate_ref[rows, :].astype(F32)
            uv = up_ref[rows, :].astype(F32)
            sg = _sigmoid(gv)
            dgv = (dact * uv * (sg * (1.0 + gv * (1.0 - sg)))).astype(BF16)
            duv = (dact * (gv * sg)).astype(BF16)
            dg_ref[rows, :] = dgv
            du_ref[rows, :] = duv
            dh[rows, :] += _dot(dgv, wg_ref[...]) + _dot(duv, wu_ref[...])

        @pl.when(s == N_SHARDS - 1)
        def _():
            dx, dn = _rms_bwd(dh[...], x_ref[...], g_ref[...])
            dx_ref[...] = dxo_ref[...] + dx
            dn_ref[...] += dn

    row_spec = pl.BlockSpec((tm, D_MODEL), lambda t, s: (t, 0))
    vec_spec = pl.BlockSpec((1, D_MODEL), lambda t, s: (0, 0))
    row_w = pl.BlockSpec((None, FF_SHARD, D_MODEL), lambda t, s: (s, 0, 0))
    act_spec = pl.BlockSpec((None, tm, FF_SHARD), lambda t, s: (s, t, 0))
    act_shape = _sds((N_SHARDS, t_len, FF_SHARD), BF16)
    return _pcall(
        body, name=name, grid=(t_len // tm, N_SHARDS),
        in_specs=[row_spec, row_spec, vec_spec, act_spec, act_spec, row_w, row_w, row_w],
        out_specs=[act_spec, act_spec, row_spec, vec_spec],
        out_shape=[act_shape, act_shape, _sds((t_len, D_MODEL), F32), _sds((1, D_MODEL), F32)],
        scratch_shapes=[pltpu.VMEM((tm, D_MODEL), BF16), pltpu.VMEM((tm, D_MODEL), F32)],
        semantics=("arbitrary", "arbitrary"), duty=duty,
    )(dxo, x, g_row, gate, up, w_gate_t, w_up_t, w_down)


def tn_matmul(name, a, b, a_spec, b_spec, out_shape, out_spec, grid, duty=None):
    def body(a_ref, b_ref, o_ref):
        @pl.when(pl.program_id(len(grid) - 1) == 0)
        def _():
            o_ref[...] = jnp.zeros_like(o_ref)

        res = _dot_tn(a_ref[...].astype(BF16), b_ref[...].astype(BF16))
        o_ref[...] += res.reshape(o_ref.shape)

    return _pcall(body, name=name, grid=grid, in_specs=[a_spec, b_spec], out_specs=out_spec,
                  out_shape=out_shape, semantics=("arbitrary",) * len(grid), duty=duty)(a, b)


def wgrad_full(name, a, b, duty=None):
    t_len, k = a.shape
    n = b.shape[1]
    tt = _wgrad_rows(t_len)
    return tn_matmul(name, a, b,
                     pl.BlockSpec((tt, k), lambda t: (t, 0)), pl.BlockSpec((tt, n), lambda t: (t, 0)),
                     _sds((k, n), F32), pl.BlockSpec((k, n), lambda t: (0, 0)), (t_len // tt,), duty)


def wgrad_col_sharded(name, a, b_sh, duty=None):
    t_len, k = a.shape
    n_sh, _, n = b_sh.shape
    tt = _wgrad_rows(t_len)
    return tn_matmul(name, a, b_sh,
                     pl.BlockSpec((tt, k), lambda s, t: (t, 0)), pl.BlockSpec((None, tt, n), lambda s, t: (s, t, 0)),
                     _sds((n_sh, k, n), F32), pl.BlockSpec((None, k, n), lambda s, t: (s, 0, 0)),
                     (n_sh, t_len // tt), duty)


def wgrad_row_sharded(name, a_sh, b, duty=None):
    t_len, n = b.shape
    n_sh, _, k = a_sh.shape
    tt = _wgrad_rows(t_len)

    def body(a_ref, b_ref, o_ref):
        s = pl.program_id(1)
        res = _dot_tn(a_ref[...], b_ref[...].astype(BF16))

        @pl.when(pl.program_id(0) == 0)
        def _():
            o_ref[s] = res

        @pl.when(pl.program_id(0) > 0)
        def _():
            o_ref[s] += res

    return _pcall(body, name=name, grid=(t_len // tt, n_sh),
                  in_specs=[pl.BlockSpec((None, tt, k), lambda t, s: (s, t, 0)),
                            pl.BlockSpec((tt, n), lambda t, s: (t, 0))],
                  out_specs=pl.BlockSpec((n_sh, k, n), lambda t, s: (0, 0, 0)),
                  out_shape=_sds((n_sh, k, n), F32), semantics=("arbitrary", "arbitrary"), duty=duty)(a_sh, b)


def wgrad_pool_group(name, p, dzs, duty=None):
    t_len = p.shape[0]
    tt = _wgrad_rows(t_len)
    gd = POOL_GROUP_DIM
    rows = gd // N_SHARDS
    return tn_matmul(name, p, dzs,
                     pl.BlockSpec((tt, gd), lambda g, t: (t, g)), pl.BlockSpec((tt, gd), lambda g, t: (t, g)),
                     _sds((N_SHARDS, 4, rows, gd), F32),
                     pl.BlockSpec((N_SHARDS, None, rows, gd), lambda g, t: (0, g, 0, 0)),
                     (4, t_len // tt), duty)


PAD_LANES = 384
QKV_PAD = 3 * PAD_LANES
N_SLABS = QKV_PAD // LANE
GROUP_REAL = tuple(GROUP_LANES[g + 1] - GROUP_LANES[g] for g in range(3))
Q_BLOCK = 512


def pad_groups(w, axis):
    parts = []
    for g in range(3):
        blk = lax.slice_in_dim(w, GROUP_LANES[g], GROUP_LANES[g + 1], axis=axis)
        pad = [(0, 0)] * w.ndim
        pad[axis] = (0, PAD_LANES - GROUP_REAL[g])
        parts.append(jnp.pad(blk, pad))
    return parts


def unpad_groups(parts, axis):
    return jnp.concatenate([lax.slice_in_dim(p, 0, GROUP_REAL[g], axis=axis) for g, p in enumerate(parts)],
                           axis=axis)


def pad_qkv_weight(w_qkv_sh):
    w = jnp.transpose(w_qkv_sh, (1, 0, 2)).reshape(D_MODEL, 3 * D_MODEL)
    q, k, v = (pad_groups(w[:, i * D_MODEL:(i + 1) * D_MODEL], 1) for i in range(3))
    return jnp.stack([jnp.concatenate([q[g], k[g], v[g]], axis=1) for g in range(3)])


def unpad_qkv_grad(g_pad):
    cols = [unpad_groups([g_pad[g][:, i * PAD_LANES:(i + 1) * PAD_LANES] for g in range(3)], 1) for i in range(3)]
    w = jnp.concatenate(cols, axis=1)
    return jnp.transpose(w.reshape(D_MODEL, N_SHARDS, QKV_SHARD), (1, 0, 2))


def rope_tables(t_len):
    inv_freq = 1.0 / (ROPE_THETA ** (jnp.arange(0, HEAD_DIM, 2, dtype=F32) / HEAD_DIM))
    ang = jnp.arange(t_len, dtype=F32)[:, None] * inv_freq[None, :]
    cos_h, sin_h = lax.optimization_barrier((jnp.cos(ang), jnp.sin(ang)))
    reps = (1, 2 * LANE // HEAD_DIM)
    return jnp.tile(cos_h, reps), jnp.tile(sin_h, reps)


def _rot_half(v):
    n = v.shape[1]
    lane = lax.broadcasted_iota(jnp.int32, v.shape, 1)
    return jnp.where(lane % HEAD_DIM < HEAD_DIM // 2,
                     -pltpu.roll(v, n - HEAD_DIM // 2, 1), pltpu.roll(v, HEAD_DIM // 2, 1))


def _lane_cols(j):
    return slice(j * LANE, (j + 1) * LANE)


def _to_residue_major(slab, j_src, dst_ref, j_dst, dil, rows):
    for r in range(dil):
        dst_ref[r, :, _lane_cols(j_dst)] = slab[j_src, pl.ds(r, rows // dil, stride=dil), :].astype(dst_ref.dtype)


def _to_natural(src_ref, j_src, slab, j_dst, dil, rows):
    for r in range(dil):
        slab[j_dst, pl.ds(r, rows // dil, stride=dil), :] = src_ref[r, :, _lane_cols(j_src)].astype(F32)


def qkv_fwd(x, g_row, w_pad, cos, sin):
    t_len = x.shape[0]
    tm = _tile_rows(t_len)

    def body(x_ref, g_ref, w_ref, cos_ref, sin_ref, h_ref, o1_ref, o4_ref, o16_ref, slabs):
        h = _rms_fwd(x_ref[...], g_ref[...]).astype(BF16)
        h_ref[...] = h
        accs = [_dot(h, w_ref[gi]) for gi in range(3)]
        cos_t = cos_ref[...]
        sin_t = sin_ref[...]
        for gi, (dil, o_ref) in enumerate(zip(GROUP_DIL, (o1_ref, o4_ref, o16_ref))):
            slab = slabs.at[gi]
            for j in range(N_SLABS):
                a = accs[gi][:, _lane_cols(j)]
                if j < 6:
                    a = a * cos_t + _rot_half(a) * sin_t
                if j < 3:
                    a = a * (HEAD_DIM ** -0.5)
                if dil == 1:
                    o_ref[0, :, _lane_cols(j)] = a.astype(BF16)
                else:
                    slab[j] = a
                    _to_residue_major(slab, j, o_ref, j, dil, tm)

    row_spec = pl.BlockSpec((tm, D_MODEL), lambda t: (t, 0))
    tab_spec = pl.BlockSpec((tm, LANE), lambda t: (t, 0))
    out_specs = [row_spec] + [pl.BlockSpec((d, tm // d, QKV_PAD), lambda t: (0, t, 0)) for d in GROUP_DIL]
    out_shape = [_sds((t_len, D_MODEL), BF16)] + [_sds((d, t_len // d, QKV_PAD), BF16) for d in GROUP_DIL]
    return _pcall(
        body, name="qkv_fwd", grid=(t_len // tm,),
        in_specs=[row_spec, pl.BlockSpec((1, D_MODEL), lambda t: (0, 0)),
                  pl.BlockSpec((3, D_MODEL, QKV_PAD), lambda t: (0, 0, 0)), tab_spec, tab_spec],
        out_specs=out_specs, out_shape=out_shape,
        scratch_shapes=[pltpu.VMEM((3, N_SLABS, tm, LANE), F32)],
        semantics=("arbitrary",),
    )(x, g_row, w_pad, cos, sin)


def _band_mask(n):
    qi = lax.broadcasted_iota(jnp.int32, (ATTN_W, 2 * ATTN_W), 0)
    kj = lax.broadcasted_iota(jnp.int32, (ATTN_W, 2 * ATTN_W), 1)
    dist = ATTN_W + qi - kj
    return (dist >= 0) & (dist <= ATTN_W) & ((kj >= ATTN_W) | (n > 0))


def _half_masks():
    lane = lax.broadcasted_iota(jnp.int32, (1, LANE), 1)
    return [lane < HEAD_DIM, lane >= HEAD_DIM]


def _live_halves(gi, j):
    hms = _half_masks()
    return hms if (gi == 0 or j < 2) else hms[:1]


def attn_fwd(qkv_g, gi, name):
    dil, l_len, _ = qkv_g.shape
    qb = min(Q_BLOCK, l_len)
    nsub = qb // ATTN_W

    def body(q_ref, kc_ref, kp_ref, vc_ref, vp_ref, o_ref, lse_ref, kbuf, vbuf):
        n = pl.program_id(1)
        kbuf[pl.ds(0, ATTN_W), :] = kp_ref[...]
        kbuf[pl.ds(ATTN_W, qb), :] = kc_ref[...]
        vbuf[pl.ds(0, ATTN_W), :] = vp_ref[...]
        vbuf[pl.ds(ATTN_W, qb), :] = vc_ref[...]

        def sub(b, carry):
            r0 = pl.multiple_of(b * ATTN_W, ATTN_W)
            mask = _band_mask(n + b)
            krows = pl.ds(r0, 2 * ATTN_W)
            scores = []
            for j in range(3):
                q = q_ref[pl.ds(r0, ATTN_W), _lane_cols(j)]
                for hm in _live_halves(gi, j):
                    scores.append(_dot_nt(jnp.where(hm, q, jnp.zeros_like(q)), kbuf[krows, _lane_cols(j)]))
            scores = iter(scores)
            for j in range(3):
                cols = _lane_cols(j)
                v = vbuf[krows, cols]
                o = jnp.zeros((ATTN_W, LANE), F32)
                lse = jnp.zeros((ATTN_W, LANE), F32)
                for hm in _live_halves(gi, j):
                    s = jnp.where(mask, next(scores), NEG_INF)
                    m = jnp.max(s, axis=-1, keepdims=True)
                    e = jnp.exp(s - m)
                    den = jnp.sum(e, axis=-1, keepdims=True)
                    p = (e * (1.0 / den)).astype(BF16)
                    o = jnp.where(hm, _dot(p, v), o)
                    lse = jnp.where(hm, m + jnp.log(den), lse)
                o_ref[pl.ds(r0, ATTN_W), cols] = o.astype(BF16)
                lse_ref[pl.ds(r0, ATTN_W), cols] = lse
            return carry

        lax.fori_loop(0, nsub, sub, 0)

    cur = lambda c: pl.BlockSpec((None, qb, PAD_LANES), lambda r, n: (r, n, c))
    prev = lambda c: pl.BlockSpec((None, ATTN_W, PAD_LANES), lambda r, n: (r, jnp.maximum(n * nsub - 1, 0), c))
    out_spec = pl.BlockSpec((None, qb, PAD_LANES), lambda r, n: (r, n, 0))
    return _pcall(
        body, name=name, grid=(dil, l_len // qb),
        in_specs=[cur(0), cur(1), prev(1), cur(2), prev(2)],
        out_specs=[out_spec, out_spec],
        out_shape=[_sds((dil, l_len, PAD_LANES), BF16), _sds((dil, l_len, PAD_LANES), F32)],
        scratch_shapes=[pltpu.VMEM((qb + ATTN_W, PAD_LANES), BF16), pltpu.VMEM((qb + ATTN_W, PAD_LANES), BF16)],
        semantics=("arbitrary", "arbitrary"),
    )(qkv_g, qkv_g, qkv_g, qkv_g, qkv_g)


def _group_stats(lse):
    lane = lax.broadcasted_iota(jnp.int32, (1, QKV_PAD), 1)
    gmask = [(lane >= g * PAD_LANES) & (lane < g * PAD_LANES + GROUP_REAL[g]) for g in range(3)]
    lses, glse = [], []
    for g in range(3):
        mx = jnp.max(jnp.where(gmask[g], lse, -jnp.inf), axis=-1, keepdims=True)
        sm = jnp.sum(jnp.where(gmask[g], jnp.exp(lse - mx), 0.0), axis=-1, keepdims=True) / HEAD_DIM
        full = mx + jnp.log(sm)
        lses.append(full)
        glse.append(full - math.log(GROUP_HEADS[g]))
    top = jnp.maximum(jnp.maximum(glse[0], glse[1]), glse[2])
    ex = [jnp.exp(v - top) for v in glse]
    tot = ex[0] + ex[1] + ex[2]
    alpha = [v / tot for v in ex]
    pick = lambda vals: jnp.where(lane < PAD_LANES, vals[0], jnp.where(lane < 2 * PAD_LANES, vals[1], vals[2]))
    return gmask, alpha, pick([3.0 * a for a in alpha]), lse - pick(lses), pick


def attn_out_fwd(x, o_parts, lse_parts, w_out_pad):
    t_len = x.shape[0]
    tm = _tile_rows(t_len)

    def body(x_ref, o1, o4, o16, l1, l4, l16, w_ref, xo_ref, mg_ref, o_ref, lse_ref, o_slab, l_slab):
        for gi, (dil, og, lg) in enumerate(zip(GROUP_DIL, (o1, o4, o16), (l1, l4, l16))):
            for j in range(3):
                _to_natural(og, j, o_slab, 3 * gi + j, dil, tm)
                _to_natural(lg, j, l_slab, 3 * gi + j, dil, tm)
        o = jnp.concatenate([o_slab[j] for j in range(N_SLABS)], axis=1)
        lse = jnp.concatenate([l_slab[j] for j in range(N_SLABS)], axis=1)
        o_ref[...] = o.astype(BF16)
        lse_ref[...] = lse
        _, _, scale, _, _ = _group_stats(lse)
        merged = (o * scale).astype(BF16)
        mg_ref[...] = merged
        xo_ref[...] = x_ref[...] + _dot(merged, w_ref[...])

    row_spec = pl.BlockSpec((tm, D_MODEL), lambda t: (t, 0))
    pad_spec = pl.BlockSpec((tm, QKV_PAD), lambda t: (t, 0))
    part_specs = [pl.BlockSpec((d, tm // d, PAD_LANES), lambda t: (0, t, 0)) for d in GROUP_DIL]
    return _pcall(
        body, name="attn_out_fwd", grid=(t_len // tm,),
        in_specs=[row_spec] + part_specs + part_specs + [pl.BlockSpec((QKV_PAD, D_MODEL), lambda t: (0, 0))],
        out_specs=[row_spec, pad_spec, pad_spec, pad_spec],
        out_shape=[_sds((t_len, D_MODEL), F32), _sds((t_len, QKV_PAD), BF16),
                   _sds((t_len, QKV_PAD), BF16), _sds((t_len, QKV_PAD), F32)],
        scratch_shapes=[pltpu.VMEM((N_SLABS, tm, LANE), F32), pltpu.VMEM((N_SLABS, tm, LANE), F32)],
        semantics=("arbitrary",),
    )(x, *o_parts, *lse_parts, w_out_pad)


def attn_out_bwd(dxo, w_out_pad, o, lse, duty=None):
    t_len = dxo.shape[0]
    tm = _tile_rows(t_len)

    def body(dx_ref, w_ref, o_ref, lse_ref, d1, d4, d16, c1, c4, c16, slab):
        dmerged = _dot_nt(dx_ref[...].astype(BF16), w_ref[...])
        o_t = o_ref[...].astype(F32)
        gmask, alpha, scale, lse_rel, pick = _group_stats(lse_ref[...])
        e = dmerged * o_t
        dalpha = [3.0 * jnp.sum(jnp.where(gmask[g], e, 0.0), axis=-1, keepdims=True) for g in range(3)]
        mean_da = alpha[0] * dalpha[0] + alpha[1] * dalpha[1] + alpha[2] * dalpha[2]
        dglse = [alpha[g] * (dalpha[g] - mean_da) for g in range(3)]
        dlse = pick(dglse) * jnp.exp(lse_rel)
        do = dmerged * scale
        es = e * scale
        lane = lax.broadcasted_iota(jnp.int32, (1, LANE), 1)
        first = lane < HEAD_DIM
        for j in range(N_SLABS):
            slab[j] = do[:, _lane_cols(j)]
        for gi, (dil, dg) in enumerate(zip(GROUP_DIL, (d1, d4, d16))):
            for j in range(3):
                _to_residue_major(slab, 3 * gi + j, dg, j, dil, tm)
        for j in range(N_SLABS):
            blk = es[:, _lane_cols(j)]
            s0 = jnp.sum(jnp.where(first, blk, 0.0), axis=-1, keepdims=True)
            s1 = jnp.sum(jnp.where(first, 0.0, blk), axis=-1, keepdims=True)
            slab[j] = jnp.where(first, s0, s1) - dlse[:, _lane_cols(j)]
        for gi, (dil, cg) in enumerate(zip(GROUP_DIL, (c1, c4, c16))):
            for j in range(3):
                _to_residue_major(slab, 3 * gi + j, cg, j, dil, tm)

    row_spec = pl.BlockSpec((tm, D_MODEL), lambda t: (t, 0))
    pad_spec = pl.BlockSpec((tm, QKV_PAD), lambda t: (t, 0))
    part_specs = [pl.BlockSpec((d, tm // d, PAD_LANES), lambda t: (0, t, 0)) for d in GROUP_DIL]
    shapes = lambda dt: [_sds((d, t_len // d, PAD_LANES), dt) for d in GROUP_DIL]
    outs = _pcall(
        body, name="attn_out_bwd", grid=(t_len // tm,),
        in_specs=[row_spec, pl.BlockSpec((QKV_PAD, D_MODEL), lambda t: (0, 0)), pad_spec, pad_spec],
        out_specs=part_specs + part_specs,
        out_shape=shapes(BF16) + shapes(F32),
        scratch_shapes=[pltpu.VMEM((N_SLABS, tm, LANE), F32)],
        semantics=("arbitrary",), duty=duty,
    )(dxo, w_out_pad, o, lse)
    if duty is None:
        return outs[:3], outs[3:]
    return (outs[0][:3], outs[0][3:]), outs[1]


def attn_bwd(qkv_g, do_g, lse_g, c_g, gi, name, duty=None):
    dil, l_len, _ = qkv_g.shape
    qb = min(Q_BLOCK, l_len)
    nsub = qb // ATTN_W
    nsb = l_len // qb

    def body(q_ref, kc_ref, kp_ref, vc_ref, vp_ref, do_ref, lse_ref, c_ref,
             qn_ref, don_ref, lsen_ref, cn_ref, o_ref, kbuf, vbuf, dkbuf, dvbuf):
        n = pl.program_id(1)
        kbuf[pl.ds(0, ATTN_W), :] = kp_ref[...]
        kbuf[pl.ds(ATTN_W, qb), :] = kc_ref[...]
        vbuf[pl.ds(0, ATTN_W), :] = vp_ref[...]
        vbuf[pl.ds(ATTN_W, qb), :] = vc_ref[...]
        dkbuf[...] = jnp.zeros_like(dkbuf)
        dvbuf[...] = jnp.zeros_like(dvbuf)

        def block(q_of, do_of, lse_of, c_of, krows, mask, dq_rows):
            heads = []
            for j in range(3):
                cols = _lane_cols(j)
                q, do_t, k, v = q_of(cols), do_of(cols), kbuf[krows, cols], vbuf[krows, cols]
                for hm in _live_halves(gi, j):
                    qh = jnp.where(hm, q, jnp.zeros_like(q))
                    doh = jnp.where(hm, do_t, jnp.zeros_like(do_t))
                    heads.append((j, hm, qh, doh, _dot_nt(qh, k), _dot_nt(doh, v)))
            for j in range(3):
                cols = _lane_cols(j)
                k = kbuf[krows, cols]
                dq = jnp.zeros((ATTN_W, LANE), F32)
                dk = jnp.zeros((k.shape[0], LANE), F32)
                dv = jnp.zeros((k.shape[0], LANE), F32)
                for hj, hm, qh, doh, s, dp in heads:
                    if hj != j:
                        continue
                    lse_h = jnp.max(jnp.where(hm, lse_of(cols), -jnp.inf), axis=-1, keepdims=True)
                    c_h = jnp.max(jnp.where(hm, c_of(cols), -jnp.inf), axis=-1, keepdims=True)
                    p = jnp.exp(jnp.where(mask, s, NEG_INF) - lse_h)
                    ds = (p * (dp - c_h)).astype(BF16)
                    if dq_rows is not None:
                        dq = jnp.where(hm, _dot(ds, k), dq)
                    dk = dk + _dot_tn(ds, qh)
                    dv = dv + _dot_tn(p.astype(BF16), doh)
                if dq_rows is not None:
                    o_ref[dq_rows, cols] = dq.astype(BF16)
                dkbuf[krows, cols] += dk
                dvbuf[krows, cols] += dv

        def sub(b, carry):
            rows = pl.ds(pl.multiple_of(b * ATTN_W, ATTN_W), ATTN_W)
            krows = pl.ds(pl.multiple_of(b * ATTN_W, ATTN_W), 2 * ATTN_W)
            block(lambda c: q_ref[rows, c], lambda c: do_ref[rows, c], lambda c: lse_ref[rows, c],
                  lambda c: c_ref[rows, c], krows, _band_mask(n + b), rows)
            return carry

        lax.fori_loop(0, nsub, sub, 0)

        qi = lax.broadcasted_iota(jnp.int32, (ATTN_W, ATTN_W), 0)
        kj = lax.broadcasted_iota(jnp.int32, (ATTN_W, ATTN_W), 1)
        nmask = (qi <= kj) & (n < nsb - 1)
        block(lambda c: qn_ref[:, c], lambda c: don_ref[:, c], lambda c: lsen_ref[:, c],
              lambda c: cn_ref[:, c], pl.ds(qb, ATTN_W), nmask, None)
        o_ref[:, pl.ds(PAD_LANES, PAD_LANES)] = dkbuf[pl.ds(ATTN_W, qb), :].astype(BF16)
        o_ref[:, pl.ds(2 * PAD_LANES, PAD_LANES)] = dvbuf[pl.ds(ATTN_W, qb), :].astype(BF16)

    cur = lambda c: pl.BlockSpec((None, qb, PAD_LANES), lambda r, n: (r, n, c))
    prev = lambda c: pl.BlockSpec((None, ATTN_W, PAD_LANES), lambda r, n: (r, jnp.maximum(n * nsub - 1, 0), c))
    nxt = pl.BlockSpec((None, ATTN_W, PAD_LANES), lambda r, n: (r, jnp.minimum((n + 1) * nsub, nsb * nsub - 1), 0))
    return _pcall(
        body, name=name, grid=(dil, nsb),
        in_specs=[cur(0), cur(1), prev(1), cur(2), prev(2), cur(0), cur(0), cur(0), nxt, nxt, nxt, nxt],
        out_specs=pl.BlockSpec((None, qb, QKV_PAD), lambda r, n: (r, n, 0)),
        out_shape=_sds((dil, l_len, QKV_PAD), BF16),
        scratch_shapes=[pltpu.VMEM((qb + ATTN_W, PAD_LANES), BF16), pltpu.VMEM((qb + ATTN_W, PAD_LANES), BF16),
                        pltpu.VMEM((qb + ATTN_W, PAD_LANES), F32), pltpu.VMEM((qb + ATTN_W, PAD_LANES), F32)],
        semantics=("arbitrary", "arbitrary"), duty=duty,
    )(qkv_g, qkv_g, qkv_g, qkv_g, qkv_g, do_g, lse_g, c_g, qkv_g, do_g, lse_g, c_g)


def qkv_bwd(dqkv_parts, w_pad, dxo, x, g_row, cos, sin):
    t_len = x.shape[0]
    tm = _tile_rows(t_len)

    def body(p1, p4, p16, w_ref, dxo_ref, x_ref, g_ref, cos_ref, sin_ref, dq_ref, dx_ref, dn_ref, slabs):
        @pl.when(pl.program_id(0) == 0)
        def _():
            dn_ref[...] = jnp.zeros_like(dn_ref)

        cos_t = cos_ref[...]
        sin_t = sin_ref[...]
        dh = None
        for gi, (dil, part) in enumerate(zip(GROUP_DIL, (p1, p4, p16))):
            slab = slabs.at[gi]
            for j in range(N_SLABS):
                if dil == 1:
                    a = part[0, :, _lane_cols(j)].astype(F32)
                else:
                    _to_natural(part, j, slab, j, dil, tm)
                    a = slab[j]
                if j < 6:
                    a = a * cos_t - _rot_half(a * sin_t)
                if j < 3:
                    a = a * (HEAD_DIM ** -0.5)
                dq_ref[gi, :, _lane_cols(j)] = a.astype(BF16)
            contrib = _dot_nt(dq_ref[gi], w_ref[gi])
            dh = contrib if dh is None else dh + contrib
        dx, dn = _rms_bwd(dh, x_ref[...], g_ref[...])
        dx_ref[...] = dxo_ref[...] + dx
        dn_ref[...] += dn

    row_spec = pl.BlockSpec((tm, D_MODEL), lambda t: (t, 0))
    vec_spec = pl.BlockSpec((1, D_MODEL), lambda t: (0, 0))
    tab_spec = pl.BlockSpec((tm, LANE), lambda t: (t, 0))
    part_specs = [pl.BlockSpec((d, tm // d, QKV_PAD), lambda t: (0, t, 0)) for d in GROUP_DIL]
    return _pcall(
        body, name="qkv_bwd", grid=(t_len // tm,),
        in_specs=part_specs + [pl.BlockSpec((3, D_MODEL, QKV_PAD), lambda t: (0, 0, 0)),
                               row_spec, row_spec, vec_spec, tab_spec, tab_spec],
        out_specs=[pl.BlockSpec((3, tm, QKV_PAD), lambda t: (0, t, 0)), row_spec, vec_spec],
        out_shape=[_sds((3, t_len, QKV_PAD), BF16), _sds((t_len, D_MODEL), F32), _sds((1, D_MODEL), F32)],
        scratch_shapes=[pltpu.VMEM((3, N_SLABS, tm, LANE), F32)],
        semantics=("arbitrary",),
    )(*dqkv_parts, w_pad, dxo, x, g_row, cos, sin)


def pool_bwd(dxo, x, g_row, w_in, w_grp, scale, w_out, zr, duty=None):
    t_len = x.shape[0]
    tm = _tile_rows(t_len)
    nt = t_len // tm

    def body(dxo_ref, x_ref, g_ref, win_ref, wgrp_ref, scale_ref, wout_ref, zr_ref,
             dzs_ref, du_ref, dx_ref, dn_ref, dsc_ref, ebuf):
        i = pl.program_id(0)
        t = nt - 1 - i

        @pl.when(i == 0)
        def _():
            ebuf[pl.ds(tm, POOL_HALO), :] = jnp.zeros((POOL_HALO, D_MODEL), F32)
            dn_ref[...] = jnp.zeros_like(dn_ref)
            dsc_ref[...] = jnp.zeros_like(dsc_ref)

        dxo_t = dxo_ref[...]
        dz = _dot_nt(dxo_t.astype(BF16), wout_ref[...])
        dsc_ref[...] += jnp.sum(dz * zr_ref[...].astype(F32), axis=0, keepdims=True)
        dzs_ref[...] = (dz * scale_ref[...]).astype(BF16)
        row = t * tm + lax.broadcasted_iota(jnp.int32, (tm, 1), 0)
        for gi, w in enumerate(POOL_WINDOWS):
            cols = pl.ds(gi * POOL_GROUP_DIM, POOL_GROUP_DIM)
            dp_g = _dot_nt(dzs_ref[:, cols], wgrp_ref[gi])
            inv_cnt = 1.0 / jnp.minimum(row + 1, w).astype(F32)
            ebuf[pl.ds(0, tm), cols] = dp_g * inv_cnt
            acc = -dp_g
            for j in range(w):
                acc = acc + ebuf[pl.ds(j, tm), cols]
            du_ref[:, cols] = acc.astype(BF16)
        ebuf[pl.ds(tm, POOL_HALO), :] = ebuf[pl.ds(0, POOL_HALO), :]
        dh = _dot_nt(du_ref[...], win_ref[...])
        dx, dn = _rms_bwd(dh, x_ref[...], g_ref[...])
        dx_ref[...] = dxo_t + dx
        dn_ref[...] += dn

    row_spec = pl.BlockSpec((tm, D_MODEL), lambda i: (nt - 1 - i, 0))
    full = lambda shape: pl.BlockSpec(shape, lambda i: (0,) * len(shape))
    vec = full((1, D_MODEL))
    return _pcall(
        body, name="pool_bwd", grid=(nt,),
        in_specs=[row_spec, row_spec, vec, full((D_MODEL, D_MODEL)), full((4, POOL_GROUP_DIM, POOL_GROUP_DIM)),
                  vec, full((D_MODEL, D_MODEL)), row_spec],
        out_specs=[row_spec, row_spec, row_spec, vec, vec],
        out_shape=[_sds((t_len, D_MODEL), BF16), _sds((t_len, D_MODEL), BF16), _sds((t_len, D_MODEL), F32),
                   _sds((1, D_MODEL), F32), _sds((1, D_MODEL), F32)],
        scratch_shapes=[pltpu.VMEM((tm + POOL_HALO, D_MODEL), F32)],
        semantics=("arbitrary",), duty=duty,
    )(dxo, x, g_row, w_in, w_grp, scale, w_out, zr)


def _mesh_pos():
    return lax.axis_index("x"), lax.axis_index("y"), lax.axis_index("c")


def _other_chips(x, y):
    return [(1 - x, y), (x, 1 - y), (1 - x, 1 - y)]


def _remote(src, dst, send_sem, recv_sem, device):
    return pltpu.make_async_remote_copy(src_ref=src, dst_ref=dst, send_sem=send_sem, recv_sem=recv_sem,
                                        device_id=device, device_id_type=MESH)


class _Duty:
    aliases = {}

    def mid(self, ins, outs, sems):
        pass


class Together(_Duty):
    def __init__(self, duties):
        self.duties = duties
        self.ins = [a for d in duties for a in d.ins]
        self.out_shape = [s for d in duties for s in d.out_shape]
        self.scratch = [s for d in duties for s in d.scratch]
        self.aliases = {}
        i0 = o0 = 0
        for d in duties:
            self.aliases.update({i0 + i: o0 + o for i, o in d.aliases.items()})
            i0 += len(d.ins)
            o0 += len(d.out_shape)

    def _each(self, ins, outs, sems):
        i0 = o0 = s0 = 0
        for d in self.duties:
            ni, no, ns = len(d.ins), len(d.out_shape), len(d.scratch)
            yield d, ins[i0:i0 + ni], outs[o0:o0 + no], sems[s0:s0 + ns]
            i0, o0, s0 = i0 + ni, o0 + no, s0 + ns

    def split(self, outs):
        return [list(o) for _, _, o, _ in self._each(self.ins, outs, self.scratch)]

    def start(self, ins, outs, sems):
        for d, i, o, s in self._each(ins, outs, sems):
            d.start(i, o, s)

    def mid(self, ins, outs, sems):
        for d, i, o, s in self._each(ins, outs, sems):
            d.mid(i, o, s)

    def finish(self, ins, outs, sems):
        for d, i, o, s in self._each(ins, outs, sems):
            d.finish(i, o, s)


def run_duty(duty, name):
    d_in, d_out = len(duty.ins), len(duty.out_shape)

    def body(*refs):
        ins, outs, sems = refs[:d_in], refs[d_in:d_in + d_out], refs[d_in + d_out:]
        duty.start(ins, outs, sems)
        duty.mid(ins, outs, sems)
        duty.finish(ins, outs, sems)

    return pl.pallas_call(
        body, name=name, out_shape=list(duty.out_shape), in_specs=[_ANY] * d_in, out_specs=[_ANY] * d_out,
        scratch_shapes=list(duty.scratch), input_output_aliases=dict(duty.aliases),
        compiler_params=pltpu.CompilerParams(has_side_effects=True),
    )(*duty.ins)


class GatherWeights(_Duty):
    def __init__(self, shards):
        n = self.n = len(shards)
        self.halves = [s.shape[0] // 2 for s in shards]
        my_slot = 2 * lax.axis_index("x") + lax.axis_index("y")
        staged = [lax.dynamic_update_slice(lax.empty((N_SHARDS,) + s.shape, s.dtype), s[None], (my_slot, 0, 0))
                  for s in shards]
        self.ins = list(shards) + staged
        self.out_shape = [_sds((N_SHARDS,) + s.shape, s.dtype) for s in shards]
        self.aliases = {n + a: a for a in range(n)}
        self.scratch = [pltpu.SemaphoreType.DMA((n, 6)), pltpu.SemaphoreType.DMA((n, 6))]

    def _over_ici(self, ins, outs, sems):
        x, y, c = _mesh_pos()
        return [_remote(ins[a].at[pl.ds(c * h, h)], outs[a].at[2 * x + y, pl.ds(c * h, h)],
                        sems[0].at[a, j], sems[1].at[a, j], (*chip, c))
                for a, h in enumerate(self.halves) for j, chip in enumerate(_other_chips(x, y))]

    def _forwards(self, outs, sems, half_of):
        x, y, c = _mesh_pos()
        cps = []
        for a, h in enumerate(self.halves):
            for j, chip in enumerate(_other_chips(x, y)):
                slot = outs[a].at[2 * chip[0] + chip[1], pl.ds(half_of(c) * h, h)]
                cps.append(_remote(slot, slot, sems[0].at[a, 3 + j], sems[1].at[a, 3 + j], (x, y, 1 - c)))
        return cps

    def start(self, ins, outs, sems):
        for cp in self._over_ici(ins, outs, sems):
            cp.start()

    def mid(self, ins, outs, sems):
        x, y, c = _mesh_pos()
        forwards = self._forwards(outs, sems, lambda core: core)
        k = 0
        for a, h in enumerate(self.halves):
            for j, chip in enumerate(_other_chips(x, y)):
                slot = outs[a].at[2 * chip[0] + chip[1], pl.ds(c * h, h)]
                _remote(slot, slot, sems[0].at[a, j], sems[1].at[a, j], (*chip, c)).wait_recv()
                forwards[k].start()
                k += 1

    def finish(self, ins, outs, sems):
        for cp in self._forwards(outs, sems, lambda core: 1 - core):
            cp.wait_recv()
        for cp in self._over_ici(ins, outs, sems) + self._forwards(outs, sems, lambda core: core):
            cp.wait_send()


class GradReducer:
    def __init__(self, c_idx, pos_idx):
        self.c_idx, self.pos_idx = c_idx, pos_idx
        self.in_flight = []
        self.done = {}

    def push(self, name, grad):
        self.in_flight.append(dict(name=name, stage="halves", data=grad))

    def _duties(self):
        make = {"halves": SiblingHalves, "exchange": ChipExchange, "share": SiblingShare}
        return Together([make[w["stage"]]([w["data"]]) for w in self.in_flight])

    def _advance(self, duties, outs):
        still = []
        for w, (res,) in zip(self.in_flight, duties.split(outs)):
            if w["stage"] == "halves":
                partial = add_my_half(w["data"], res, self.c_idx, f"rs_add_{w['name']}")
                still.append(dict(name=w["name"], stage="exchange", data=partial))
            elif w["stage"] == "exchange":
                reduced = sum_slots(res, w["data"], self.pos_idx, f"rs_sum_{w['name']}")
                still.append(dict(name=w["name"], stage="share", data=reduced))
            else:
                self.done[w["name"]] = res
        self.in_flight = still

    def carried_by(self, fn, *args, **kw):
        if not self.in_flight:
            return fn(*args, **kw)
        duties = self._duties()
        out, duty_outs = fn(*args, duty=duties, **kw)
        self._advance(duties, duty_outs)
        return out

    def drain(self, name):
        step = 0
        while self.in_flight:
            duties = self._duties()
            self._advance(duties, run_duty(duties, f"{name}{step}"))
            step += 1


class SiblingHalves(_Duty):
    def __init__(self, grads):
        n = len(grads)
        self.halves = [g.shape[1] // 2 for g in grads]
        self.ins = list(grads)
        self.out_shape = [_sds((N_SHARDS, h, g.shape[2]), g.dtype) for g, h in zip(grads, self.halves)]
        self.scratch = [pltpu.SemaphoreType.DMA((n,)), pltpu.SemaphoreType.DMA((n,))]

    def _copies(self, ins, outs, sems):
        x, y, c = _mesh_pos()
        return [_remote(ins[a].at[:, pl.ds((1 - c) * h, h)], outs[a], sems[0].at[a], sems[1].at[a], (x, y, 1 - c))
                for a, h in enumerate(self.halves)]

    def start(self, ins, outs, sems):
        for cp in self._copies(ins, outs, sems):
            cp.start()

    def finish(self, ins, outs, sems):
        for cp in self._copies(ins, outs, sems):
            cp.wait()


class ChipExchange(_Duty):
    def __init__(self, parts):
        n = self.n = len(parts)
        self.ins = list(parts)
        self.out_shape = [_sds(p.shape, p.dtype) for p in parts]
        self.scratch = [pltpu.SemaphoreType.DMA((n, 3)), pltpu.SemaphoreType.DMA((n, 3))]

    def _copies(self, ins, outs, sems, arriving):
        x, y, c = _mesh_pos()
        cps = []
        for a in range(self.n):
            for j, chip in enumerate(_other_chips(x, y)):
                theirs = 2 * chip[0] + chip[1]
                src = outs[a].at[theirs] if arriving else ins[a].at[theirs]
                dst = outs[a].at[theirs] if arriving else outs[a].at[2 * x + y]
                cps.append(_remote(src, dst, sems[0].at[a, j], sems[1].at[a, j], (*chip, c)))
        return cps

    def start(self, ins, outs, sems):
        for cp in self._copies(ins, outs, sems, False):
            cp.start()

    def finish(self, ins, outs, sems):
        for cp in self._copies(ins, outs, sems, True):
            cp.wait_recv()
        for cp in self._copies(ins, outs, sems, False):
            cp.wait_send()


class SiblingShare(_Duty):
    def __init__(self, reduced):
        n = self.n = len(reduced)
        self.ins = list(reduced)
        self.out_shape = [_sds(r.shape, r.dtype) for r in reduced]
        self.aliases = {a: a for a in range(n)}
        self.scratch = [pltpu.SemaphoreType.DMA((n,)), pltpu.SemaphoreType.DMA((n,))]

    def _copies(self, outs, sems, half_of):
        x, y, c = _mesh_pos()
        cps = []
        for a in range(self.n):
            h = outs[a].shape[0] // 2
            rows = outs[a].at[pl.ds(half_of(c) * h, h)]
            cps.append(_remote(rows, rows, sems[0].at[a], sems[1].at[a], (x, y, 1 - c)))
        return cps

    def start(self, ins, outs, sems):
        for cp in self._copies(outs, sems, lambda core: core):
            cp.start()

    def finish(self, ins, outs, sems):
        for cp in self._copies(outs, sems, lambda core: 1 - core):
            cp.wait_recv()
        for cp in self._copies(outs, sems, lambda core: core):
            cp.wait_send()


def allreduce_small(v):
    def body(v_ref, o_ref, buf, send_sems, recv_sems):
        x, y, c = _mesh_pos()
        me = 4 * x + 2 * y + c
        buf[me] = v_ref[...]
        flip = lambda p, f: 1 - p if f else p
        peers = [(flip(x, k & 4), flip(y, k & 2), flip(c, k & 1)) for k in range(1, N_DEV)]
        cps = []
        for k, peer in enumerate(peers):
            cp = _remote(v_ref, buf.at[me], send_sems.at[k], recv_sems.at[k], peer)
            cp.start()
            cps.append(cp)
        for k, peer in enumerate(peers):
            slot = buf.at[4 * peer[0] + 2 * peer[1] + peer[2]]
            _remote(slot, slot, send_sems.at[k], recv_sems.at[k], peer).wait_recv()
        for cp in cps:
            cp.wait_send()
        acc = buf[0]
        for i in range(1, N_DEV):
            acc = acc + buf[i]
        o_ref[...] = acc

    vm = pl.BlockSpec(memory_space=pltpu.VMEM)
    return pl.pallas_call(
        body, name="allreduce_small", out_shape=_sds(v.shape, v.dtype), in_specs=[vm], out_specs=vm,
        scratch_shapes=[pltpu.VMEM((N_DEV,) + v.shape, v.dtype),
                        pltpu.SemaphoreType.DMA((N_DEV - 1,)), pltpu.SemaphoreType.DMA((N_DEV - 1,))],
        compiler_params=pltpu.CompilerParams(has_side_effects=True),
    )(v)


def add_my_half(grad, theirs, c_idx, name):
    _, r, cols = grad.shape
    h = r // 2

    def body(c_ref, g_ref, t_ref, o_ref):
        o_ref[...] = (g_ref[...] + t_ref[...]).astype(BF16)

    slot = pl.BlockSpec((None, h, cols), lambda s, c: (s, 0, 0))
    grid_spec = pltpu.PrefetchScalarGridSpec(
        num_scalar_prefetch=1, grid=(N_SHARDS,),
        in_specs=[pl.BlockSpec((None, h, cols), lambda s, c: (s, c[0], 0)), slot], out_specs=slot)
    return pl.pallas_call(
        body, name=name, grid_spec=grid_spec, out_shape=_sds((N_SHARDS, h, cols), BF16),
        compiler_params=pltpu.CompilerParams(dimension_semantics=("arbitrary",), vmem_limit_bytes=VMEM_LIMIT_BYTES),
    )(c_idx, grad, theirs)


def sum_slots(received, mine, pos_idx, name):
    _, h, cols = received.shape

    def body(pos_ref, r_ref, m_ref, o_ref):
        acc = None
        for k in range(N_SHARDS):
            term = jnp.where(pos_ref[0] == k, m_ref[k], r_ref[k]).astype(F32)
            acc = term if acc is None else acc + term
        o_ref[...] = acc

    whole = pl.BlockSpec((N_SHARDS, h, cols), lambda i, pos: (0, 0, 0))
    grid_spec = pltpu.PrefetchScalarGridSpec(
        num_scalar_prefetch=1, grid=(1,), in_specs=[whole, whole],
        out_specs=pl.BlockSpec((h, cols), lambda i, pos: (pos[1], 0)))
    return pl.pallas_call(
        body, name=name, grid_spec=grid_spec, out_shape=_sds((2 * h, cols), F32),
        compiler_params=pltpu.CompilerParams(dimension_semantics=("arbitrary",), vmem_limit_bytes=VMEM_LIMIT_BYTES),
    )(pos_idx, received, mine)


def adamw(name, grads, w, m, v):
    n_layers, r, cols = w.shape
    tr = r // 2 if r % 16 == 0 else r
    bias1 = 1.0 - ADAM_B1 ** ADAM_STEP
    bias2 = 1.0 - ADAM_B2 ** ADAM_STEP

    def body(*refs):
        g_refs = refs[:n_layers]
        w_ref, m_ref, v_ref, go_ref, d_ref, mo_ref, vo_ref = refs[n_layers:]
        g = g_refs[0][...]
        for layer in range(1, n_layers):
            g = jnp.where(pl.program_id(0) == layer, g_refs[layer][...], g)
        m_new = ADAM_B1 * m_ref[...] + (1.0 - ADAM_B1) * g
        v_new = ADAM_B2 * v_ref[...] + (1.0 - ADAM_B2) * (g * g)
        m_hat = m_new / bias1
        v_hat = v_new / bias2
        go_ref[...] = g
        d_ref[...] = -ADAM_LR * (m_hat / (jnp.sqrt(v_hat) + ADAM_EPS) + ADAM_WD * w_ref[...])
        mo_ref[...] = m_new
        vo_ref[...] = v_new

    g_spec = pl.BlockSpec((tr, cols), lambda l, i: (i, 0))
    lay_spec = pl.BlockSpec((None, tr, cols), lambda l, i: (l, i, 0))
    shape = _sds((n_layers, r, cols), F32)
    return _pcall(
        body, name=name, grid=(n_layers, r // tr),
        in_specs=[g_spec] * n_layers + [lay_spec] * 3, out_specs=[lay_spec] * 4,
        out_shape=[shape] * 4, semantics=("arbitrary", "arbitrary"),
    )(*grads, w, m, v)


def kernel(x, norm_mix, norm_ffn, norm_final, pool_w_in, pool_w_group, pool_scale, pool_w_out, attn_w_qkv, attn_w_out, ffn_w_gate, ffn_w_up, ffn_w_down, loss_target, m_norm_mix, m_norm_ffn, m_norm_final, m_pool_w_in, m_pool_w_group, m_pool_scale, m_pool_w_out, m_attn_w_qkv, m_attn_w_out, m_ffn_w_gate, m_ffn_w_up, m_ffn_w_down, v_norm_mix, v_norm_ffn, v_norm_final, v_pool_w_in, v_pool_w_group, v_pool_scale, v_pool_w_out, v_attn_w_qkv, v_attn_w_out, v_ffn_w_gate, v_ffn_w_up, v_ffn_w_down):
    t_len = x.shape[1]
    x0 = x.reshape(t_len, D_MODEL)
    target = loss_target.reshape(t_len, D_MODEL)
    row = lambda a: a.reshape(1, D_MODEL)

    grp_rows = POOL_GROUP_DIM // N_SHARDS
    bf = lambda a: a.astype(BF16)
    gate_t, up_t = jnp.swapaxes(ffn_w_gate, 1, 2), jnp.swapaxes(ffn_w_up, 1, 2)
    pool_shards = [bf(pool_w_in[0]), bf(pool_w_group[0].reshape(4 * grp_rows, POOL_GROUP_DIM)), bf(pool_w_out[0])]
    ffn0_shards = [bf(gate_t[0]), bf(up_t[0]), bf(ffn_w_down[0])]
    late_shards = [bf(attn_w_qkv[0]), bf(attn_w_out[0]), bf(gate_t[1]), bf(up_t[1]), bf(ffn_w_down[1])]
    cos, sin = rope_tables(t_len)
    c_idx = lax.axis_index("c").astype(jnp.int32).reshape(1)
    pos_idx = jnp.stack([2 * lax.axis_index("x") + lax.axis_index("y"), lax.axis_index("c")]).astype(jnp.int32)
    chip_rows = lambda g: g.reshape(N_SHARDS, D_MODEL // N_SHARDS, D_MODEL)

    g_pool = run_duty(GatherWeights(pool_shards), "gather_pool")
    w_in = g_pool[0].reshape(D_MODEL, D_MODEL)
    w_grp = g_pool[1].reshape(N_SHARDS, 4, grp_rows, POOL_GROUP_DIM).transpose(1, 0, 2, 3).reshape(
        4, POOL_GROUP_DIM, POOL_GROUP_DIM)
    w_out = g_pool[2].reshape(D_MODEL, D_MODEL)
    (h0, p, zr, z, x1), ffn0 = pool_fwd(x0, row(norm_mix[0]), w_in, w_grp, pool_scale, w_out,
                                        duty=GatherWeights(ffn0_shards))
    (h1, gate0, up0, act0, x2), late = ffn_fwd(x1, row(norm_ffn[0]), *ffn0, "ffn_fwd0",
                                               duty=GatherWeights(late_shards))
    w_qkv = pad_qkv_weight(late[0])
    w_ao = jnp.concatenate(pad_groups(late[1].reshape(D_MODEL, D_MODEL), 0), axis=0)
    ffn1 = late[2:5]
    h2, *qkv_parts = qkv_fwd(x2, row(norm_mix[1]), w_qkv, cos, sin)
    o_parts, lse_parts = [], []
    for gi in range(3):
        o_g, lse_g = attn_fwd(qkv_parts[gi], gi, f"attn_fwd_g{gi}")
        o_parts.append(o_g)
        lse_parts.append(lse_g)
    x3, merged, o_nat, lse_nat = attn_out_fwd(x2, o_parts, lse_parts, w_ao)
    h3, gate1, up1, act1, dx4, d_norm_final, loss_local = ffn_fwd(
        x3, row(norm_ffn[1]), *ffn1, "ffn_fwd1", loss_head=(row(norm_final), target))

    red = GradReducer(c_idx, pos_idx)
    dgate1, dup1, dx3, d_nf1 = ffn_bwd(dx4, x3, row(norm_ffn[1]), gate1, up1, *ffn1, "ffn_bwd1")
    red.push("gate1", wgrad_row_sharded("wgrad_gate1", dgate1, h3))
    red.push("up1", red.carried_by(wgrad_row_sharded, "wgrad_up1", dup1, h3))
    red.push("down1", red.carried_by(wgrad_row_sharded, "wgrad_down1", act1, dx4))
    do_parts, c_parts = red.carried_by(attn_out_bwd, dx3, w_ao, o_nat, lse_nat)
    g_ao = red.carried_by(wgrad_full, "wgrad_attn_out", merged, dx3)
    red.push("attn_out", chip_rows(unpad_groups(jnp.split(g_ao, 3, axis=0), 0)))
    dqkv_parts = [red.carried_by(attn_bwd, qkv_parts[gi], do_parts[gi], lse_parts[gi], c_parts[gi], gi,
                                 f"attn_bwd_g{gi}") for gi in range(3)]
    dqkv, dx2, d_nm1 = qkv_bwd(dqkv_parts, w_qkv, dx3, x2, row(norm_mix[1]), cos, sin)
    red.push("qkv", unpad_qkv_grad(wgrad_col_sharded("wgrad_qkv", h2, dqkv)))

    dgate0, dup0, dx1, d_nf0 = red.carried_by(ffn_bwd, dx2, x1, row(norm_ffn[0]), gate0, up0, *ffn0, "ffn_bwd0")
    red.push("gate0", red.carried_by(wgrad_row_sharded, "wgrad_gate0", dgate0, h1))
    red.push("up0", red.carried_by(wgrad_row_sharded, "wgrad_up0", dup0, h1))
    red.push("down0", red.carried_by(wgrad_row_sharded, "wgrad_down0", act0, dx2))
    red.push("pool_out", chip_rows(red.carried_by(wgrad_full, "wgrad_pool_out", z, dx1)))
    dzs, du, dx0, d_nm0, d_scale = pool_bwd(dx1, x0, row(norm_mix[0]), w_in, w_grp, pool_scale, w_out, zr)
    g_grp = red.carried_by(wgrad_pool_group, "wgrad_pool_group", p, dzs)
    red.push("pool_group", g_grp.reshape(N_SHARDS, 4 * grp_rows, POOL_GROUP_DIM))
    red.push("pool_in", chip_rows(red.carried_by(wgrad_full, "wgrad_pool_in", h0, du)))
    red.drain("rs_tail")
    full = [red.done[nm] for nm in ("pool_in", "pool_group", "pool_out", "qkv", "attn_out",
                                    "gate0", "gate1", "up0", "up1", "down0", "down1")]

    zero_row = jnp.zeros((1, D_MODEL), F32)
    small = jnp.concatenate([d_nm0, d_nm1, d_nf0, d_nf1, d_norm_final, d_scale,
                             jnp.broadcast_to(loss_local, (1, D_MODEL)), zero_row], axis=0)
    small = allreduce_small(small)
    loss = small[6, 0]

    pack = lambda a, b, c, d: jnp.concatenate([a, b, row(c), d, zero_row, zero_row], axis=0)[None]
    sg, sd, sm, sv = adamw("adamw_small", [small],
                           pack(norm_mix, norm_ffn, norm_final, pool_scale),
                           pack(m_norm_mix, m_norm_ffn, m_norm_final, m_pool_scale),
                           pack(v_norm_mix, v_norm_ffn, v_norm_final, v_pool_scale))
    unpack = lambda a: (a[0, 0:2], a[0, 2:4], a[0, 4], a[0, 5:6])

    def update(name, grads, w, m, v, transposed=False):
        if transposed:
            w, m, v = (jnp.swapaxes(a, 1, 2) for a in (w, m, v))
        n_layers = len(grads)
        shp = (n_layers,) + grads[0].shape
        outs = [o.reshape(w.shape) for o in adamw(name, grads, w.reshape(shp), m.reshape(shp), v.reshape(shp))]
        return [jnp.swapaxes(o, 1, 2) for o in outs] if transposed else outs

    big = [
        update("adamw_pool_in", [full[0]], pool_w_in, m_pool_w_in, v_pool_w_in),
        update("adamw_pool_group", [full[1]], pool_w_group, m_pool_w_group, v_pool_w_group),
        update("adamw_pool_out", [full[2]], pool_w_out, m_pool_w_out, v_pool_w_out),
        update("adamw_qkv", [full[3]], attn_w_qkv, m_attn_w_qkv, v_attn_w_qkv),
        update("adamw_attn_out", [full[4]], attn_w_out, m_attn_w_out, v_attn_w_out),
        update("adamw_gate", [full[5], full[6]], ffn_w_gate, m_ffn_w_gate, v_ffn_w_gate, transposed=True),
        update("adamw_up", [full[7], full[8]], ffn_w_up, m_ffn_w_up, v_ffn_w_up, transposed=True),
        update("adamw_down", [full[9], full[10]], ffn_w_down, m_ffn_w_down, v_ffn_w_down),
    ]

    def leaves(k, small_vals):
        nm, nf, nfin, psc = unpack(small_vals)
        return [nm, nf, nfin, big[0][k], big[1][k], psc, big[2][k], big[3][k], big[4][k],
                big[5][k], big[6][k], big[7][k]]

    grad_x = dx0.reshape(x.shape)
    return (loss, grad_x, *leaves(0, sg), *leaves(1, sd), *leaves(2, sm), *leaves(3, sv))
```

```python
import math

import jax
import jax.numpy as jnp
from jax import lax
from jax.experimental import pallas as pl
from jax.experimental.pallas import tpu as pltpu

F32 = jnp.float32
BF16 = jnp.bfloat16

D_MODEL = 1024
N_SHARDS = 4
N_DEV = 8
D_FF = 2816
FF_SHARD = D_FF // N_SHARDS
HEAD_DIM = 64
QKV_SHARD = 3 * D_MODEL // N_SHARDS
POOL_WINDOWS = (2, 4, 8, 16)
POOL_GROUP_DIM = 256
POOL_HALO = 16
ATTN_W = 128
GROUP_LANES = (0, 384, 704, 1024)
GROUP_HEADS = (6, 5, 5)
GROUP_DIL = (1, 4, 16)
ROPE_THETA = 10000.0
EPS = 1e-6
NEG_INF = -1e30
LANE = 128
VMEM_LIMIT_BYTES = 60 * 1024 * 1024

ADAM_LR = 0.001
ADAM_B1 = 0.9
ADAM_B2 = 0.999
ADAM_EPS = 1e-08
ADAM_WD = 0.01
ADAM_STEP = 10

NT_DIMS = (((1,), (1,)), ((), ()))
TN_DIMS = (((0,), (0,)), ((), ()))
MESH = pl.DeviceIdType.MESH


_ANY = pl.BlockSpec(memory_space=pl.ANY)


def _pcall(body, *, name, out_shape, grid=None, in_specs=None, out_specs=None, scratch_shapes=(),
           semantics=None, duty=None):
    kw = {}
    if in_specs is not None and duty is None:
        kw["in_specs"] = in_specs
    if out_specs is not None and duty is None:
        kw["out_specs"] = out_specs
    if grid is not None:
        kw["grid"] = grid
    params = dict(dimension_semantics=semantics, vmem_limit_bytes=VMEM_LIMIT_BYTES)
    if duty is None:
        return pl.pallas_call(body, name=name, out_shape=out_shape, scratch_shapes=list(scratch_shapes),
                              compiler_params=pltpu.CompilerParams(**params), **kw)

    single = not isinstance(out_shape, (list, tuple))
    c_out_shape = [out_shape] if single else list(out_shape)
    c_out_specs = [out_specs] if single else list(out_specs)
    n_in, n_out, n_scr = len(in_specs), len(c_out_shape), len(scratch_shapes)
    d_in, d_out = len(duty.ins), len(duty.out_shape)
    total = math.prod(grid)
    mid_step = (5 * total) // 6

    def wrapped(*refs):
        c_in, d_ins = refs[:n_in], refs[n_in:n_in + d_in]
        o0 = n_in + d_in
        c_outs, d_outs = refs[o0:o0 + n_out], refs[o0 + n_out:o0 + n_out + d_out]
        s0 = o0 + n_out + d_out
        c_scr, d_sems = refs[s0:s0 + n_scr], refs[s0 + n_scr:]
        step = pl.program_id(0)
        for ax in range(1, len(grid)):
            step = step * grid[ax] + pl.program_id(ax)

        @pl.when(step == 0)
        def _():
            duty.start(d_ins, d_outs, d_sems)

        body(*c_in, *c_outs, *c_scr)

        @pl.when(step == mid_step)
        def _():
            duty.mid(d_ins, d_outs, d_sems)

        @pl.when(step == total - 1)
        def _():
            duty.finish(d_ins, d_outs, d_sems)

    call = pl.pallas_call(
        wrapped, name=name, grid=grid,
        in_specs=list(in_specs) + [_ANY] * d_in, out_specs=c_out_specs + [_ANY] * d_out,
        out_shape=c_out_shape + list(duty.out_shape),
        scratch_shapes=list(scratch_shapes) + list(duty.scratch),
        input_output_aliases={n_in + i: n_out + o for i, o in duty.aliases.items()},
        compiler_params=pltpu.CompilerParams(has_side_effects=True, **params))

    def run(*args):
        outs = call(*args, *duty.ins)
        c = outs[:n_out]
        return (c[0] if single else list(c)), list(outs[n_out:])

    return run


def _sds(shape, dtype):
    return jax.ShapeDtypeStruct(tuple(shape), dtype)


def _dot(a, b):
    return jnp.dot(a, b, preferred_element_type=F32)


def _dot_nt(a, b):
    return lax.dot_general(a, b, NT_DIMS, preferred_element_type=F32)


def _dot_tn(a, b):
    return lax.dot_general(a, b, TN_DIMS, preferred_element_type=F32)


def _rms_fwd(x, g):
    r = lax.rsqrt(jnp.mean(x * x, axis=-1, keepdims=True) + EPS)
    return x * r * g


def _rms_bwd(dh, x, g):
    r = lax.rsqrt(jnp.mean(x * x, axis=-1, keepdims=True) + EPS)
    xh = x * r
    dg = jnp.sum(dh * xh, axis=0, keepdims=True)
    dxh = dh * g
    dx = r * (dxh - xh * jnp.mean(dxh * xh, axis=-1, keepdims=True))
    return dx, dg


def _sigmoid(x):
    return 0.5 * jnp.tanh(0.5 * x) + 0.5


def _tile_rows(t):
    return min(512, t)


def _sub_tiles(tm, n_sub=2):
    rows = tm // n_sub
    return [pl.ds(i * rows, rows) for i in range(n_sub)]


def _wgrad_rows(t):
    return min(2048, t)


def pool_fwd(x, g_row, w_in, w_grp, scale, w_out, duty=None):
    t_len = x.shape[0]
    tm = _tile_rows(t_len)

    def body(x_ref, g_ref, win_ref, wgrp_ref, scale_ref, wout_ref,
             h_ref, p_ref, zr_ref, z_ref, xo_ref, ubuf):
        t = pl.program_id(0)

        @pl.when(t == 0)
        def _():
            ubuf[pl.ds(0, POOL_HALO), :] = jnp.zeros((POOL_HALO, D_MODEL), F32)

        x_t = x_ref[...]
        h = _rms_fwd(x_t, g_ref[...]).astype(BF16)
        h_ref[...] = h
        ubuf[pl.ds(POOL_HALO, tm), :] = _dot(h, win_ref[...])
        row = t * tm + lax.broadcasted_iota(jnp.int32, (tm, 1), 0)
        for gi, w in enumerate(POOL_WINDOWS):
            cols = pl.ds(gi * POOL_GROUP_DIM, POOL_GROUP_DIM)
            u_g = ubuf[pl.ds(POOL_HALO, tm), cols]
            acc = u_g
            for j in range(1, w):
                acc = acc + ubuf[pl.ds(POOL_HALO - j, tm), cols]
            inv_cnt = 1.0 / jnp.minimum(row + 1, w).astype(F32)
            p_g = (acc * inv_cnt - u_g).astype(BF16)
            p_ref[:, cols] = p_g
            z_g = _dot(p_g, wgrp_ref[gi])
            zr_ref[:, cols] = z_g.astype(BF16)
            z_ref[:, cols] = (z_g * scale_ref[:, cols]).astype(BF16)
        ubuf[pl.ds(0, POOL_HALO), :] = ubuf[pl.ds(tm, POOL_HALO), :]
        xo_ref[...] = x_t + _dot(z_ref[...], wout_ref[...])

    row_spec = pl.BlockSpec((tm, D_MODEL), lambda t: (t, 0))
    full2 = lambda shape: pl.BlockSpec(shape, lambda t: (0,) * len(shape))
    return _pcall(
        body, name="pool_fwd", grid=(t_len // tm,),
        in_specs=[row_spec, full2((1, D_MODEL)), full2((D_MODEL, D_MODEL)),
                  full2((4, POOL_GROUP_DIM, POOL_GROUP_DIM)), full2((1, D_MODEL)), full2((D_MODEL, D_MODEL))],
        out_specs=[row_spec] * 5,
        out_shape=[_sds((t_len, D_MODEL), BF16)] * 4 + [_sds((t_len, D_MODEL), F32)],
        scratch_shapes=[pltpu.VMEM((tm + POOL_HALO, D_MODEL), F32)],
        semantics=("arbitrary",), duty=duty,
    )(x, g_row, w_in, w_grp, scale, w_out)


def ffn_fwd(x, g_row, w_gate_t, w_up_t, w_down, name, duty=None, loss_head=None):
    t_len = x.shape[0]
    tm = min(1024 if loss_head is None else 512, t_len)
    n_in = 5 if loss_head is None else 7

    def body(*refs):
        x_ref, g_ref, wg_ref, wu_ref, wd_ref = refs[:5]
        h_ref, go_ref, uo_ref, ao_ref = refs[n_in:n_in + 4]
        hbuf, acc = refs[-2:]
        t = pl.program_id(0)
        s = pl.program_id(1)

        @pl.when(s == 0)
        def _():
            h = _rms_fwd(x_ref[...], g_ref[...]).astype(BF16)
            hbuf[...] = h
            h_ref[...] = h
            acc[...] = jnp.zeros_like(acc)

        h = hbuf[...]
        gate = _dot_nt(h, wg_ref[...])
        up = _dot_nt(h, wu_ref[...])
        go_ref[...] = gate.astype(BF16)
        uo_ref[...] = up.astype(BF16)
        act = (gate * _sigmoid(gate) * up).astype(BF16)
        ao_ref[...] = act
        acc[...] += _dot(act, wd_ref[...])

        if loss_head is None:
            @pl.when(s == N_SHARDS - 1)
            def _():
                refs[n_in + 4][...] = x_ref[...] + acc[...]
        else:
            gf_ref, tgt_ref = refs[5:7]
            dx_ref, dn_ref, loss_ref = refs[n_in + 4:n_in + 7]

            @pl.when(jnp.logical_and(s == 0, t == 0))
            def _():
                dn_ref[...] = jnp.zeros_like(dn_ref)
                loss_ref[...] = jnp.zeros_like(loss_ref)

            @pl.when(s == N_SHARDS - 1)
            def _():
                x_out = x_ref[...] + acc[...]
                gf = gf_ref[...]
                diff = _rms_fwd(x_out, gf) - tgt_ref[...]
                loss_ref[...] += 0.5 * jnp.sum(jnp.mean(diff * diff, axis=-1, keepdims=True), axis=0, keepdims=True)
                dx, dn = _rms_bwd(diff * (1.0 / D_MODEL), x_out, gf)
                dx_ref[...] = dx
                dn_ref[...] += dn

    row_spec = pl.BlockSpec((tm, D_MODEL), lambda t, s: (t, 0))
    vec_spec = pl.BlockSpec((1, D_MODEL), lambda t, s: (0, 0))
    row_w = pl.BlockSpec((None, FF_SHARD, D_MODEL), lambda t, s: (s, 0, 0))
    act_spec = pl.BlockSpec((None, tm, FF_SHARD), lambda t, s: (s, t, 0))
    in_specs = [row_spec, vec_spec, row_w, row_w, row_w]
    out_specs = [row_spec, act_spec, act_spec, act_spec, row_spec]
    out_shape = [_sds((t_len, D_MODEL), BF16)] + [_sds((N_SHARDS, t_len, FF_SHARD), BF16)] * 3 + [
        _sds((t_len, D_MODEL), F32)]
    args = [x, g_row, w_gate_t, w_up_t, w_down]
    if loss_head is not None:
        in_specs += [vec_spec, row_spec]
        out_specs += [vec_spec, pl.BlockSpec((1, 1), lambda t, s: (0, 0))]
        out_shape += [_sds((1, D_MODEL), F32), _sds((1, 1), F32)]
        args += list(loss_head)
    return _pcall(
        body, name=name, grid=(t_len // tm, N_SHARDS), in_specs=in_specs, out_specs=out_specs, out_shape=out_shape,
        scratch_shapes=[pltpu.VMEM((tm, D_MODEL), BF16), pltpu.VMEM((tm, D_MODEL), F32)],
        semantics=("arbitrary", "arbitrary"), duty=duty,
    )(*args)


def ffn_bwd(dxo, x, g_row, gate, up, w_gate_t, w_up_t, w_down, name, duty=None):
    t_len = x.shape[0]
    tm = _tile_rows(t_len)

    def body(dxo_ref, x_ref, g_ref, gate_ref, up_ref, wg_ref, wu_ref, wd_ref,
             dg_ref, du_ref, dx_ref, dn_ref, dxb, dh):
        t = pl.program_id(0)
        s = pl.program_id(1)

        @pl.when(s == 0)
        def _():
            dxb[...] = dxo_ref[...].astype(BF16)
            dh[...] = jnp.zeros_like(dh)

        @pl.when(jnp.logical_and(s == 0, t == 0))
        def _():
            dn_ref[...] = jnp.zeros_like(dn_ref)

        sub_tiles = _sub_tiles(tm)
        dacts = [_dot_nt(dxb[rows, :], wd_ref[...]) for rows in sub_tiles]
        for rows, dact in zip(sub_tiles, dacts):
            gv = gate_ref[rows, :].astype(F32)
            uv = up_ref[rows, :].astype(F32)
            sg = _sigmoid(gv)
            dgv = (dact * uv * (sg * (1.0 + gv * (1.0 - sg)))).astype(BF16)
            duv = (dact * (gv * sg)).astype(BF16)
            dg_ref[rows, :] = dgv
            du_ref[rows, :] = duv
            dh[rows, :] += _dot(dgv, wg_ref[...]) + _dot(duv, wu_ref[...])

        @pl.when(s == N_SHARDS - 1)
        def _():
            dx, dn = _rms_bwd(dh[...], x_ref[...], g_ref[...])
            dx_ref[...] = dxo_ref[...] + dx
            dn_ref[...] += dn

    row_spec = pl.BlockSpec((tm, D_MODEL), lambda t, s: (t, 0))
    vec_spec = pl.BlockSpec((1, D_MODEL), lambda t, s: (0, 0))
    row_w = pl.BlockSpec((None, FF_SHARD, D_MODEL), lambda t, s: (s, 0, 0))
    act_spec = pl.BlockSpec((None, tm, FF_SHARD), lambda t, s: (s, t, 0))
    act_shape = _sds((N_SHARDS, t_len, FF_SHARD), BF16)
    return _pcall(
        body, name=name, grid=(t_len // tm, N_SHARDS),
        in_specs=[row_spec, row_spec, vec_spec, act_spec, act_spec, row_w, row_w, row_w],
        out_specs=[act_spec, act_spec, row_spec, vec_spec],
        out_shape=[act_shape, act_shape, _sds((t_len, D_MODEL), F32), _sds((1, D_MODEL), F32)],
        scratch_shapes=[pltpu.VMEM((tm, D_MODEL), BF16), pltpu.VMEM((tm, D_MODEL), F32)],
        semantics=("arbitrary", "arbitrary"), duty=duty,
    )(dxo, x, g_row, gate, up, w_gate_t, w_up_t, w_down)


def tn_matmul(name, a, b, a_spec, b_spec, out_shape, out_spec, grid, duty=None):
    def body(a_ref, b_ref, o_ref):
        @pl.when(pl.program_id(len(grid) - 1) == 0)
        def _():
            o_ref[...] = jnp.zeros_like(o_ref)

        res = _dot_tn(a_ref[...].astype(BF16), b_ref[...].astype(BF16))
        o_ref[...] += res.reshape(o_ref.shape)

    return _pcall(body, name=name, grid=grid, in_specs=[a_spec, b_spec], out_specs=out_spec,
                  out_shape=out_shape, semantics=("arbitrary",) * len(grid), duty=duty)(a, b)


def wgrad_full(name, a, b, duty=None):
    t_len, k = a.shape
    n = b.shape[1]
    tt = _wgrad_rows(t_len)
    return tn_matmul(name, a, b,
                     pl.BlockSpec((tt, k), lambda t: (t, 0)), pl.BlockSpec((tt, n), lambda t: (t, 0)),
                     _sds((k, n), F32), pl.BlockSpec((k, n), lambda t: (0, 0)), (t_len // tt,), duty)


def wgrad_col_sharded(name, a, b_sh, duty=None):
    t_len, k = a.shape
    n_sh, _, n = b_sh.shape
    tt = _wgrad_rows(t_len)
    return tn_matmul(name, a, b_sh,
                     pl.BlockSpec((tt, k), lambda s, t: (t, 0)), pl.BlockSpec((None, tt, n), lambda s, t: (s, t, 0)),
                     _sds((n_sh, k, n), F32), pl.BlockSpec((None, k, n), lambda s, t: (s, 0, 0)),
                     (n_sh, t_len // tt), duty)


def wgrad_row_sharded(name, a_sh, b, duty=None):
    t_len, n = b.shape
    n_sh, _, k = a_sh.shape
    tt = _wgrad_rows(t_len)

    def body(a_ref, b_ref, o_ref):
        s = pl.program_id(1)
        res = _dot_tn(a_ref[...], b_ref[...].astype(BF16))

        @pl.when(pl.program_id(0) == 0)
        def _():
            o_ref[s] = res

        @pl.when(pl.program_id(0) > 0)
        def _():
            o_ref[s] += res

    return _pcall(body, name=name, grid=(t_len // tt, n_sh),
                  in_specs=[pl.BlockSpec((None, tt, k), lambda t, s: (s, t, 0)),
                            pl.BlockSpec((tt, n), lambda t, s: (t, 0))],
                  out_specs=pl.BlockSpec((n_sh, k, n), lambda t, s: (0, 0, 0)),
                  out_shape=_sds((n_sh, k, n), F32), semantics=("arbitrary", "arbitrary"), duty=duty)(a_sh, b)


def wgrad_pool_group(name, p, dzs, duty=None):
    t_len = p.shape[0]
    tt = _wgrad_rows(t_len)
    gd = POOL_GROUP_DIM
    rows = gd // N_SHARDS
    return tn_matmul(name, p, dzs,
                     pl.BlockSpec((tt, gd), lambda g, t: (t, g)), pl.BlockSpec((tt, gd), lambda g, t: (t, g)),
                     _sds((N_SHARDS, 4, rows, gd), F32),
                     pl.BlockSpec((N_SHARDS, None, rows, gd), lambda g, t: (0, g, 0, 0)),
                     (4, t_len // tt), duty)


PAD_LANES = 384
QKV_PAD = 3 * PAD_LANES
N_SLABS = QKV_PAD // LANE
GROUP_REAL = tuple(GROUP_LANES[g + 1] - GROUP_LANES[g] for g in range(3))
Q_BLOCK = 512


def pad_groups(w, axis):
    parts = []
    for g in range(3):
        blk = lax.slice_in_dim(w, GROUP_LANES[g], GROUP_LANES[g + 1], axis=axis)
        pad = [(0, 0)] * w.ndim
        pad[axis] = (0, PAD_LANES - GROUP_REAL[g])
        parts.append(jnp.pad(blk, pad))
    return parts


def unpad_groups(parts, axis):
    return jnp.concatenate([lax.slice_in_dim(p, 0, GROUP_REAL[g], axis=axis) for g, p in enumerate(parts)],
                           axis=axis)


def pad_qkv_weight(w_qkv_sh):
    w = jnp.transpose(w_qkv_sh, (1, 0, 2)).reshape(D_MODEL, 3 * D_MODEL)
    q, k, v = (pad_groups(w[:, i * D_MODEL:(i + 1) * D_MODEL], 1) for i in range(3))
    return jnp.stack([jnp.concatenate([q[g], k[g], v[g]], axis=1) for g in range(3)])


def unpad_qkv_grad(g_pad):
    cols = [unpad_groups([g_pad[g][:, i * PAD_LANES:(i + 1) * PAD_LANES] for g in range(3)], 1) for i in range(3)]
    w = jnp.concatenate(cols, axis=1)
    return jnp.transpose(w.reshape(D_MODEL, N_SHARDS, QKV_SHARD), (1, 0, 2))


def rope_tables(t_len):
    inv_freq = 1.0 / (ROPE_THETA ** (jnp.arange(0, HEAD_DIM, 2, dtype=F32) / HEAD_DIM))
    ang = jnp.arange(t_len, dtype=F32)[:, None] * inv_freq[None, :]
    cos_h, sin_h = lax.optimization_barrier((jnp.cos(ang), jnp.sin(ang)))
    reps = (1, 2 * LANE // HEAD_DIM)
    return jnp.tile(cos_h, reps), jnp.tile(sin_h, reps)


def _rot_half(v):
    n = v.shape[1]
    lane = lax.broadcasted_iota(jnp.int32, v.shape, 1)
    return jnp.where(lane % HEAD_DIM < HEAD_DIM // 2,
                     -pltpu.roll(v, n - HEAD_DIM // 2, 1), pltpu.roll(v, HEAD_DIM // 2, 1))


def _lane_cols(j):
    return slice(j * LANE, (j + 1) * LANE)


def _to_residue_major(slab, j_src, dst_ref, j_dst, dil, rows):
    for r in range(dil):
        dst_ref[r, :, _lane_cols(j_dst)] = slab[j_src, pl.ds(r, rows // dil, stride=dil), :].astype(dst_ref.dtype)


def _to_natural(src_ref, j_src, slab, j_dst, dil, rows):
    for r in range(dil):
        slab[j_dst, pl.ds(r, rows // dil, stride=dil), :] = src_ref[r, :, _lane_cols(j_src)].astype(F32)


def qkv_fwd(x, g_row, w_pad, cos, sin):
    t_len = x.shape[0]
    tm = _tile_rows(t_len)

    def body(x_ref, g_ref, w_ref, cos_ref, sin_ref, h_ref, o1_ref, o4_ref, o16_ref, slabs):
        h = _rms_fwd(x_ref[...], g_ref[...]).astype(BF16)
        h_ref[...] = h
        accs = [_dot(h, w_ref[gi]) for gi in range(3)]
        cos_t = cos_ref[...]
        sin_t = sin_ref[...]
        for gi, (dil, o_ref) in enumerate(zip(GROUP_DIL, (o1_ref, o4_ref, o16_ref))):
            slab = slabs.at[gi]
            for j in range(N_SLABS):
                a = accs[gi][:, _lane_cols(j)]
                if j < 6:
                    a = a * cos_t + _rot_half(a) * sin_t
                if j < 3:
                    a = a * (HEAD_DIM ** -0.5)
                if dil == 1:
                    o_ref[0, :, _lane_cols(j)] = a.astype(BF16)
                else:
                    slab[j] = a
                    _to_residue_major(slab, j, o_ref, j, dil, tm)

    row_spec = pl.BlockSpec((tm, D_MODEL), lambda t: (t, 0))
    tab_spec = pl.BlockSpec((tm, LANE), lambda t: (t, 0))
    out_specs = [row_spec] + [pl.BlockSpec((d, tm // d, QKV_PAD), lambda t: (0, t, 0)) for d in GROUP_DIL]
    out_shape = [_sds((t_len, D_MODEL), BF16)] + [_sds((d, t_len // d, QKV_PAD), BF16) for d in GROUP_DIL]
    return _pcall(
        body, name="qkv_fwd", grid=(t_len // tm,),
        in_specs=[row_spec, pl.BlockSpec((1, D_MODEL), lambda t: (0, 0)),
                  pl.BlockSpec((3, D_MODEL, QKV_PAD), lambda t: (0, 0, 0)), tab_spec, tab_spec],
        out_specs=out_specs, out_shape=out_shape,
        scratch_shapes=[pltpu.VMEM((3, N_SLABS, tm, LANE), F32)],
        semantics=("arbitrary",),
    )(x, g_row, w_pad, cos, sin)


def _band_mask(n):
    qi = lax.broadcasted_iota(jnp.int32, (ATTN_W, 2 * ATTN_W), 0)
    kj = lax.broadcasted_iota(jnp.int32, (ATTN_W, 2 * ATTN_W), 1)
    dist = ATTN_W + qi - kj
    return (dist >= 0) & (dist <= ATTN_W) & ((kj >= ATTN_W) | (n > 0))


def _half_masks():
    lane = lax.broadcasted_iota(jnp.int32, (1, LANE), 1)
    return [lane < HEAD_DIM, lane >= HEAD_DIM]


def _live_halves(gi, j):
    hms = _half_masks()
    return hms if (gi == 0 or j < 2) else hms[:1]


def attn_fwd(qkv_g, gi, name):
    dil, l_len, _ = qkv_g.shape
    qb = min(Q_BLOCK, l_len)
    nsub = qb // ATTN_W

    def body(q_ref, kc_ref, kp_ref, vc_ref, vp_ref, o_ref, lse_ref, kbuf, vbuf):
        n = pl.program_id(1)
        kbuf[pl.ds(0, ATTN_W), :] = kp_ref[...]
        kbuf[pl.ds(ATTN_W, qb), :] = kc_ref[...]
        vbuf[pl.ds(0, ATTN_W), :] = vp_ref[...]
        vbuf[pl.ds(ATTN_W, qb), :] = vc_ref[...]

        def sub(b, carry):
            r0 = pl.multiple_of(b * ATTN_W, ATTN_W)
            mask = _band_mask(n + b)
            krows = pl.ds(r0, 2 * ATTN_W)
            scores = []
            for j in range(3):
                q = q_ref[pl.ds(r0, ATTN_W), _lane_cols(j)]
                for hm in _live_halves(gi, j):
                    scores.append(_dot_nt(jnp.where(hm, q, jnp.zeros_like(q)), kbuf[krows, _lane_cols(j)]))
            scores = iter(scores)
            head_lane = lax.broadcasted_iota(jnp.int32, (1, LANE), 1)
            lse = jnp.zeros((ATTN_W, LANE), F32)
            for j in range(3):
                cols = _lane_cols(j)
                v = vbuf[krows, cols]
                o = jnp.zeros((ATTN_W, LANE), F32)
                for half, hm in enumerate(_live_halves(gi, j)):
                    s = jnp.where(mask, next(scores), NEG_INF)
                    m = jnp.max(s, axis=-1, keepdims=True)
                    e = jnp.exp(s - m)
                    den = jnp.sum(e, axis=-1, keepdims=True)
                    p = (e * (1.0 / den)).astype(BF16)
                    o = jnp.where(hm, _dot(p, v), o)
                    lse = jnp.where(head_lane == 2 * j + half, m + jnp.log(den), lse)
                o_ref[pl.ds(r0, ATTN_W), cols] = o.astype(BF16)
            lse_ref[pl.ds(r0, ATTN_W), :] = lse
            return carry

        lax.fori_loop(0, nsub, sub, 0)

    cur = lambda c: pl.BlockSpec((None, qb, PAD_LANES), lambda r, n: (r, n, c))
    prev = lambda c: pl.BlockSpec((None, ATTN_W, PAD_LANES), lambda r, n: (r, jnp.maximum(n * nsub - 1, 0), c))
    return _pcall(
        body, name=name, grid=(dil, l_len // qb),
        in_specs=[cur(0), cur(1), prev(1), cur(2), prev(2)],
        out_specs=[pl.BlockSpec((None, qb, PAD_LANES), lambda r, n: (r, n, 0)),
                   pl.BlockSpec((None, qb, LANE), lambda r, n: (r, n, 0))],
        out_shape=[_sds((dil, l_len, PAD_LANES), BF16), _sds((dil, l_len, LANE), F32)],
        scratch_shapes=[pltpu.VMEM((qb + ATTN_W, PAD_LANES), BF16), pltpu.VMEM((qb + ATTN_W, PAD_LANES), BF16)],
        semantics=("arbitrary", "arbitrary"),
    )(qkv_g, qkv_g, qkv_g, qkv_g, qkv_g)


def _group_stats(lses):
    head_lane = lax.broadcasted_iota(jnp.int32, (1, LANE), 1)
    fulls, glse = [], []
    for g in range(3):
        real = head_lane < GROUP_HEADS[g]
        mx = jnp.max(jnp.where(real, lses[g], -jnp.inf), axis=-1, keepdims=True)
        sm = jnp.sum(jnp.where(real, jnp.exp(lses[g] - mx), 0.0), axis=-1, keepdims=True)
        fulls.append(mx + jnp.log(sm))
        glse.append(fulls[g] - math.log(GROUP_HEADS[g]))
    top = jnp.maximum(jnp.maximum(glse[0], glse[1]), glse[2])
    ex = [jnp.exp(v - top) for v in glse]
    tot = ex[0] + ex[1] + ex[2]
    alpha = [v / tot for v in ex]
    lane = lax.broadcasted_iota(jnp.int32, (1, QKV_PAD), 1)
    scale = jnp.where(lane < PAD_LANES, 3.0 * alpha[0],
                      jnp.where(lane < 2 * PAD_LANES, 3.0 * alpha[1], 3.0 * alpha[2]))
    return alpha, fulls, scale


def attn_out_fwd(x, o_parts, lse_parts, w_out_pad):
    t_len = x.shape[0]
    tm = _tile_rows(t_len)

    def body(x_ref, o1, o4, o16, l1, l4, l16, w_ref, xo_ref, mg_ref, o_ref, lse_ref, o_slab, l_slab):
        for gi, (dil, og, lg) in enumerate(zip(GROUP_DIL, (o1, o4, o16), (l1, l4, l16))):
            for j in range(3):
                _to_natural(og, j, o_slab, 3 * gi + j, dil, tm)
            _to_natural(lg, 0, l_slab, gi, dil, tm)
        o = jnp.concatenate([o_slab[j] for j in range(N_SLABS)], axis=1)
        lses = [l_slab[gi] for gi in range(3)]
        o_ref[...] = o.astype(BF16)
        for gi in range(3):
            lse_ref[:, _lane_cols(gi)] = lses[gi]
        _, _, scale = _group_stats(lses)
        merged = (o * scale).astype(BF16)
        mg_ref[...] = merged
        xo_ref[...] = x_ref[...] + _dot(merged, w_ref[...])

    row_spec = pl.BlockSpec((tm, D_MODEL), lambda t: (t, 0))
    pad_spec = pl.BlockSpec((tm, QKV_PAD), lambda t: (t, 0))
    o_specs = [pl.BlockSpec((d, tm // d, PAD_LANES), lambda t: (0, t, 0)) for d in GROUP_DIL]
    lse_specs = [pl.BlockSpec((d, tm // d, LANE), lambda t: (0, t, 0)) for d in GROUP_DIL]
    return _pcall(
        body, name="attn_out_fwd", grid=(t_len // tm,),
        in_specs=[row_spec] + o_specs + lse_specs + [pl.BlockSpec((QKV_PAD, D_MODEL), lambda t: (0, 0))],
        out_specs=[row_spec, pad_spec, pad_spec, pl.BlockSpec((tm, 3 * LANE), lambda t: (t, 0))],
        out_shape=[_sds((t_len, D_MODEL), F32), _sds((t_len, QKV_PAD), BF16),
                   _sds((t_len, QKV_PAD), BF16), _sds((t_len, 3 * LANE), F32)],
        scratch_shapes=[pltpu.VMEM((N_SLABS, tm, LANE), F32), pltpu.VMEM((3, tm, LANE), F32)],
        semantics=("arbitrary",),
    )(x, *o_parts, *lse_parts, w_out_pad)


def attn_out_bwd(dxo, w_out_pad, o, lse, duty=None):
    t_len = dxo.shape[0]
    tm = _tile_rows(t_len)

    def body(dx_ref, w_ref, o_ref, lse_ref, d1, d4, d16, c1, c4, c16, slab):
        dmerged = _dot_nt(dx_ref[...].astype(BF16), w_ref[...])
        o_t = o_ref[...].astype(F32)
        lses = [lse_ref[:, _lane_cols(gi)] for gi in range(3)]
        alpha, fulls, scale = _group_stats(lses)
        e = dmerged * o_t
        lane = lax.broadcasted_iota(jnp.int32, (1, QKV_PAD), 1)
        dalpha = [3.0 * jnp.sum(jnp.where((lane >= g * PAD_LANES) & (lane < g * PAD_LANES + GROUP_REAL[g]), e, 0.0),
                                axis=-1, keepdims=True) for g in range(3)]
        mean_da = alpha[0] * dalpha[0] + alpha[1] * dalpha[1] + alpha[2] * dalpha[2]
        dglse = [alpha[g] * (dalpha[g] - mean_da) for g in range(3)]
        do = dmerged * scale
        es = e * scale
        for j in range(N_SLABS):
            slab[j] = do[:, _lane_cols(j)]
        for gi, (dil, dg) in enumerate(zip(GROUP_DIL, (d1, d4, d16))):
            for j in range(3):
                _to_residue_major(slab, 3 * gi + j, dg, j, dil, tm)
        head_lane = lax.broadcasted_iota(jnp.int32, (1, LANE), 1)
        first = head_lane < HEAD_DIM
        for gi, (dil, cg) in enumerate(zip(GROUP_DIL, (c1, c4, c16))):
            c_g = -(dglse[gi] * jnp.exp(lses[gi] - fulls[gi]))
            for j in range(3):
                blk = es[:, _lane_cols(3 * gi + j)]
                halves = (jnp.sum(jnp.where(first, blk, 0.0), axis=-1, keepdims=True),
                          jnp.sum(jnp.where(first, 0.0, blk), axis=-1, keepdims=True))
                for half in range(2):
                    c_g = c_g + jnp.where(head_lane == 2 * j + half, halves[half], 0.0)
            slab[gi] = c_g
            _to_residue_major(slab, gi, cg, 0, dil, tm)

    row_spec = pl.BlockSpec((tm, D_MODEL), lambda t: (t, 0))
    pad_spec = pl.BlockSpec((tm, QKV_PAD), lambda t: (t, 0))
    do_specs = [pl.BlockSpec((d, tm // d, PAD_LANES), lambda t: (0, t, 0)) for d in GROUP_DIL]
    c_specs = [pl.BlockSpec((d, tm // d, LANE), lambda t: (0, t, 0)) for d in GROUP_DIL]
    outs = _pcall(
        body, name="attn_out_bwd", grid=(t_len // tm,),
        in_specs=[row_spec, pl.BlockSpec((QKV_PAD, D_MODEL), lambda t: (0, 0)), pad_spec,
                  pl.BlockSpec((tm, 3 * LANE), lambda t: (t, 0))],
        out_specs=do_specs + c_specs,
        out_shape=[_sds((d, t_len // d, PAD_LANES), BF16) for d in GROUP_DIL]
                  + [_sds((d, t_len // d, LANE), F32) for d in GROUP_DIL],
        scratch_shapes=[pltpu.VMEM((N_SLABS, tm, LANE), F32)],
        semantics=("arbitrary",), duty=duty,
    )(dxo, w_out_pad, o, lse)
    if duty is None:
        return outs[:3], outs[3:]
    return (outs[0][:3], outs[0][3:]), outs[1]


def attn_bwd(qkv_g, do_g, lse_g, c_g, gi, name, duty=None):
    dil, l_len, _ = qkv_g.shape
    qb = min(Q_BLOCK, l_len)
    nsub = qb // ATTN_W
    nsb = l_len // qb

    def body(q_ref, kc_ref, kp_ref, vc_ref, vp_ref, do_ref, lse_ref, c_ref,
             qn_ref, don_ref, lsen_ref, cn_ref, o_ref, kbuf, vbuf, dkbuf, dvbuf):
        n = pl.program_id(1)
        kbuf[pl.ds(0, ATTN_W), :] = kp_ref[...]
        kbuf[pl.ds(ATTN_W, qb), :] = kc_ref[...]
        vbuf[pl.ds(0, ATTN_W), :] = vp_ref[...]
        vbuf[pl.ds(ATTN_W, qb), :] = vc_ref[...]
        dkbuf[...] = jnp.zeros_like(dkbuf)
        dvbuf[...] = jnp.zeros_like(dvbuf)

        def block(q_of, do_of, lse_of, c_of, krows, mask, dq_rows):
            heads = []
            for j in range(3):
                cols = _lane_cols(j)
                q, do_t, k, v = q_of(cols), do_of(cols), kbuf[krows, cols], vbuf[krows, cols]
                for half, hm in enumerate(_live_halves(gi, j)):
                    qh = jnp.where(hm, q, jnp.zeros_like(q))
                    doh = jnp.where(hm, do_t, jnp.zeros_like(do_t))
                    heads.append((j, 2 * j + half, hm, qh, doh, _dot_nt(qh, k), _dot_nt(doh, v)))
            head_lane = lax.broadcasted_iota(jnp.int32, (1, LANE), 1)
            lse_t, c_t = lse_of(), c_of()
            for j in range(3):
                cols = _lane_cols(j)
                k = kbuf[krows, cols]
                dq = jnp.zeros((ATTN_W, LANE), F32)
                dk = jnp.zeros((k.shape[0], LANE), F32)
                dv = jnp.zeros((k.shape[0], LANE), F32)
                for hj, head, hm, qh, doh, s, dp in heads:
                    if hj != j:
                        continue
                    lse_h = jnp.max(jnp.where(head_lane == head, lse_t, -jnp.inf), axis=-1, keepdims=True)
                    c_h = jnp.max(jnp.where(head_lane == head, c_t, -jnp.inf), axis=-1, keepdims=True)
                    p = jnp.exp(jnp.where(mask, s, NEG_INF) - lse_h)
                    ds = (p * (dp - c_h)).astype(BF16)
                    if dq_rows is not None:
                        dq = jnp.where(hm, _dot(ds, k), dq)
                    dk = dk + _dot_tn(ds, qh)
                    dv = dv + _dot_tn(p.astype(BF16), doh)
                if dq_rows is not None:
                    o_ref[dq_rows, cols] = dq.astype(BF16)
                dkbuf[krows, cols] += dk
                dvbuf[krows, cols] += dv

        def sub(b, carry):
            rows = pl.ds(pl.multiple_of(b * ATTN_W, ATTN_W), ATTN_W)
            krows = pl.ds(pl.multiple_of(b * ATTN_W, ATTN_W), 2 * ATTN_W)
            block(lambda c: q_ref[rows, c], lambda c: do_ref[rows, c], lambda: lse_ref[rows, :],
                  lambda: c_ref[rows, :], krows, _band_mask(n + b), rows)
            return carry

        lax.fori_loop(0, nsub, sub, 0)

        qi = lax.broadcasted_iota(jnp.int32, (ATTN_W, ATTN_W), 0)
        kj = lax.broadcasted_iota(jnp.int32, (ATTN_W, ATTN_W), 1)
        nmask = (qi <= kj) & (n < nsb - 1)
        block(lambda c: qn_ref[:, c], lambda c: don_ref[:, c], lambda: lsen_ref[...],
              lambda: cn_ref[...], pl.ds(qb, ATTN_W), nmask, None)
        o_ref[:, pl.ds(PAD_LANES, PAD_LANES)] = dkbuf[pl.ds(ATTN_W, qb), :].astype(BF16)
        o_ref[:, pl.ds(2 * PAD_LANES, PAD_LANES)] = dvbuf[pl.ds(ATTN_W, qb), :].astype(BF16)

    cur = lambda c: pl.BlockSpec((None, qb, PAD_LANES), lambda r, n: (r, n, c))
    prev = lambda c: pl.BlockSpec((None, ATTN_W, PAD_LANES), lambda r, n: (r, jnp.maximum(n * nsub - 1, 0), c))
    nxt_row = lambda r, n: (r, jnp.minimum((n + 1) * nsub, nsb * nsub - 1), 0)
    nxt = pl.BlockSpec((None, ATTN_W, PAD_LANES), nxt_row)
    head_cur = pl.BlockSpec((None, qb, LANE), lambda r, n: (r, n, 0))
    head_nxt = pl.BlockSpec((None, ATTN_W, LANE), nxt_row)
    return _pcall(
        body, name=name, grid=(dil, nsb),
        in_specs=[cur(0), cur(1), prev(1), cur(2), prev(2), cur(0), head_cur, head_cur, nxt, nxt, head_nxt, head_nxt],
        out_specs=pl.BlockSpec((None, qb, QKV_PAD), lambda r, n: (r, n, 0)),
        out_shape=_sds((dil, l_len, QKV_PAD), BF16),
        scratch_shapes=[pltpu.VMEM((qb + ATTN_W, PAD_LANES), BF16), pltpu.VMEM((qb + ATTN_W, PAD_LANES), BF16),
                        pltpu.VMEM((qb + ATTN_W, PAD_LANES), F32), pltpu.VMEM((qb + ATTN_W, PAD_LANES), F32)],
        semantics=("arbitrary", "arbitrary"), duty=duty,
    )(qkv_g, qkv_g, qkv_g, qkv_g, qkv_g, do_g, lse_g, c_g, qkv_g, do_g, lse_g, c_g)


def qkv_bwd(dqkv_parts, w_pad, dxo, x, g_row, cos, sin):
    t_len = x.shape[0]
    tm = _tile_rows(t_len)

    def body(p1, p4, p16, w_ref, dxo_ref, x_ref, g_ref, cos_ref, sin_ref, dq_ref, dx_ref, dn_ref, slabs):
        @pl.when(pl.program_id(0) == 0)
        def _():
            dn_ref[...] = jnp.zeros_like(dn_ref)

        cos_t = cos_ref[...]
        sin_t = sin_ref[...]
        dh = None
        for gi, (dil, part) in enumerate(zip(GROUP_DIL, (p1, p4, p16))):
            slab = slabs.at[gi]
            for j in range(N_SLABS):
                if dil == 1:
                    a = part[0, :, _lane_cols(j)].astype(F32)
                else:
                    _to_natural(part, j, slab, j, dil, tm)
                    a = slab[j]
                if j < 6:
                    a = a * cos_t - _rot_half(a * sin_t)
                if j < 3:
                    a = a * (HEAD_DIM ** -0.5)
                dq_ref[gi, :, _lane_cols(j)] = a.astype(BF16)
            contrib = _dot_nt(dq_ref[gi], w_ref[gi])
            dh = contrib if dh is None else dh + contrib
        dx, dn = _rms_bwd(dh, x_ref[...], g_ref[...])
        dx_ref[...] = dxo_ref[...] + dx
        dn_ref[...] += dn

    row_spec = pl.BlockSpec((tm, D_MODEL), lambda t: (t, 0))
    vec_spec = pl.BlockSpec((1, D_MODEL), lambda t: (0, 0))
    tab_spec = pl.BlockSpec((tm, LANE), lambda t: (t, 0))
    part_specs = [pl.BlockSpec((d, tm // d, QKV_PAD), lambda t: (0, t, 0)) for d in GROUP_DIL]
    return _pcall(
        body, name="qkv_bwd", grid=(t_len // tm,),
        in_specs=part_specs + [pl.BlockSpec((3, D_MODEL, QKV_PAD), lambda t: (0, 0, 0)),
                               row_spec, row_spec, vec_spec, tab_spec, tab_spec],
        out_specs=[pl.BlockSpec((3, tm, QKV_PAD), lambda t: (0, t, 0)), row_spec, vec_spec],
        out_shape=[_sds((3, t_len, QKV_PAD), BF16), _sds((t_len, D_MODEL), F32), _sds((1, D_MODEL), F32)],
        scratch_shapes=[pltpu.VMEM((3, N_SLABS, tm, LANE), F32)],
        semantics=("arbitrary",),
    )(*dqkv_parts, w_pad, dxo, x, g_row, cos, sin)


def pool_bwd(dxo, x, g_row, w_in, w_grp, scale, w_out, zr, duty=None):
    t_len = x.shape[0]
    tm = _tile_rows(t_len)
    nt = t_len // tm

    def body(dxo_ref, x_ref, g_ref, win_ref, wgrp_ref, scale_ref, wout_ref, zr_ref,
             dzs_ref, du_ref, dx_ref, dn_ref, dsc_ref, ebuf):
        i = pl.program_id(0)
        t = nt - 1 - i

        @pl.when(i == 0)
        def _():
            ebuf[pl.ds(tm, POOL_HALO), :] = jnp.zeros((POOL_HALO, D_MODEL), F32)
            dn_ref[...] = jnp.zeros_like(dn_ref)
            dsc_ref[...] = jnp.zeros_like(dsc_ref)

        dxo_t = dxo_ref[...]
        dz = _dot_nt(dxo_t.astype(BF16), wout_ref[...])
        dsc_ref[...] += jnp.sum(dz * zr_ref[...].astype(F32), axis=0, keepdims=True)
        dzs_ref[...] = (dz * scale_ref[...]).astype(BF16)
        row = t * tm + lax.broadcasted_iota(jnp.int32, (tm, 1), 0)
        for gi, w in enumerate(POOL_WINDOWS):
            cols = pl.ds(gi * POOL_GROUP_DIM, POOL_GROUP_DIM)
            dp_g = _dot_nt(dzs_ref[:, cols], wgrp_ref[gi])
            inv_cnt = 1.0 / jnp.minimum(row + 1, w).astype(F32)
            ebuf[pl.ds(0, tm), cols] = dp_g * inv_cnt
            acc = -dp_g
            for j in range(w):
                acc = acc + ebuf[pl.ds(j, tm), cols]
            du_ref[:, cols] = acc.astype(BF16)
        ebuf[pl.ds(tm, POOL_HALO), :] = ebuf[pl.ds(0, POOL_HALO), :]
        dh = _dot_nt(du_ref[...], win_ref[...])
        dx, dn = _rms_bwd(dh, x_ref[...], g_ref[...])
        dx_ref[...] = dxo_t + dx
        dn_ref[...] += dn

    row_spec = pl.BlockSpec((tm, D_MODEL), lambda i: (nt - 1 - i, 0))
    full = lambda shape: pl.BlockSpec(shape, lambda i: (0,) * len(shape))
    vec = full((1, D_MODEL))
    return _pcall(
        body, name="pool_bwd", grid=(nt,),
        in_specs=[row_spec, row_spec, vec, full((D_MODEL, D_MODEL)), full((4, POOL_GROUP_DIM, POOL_GROUP_DIM)),
                  vec, full((D_MODEL, D_MODEL)), row_spec],
        out_specs=[row_spec, row_spec, row_spec, vec, vec],
        out_shape=[_sds((t_len, D_MODEL), BF16), _sds((t_len, D_MODEL), BF16), _sds((t_len, D_MODEL), F32),
                   _sds((1, D_MODEL), F32), _sds((1, D_MODEL), F32)],
        scratch_shapes=[pltpu.VMEM((tm + POOL_HALO, D_MODEL), F32)],
        semantics=("arbitrary",), duty=duty,
    )(dxo, x, g_row, w_in, w_grp, scale, w_out, zr)


def _mesh_pos():
    return lax.axis_index("x"), lax.axis_index("y"), lax.axis_index("c")


def _other_chips(x, y):
    return [(1 - x, y), (x, 1 - y), (1 - x, 1 - y)]


def _remote(src, dst, send_sem, recv_sem, device):
    return pltpu.make_async_remote_copy(src_ref=src, dst_ref=dst, send_sem=send_sem, recv_sem=recv_sem,
                                        device_id=device, device_id_type=MESH)


class _Duty:
    aliases = {}

    def mid(self, ins, outs, sems):
        pass


class Together(_Duty):
    def __init__(self, duties):
        self.duties = duties
        self.ins = [a for d in duties for a in d.ins]
        self.out_shape = [s for d in duties for s in d.out_shape]
        self.scratch = [s for d in duties for s in d.scratch]
        self.aliases = {}
        i0 = o0 = 0
        for d in duties:
            self.aliases.update({i0 + i: o0 + o for i, o in d.aliases.items()})
            i0 += len(d.ins)
            o0 += len(d.out_shape)

    def _each(self, ins, outs, sems):
        i0 = o0 = s0 = 0
        for d in self.duties:
            ni, no, ns = len(d.ins), len(d.out_shape), len(d.scratch)
            yield d, ins[i0:i0 + ni], outs[o0:o0 + no], sems[s0:s0 + ns]
            i0, o0, s0 = i0 + ni, o0 + no, s0 + ns

    def split(self, outs):
        return [list(o) for _, _, o, _ in self._each(self.ins, outs, self.scratch)]

    def start(self, ins, outs, sems):
        for d, i, o, s in self._each(ins, outs, sems):
            d.start(i, o, s)

    def mid(self, ins, outs, sems):
        for d, i, o, s in self._each(ins, outs, sems):
            d.mid(i, o, s)

    def finish(self, ins, outs, sems):
        for d, i, o, s in self._each(ins, outs, sems):
            d.finish(i, o, s)


def run_duty(duty, name):
    d_in, d_out = len(duty.ins), len(duty.out_shape)

    def body(*refs):
        ins, outs, sems = refs[:d_in], refs[d_in:d_in + d_out], refs[d_in + d_out:]
        duty.start(ins, outs, sems)
        duty.mid(ins, outs, sems)
        duty.finish(ins, outs, sems)

    return pl.pallas_call(
        body, name=name, out_shape=list(duty.out_shape), in_specs=[_ANY] * d_in, out_specs=[_ANY] * d_out,
        scratch_shapes=list(duty.scratch), input_output_aliases=dict(duty.aliases),
        compiler_params=pltpu.CompilerParams(has_side_effects=True),
    )(*duty.ins)


class GatherWeights(_Duty):
    def __init__(self, shards):
        n = self.n = len(shards)
        self.halves = [s.shape[0] // 2 for s in shards]
        my_slot = 2 * lax.axis_index("x") + lax.axis_index("y")
        staged = [lax.dynamic_update_slice(lax.empty((N_SHARDS,) + s.shape, s.dtype), s[None], (my_slot, 0, 0))
                  for s in shards]
        self.ins = list(shards) + staged
        self.out_shape = [_sds((N_SHARDS,) + s.shape, s.dtype) for s in shards]
        self.aliases = {n + a: a for a in range(n)}
        self.scratch = [pltpu.SemaphoreType.DMA((n, 6)), pltpu.SemaphoreType.DMA((n, 6))]

    def _over_ici(self, ins, outs, sems):
        x, y, c = _mesh_pos()
        return [_remote(ins[a].at[pl.ds(c * h, h)], outs[a].at[2 * x + y, pl.ds(c * h, h)],
                        sems[0].at[a, j], sems[1].at[a, j], (*chip, c))
                for a, h in enumerate(self.halves) for j, chip in enumerate(_other_chips(x, y))]

    def _forwards(self, outs, sems, half_of):
        x, y, c = _mesh_pos()
        cps = []
        for a, h in enumerate(self.halves):
            for j, chip in enumerate(_other_chips(x, y)):
                slot = outs[a].at[2 * chip[0] + chip[1], pl.ds(half_of(c) * h, h)]
                cps.append(_remote(slot, slot, sems[0].at[a, 3 + j], sems[1].at[a, 3 + j], (x, y, 1 - c)))
        return cps

    def start(self, ins, outs, sems):
        for cp in self._over_ici(ins, outs, sems):
            cp.start()

    def mid(self, ins, outs, sems):
        x, y, c = _mesh_pos()
        forwards = self._forwards(outs, sems, lambda core: core)
        k = 0
        for a, h in enumerate(self.halves):
            for j, chip in enumerate(_other_chips(x, y)):
                slot = outs[a].at[2 * chip[0] + chip[1], pl.ds(c * h, h)]
                _remote(slot, slot, sems[0].at[a, j], sems[1].at[a, j], (*chip, c)).wait_recv()
                forwards[k].start()
                k += 1

    def finish(self, ins, outs, sems):
        for cp in self._forwards(outs, sems, lambda core: 1 - core):
            cp.wait_recv()
        for cp in self._over_ici(ins, outs, sems) + self._forwards(outs, sems, lambda core: core):
            cp.wait_send()


class GradReducer:
    def __init__(self, c_idx, pos_idx):
        self.c_idx, self.pos_idx = c_idx, pos_idx
        self.in_flight = []
        self.done = {}

    def push(self, name, grad):
        self.in_flight.append(dict(name=name, stage="halves", data=grad))

    def _duties(self):
        make = {"halves": SiblingHalves, "exchange": ChipExchange, "share": SiblingShare}
        return Together([make[w["stage"]]([w["data"]]) for w in self.in_flight])

    def _advance(self, duties, outs):
        still = []
        for w, (res,) in zip(self.in_flight, duties.split(outs)):
            if w["stage"] == "halves":
                partial = add_my_half(w["data"], res, self.c_idx, f"rs_add_{w['name']}")
                still.append(dict(name=w["name"], stage="exchange", data=partial))
            elif w["stage"] == "exchange":
                reduced = sum_slots(res, w["data"], self.pos_idx, f"rs_sum_{w['name']}")
                still.append(dict(name=w["name"], stage="share", data=reduced))
            else:
                self.done[w["name"]] = res
        self.in_flight = still

    def carried_by(self, fn, *args, **kw):
        if not self.in_flight:
            return fn(*args, **kw)
        duties = self._duties()
        out, duty_outs = fn(*args, duty=duties, **kw)
        self._advance(duties, duty_outs)
        return out

    def drain(self, name):
        step = 0
        while self.in_flight:
            duties = self._duties()
            self._advance(duties, run_duty(duties, f"{name}{step}"))
            step += 1


class SiblingHalves(_Duty):
    def __init__(self, grads):
        n = len(grads)
        self.halves = [g.shape[1] // 2 for g in grads]
        self.ins = list(grads)
        self.out_shape = [_sds((N_SHARDS, h, g.shape[2]), g.dtype) for g, h in zip(grads, self.halves)]
        self.scratch = [pltpu.SemaphoreType.DMA((n,)), pltpu.SemaphoreType.DMA((n,))]

    def _copies(self, ins, outs, sems):
        x, y, c = _mesh_pos()
        return [_remote(ins[a].at[:, pl.ds((1 - c) * h, h)], outs[a], sems[0].at[a], sems[1].at[a], (x, y, 1 - c))
                for a, h in enumerate(self.halves)]

    def start(self, ins, outs, sems):
        for cp in self._copies(ins, outs, sems):
            cp.start()

    def finish(self, ins, outs, sems):
        for cp in self._copies(ins, outs, sems):
            cp.wait()


class ChipExchange(_Duty):
    def __init__(self, parts):
        n = self.n = len(parts)
        self.ins = list(parts)
        self.out_shape = [_sds(p.shape, p.dtype) for p in parts]
        self.scratch = [pltpu.SemaphoreType.DMA((n, 3)), pltpu.SemaphoreType.DMA((n, 3))]

    def _copies(self, ins, outs, sems, arriving):
        x, y, c = _mesh_pos()
        cps = []
        for a in range(self.n):
            for j, chip in enumerate(_other_chips(x, y)):
                theirs = 2 * chip[0] + chip[1]
                src = outs[a].at[theirs] if arriving else ins[a].at[theirs]
                dst = outs[a].at[theirs] if arriving else outs[a].at[2 * x + y]
                cps.append(_remote(src, dst, sems[0].at[a, j], sems[1].at[a, j], (*chip, c)))
        return cps

    def start(self, ins, outs, sems):
        for cp in self._copies(ins, outs, sems, False):
            cp.start()

    def finish(self, ins, outs, sems):
        for cp in self._copies(ins, outs, sems, True):
            cp.wait_recv()
        for cp in self._copies(ins, outs, sems, False):
            cp.wait_send()


class SiblingShare(_Duty):
    def __init__(self, reduced):
        n = self.n = len(reduced)
        self.ins = list(reduced)
        self.out_shape = [_sds(r.shape, r.dtype) for r in reduced]
        self.aliases = {a: a for a in range(n)}
        self.scratch = [pltpu.SemaphoreType.DMA((n,)), pltpu.SemaphoreType.DMA((n,))]

    def _copies(self, outs, sems, half_of):
        x, y, c = _mesh_pos()
        cps = []
        for a in range(self.n):
            h = outs[a].shape[0] // 2
            rows = outs[a].at[pl.ds(half_of(c) * h, h)]
            cps.append(_remote(rows, rows, sems[0].at[a], sems[1].at[a], (x, y, 1 - c)))
        return cps

    def start(self, ins, outs, sems):
        for cp in self._copies(outs, sems, lambda core: core):
            cp.start()

    def finish(self, ins, outs, sems):
        for cp in self._copies(outs, sems, lambda core: 1 - core):
            cp.wait_recv()
        for cp in self._copies(outs, sems, lambda core: core):
            cp.wait_send()


def allreduce_small(v):
    def body(v_ref, o_ref, buf, send_sems, recv_sems):
        x, y, c = _mesh_pos()
        me = 4 * x + 2 * y + c
        buf[me] = v_ref[...]
        flip = lambda p, f: 1 - p if f else p
        peers = [(flip(x, k & 4), flip(y, k & 2), flip(c, k & 1)) for k in range(1, N_DEV)]
        cps = []
        for k, peer in enumerate(peers):
            cp = _remote(v_ref, buf.at[me], send_sems.at[k], recv_sems.at[k], peer)
            cp.start()
            cps.append(cp)
        for k, peer in enumerate(peers):
            slot = buf.at[4 * peer[0] + 2 * peer[1] + peer[2]]
            _remote(slot, slot, send_sems.at[k], recv_sems.at[k], peer).wait_recv()
        for cp in cps:
            cp.wait_send()
        acc = buf[0]
        for i in range(1, N_DEV):
            acc = acc + buf[i]
        o_ref[...] = acc

    vm = pl.BlockSpec(memory_space=pltpu.VMEM)
    return pl.pallas_call(
        body, name="allreduce_small", out_shape=_sds(v.shape, v.dtype), in_specs=[vm], out_specs=vm,
        scratch_shapes=[pltpu.VMEM((N_DEV,) + v.shape, v.dtype),
                        pltpu.SemaphoreType.DMA((N_DEV - 1,)), pltpu.SemaphoreType.DMA((N_DEV - 1,))],
        compiler_params=pltpu.CompilerParams(has_side_effects=True),
    )(v)


def add_my_half(grad, theirs, c_idx, name):
    _, r, cols = grad.shape
    h = r // 2

    def body(c_ref, g_ref, t_ref, o_ref):
        o_ref[...] = (g_ref[...] + t_ref[...]).astype(BF16)

    slot = pl.BlockSpec((None, h, cols), lambda s, c: (s, 0, 0))
    grid_spec = pltpu.PrefetchScalarGridSpec(
        num_scalar_prefetch=1, grid=(N_SHARDS,),
        in_specs=[pl.BlockSpec((None, h, cols), lambda s, c: (s, c[0], 0)), slot], out_specs=slot)
    return pl.pallas_call(
        body, name=name, grid_spec=grid_spec, out_shape=_sds((N_SHARDS, h, cols), BF16),
        compiler_params=pltpu.CompilerParams(dimension_semantics=("arbitrary",), vmem_limit_bytes=VMEM_LIMIT_BYTES),
    )(c_idx, grad, theirs)


def sum_slots(received, mine, pos_idx, name):
    _, h, cols = received.shape

    def body(pos_ref, r_ref, m_ref, o_ref):
        acc = None
        for k in range(N_SHARDS):
            term = jnp.where(pos_ref[0] == k, m_ref[k], r_ref[k]).astype(F32)
            acc = term if acc is None else acc + term
        o_ref[...] = acc

    whole = pl.BlockSpec((N_SHARDS, h, cols), lambda i, pos: (0, 0, 0))
    grid_spec = pltpu.PrefetchScalarGridSpec(
        num_scalar_prefetch=1, grid=(1,), in_specs=[whole, whole],
        out_specs=pl.BlockSpec((h, cols), lambda i, pos: (pos[1], 0)))
    return pl.pallas_call(
        body, name=name, grid_spec=grid_spec, out_shape=_sds((2 * h, cols), F32),
        compiler_params=pltpu.CompilerParams(dimension_semantics=("arbitrary",), vmem_limit_bytes=VMEM_LIMIT_BYTES),
    )(pos_idx, received, mine)


def adamw(name, grads, w, m, v):
    n_layers, r, cols = w.shape
    tr = r // 2 if r % 16 == 0 else r
    bias1 = 1.0 - ADAM_B1 ** ADAM_STEP
    bias2 = 1.0 - ADAM_B2 ** ADAM_STEP

    def body(*refs):
        g_refs = refs[:n_layers]
        w_ref, m_ref, v_ref, go_ref, d_ref, mo_ref, vo_ref = refs[n_layers:]
        g = g_refs[0][...]
        for layer in range(1, n_layers):
            g = jnp.where(pl.program_id(0) == layer, g_refs[layer][...], g)
        m_new = ADAM_B1 * m_ref[...] + (1.0 - ADAM_B1) * g
        v_new = ADAM_B2 * v_ref[...] + (1.0 - ADAM_B2) * (g * g)
        m_hat = m_new / bias1
        v_hat = v_new / bias2
        go_ref[...] = g
        d_ref[...] = -ADAM_LR * (m_hat / (jnp.sqrt(v_hat) + ADAM_EPS) + ADAM_WD * w_ref[...])
        mo_ref[...] = m_new
        vo_ref[...] = v_new

    g_spec = pl.BlockSpec((tr, cols), lambda l, i: (i, 0))
    lay_spec = pl.BlockSpec((None, tr, cols), lambda l, i: (l, i, 0))
    shape = _sds((n_layers, r, cols), F32)
    return _pcall(
        body, name=name, grid=(n_layers, r // tr),
        in_specs=[g_spec] * n_layers + [lay_spec] * 3, out_specs=[lay_spec] * 4,
        out_shape=[shape] * 4, semantics=("arbitrary", "arbitrary"),
    )(*grads, w, m, v)


def kernel(x, norm_mix, norm_ffn, norm_final, pool_w_in, pool_w_group, pool_scale, pool_w_out, attn_w_qkv, attn_w_out, ffn_w_gate, ffn_w_up, ffn_w_down, loss_target, m_norm_mix, m_norm_ffn, m_norm_final, m_pool_w_in, m_pool_w_group, m_pool_scale, m_pool_w_out, m_attn_w_qkv, m_attn_w_out, m_ffn_w_gate, m_ffn_w_up, m_ffn_w_down, v_norm_mix, v_norm_ffn, v_norm_final, v_pool_w_in, v_pool_w_group, v_pool_scale, v_pool_w_out, v_attn_w_qkv, v_attn_w_out, v_ffn_w_gate, v_ffn_w_up, v_ffn_w_down):
    t_len = x.shape[1]
    x0 = x.reshape(t_len, D_MODEL)
    target = loss_target.reshape(t_len, D_MODEL)
    row = lambda a: a.reshape(1, D_MODEL)

    grp_rows = POOL_GROUP_DIM // N_SHARDS
    bf = lambda a: a.astype(BF16)
    gate_t, up_t = jnp.swapaxes(ffn_w_gate, 1, 2), jnp.swapaxes(ffn_w_up, 1, 2)
    pool_shards = [bf(pool_w_in[0]), bf(pool_w_group[0].reshape(4 * grp_rows, POOL_GROUP_DIM)), bf(pool_w_out[0])]
    ffn0_shards = [bf(gate_t[0]), bf(up_t[0]), bf(ffn_w_down[0])]
    late_shards = [bf(attn_w_qkv[0]), bf(attn_w_out[0]), bf(gate_t[1]), bf(up_t[1]), bf(ffn_w_down[1])]
    cos, sin = rope_tables(t_len)
    c_idx = lax.axis_index("c").astype(jnp.int32).reshape(1)
    pos_idx = jnp.stack([2 * lax.axis_index("x") + lax.axis_index("y"), lax.axis_index("c")]).astype(jnp.int32)
    chip_rows = lambda g: g.reshape(N_SHARDS, D_MODEL // N_SHARDS, D_MODEL)

    g_pool = run_duty(GatherWeights(pool_shards), "gather_pool")
    w_in = g_pool[0].reshape(D_MODEL, D_MODEL)
    w_grp = g_pool[1].reshape(N_SHARDS, 4, grp_rows, POOL_GROUP_DIM).transpose(1, 0, 2, 3).reshape(
        4, POOL_GROUP_DIM, POOL_GROUP_DIM)
    w_out = g_pool[2].reshape(D_MODEL, D_MODEL)
    (h0, p, zr, z, x1), ffn0 = pool_fwd(x0, row(norm_mix[0]), w_in, w_grp, pool_scale, w_out,
                                        duty=GatherWeights(ffn0_shards))
    (h1, gate0, up0, act0, x2), late = ffn_fwd(x1, row(norm_ffn[0]), *ffn0, "ffn_fwd0",
                                               duty=GatherWeights(late_shards))
    w_qkv = pad_qkv_weight(late[0])
    w_ao = jnp.concatenate(pad_groups(late[1].reshape(D_MODEL, D_MODEL), 0), axis=0)
    ffn1 = late[2:5]
    h2, *qkv_parts = qkv_fwd(x2, row(norm_mix[1]), w_qkv, cos, sin)
    o_parts, lse_parts = [], []
    for gi in range(3):
        o_g, lse_g = attn_fwd(qkv_parts[gi], gi, f"attn_fwd_g{gi}")
        o_parts.append(o_g)
        lse_parts.append(lse_g)
    x3, merged, o_nat, lse_nat = attn_out_fwd(x2, o_parts, lse_parts, w_ao)
    h3, gate1, up1, act1, dx4, d_norm_final, loss_local = ffn_fwd(
        x3, row(norm_ffn[1]), *ffn1, "ffn_fwd1", loss_head=(row(norm_final), target))

    red = GradReducer(c_idx, pos_idx)
    dgate1, dup1, dx3, d_nf1 = ffn_bwd(dx4, x3, row(norm_ffn[1]), gate1, up1, *ffn1, "ffn_bwd1")
    red.push("gate1", wgrad_row_sharded("wgrad_gate1", dgate1, h3))
    red.push("up1", red.carried_by(wgrad_row_sharded, "wgrad_up1", dup1, h3))
    red.push("down1", red.carried_by(wgrad_row_sharded, "wgrad_down1", act1, dx4))
    do_parts, c_parts = red.carried_by(attn_out_bwd, dx3, w_ao, o_nat, lse_nat)
    g_ao = red.carried_by(wgrad_full, "wgrad_attn_out", merged, dx3)
    red.push("attn_out", chip_rows(unpad_groups(jnp.split(g_ao, 3, axis=0), 0)))
    dqkv_parts = [red.carried_by(attn_bwd, qkv_parts[gi], do_parts[gi], lse_parts[gi], c_parts[gi], gi,
                                 f"attn_bwd_g{gi}") for gi in range(3)]
    dqkv, dx2, d_nm1 = qkv_bwd(dqkv_parts, w_qkv, dx3, x2, row(norm_mix[1]), cos, sin)
    red.push("qkv", unpad_qkv_grad(wgrad_col_sharded("wgrad_qkv", h2, dqkv)))

    dgate0, dup0, dx1, d_nf0 = red.carried_by(ffn_bwd, dx2, x1, row(norm_ffn[0]), gate0, up0, *ffn0, "ffn_bwd0")
    red.push("gate0", red.carried_by(wgrad_row_sharded, "wgrad_gate0", dgate0, h1))
    red.push("up0", red.carried_by(wgrad_row_sharded, "wgrad_up0", dup0, h1))
    red.push("down0", red.carried_by(wgrad_row_sharded, "wgrad_down0", act0, dx2))
    red.push("pool_out", chip_rows(red.carried_by(wgrad_full, "wgrad_pool_out", z, dx1)))
    dzs, du, dx0, d_nm0, d_scale = pool_bwd(dx1, x0, row(norm_mix[0]), w_in, w_grp, pool_scale, w_out, zr)
    g_grp = red.carried_by(wgrad_pool_group, "wgrad_pool_group", p, dzs)
    red.push("pool_group", g_grp.reshape(N_SHARDS, 4 * grp_rows, POOL_GROUP_DIM))
    red.push("pool_in", chip_rows(red.carried_by(wgrad_full, "wgrad_pool_in", h0, du)))
    red.drain("rs_tail")
    full = [red.done[nm] for nm in ("pool_in", "pool_group", "pool_out", "qkv", "attn_out",
                                    "gate0", "gate1", "up0", "up1", "down0", "down1")]

    zero_row = jnp.zeros((1, D_MODEL), F32)
    small = jnp.concatenate([d_nm0, d_nm1, d_nf0, d_nf1, d_norm_final, d_scale,
                             jnp.broadcast_to(loss_local, (1, D_MODEL)), zero_row], axis=0)
    small = allreduce_small(small)
    loss = small[6, 0]

    pack = lambda a, b, c, d: jnp.concatenate([a, b, row(c), d, zero_row, zero_row], axis=0)[None]
    sg, sd, sm, sv = adamw("adamw_small", [small],
                           pack(norm_mix, norm_ffn, norm_final, pool_scale),
                           pack(m_norm_mix, m_norm_ffn, m_norm_final, m_pool_scale),
                           pack(v_norm_mix, v_norm_ffn, v_norm_final, v_pool_scale))
    unpack = lambda a: (a[0, 0:2], a[0, 2:4], a[0, 4], a[0, 5:6])

    def update(name, grads, w, m, v, transposed=False):
        if transposed:
            w, m, v = (jnp.swapaxes(a, 1, 2) for a in (w, m, v))
        n_layers = len(grads)
        shp = (n_layers,) + grads[0].shape
        outs = [o.reshape(w.shape) for o in adamw(name, grads, w.reshape(shp), m.reshape(shp), v.reshape(shp))]
        return [jnp.swapaxes(o, 1, 2) for o in outs] if transposed else outs

    big = [
        update("adamw_pool_in", [full[0]], pool_w_in, m_pool_w_in, v_pool_w_in),
        update("adamw_pool_group", [full[1]], pool_w_group, m_pool_w_group, v_pool_w_group),
        update("adamw_pool_out", [full[2]], pool_w_out, m_pool_w_out, v_pool_w_out),
        update("adamw_qkv", [full[3]], attn_w_qkv, m_attn_w_qkv, v_attn_w_qkv),
        update("adamw_attn_out", [full[4]], attn_w_out, m_attn_w_out, v_attn_w_out),
        update("adamw_gate", [full[5], full[6]], ffn_w_gate, m_ffn_w_gate, v_ffn_w_gate, transposed=True),
        update("adamw_up", [full[7], full[8]], ffn_w_up, m_ffn_w_up, v_ffn_w_up, transposed=True),
        update("adamw_down", [full[9], full[10]], ffn_w_down, m_ffn_w_down, v_ffn_w_down),
    ]

    def leaves(k, small_vals):
        nm, nf, nfin, psc = unpack(small_vals)
        return [nm, nf, nfin, big[0][k], big[1][k], psc, big[2][k], big[3][k], big[4][k],
                big[5][k], big[6][k], big[7][k]]

    grad_x = dx0.reshape(x.shape)
    return (loss, grad_x, *leaves(0, sg), *leaves(1, sd), *leaves(2, sm), *leaves(3, sv))
```

```python
import math

import jax
import jax.numpy as jnp
from jax import lax
from jax.experimental import pallas as pl
from jax.experimental.pallas import tpu as pltpu

F32 = jnp.float32
BF16 = jnp.bfloat16

D_MODEL = 1024
N_SHARDS = 4
N_DEV = 8
D_FF = 2816
FF_SHARD = D_FF // N_SHARDS
HEAD_DIM = 64
QKV_SHARD = 3 * D_MODEL // N_SHARDS
POOL_WINDOWS = (2, 4, 8, 16)
POOL_GROUP_DIM = 256
POOL_HALO = 16
ATTN_W = 128
GROUP_LANES = (0, 384, 704, 1024)
GROUP_HEADS = (6, 5, 5)
GROUP_DIL = (1, 4, 16)
ROPE_THETA = 10000.0
EPS = 1e-6
NEG_INF = -1e30
LANE = 128
VMEM_LIMIT_BYTES = 60 * 1024 * 1024

ADAM_LR = 0.001
ADAM_B1 = 0.9
ADAM_B2 = 0.999
ADAM_EPS = 1e-08
ADAM_WD = 0.01
ADAM_STEP = 10

NT_DIMS = (((1,), (1,)), ((), ()))
TN_DIMS = (((0,), (0,)), ((), ()))
MESH = pl.DeviceIdType.MESH


_ANY = pl.BlockSpec(memory_space=pl.ANY)


def _pcall(body, *, name, out_shape, grid=None, in_specs=None, out_specs=None, scratch_shapes=(),
           semantics=None, duty=None):
    kw = {}
    if in_specs is not None and duty is None:
        kw["in_specs"] = in_specs
    if out_specs is not None and duty is None:
        kw["out_specs"] = out_specs
    if grid is not None:
        kw["grid"] = grid
    params = dict(dimension_semantics=semantics, vmem_limit_bytes=VMEM_LIMIT_BYTES)
    if duty is None:
        return pl.pallas_call(body, name=name, out_shape=out_shape, scratch_shapes=list(scratch_shapes),
                              compiler_params=pltpu.CompilerParams(**params), **kw)

    single = not isinstance(out_shape, (list, tuple))
    c_out_shape = [out_shape] if single else list(out_shape)
    c_out_specs = [out_specs] if single else list(out_specs)
    n_in, n_out, n_scr = len(in_specs), len(c_out_shape), len(scratch_shapes)
    d_in, d_out = len(duty.ins), len(duty.out_shape)
    total = math.prod(grid)
    mid_step = (5 * total) // 6

    def wrapped(*refs):
        c_in, d_ins = refs[:n_in], refs[n_in:n_in + d_in]
        o0 = n_in + d_in
        c_outs, d_outs = refs[o0:o0 + n_out], refs[o0 + n_out:o0 + n_out + d_out]
        s0 = o0 + n_out + d_out
        c_scr, d_sems = refs[s0:s0 + n_scr], refs[s0 + n_scr:]
        step = pl.program_id(0)
        for ax in range(1, len(grid)):
            step = step * grid[ax] + pl.program_id(ax)

        @pl.when(step == 0)
        def _():
            duty.start(d_ins, d_outs, d_sems)

        body(*c_in, *c_outs, *c_scr)

        @pl.when(step == mid_step)
        def _():
            duty.mid(d_ins, d_outs, d_sems)

        @pl.when(step == total - 1)
        def _():
            duty.finish(d_ins, d_outs, d_sems)

    call = pl.pallas_call(
        wrapped, name=name, grid=grid,
        in_specs=list(in_specs) + [_ANY] * d_in, out_specs=c_out_specs + [_ANY] * d_out,
        out_shape=c_out_shape + list(duty.out_shape),
        scratch_shapes=list(scratch_shapes) + list(duty.scratch),
        input_output_aliases={n_in + i: n_out + o for i, o in duty.aliases.items()},
        compiler_params=pltpu.CompilerParams(has_side_effects=True, **params))

    def run(*args):
        outs = call(*args, *duty.ins)
        c = outs[:n_out]
        return (c[0] if single else list(c)), list(outs[n_out:])

    return run


def _sds(shape, dtype):
    return jax.ShapeDtypeStruct(tuple(shape), dtype)


def _dot(a, b):
    return jnp.dot(a, b, preferred_element_type=F32)


def _dot_nt(a, b):
    return lax.dot_general(a, b, NT_DIMS, preferred_element_type=F32)


def _dot_tn(a, b):
    return lax.dot_general(a, b, TN_DIMS, preferred_element_type=F32)


def _rms_fwd(x, g):
    r = lax.rsqrt(jnp.mean(x * x, axis=-1, keepdims=True) + EPS)
    return x * r * g


def _rms_bwd(dh, x, g):
    r = lax.rsqrt(jnp.mean(x * x, axis=-1, keepdims=True) + EPS)
    xh = x * r
    dg = jnp.sum(dh * xh, axis=0, keepdims=True)
    dxh = dh * g
    dx = r * (dxh - xh * jnp.mean(dxh * xh, axis=-1, keepdims=True))
    return dx, dg


def _sigmoid(x):
    return 0.5 * jnp.tanh(0.5 * x) + 0.5


def _tile_rows(t):
    return min(512, t)


def _sub_tiles(tm, n_sub=2):
    rows = tm // n_sub
    return [pl.ds(i * rows, rows) for i in range(n_sub)]


def _wgrad_rows(t):
    return min(2048, t)


def pool_fwd(x, g_row, w_in, w_grp, scale, w_out, duty=None):
    t_len = x.shape[0]
    tm = _tile_rows(t_len)

    def body(x_ref, g_ref, win_ref, wgrp_ref, scale_ref, wout_ref,
             h_ref, p_ref, zr_ref, z_ref, xo_ref, ubuf):
        t = pl.program_id(0)

        @pl.when(t == 0)
        def _():
            ubuf[pl.ds(0, POOL_HALO), :] = jnp.zeros((POOL_HALO, D_MODEL), F32)

        x_t = x_ref[...]
        h = _rms_fwd(x_t, g_ref[...]).astype(BF16)
        h_ref[...] = h
        ubuf[pl.ds(POOL_HALO, tm), :] = _dot(h, win_ref[...])
        row = t * tm + lax.broadcasted_iota(jnp.int32, (tm, 1), 0)
        for gi, w in enumerate(POOL_WINDOWS):
            cols = pl.ds(gi * POOL_GROUP_DIM, POOL_GROUP_DIM)
            u_g = ubuf[pl.ds(POOL_HALO, tm), cols]
            acc = u_g
            for j in range(1, w):
                acc = acc + ubuf[pl.ds(POOL_HALO - j, tm), cols]
            inv_cnt = 1.0 / jnp.minimum(row + 1, w).astype(F32)
            p_g = (acc * inv_cnt - u_g).astype(BF16)
            p_ref[:, cols] = p_g
            z_g = _dot(p_g, wgrp_ref[gi])
            zr_ref[:, cols] = z_g.astype(BF16)
            z_ref[:, cols] = (z_g * scale_ref[:, cols]).astype(BF16)
        ubuf[pl.ds(0, POOL_HALO), :] = ubuf[pl.ds(tm, POOL_HALO), :]
        xo_ref[...] = x_t + _dot(z_ref[...], wout_ref[...])

    row_spec = pl.BlockSpec((tm, D_MODEL), lambda t: (t, 0))
    full2 = lambda shape: pl.BlockSpec(shape, lambda t: (0,) * len(shape))
    return _pcall(
        body, name="pool_fwd", grid=(t_len // tm,),
        in_specs=[row_spec, full2((1, D_MODEL)), full2((D_MODEL, D_MODEL)),
                  full2((4, POOL_GROUP_DIM, POOL_GROUP_DIM)), full2((1, D_MODEL)), full2((D_MODEL, D_MODEL))],
        out_specs=[row_spec] * 5,
        out_shape=[_sds((t_len, D_MODEL), BF16)] * 4 + [_sds((t_len, D_MODEL), F32)],
        scratch_shapes=[pltpu.VMEM((tm + POOL_HALO, D_MODEL), F32)],
        semantics=("arbitrary",), duty=duty,
    )(x, g_row, w_in, w_grp, scale, w_out)


def ffn_fwd(x, g_row, w_gate_t, w_up_t, w_down, name, duty=None, loss_head=None):
    t_len = x.shape[0]
    tm = min(1024 if loss_head is None else 512, t_len)
    n_in = 5 if loss_head is None else 7

    def body(*refs):
        x_ref, g_ref, wg_ref, wu_ref, wd_ref = refs[:5]
        h_ref, go_ref, uo_ref, ao_ref = refs[n_in:n_in + 4]
        hbuf, acc = refs[-2:]
        t = pl.program_id(0)
        s = pl.program_id(1)

        @pl.when(s == 0)
        def _():
            h = _rms_fwd(x_ref[...], g_ref[...]).astype(BF16)
            hbuf[...] = h
            h_ref[...] = h
            acc[...] = jnp.zeros_like(acc)

        h = hbuf[...]
        gate = _dot_nt(h, wg_ref[...])
        up = _dot_nt(h, wu_ref[...])
        go_ref[...] = gate.astype(BF16)
        uo_ref[...] = up.astype(BF16)
        act = (gate * _sigmoid(gate) * up).astype(BF16)
        ao_ref[...] = act
        acc[...] += _dot(act, wd_ref[...])

        if loss_head is None:
            @pl.when(s == N_SHARDS - 1)
            def _():
                refs[n_in + 4][...] = x_ref[...] + acc[...]
        else:
            gf_ref, tgt_ref = refs[5:7]
            dx_ref, dn_ref, loss_ref = refs[n_in + 4:n_in + 7]

            @pl.when(jnp.logical_and(s == 0, t == 0))
            def _():
                dn_ref[...] = jnp.zeros_like(dn_ref)
                loss_ref[...] = jnp.zeros_like(loss_ref)

            @pl.when(s == N_SHARDS - 1)
            def _():
                x_out = x_ref[...] + acc[...]
                gf = gf_ref[...]
                diff = _rms_fwd(x_out, gf) - tgt_ref[...]
                loss_ref[...] += 0.5 * jnp.sum(jnp.mean(diff * diff, axis=-1, keepdims=True), axis=0, keepdims=True)
                dx, dn = _rms_bwd(diff * (1.0 / D_MODEL), x_out, gf)
                dx_ref[...] = dx
                dn_ref[...] += dn

    row_spec = pl.BlockSpec((tm, D_MODEL), lambda t, s: (t, 0))
    vec_spec = pl.BlockSpec((1, D_MODEL), lambda t, s: (0, 0))
    row_w = pl.BlockSpec((None, FF_SHARD, D_MODEL), lambda t, s: (s, 0, 0))
    act_spec = pl.BlockSpec((None, tm, FF_SHARD), lambda t, s: (s, t, 0))
    in_specs = [row_spec, vec_spec, row_w, row_w, row_w]
    out_specs = [row_spec, act_spec, act_spec, act_spec, row_spec]
    out_shape = [_sds((t_len, D_MODEL), BF16)] + [_sds((N_SHARDS, t_len, FF_SHARD), BF16)] * 3 + [
        _sds((t_len, D_MODEL), F32)]
    args = [x, g_row, w_gate_t, w_up_t, w_down]
    if loss_head is not None:
        in_specs += [vec_spec, row_spec]
        out_specs += [vec_spec, pl.BlockSpec((1, 1), lambda t, s: (0, 0))]
        out_shape += [_sds((1, D_MODEL), F32), _sds((1, 1), F32)]
        args += list(loss_head)
    return _pcall(
        body, name=name, grid=(t_len // tm, N_SHARDS), in_specs=in_specs, out_specs=out_specs, out_shape=out_shape,
        scratch_shapes=[pltpu.VMEM((tm, D_MODEL), BF16), pltpu.VMEM((tm, D_MODEL), F32)],
        semantics=("arbitrary", "arbitrary"), duty=duty,
    )(*args)


def ffn_bwd(dxo, x, g_row, gate, up, w_gate_t, w_up_t, w_down, name, duty=None):
    t_len = x.shape[0]
    tm = _tile_rows(t_len)

    def body(dxo_ref, x_ref, g_ref, gate_ref, up_ref, wg_ref, wu_ref, wd_ref,
             dg_ref, du_ref, dx_ref, dn_ref, dxb, dh):
        t = pl.program_id(0)
        s = pl.program_id(1)

        @pl.when(s == 0)
        def _():
            dxb[...] = dxo_ref[...].astype(BF16)
            dh[...] = jnp.zeros_like(dh)

        @pl.when(jnp.logical_and(s == 0, t == 0))
        def _():
            dn_ref[...] = jnp.zeros_like(dn_ref)

        sub_tiles = _sub_tiles(tm)
        dacts = [_dot_nt(dxb[rows, :], wd_ref[...]) for rows in sub_tiles]
        for rows, dact in zip(sub_tiles, dacts):
            gv = gate_ref[rows, :].astype(F32)
            uv = up_ref[rows, :].astype(F32)
            sg = _sigmoid(gv)
            dgv = (dact * uv * (sg * (1.0 + gv * (1.0 - sg)))).astype(BF16)
            duv = (dact * (gv * sg)).astype(BF16)
            dg_ref[rows, :] = dgv
            du_ref[rows, :] = duv
            dh[rows, :] += _dot(dgv, wg_ref[...]) + _dot(duv, wu_ref[...])

        @pl.when(s == N_SHARDS - 1)
        def _():
            dx, dn = _rms_bwd(dh[...], x_ref[...], g_ref[...])
            dx_ref[...] = dxo_ref[...] + dx
            dn_ref[...] += dn

    row_spec = pl.BlockSpec((tm, D_MODEL), lambda t, s: (t, 0))
    vec_spec = pl.BlockSpec((1, D_MODEL), lambda t, s: (0, 0))
    row_w = pl.BlockSpec((None, FF_SHARD, D_MODEL), lambda t, s: (s, 0, 0))
    act_spec = pl.BlockSpec((None, tm, FF_SHARD), lambda t, s: (s, t, 0))
    act_shape = _sds((N_SHARDS, t_len, FF_SHARD), BF16)
    return _pcall(
        body, name=name, grid=(t_len // tm, N_SHARDS),
        in_specs=[row_spec, row_spec, vec_spec, act_spec, act_spec, row_w, row_w, row_w],
        out_specs=[act_spec, act_spec, row_spec, vec_spec],
        out_shape=[act_shape, act_shape, _sds((t_len, D_MODEL), F32), _sds((1, D_MODEL), F32)],
        scratch_shapes=[pltpu.VMEM((tm, D_MODEL), BF16), pltpu.VMEM((tm, D_MODEL), F32)],
        semantics=("arbitrary", "arbitrary"), duty=duty,
    )(dxo, x, g_row, gate, up, w_gate_t, w_up_t, w_down)


def tn_matmul(name, a, b, a_spec, b_spec, out_shape, out_spec, grid, duty=None):
    def body(a_ref, b_ref, o_ref):
        @pl.when(pl.program_id(len(grid) - 1) == 0)
        def _():
            o_ref[...] = jnp.zeros_like(o_ref)

        res = _dot_tn(a_ref[...].astype(BF16), b_ref[...].astype(BF16))
        o_ref[...] += res.reshape(o_ref.shape)

    return _pcall(body, name=name, grid=grid, in_specs=[a_spec, b_spec], out_specs=out_spec,
                  out_shape=out_shape, semantics=("arbitrary",) * len(grid), duty=duty)(a, b)


def wgrad_full(name, a, b, duty=None):
    t_len, k = a.shape
    n = b.shape[1]
    tt = _wgrad_rows(t_len)
    return tn_matmul(name, a, b,
                     pl.BlockSpec((tt, k), lambda t: (t, 0)), pl.BlockSpec((tt, n), lambda t: (t, 0)),
                     _sds((k, n), F32), pl.BlockSpec((k, n), lambda t: (0, 0)), (t_len // tt,), duty)


def wgrad_col_sharded(name, a, b_sh, duty=None):
    t_len, k = a.shape
    n_sh, _, n = b_sh.shape
    tt = _wgrad_rows(t_len)
    return tn_matmul(name, a, b_sh,
                     pl.BlockSpec((tt, k), lambda s, t: (t, 0)), pl.BlockSpec((None, tt, n), lambda s, t: (s, t, 0)),
                     _sds((n_sh, k, n), F32), pl.BlockSpec((None, k, n), lambda s, t: (s, 0, 0)),
                     (n_sh, t_len // tt), duty)


def wgrad_row_sharded(name, a_sh, b, duty=None):
    t_len, n = b.shape
    n_sh, _, k = a_sh.shape
    tt = _wgrad_rows(t_len)

    def body(a_ref, b_ref, o_ref):
        s = pl.program_id(1)
        res = _dot_tn(a_ref[...], b_ref[...].astype(BF16))

        @pl.when(pl.program_id(0) == 0)
        def _():
            o_ref[s] = res

        @pl.when(pl.program_id(0) > 0)
        def _():
            o_ref[s] += res

    return _pcall(body, name=name, grid=(t_len // tt, n_sh),
                  in_specs=[pl.BlockSpec((None, tt, k), lambda t, s: (s, t, 0)),
                            pl.BlockSpec((tt, n), lambda t, s: (t, 0))],
                  out_specs=pl.BlockSpec((n_sh, k, n), lambda t, s: (0, 0, 0)),
                  out_shape=_sds((n_sh, k, n), F32), semantics=("arbitrary", "arbitrary"), duty=duty)(a_sh, b)


def wgrad_pool_group(name, p, dzs, duty=None):
    t_len = p.shape[0]
    tt = _wgrad_rows(t_len)
    gd = POOL_GROUP_DIM
    rows = gd // N_SHARDS
    return tn_matmul(name, p, dzs,
                     pl.BlockSpec((tt, gd), lambda g, t: (t, g)), pl.BlockSpec((tt, gd), lambda g, t: (t, g)),
                     _sds((N_SHARDS, 4, rows, gd), F32),
                     pl.BlockSpec((N_SHARDS, None, rows, gd), lambda g, t: (0, g, 0, 0)),
                     (4, t_len // tt), duty)


PAD_LANES = 384
QKV_PAD = 3 * PAD_LANES
N_SLABS = QKV_PAD // LANE
GROUP_REAL = tuple(GROUP_LANES[g + 1] - GROUP_LANES[g] for g in range(3))
Q_BLOCK = 512


def pad_groups(w, axis):
    parts = []
    for g in range(3):
        blk = lax.slice_in_dim(w, GROUP_LANES[g], GROUP_LANES[g + 1], axis=axis)
        pad = [(0, 0)] * w.ndim
        pad[axis] = (0, PAD_LANES - GROUP_REAL[g])
        parts.append(jnp.pad(blk, pad))
    return parts


def unpad_groups(parts, axis):
    return jnp.concatenate([lax.slice_in_dim(p, 0, GROUP_REAL[g], axis=axis) for g, p in enumerate(parts)],
                           axis=axis)


def pad_qkv_weight(w_qkv_sh):
    w = jnp.transpose(w_qkv_sh, (1, 0, 2)).reshape(D_MODEL, 3 * D_MODEL)
    q, k, v = (pad_groups(w[:, i * D_MODEL:(i + 1) * D_MODEL], 1) for i in range(3))
    return jnp.stack([jnp.concatenate([q[g], k[g], v[g]], axis=1) for g in range(3)])


def unpad_qkv_grad(g_pad):
    cols = [unpad_groups([g_pad[g][:, i * PAD_LANES:(i + 1) * PAD_LANES] for g in range(3)], 1) for i in range(3)]
    w = jnp.concatenate(cols, axis=1)
    return jnp.transpose(w.reshape(D_MODEL, N_SHARDS, QKV_SHARD), (1, 0, 2))


def rope_tables(t_len):
    inv_freq = 1.0 / (ROPE_THETA ** (jnp.arange(0, HEAD_DIM, 2, dtype=F32) / HEAD_DIM))
    ang = jnp.arange(t_len, dtype=F32)[:, None] * inv_freq[None, :]
    cos_h, sin_h = lax.optimization_barrier((jnp.cos(ang), jnp.sin(ang)))
    reps = (1, 2 * LANE // HEAD_DIM)
    return jnp.tile(cos_h, reps), jnp.tile(sin_h, reps)


def _rot_half(v):
    n = v.shape[1]
    lane = lax.broadcasted_iota(jnp.int32, v.shape, 1)
    return jnp.where(lane % HEAD_DIM < HEAD_DIM // 2,
                     -pltpu.roll(v, n - HEAD_DIM // 2, 1), pltpu.roll(v, HEAD_DIM // 2, 1))


def _lane_cols(j):
    return slice(j * LANE, (j + 1) * LANE)


def _to_residue_major(slab, j_src, dst_ref, j_dst, dil, rows):
    for r in range(dil):
        dst_ref[r, :, _lane_cols(j_dst)] = slab[j_src, pl.ds(r, rows // dil, stride=dil), :].astype(dst_ref.dtype)


def _to_natural(src_ref, j_src, slab, j_dst, dil, rows):
    for r in range(dil):
        slab[j_dst, pl.ds(r, rows // dil, stride=dil), :] = src_ref[r, :, _lane_cols(j_src)].astype(F32)


def qkv_fwd(x, g_row, w_pad, cos, sin):
    t_len = x.shape[0]
    tm = _tile_rows(t_len)

    def body(x_ref, g_ref, w_ref, cos_ref, sin_ref, h_ref, o1_ref, o4_ref, o16_ref, slabs):
        h = _rms_fwd(x_ref[...], g_ref[...]).astype(BF16)
        h_ref[...] = h
        accs = [_dot(h, w_ref[gi]) for gi in range(3)]
        cos_t = cos_ref[...]
        sin_t = sin_ref[...]
        for gi, (dil, o_ref) in enumerate(zip(GROUP_DIL, (o1_ref, o4_ref, o16_ref))):
            slab = slabs.at[gi]
            for j in range(N_SLABS):
                a = accs[gi][:, _lane_cols(j)]
                if j < 6:
                    a = a * cos_t + _rot_half(a) * sin_t
                if j < 3:
                    a = a * (HEAD_DIM ** -0.5)
                if dil == 1:
                    o_ref[0, :, _lane_cols(j)] = a.astype(BF16)
                else:
                    slab[j] = a
                    _to_residue_major(slab, j, o_ref, j, dil, tm)

    row_spec = pl.BlockSpec((tm, D_MODEL), lambda t: (t, 0))
    tab_spec = pl.BlockSpec((tm, LANE), lambda t: (t, 0))
    out_specs = [row_spec] + [pl.BlockSpec((d, tm // d, QKV_PAD), lambda t: (0, t, 0)) for d in GROUP_DIL]
    out_shape = [_sds((t_len, D_MODEL), BF16)] + [_sds((d, t_len // d, QKV_PAD), BF16) for d in GROUP_DIL]
    return _pcall(
        body, name="qkv_fwd", grid=(t_len // tm,),
        in_specs=[row_spec, pl.BlockSpec((1, D_MODEL), lambda t: (0, 0)),
                  pl.BlockSpec((3, D_MODEL, QKV_PAD), lambda t: (0, 0, 0)), tab_spec, tab_spec],
        out_specs=out_specs, out_shape=out_shape,
        scratch_shapes=[pltpu.VMEM((3, N_SLABS, tm, LANE), F32)],
        semantics=("arbitrary",),
    )(x, g_row, w_pad, cos, sin)


def _band_mask(n):
    qi = lax.broadcasted_iota(jnp.int32, (ATTN_W, 2 * ATTN_W), 0)
    kj = lax.broadcasted_iota(jnp.int32, (ATTN_W, 2 * ATTN_W), 1)
    dist = ATTN_W + qi - kj
    return (dist >= 0) & (dist <= ATTN_W) & ((kj >= ATTN_W) | (n > 0))


def _half_masks():
    lane = lax.broadcasted_iota(jnp.int32, (1, LANE), 1)
    return [lane < HEAD_DIM, lane >= HEAD_DIM]


def _live_halves(gi, j):
    hms = _half_masks()
    return hms if (gi == 0 or j < 2) else hms[:1]


def attn_fwd(qkv_g, gi, name):
    dil, l_len, _ = qkv_g.shape
    qb = min(Q_BLOCK, l_len)
    nsub = qb // ATTN_W

    def body(q_ref, kc_ref, kp_ref, vc_ref, vp_ref, o_ref, lse_ref, kbuf, vbuf):
        n = pl.program_id(1)
        kbuf[pl.ds(0, ATTN_W), :] = kp_ref[...]
        kbuf[pl.ds(ATTN_W, qb), :] = kc_ref[...]
        vbuf[pl.ds(0, ATTN_W), :] = vp_ref[...]
        vbuf[pl.ds(ATTN_W, qb), :] = vc_ref[...]

        def sub(b, carry):
            r0 = pl.multiple_of(b * ATTN_W, ATTN_W)
            mask = _band_mask(n + b)
            krows = pl.ds(r0, 2 * ATTN_W)
            scores = []
            for j in range(3):
                q = q_ref[pl.ds(r0, ATTN_W), _lane_cols(j)]
                for hm in _live_halves(gi, j):
                    scores.append(_dot_nt(jnp.where(hm, q, jnp.zeros_like(q)), kbuf[krows, _lane_cols(j)]))
            scores = iter(scores)
            head_lane = lax.broadcasted_iota(jnp.int32, (1, LANE), 1)
            lse = jnp.zeros((ATTN_W, LANE), F32)
            for j in range(3):
                cols = _lane_cols(j)
                v = vbuf[krows, cols]
                o = jnp.zeros((ATTN_W, LANE), F32)
                for half, hm in enumerate(_live_halves(gi, j)):
                    s = jnp.where(mask, next(scores), NEG_INF)
                    m = jnp.max(s, axis=-1, keepdims=True)
                    e = jnp.exp(s - m)
                    den = jnp.sum(e, axis=-1, keepdims=True)
                    p = (e * (1.0 / den)).astype(BF16)
                    o = jnp.where(hm, _dot(p, v), o)
                    lse = jnp.where(head_lane == 2 * j + half, m + jnp.log(den), lse)
                o_ref[pl.ds(r0, ATTN_W), cols] = o.astype(BF16)
            lse_ref[pl.ds(r0, ATTN_W), :] = lse
            return carry

        lax.fori_loop(0, nsub, sub, 0)

    cur = lambda c: pl.BlockSpec((None, qb, PAD_LANES), lambda r, n: (r, n, c))
    prev = lambda c: pl.BlockSpec((None, ATTN_W, PAD_LANES), lambda r, n: (r, jnp.maximum(n * nsub - 1, 0), c))
    return _pcall(
        body, name=name, grid=(dil, l_len // qb),
        in_specs=[cur(0), cur(1), prev(1), cur(2), prev(2)],
        out_specs=[pl.BlockSpec((None, qb, PAD_LANES), lambda r, n: (r, n, 0)),
                   pl.BlockSpec((None, qb, LANE), lambda r, n: (r, n, 0))],
        out_shape=[_sds((dil, l_len, PAD_LANES), BF16), _sds((dil, l_len, LANE), F32)],
        scratch_shapes=[pltpu.VMEM((qb + ATTN_W, PAD_LANES), BF16), pltpu.VMEM((qb + ATTN_W, PAD_LANES), BF16)],
        semantics=("arbitrary", "arbitrary"),
    )(qkv_g, qkv_g, qkv_g, qkv_g, qkv_g)


def _group_stats(lses):
    head_lane = lax.broadcasted_iota(jnp.int32, (1, LANE), 1)
    fulls, glse = [], []
    for g in range(3):
        real = head_lane < GROUP_HEADS[g]
        mx = jnp.max(jnp.where(real, lses[g], -jnp.inf), axis=-1, keepdims=True)
        sm = jnp.sum(jnp.where(real, jnp.exp(lses[g] - mx), 0.0), axis=-1, keepdims=True)
        fulls.append(mx + jnp.log(sm))
        glse.append(fulls[g] - math.log(GROUP_HEADS[g]))
    top = jnp.maximum(jnp.maximum(glse[0], glse[1]), glse[2])
    ex = [jnp.exp(v - top) for v in glse]
    tot = ex[0] + ex[1] + ex[2]
    alpha = [v / tot for v in ex]
    lane = lax.broadcasted_iota(jnp.int32, (1, QKV_PAD), 1)
    scale = jnp.where(lane < PAD_LANES, 3.0 * alpha[0],
                      jnp.where(lane < 2 * PAD_LANES, 3.0 * alpha[1], 3.0 * alpha[2]))
    return alpha, fulls, scale


def attn_out_fwd(x, o_parts, lse_parts, w_out_pad):
    t_len = x.shape[0]
    tm = _tile_rows(t_len)

    def body(x_ref, o1, o4, o16, l1, l4, l16, w_ref, xo_ref, mg_ref, o_ref, lse_ref, o_slab, l_slab):
        for gi, (dil, og, lg) in enumerate(zip(GROUP_DIL, (o1, o4, o16), (l1, l4, l16))):
            for j in range(3):
                _to_natural(og, j, o_slab, 3 * gi + j, dil, tm)
            _to_natural(lg, 0, l_slab, gi, dil, tm)
        o = jnp.concatenate([o_slab[j] for j in range(N_SLABS)], axis=1)
        lses = [l_slab[gi] for gi in range(3)]
        o_ref[...] = o.astype(BF16)
        for gi in range(3):
            lse_ref[:, _lane_cols(gi)] = lses[gi]
        _, _, scale = _group_stats(lses)
        merged = (o * scale).astype(BF16)
        mg_ref[...] = merged
        xo_ref[...] = x_ref[...] + _dot(merged, w_ref[...])

    row_spec = pl.BlockSpec((tm, D_MODEL), lambda t: (t, 0))
    pad_spec = pl.BlockSpec((tm, QKV_PAD), lambda t: (t, 0))
    o_specs = [pl.BlockSpec((d, tm // d, PAD_LANES), lambda t: (0, t, 0)) for d in GROUP_DIL]
    lse_specs = [pl.BlockSpec((d, tm // d, LANE), lambda t: (0, t, 0)) for d in GROUP_DIL]
    return _pcall(
        body, name="attn_out_fwd", grid=(t_len // tm,),
        in_specs=[row_spec] + o_specs + lse_specs + [pl.BlockSpec((QKV_PAD, D_MODEL), lambda t: (0, 0))],
        out_specs=[row_spec, pad_spec, pad_spec, pl.BlockSpec((tm, 3 * LANE), lambda t: (t, 0))],
        out_shape=[_sds((t_len, D_MODEL), F32), _sds((t_len, QKV_PAD), BF16),
                   _sds((t_len, QKV_PAD), BF16), _sds((t_len, 3 * LANE), F32)],
        scratch_shapes=[pltpu.VMEM((N_SLABS, tm, LANE), F32), pltpu.VMEM((3, tm, LANE), F32)],
        semantics=("arbitrary",),
    )(x, *o_parts, *lse_parts, w_out_pad)


def attn_out_bwd(dxo, w_out_pad, o, lse, duty=None):
    t_len = dxo.shape[0]
    tm = _tile_rows(t_len)

    def body(dx_ref, w_ref, o_ref, lse_ref, d1, d4, d16, c1, c4, c16, slab):
        dmerged = _dot_nt(dx_ref[...].astype(BF16), w_ref[...])
        o_t = o_ref[...].astype(F32)
        lses = [lse_ref[:, _lane_cols(gi)] for gi in range(3)]
        alpha, fulls, scale = _group_stats(lses)
        e = dmerged * o_t
        lane = lax.broadcasted_iota(jnp.int32, (1, QKV_PAD), 1)
        dalpha = [3.0 * jnp.sum(jnp.where((lane >= g * PAD_LANES) & (lane < g * PAD_LANES + GROUP_REAL[g]), e, 0.0),
                                axis=-1, keepdims=True) for g in range(3)]
        mean_da = alpha[0] * dalpha[0] + alpha[1] * dalpha[1] + alpha[2] * dalpha[2]
        dglse = [alpha[g] * (dalpha[g] - mean_da) for g in range(3)]
        do = dmerged * scale
        es = e * scale
        for j in range(N_SLABS):
            slab[j] = do[:, _lane_cols(j)]
        for gi, (dil, dg) in enumerate(zip(GROUP_DIL, (d1, d4, d16))):
            for j in range(3):
                _to_residue_major(slab, 3 * gi + j, dg, j, dil, tm)
        head_lane = lax.broadcasted_iota(jnp.int32, (1, LANE), 1)
        first = head_lane < HEAD_DIM
        for gi, (dil, cg) in enumerate(zip(GROUP_DIL, (c1, c4, c16))):
            c_g = -(dglse[gi] * jnp.exp(lses[gi] - fulls[gi]))
            for j in range(3):
                blk = es[:, _lane_cols(3 * gi + j)]
                halves = (jnp.sum(jnp.where(first, blk, 0.0), axis=-1, keepdims=True),
                          jnp.sum(jnp.where(first, 0.0, blk), axis=-1, keepdims=True))
                for half in range(2):
                    c_g = c_g + jnp.where(head_lane == 2 * j + half, halves[half], 0.0)
            slab[gi] = c_g
            _to_residue_major(slab, gi, cg, 0, dil, tm)

    row_spec = pl.BlockSpec((tm, D_MODEL), lambda t: (t, 0))
    pad_spec = pl.BlockSpec((tm, QKV_PAD), lambda t: (t, 0))
    do_specs = [pl.BlockSpec((d, tm // d, PAD_LANES), lambda t: (0, t, 0)) for d in GROUP_DIL]
    c_specs = [pl.BlockSpec((d, tm // d, LANE), lambda t: (0, t, 0)) for d in GROUP_DIL]
    outs = _pcall(
        body, name="attn_out_bwd", grid=(t_len // tm,),
        in_specs=[row_spec, pl.BlockSpec((QKV_PAD, D_MODEL), lambda t: (0, 0)), pad_spec,
                  pl.BlockSpec((tm, 3 * LANE), lambda t: (t, 0))],
        out_specs=do_specs + c_specs,
        out_shape=[_sds((d, t_len // d, PAD_LANES), BF16) for d in GROUP_DIL]
                  + [_sds((d, t_len // d, LANE), F32) for d in GROUP_DIL],
        scratch_shapes=[pltpu.VMEM((N_SLABS, tm, LANE), F32)],
        semantics=("arbitrary",), duty=duty,
    )(dxo, w_out_pad, o, lse)
    if duty is None:
        return outs[:3], outs[3:]
    return (outs[0][:3], outs[0][3:]), outs[1]


def attn_bwd(qkv_g, do_g, lse_g, c_g, gi, name, duty=None):
    dil, l_len, _ = qkv_g.shape
    qb = min(Q_BLOCK, l_len)
    nsub = qb // ATTN_W
    nsb = l_len // qb

    def body(q_ref, kc_ref, kp_ref, vc_ref, vp_ref, do_ref, lse_ref, c_ref,
             qn_ref, don_ref, lsen_ref, cn_ref, o_ref, kbuf, vbuf, dkbuf, dvbuf):
        n = pl.program_id(1)
        kbuf[pl.ds(0, ATTN_W), :] = kp_ref[...]
        kbuf[pl.ds(ATTN_W, qb), :] = kc_ref[...]
        vbuf[pl.ds(0, ATTN_W), :] = vp_ref[...]
        vbuf[pl.ds(ATTN_W, qb), :] = vc_ref[...]
        dkbuf[...] = jnp.zeros_like(dkbuf)
        dvbuf[...] = jnp.zeros_like(dvbuf)

        def block(q_of, do_of, lse_of, c_of, krows, mask, dq_rows):
            heads = []
            for j in range(3):
                cols = _lane_cols(j)
                q, do_t, k, v = q_of(cols), do_of(cols), kbuf[krows, cols], vbuf[krows, cols]
                for half, hm in enumerate(_live_halves(gi, j)):
                    qh = jnp.where(hm, q, jnp.zeros_like(q))
                    doh = jnp.where(hm, do_t, jnp.zeros_like(do_t))
                    heads.append((j, 2 * j + half, hm, qh, doh, _dot_nt(qh, k), _dot_nt(doh, v)))
            head_lane = lax.broadcasted_iota(jnp.int32, (1, LANE), 1)
            lse_t, c_t = lse_of(), c_of()
            for j in range(3):
                cols = _lane_cols(j)
                k = kbuf[krows, cols]
                dq = jnp.zeros((ATTN_W, LANE), F32)
                dk = jnp.zeros((k.shape[0], LANE), F32)
                dv = jnp.zeros((k.shape[0], LANE), F32)
                for hj, head, hm, qh, doh, s, dp in heads:
                    if hj != j:
                        continue
                    lse_h = jnp.max(jnp.where(head_lane == head, lse_t, -jnp.inf), axis=-1, keepdims=True)
                    c_h = jnp.max(jnp.where(head_lane == head, c_t, -jnp.inf), axis=-1, keepdims=True)
                    p = jnp.exp(jnp.where(mask, s, NEG_INF) - lse_h)
                    ds = (p * (dp - c_h)).astype(BF16)
                    if dq_rows is not None:
                        dq = jnp.where(hm, _dot(ds, k), dq)
                    dk = dk + _dot_tn(ds, qh)
                    dv = dv + _dot_tn(p.astype(BF16), doh)
                if dq_rows is not None:
                    o_ref[dq_rows, cols] = dq.astype(BF16)
                dkbuf[krows, cols] += dk
                dvbuf[krows, cols] += dv

        def sub(b, carry):
            rows = pl.ds(pl.multiple_of(b * ATTN_W, ATTN_W), ATTN_W)
            krows = pl.ds(pl.multiple_of(b * ATTN_W, ATTN_W), 2 * ATTN_W)
            block(lambda c: q_ref[rows, c], lambda c: do_ref[rows, c], lambda: lse_ref[rows, :],
                  lambda: c_ref[rows, :], krows, _band_mask(n + b), rows)
            return carry

        lax.fori_loop(0, nsub, sub, 0)

        qi = lax.broadcasted_iota(jnp.int32, (ATTN_W, ATTN_W), 0)
        kj = lax.broadcasted_iota(jnp.int32, (ATTN_W, ATTN_W), 1)
        nmask = (qi <= kj) & (n < nsb - 1)
        block(lambda c: qn_ref[:, c], lambda c: don_ref[:, c], lambda: lsen_ref[...],
              lambda: cn_ref[...], pl.ds(qb, ATTN_W), nmask, None)
        o_ref[:, pl.ds(PAD_LANES, PAD_LANES)] = dkbuf[pl.ds(ATTN_W, qb), :].astype(BF16)
        o_ref[:, pl.ds(2 * PAD_LANES, PAD_LANES)] = dvbuf[pl.ds(ATTN_W, qb), :].astype(BF16)

    cur = lambda c: pl.BlockSpec((None, qb, PAD_LANES), lambda r, n: (r, n, c))
    prev = lambda c: pl.BlockSpec((None, ATTN_W, PAD_LANES), lambda r, n: (r, jnp.maximum(n * nsub - 1, 0), c))
    nxt_row = lambda r, n: (r, jnp.minimum((n + 1) * nsub, nsb * nsub - 1), 0)
    nxt = pl.BlockSpec((None, ATTN_W, PAD_LANES), nxt_row)
    head_cur = pl.BlockSpec((None, qb, LANE), lambda r, n: (r, n, 0))
    head_nxt = pl.BlockSpec((None, ATTN_W, LANE), nxt_row)
    return _pcall(
        body, name=name, grid=(dil, nsb),
        in_specs=[cur(0), cur(1), prev(1), cur(2), prev(2), cur(0), head_cur, head_cur, nxt, nxt, head_nxt, head_nxt],
        out_specs=pl.BlockSpec((None, qb, QKV_PAD), lambda r, n: (r, n, 0)),
        out_shape=_sds((dil, l_len, QKV_PAD), BF16),
        scratch_shapes=[pltpu.VMEM((qb + ATTN_W, PAD_LANES), BF16), pltpu.VMEM((qb + ATTN_W, PAD_LANES), BF16),
                        pltpu.VMEM((qb + ATTN_W, PAD_LANES), F32), pltpu.VMEM((qb + ATTN_W, PAD_LANES), F32)],
        semantics=("arbitrary", "arbitrary"), duty=duty,
    )(qkv_g, qkv_g, qkv_g, qkv_g, qkv_g, do_g, lse_g, c_g, qkv_g, do_g, lse_g, c_g)


def qkv_bwd(dqkv_parts, w_pad, dxo, x, g_row, cos, sin):
    t_len = x.shape[0]
    tm = _tile_rows(t_len)

    def body(p1, p4, p16, w_ref, dxo_ref, x_ref, g_ref, cos_ref, sin_ref, dq_ref, dx_ref, dn_ref, slabs):
        @pl.when(pl.program_id(0) == 0)
        def _():
            dn_ref[...] = jnp.zeros_like(dn_ref)

        cos_t = cos_ref[...]
        sin_t = sin_ref[...]
        dh = None
        for gi, (dil, part) in enumerate(zip(GROUP_DIL, (p1, p4, p16))):
            slab = slabs.at[gi]
            for j in range(N_SLABS):
                if dil == 1:
                    a = part[0, :, _lane_cols(j)].astype(F32)
                else:
                    _to_natural(part, j, slab, j, dil, tm)
                    a = slab[j]
                if j < 6:
                    a = a * cos_t - _rot_half(a * sin_t)
                if j < 3:
                    a = a * (HEAD_DIM ** -0.5)
                dq_ref[gi, :, _lane_cols(j)] = a.astype(BF16)
            contrib = _dot_nt(dq_ref[gi], w_ref[gi])
            dh = contrib if dh is None else dh + contrib
        dx, dn = _rms_bwd(dh, x_ref[...], g_ref[...])
        dx_ref[...] = dxo_ref[...] + dx
        dn_ref[...] += dn

    row_spec = pl.BlockSpec((tm, D_MODEL), lambda t: (t, 0))
    vec_spec = pl.BlockSpec((1, D_MODEL), lambda t: (0, 0))
    tab_spec = pl.BlockSpec((tm, LANE), lambda t: (t, 0))
    part_specs = [pl.BlockSpec((d, tm // d, QKV_PAD), lambda t: (0, t, 0)) for d in GROUP_DIL]
    return _pcall(
        body, name="qkv_bwd", grid=(t_len // tm,),
        in_specs=part_specs + [pl.BlockSpec((3, D_MODEL, QKV_PAD), lambda t: (0, 0, 0)),
                               row_spec, row_spec, vec_spec, tab_spec, tab_spec],
        out_specs=[pl.BlockSpec((3, tm, QKV_PAD), lambda t: (0, t, 0)), row_spec, vec_spec],
        out_shape=[_sds((3, t_len, QKV_PAD), BF16), _sds((t_len, D_MODEL), F32), _sds((1, D_MODEL), F32)],
        scratch_shapes=[pltpu.VMEM((3, N_SLABS, tm, LANE), F32)],
        semantics=("arbitrary",),
    )(*dqkv_parts, w_pad, dxo, x, g_row, cos, sin)


def pool_bwd(dxo, x, g_row, w_in, w_grp, scale, w_out, zr, duty=None):
    t_len = x.shape[0]
    tm = _tile_rows(t_len)
    nt = t_len // tm

    def body(dxo_ref, x_ref, g_ref, win_ref, wgrp_ref, scale_ref, wout_ref, zr_ref,
             dzs_ref, du_ref, dx_ref, dn_ref, dsc_ref, ebuf):
        i = pl.program_id(0)
        t = nt - 1 - i

        @pl.when(i == 0)
        def _():
            ebuf[pl.ds(tm, POOL_HALO), :] = jnp.zeros((POOL_HALO, D_MODEL), F32)
            dn_ref[...] = jnp.zeros_like(dn_ref)
            dsc_ref[...] = jnp.zeros_like(dsc_ref)

        dxo_t = dxo_ref[...]
        dz = _dot_nt(dxo_t.astype(BF16), wout_ref[...])
        dsc_ref[...] += jnp.sum(dz * zr_ref[...].astype(F32), axis=0, keepdims=True)
        dzs_ref[...] = (dz * scale_ref[...]).astype(BF16)
        row = t * tm + lax.broadcasted_iota(jnp.int32, (tm, 1), 0)
        for gi, w in enumerate(POOL_WINDOWS):
            cols = pl.ds(gi * POOL_GROUP_DIM, POOL_GROUP_DIM)
            dp_g = _dot_nt(dzs_ref[:, cols], wgrp_ref[gi])
            inv_cnt = 1.0 / jnp.minimum(row + 1, w).astype(F32)
            ebuf[pl.ds(0, tm), cols] = dp_g * inv_cnt
            acc = -dp_g
            for j in range(w):
                acc = acc + ebuf[pl.ds(j, tm), cols]
            du_ref[:, cols] = acc.astype(BF16)
        ebuf[pl.ds(tm, POOL_HALO), :] = ebuf[pl.ds(0, POOL_HALO), :]
        dh = _dot_nt(du_ref[...], win_ref[...])
        dx, dn = _rms_bwd(dh, x_ref[...], g_ref[...])
        dx_ref[...] = dxo_t + dx
        dn_ref[...] += dn

    row_spec = pl.BlockSpec((tm, D_MODEL), lambda i: (nt - 1 - i, 0))
    full = lambda shape: pl.BlockSpec(shape, lambda i: (0,) * len(shape))
    vec = full((1, D_MODEL))
    return _pcall(
        body, name="pool_bwd", grid=(nt,),
        in_specs=[row_spec, row_spec, vec, full((D_MODEL, D_MODEL)), full((4, POOL_GROUP_DIM, POOL_GROUP_DIM)),
                  vec, full((D_MODEL, D_MODEL)), row_spec],
        out_specs=[row_spec, row_spec, row_spec, vec, vec],
        out_shape=[_sds((t_len, D_MODEL), BF16), _sds((t_len, D_MODEL), BF16), _sds((t_len, D_MODEL), F32),
                   _sds((1, D_MODEL), F32), _sds((1, D_MODEL), F32)],
        scratch_shapes=[pltpu.VMEM((tm + POOL_HALO, D_MODEL), F32)],
        semantics=("arbitrary",), duty=duty,
    )(dxo, x, g_row, w_in, w_grp, scale, w_out, zr)


def _mesh_pos():
    return lax.axis_index("x"), lax.axis_index("y"), lax.axis_index("c")


def _other_chips(x, y):
    return [(1 - x, y), (x, 1 - y), (1 - x, 1 - y)]


def _remote(src, dst, send_sem, recv_sem, device):
    return pltpu.make_async_remote_copy(src_ref=src, dst_ref=dst, send_sem=send_sem, recv_sem=recv_sem,
                                        device_id=device, device_id_type=MESH)


class _Duty:
    aliases = {}

    def mid(self, ins, outs, sems):
        pass


class Together(_Duty):
    def __init__(self, duties):
        self.duties = duties
        self.ins = [a for d in duties for a in d.ins]
        self.out_shape = [s for d in duties for s in d.out_shape]
        self.scratch = [s for d in duties for s in d.scratch]
        self.aliases = {}
        i0 = o0 = 0
        for d in duties:
            self.aliases.update({i0 + i: o0 + o for i, o in d.aliases.items()})
            i0 += len(d.ins)
            o0 += len(d.out_shape)

    def _each(self, ins, outs, sems):
        i0 = o0 = s0 = 0
        for d in self.duties:
            ni, no, ns = len(d.ins), len(d.out_shape), len(d.scratch)
            yield d, ins[i0:i0 + ni], outs[o0:o0 + no], sems[s0:s0 + ns]
            i0, o0, s0 = i0 + ni, o0 + no, s0 + ns

    def split(self, outs):
        return [list(o) for _, _, o, _ in self._each(self.ins, outs, self.scratch)]

    def start(self, ins, outs, sems):
        for d, i, o, s in self._each(ins, outs, sems):
            d.start(i, o, s)

    def mid(self, ins, outs, sems):
        for d, i, o, s in self._each(ins, outs, sems):
            d.mid(i, o, s)

    def finish(self, ins, outs, sems):
        for d, i, o, s in self._each(ins, outs, sems):
            d.finish(i, o, s)


def run_duty(duty, name):
    d_in, d_out = len(duty.ins), len(duty.out_shape)

    def body(*refs):
        ins, outs, sems = refs[:d_in], refs[d_in:d_in + d_out], refs[d_in + d_out:]
        duty.start(ins, outs, sems)
        duty.mid(ins, outs, sems)
        duty.finish(ins, outs, sems)

    return pl.pallas_call(
        body, name=name, out_shape=list(duty.out_shape), in_specs=[_ANY] * d_in, out_specs=[_ANY] * d_out,
        scratch_shapes=list(duty.scratch), input_output_aliases=dict(duty.aliases),
        compiler_params=pltpu.CompilerParams(has_side_effects=True),
    )(*duty.ins)


class GatherWeights(_Duty):
    def __init__(self, shards):
        n = self.n = len(shards)
        self.halves = [s.shape[0] // 2 for s in shards]
        my_slot = 2 * lax.axis_index("x") + lax.axis_index("y")
        staged = [lax.dynamic_update_slice(lax.empty((N_SHARDS,) + s.shape, s.dtype), s[None], (my_slot, 0, 0))
                  for s in shards]
        self.ins = list(shards) + staged
        self.out_shape = [_sds((N_SHARDS,) + s.shape, s.dtype) for s in shards]
        self.aliases = {n + a: a for a in range(n)}
        self.scratch = [pltpu.SemaphoreType.DMA((n, 6)), pltpu.SemaphoreType.DMA((n, 6))]

    def _over_ici(self, ins, outs, sems):
        x, y, c = _mesh_pos()
        return [_remote(ins[a].at[pl.ds(c * h, h)], outs[a].at[2 * x + y, pl.ds(c * h, h)],
                        sems[0].at[a, j], sems[1].at[a, j], (*chip, c))
                for a, h in enumerate(self.halves) for j, chip in enumerate(_other_chips(x, y))]

    def _forwards(self, outs, sems, half_of):
        x, y, c = _mesh_pos()
        cps = []
        for a, h in enumerate(self.halves):
            for j, chip in enumerate(_other_chips(x, y)):
                slot = outs[a].at[2 * chip[0] + chip[1], pl.ds(half_of(c) * h, h)]
                cps.append(_remote(slot, slot, sems[0].at[a, 3 + j], sems[1].at[a, 3 + j], (x, y, 1 - c)))
        return cps

    def start(self, ins, outs, sems):
        for cp in self._over_ici(ins, outs, sems):
            cp.start()

    def mid(self, ins, outs, sems):
        x, y, c = _mesh_pos()
        forwards = self._forwards(outs, sems, lambda core: core)
        k = 0
        for a, h in enumerate(self.halves):
            for j, chip in enumerate(_other_chips(x, y)):
                slot = outs[a].at[2 * chip[0] + chip[1], pl.ds(c * h, h)]
                _remote(slot, slot, sems[0].at[a, j], sems[1].at[a, j], (*chip, c)).wait_recv()
                forwards[k].start()
                k += 1

    def finish(self, ins, outs, sems):
        for cp in self._forwards(outs, sems, lambda core: 1 - core):
            cp.wait_recv()
        for cp in self._over_ici(ins, outs, sems) + self._forwards(outs, sems, lambda core: core):
            cp.wait_send()


class GradReducer:
    def __init__(self, c_idx, pos_idx):
        self.c_idx, self.pos_idx = c_idx, pos_idx
        self.in_flight = []
        self.done = {}

    def push(self, name, grad):
        self.in_flight.append(dict(name=name, stage="halves", data=grad))

    def _duties(self):
        make = {"halves": SiblingHalves, "exchange": ChipExchange, "share": SiblingShare}
        return Together([make[w["stage"]]([w["data"]]) for w in self.in_flight])

    def _advance(self, duties, outs):
        still = []
        for w, (res,) in zip(self.in_flight, duties.split(outs)):
            if w["stage"] == "halves":
                partial = add_my_half(w["data"], res, self.c_idx, f"rs_add_{w['name']}")
                still.append(dict(name=w["name"], stage="exchange", data=partial))
            elif w["stage"] == "exchange":
                reduced = sum_slots(res, w["data"], self.pos_idx, f"rs_sum_{w['name']}")
                still.append(dict(name=w["name"], stage="share", data=reduced))
            else:
                self.done[w["name"]] = res
        self.in_flight = still

    def carried_by(self, fn, *args, **kw):
        if not self.in_flight:
            return fn(*args, **kw)
        duties = self._duties()
        out, duty_outs = fn(*args, duty=duties, **kw)
        self._advance(duties, duty_outs)
        return out

    def drain(self, name):
        step = 0
        while self.in_flight:
            duties = self._duties()
            self._advance(duties, run_duty(duties, f"{name}{step}"))
            step += 1


class SiblingHalves(_Duty):
    def __init__(self, grads):
        n = len(grads)
        self.halves = [g.shape[1] // 2 for g in grads]
        self.ins = list(grads)
        self.out_shape = [_sds((N_SHARDS, h, g.shape[2]), g.dtype) for g, h in zip(grads, self.halves)]
        self.scratch = [pltpu.SemaphoreType.DMA((n,)), pltpu.SemaphoreType.DMA((n,))]

    def _copies(self, ins, outs, sems):
        x, y, c = _mesh_pos()
        return [_remote(ins[a].at[:, pl.ds((1 - c) * h, h)], outs[a], sems[0].at[a], sems[1].at[a], (x, y, 1 - c))
                for a, h in enumerate(self.halves)]

    def start(self, ins, outs, sems):
        for cp in self._copies(ins, outs, sems):
            cp.start()

    def finish(self, ins, outs, sems):
        for cp in self._copies(ins, outs, sems):
            cp.wait()


class ChipExchange(_Duty):
    def __init__(self, parts):
        n = self.n = len(parts)
        self.ins = list(parts)
        self.out_shape = [_sds(p.shape, p.dtype) for p in parts]
        self.scratch = [pltpu.SemaphoreType.DMA((n, 3)), pltpu.SemaphoreType.DMA((n, 3))]

    def _copies(self, ins, outs, sems, arriving):
        x, y, c = _mesh_pos()
        cps = []
        for a in range(self.n):
            for j, chip in enumerate(_other_chips(x, y)):
                theirs = 2 * chip[0] + chip[1]
                src = outs[a].at[theirs] if arriving else ins[a].at[theirs]
                dst = outs[a].at[theirs] if arriving else outs[a].at[2 * x + y]
                cps.append(_remote(src, dst, sems[0].at[a, j], sems[1].at[a, j], (*chip, c)))
        return cps

    def start(self, ins, outs, sems):
        for cp in self._copies(ins, outs, sems, False):
            cp.start()

    def finish(self, ins, outs, sems):
        for cp in self._copies(ins, outs, sems, True):
            cp.wait_recv()
        for cp in self._copies(ins, outs, sems, False):
            cp.wait_send()


class SiblingShare(_Duty):
    def __init__(self, reduced):
        n = self.n = len(reduced)
        self.ins = list(reduced)
        self.out_shape = [_sds(r.shape, r.dtype) for r in reduced]
        self.aliases = {a: a for a in range(n)}
        self.scratch = [pltpu.SemaphoreType.DMA((n,)), pltpu.SemaphoreType.DMA((n,))]

    def _copies(self, outs, sems, half_of):
        x, y, c = _mesh_pos()
        cps = []
        for a in range(self.n):
            h = outs[a].shape[0] // 2
            rows = outs[a].at[pl.ds(half_of(c) * h, h)]
            cps.append(_remote(rows, rows, sems[0].at[a], sems[1].at[a], (x, y, 1 - c)))
        return cps

    def start(self, ins, outs, sems):
        for cp in self._copies(outs, sems, lambda core: core):
            cp.start()

    def finish(self, ins, outs, sems):
        for cp in self._copies(outs, sems, lambda core: 1 - core):
            cp.wait_recv()
        for cp in self._copies(outs, sems, lambda core: core):
            cp.wait_send()


def allreduce_small(v):
    def body(v_ref, o_ref, buf, send_sems, recv_sems):
        x, y, c = _mesh_pos()
        me = 4 * x + 2 * y + c
        buf[me] = v_ref[...]
        flip = lambda p, f: 1 - p if f else p
        peers = [(flip(x, k & 4), flip(y, k & 2), flip(c, k & 1)) for k in range(1, N_DEV)]
        cps = []
        for k, peer in enumerate(peers):
            cp = _remote(v_ref, buf.at[me], send_sems.at[k], recv_sems.at[k], peer)
            cp.start()
            cps.append(cp)
        for k, peer in enumerate(peers):
            slot = buf.at[4 * peer[0] + 2 * peer[1] + peer[2]]
            _remote(slot, slot, send_sems.at[k], recv_sems.at[k], peer).wait_recv()
        for cp in cps:
            cp.wait_send()
        acc = buf[0]
        for i in range(1, N_DEV):
            acc = acc + buf[i]
        o_ref[...] = acc

    vm = pl.BlockSpec(memory_space=pltpu.VMEM)
    return pl.pallas_call(
        body, name="allreduce_small", out_shape=_sds(v.shape, v.dtype), in_specs=[vm], out_specs=vm,
        scratch_shapes=[pltpu.VMEM((N_DEV,) + v.shape, v.dtype),
                        pltpu.SemaphoreType.DMA((N_DEV - 1,)), pltpu.SemaphoreType.DMA((N_DEV - 1,))],
        compiler_params=pltpu.CompilerParams(has_side_effects=True),
    )(v)


def add_my_half(grad, theirs, c_idx, name):
    _, r, cols = grad.shape
    h = r // 2

    def body(c_ref, g_ref, t_ref, o_ref):
        o_ref[...] = (g_ref[...] + t_ref[...]).astype(BF16)

    slot = pl.BlockSpec((None, h, cols), lambda s, c: (s, 0, 0))
    grid_spec = pltpu.PrefetchScalarGridSpec(
        num_scalar_prefetch=1, grid=(N_SHARDS,),
        in_specs=[pl.BlockSpec((None, h, cols), lambda s, c: (s, c[0], 0)), slot], out_specs=slot)
    return pl.pallas_call(
        body, name=name, grid_spec=grid_spec, out_shape=_sds((N_SHARDS, h, cols), BF16),
        compiler_params=pltpu.CompilerParams(dimension_semantics=("arbitrary",), vmem_limit_bytes=VMEM_LIMIT_BYTES),
    )(c_idx, grad, theirs)


def sum_slots(received, mine, pos_idx, name):
    _, h, cols = received.shape

    def body(pos_ref, r_ref, m_ref, o_ref):
        acc = None
        for k in range(N_SHARDS):
            term = jnp.where(pos_ref[0] == k, m_ref[k], r_ref[k]).astype(F32)
            acc = term if acc is None else acc + term
        o_ref[...] = acc

    whole = pl.BlockSpec((N_SHARDS, h, cols), lambda i, pos: (0, 0, 0))
    grid_spec = pltpu.PrefetchScalarGridSpec(
        num_scalar_prefetch=1, grid=(1,), in_specs=[whole, whole],
        out_specs=pl.BlockSpec((h, cols), lambda i, pos: (pos[1], 0)))
    return pl.pallas_call(
        body, name=name, grid_spec=grid_spec, out_shape=_sds((2 * h, cols), F32),
        compiler_params=pltpu.CompilerParams(dimension_semantics=("arbitrary",), vmem_limit_bytes=VMEM_LIMIT_BYTES),
    )(pos_idx, received, mine)


def adamw(name, grads, w, m, v):
    n_layers, r, cols = w.shape
    tr = r // 2 if r % 16 == 0 else r
    bias1 = 1.0 - ADAM_B1 ** ADAM_STEP
    bias2 = 1.0 - ADAM_B2 ** ADAM_STEP

    def body(*refs):
        g_refs = refs[:n_layers]
        w_ref, m_ref, v_ref, go_ref, d_ref, mo_ref, vo_ref = refs[n_layers:]
        g = g_refs[0][...]
        for layer in range(1, n_layers):
            g = jnp.where(pl.program_id(0) == layer, g_refs[layer][...], g)
        m_new = ADAM_B1 * m_ref[...] + (1.0 - ADAM_B1) * g
        v_new = ADAM_B2 * v_ref[...] + (1.0 - ADAM_B2) * (g * g)
        m_hat = m_new / bias1
        v_hat = v_new / bias2
        go_ref[...] = g
        d_ref[...] = -ADAM_LR * (m_hat / (jnp.sqrt(v_hat) + ADAM_EPS) + ADAM_WD * w_ref[...])
        mo_ref[...] = m_new
        vo_ref[...] = v_new

    g_spec = pl.BlockSpec((tr, cols), lambda l, i: (i, 0))
    lay_spec = pl.BlockSpec((None, tr, cols), lambda l, i: (l, i, 0))
    shape = _sds((n_layers, r, cols), F32)
    return _pcall(
        body, name=name, grid=(n_layers, r // tr),
        in_specs=[g_spec] * n_layers + [lay_spec] * 3, out_specs=[lay_spec] * 4,
        out_shape=[shape] * 4, semantics=("arbitrary", "arbitrary"),
    )(*grads, w, m, v)


def kernel(x, norm_mix, norm_ffn, norm_final, pool_w_in, pool_w_group, pool_scale, pool_w_out, attn_w_qkv, attn_w_out, ffn_w_gate, ffn_w_up, ffn_w_down, loss_target, m_norm_mix, m_norm_ffn, m_norm_final, m_pool_w_in, m_pool_w_group, m_pool_scale, m_pool_w_out, m_attn_w_qkv, m_attn_w_out, m_ffn_w_gate, m_ffn_w_up, m_ffn_w_down, v_norm_mix, v_norm_ffn, v_norm_final, v_pool_w_in, v_pool_w_group, v_pool_scale, v_pool_w_out, v_attn_w_qkv, v_attn_w_out, v_ffn_w_gate, v_ffn_w_up, v_ffn_w_down):
    t_len = x.shape[1]
    x0 = x.reshape(t_len, D_MODEL)
    target = loss_target.reshape(t_len, D_MODEL)
    row = lambda a: a.reshape(1, D_MODEL)

    grp_rows = POOL_GROUP_DIM // N_SHARDS
    bf = lambda a: a.astype(BF16)
    gate_t, up_t = jnp.swapaxes(ffn_w_gate, 1, 2), jnp.swapaxes(ffn_w_up, 1, 2)
    pool_shards = [bf(pool_w_in[0]), bf(pool_w_group[0].reshape(4 * grp_rows, POOL_GROUP_DIM)), bf(pool_w_out[0])]
    ffn0_shards = [bf(gate_t[0]), bf(up_t[0]), bf(ffn_w_down[0])]
    late_shards = [bf(attn_w_qkv[0]), bf(attn_w_out[0]), bf(gate_t[1]), bf(up_t[1]), bf(ffn_w_down[1])]
    cos, sin = rope_tables(t_len)
    c_idx = lax.axis_index("c").astype(jnp.int32).reshape(1)
    pos_idx = jnp.stack([2 * lax.axis_index("x") + lax.axis_index("y"), lax.axis_index("c")]).astype(jnp.int32)
    chip_rows = lambda g: g.reshape(N_SHARDS, D_MODEL // N_SHARDS, D_MODEL)

    g_pool = run_duty(GatherWeights(pool_shards), "gather_pool")
    w_in = g_pool[0].reshape(D_MODEL, D_MODEL)
    w_grp = g_pool[1].reshape(N_SHARDS, 4, grp_rows, POOL_GROUP_DIM).transpose(1, 0, 2, 3).reshape(
        4, POOL_GROUP_DIM, POOL_GROUP_DIM)
    w_out = g_pool[2].reshape(D_MODEL, D_MODEL)
    (h0, p, zr, z, x1), ffn0 = pool_fwd(x0, row(norm_mix[0]), w_in, w_grp, pool_scale, w_out,
                                        duty=GatherWeights(ffn0_shards))
    (h1, gate0, up0, act0, x2), late = ffn_fwd(x1, row(norm_ffn[0]), *ffn0, "ffn_fwd0",
                                               duty=GatherWeights(late_shards))
    w_qkv = pad_qkv_weight(late[0])
    w_ao = jnp.concatenate(pad_groups(late[1].reshape(D_MODEL, D_MODEL), 0), axis=0)
    ffn1 = late[2:5]
    h2, *qkv_parts = qkv_fwd(x2, row(norm_mix[1]), w_qkv, cos, sin)
    o_parts, lse_parts = [], []
    for gi in range(3):
        o_g, lse_g = attn_fwd(qkv_parts[gi], gi, f"attn_fwd_g{gi}")
        o_parts.append(o_g)
        lse_parts.append(lse_g)
    x3, merged, o_nat, lse_nat = attn_out_fwd(x2, o_parts, lse_parts, w_ao)
    h3, gate1, up1, act1, dx4, d_norm_final, loss_local = ffn_fwd(
        x3, row(norm_ffn[1]), *ffn1, "ffn_fwd1", loss_head=(row(norm_final), target))

    red = GradReducer(c_idx, pos_idx)
    dgate1, dup1, dx3, d_nf1 = ffn_bwd(dx4, x3, row(norm_ffn[1]), gate1, up1, *ffn1, "ffn_bwd1")
    g_gate1 = wgrad_row_sharded("wgrad_gate1", dgate1, h3)
    g_up1 = wgrad_row_sharded("wgrad_up1", dup1, h3)
    g_down1 = wgrad_row_sharded("wgrad_down1", act1, dx4)
    red.push("gate1", g_gate1)
    red.push("up1", g_up1)
    do_parts, c_parts = red.carried_by(attn_out_bwd, dx3, w_ao, o_nat, lse_nat)
    g_ao = wgrad_full("wgrad_attn_out", merged, dx3)
    red.push("down1", g_down1)
    red.push("attn_out", chip_rows(unpad_groups(jnp.split(g_ao, 3, axis=0), 0)))
    dqkv_parts = [red.carried_by(attn_bwd, qkv_parts[gi], do_parts[gi], lse_parts[gi], c_parts[gi], gi,
                                 f"attn_bwd_g{gi}") for gi in range(3)]
    dqkv, dx2, d_nm1 = qkv_bwd(dqkv_parts, w_qkv, dx3, x2, row(norm_mix[1]), cos, sin)
    red.push("qkv", unpad_qkv_grad(wgrad_col_sharded("wgrad_qkv", h2, dqkv)))

    dgate0, dup0, dx1, d_nf0 = red.carried_by(ffn_bwd, dx2, x1, row(norm_ffn[0]), gate0, up0, *ffn0, "ffn_bwd0")
    red.push("gate0", red.carried_by(wgrad_row_sharded, "wgrad_gate0", dgate0, h1))
    red.push("up0", red.carried_by(wgrad_row_sharded, "wgrad_up0", dup0, h1))
    red.push("down0", red.carried_by(wgrad_row_sharded, "wgrad_down0", act0, dx2))
    red.push("pool_out", chip_rows(red.carried_by(wgrad_full, "wgrad_pool_out", z, dx1)))
    dzs, du, dx0, d_nm0, d_scale = pool_bwd(dx1, x0, row(norm_mix[0]), w_in, w_grp, pool_scale, w_out, zr)
    g_grp = red.carried_by(wgrad_pool_group, "wgrad_pool_group", p, dzs)
    red.push("pool_group", g_grp.reshape(N_SHARDS, 4 * grp_rows, POOL_GROUP_DIM))
    red.push("pool_in", chip_rows(red.carried_by(wgrad_full, "wgrad_pool_in", h0, du)))
    red.drain("rs_tail")
    full = [red.done[nm] for nm in ("pool_in", "pool_group", "pool_out", "qkv", "attn_out",
                                    "gate0", "gate1", "up0", "up1", "down0", "down1")]

    zero_row = jnp.zeros((1, D_MODEL), F32)
    small = jnp.concatenate([d_nm0, d_nm1, d_nf0, d_nf1, d_norm_final, d_scale,
                             jnp.broadcast_to(loss_local, (1, D_MODEL)), zero_row], axis=0)
    small = allreduce_small(small)
    loss = small[6, 0]

    pack = lambda a, b, c, d: jnp.concatenate([a, b, row(c), d, zero_row, zero_row], axis=0)[None]
    sg, sd, sm, sv = adamw("adamw_small", [small],
                           pack(norm_mix, norm_ffn, norm_final, pool_scale),
                           pack(m_norm_mix, m_norm_ffn, m_norm_final, m_pool_scale),
                           pack(v_norm_mix, v_norm_ffn, v_norm_final, v_pool_scale))
    unpack = lambda a: (a[0, 0:2], a[0, 2:4], a[0, 4], a[0, 5:6])

    def update(name, grads, w, m, v, transposed=False):
        if transposed:
            w, m, v = (jnp.swapaxes(a, 1, 2) for a in (w, m, v))
        n_layers = len(grads)
        shp = (n_layers,) + grads[0].shape
        outs = [o.reshape(w.shape) for o in adamw(name, grads, w.reshape(shp), m.reshape(shp), v.reshape(shp))]
        return [jnp.swapaxes(o, 1, 2) for o in outs] if transposed else outs

    big = [
        update("adamw_pool_in", [full[0]], pool_w_in, m_pool_w_in, v_pool_w_in),
        update("adamw_pool_group", [full[1]], pool_w_group, m_pool_w_group, v_pool_w_group),
        update("adamw_pool_out", [full[2]], pool_w_out, m_pool_w_out, v_pool_w_out),
        update("adamw_qkv", [full[3]], attn_w_qkv, m_attn_w_qkv, v_attn_w_qkv),
        update("adamw_attn_out", [full[4]], attn_w_out, m_attn_w_out, v_attn_w_out),
        update("adamw_gate", [full[5], full[6]], ffn_w_gate, m_ffn_w_gate, v_ffn_w_gate, transposed=True),
        update("adamw_up", [full[7], full[8]], ffn_w_up, m_ffn_w_up, v_ffn_w_up, transposed=True),
        update("adamw_down", [full[9], full[10]], ffn_w_down, m_ffn_w_down, v_ffn_w_down),
    ]

    def leaves(k, small_vals):
        nm, nf, nfin, psc = unpack(small_vals)
        return [nm, nf, nfin, big[0][k], big[1][k], psc, big[2][k], big[3][k], big[4][k],
                big[5][k], big[6][k], big[7][k]]

    grad_x = dx0.reshape(x.shape)
    return (loss, grad_x, *leaves(0, sg), *leaves(1, sd), *leaves(2, sm), *leaves(3, sv))
```

```python
import math

import jax
import jax.numpy as jnp
from jax import lax
from jax.experimental import pallas as pl
from jax.experimental.pallas import tpu as pltpu

F32 = jnp.float32
BF16 = jnp.bfloat16

D_MODEL = 1024
N_SHARDS = 4
N_DEV = 8
D_FF = 2816
FF_SHARD = D_FF // N_SHARDS
HEAD_DIM = 64
QKV_SHARD = 3 * D_MODEL // N_SHARDS
POOL_WINDOWS = (2, 4, 8, 16)
POOL_GROUP_DIM = 256
POOL_HALO = 16
ATTN_W = 128
GROUP_LANES = (0, 384, 704, 1024)
GROUP_HEADS = (6, 5, 5)
GROUP_DIL = (1, 4, 16)
ROPE_THETA = 10000.0
EPS = 1e-6
NEG_INF = -1e30
LANE = 128
VMEM_LIMIT_BYTES = 60 * 1024 * 1024

ADAM_LR = 0.001
ADAM_B1 = 0.9
ADAM_B2 = 0.999
ADAM_EPS = 1e-08
ADAM_WD = 0.01
ADAM_STEP = 10

NT_DIMS = (((1,), (1,)), ((), ()))
TN_DIMS = (((0,), (0,)), ((), ()))
MESH = pl.DeviceIdType.MESH


_ANY = pl.BlockSpec(memory_space=pl.ANY)


def _pcall(body, *, name, out_shape, grid=None, in_specs=None, out_specs=None, scratch_shapes=(),
           semantics=None, duty=None):
    kw = {}
    if in_specs is not None and duty is None:
        kw["in_specs"] = in_specs
    if out_specs is not None and duty is None:
        kw["out_specs"] = out_specs
    if grid is not None:
        kw["grid"] = grid
    params = dict(dimension_semantics=semantics, vmem_limit_bytes=VMEM_LIMIT_BYTES)
    if duty is None:
        return pl.pallas_call(body, name=name, out_shape=out_shape, scratch_shapes=list(scratch_shapes),
                              compiler_params=pltpu.CompilerParams(**params), **kw)

    single = not isinstance(out_shape, (list, tuple))
    c_out_shape = [out_shape] if single else list(out_shape)
    c_out_specs = [out_specs] if single else list(out_specs)
    n_in, n_out, n_scr = len(in_specs), len(c_out_shape), len(scratch_shapes)
    d_in, d_out = len(duty.ins), len(duty.out_shape)
    total = math.prod(grid)
    mid_step = (5 * total) // 6

    def wrapped(*refs):
        c_in, d_ins = refs[:n_in], refs[n_in:n_in + d_in]
        o0 = n_in + d_in
        c_outs, d_outs = refs[o0:o0 + n_out], refs[o0 + n_out:o0 + n_out + d_out]
        s0 = o0 + n_out + d_out
        c_scr, d_sems = refs[s0:s0 + n_scr], refs[s0 + n_scr:]
        step = pl.program_id(0)
        for ax in range(1, len(grid)):
            step = step * grid[ax] + pl.program_id(ax)

        @pl.when(step == 0)
        def _():
            duty.start(d_ins, d_outs, d_sems)

        body(*c_in, *c_outs, *c_scr)

        @pl.when(step == mid_step)
        def _():
            duty.mid(d_ins, d_outs, d_sems)

        @pl.when(step == total - 1)
        def _():
            duty.finish(d_ins, d_outs, d_sems)

    call = pl.pallas_call(
        wrapped, name=name, grid=grid,
        in_specs=list(in_specs) + [_ANY] * d_in, out_specs=c_out_specs + [_ANY] * d_out,
        out_shape=c_out_shape + list(duty.out_shape),
        scratch_shapes=list(scratch_shapes) + list(duty.scratch),
        input_output_aliases={n_in + i: n_out + o for i, o in duty.aliases.items()},
        compiler_params=pltpu.CompilerParams(has_side_effects=True, **params))

    def run(*args):
        outs = call(*args, *duty.ins)
        c = outs[:n_out]
        return (c[0] if single else list(c)), list(outs[n_out:])

    return run


def _sds(shape, dtype):
    return jax.ShapeDtypeStruct(tuple(shape), dtype)


def _dot(a, b):
    return jnp.dot(a, b, preferred_element_type=F32)


def _dot_nt(a, b):
    return lax.dot_general(a, b, NT_DIMS, preferred_element_type=F32)


def _dot_tn(a, b):
    return lax.dot_general(a, b, TN_DIMS, preferred_element_type=F32)


def _rms_fwd(x, g):
    r = lax.rsqrt(jnp.mean(x * x, axis=-1, keepdims=True) + EPS)
    return x * r * g


def _rms_bwd(dh, x, g):
    r = lax.rsqrt(jnp.mean(x * x, axis=-1, keepdims=True) + EPS)
    xh = x * r
    dg = jnp.sum(dh * xh, axis=0, keepdims=True)
    dxh = dh * g
    dx = r * (dxh - xh * jnp.mean(dxh * xh, axis=-1, keepdims=True))
    return dx, dg


def _sigmoid(x):
    return 0.5 * jnp.tanh(0.5 * x) + 0.5


def _tile_rows(t):
    return min(512, t)


def _sub_tiles(tm, n_sub=2):
    rows = tm // n_sub
    return [pl.ds(i * rows, rows) for i in range(n_sub)]


def _wgrad_rows(t):
    return min(2048, t)


def pool_fwd(x, g_row, w_in, w_grp, scale, w_out, duty=None):
    t_len = x.shape[0]
    tm = _tile_rows(t_len)

    def body(x_ref, g_ref, win_ref, wgrp_ref, scale_ref, wout_ref,
             h_ref, p_ref, zr_ref, z_ref, xo_ref, ubuf):
        t = pl.program_id(0)

        @pl.when(t == 0)
        def _():
            ubuf[pl.ds(0, POOL_HALO), :] = jnp.zeros((POOL_HALO, D_MODEL), F32)

        x_t = x_ref[...]
        h = _rms_fwd(x_t, g_ref[...]).astype(BF16)
        h_ref[...] = h
        ubuf[pl.ds(POOL_HALO, tm), :] = _dot(h, win_ref[...])
        row = t * tm + lax.broadcasted_iota(jnp.int32, (tm, 1), 0)
        for gi, w in enumerate(POOL_WINDOWS):
            cols = pl.ds(gi * POOL_GROUP_DIM, POOL_GROUP_DIM)
            u_g = ubuf[pl.ds(POOL_HALO, tm), cols]
            acc = u_g
            for j in range(1, w):
                acc = acc + ubuf[pl.ds(POOL_HALO - j, tm), cols]
            inv_cnt = 1.0 / jnp.minimum(row + 1, w).astype(F32)
            p_g = (acc * inv_cnt - u_g).astype(BF16)
            p_ref[:, cols] = p_g
            z_g = _dot(p_g, wgrp_ref[gi])
            zr_ref[:, cols] = z_g.astype(BF16)
            z_ref[:, cols] = (z_g * scale_ref[:, cols]).astype(BF16)
        ubuf[pl.ds(0, POOL_HALO), :] = ubuf[pl.ds(tm, POOL_HALO), :]
        xo_ref[...] = x_t + _dot(z_ref[...], wout_ref[...])

    row_spec = pl.BlockSpec((tm, D_MODEL), lambda t: (t, 0))
    full2 = lambda shape: pl.BlockSpec(shape, lambda t: (0,) * len(shape))
    return _pcall(
        body, name="pool_fwd", grid=(t_len // tm,),
        in_specs=[row_spec, full2((1, D_MODEL)), full2((D_MODEL, D_MODEL)),
                  full2((4, POOL_GROUP_DIM, POOL_GROUP_DIM)), full2((1, D_MODEL)), full2((D_MODEL, D_MODEL))],
        out_specs=[row_spec] * 5,
        out_shape=[_sds((t_len, D_MODEL), BF16)] * 4 + [_sds((t_len, D_MODEL), F32)],
        scratch_shapes=[pltpu.VMEM((tm + POOL_HALO, D_MODEL), F32)],
        semantics=("arbitrary",), duty=duty,
    )(x, g_row, w_in, w_grp, scale, w_out)


def ffn_fwd(x, g_row, w_gate_t, w_up_t, w_down, name, duty=None):
    t_len = x.shape[0]
    tm = min(1024, t_len)

    def body(x_ref, g_ref, wg_ref, wu_ref, wd_ref, h_ref, go_ref, uo_ref, ao_ref, xo_ref, hbuf, acc):
        s = pl.program_id(1)

        @pl.when(s == 0)
        def _():
            h = _rms_fwd(x_ref[...], g_ref[...]).astype(BF16)
            hbuf[...] = h
            h_ref[...] = h
            acc[...] = jnp.zeros_like(acc)

        h = hbuf[...]
        gate = _dot_nt(h, wg_ref[...])
        up = _dot_nt(h, wu_ref[...])
        go_ref[...] = gate.astype(BF16)
        uo_ref[...] = up.astype(BF16)
        act = (gate * _sigmoid(gate) * up).astype(BF16)
        ao_ref[...] = act
        acc[...] += _dot(act, wd_ref[...])

        @pl.when(s == N_SHARDS - 1)
        def _():
            xo_ref[...] = x_ref[...] + acc[...]

    row_spec = pl.BlockSpec((tm, D_MODEL), lambda t, s: (t, 0))
    row_w = pl.BlockSpec((None, FF_SHARD, D_MODEL), lambda t, s: (s, 0, 0))
    act_spec = pl.BlockSpec((None, tm, FF_SHARD), lambda t, s: (s, t, 0))
    return _pcall(
        body, name=name, grid=(t_len // tm, N_SHARDS),
        in_specs=[row_spec, pl.BlockSpec((1, D_MODEL), lambda t, s: (0, 0)), row_w, row_w, row_w],
        out_specs=[row_spec, act_spec, act_spec, act_spec, row_spec],
        out_shape=[_sds((t_len, D_MODEL), BF16)] + [_sds((N_SHARDS, t_len, FF_SHARD), BF16)] * 3
                  + [_sds((t_len, D_MODEL), F32)],
        scratch_shapes=[pltpu.VMEM((tm, D_MODEL), BF16), pltpu.VMEM((tm, D_MODEL), F32)],
        semantics=("arbitrary", "arbitrary"), duty=duty,
    )(x, g_row, w_gate_t, w_up_t, w_down)


def ffn_bwd(dxo, x, g_row, gate, up, w_gate_t, w_up_t, w_down, name, duty=None):
    t_len = x.shape[0]
    tm = _tile_rows(t_len)

    def body(dxo_ref, x_ref, g_ref, gate_ref, up_ref, wg_ref, wu_ref, wd_ref,
             dg_ref, du_ref, dx_ref, dn_ref, dxb, dh):
        t = pl.program_id(0)
        s = pl.program_id(1)

        @pl.when(s == 0)
        def _():
            dxb[...] = dxo_ref[...].astype(BF16)
            dh[...] = jnp.zeros_like(dh)

        @pl.when(jnp.logical_and(s == 0, t == 0))
        def _():
            dn_ref[...] = jnp.zeros_like(dn_ref)

        sub_tiles = _sub_tiles(tm)
        dacts = [_dot_nt(dxb[rows, :], wd_ref[...]) for rows in sub_tiles]
        for rows, dact in zip(sub_tiles, dacts):
            gv = gate_ref[rows, :].astype(F32)
            uv = up_ref[rows, :].astype(F32)
            sg = _sigmoid(gv)
            dgv = (dact * uv * (sg * (1.0 + gv * (1.0 - sg)))).astype(BF16)
            duv = (dact * (gv * sg)).astype(BF16)
            dg_ref[rows, :] = dgv
            du_ref[rows, :] = duv
            dh[rows, :] += _dot(dgv, wg_ref[...]) + _dot(duv, wu_ref[...])

        @pl.when(s == N_SHARDS - 1)
        def _():
            dx, dn = _rms_bwd(dh[...], x_ref[...], g_ref[...])
            dx_ref[...] = dxo_ref[...] + dx
            dn_ref[...] += dn

    row_spec = pl.BlockSpec((tm, D_MODEL), lambda t, s: (t, 0))
    vec_spec = pl.BlockSpec((1, D_MODEL), lambda t, s: (0, 0))
    row_w = pl.BlockSpec((None, FF_SHARD, D_MODEL), lambda t, s: (s, 0, 0))
    act_spec = pl.BlockSpec((None, tm, FF_SHARD), lambda t, s: (s, t, 0))
    act_shape = _sds((N_SHARDS, t_len, FF_SHARD), BF16)
    return _pcall(
        body, name=name, grid=(t_len // tm, N_SHARDS),
        in_specs=[row_spec, row_spec, vec_spec, act_spec, act_spec, row_w, row_w, row_w],
        out_specs=[act_spec, act_spec, row_spec, vec_spec],
        out_shape=[act_shape, act_shape, _sds((t_len, D_MODEL), F32), _sds((1, D_MODEL), F32)],
        scratch_shapes=[pltpu.VMEM((tm, D_MODEL), BF16), pltpu.VMEM((tm, D_MODEL), F32)],
        semantics=("arbitrary", "arbitrary"), duty=duty,
    )(dxo, x, g_row, gate, up, w_gate_t, w_up_t, w_down)


def tn_matmul(name, a, b, a_spec, b_spec, out_shape, out_spec, grid, duty=None):
    def body(a_ref, b_ref, o_ref):
        @pl.when(pl.program_id(len(grid) - 1) == 0)
        def _():
            o_ref[...] = jnp.zeros_like(o_ref)

        res = _dot_tn(a_ref[...].astype(BF16), b_ref[...].astype(BF16))
        o_ref[...] += res.reshape(o_ref.shape)

    return _pcall(body, name=name, grid=grid, in_specs=[a_spec, b_spec], out_specs=out_spec,
                  out_shape=out_shape, semantics=("arbitrary",) * len(grid), duty=duty)(a, b)


def wgrad_full(name, a, b, duty=None):
    t_len, k = a.shape
    n = b.shape[1]
    tt = _wgrad_rows(t_len)
    return tn_matmul(name, a, b,
                     pl.BlockSpec((tt, k), lambda t: (t, 0)), pl.BlockSpec((tt, n), lambda t: (t, 0)),
                     _sds((k, n), F32), pl.BlockSpec((k, n), lambda t: (0, 0)), (t_len // tt,), duty)


def wgrad_col_sharded(name, a, b_sh, duty=None):
    t_len, k = a.shape
    n_sh, _, n = b_sh.shape
    tt = _wgrad_rows(t_len)
    return tn_matmul(name, a, b_sh,
                     pl.BlockSpec((tt, k), lambda s, t: (t, 0)), pl.BlockSpec((None, tt, n), lambda s, t: (s, t, 0)),
                     _sds((n_sh, k, n), F32), pl.BlockSpec((None, k, n), lambda s, t: (s, 0, 0)),
                     (n_sh, t_len // tt), duty)


def wgrad_row_sharded(name, a_sh, b, duty=None):
    t_len, n = b.shape
    n_sh, _, k = a_sh.shape
    tt = _wgrad_rows(t_len)

    def body(a_ref, b_ref, o_ref):
        s = pl.program_id(1)
        res = _dot_tn(a_ref[...], b_ref[...].astype(BF16))

        @pl.when(pl.program_id(0) == 0)
        def _():
            o_ref[s] = res

        @pl.when(pl.program_id(0) > 0)
        def _():
            o_ref[s] += res

    return _pcall(body, name=name, grid=(t_len // tt, n_sh),
                  in_specs=[pl.BlockSpec((None, tt, k), lambda t, s: (s, t, 0)),
                            pl.BlockSpec((tt, n), lambda t, s: (t, 0))],
                  out_specs=pl.BlockSpec((n_sh, k, n), lambda t, s: (0, 0, 0)),
                  out_shape=_sds((n_sh, k, n), F32), semantics=("arbitrary", "arbitrary"), duty=duty)(a_sh, b)


def wgrad_pool_group(name, p, dzs, duty=None):
    t_len = p.shape[0]
    tt = _wgrad_rows(t_len)
    gd = POOL_GROUP_DIM
    rows = gd // N_SHARDS
    return tn_matmul(name, p, dzs,
                     pl.BlockSpec((tt, gd), lambda g, t: (t, g)), pl.BlockSpec((tt, gd), lambda g, t: (t, g)),
                     _sds((N_SHARDS, 4, rows, gd), F32),
                     pl.BlockSpec((N_SHARDS, None, rows, gd), lambda g, t: (0, g, 0, 0)),
                     (4, t_len // tt), duty)


PAD_LANES = 384
QKV_PAD = 3 * PAD_LANES
N_SLABS = QKV_PAD // LANE
GROUP_REAL = tuple(GROUP_LANES[g + 1] - GROUP_LANES[g] for g in range(3))
Q_BLOCK = 512


def pad_groups(w, axis):
    parts = []
    for g in range(3):
        blk = lax.slice_in_dim(w, GROUP_LANES[g], GROUP_LANES[g + 1], axis=axis)
        pad = [(0, 0)] * w.ndim
        pad[axis] = (0, PAD_LANES - GROUP_REAL[g])
        parts.append(jnp.pad(blk, pad))
    return parts


def unpad_groups(parts, axis):
    return jnp.concatenate([lax.slice_in_dim(p, 0, GROUP_REAL[g], axis=axis) for g, p in enumerate(parts)],
                           axis=axis)


def pad_qkv_weight(w_qkv_sh):
    w = jnp.transpose(w_qkv_sh, (1, 0, 2)).reshape(D_MODEL, 3 * D_MODEL)
    q, k, v = (pad_groups(w[:, i * D_MODEL:(i + 1) * D_MODEL], 1) for i in range(3))
    return jnp.stack([jnp.concatenate([q[g], k[g], v[g]], axis=1) for g in range(3)])


def unpad_qkv_grad(g_pad):
    cols = [unpad_groups([g_pad[g][:, i * PAD_LANES:(i + 1) * PAD_LANES] for g in range(3)], 1) for i in range(3)]
    w = jnp.concatenate(cols, axis=1)
    return jnp.transpose(w.reshape(D_MODEL, N_SHARDS, QKV_SHARD), (1, 0, 2))


def rope_tables(t_len):
    inv_freq = 1.0 / (ROPE_THETA ** (jnp.arange(0, HEAD_DIM, 2, dtype=F32) / HEAD_DIM))
    ang = jnp.arange(t_len, dtype=F32)[:, None] * inv_freq[None, :]
    cos_h, sin_h = lax.optimization_barrier((jnp.cos(ang), jnp.sin(ang)))
    reps = (1, 2 * LANE // HEAD_DIM)
    return jnp.tile(cos_h, reps), jnp.tile(sin_h, reps)


def _rot_half(v):
    n = v.shape[1]
    lane = lax.broadcasted_iota(jnp.int32, v.shape, 1)
    return jnp.where(lane % HEAD_DIM < HEAD_DIM // 2,
                     -pltpu.roll(v, n - HEAD_DIM // 2, 1), pltpu.roll(v, HEAD_DIM // 2, 1))


def _lane_cols(j):
    return slice(j * LANE, (j + 1) * LANE)


def _to_residue_major(slab, j_src, dst_ref, j_dst, dil, rows):
    for r in range(dil):
        dst_ref[r, :, _lane_cols(j_dst)] = slab[j_src, pl.ds(r, rows // dil, stride=dil), :].astype(dst_ref.dtype)


def _to_natural(src_ref, j_src, slab, j_dst, dil, rows):
    for r in range(dil):
        slab[j_dst, pl.ds(r, rows // dil, stride=dil), :] = src_ref[r, :, _lane_cols(j_src)].astype(F32)


def qkv_fwd(x, g_row, w_pad, cos, sin):
    t_len = x.shape[0]
    tm = _tile_rows(t_len)

    def body(x_ref, g_ref, w_ref, cos_ref, sin_ref, h_ref, o1_ref, o4_ref, o16_ref, slabs):
        h = _rms_fwd(x_ref[...], g_ref[...]).astype(BF16)
        h_ref[...] = h
        accs = [_dot(h, w_ref[gi]) for gi in range(3)]
        cos_t = cos_ref[...]
        sin_t = sin_ref[...]
        for gi, (dil, o_ref) in enumerate(zip(GROUP_DIL, (o1_ref, o4_ref, o16_ref))):
            slab = slabs.at[gi]
            for j in range(N_SLABS):
                a = accs[gi][:, _lane_cols(j)]
                if j < 6:
                    a = a * cos_t + _rot_half(a) * sin_t
                if j < 3:
                    a = a * (HEAD_DIM ** -0.5)
                if dil == 1:
                    o_ref[0, :, _lane_cols(j)] = a.astype(BF16)
                else:
                    slab[j] = a
                    _to_residue_major(slab, j, o_ref, j, dil, tm)

    row_spec = pl.BlockSpec((tm, D_MODEL), lambda t: (t, 0))
    tab_spec = pl.BlockSpec((tm, LANE), lambda t: (t, 0))
    out_specs = [row_spec] + [pl.BlockSpec((d, tm // d, QKV_PAD), lambda t: (0, t, 0)) for d in GROUP_DIL]
    out_shape = [_sds((t_len, D_MODEL), BF16)] + [_sds((d, t_len // d, QKV_PAD), BF16) for d in GROUP_DIL]
    return _pcall(
        body, name="qkv_fwd", grid=(t_len // tm,),
        in_specs=[row_spec, pl.BlockSpec((1, D_MODEL), lambda t: (0, 0)),
                  pl.BlockSpec((3, D_MODEL, QKV_PAD), lambda t: (0, 0, 0)), tab_spec, tab_spec],
        out_specs=out_specs, out_shape=out_shape,
        scratch_shapes=[pltpu.VMEM((3, N_SLABS, tm, LANE), F32)],
        semantics=("arbitrary",),
    )(x, g_row, w_pad, cos, sin)


def _band_mask(n):
    qi = lax.broadcasted_iota(jnp.int32, (ATTN_W, 2 * ATTN_W), 0)
    kj = lax.broadcasted_iota(jnp.int32, (ATTN_W, 2 * ATTN_W), 1)
    dist = ATTN_W + qi - kj
    return (dist >= 0) & (dist <= ATTN_W) & ((kj >= ATTN_W) | (n > 0))


def _half_masks():
    lane = lax.broadcasted_iota(jnp.int32, (1, LANE), 1)
    return [lane < HEAD_DIM, lane >= HEAD_DIM]


def _live_halves(gi, j):
    hms = _half_masks()
    return hms if (gi == 0 or j < 2) else hms[:1]


def attn_fwd(qkv_g, gi, name):
    dil, l_len, _ = qkv_g.shape
    qb = min(Q_BLOCK, l_len)
    nsub = qb // ATTN_W

    def body(q_ref, kc_ref, kp_ref, vc_ref, vp_ref, o_ref, lse_ref, kbuf, vbuf):
        n = pl.program_id(1)
        kbuf[pl.ds(0, ATTN_W), :] = kp_ref[...]
        kbuf[pl.ds(ATTN_W, qb), :] = kc_ref[...]
        vbuf[pl.ds(0, ATTN_W), :] = vp_ref[...]
        vbuf[pl.ds(ATTN_W, qb), :] = vc_ref[...]

        def sub(b, carry):
            r0 = pl.multiple_of(b * ATTN_W, ATTN_W)
            mask = _band_mask(n + b)
            krows = pl.ds(r0, 2 * ATTN_W)
            scores = []
            for j in range(3):
                q = q_ref[pl.ds(r0, ATTN_W), _lane_cols(j)]
                for hm in _live_halves(gi, j):
                    scores.append(_dot_nt(jnp.where(hm, q, jnp.zeros_like(q)), kbuf[krows, _lane_cols(j)]))
            scores = iter(scores)
            head_lane = lax.broadcasted_iota(jnp.int32, (1, LANE), 1)
            lse = jnp.zeros((ATTN_W, LANE), F32)
            for j in range(3):
                cols = _lane_cols(j)
                v = vbuf[krows, cols]
                o = jnp.zeros((ATTN_W, LANE), F32)
                for half, hm in enumerate(_live_halves(gi, j)):
                    s = jnp.where(mask, next(scores), NEG_INF)
                    m = jnp.max(s, axis=-1, keepdims=True)
                    e = jnp.exp(s - m)
                    den = jnp.sum(e, axis=-1, keepdims=True)
                    p = (e * (1.0 / den)).astype(BF16)
                    o = jnp.where(hm, _dot(p, v), o)
                    lse = jnp.where(head_lane == 2 * j + half, m + jnp.log(den), lse)
                o_ref[pl.ds(r0, ATTN_W), cols] = o.astype(BF16)
            lse_ref[pl.ds(r0, ATTN_W), :] = lse
            return carry

        lax.fori_loop(0, nsub, sub, 0)

    cur = lambda c: pl.BlockSpec((None, qb, PAD_LANES), lambda r, n: (r, n, c))
    prev = lambda c: pl.BlockSpec((None, ATTN_W, PAD_LANES), lambda r, n: (r, jnp.maximum(n * nsub - 1, 0), c))
    return _pcall(
        body, name=name, grid=(dil, l_len // qb),
        in_specs=[cur(0), cur(1), prev(1), cur(2), prev(2)],
        out_specs=[pl.BlockSpec((None, qb, PAD_LANES), lambda r, n: (r, n, 0)),
                   pl.BlockSpec((None, qb, LANE), lambda r, n: (r, n, 0))],
        out_shape=[_sds((dil, l_len, PAD_LANES), BF16), _sds((dil, l_len, LANE), F32)],
        scratch_shapes=[pltpu.VMEM((qb + ATTN_W, PAD_LANES), BF16), pltpu.VMEM((qb + ATTN_W, PAD_LANES), BF16)],
        semantics=("arbitrary", "arbitrary"),
    )(qkv_g, qkv_g, qkv_g, qkv_g, qkv_g)


def _group_stats(lses):
    head_lane = lax.broadcasted_iota(jnp.int32, (1, LANE), 1)
    fulls, glse = [], []
    for g in range(3):
        real = head_lane < GROUP_HEADS[g]
        mx = jnp.max(jnp.where(real, lses[g], -jnp.inf), axis=-1, keepdims=True)
        sm = jnp.sum(jnp.where(real, jnp.exp(lses[g] - mx), 0.0), axis=-1, keepdims=True)
        fulls.append(mx + jnp.log(sm))
        glse.append(fulls[g] - math.log(GROUP_HEADS[g]))
    top = jnp.maximum(jnp.maximum(glse[0], glse[1]), glse[2])
    ex = [jnp.exp(v - top) for v in glse]
    tot = ex[0] + ex[1] + ex[2]
    alpha = [v / tot for v in ex]
    lane = lax.broadcasted_iota(jnp.int32, (1, QKV_PAD), 1)
    scale = jnp.where(lane < PAD_LANES, 3.0 * alpha[0],
                      jnp.where(lane < 2 * PAD_LANES, 3.0 * alpha[1], 3.0 * alpha[2]))
    return alpha, fulls, scale


def attn_out_fwd(x, o_parts, lse_parts, w_out_pad):
    t_len = x.shape[0]
    tm = _tile_rows(t_len)

    def body(x_ref, o1, o4, o16, l1, l4, l16, w_ref, xo_ref, mg_ref, o_ref, lse_ref, o_slab, l_slab):
        for gi, (dil, og, lg) in enumerate(zip(GROUP_DIL, (o1, o4, o16), (l1, l4, l16))):
            for j in range(3):
                _to_natural(og, j, o_slab, 3 * gi + j, dil, tm)
            _to_natural(lg, 0, l_slab, gi, dil, tm)
        o = jnp.concatenate([o_slab[j] for j in range(N_SLABS)], axis=1)
        lses = [l_slab[gi] for gi in range(3)]
        o_ref[...] = o.astype(BF16)
        for gi in range(3):
            lse_ref[:, _lane_cols(gi)] = lses[gi]
        _, _, scale = _group_stats(lses)
        merged = (o * scale).astype(BF16)
        mg_ref[...] = merged
        xo_ref[...] = x_ref[...] + _dot(merged, w_ref[...])

    row_spec = pl.BlockSpec((tm, D_MODEL), lambda t: (t, 0))
    pad_spec = pl.BlockSpec((tm, QKV_PAD), lambda t: (t, 0))
    o_specs = [pl.BlockSpec((d, tm // d, PAD_LANES), lambda t: (0, t, 0)) for d in GROUP_DIL]
    lse_specs = [pl.BlockSpec((d, tm // d, LANE), lambda t: (0, t, 0)) for d in GROUP_DIL]
    return _pcall(
        body, name="attn_out_fwd", grid=(t_len // tm,),
        in_specs=[row_spec] + o_specs + lse_specs + [pl.BlockSpec((QKV_PAD, D_MODEL), lambda t: (0, 0))],
        out_specs=[row_spec, pad_spec, pad_spec, pl.BlockSpec((tm, 3 * LANE), lambda t: (t, 0))],
        out_shape=[_sds((t_len, D_MODEL), F32), _sds((t_len, QKV_PAD), BF16),
                   _sds((t_len, QKV_PAD), BF16), _sds((t_len, 3 * LANE), F32)],
        scratch_shapes=[pltpu.VMEM((N_SLABS, tm, LANE), F32), pltpu.VMEM((3, tm, LANE), F32)],
        semantics=("arbitrary",),
    )(x, *o_parts, *lse_parts, w_out_pad)


def attn_out_bwd(dxo, w_out_pad, o, lse, duty=None):
    t_len = dxo.shape[0]
    tm = _tile_rows(t_len)

    def body(dx_ref, w_ref, o_ref, lse_ref, d1, d4, d16, c1, c4, c16, slab):
        dmerged = _dot_nt(dx_ref[...].astype(BF16), w_ref[...])
        o_t = o_ref[...].astype(F32)
        lses = [lse_ref[:, _lane_cols(gi)] for gi in range(3)]
        alpha, fulls, scale = _group_stats(lses)
        e = dmerged * o_t
        lane = lax.broadcasted_iota(jnp.int32, (1, QKV_PAD), 1)
        dalpha = [3.0 * jnp.sum(jnp.where((lane >= g * PAD_LANES) & (lane < g * PAD_LANES + GROUP_REAL[g]), e, 0.0),
                                axis=-1, keepdims=True) for g in range(3)]
        mean_da = alpha[0] * dalpha[0] + alpha[1] * dalpha[1] + alpha[2] * dalpha[2]
        dglse = [alpha[g] * (dalpha[g] - mean_da) for g in range(3)]
        do = dmerged * scale
        es = e * scale
        for j in range(N_SLABS):
            slab[j] = do[:, _lane_cols(j)]
        for gi, (dil, dg) in enumerate(zip(GROUP_DIL, (d1, d4, d16))):
            for j in range(3):
                _to_residue_major(slab, 3 * gi + j, dg, j, dil, tm)
        head_lane = lax.broadcasted_iota(jnp.int32, (1, LANE), 1)
        first = head_lane < HEAD_DIM
        for gi, (dil, cg) in enumerate(zip(GROUP_DIL, (c1, c4, c16))):
            c_g = -(dglse[gi] * jnp.exp(lses[gi] - fulls[gi]))
            for j in range(3):
                blk = es[:, _lane_cols(3 * gi + j)]
                halves = (jnp.sum(jnp.where(first, blk, 0.0), axis=-1, keepdims=True),
                          jnp.sum(jnp.where(first, 0.0, blk), axis=-1, keepdims=True))
                for half in range(2):
                    c_g = c_g + jnp.where(head_lane == 2 * j + half, halves[half], 0.0)
            slab[gi] = c_g
            _to_residue_major(slab, gi, cg, 0, dil, tm)

    row_spec = pl.BlockSpec((tm, D_MODEL), lambda t: (t, 0))
    pad_spec = pl.BlockSpec((tm, QKV_PAD), lambda t: (t, 0))
    do_specs = [pl.BlockSpec((d, tm // d, PAD_LANES), lambda t: (0, t, 0)) for d in GROUP_DIL]
    c_specs = [pl.BlockSpec((d, tm // d, LANE), lambda t: (0, t, 0)) for d in GROUP_DIL]
    outs = _pcall(
        body, name="attn_out_bwd", grid=(t_len // tm,),
        in_specs=[row_spec, pl.BlockSpec((QKV_PAD, D_MODEL), lambda t: (0, 0)), pad_spec,
                  pl.BlockSpec((tm, 3 * LANE), lambda t: (t, 0))],
        out_specs=do_specs + c_specs,
        out_shape=[_sds((d, t_len // d, PAD_LANES), BF16) for d in GROUP_DIL]
                  + [_sds((d, t_len // d, LANE), F32) for d in GROUP_DIL],
        scratch_shapes=[pltpu.VMEM((N_SLABS, tm, LANE), F32)],
        semantics=("arbitrary",), duty=duty,
    )(dxo, w_out_pad, o, lse)
    if duty is None:
        return outs[:3], outs[3:]
    return (outs[0][:3], outs[0][3:]), outs[1]


def attn_bwd(qkv_g, do_g, lse_g, c_g, gi, name, duty=None):
    dil, l_len, _ = qkv_g.shape
    qb = min(Q_BLOCK, l_len)
    nsub = qb // ATTN_W
    nsb = l_len // qb

    def body(q_ref, kc_ref, kp_ref, vc_ref, vp_ref, do_ref, lse_ref, c_ref,
             qn_ref, don_ref, lsen_ref, cn_ref, o_ref, kbuf, vbuf, dkbuf, dvbuf):
        n = pl.program_id(1)
        kbuf[pl.ds(0, ATTN_W), :] = kp_ref[...]
        kbuf[pl.ds(ATTN_W, qb), :] = kc_ref[...]
        vbuf[pl.ds(0, ATTN_W), :] = vp_ref[...]
        vbuf[pl.ds(ATTN_W, qb), :] = vc_ref[...]
        dkbuf[...] = jnp.zeros_like(dkbuf)
        dvbuf[...] = jnp.zeros_like(dvbuf)

        def block(q_of, do_of, lse_of, c_of, krows, mask, dq_rows):
            heads = []
            for j in range(3):
                cols = _lane_cols(j)
                q, do_t, k, v = q_of(cols), do_of(cols), kbuf[krows, cols], vbuf[krows, cols]
                for half, hm in enumerate(_live_halves(gi, j)):
                    qh = jnp.where(hm, q, jnp.zeros_like(q))
                    doh = jnp.where(hm, do_t, jnp.zeros_like(do_t))
                    heads.append((j, 2 * j + half, hm, qh, doh, _dot_nt(qh, k), _dot_nt(doh, v)))
            head_lane = lax.broadcasted_iota(jnp.int32, (1, LANE), 1)
            lse_t, c_t = lse_of(), c_of()
            for j in range(3):
                cols = _lane_cols(j)
                k = kbuf[krows, cols]
                dq = jnp.zeros((ATTN_W, LANE), F32)
                dk = jnp.zeros((k.shape[0], LANE), F32)
                dv = jnp.zeros((k.shape[0], LANE), F32)
                for hj, head, hm, qh, doh, s, dp in heads:
                    if hj != j:
                        continue
                    lse_h = jnp.max(jnp.where(head_lane == head, lse_t, -jnp.inf), axis=-1, keepdims=True)
                    c_h = jnp.max(jnp.where(head_lane == head, c_t, -jnp.inf), axis=-1, keepdims=True)
                    p = jnp.exp(jnp.where(mask, s, NEG_INF) - lse_h)
                    ds = (p * (dp - c_h)).astype(BF16)
                    if dq_rows is not None:
                        dq = jnp.where(hm, _dot(ds, k), dq)
                    dk = dk + _dot_tn(ds, qh)
                    dv = dv + _dot_tn(p.astype(BF16), doh)
                if dq_rows is not None:
                    o_ref[dq_rows, cols] = dq.astype(BF16)
                dkbuf[krows, cols] += dk
                dvbuf[krows, cols] += dv

        def sub(b, carry):
            rows = pl.ds(pl.multiple_of(b * ATTN_W, ATTN_W), ATTN_W)
            krows = pl.ds(pl.multiple_of(b * ATTN_W, ATTN_W), 2 * ATTN_W)
            block(lambda c: q_ref[rows, c], lambda c: do_ref[rows, c], lambda: lse_ref[rows, :],
                  lambda: c_ref[rows, :], krows, _band_mask(n + b), rows)
            return carry

        lax.fori_loop(0, nsub, sub, 0)

        qi = lax.broadcasted_iota(jnp.int32, (ATTN_W, ATTN_W), 0)
        kj = lax.broadcasted_iota(jnp.int32, (ATTN_W, ATTN_W), 1)
        nmask = (qi <= kj) & (n < nsb - 1)
        block(lambda c: qn_ref[:, c], lambda c: don_ref[:, c], lambda: lsen_ref[...],
              lambda: cn_ref[...], pl.ds(qb, ATTN_W), nmask, None)
        o_ref[:, pl.ds(PAD_LANES, PAD_LANES)] = dkbuf[pl.ds(ATTN_W, qb), :].astype(BF16)
        o_ref[:, pl.ds(2 * PAD_LANES, PAD_LANES)] = dvbuf[pl.ds(ATTN_W, qb), :].astype(BF16)

    cur = lambda c: pl.BlockSpec((None, qb, PAD_LANES), lambda r, n: (r, n, c))
    prev = lambda c: pl.BlockSpec((None, ATTN_W, PAD_LANES), lambda r, n: (r, jnp.maximum(n * nsub - 1, 0), c))
    nxt_row = lambda r, n: (r, jnp.minimum((n + 1) * nsub, nsb * nsub - 1), 0)
    nxt = pl.BlockSpec((None, ATTN_W, PAD_LANES), nxt_row)
    head_cur = pl.BlockSpec((None, qb, LANE), lambda r, n: (r, n, 0))
    head_nxt = pl.BlockSpec((None, ATTN_W, LANE), nxt_row)
    return _pcall(
        body, name=name, grid=(dil, nsb),
        in_specs=[cur(0), cur(1), prev(1), cur(2), prev(2), cur(0), head_cur, head_cur, nxt, nxt, head_nxt, head_nxt],
        out_specs=pl.BlockSpec((None, qb, QKV_PAD), lambda r, n: (r, n, 0)),
        out_shape=_sds((dil, l_len, QKV_PAD), BF16),
        scratch_shapes=[pltpu.VMEM((qb + ATTN_W, PAD_LANES), BF16), pltpu.VMEM((qb + ATTN_W, PAD_LANES), BF16),
                        pltpu.VMEM((qb + ATTN_W, PAD_LANES), F32), pltpu.VMEM((qb + ATTN_W, PAD_LANES), F32)],
        semantics=("arbitrary", "arbitrary"), duty=duty,
    )(qkv_g, qkv_g, qkv_g, qkv_g, qkv_g, do_g, lse_g, c_g, qkv_g, do_g, lse_g, c_g)


def qkv_bwd(dqkv_parts, w_pad, dxo, x, g_row, cos, sin):
    t_len = x.shape[0]
    tm = _tile_rows(t_len)

    def body(p1, p4, p16, w_ref, dxo_ref, x_ref, g_ref, cos_ref, sin_ref, dq_ref, dx_ref, dn_ref, slabs):
        @pl.when(pl.program_id(0) == 0)
        def _():
            dn_ref[...] = jnp.zeros_like(dn_ref)

        cos_t = cos_ref[...]
        sin_t = sin_ref[...]
        dh = None
        for gi, (dil, part) in enumerate(zip(GROUP_DIL, (p1, p4, p16))):
            slab = slabs.at[gi]
            for j in range(N_SLABS):
                if dil == 1:
                    a = part[0, :, _lane_cols(j)].astype(F32)
                else:
                    _to_natural(part, j, slab, j, dil, tm)
                    a = slab[j]
                if j < 6:
                    a = a * cos_t - _rot_half(a * sin_t)
                if j < 3:
                    a = a * (HEAD_DIM ** -0.5)
                dq_ref[gi, :, _lane_cols(j)] = a.astype(BF16)
            contrib = _dot_nt(dq_ref[gi], w_ref[gi])
            dh = contrib if dh is None else dh + contrib
        dx, dn = _rms_bwd(dh, x_ref[...], g_ref[...])
        dx_ref[...] = dxo_ref[...] + dx
        dn_ref[...] += dn

    row_spec = pl.BlockSpec((tm, D_MODEL), lambda t: (t, 0))
    vec_spec = pl.BlockSpec((1, D_MODEL), lambda t: (0, 0))
    tab_spec = pl.BlockSpec((tm, LANE), lambda t: (t, 0))
    part_specs = [pl.BlockSpec((d, tm // d, QKV_PAD), lambda t: (0, t, 0)) for d in GROUP_DIL]
    return _pcall(
        body, name="qkv_bwd", grid=(t_len // tm,),
        in_specs=part_specs + [pl.BlockSpec((3, D_MODEL, QKV_PAD), lambda t: (0, 0, 0)),
                               row_spec, row_spec, vec_spec, tab_spec, tab_spec],
        out_specs=[pl.BlockSpec((3, tm, QKV_PAD), lambda t: (0, t, 0)), row_spec, vec_spec],
        out_shape=[_sds((3, t_len, QKV_PAD), BF16), _sds((t_len, D_MODEL), F32), _sds((1, D_MODEL), F32)],
        scratch_shapes=[pltpu.VMEM((3, N_SLABS, tm, LANE), F32)],
        semantics=("arbitrary",),
    )(*dqkv_parts, w_pad, dxo, x, g_row, cos, sin)


def final_fwd_bwd(x, g_row, target):
    t_len = x.shape[0]
    tm = _tile_rows(t_len)

    def body(x_ref, g_ref, tgt_ref, dx_ref, dn_ref, loss_ref):
        @pl.when(pl.program_id(0) == 0)
        def _():
            dn_ref[...] = jnp.zeros_like(dn_ref)
            loss_ref[...] = jnp.zeros_like(loss_ref)

        x_t = x_ref[...]
        g = g_ref[...]
        diff = _rms_fwd(x_t, g) - tgt_ref[...]
        loss_ref[...] += 0.5 * jnp.sum(jnp.mean(diff * diff, axis=-1, keepdims=True), axis=0, keepdims=True)
        dx, dn = _rms_bwd(diff * (1.0 / D_MODEL), x_t, g)
        dx_ref[...] = dx
        dn_ref[...] += dn

    row_spec = pl.BlockSpec((tm, D_MODEL), lambda t: (t, 0))
    vec_spec = pl.BlockSpec((1, D_MODEL), lambda t: (0, 0))
    return _pcall(
        body, name="final_fwd_bwd", grid=(t_len // tm,),
        in_specs=[row_spec, vec_spec, row_spec],
        out_specs=[row_spec, vec_spec, pl.BlockSpec((1, 1), lambda t: (0, 0))],
        out_shape=[_sds((t_len, D_MODEL), F32), _sds((1, D_MODEL), F32), _sds((1, 1), F32)],
        semantics=("arbitrary",),
    )(x, g_row, target)


def pool_bwd(dxo, x, g_row, w_in, w_grp, scale, w_out, zr, duty=None):
    t_len = x.shape[0]
    tm = _tile_rows(t_len)
    nt = t_len // tm

    def body(dxo_ref, x_ref, g_ref, win_ref, wgrp_ref, scale_ref, wout_ref, zr_ref,
             dzs_ref, du_ref, dx_ref, dn_ref, dsc_ref, ebuf):
        i = pl.program_id(0)
        t = nt - 1 - i

        @pl.when(i == 0)
        def _():
            ebuf[pl.ds(tm, POOL_HALO), :] = jnp.zeros((POOL_HALO, D_MODEL), F32)
            dn_ref[...] = jnp.zeros_like(dn_ref)
            dsc_ref[...] = jnp.zeros_like(dsc_ref)

        dxo_t = dxo_ref[...]
        dz = _dot_nt(dxo_t.astype(BF16), wout_ref[...])
        dsc_ref[...] += jnp.sum(dz * zr_ref[...].astype(F32), axis=0, keepdims=True)
        dzs_ref[...] = (dz * scale_ref[...]).astype(BF16)
        row = t * tm + lax.broadcasted_iota(jnp.int32, (tm, 1), 0)
        for gi, w in enumerate(POOL_WINDOWS):
            cols = pl.ds(gi * POOL_GROUP_DIM, POOL_GROUP_DIM)
            dp_g = _dot_nt(dzs_ref[:, cols], wgrp_ref[gi])
            inv_cnt = 1.0 / jnp.minimum(row + 1, w).astype(F32)
            ebuf[pl.ds(0, tm), cols] = dp_g * inv_cnt
            acc = -dp_g
            for j in range(w):
                acc = acc + ebuf[pl.ds(j, tm), cols]
            du_ref[:, cols] = acc.astype(BF16)
        ebuf[pl.ds(tm, POOL_HALO), :] = ebuf[pl.ds(0, POOL_HALO), :]
        dh = _dot_nt(du_ref[...], win_ref[...])
        dx, dn = _rms_bwd(dh, x_ref[...], g_ref[...])
        dx_ref[...] = dxo_t + dx
        dn_ref[...] += dn

    row_spec = pl.BlockSpec((tm, D_MODEL), lambda i: (nt - 1 - i, 0))
    full = lambda shape: pl.BlockSpec(shape, lambda i: (0,) * len(shape))
    vec = full((1, D_MODEL))
    return _pcall(
        body, name="pool_bwd", grid=(nt,),
        in_specs=[row_spec, row_spec, vec, full((D_MODEL, D_MODEL)), full((4, POOL_GROUP_DIM, POOL_GROUP_DIM)),
                  vec, full((D_MODEL, D_MODEL)), row_spec],
        out_specs=[row_spec, row_spec, row_spec, vec, vec],
        out_shape=[_sds((t_len, D_MODEL), BF16), _sds((t_len, D_MODEL), BF16), _sds((t_len, D_MODEL), F32),
                   _sds((1, D_MODEL), F32), _sds((1, D_MODEL), F32)],
        scratch_shapes=[pltpu.VMEM((tm + POOL_HALO, D_MODEL), F32)],
        semantics=("arbitrary",), duty=duty,
    )(dxo, x, g_row, w_in, w_grp, scale, w_out, zr)


def _mesh_pos():
    return lax.axis_index("x"), lax.axis_index("y"), lax.axis_index("c")


def _other_chips(x, y):
    return [(1 - x, y), (x, 1 - y), (1 - x, 1 - y)]


def _remote(src, dst, send_sem, recv_sem, device):
    return pltpu.make_async_remote_copy(src_ref=src, dst_ref=dst, send_sem=send_sem, recv_sem=recv_sem,
                                        device_id=device, device_id_type=MESH)


class _Duty:
    aliases = {}

    def mid(self, ins, outs, sems):
        pass


class Together(_Duty):
    def __init__(self, duties):
        self.duties = duties
        self.ins = [a for d in duties for a in d.ins]
        self.out_shape = [s for d in duties for s in d.out_shape]
        self.scratch = [s for d in duties for s in d.scratch]
        self.aliases = {}
        i0 = o0 = 0
        for d in duties:
            self.aliases.update({i0 + i: o0 + o for i, o in d.aliases.items()})
            i0 += len(d.ins)
            o0 += len(d.out_shape)

    def _each(self, ins, outs, sems):
        i0 = o0 = s0 = 0
        for d in self.duties:
            ni, no, ns = len(d.ins), len(d.out_shape), len(d.scratch)
            yield d, ins[i0:i0 + ni], outs[o0:o0 + no], sems[s0:s0 + ns]
            i0, o0, s0 = i0 + ni, o0 + no, s0 + ns

    def split(self, outs):
        return [list(o) for _, _, o, _ in self._each(self.ins, outs, self.scratch)]

    def start(self, ins, outs, sems):
        for d, i, o, s in self._each(ins, outs, sems):
            d.start(i, o, s)

    def mid(self, ins, outs, sems):
        for d, i, o, s in self._each(ins, outs, sems):
            d.mid(i, o, s)

    def finish(self, ins, outs, sems):
        for d, i, o, s in self._each(ins, outs, sems):
            d.finish(i, o, s)


def run_duty(duty, name):
    d_in, d_out = len(duty.ins), len(duty.out_shape)

    def body(*refs):
        ins, outs, sems = refs[:d_in], refs[d_in:d_in + d_out], refs[d_in + d_out:]
        duty.start(ins, outs, sems)
        duty.mid(ins, outs, sems)
        duty.finish(ins, outs, sems)

    return pl.pallas_call(
        body, name=name, out_shape=list(duty.out_shape), in_specs=[_ANY] * d_in, out_specs=[_ANY] * d_out,
        scratch_shapes=list(duty.scratch), input_output_aliases=dict(duty.aliases),
        compiler_params=pltpu.CompilerParams(has_side_effects=True),
    )(*duty.ins)


class GatherWeights(_Duty):
    def __init__(self, shards):
        n = self.n = len(shards)
        self.halves = [s.shape[0] // 2 for s in shards]
        my_slot = 2 * lax.axis_index("x") + lax.axis_index("y")
        staged = [lax.dynamic_update_slice(lax.empty((N_SHARDS,) + s.shape, s.dtype), s[None], (my_slot, 0, 0))
                  for s in shards]
        self.ins = list(shards) + staged
        self.out_shape = [_sds((N_SHARDS,) + s.shape, s.dtype) for s in shards]
        self.aliases = {n + a: a for a in range(n)}
        self.scratch = [pltpu.SemaphoreType.DMA((n, 6)), pltpu.SemaphoreType.DMA((n, 6))]

    def _over_ici(self, ins, outs, sems):
        x, y, c = _mesh_pos()
        return [_remote(ins[a].at[pl.ds(c * h, h)], outs[a].at[2 * x + y, pl.ds(c * h, h)],
                        sems[0].at[a, j], sems[1].at[a, j], (*chip, c))
                for a, h in enumerate(self.halves) for j, chip in enumerate(_other_chips(x, y))]

    def _forwards(self, outs, sems, half_of):
        x, y, c = _mesh_pos()
        cps = []
        for a, h in enumerate(self.halves):
            for j, chip in enumerate(_other_chips(x, y)):
                slot = outs[a].at[2 * chip[0] + chip[1], pl.ds(half_of(c) * h, h)]
                cps.append(_remote(slot, slot, sems[0].at[a, 3 + j], sems[1].at[a, 3 + j], (x, y, 1 - c)))
        return cps

    def start(self, ins, outs, sems):
        for cp in self._over_ici(ins, outs, sems):
            cp.start()

    def mid(self, ins, outs, sems):
        x, y, c = _mesh_pos()
        forwards = self._forwards(outs, sems, lambda core: core)
        k = 0
        for a, h in enumerate(self.halves):
            for j, chip in enumerate(_other_chips(x, y)):
                slot = outs[a].at[2 * chip[0] + chip[1], pl.ds(c * h, h)]
                _remote(slot, slot, sems[0].at[a, j], sems[1].at[a, j], (*chip, c)).wait_recv()
                forwards[k].start()
                k += 1

    def finish(self, ins, outs, sems):
        for cp in self._forwards(outs, sems, lambda core: 1 - core):
            cp.wait_recv()
        for cp in self._over_ici(ins, outs, sems) + self._forwards(outs, sems, lambda core: core):
            cp.wait_send()


class GradReducer:
    def __init__(self, c_idx, pos_idx):
        self.c_idx, self.pos_idx = c_idx, pos_idx
        self.in_flight = []
        self.done = {}

    def push(self, name, grad):
        self.in_flight.append(dict(name=name, stage="halves", data=grad))

    def _duties(self):
        make = {"halves": SiblingHalves, "exchange": ChipExchange, "share": SiblingShare}
        return Together([make[w["stage"]]([w["data"]]) for w in self.in_flight])

    def _advance(self, duties, outs):
        still = []
        for w, (res,) in zip(self.in_flight, duties.split(outs)):
            if w["stage"] == "halves":
                partial = add_my_half(w["data"], res, self.c_idx, f"rs_add_{w['name']}")
                still.append(dict(name=w["name"], stage="exchange", data=partial))
            elif w["stage"] == "exchange":
                reduced = sum_slots(res, w["data"], self.pos_idx, f"rs_sum_{w['name']}")
                still.append(dict(name=w["name"], stage="share", data=reduced))
            else:
                self.done[w["name"]] = res
        self.in_flight = still

    def carried_by(self, fn, *args, **kw):
        if not self.in_flight:
            return fn(*args, **kw)
        duties = self._duties()
        out, duty_outs = fn(*args, duty=duties, **kw)
        self._advance(duties, duty_outs)
        return out

    def drain(self, name):
        step = 0
        while self.in_flight:
            duties = self._duties()
            self._advance(duties, run_duty(duties, f"{name}{step}"))
            step += 1


class SiblingHalves(_Duty):
    def __init__(self, grads):
        n = len(grads)
        self.halves = [g.shape[1] // 2 for g in grads]
        self.ins = list(grads)
        self.out_shape = [_sds((N_SHARDS, h, g.shape[2]), g.dtype) for g, h in zip(grads, self.halves)]
        self.scratch = [pltpu.SemaphoreType.DMA((n,)), pltpu.SemaphoreType.DMA((n,))]

    def _copies(self, ins, outs, sems):
        x, y, c = _mesh_pos()
        return [_remote(ins[a].at[:, pl.ds((1 - c) * h, h)], outs[a], sems[0].at[a], sems[1].at[a], (x, y, 1 - c))
                for a, h in enumerate(self.halves)]

    def start(self, ins, outs, sems):
        for cp in self._copies(ins, outs, sems):
            cp.start()

    def finish(self, ins, outs, sems):
        for cp in self._copies(ins, outs, sems):
            cp.wait()


class ChipExchange(_Duty):
    def __init__(self, parts):
        n = self.n = len(parts)
        self.ins = list(parts)
        self.out_shape = [_sds(p.shape, p.dtype) for p in parts]
        self.scratch = [pltpu.SemaphoreType.DMA((n, 3)), pltpu.SemaphoreType.DMA((n, 3))]

    def _copies(self, ins, outs, sems, arriving):
        x, y, c = _mesh_pos()
        cps = []
        for a in range(self.n):
            for j, chip in enumerate(_other_chips(x, y)):
                theirs = 2 * chip[0] + chip[1]
                src = outs[a].at[theirs] if arriving else ins[a].at[theirs]
                dst = outs[a].at[theirs] if arriving else outs[a].at[2 * x + y]
                cps.append(_remote(src, dst, sems[0].at[a, j], sems[1].at[a, j], (*chip, c)))
        return cps

    def start(self, ins, outs, sems):
        for cp in self._copies(ins, outs, sems, False):
            cp.start()

    def finish(self, ins, outs, sems):
        for cp in self._copies(ins, outs, sems, True):
            cp.wait_recv()
        for cp in self._copies(ins, outs, sems, False):
            cp.wait_send()


class SiblingShare(_Duty):
    def __init__(self, reduced):
        n = self.n = len(reduced)
        self.ins = list(reduced)
        self.out_shape = [_sds(r.shape, r.dtype) for r in reduced]
        self.aliases = {a: a for a in range(n)}
        self.scratch = [pltpu.SemaphoreType.DMA((n,)), pltpu.SemaphoreType.DMA((n,))]

    def _copies(self, outs, sems, half_of):
        x, y, c = _mesh_pos()
        cps = []
        for a in range(self.n):
            h = outs[a].shape[0] // 2
            rows = outs[a].at[pl.ds(half_of(c) * h, h)]
            cps.append(_remote(rows, rows, sems[0].at[a], sems[1].at[a], (x, y, 1 - c)))
        return cps

    def start(self, ins, outs, sems):
        for cp in self._copies(outs, sems, lambda core: core):
            cp.start()

    def finish(self, ins, outs, sems):
        for cp in self._copies(outs, sems, lambda core: 1 - core):
            cp.wait_recv()
        for cp in self._copies(outs, sems, lambda core: core):
            cp.wait_send()


def allreduce_small(v):
    def body(v_ref, o_ref, buf, send_sems, recv_sems):
        x, y, c = _mesh_pos()
        me = 4 * x + 2 * y + c
        buf[me] = v_ref[...]
        flip = lambda p, f: 1 - p if f else p
        peers = [(flip(x, k & 4), flip(y, k & 2), flip(c, k & 1)) for k in range(1, N_DEV)]
        cps = []
        for k, peer in enumerate(peers):
            cp = _remote(v_ref, buf.at[me], send_sems.at[k], recv_sems.at[k], peer)
            cp.start()
            cps.append(cp)
        for k, peer in enumerate(peers):
            slot = buf.at[4 * peer[0] + 2 * peer[1] + peer[2]]
            _remote(slot, slot, send_sems.at[k], recv_sems.at[k], peer).wait_recv()
        for cp in cps:
            cp.wait_send()
        acc = buf[0]
        for i in range(1, N_DEV):
            acc = acc + buf[i]
        o_ref[...] = acc

    vm = pl.BlockSpec(memory_space=pltpu.VMEM)
    return pl.pallas_call(
        body, name="allreduce_small", out_shape=_sds(v.shape, v.dtype), in_specs=[vm], out_specs=vm,
        scratch_shapes=[pltpu.VMEM((N_DEV,) + v.shape, v.dtype),
                        pltpu.SemaphoreType.DMA((N_DEV - 1,)), pltpu.SemaphoreType.DMA((N_DEV - 1,))],
        compiler_params=pltpu.CompilerParams(has_side_effects=True),
    )(v)


def add_my_half(grad, theirs, c_idx, name):
    _, r, cols = grad.shape
    h = r // 2

    def body(c_ref, g_ref, t_ref, o_ref):
        o_ref[...] = (g_ref[...] + t_ref[...]).astype(BF16)

    slot = pl.BlockSpec((None, h, cols), lambda s, c: (s, 0, 0))
    grid_spec = pltpu.PrefetchScalarGridSpec(
        num_scalar_prefetch=1, grid=(N_SHARDS,),
        in_specs=[pl.BlockSpec((None, h, cols), lambda s, c: (s, c[0], 0)), slot], out_specs=slot)
    return pl.pallas_call(
        body, name=name, grid_spec=grid_spec, out_shape=_sds((N_SHARDS, h, cols), BF16),
        compiler_params=pltpu.CompilerParams(dimension_semantics=("arbitrary",), vmem_limit_bytes=VMEM_LIMIT_BYTES),
    )(c_idx, grad, theirs)


def sum_slots(received, mine, pos_idx, name):
    _, h, cols = received.shape

    def body(pos_ref, r_ref, m_ref, o_ref):
        acc = None
        for k in range(N_SHARDS):
            term = jnp.where(pos_ref[0] == k, m_ref[k], r_ref[k]).astype(F32)
            acc = term if acc is None else acc + term
        o_ref[...] = acc

    whole = pl.BlockSpec((N_SHARDS, h, cols), lambda i, pos: (0, 0, 0))
    grid_spec = pltpu.PrefetchScalarGridSpec(
        num_scalar_prefetch=1, grid=(1,), in_specs=[whole, whole],
        out_specs=pl.BlockSpec((h, cols), lambda i, pos: (pos[1], 0)))
    return pl.pallas_call(
        body, name=name, grid_spec=grid_spec, out_shape=_sds((2 * h, cols), F32),
        compiler_params=pltpu.CompilerParams(dimension_semantics=("arbitrary",), vmem_limit_bytes=VMEM_LIMIT_BYTES),
    )(pos_idx, received, mine)


def adamw(name, grads, w, m, v):
    n_layers, r, cols = w.shape
    tr = r // 2 if r % 16 == 0 else r
    bias1 = 1.0 - ADAM_B1 ** ADAM_STEP
    bias2 = 1.0 - ADAM_B2 ** ADAM_STEP

    def body(*refs):
        g_refs = refs[:n_layers]
        w_ref, m_ref, v_ref, go_ref, d_ref, mo_ref, vo_ref = refs[n_layers:]
        g = g_refs[0][...]
        for layer in range(1, n_layers):
            g = jnp.where(pl.program_id(0) == layer, g_refs[layer][...], g)
        m_new = ADAM_B1 * m_ref[...] + (1.0 - ADAM_B1) * g
        v_new = ADAM_B2 * v_ref[...] + (1.0 - ADAM_B2) * (g * g)
        m_hat = m_new / bias1
        v_hat = v_new / bias2
        go_ref[...] = g
        d_ref[...] = -ADAM_LR * (m_hat / (jnp.sqrt(v_hat) + ADAM_EPS) + ADAM_WD * w_ref[...])
        mo_ref[...] = m_new
        vo_ref[...] = v_new

    g_spec = pl.BlockSpec((tr, cols), lambda l, i: (i, 0))
    lay_spec = pl.BlockSpec((None, tr, cols), lambda l, i: (l, i, 0))
    shape = _sds((n_layers, r, cols), F32)
    return _pcall(
        body, name=name, grid=(n_layers, r // tr),
        in_specs=[g_spec] * n_layers + [lay_spec] * 3, out_specs=[lay_spec] * 4,
        out_shape=[shape] * 4, semantics=("arbitrary", "arbitrary"),
    )(*grads, w, m, v)


def kernel(x, norm_mix, norm_ffn, norm_final, pool_w_in, pool_w_group, pool_scale, pool_w_out, attn_w_qkv, attn_w_out, ffn_w_gate, ffn_w_up, ffn_w_down, loss_target, m_norm_mix, m_norm_ffn, m_norm_final, m_pool_w_in, m_pool_w_group, m_pool_scale, m_pool_w_out, m_attn_w_qkv, m_attn_w_out, m_ffn_w_gate, m_ffn_w_up, m_ffn_w_down, v_norm_mix, v_norm_ffn, v_norm_final, v_pool_w_in, v_pool_w_group, v_pool_scale, v_pool_w_out, v_attn_w_qkv, v_attn_w_out, v_ffn_w_gate, v_ffn_w_up, v_ffn_w_down):
    t_len = x.shape[1]
    x0 = x.reshape(t_len, D_MODEL)
    target = loss_target.reshape(t_len, D_MODEL)
    row = lambda a: a.reshape(1, D_MODEL)

    grp_rows = POOL_GROUP_DIM // N_SHARDS
    bf = lambda a: a.astype(BF16)
    gate_t, up_t = jnp.swapaxes(ffn_w_gate, 1, 2), jnp.swapaxes(ffn_w_up, 1, 2)
    pool_shards = [bf(pool_w_in[0]), bf(pool_w_group[0].reshape(4 * grp_rows, POOL_GROUP_DIM)), bf(pool_w_out[0])]
    ffn0_shards = [bf(gate_t[0]), bf(up_t[0]), bf(ffn_w_down[0])]
    late_shards = [bf(attn_w_qkv[0]), bf(attn_w_out[0]), bf(gate_t[1]), bf(up_t[1]), bf(ffn_w_down[1])]
    cos, sin = rope_tables(t_len)
    c_idx = lax.axis_index("c").astype(jnp.int32).reshape(1)
    pos_idx = jnp.stack([2 * lax.axis_index("x") + lax.axis_index("y"), lax.axis_index("c")]).astype(jnp.int32)
    chip_rows = lambda g: g.reshape(N_SHARDS, D_MODEL // N_SHARDS, D_MODEL)

    g_pool = run_duty(GatherWeights(pool_shards), "gather_pool")
    w_in = g_pool[0].reshape(D_MODEL, D_MODEL)
    w_grp = g_pool[1].reshape(N_SHARDS, 4, grp_rows, POOL_GROUP_DIM).transpose(1, 0, 2, 3).reshape(
        4, POOL_GROUP_DIM, POOL_GROUP_DIM)
    w_out = g_pool[2].reshape(D_MODEL, D_MODEL)
    (h0, p, zr, z, x1), ffn0 = pool_fwd(x0, row(norm_mix[0]), w_in, w_grp, pool_scale, w_out,
                                        duty=GatherWeights(ffn0_shards))
    (h1, gate0, up0, act0, x2), late = ffn_fwd(x1, row(norm_ffn[0]), *ffn0, "ffn_fwd0",
                                               duty=GatherWeights(late_shards))
    w_qkv = pad_qkv_weight(late[0])
    w_ao = jnp.concatenate(pad_groups(late[1].reshape(D_MODEL, D_MODEL), 0), axis=0)
    ffn1 = late[2:5]
    h2, *qkv_parts = qkv_fwd(x2, row(norm_mix[1]), w_qkv, cos, sin)
    o_parts, lse_parts = [], []
    for gi in range(3):
        o_g, lse_g = attn_fwd(qkv_parts[gi], gi, f"attn_fwd_g{gi}")
        o_parts.append(o_g)
        lse_parts.append(lse_g)
    x3, merged, o_nat, lse_nat = attn_out_fwd(x2, o_parts, lse_parts, w_ao)
    h3, gate1, up1, act1, x4 = ffn_fwd(x3, row(norm_ffn[1]), *ffn1, "ffn_fwd1")
    dx4, d_norm_final, loss_local = final_fwd_bwd(x4, row(norm_final), target)

    red = GradReducer(c_idx, pos_idx)
    dgate1, dup1, dx3, d_nf1 = ffn_bwd(dx4, x3, row(norm_ffn[1]), gate1, up1, *ffn1, "ffn_bwd1")
    g_gate1 = wgrad_row_sharded("wgrad_gate1", dgate1, h3)
    g_up1 = wgrad_row_sharded("wgrad_up1", dup1, h3)
    g_down1 = wgrad_row_sharded("wgrad_down1", act1, dx4)
    red.push("gate1", g_gate1)
    red.push("up1", g_up1)
    do_parts, c_parts = red.carried_by(attn_out_bwd, dx3, w_ao, o_nat, lse_nat)
    g_ao = wgrad_full("wgrad_attn_out", merged, dx3)
    red.push("down1", g_down1)
    red.push("attn_out", chip_rows(unpad_groups(jnp.split(g_ao, 3, axis=0), 0)))
    dqkv_parts = [red.carried_by(attn_bwd, qkv_parts[gi], do_parts[gi], lse_parts[gi], c_parts[gi], gi,
                                 f"attn_bwd_g{gi}") for gi in range(3)]
    dqkv, dx2, d_nm1 = qkv_bwd(dqkv_parts, w_qkv, dx3, x2, row(norm_mix[1]), cos, sin)
    red.push("qkv", unpad_qkv_grad(wgrad_col_sharded("wgrad_qkv", h2, dqkv)))

    dgate0, dup0, dx1, d_nf0 = red.carried_by(ffn_bwd, dx2, x1, row(norm_ffn[0]), gate0, up0, *ffn0, "ffn_bwd0")
    red.push("gate0", red.carried_by(wgrad_row_sharded, "wgrad_gate0", dgate0, h1))
    red.push("up0", red.carried_by(wgrad_row_sharded, "wgrad_up0", dup0, h1))
    red.push("down0", red.carried_by(wgrad_row_sharded, "wgrad_down0", act0, dx2))
    red.push("pool_out", chip_rows(red.carried_by(wgrad_full, "wgrad_pool_out", z, dx1)))
    dzs, du, dx0, d_nm0, d_scale = pool_bwd(dx1, x0, row(norm_mix[0]), w_in, w_grp, pool_scale, w_out, zr)
    g_grp = red.carried_by(wgrad_pool_group, "wgrad_pool_group", p, dzs)
    red.push("pool_group", g_grp.reshape(N_SHARDS, 4 * grp_rows, POOL_GROUP_DIM))
    red.push("pool_in", chip_rows(red.carried_by(wgrad_full, "wgrad_pool_in", h0, du)))
    red.drain("rs_tail")
    full = [red.done[nm] for nm in ("pool_in", "pool_group", "pool_out", "qkv", "attn_out",
                                    "gate0", "gate1", "up0", "up1", "down0", "down1")]

    zero_row = jnp.zeros((1, D_MODEL), F32)
    small = jnp.concatenate([d_nm0, d_nm1, d_nf0, d_nf1, d_norm_final, d_scale,
                             jnp.broadcast_to(loss_local, (1, D_MODEL)), zero_row], axis=0)
    small = allreduce_small(small)
    loss = small[6, 0]

    pack = lambda a, b, c, d: jnp.concatenate([a, b, row(c), d, zero_row, zero_row], axis=0)[None]
    sg, sd, sm, sv = adamw("adamw_small", [small],
                           pack(norm_mix, norm_ffn, norm_final, pool_scale),
                           pack(m_norm_mix, m_norm_ffn, m_norm_final, m_pool_scale),
                           pack(v_norm_mix, v_norm_ffn, v_norm_final, v_pool_scale))
    unpack = lambda a: (a[0, 0:2], a[0, 2:4], a[0, 4], a[0, 5:6])

    def update(name, grads, w, m, v, transposed=False):
        if transposed:
            w, m, v = (jnp.swapaxes(a, 1, 2) for a in (w, m, v))
        n_layers = len(grads)
        shp = (n_layers,) + grads[0].shape
        outs = [o.reshape(w.shape) for o in adamw(name, grads, w.reshape(shp), m.reshape(shp), v.reshape(shp))]
        return [jnp.swapaxes(o, 1, 2) for o in outs] if transposed else outs

    big = [
        update("adamw_pool_in", [full[0]], pool_w_in, m_pool_w_in, v_pool_w_in),
        update("adamw_pool_group", [full[1]], pool_w_group, m_pool_w_group, v_pool_w_group),
        update("adamw_pool_out", [full[2]], pool_w_out, m_pool_w_out, v_pool_w_out),
        update("adamw_qkv", [full[3]], attn_w_qkv, m_attn_w_qkv, v_attn_w_qkv),
        update("adamw_attn_out", [full[4]], attn_w_out, m_attn_w_out, v_attn_w_out),
        update("adamw_gate", [full[5], full[6]], ffn_w_gate, m_ffn_w_gate, v_ffn_w_gate, transposed=True),
        update("adamw_up", [full[7], full[8]], ffn_w_up, m_ffn_w_up, v_ffn_w_up, transposed=True),
        update("adamw_down", [full[9], full[10]], ffn_w_down, m_ffn_w_down, v_ffn_w_down),
    ]

    def leaves(k, small_vals):
        nm, nf, nfin, psc = unpack(small_vals)
        return [nm, nf, nfin, big[0][k], big[1][k], psc, big[2][k], big[3][k], big[4][k],
                big[5][k], big[6][k], big[7][k]]

    grad_x = dx0.reshape(x.shape)
    return (loss, grad_x, *leaves(0, sg), *leaves(1, sd), *leaves(2, sm), *leaves(3, sv))
```

```python
import math

import jax
import jax.numpy as jnp
from jax import lax
from jax.experimental import pallas as pl
from jax.experimental.pallas import tpu as pltpu

F32 = jnp.float32
BF16 = jnp.bfloat16

D_MODEL = 1024
N_SHARDS = 4
N_DEV = 8
D_FF = 2816
FF_SHARD = D_FF // N_SHARDS
HEAD_DIM = 64
QKV_SHARD = 3 * D_MODEL // N_SHARDS
POOL_WINDOWS = (2, 4, 8, 16)
POOL_GROUP_DIM = 256
POOL_HALO = 16
ATTN_W = 128
GROUP_LANES = (0, 384, 704, 1024)
GROUP_HEADS = (6, 5, 5)
GROUP_DIL = (1, 4, 16)
ROPE_THETA = 10000.0
EPS = 1e-6
NEG_INF = -1e30
LANE = 128
VMEM_LIMIT_BYTES = 60 * 1024 * 1024

ADAM_LR = 0.001
ADAM_B1 = 0.9
ADAM_B2 = 0.999
ADAM_EPS = 1e-08
ADAM_WD = 0.01
ADAM_STEP = 10

NT_DIMS = (((1,), (1,)), ((), ()))
TN_DIMS = (((0,), (0,)), ((), ()))
MESH = pl.DeviceIdType.MESH


_ANY = pl.BlockSpec(memory_space=pl.ANY)


def _pcall(body, *, name, out_shape, grid=None, in_specs=None, out_specs=None, scratch_shapes=(),
           semantics=None, duty=None):
    kw = {}
    if in_specs is not None and duty is None:
        kw["in_specs"] = in_specs
    if out_specs is not None and duty is None:
        kw["out_specs"] = out_specs
    if grid is not None:
        kw["grid"] = grid
    params = dict(dimension_semantics=semantics, vmem_limit_bytes=VMEM_LIMIT_BYTES)
    if duty is None:
        return pl.pallas_call(body, name=name, out_shape=out_shape, scratch_shapes=list(scratch_shapes),
                              compiler_params=pltpu.CompilerParams(**params), **kw)

    single = not isinstance(out_shape, (list, tuple))
    c_out_shape = [out_shape] if single else list(out_shape)
    c_out_specs = [out_specs] if single else list(out_specs)
    n_in, n_out, n_scr = len(in_specs), len(c_out_shape), len(scratch_shapes)
    d_in, d_out = len(duty.ins), len(duty.out_shape)
    total = math.prod(grid)
    mid_step = (5 * total) // 6

    def wrapped(*refs):
        c_in, d_ins = refs[:n_in], refs[n_in:n_in + d_in]
        o0 = n_in + d_in
        c_outs, d_outs = refs[o0:o0 + n_out], refs[o0 + n_out:o0 + n_out + d_out]
        s0 = o0 + n_out + d_out
        c_scr, d_sems = refs[s0:s0 + n_scr], refs[s0 + n_scr:]
        step = pl.program_id(0)
        for ax in range(1, len(grid)):
            step = step * grid[ax] + pl.program_id(ax)

        @pl.when(step == 0)
        def _():
            duty.start(d_ins, d_outs, d_sems)

        body(*c_in, *c_outs, *c_scr)

        @pl.when(step == mid_step)
        def _():
            duty.mid(d_ins, d_outs, d_sems)

        @pl.when(step == total - 1)
        def _():
            duty.finish(d_ins, d_outs, d_sems)

    call = pl.pallas_call(
        wrapped, name=name, grid=grid,
        in_specs=list(in_specs) + [_ANY] * d_in, out_specs=c_out_specs + [_ANY] * d_out,
        out_shape=c_out_shape + list(duty.out_shape),
        scratch_shapes=list(scratch_shapes) + list(duty.scratch),
        input_output_aliases={n_in + i: n_out + o for i, o in duty.aliases.items()},
        compiler_params=pltpu.CompilerParams(has_side_effects=True, **params))

    def run(*args):
        outs = call(*args, *duty.ins)
        c = outs[:n_out]
        return (c[0] if single else list(c)), list(outs[n_out:])

    return run


def _sds(shape, dtype):
    return jax.ShapeDtypeStruct(tuple(shape), dtype)


def _dot(a, b):
    return jnp.dot(a, b, preferred_element_type=F32)


def _dot_nt(a, b):
    return lax.dot_general(a, b, NT_DIMS, preferred_element_type=F32)


def _dot_tn(a, b):
    return lax.dot_general(a, b, TN_DIMS, preferred_element_type=F32)


def _rms_fwd(x, g):
    r = lax.rsqrt(jnp.mean(x * x, axis=-1, keepdims=True) + EPS)
    return x * r * g


def _rms_bwd(dh, x, g):
    r = lax.rsqrt(jnp.mean(x * x, axis=-1, keepdims=True) + EPS)
    xh = x * r
    dg = jnp.sum(dh * xh, axis=0, keepdims=True)
    dxh = dh * g
    dx = r * (dxh - xh * jnp.mean(dxh * xh, axis=-1, keepdims=True))
    return dx, dg


def _sigmoid(x):
    return 0.5 * jnp.tanh(0.5 * x) + 0.5


def _tile_rows(t):
    return min(512, t)


def _sub_tiles(tm, n_sub=2):
    rows = tm // n_sub
    return [pl.ds(i * rows, rows) for i in range(n_sub)]


def _wgrad_rows(t):
    return min(2048, t)


def pool_fwd(x, g_row, w_in, w_grp, scale, w_out, duty=None):
    t_len = x.shape[0]
    tm = _tile_rows(t_len)

    def body(x_ref, g_ref, win_ref, wgrp_ref, scale_ref, wout_ref,
             h_ref, p_ref, zr_ref, z_ref, xo_ref, ubuf):
        t = pl.program_id(0)

        @pl.when(t == 0)
        def _():
            ubuf[pl.ds(0, POOL_HALO), :] = jnp.zeros((POOL_HALO, D_MODEL), F32)

        x_t = x_ref[...]
        h = _rms_fwd(x_t, g_ref[...]).astype(BF16)
        h_ref[...] = h
        ubuf[pl.ds(POOL_HALO, tm), :] = _dot(h, win_ref[...])
        row = t * tm + lax.broadcasted_iota(jnp.int32, (tm, 1), 0)
        for gi, w in enumerate(POOL_WINDOWS):
            cols = pl.ds(gi * POOL_GROUP_DIM, POOL_GROUP_DIM)
            u_g = ubuf[pl.ds(POOL_HALO, tm), cols]
            acc = u_g
            for j in range(1, w):
                acc = acc + ubuf[pl.ds(POOL_HALO - j, tm), cols]
            inv_cnt = 1.0 / jnp.minimum(row + 1, w).astype(F32)
            p_g = (acc * inv_cnt - u_g).astype(BF16)
            p_ref[:, cols] = p_g
            z_g = _dot(p_g, wgrp_ref[gi])
            zr_ref[:, cols] = z_g.astype(BF16)
            z_ref[:, cols] = (z_g * scale_ref[:, cols]).astype(BF16)
        ubuf[pl.ds(0, POOL_HALO), :] = ubuf[pl.ds(tm, POOL_HALO), :]
        xo_ref[...] = x_t + _dot(z_ref[...], wout_ref[...])

    row_spec = pl.BlockSpec((tm, D_MODEL), lambda t: (t, 0))
    full2 = lambda shape: pl.BlockSpec(shape, lambda t: (0,) * len(shape))
    return _pcall(
        body, name="pool_fwd", grid=(t_len // tm,),
        in_specs=[row_spec, full2((1, D_MODEL)), full2((D_MODEL, D_MODEL)),
                  full2((4, POOL_GROUP_DIM, POOL_GROUP_DIM)), full2((1, D_MODEL)), full2((D_MODEL, D_MODEL))],
        out_specs=[row_spec] * 5,
        out_shape=[_sds((t_len, D_MODEL), BF16)] * 4 + [_sds((t_len, D_MODEL), F32)],
        scratch_shapes=[pltpu.VMEM((tm + POOL_HALO, D_MODEL), F32)],
        semantics=("arbitrary",), duty=duty,
    )(x, g_row, w_in, w_grp, scale, w_out)


def ffn_fwd(x, g_row, w_gate_t, w_up_t, w_down, name, duty=None):
    t_len = x.shape[0]
    tm = min(1024, t_len)

    def body(x_ref, g_ref, wg_ref, wu_ref, wd_ref, h_ref, go_ref, uo_ref, ao_ref, xo_ref, hbuf, acc):
        s = pl.program_id(1)

        @pl.when(s == 0)
        def _():
            h = _rms_fwd(x_ref[...], g_ref[...]).astype(BF16)
            hbuf[...] = h
            h_ref[...] = h
            acc[...] = jnp.zeros_like(acc)

        h = hbuf[...]
        gate = _dot_nt(h, wg_ref[...])
        up = _dot_nt(h, wu_ref[...])
        go_ref[...] = gate.astype(BF16)
        uo_ref[...] = up.astype(BF16)
        act = (gate * _sigmoid(gate) * up).astype(BF16)
        ao_ref[...] = act
        acc[...] += _dot(act, wd_ref[...])

        @pl.when(s == N_SHARDS - 1)
        def _():
            xo_ref[...] = x_ref[...] + acc[...]

    row_spec = pl.BlockSpec((tm, D_MODEL), lambda t, s: (t, 0))
    row_w = pl.BlockSpec((None, FF_SHARD, D_MODEL), lambda t, s: (s, 0, 0))
    act_spec = pl.BlockSpec((None, tm, FF_SHARD), lambda t, s: (s, t, 0))
    return _pcall(
        body, name=name, grid=(t_len // tm, N_SHARDS),
        in_specs=[row_spec, pl.BlockSpec((1, D_MODEL), lambda t, s: (0, 0)), row_w, row_w, row_w],
        out_specs=[row_spec, act_spec, act_spec, act_spec, row_spec],
        out_shape=[_sds((t_len, D_MODEL), BF16)] + [_sds((N_SHARDS, t_len, FF_SHARD), BF16)] * 3
                  + [_sds((t_len, D_MODEL), F32)],
        scratch_shapes=[pltpu.VMEM((tm, D_MODEL), BF16), pltpu.VMEM((tm, D_MODEL), F32)],
        semantics=("arbitrary", "arbitrary"), duty=duty,
    )(x, g_row, w_gate_t, w_up_t, w_down)


def ffn_bwd(dxo, x, g_row, gate, up, w_gate_t, w_up_t, w_down, name, duty=None):
    t_len = x.shape[0]
    tm = _tile_rows(t_len)

    def body(dxo_ref, x_ref, g_ref, gate_ref, up_ref, wg_ref, wu_ref, wd_ref,
             dg_ref, du_ref, dx_ref, dn_ref, dxb, dh):
        t = pl.program_id(0)
        s = pl.program_id(1)

        @pl.when(s == 0)
        def _():
            dxb[...] = dxo_ref[...].astype(BF16)
            dh[...] = jnp.zeros_like(dh)

        @pl.when(jnp.logical_and(s == 0, t == 0))
        def _():
            dn_ref[...] = jnp.zeros_like(dn_ref)

        sub_tiles = _sub_tiles(tm)
        dacts = [_dot_nt(dxb[rows, :], wd_ref[...]) for rows in sub_tiles]
        for rows, dact in zip(sub_tiles, dacts):
            gv = gate_ref[rows, :].astype(F32)
            uv = up_ref[rows, :].astype(F32)
            sg = _sigmoid(gv)
            dgv = (dact * uv * (sg * (1.0 + gv * (1.0 - sg)))).astype(BF16)
            duv = (dact * (gv * sg)).astype(BF16)
            dg_ref[rows, :] = dgv
            du_ref[rows, :] = duv
            dh[rows, :] += _dot(dgv, wg_ref[...]) + _dot(duv, wu_ref[...])

        @pl.when(s == N_SHARDS - 1)
        def _():
            dx, dn = _rms_bwd(dh[...], x_ref[...], g_ref[...])
            dx_ref[...] = dxo_ref[...] + dx
            dn_ref[...] += dn

    row_spec = pl.BlockSpec((tm, D_MODEL), lambda t, s: (t, 0))
    vec_spec = pl.BlockSpec((1, D_MODEL), lambda t, s: (0, 0))
    row_w = pl.BlockSpec((None, FF_SHARD, D_MODEL), lambda t, s: (s, 0, 0))
    act_spec = pl.BlockSpec((None, tm, FF_SHARD), lambda t, s: (s, t, 0))
    act_shape = _sds((N_SHARDS, t_len, FF_SHARD), BF16)
    return _pcall(
        body, name=name, grid=(t_len // tm, N_SHARDS),
        in_specs=[row_spec, row_spec, vec_spec, act_spec, act_spec, row_w, row_w, row_w],
        out_specs=[act_spec, act_spec, row_spec, vec_spec],
        out_shape=[act_shape, act_shape, _sds((t_len, D_MODEL), F32), _sds((1, D_MODEL), F32)],
        scratch_shapes=[pltpu.VMEM((tm, D_MODEL), BF16), pltpu.VMEM((tm, D_MODEL), F32)],
        semantics=("arbitrary", "arbitrary"), duty=duty,
    )(dxo, x, g_row, gate, up, w_gate_t, w_up_t, w_down)


def tn_matmul(name, a, b, a_spec, b_spec, out_shape, out_spec, grid, duty=None):
    def body(a_ref, b_ref, o_ref):
        @pl.when(pl.program_id(len(grid) - 1) == 0)
        def _():
            o_ref[...] = jnp.zeros_like(o_ref)

        res = _dot_tn(a_ref[...].astype(BF16), b_ref[...].astype(BF16))
        o_ref[...] += res.reshape(o_ref.shape)

    return _pcall(body, name=name, grid=grid, in_specs=[a_spec, b_spec], out_specs=out_spec,
                  out_shape=out_shape, semantics=("arbitrary",) * len(grid), duty=duty)(a, b)


def wgrad_full(name, a, b, duty=None):
    t_len, k = a.shape
    n = b.shape[1]
    tt = _wgrad_rows(t_len)
    return tn_matmul(name, a, b,
                     pl.BlockSpec((tt, k), lambda t: (t, 0)), pl.BlockSpec((tt, n), lambda t: (t, 0)),
                     _sds((k, n), F32), pl.BlockSpec((k, n), lambda t: (0, 0)), (t_len // tt,), duty)


def wgrad_col_sharded(name, a, b_sh, duty=None):
    t_len, k = a.shape
    n_sh, _, n = b_sh.shape
    tt = _wgrad_rows(t_len)
    return tn_matmul(name, a, b_sh,
                     pl.BlockSpec((tt, k), lambda s, t: (t, 0)), pl.BlockSpec((None, tt, n), lambda s, t: (s, t, 0)),
                     _sds((n_sh, k, n), F32), pl.BlockSpec((None, k, n), lambda s, t: (s, 0, 0)),
                     (n_sh, t_len // tt), duty)


def wgrad_row_sharded(name, a_sh, b, duty=None):
    t_len, n = b.shape
    n_sh, _, k = a_sh.shape
    tt = _wgrad_rows(t_len)

    def body(a_ref, b_ref, o_ref):
        s = pl.program_id(1)
        res = _dot_tn(a_ref[...], b_ref[...].astype(BF16))

        @pl.when(pl.program_id(0) == 0)
        def _():
            o_ref[s] = res

        @pl.when(pl.program_id(0) > 0)
        def _():
            o_ref[s] += res

    return _pcall(body, name=name, grid=(t_len // tt, n_sh),
                  in_specs=[pl.BlockSpec((None, tt, k), lambda t, s: (s, t, 0)),
                            pl.BlockSpec((tt, n), lambda t, s: (t, 0))],
                  out_specs=pl.BlockSpec((n_sh, k, n), lambda t, s: (0, 0, 0)),
                  out_shape=_sds((n_sh, k, n), F32), semantics=("arbitrary", "arbitrary"), duty=duty)(a_sh, b)


def wgrad_pool_group(name, p, dzs, duty=None):
    t_len = p.shape[0]
    tt = _wgrad_rows(t_len)
    gd = POOL_GROUP_DIM
    rows = gd // N_SHARDS
    return tn_matmul(name, p, dzs,
                     pl.BlockSpec((tt, gd), lambda g, t: (t, g)), pl.BlockSpec((tt, gd), lambda g, t: (t, g)),
                     _sds((N_SHARDS, 4, rows, gd), F32),
                     pl.BlockSpec((N_SHARDS, None, rows, gd), lambda g, t: (0, g, 0, 0)),
                     (4, t_len // tt), duty)


PAD_LANES = 384
QKV_PAD = 3 * PAD_LANES
N_SLABS = QKV_PAD // LANE
GROUP_REAL = tuple(GROUP_LANES[g + 1] - GROUP_LANES[g] for g in range(3))
Q_BLOCK = 512


def pad_groups(w, axis):
    parts = []
    for g in range(3):
        blk = lax.slice_in_dim(w, GROUP_LANES[g], GROUP_LANES[g + 1], axis=axis)
        pad = [(0, 0)] * w.ndim
        pad[axis] = (0, PAD_LANES - GROUP_REAL[g])
        parts.append(jnp.pad(blk, pad))
    return parts


def unpad_groups(parts, axis):
    return jnp.concatenate([lax.slice_in_dim(p, 0, GROUP_REAL[g], axis=axis) for g, p in enumerate(parts)],
                           axis=axis)


def _qkv_pieces(group, part):
    lo, hi = part * D_MODEL + GROUP_LANES[group], part * D_MODEL + GROUP_LANES[group + 1]
    pieces = []
    while lo < hi:
        shard = lo // QKV_SHARD
        end = min(hi, (shard + 1) * QKV_SHARD)
        pieces.append((shard, lo - shard * QKV_SHARD, end - shard * QKV_SHARD))
        lo = end
    return pieces


def pad_qkv_weight(w_qkv_sh):
    groups = []
    for g in range(3):
        cols = []
        for part in range(3):
            cols += [w_qkv_sh[s][:, lo:hi] for s, lo, hi in _qkv_pieces(g, part)]
            if GROUP_REAL[g] < PAD_LANES:
                cols.append(jnp.zeros((D_MODEL, PAD_LANES - GROUP_REAL[g]), w_qkv_sh.dtype))
        groups.append(jnp.concatenate(cols, axis=1))
    return jnp.stack(groups)


def unpad_qkv_grad(g_pad):
    shard_cols = [[] for _ in range(N_SHARDS)]
    for part in range(3):
        for g in range(3):
            at = part * PAD_LANES
            for s, lo, hi in _qkv_pieces(g, part):
                shard_cols[s].append(g_pad[g][:, at:at + hi - lo])
                at += hi - lo
    return jnp.stack([jnp.concatenate(cols, axis=1) for cols in shard_cols])


def rope_tables(t_len):
    inv_freq = 1.0 / (ROPE_THETA ** (jnp.arange(0, HEAD_DIM, 2, dtype=F32) / HEAD_DIM))
    ang = jnp.arange(t_len, dtype=F32)[:, None] * inv_freq[None, :]
    cos_h, sin_h = lax.optimization_barrier((jnp.cos(ang), jnp.sin(ang)))
    reps = (1, 2 * LANE // HEAD_DIM)
    return jnp.tile(cos_h, reps), jnp.tile(sin_h, reps)


def _rot_half(v):
    n = v.shape[1]
    lane = lax.broadcasted_iota(jnp.int32, v.shape, 1)
    return jnp.where(lane % HEAD_DIM < HEAD_DIM // 2,
                     -pltpu.roll(v, n - HEAD_DIM // 2, 1), pltpu.roll(v, HEAD_DIM // 2, 1))


def _lane_cols(j):
    return slice(j * LANE, (j + 1) * LANE)


def _to_residue_major(slab, j_src, dst_ref, j_dst, dil, rows):
    for r in range(dil):
        dst_ref[r, :, _lane_cols(j_dst)] = slab[j_src, pl.ds(r, rows // dil, stride=dil), :].astype(dst_ref.dtype)


def _to_natural(src_ref, j_src, slab, j_dst, dil, rows):
    for r in range(dil):
        slab[j_dst, pl.ds(r, rows // dil, stride=dil), :] = src_ref[r, :, _lane_cols(j_src)].astype(F32)


def qkv_fwd(x, g_row, w_pad, cos, sin):
    t_len = x.shape[0]
    tm = _tile_rows(t_len)

    def body(x_ref, g_ref, w_ref, cos_ref, sin_ref, h_ref, o1_ref, o4_ref, o16_ref, slabs):
        h = _rms_fwd(x_ref[...], g_ref[...]).astype(BF16)
        h_ref[...] = h
        accs = [_dot(h, w_ref[gi]) for gi in range(3)]
        cos_t = cos_ref[...]
        sin_t = sin_ref[...]
        for gi, (dil, o_ref) in enumerate(zip(GROUP_DIL, (o1_ref, o4_ref, o16_ref))):
            slab = slabs.at[gi]
            for j in range(N_SLABS):
                a = accs[gi][:, _lane_cols(j)]
                if j < 6:
                    a = a * cos_t + _rot_half(a) * sin_t
                if j < 3:
                    a = a * (HEAD_DIM ** -0.5)
                if dil == 1:
                    o_ref[0, :, _lane_cols(j)] = a.astype(BF16)
                else:
                    slab[j] = a
                    _to_residue_major(slab, j, o_ref, j, dil, tm)

    row_spec = pl.BlockSpec((tm, D_MODEL), lambda t: (t, 0))
    tab_spec = pl.BlockSpec((tm, LANE), lambda t: (t, 0))
    out_specs = [row_spec] + [pl.BlockSpec((d, tm // d, QKV_PAD), lambda t: (0, t, 0)) for d in GROUP_DIL]
    out_shape = [_sds((t_len, D_MODEL), BF16)] + [_sds((d, t_len // d, QKV_PAD), BF16) for d in GROUP_DIL]
    return _pcall(
        body, name="qkv_fwd", grid=(t_len // tm,),
        in_specs=[row_spec, pl.BlockSpec((1, D_MODEL), lambda t: (0, 0)),
                  pl.BlockSpec((3, D_MODEL, QKV_PAD), lambda t: (0, 0, 0)), tab_spec, tab_spec],
        out_specs=out_specs, out_shape=out_shape,
        scratch_shapes=[pltpu.VMEM((3, N_SLABS, tm, LANE), F32)],
        semantics=("arbitrary",),
    )(x, g_row, w_pad, cos, sin)


def _band_masks():
    qi = lax.broadcasted_iota(jnp.int32, (ATTN_W, 2 * ATTN_W), 0)
    kj = lax.broadcasted_iota(jnp.int32, (ATTN_W, 2 * ATTN_W), 1)
    dist = ATTN_W + qi - kj
    band = (dist >= 0) & (dist <= ATTN_W)
    return band, band & (kj >= ATTN_W)


def _half_masks():
    lane = lax.broadcasted_iota(jnp.int32, (1, LANE), 1)
    return [lane < HEAD_DIM, lane >= HEAD_DIM]


def _live_halves(gi, j):
    hms = _half_masks()
    return hms if (gi == 0 or j < 2) else hms[:1]


def attn_fwd(qkv_g, gi, name):
    dil, l_len, _ = qkv_g.shape
    qb = min(Q_BLOCK, l_len)
    nsub = qb // ATTN_W

    def body(q_ref, kc_ref, kp_ref, vc_ref, vp_ref, o_ref, lse_ref, kbuf, vbuf):
        n = pl.program_id(1)
        kbuf[pl.ds(0, ATTN_W), :] = kp_ref[...]
        kbuf[pl.ds(ATTN_W, qb), :] = kc_ref[...]
        vbuf[pl.ds(0, ATTN_W), :] = vp_ref[...]
        vbuf[pl.ds(ATTN_W, qb), :] = vc_ref[...]
        band, band_first = _band_masks()

        def sub(b, carry):
            r0 = pl.multiple_of(b * ATTN_W, ATTN_W)
            mask = band_first | (band & (n + b > 0))
            krows = pl.ds(r0, 2 * ATTN_W)
            scores = []
            for j in range(3):
                q = q_ref[pl.ds(r0, ATTN_W), _lane_cols(j)]
                for hm in _live_halves(gi, j):
                    scores.append(_dot_nt(jnp.where(hm, q, jnp.zeros_like(q)), kbuf[krows, _lane_cols(j)]))
            scores = iter(scores)
            head_lane = lax.broadcasted_iota(jnp.int32, (1, LANE), 1)
            lse = jnp.zeros((ATTN_W, LANE), F32)
            for j in range(3):
                cols = _lane_cols(j)
                v = vbuf[krows, cols]
                o = jnp.zeros((ATTN_W, LANE), F32)
                for half, hm in enumerate(_live_halves(gi, j)):
                    s = jnp.where(mask, next(scores), NEG_INF)
                    m = jnp.max(s, axis=-1, keepdims=True)
                    e = jnp.exp(s - m)
                    den = jnp.sum(e, axis=-1, keepdims=True)
                    p = (e * (1.0 / den)).astype(BF16)
                    o = jnp.where(hm, _dot(p, v), o)
                    lse = jnp.where(head_lane == 2 * j + half, m + jnp.log(den), lse)
                o_ref[pl.ds(r0, ATTN_W), cols] = o.astype(BF16)
            lse_ref[pl.ds(r0, ATTN_W), :] = lse
            return carry

        lax.fori_loop(0, nsub, sub, 0)

    cur = lambda c: pl.BlockSpec((None, qb, PAD_LANES), lambda r, n: (r, n, c))
    prev = lambda c: pl.BlockSpec((None, ATTN_W, PAD_LANES), lambda r, n: (r, jnp.maximum(n * nsub - 1, 0), c))
    return _pcall(
        body, name=name, grid=(dil, l_len // qb),
        in_specs=[cur(0), cur(1), prev(1), cur(2), prev(2)],
        out_specs=[pl.BlockSpec((None, qb, PAD_LANES), lambda r, n: (r, n, 0)),
                   pl.BlockSpec((None, qb, LANE), lambda r, n: (r, n, 0))],
        out_shape=[_sds((dil, l_len, PAD_LANES), BF16), _sds((dil, l_len, LANE), F32)],
        scratch_shapes=[pltpu.VMEM((qb + ATTN_W, PAD_LANES), BF16), pltpu.VMEM((qb + ATTN_W, PAD_LANES), BF16)],
        semantics=("arbitrary", "arbitrary"),
    )(qkv_g, qkv_g, qkv_g, qkv_g, qkv_g)


def _group_stats(lses):
    head_lane = lax.broadcasted_iota(jnp.int32, (1, LANE), 1)
    fulls, glse = [], []
    for g in range(3):
        real = head_lane < GROUP_HEADS[g]
        mx = jnp.max(jnp.where(real, lses[g], -jnp.inf), axis=-1, keepdims=True)
        sm = jnp.sum(jnp.where(real, jnp.exp(lses[g] - mx), 0.0), axis=-1, keepdims=True)
        fulls.append(mx + jnp.log(sm))
        glse.append(fulls[g] - math.log(GROUP_HEADS[g]))
    top = jnp.maximum(jnp.maximum(glse[0], glse[1]), glse[2])
    ex = [jnp.exp(v - top) for v in glse]
    tot = ex[0] + ex[1] + ex[2]
    alpha = [v / tot for v in ex]
    lane = lax.broadcasted_iota(jnp.int32, (1, QKV_PAD), 1)
    scale = jnp.where(lane < PAD_LANES, 3.0 * alpha[0],
                      jnp.where(lane < 2 * PAD_LANES, 3.0 * alpha[1], 3.0 * alpha[2]))
    return alpha, fulls, scale


def attn_out_fwd(x, o_parts, lse_parts, w_out_pad):
    t_len = x.shape[0]
    tm = _tile_rows(t_len)

    def body(x_ref, o1, o4, o16, l1, l4, l16, w_ref, xo_ref, mg_ref, o_ref, lse_ref, o_slab, l_slab):
        for gi, (dil, og, lg) in enumerate(zip(GROUP_DIL, (o1, o4, o16), (l1, l4, l16))):
            for j in range(3):
                _to_natural(og, j, o_slab, 3 * gi + j, dil, tm)
            _to_natural(lg, 0, l_slab, gi, dil, tm)
        o = jnp.concatenate([o_slab[j] for j in range(N_SLABS)], axis=1)
        lses = [l_slab[gi] for gi in range(3)]
        o_ref[...] = o.astype(BF16)
        for gi in range(3):
            lse_ref[:, _lane_cols(gi)] = lses[gi]
        _, _, scale = _group_stats(lses)
        merged = (o * scale).astype(BF16)
        mg_ref[...] = merged
        xo_ref[...] = x_ref[...] + _dot(merged, w_ref[...])

    row_spec = pl.BlockSpec((tm, D_MODEL), lambda t: (t, 0))
    pad_spec = pl.BlockSpec((tm, QKV_PAD), lambda t: (t, 0))
    o_specs = [pl.BlockSpec((d, tm // d, PAD_LANES), lambda t: (0, t, 0)) for d in GROUP_DIL]
    lse_specs = [pl.BlockSpec((d, tm // d, LANE), lambda t: (0, t, 0)) for d in GROUP_DIL]
    return _pcall(
        body, name="attn_out_fwd", grid=(t_len // tm,),
        in_specs=[row_spec] + o_specs + lse_specs + [pl.BlockSpec((QKV_PAD, D_MODEL), lambda t: (0, 0))],
        out_specs=[row_spec, pad_spec, pad_spec, pl.BlockSpec((tm, 3 * LANE), lambda t: (t, 0))],
        out_shape=[_sds((t_len, D_MODEL), F32), _sds((t_len, QKV_PAD), BF16),
                   _sds((t_len, QKV_PAD), BF16), _sds((t_len, 3 * LANE), F32)],
        scratch_shapes=[pltpu.VMEM((N_SLABS, tm, LANE), F32), pltpu.VMEM((3, tm, LANE), F32)],
        semantics=("arbitrary",),
    )(x, *o_parts, *lse_parts, w_out_pad)


def attn_out_bwd(dxo, w_out_pad, o, lse, duty=None):
    t_len = dxo.shape[0]
    tm = _tile_rows(t_len)

    def body(dx_ref, w_ref, o_ref, lse_ref, d1, d4, d16, c1, c4, c16, slab):
        dmerged = _dot_nt(dx_ref[...].astype(BF16), w_ref[...])
        o_t = o_ref[...].astype(F32)
        lses = [lse_ref[:, _lane_cols(gi)] for gi in range(3)]
        alpha, fulls, scale = _group_stats(lses)
        e = dmerged * o_t
        lane = lax.broadcasted_iota(jnp.int32, (1, QKV_PAD), 1)
        dalpha = [3.0 * jnp.sum(jnp.where((lane >= g * PAD_LANES) & (lane < g * PAD_LANES + GROUP_REAL[g]), e, 0.0),
                                axis=-1, keepdims=True) for g in range(3)]
        mean_da = alpha[0] * dalpha[0] + alpha[1] * dalpha[1] + alpha[2] * dalpha[2]
        dglse = [alpha[g] * (dalpha[g] - mean_da) for g in range(3)]
        do = dmerged * scale
        es = e * scale
        for j in range(N_SLABS):
            slab[j] = do[:, _lane_cols(j)]
        for gi, (dil, dg) in enumerate(zip(GROUP_DIL, (d1, d4, d16))):
            for j in range(3):
                _to_residue_major(slab, 3 * gi + j, dg, j, dil, tm)
        head_lane = lax.broadcasted_iota(jnp.int32, (1, LANE), 1)
        first = head_lane < HEAD_DIM
        for gi, (dil, cg) in enumerate(zip(GROUP_DIL, (c1, c4, c16))):
            c_g = -(dglse[gi] * jnp.exp(lses[gi] - fulls[gi]))
            for j in range(3):
                blk = es[:, _lane_cols(3 * gi + j)]
                halves = (jnp.sum(jnp.where(first, blk, 0.0), axis=-1, keepdims=True),
                          jnp.sum(jnp.where(first, 0.0, blk), axis=-1, keepdims=True))
                for half in range(2):
                    c_g = c_g + jnp.where(head_lane == 2 * j + half, halves[half], 0.0)
            slab[gi] = c_g
            _to_residue_major(slab, gi, cg, 0, dil, tm)

    row_spec = pl.BlockSpec((tm, D_MODEL), lambda t: (t, 0))
    pad_spec = pl.BlockSpec((tm, QKV_PAD), lambda t: (t, 0))
    do_specs = [pl.BlockSpec((d, tm // d, PAD_LANES), lambda t: (0, t, 0)) for d in GROUP_DIL]
    c_specs = [pl.BlockSpec((d, tm // d, LANE), lambda t: (0, t, 0)) for d in GROUP_DIL]
    outs = _pcall(
        body, name="attn_out_bwd", grid=(t_len // tm,),
        in_specs=[row_spec, pl.BlockSpec((QKV_PAD, D_MODEL), lambda t: (0, 0)), pad_spec,
                  pl.BlockSpec((tm, 3 * LANE), lambda t: (t, 0))],
        out_specs=do_specs + c_specs,
        out_shape=[_sds((d, t_len // d, PAD_LANES), BF16) for d in GROUP_DIL]
                  + [_sds((d, t_len // d, LANE), F32) for d in GROUP_DIL],
        scratch_shapes=[pltpu.VMEM((N_SLABS, tm, LANE), F32)],
        semantics=("arbitrary",), duty=duty,
    )(dxo, w_out_pad, o, lse)
    if duty is None:
        return outs[:3], outs[3:]
    return (outs[0][:3], outs[0][3:]), outs[1]


def attn_bwd(qkv_g, do_g, lse_g, c_g, gi, name, duty=None):
    dil, l_len, _ = qkv_g.shape
    qb = min(Q_BLOCK, l_len)
    nsub = qb // ATTN_W
    nsb = l_len // qb

    def body(q_ref, kc_ref, kp_ref, vc_ref, vp_ref, do_ref, lse_ref, c_ref,
             qn_ref, don_ref, lsen_ref, cn_ref, o_ref, kbuf, vbuf, dkbuf, dvbuf):
        n = pl.program_id(1)
        kbuf[pl.ds(0, ATTN_W), :] = kp_ref[...]
        kbuf[pl.ds(ATTN_W, qb), :] = kc_ref[...]
        vbuf[pl.ds(0, ATTN_W), :] = vp_ref[...]
        vbuf[pl.ds(ATTN_W, qb), :] = vc_ref[...]
        dkbuf[...] = jnp.zeros_like(dkbuf)
        dvbuf[...] = jnp.zeros_like(dvbuf)

        def block(q_of, do_of, lse_of, c_of, krows, mask, dq_rows):
            heads = []
            for j in range(3):
                cols = _lane_cols(j)
                q, do_t, k, v = q_of(cols), do_of(cols), kbuf[krows, cols], vbuf[krows, cols]
                for half, hm in enumerate(_live_halves(gi, j)):
                    qh = jnp.where(hm, q, jnp.zeros_like(q))
                    doh = jnp.where(hm, do_t, jnp.zeros_like(do_t))
                    heads.append((j, 2 * j + half, hm, qh, doh, _dot_nt(qh, k), _dot_nt(doh, v)))
            head_lane = lax.broadcasted_iota(jnp.int32, (1, LANE), 1)
            lse_t, c_t = lse_of(), c_of()
            for j in range(3):
                cols = _lane_cols(j)
                k = kbuf[krows, cols]
                dq = jnp.zeros((ATTN_W, LANE), F32)
                dk = jnp.zeros((k.shape[0], LANE), F32)
                dv = jnp.zeros((k.shape[0], LANE), F32)
                for hj, head, hm, qh, doh, s, dp in heads:
                    if hj != j:
                        continue
                    lse_h = jnp.max(jnp.where(head_lane == head, lse_t, -jnp.inf), axis=-1, keepdims=True)
                    c_h = jnp.max(jnp.where(head_lane == head, c_t, -jnp.inf), axis=-1, keepdims=True)
                    p = jnp.exp(jnp.where(mask, s, NEG_INF) - lse_h)
                    ds = (p * (dp - c_h)).astype(BF16)
                    if dq_rows is not None:
                        dq = jnp.where(hm, _dot(ds, k), dq)
                    dk = dk + _dot_tn(ds, qh)
                    dv = dv + _dot_tn(p.astype(BF16), doh)
                if dq_rows is not None:
                    o_ref[dq_rows, cols] = dq.astype(BF16)
                dkbuf[krows, cols] += dk
                dvbuf[krows, cols] += dv

        band, band_first = _band_masks()

        def sub(b, carry):
            rows = pl.ds(pl.multiple_of(b * ATTN_W, ATTN_W), ATTN_W)
            krows = pl.ds(pl.multiple_of(b * ATTN_W, ATTN_W), 2 * ATTN_W)
            block(lambda c: q_ref[rows, c], lambda c: do_ref[rows, c], lambda: lse_ref[rows, :],
                  lambda: c_ref[rows, :], krows, band_first | (band & (n + b > 0)), rows)
            return carry

        lax.fori_loop(0, nsub, sub, 0)

        qi = lax.broadcasted_iota(jnp.int32, (ATTN_W, ATTN_W), 0)
        kj = lax.broadcasted_iota(jnp.int32, (ATTN_W, ATTN_W), 1)
        nmask = (qi <= kj) & (n < nsb - 1)
        block(lambda c: qn_ref[:, c], lambda c: don_ref[:, c], lambda: lsen_ref[...],
              lambda: cn_ref[...], pl.ds(qb, ATTN_W), nmask, None)
        o_ref[:, pl.ds(PAD_LANES, PAD_LANES)] = dkbuf[pl.ds(ATTN_W, qb), :].astype(BF16)
        o_ref[:, pl.ds(2 * PAD_LANES, PAD_LANES)] = dvbuf[pl.ds(ATTN_W, qb), :].astype(BF16)

    cur = lambda c: pl.BlockSpec((None, qb, PAD_LANES), lambda r, n: (r, n, c))
    prev = lambda c: pl.BlockSpec((None, ATTN_W, PAD_LANES), lambda r, n: (r, jnp.maximum(n * nsub - 1, 0), c))
    nxt_row = lambda r, n: (r, jnp.minimum((n + 1) * nsub, nsb * nsub - 1), 0)
    nxt = pl.BlockSpec((None, ATTN_W, PAD_LANES), nxt_row)
    head_cur = pl.BlockSpec((None, qb, LANE), lambda r, n: (r, n, 0))
    head_nxt = pl.BlockSpec((None, ATTN_W, LANE), nxt_row)
    return _pcall(
        body, name=name, grid=(dil, nsb),
        in_specs=[cur(0), cur(1), prev(1), cur(2), prev(2), cur(0), head_cur, head_cur, nxt, nxt, head_nxt, head_nxt],
        out_specs=pl.BlockSpec((None, qb, QKV_PAD), lambda r, n: (r, n, 0)),
        out_shape=_sds((dil, l_len, QKV_PAD), BF16),
        scratch_shapes=[pltpu.VMEM((qb + ATTN_W, PAD_LANES), BF16), pltpu.VMEM((qb + ATTN_W, PAD_LANES), BF16),
                        pltpu.VMEM((qb + ATTN_W, PAD_LANES), F32), pltpu.VMEM((qb + ATTN_W, PAD_LANES), F32)],
        semantics=("arbitrary", "arbitrary"), duty=duty,
    )(qkv_g, qkv_g, qkv_g, qkv_g, qkv_g, do_g, lse_g, c_g, qkv_g, do_g, lse_g, c_g)


def qkv_bwd(dqkv_parts, w_pad, dxo, x, g_row, cos, sin):
    t_len = x.shape[0]
    tm = _tile_rows(t_len)

    def body(p1, p4, p16, w_ref, dxo_ref, x_ref, g_ref, cos_ref, sin_ref, dq_ref, dx_ref, dn_ref, slabs):
        @pl.when(pl.program_id(0) == 0)
        def _():
            dn_ref[...] = jnp.zeros_like(dn_ref)

        cos_t = cos_ref[...]
        sin_t = sin_ref[...]
        dh = None
        for gi, (dil, part) in enumerate(zip(GROUP_DIL, (p1, p4, p16))):
            slab = slabs.at[gi]
            for j in range(N_SLABS):
                if dil == 1:
                    a = part[0, :, _lane_cols(j)].astype(F32)
                else:
                    _to_natural(part, j, slab, j, dil, tm)
                    a = slab[j]
                if j < 6:
                    a = a * cos_t - _rot_half(a * sin_t)
                if j < 3:
                    a = a * (HEAD_DIM ** -0.5)
                dq_ref[gi, :, _lane_cols(j)] = a.astype(BF16)
            contrib = _dot_nt(dq_ref[gi], w_ref[gi])
            dh = contrib if dh is None else dh + contrib
        dx, dn = _rms_bwd(dh, x_ref[...], g_ref[...])
        dx_ref[...] = dxo_ref[...] + dx
        dn_ref[...] += dn

    row_spec = pl.BlockSpec((tm, D_MODEL), lambda t: (t, 0))
    vec_spec = pl.BlockSpec((1, D_MODEL), lambda t: (0, 0))
    tab_spec = pl.BlockSpec((tm, LANE), lambda t: (t, 0))
    part_specs = [pl.BlockSpec((d, tm // d, QKV_PAD), lambda t: (0, t, 0)) for d in GROUP_DIL]
    return _pcall(
        body, name="qkv_bwd", grid=(t_len // tm,),
        in_specs=part_specs + [pl.BlockSpec((3, D_MODEL, QKV_PAD), lambda t: (0, 0, 0)),
                               row_spec, row_spec, vec_spec, tab_spec, tab_spec],
        out_specs=[pl.BlockSpec((3, tm, QKV_PAD), lambda t: (0, t, 0)), row_spec, vec_spec],
        out_shape=[_sds((3, t_len, QKV_PAD), BF16), _sds((t_len, D_MODEL), F32), _sds((1, D_MODEL), F32)],
        scratch_shapes=[pltpu.VMEM((3, N_SLABS, tm, LANE), F32)],
        semantics=("arbitrary",),
    )(*dqkv_parts, w_pad, dxo, x, g_row, cos, sin)


def final_fwd_bwd(x, g_row, target):
    t_len = x.shape[0]
    tm = _tile_rows(t_len)

    def body(x_ref, g_ref, tgt_ref, dx_ref, dn_ref, loss_ref):
        @pl.when(pl.program_id(0) == 0)
        def _():
            dn_ref[...] = jnp.zeros_like(dn_ref)
            loss_ref[...] = jnp.zeros_like(loss_ref)

        x_t = x_ref[...]
        g = g_ref[...]
        diff = _rms_fwd(x_t, g) - tgt_ref[...]
        loss_ref[...] += 0.5 * jnp.sum(jnp.mean(diff * diff, axis=-1, keepdims=True), axis=0, keepdims=True)
        dx, dn = _rms_bwd(diff * (1.0 / D_MODEL), x_t, g)
        dx_ref[...] = dx
        dn_ref[...] += dn

    row_spec = pl.BlockSpec((tm, D_MODEL), lambda t: (t, 0))
    vec_spec = pl.BlockSpec((1, D_MODEL), lambda t: (0, 0))
    return _pcall(
        body, name="final_fwd_bwd", grid=(t_len // tm,),
        in_specs=[row_spec, vec_spec, row_spec],
        out_specs=[row_spec, vec_spec, pl.BlockSpec((1, 1), lambda t: (0, 0))],
        out_shape=[_sds((t_len, D_MODEL), F32), _sds((1, D_MODEL), F32), _sds((1, 1), F32)],
        semantics=("arbitrary",),
    )(x, g_row, target)


def pool_bwd(dxo, x, g_row, w_in, w_grp, scale, w_out, zr, duty=None):
    t_len = x.shape[0]
    tm = _tile_rows(t_len)
    nt = t_len // tm

    def body(dxo_ref, x_ref, g_ref, win_ref, wgrp_ref, scale_ref, wout_ref, zr_ref,
             dzs_ref, du_ref, dx_ref, dn_ref, dsc_ref, ebuf):
        i = pl.program_id(0)
        t = nt - 1 - i

        @pl.when(i == 0)
        def _():
            ebuf[pl.ds(tm, POOL_HALO), :] = jnp.zeros((POOL_HALO, D_MODEL), F32)
            dn_ref[...] = jnp.zeros_like(dn_ref)
            dsc_ref[...] = jnp.zeros_like(dsc_ref)

        dxo_t = dxo_ref[...]
        dz = _dot_nt(dxo_t.astype(BF16), wout_ref[...])
        dsc_ref[...] += jnp.sum(dz * zr_ref[...].astype(F32), axis=0, keepdims=True)
        dzs_ref[...] = (dz * scale_ref[...]).astype(BF16)
        row = t * tm + lax.broadcasted_iota(jnp.int32, (tm, 1), 0)
        for gi, w in enumerate(POOL_WINDOWS):
            cols = pl.ds(gi * POOL_GROUP_DIM, POOL_GROUP_DIM)
            dp_g = _dot_nt(dzs_ref[:, cols], wgrp_ref[gi])
            inv_cnt = 1.0 / jnp.minimum(row + 1, w).astype(F32)
            ebuf[pl.ds(0, tm), cols] = dp_g * inv_cnt
            acc = -dp_g
            for j in range(w):
                acc = acc + ebuf[pl.ds(j, tm), cols]
            du_ref[:, cols] = acc.astype(BF16)
        ebuf[pl.ds(tm, POOL_HALO), :] = ebuf[pl.ds(0, POOL_HALO), :]
        dh = _dot_nt(du_ref[...], win_ref[...])
        dx, dn = _rms_bwd(dh, x_ref[...], g_ref[...])
        dx_ref[...] = dxo_t + dx
        dn_ref[...] += dn

    row_spec = pl.BlockSpec((tm, D_MODEL), lambda i: (nt - 1 - i, 0))
    full = lambda shape: pl.BlockSpec(shape, lambda i: (0,) * len(shape))
    vec = full((1, D_MODEL))
    return _pcall(
        body, name="pool_bwd", grid=(nt,),
        in_specs=[row_spec, row_spec, vec, full((D_MODEL, D_MODEL)), full((4, POOL_GROUP_DIM, POOL_GROUP_DIM)),
                  vec, full((D_MODEL, D_MODEL)), row_spec],
        out_specs=[row_spec, row_spec, row_spec, vec, vec],
        out_shape=[_sds((t_len, D_MODEL), BF16), _sds((t_len, D_MODEL), BF16), _sds((t_len, D_MODEL), F32),
                   _sds((1, D_MODEL), F32), _sds((1, D_MODEL), F32)],
        scratch_shapes=[pltpu.VMEM((tm + POOL_HALO, D_MODEL), F32)],
        semantics=("arbitrary",), duty=duty,
    )(dxo, x, g_row, w_in, w_grp, scale, w_out, zr)


def _mesh_pos():
    return lax.axis_index("x"), lax.axis_index("y"), lax.axis_index("c")


def _other_chips(x, y):
    return [(1 - x, y), (x, 1 - y), (1 - x, 1 - y)]


def _remote(src, dst, send_sem, recv_sem, device):
    return pltpu.make_async_remote_copy(src_ref=src, dst_ref=dst, send_sem=send_sem, recv_sem=recv_sem,
                                        device_id=device, device_id_type=MESH)


class _Duty:
    aliases = {}

    def mid(self, ins, outs, sems):
        pass


class Together(_Duty):
    def __init__(self, duties):
        self.duties = duties
        self.ins = [a for d in duties for a in d.ins]
        self.out_shape = [s for d in duties for s in d.out_shape]
        self.scratch = [s for d in duties for s in d.scratch]
        self.aliases = {}
        i0 = o0 = 0
        for d in duties:
            self.aliases.update({i0 + i: o0 + o for i, o in d.aliases.items()})
            i0 += len(d.ins)
            o0 += len(d.out_shape)

    def _each(self, ins, outs, sems):
        i0 = o0 = s0 = 0
        for d in self.duties:
            ni, no, ns = len(d.ins), len(d.out_shape), len(d.scratch)
            yield d, ins[i0:i0 + ni], outs[o0:o0 + no], sems[s0:s0 + ns]
            i0, o0, s0 = i0 + ni, o0 + no, s0 + ns

    def split(self, outs):
        return [list(o) for _, _, o, _ in self._each(self.ins, outs, self.scratch)]

    def start(self, ins, outs, sems):
        for d, i, o, s in self._each(ins, outs, sems):
            d.start(i, o, s)

    def mid(self, ins, outs, sems):
        for d, i, o, s in self._each(ins, outs, sems):
            d.mid(i, o, s)

    def finish(self, ins, outs, sems):
        for d, i, o, s in self._each(ins, outs, sems):
            d.finish(i, o, s)


def run_duty(duty, name):
    d_in, d_out = len(duty.ins), len(duty.out_shape)

    def body(*refs):
        ins, outs, sems = refs[:d_in], refs[d_in:d_in + d_out], refs[d_in + d_out:]
        duty.start(ins, outs, sems)
        duty.mid(ins, outs, sems)
        duty.finish(ins, outs, sems)

    return pl.pallas_call(
        body, name=name, out_shape=list(duty.out_shape), in_specs=[_ANY] * d_in, out_specs=[_ANY] * d_out,
        scratch_shapes=list(duty.scratch), input_output_aliases=dict(duty.aliases),
        compiler_params=pltpu.CompilerParams(has_side_effects=True),
    )(*duty.ins)


class GatherWeights(_Duty):
    def __init__(self, shards):
        n = self.n = len(shards)
        self.halves = [s.shape[0] // 2 for s in shards]
        my_slot = 2 * lax.axis_index("x") + lax.axis_index("y")
        staged = [lax.dynamic_update_slice(lax.empty((N_SHARDS,) + s.shape, s.dtype), s[None], (my_slot, 0, 0))
                  for s in shards]
        self.ins = list(shards) + staged
        self.out_shape = [_sds((N_SHARDS,) + s.shape, s.dtype) for s in shards]
        self.aliases = {n + a: a for a in range(n)}
        self.scratch = [pltpu.SemaphoreType.DMA((n, 6)), pltpu.SemaphoreType.DMA((n, 6))]

    def _over_ici(self, ins, outs, sems):
        x, y, c = _mesh_pos()
        return [_remote(ins[a].at[pl.ds(c * h, h)], outs[a].at[2 * x + y, pl.ds(c * h, h)],
                        sems[0].at[a, j], sems[1].at[a, j], (*chip, c))
                for a, h in enumerate(self.halves) for j, chip in enumerate(_other_chips(x, y))]

    def _forwards(self, outs, sems, half_of):
        x, y, c = _mesh_pos()
        cps = []
        for a, h in enumerate(self.halves):
            for j, chip in enumerate(_other_chips(x, y)):
                slot = outs[a].at[2 * chip[0] + chip[1], pl.ds(half_of(c) * h, h)]
                cps.append(_remote(slot, slot, sems[0].at[a, 3 + j], sems[1].at[a, 3 + j], (x, y, 1 - c)))
        return cps

    def start(self, ins, outs, sems):
        for cp in self._over_ici(ins, outs, sems):
            cp.start()

    def mid(self, ins, outs, sems):
        x, y, c = _mesh_pos()
        forwards = self._forwards(outs, sems, lambda core: core)
        k = 0
        for a, h in enumerate(self.halves):
            for j, chip in enumerate(_other_chips(x, y)):
                slot = outs[a].at[2 * chip[0] + chip[1], pl.ds(c * h, h)]
                _remote(slot, slot, sems[0].at[a, j], sems[1].at[a, j], (*chip, c)).wait_recv()
                forwards[k].start()
                k += 1

    def finish(self, ins, outs, sems):
        for cp in self._forwards(outs, sems, lambda core: 1 - core):
            cp.wait_recv()
        for cp in self._over_ici(ins, outs, sems) + self._forwards(outs, sems, lambda core: core):
            cp.wait_send()


class GradReducer:
    def __init__(self, c_idx, pos_idx):
        self.c_idx, self.pos_idx = c_idx, pos_idx
        self.in_flight = []
        self.done = {}

    def push(self, name, grad):
        self.in_flight.append(dict(name=name, stage="halves", data=grad))

    def _duties(self):
        make = {"halves": SiblingHalves, "exchange": ChipExchange, "share": SiblingShare}
        return Together([make[w["stage"]]([w["data"]]) for w in self.in_flight])

    def _advance(self, duties, outs):
        still = []
        for w, (res,) in zip(self.in_flight, duties.split(outs)):
            if w["stage"] == "halves":
                partial = add_my_half(w["data"], res, self.c_idx, f"rs_add_{w['name']}")
                still.append(dict(name=w["name"], stage="exchange", data=partial))
            elif w["stage"] == "exchange":
                reduced = sum_slots(res, w["data"], self.pos_idx, f"rs_sum_{w['name']}")
                still.append(dict(name=w["name"], stage="share", data=reduced))
            else:
                self.done[w["name"]] = res
        self.in_flight = still

    def carried_by(self, fn, *args, **kw):
        if not self.in_flight:
            return fn(*args, **kw)
        duties = self._duties()
        out, duty_outs = fn(*args, duty=duties, **kw)
        self._advance(duties, duty_outs)
        return out

    def drain(self, name):
        step = 0
        while self.in_flight:
            duties = self._duties()
            self._advance(duties, run_duty(duties, f"{name}{step}"))
            step += 1


class SiblingHalves(_Duty):
    def __init__(self, grads):
        n = len(grads)
        self.halves = [g.shape[1] // 2 for g in grads]
        self.ins = list(grads)
        self.out_shape = [_sds((N_SHARDS, h, g.shape[2]), g.dtype) for g, h in zip(grads, self.halves)]
        self.scratch = [pltpu.SemaphoreType.DMA((n,)), pltpu.SemaphoreType.DMA((n,))]

    def _copies(self, ins, outs, sems):
        x, y, c = _mesh_pos()
        return [_remote(ins[a].at[:, pl.ds((1 - c) * h, h)], outs[a], sems[0].at[a], sems[1].at[a], (x, y, 1 - c))
                for a, h in enumerate(self.halves)]

    def start(self, ins, outs, sems):
        for cp in self._copies(ins, outs, sems):
            cp.start()

    def finish(self, ins, outs, sems):
        for cp in self._copies(ins, outs, sems):
            cp.wait()


class ChipExchange(_Duty):
    def __init__(self, parts):
        n = self.n = len(parts)
        self.ins = list(parts)
        self.out_shape = [_sds(p.shape, p.dtype) for p in parts]
        self.scratch = [pltpu.SemaphoreType.DMA((n, 3)), pltpu.SemaphoreType.DMA((n, 3))]

    def _copies(self, ins, outs, sems, arriving):
        x, y, c = _mesh_pos()
        cps = []
        for a in range(self.n):
            for j, chip in enumerate(_other_chips(x, y)):
                theirs = 2 * chip[0] + chip[1]
                src = outs[a].at[theirs] if arriving else ins[a].at[theirs]
                dst = outs[a].at[theirs] if arriving else outs[a].at[2 * x + y]
                cps.append(_remote(src, dst, sems[0].at[a, j], sems[1].at[a, j], (*chip, c)))
        return cps

    def start(self, ins, outs, sems):
        for cp in self._copies(ins, outs, sems, False):
            cp.start()

    def finish(self, ins, outs, sems):
        for cp in self._copies(ins, outs, sems, True):
            cp.wait_recv()
        for cp in self._copies(ins, outs, sems, False):
            cp.wait_send()


class SiblingShare(_Duty):
    def __init__(self, reduced):
        n = self.n = len(reduced)
        self.ins = list(reduced)
        self.out_shape = [_sds(r.shape, r.dtype) for r in reduced]
        self.aliases = {a: a for a in range(n)}
        self.scratch = [pltpu.SemaphoreType.DMA((n,)), pltpu.SemaphoreType.DMA((n,))]

    def _copies(self, outs, sems, half_of):
        x, y, c = _mesh_pos()
        cps = []
        for a in range(self.n):
            h = outs[a].shape[0] // 2
            rows = outs[a].at[pl.ds(half_of(c) * h, h)]
            cps.append(_remote(rows, rows, sems[0].at[a], sems[1].at[a], (x, y, 1 - c)))
        return cps

    def start(self, ins, outs, sems):
        for cp in self._copies(outs, sems, lambda core: core):
            cp.start()

    def finish(self, ins, outs, sems):
        for cp in self._copies(outs, sems, lambda core: 1 - core):
            cp.wait_recv()
        for cp in self._copies(outs, sems, lambda core: core):
            cp.wait_send()


def allreduce_small(v):
    def body(v_ref, o_ref, buf, send_sems, recv_sems):
        x, y, c = _mesh_pos()
        me = 4 * x + 2 * y + c
        buf[me] = v_ref[...]
        flip = lambda p, f: 1 - p if f else p
        peers = [(flip(x, k & 4), flip(y, k & 2), flip(c, k & 1)) for k in range(1, N_DEV)]
        cps = []
        for k, peer in enumerate(peers):
            cp = _remote(v_ref, buf.at[me], send_sems.at[k], recv_sems.at[k], peer)
            cp.start()
            cps.append(cp)
        for k, peer in enumerate(peers):
            slot = buf.at[4 * peer[0] + 2 * peer[1] + peer[2]]
            _remote(slot, slot, send_sems.at[k], recv_sems.at[k], peer).wait_recv()
        for cp in cps:
            cp.wait_send()
        acc = buf[0]
        for i in range(1, N_DEV):
            acc = acc + buf[i]
        o_ref[...] = acc

    vm = pl.BlockSpec(memory_space=pltpu.VMEM)
    return pl.pallas_call(
        body, name="allreduce_small", out_shape=_sds(v.shape, v.dtype), in_specs=[vm], out_specs=vm,
        scratch_shapes=[pltpu.VMEM((N_DEV,) + v.shape, v.dtype),
                        pltpu.SemaphoreType.DMA((N_DEV - 1,)), pltpu.SemaphoreType.DMA((N_DEV - 1,))],
        compiler_params=pltpu.CompilerParams(has_side_effects=True),
    )(v)


def add_my_half(grad, theirs, c_idx, name):
    _, r, cols = grad.shape
    h = r // 2

    def body(c_ref, g_ref, t_ref, o_ref):
        o_ref[...] = (g_ref[...] + t_ref[...]).astype(BF16)

    slot = pl.BlockSpec((None, h, cols), lambda s, c: (s, 0, 0))
    grid_spec = pltpu.PrefetchScalarGridSpec(
        num_scalar_prefetch=1, grid=(N_SHARDS,),
        in_specs=[pl.BlockSpec((None, h, cols), lambda s, c: (s, c[0], 0)), slot], out_specs=slot)
    return pl.pallas_call(
        body, name=name, grid_spec=grid_spec, out_shape=_sds((N_SHARDS, h, cols), BF16),
        compiler_params=pltpu.CompilerParams(dimension_semantics=("arbitrary",), vmem_limit_bytes=VMEM_LIMIT_BYTES),
    )(c_idx, grad, theirs)


def sum_slots(received, mine, pos_idx, name):
    _, h, cols = received.shape

    def body(pos_ref, r_ref, m_ref, o_ref):
        acc = None
        for k in range(N_SHARDS):
            term = jnp.where(pos_ref[0] == k, m_ref[k], r_ref[k]).astype(F32)
            acc = term if acc is None else acc + term
        o_ref[...] = acc

    whole = pl.BlockSpec((N_SHARDS, h, cols), lambda i, pos: (0, 0, 0))
    grid_spec = pltpu.PrefetchScalarGridSpec(
        num_scalar_prefetch=1, grid=(1,), in_specs=[whole, whole],
        out_specs=pl.BlockSpec((h, cols), lambda i, pos: (pos[1], 0)))
    return pl.pallas_call(
        body, name=name, grid_spec=grid_spec, out_shape=_sds((2 * h, cols), F32),
        compiler_params=pltpu.CompilerParams(dimension_semantics=("arbitrary",), vmem_limit_bytes=VMEM_LIMIT_BYTES),
    )(pos_idx, received, mine)


def adamw(name, grads, w, m, v):
    n_layers, r, cols = w.shape
    tr = r // 2 if r % 16 == 0 else r
    bias1 = 1.0 - ADAM_B1 ** ADAM_STEP
    bias2 = 1.0 - ADAM_B2 ** ADAM_STEP

    def body(*refs):
        g_refs = refs[:n_layers]
        w_ref, m_ref, v_ref, go_ref, d_ref, mo_ref, vo_ref = refs[n_layers:]
        g = g_refs[0][...]
        for layer in range(1, n_layers):
            g = jnp.where(pl.program_id(0) == layer, g_refs[layer][...], g)
        m_new = ADAM_B1 * m_ref[...] + (1.0 - ADAM_B1) * g
        v_new = ADAM_B2 * v_ref[...] + (1.0 - ADAM_B2) * (g * g)
        m_hat = m_new / bias1
        v_hat = v_new / bias2
        go_ref[...] = g
        d_ref[...] = -ADAM_LR * (m_hat / (jnp.sqrt(v_hat) + ADAM_EPS) + ADAM_WD * w_ref[...])
        mo_ref[...] = m_new
        vo_ref[...] = v_new

    g_spec = pl.BlockSpec((tr, cols), lambda l, i: (i, 0))
    lay_spec = pl.BlockSpec((None, tr, cols), lambda l, i: (l, i, 0))
    shape = _sds((n_layers, r, cols), F32)
    return _pcall(
        body, name=name, grid=(n_layers, r // tr),
        in_specs=[g_spec] * n_layers + [lay_spec] * 3, out_specs=[lay_spec] * 4,
        out_shape=[shape] * 4, semantics=("arbitrary", "arbitrary"),
    )(*grads, w, m, v)


def kernel(x, norm_mix, norm_ffn, norm_final, pool_w_in, pool_w_group, pool_scale, pool_w_out, attn_w_qkv, attn_w_out, ffn_w_gate, ffn_w_up, ffn_w_down, loss_target, m_norm_mix, m_norm_ffn, m_norm_final, m_pool_w_in, m_pool_w_group, m_pool_scale, m_pool_w_out, m_attn_w_qkv, m_attn_w_out, m_ffn_w_gate, m_ffn_w_up, m_ffn_w_down, v_norm_mix, v_norm_ffn, v_norm_final, v_pool_w_in, v_pool_w_group, v_pool_scale, v_pool_w_out, v_attn_w_qkv, v_attn_w_out, v_ffn_w_gate, v_ffn_w_up, v_ffn_w_down):
    t_len = x.shape[1]
    x0 = x.reshape(t_len, D_MODEL)
    target = loss_target.reshape(t_len, D_MODEL)
    row = lambda a: a.reshape(1, D_MODEL)

    grp_rows = POOL_GROUP_DIM // N_SHARDS
    bf = lambda a: a.astype(BF16)
    gate_t, up_t = jnp.swapaxes(ffn_w_gate, 1, 2), jnp.swapaxes(ffn_w_up, 1, 2)
    pool_shards = [bf(pool_w_in[0]), bf(pool_w_group[0].reshape(4 * grp_rows, POOL_GROUP_DIM)), bf(pool_w_out[0])]
    ffn0_shards = [bf(gate_t[0]), bf(up_t[0]), bf(ffn_w_down[0])]
    late_shards = [bf(attn_w_qkv[0]), bf(attn_w_out[0]), bf(gate_t[1]), bf(up_t[1]), bf(ffn_w_down[1])]
    cos, sin = rope_tables(t_len)
    c_idx = lax.axis_index("c").astype(jnp.int32).reshape(1)
    pos_idx = jnp.stack([2 * lax.axis_index("x") + lax.axis_index("y"), lax.axis_index("c")]).astype(jnp.int32)
    chip_rows = lambda g: g.reshape(N_SHARDS, D_MODEL // N_SHARDS, D_MODEL)

    g_pool = run_duty(GatherWeights(pool_shards), "gather_pool")
    w_in = g_pool[0].reshape(D_MODEL, D_MODEL)
    w_grp = g_pool[1].reshape(N_SHARDS, 4, grp_rows, POOL_GROUP_DIM).transpose(1, 0, 2, 3).reshape(
        4, POOL_GROUP_DIM, POOL_GROUP_DIM)
    w_out = g_pool[2].reshape(D_MODEL, D_MODEL)
    (h0, p, zr, z, x1), ffn0 = pool_fwd(x0, row(norm_mix[0]), w_in, w_grp, pool_scale, w_out,
                                        duty=GatherWeights(ffn0_shards))
    (h1, gate0, up0, act0, x2), late = ffn_fwd(x1, row(norm_ffn[0]), *ffn0, "ffn_fwd0",
                                               duty=GatherWeights(late_shards))
    w_qkv = pad_qkv_weight(late[0])
    w_ao = jnp.concatenate(pad_groups(late[1].reshape(D_MODEL, D_MODEL), 0), axis=0)
    ffn1 = late[2:5]
    h2, *qkv_parts = qkv_fwd(x2, row(norm_mix[1]), w_qkv, cos, sin)
    o_parts, lse_parts = [], []
    for gi in range(3):
        o_g, lse_g = attn_fwd(qkv_parts[gi], gi, f"attn_fwd_g{gi}")
        o_parts.append(o_g)
        lse_parts.append(lse_g)
    x3, merged, o_nat, lse_nat = attn_out_fwd(x2, o_parts, lse_parts, w_ao)
    h3, gate1, up1, act1, x4 = ffn_fwd(x3, row(norm_ffn[1]), *ffn1, "ffn_fwd1")
    dx4, d_norm_final, loss_local = final_fwd_bwd(x4, row(norm_final), target)

    red = GradReducer(c_idx, pos_idx)
    dgate1, dup1, dx3, d_nf1 = ffn_bwd(dx4, x3, row(norm_ffn[1]), gate1, up1, *ffn1, "ffn_bwd1")
    g_gate1 = wgrad_row_sharded("wgrad_gate1", dgate1, h3)
    g_up1 = wgrad_row_sharded("wgrad_up1", dup1, h3)
    g_down1 = wgrad_row_sharded("wgrad_down1", act1, dx4)
    red.push("gate1", g_gate1)
    red.push("up1", g_up1)
    do_parts, c_parts = red.carried_by(attn_out_bwd, dx3, w_ao, o_nat, lse_nat)
    g_ao = wgrad_full("wgrad_attn_out", merged, dx3)
    red.push("down1", g_down1)
    red.push("attn_out", chip_rows(unpad_groups(jnp.split(g_ao, 3, axis=0), 0)))
    dqkv_parts = [red.carried_by(attn_bwd, qkv_parts[gi], do_parts[gi], lse_parts[gi], c_parts[gi], gi,
                                 f"attn_bwd_g{gi}") for gi in range(3)]
    dqkv, dx2, d_nm1 = qkv_bwd(dqkv_parts, w_qkv, dx3, x2, row(norm_mix[1]), cos, sin)
    red.push("qkv", unpad_qkv_grad(wgrad_col_sharded("wgrad_qkv", h2, dqkv)))

    dgate0, dup0, dx1, d_nf0 = red.carried_by(ffn_bwd, dx2, x1, row(norm_ffn[0]), gate0, up0, *ffn0, "ffn_bwd0")
    red.push("gate0", red.carried_by(wgrad_row_sharded, "wgrad_gate0", dgate0, h1))
    red.push("up0", red.carried_by(wgrad_row_sharded, "wgrad_up0", dup0, h1))
    red.push("down0", red.carried_by(wgrad_row_sharded, "wgrad_down0", act0, dx2))
    red.push("pool_out", chip_rows(red.carried_by(wgrad_full, "wgrad_pool_out", z, dx1)))
    dzs, du, dx0, d_nm0, d_scale = pool_bwd(dx1, x0, row(norm_mix[0]), w_in, w_grp, pool_scale, w_out, zr)
    g_grp = red.carried_by(wgrad_pool_group, "wgrad_pool_group", p, dzs)
    red.push("pool_group", g_grp.reshape(N_SHARDS, 4 * grp_rows, POOL_GROUP_DIM))
    red.push("pool_in", chip_rows(red.carried_by(wgrad_full, "wgrad_pool_in", h0, du)))
    red.drain("rs_tail")
    full = [red.done[nm] for nm in ("pool_in", "pool_group", "pool_out", "qkv", "attn_out",
                                    "gate0", "gate1", "up0", "up1", "down0", "down1")]

    zero_row = jnp.zeros((1, D_MODEL), F32)
    small = jnp.concatenate([d_nm0, d_nm1, d_nf0, d_nf1, d_norm_final, d_scale,
                             jnp.broadcast_to(loss_local, (1, D_MODEL)), zero_row], axis=0)
    small = allreduce_small(small)
    loss = small[6, 0]

    pack = lambda a, b, c, d: jnp.concatenate([a, b, row(c), d, zero_row, zero_row], axis=0)[None]
    sg, sd, sm, sv = adamw("adamw_small", [small],
                           pack(norm_mix, norm_ffn, norm_final, pool_scale),
                           pack(m_norm_mix, m_norm_ffn, m_norm_final, m_pool_scale),
                           pack(v_norm_mix, v_norm_ffn, v_norm_final, v_pool_scale))
    unpack = lambda a: (a[0, 0:2], a[0, 2:4], a[0, 4], a[0, 5:6])

    def update(name, grads, w, m, v, transposed=False):
        if transposed:
            w, m, v = (jnp.swapaxes(a, 1, 2) for a in (w, m, v))
        n_layers = len(grads)
        shp = (n_layers,) + grads[0].shape
        outs = [o.reshape(w.shape) for o in adamw(name, grads, w.reshape(shp), m.reshape(shp), v.reshape(shp))]
        return [jnp.swapaxes(o, 1, 2) for o in outs] if transposed else outs

    big = [
        update("adamw_pool_in", [full[0]], pool_w_in, m_pool_w_in, v_pool_w_in),
        update("adamw_pool_group", [full[1]], pool_w_group, m_pool_w_group, v_pool_w_group),
        update("adamw_pool_out", [full[2]], pool_w_out, m_pool_w_out, v_pool_w_out),
        update("adamw_qkv", [full[3]], attn_w_qkv, m_attn_w_qkv, v_attn_w_qkv),
        update("adamw_attn_out", [full[4]], attn_w_out, m_attn_w_out, v_attn_w_out),
        update("adamw_gate", [full[5], full[6]], ffn_w_gate, m_ffn_w_gate, v_ffn_w_gate, transposed=True),
        update("adamw_up", [full[7], full[8]], ffn_w_up, m_ffn_w_up, v_ffn_w_up, transposed=True),
        update("adamw_down", [full[9], full[10]], ffn_w_down, m_ffn_w_down, v_ffn_w_down),
    ]

    def leaves(k, small_vals):
        nm, nf, nfin, psc = unpack(small_vals)
        return [nm, nf, nfin, big[0][k], big[1][k], psc, big[2][k], big[3][k], big[4][k],
                big[5][k], big[6][k], big[7][k]]

    grad_x = dx0.reshape(x.shape)
    return (loss, grad_x, *leaves(0, sg), *leaves(1, sd), *leaves(2, sm), *leaves(3, sv))
```

```python
import math

import jax
import jax.numpy as jnp
from jax import lax
from jax.experimental import pallas as pl
from jax.experimental.pallas import tpu as pltpu

F32 = jnp.float32
BF16 = jnp.bfloat16

D_MODEL = 1024
N_SHARDS = 4
N_DEV = 8
D_FF = 2816
FF_SHARD = D_FF // N_SHARDS
HEAD_DIM = 64
QKV_SHARD = 3 * D_MODEL // N_SHARDS
POOL_WINDOWS = (2, 4, 8, 16)
POOL_GROUP_DIM = 256
POOL_HALO = 16
ATTN_W = 128
GROUP_LANES = (0, 384, 704, 1024)
GROUP_HEADS = (6, 5, 5)
GROUP_DIL = (1, 4, 16)
ROPE_THETA = 10000.0
EPS = 1e-6
NEG_INF = -1e30
LANE = 128
VMEM_LIMIT_BYTES = 60 * 1024 * 1024

ADAM_LR = 0.001
ADAM_B1 = 0.9
ADAM_B2 = 0.999
ADAM_EPS = 1e-08
ADAM_WD = 0.01
ADAM_STEP = 10

NT_DIMS = (((1,), (1,)), ((), ()))
TN_DIMS = (((0,), (0,)), ((), ()))
MESH = pl.DeviceIdType.MESH


_ANY = pl.BlockSpec(memory_space=pl.ANY)


def _pcall(body, *, name, out_shape, grid=None, in_specs=None, out_specs=None, scratch_shapes=(),
           semantics=None, duty=None):
    kw = {}
    if in_specs is not None and duty is None:
        kw["in_specs"] = in_specs
    if out_specs is not None and duty is None:
        kw["out_specs"] = out_specs
    if grid is not None:
        kw["grid"] = grid
    params = dict(dimension_semantics=semantics, vmem_limit_bytes=VMEM_LIMIT_BYTES)
    if duty is None:
        return pl.pallas_call(body, name=name, out_shape=out_shape, scratch_shapes=list(scratch_shapes),
                              compiler_params=pltpu.CompilerParams(**params), **kw)

    single = not isinstance(out_shape, (list, tuple))
    c_out_shape = [out_shape] if single else list(out_shape)
    c_out_specs = [out_specs] if single else list(out_specs)
    n_in, n_out, n_scr = len(in_specs), len(c_out_shape), len(scratch_shapes)
    d_in, d_out = len(duty.ins), len(duty.out_shape)
    total = math.prod(grid)
    mid_step = (5 * total) // 6

    def wrapped(*refs):
        c_in, d_ins = refs[:n_in], refs[n_in:n_in + d_in]
        o0 = n_in + d_in
        c_outs, d_outs = refs[o0:o0 + n_out], refs[o0 + n_out:o0 + n_out + d_out]
        s0 = o0 + n_out + d_out
        c_scr, d_sems = refs[s0:s0 + n_scr], refs[s0 + n_scr:]
        step = pl.program_id(0)
        for ax in range(1, len(grid)):
            step = step * grid[ax] + pl.program_id(ax)

        @pl.when(step == 0)
        def _():
            duty.start(d_ins, d_outs, d_sems)

        body(*c_in, *c_outs, *c_scr)

        @pl.when(step == mid_step)
        def _():
            duty.mid(d_ins, d_outs, d_sems)

        @pl.when(step == total - 1)
        def _():
            duty.finish(d_ins, d_outs, d_sems)

    call = pl.pallas_call(
        wrapped, name=name, grid=grid,
        in_specs=list(in_specs) + [_ANY] * d_in, out_specs=c_out_specs + [_ANY] * d_out,
        out_shape=c_out_shape + list(duty.out_shape),
        scratch_shapes=list(scratch_shapes) + list(duty.scratch),
        input_output_aliases={n_in + i: n_out + o for i, o in duty.aliases.items()},
        compiler_params=pltpu.CompilerParams(has_side_effects=True, **params))

    def run(*args):
        outs = call(*args, *duty.ins)
        c = outs[:n_out]
        return (c[0] if single else list(c)), list(outs[n_out:])

    return run


def _sds(shape, dtype):
    return jax.ShapeDtypeStruct(tuple(shape), dtype)


def _dot(a, b):
    return jnp.dot(a, b, preferred_element_type=F32)


def _dot_nt(a, b):
    return lax.dot_general(a, b, NT_DIMS, preferred_element_type=F32)


def _dot_tn(a, b):
    return lax.dot_general(a, b, TN_DIMS, preferred_element_type=F32)


def _rms_fwd(x, g):
    r = lax.rsqrt(jnp.mean(x * x, axis=-1, keepdims=True) + EPS)
    return x * r * g


def _rms_bwd(dh, x, g):
    r = lax.rsqrt(jnp.mean(x * x, axis=-1, keepdims=True) + EPS)
    xh = x * r
    dg = jnp.sum(dh * xh, axis=0, keepdims=True)
    dxh = dh * g
    dx = r * (dxh - xh * jnp.mean(dxh * xh, axis=-1, keepdims=True))
    return dx, dg


def _sigmoid(x):
    return 0.5 * jnp.tanh(0.5 * x) + 0.5


def _tile_rows(t):
    return min(512, t)


def _sub_tiles(tm, n_sub=2):
    rows = tm // n_sub
    return [pl.ds(i * rows, rows) for i in range(n_sub)]


def _wgrad_rows(t):
    return min(2048, t)


def pool_fwd(x, g_row, w_in, w_grp, scale, w_out, duty=None):
    t_len = x.shape[0]
    tm = _tile_rows(t_len)

    def body(x_ref, g_ref, win_ref, wgrp_ref, scale_ref, wout_ref,
             h_ref, p_ref, zr_ref, z_ref, xo_ref, ubuf):
        t = pl.program_id(0)

        @pl.when(t == 0)
        def _():
            ubuf[pl.ds(0, POOL_HALO), :] = jnp.zeros((POOL_HALO, D_MODEL), F32)

        x_t = x_ref[...]
        h = _rms_fwd(x_t, g_ref[...]).astype(BF16)
        h_ref[...] = h
        ubuf[pl.ds(POOL_HALO, tm), :] = _dot(h, win_ref[...])
        row = t * tm + lax.broadcasted_iota(jnp.int32, (tm, 1), 0)
        for gi, w in enumerate(POOL_WINDOWS):
            cols = pl.ds(gi * POOL_GROUP_DIM, POOL_GROUP_DIM)
            u_g = ubuf[pl.ds(POOL_HALO, tm), cols]
            acc = u_g
            for j in range(1, w):
                acc = acc + ubuf[pl.ds(POOL_HALO - j, tm), cols]
            inv_cnt = 1.0 / jnp.minimum(row + 1, w).astype(F32)
            p_g = (acc * inv_cnt - u_g).astype(BF16)
            p_ref[:, cols] = p_g
            z_g = _dot(p_g, wgrp_ref[gi])
            zr_ref[:, cols] = z_g.astype(BF16)
            z_ref[:, cols] = (z_g * scale_ref[:, cols]).astype(BF16)
        ubuf[pl.ds(0, POOL_HALO), :] = ubuf[pl.ds(tm, POOL_HALO), :]
        xo_ref[...] = x_t + _dot(z_ref[...], wout_ref[...])

    row_spec = pl.BlockSpec((tm, D_MODEL), lambda t: (t, 0))
    full2 = lambda shape: pl.BlockSpec(shape, lambda t: (0,) * len(shape))
    return _pcall(
        body, name="pool_fwd", grid=(t_len // tm,),
        in_specs=[row_spec, full2((1, D_MODEL)), full2((D_MODEL, D_MODEL)),
                  full2((4, POOL_GROUP_DIM, POOL_GROUP_DIM)), full2((1, D_MODEL)), full2((D_MODEL, D_MODEL))],
        out_specs=[row_spec] * 5,
        out_shape=[_sds((t_len, D_MODEL), BF16)] * 4 + [_sds((t_len, D_MODEL), F32)],
        scratch_shapes=[pltpu.VMEM((tm + POOL_HALO, D_MODEL), F32)],
        semantics=("arbitrary",), duty=duty,
    )(x, g_row, w_in, w_grp, scale, w_out)


def ffn_fwd(x, g_row, w_gate_t, w_up_t, w_down, name, duty=None):
    t_len = x.shape[0]
    tm = min(1024, t_len)

    def body(x_ref, g_ref, wg_ref, wu_ref, wd_ref, h_ref, go_ref, uo_ref, ao_ref, xo_ref, hbuf, acc):
        s = pl.program_id(1)

        @pl.when(s == 0)
        def _():
            h = _rms_fwd(x_ref[...], g_ref[...]).astype(BF16)
            hbuf[...] = h
            h_ref[...] = h
            acc[...] = jnp.zeros_like(acc)

        h = hbuf[...]
        gate = _dot_nt(h, wg_ref[...])
        up = _dot_nt(h, wu_ref[...])
        go_ref[...] = gate.astype(BF16)
        uo_ref[...] = up.astype(BF16)
        act = (gate * _sigmoid(gate) * up).astype(BF16)
        ao_ref[...] = act
        acc[...] += _dot(act, wd_ref[...])

        @pl.when(s == N_SHARDS - 1)
        def _():
            xo_ref[...] = x_ref[...] + acc[...]

    row_spec = pl.BlockSpec((tm, D_MODEL), lambda t, s: (t, 0))
    row_w = pl.BlockSpec((None, FF_SHARD, D_MODEL), lambda t, s: (s, 0, 0))
    act_spec = pl.BlockSpec((None, tm, FF_SHARD), lambda t, s: (s, t, 0))
    return _pcall(
        body, name=name, grid=(t_len // tm, N_SHARDS),
        in_specs=[row_spec, pl.BlockSpec((1, D_MODEL), lambda t, s: (0, 0)), row_w, row_w, row_w],
        out_specs=[row_spec, act_spec, act_spec, act_spec, row_spec],
        out_shape=[_sds((t_len, D_MODEL), BF16)] + [_sds((N_SHARDS, t_len, FF_SHARD), BF16)] * 3
                  + [_sds((t_len, D_MODEL), F32)],
        scratch_shapes=[pltpu.VMEM((tm, D_MODEL), BF16), pltpu.VMEM((tm, D_MODEL), F32)],
        semantics=("arbitrary", "arbitrary"), duty=duty,
    )(x, g_row, w_gate_t, w_up_t, w_down)


def ffn_bwd(dxo, x, g_row, gate, up, w_gate_t, w_up_t, w_down, name, duty=None):
    t_len = x.shape[0]
    tm = _tile_rows(t_len)

    def body(dxo_ref, x_ref, g_ref, gate_ref, up_ref, wg_ref, wu_ref, wd_ref,
             dg_ref, du_ref, dx_ref, dn_ref, dxb, dh):
        t = pl.program_id(0)
        s = pl.program_id(1)

        @pl.when(s == 0)
        def _():
            dxb[...] = dxo_ref[...].astype(BF16)
            dh[...] = jnp.zeros_like(dh)

        @pl.when(jnp.logical_and(s == 0, t == 0))
        def _():
            dn_ref[...] = jnp.zeros_like(dn_ref)

        sub_tiles = _sub_tiles(tm)
        dacts = [_dot_nt(dxb[rows, :], wd_ref[...]) for rows in sub_tiles]
        for rows, dact in zip(sub_tiles, dacts):
            gv = gate_ref[rows, :].astype(F32)
            uv = up_ref[rows, :].astype(F32)
            sg = _sigmoid(gv)
            dgv = (dact * uv * (sg * (1.0 + gv * (1.0 - sg)))).astype(BF16)
            duv = (dact * (gv * sg)).astype(BF16)
            dg_ref[rows, :] = dgv
            du_ref[rows, :] = duv
            dh[rows, :] += _dot(dgv, wg_ref[...]) + _dot(duv, wu_ref[...])

        @pl.when(s == N_SHARDS - 1)
        def _():
            dx, dn = _rms_bwd(dh[...], x_ref[...], g_ref[...])
            dx_ref[...] = dxo_ref[...] + dx
            dn_ref[...] += dn

    row_spec = pl.BlockSpec((tm, D_MODEL), lambda t, s: (t, 0))
    vec_spec = pl.BlockSpec((1, D_MODEL), lambda t, s: (0, 0))
    row_w = pl.BlockSpec((None, FF_SHARD, D_MODEL), lambda t, s: (s, 0, 0))
    act_spec = pl.BlockSpec((None, tm, FF_SHARD), lambda t, s: (s, t, 0))
    act_shape = _sds((N_SHARDS, t_len, FF_SHARD), BF16)
    return _pcall(
        body, name=name, grid=(t_len // tm, N_SHARDS),
        in_specs=[row_spec, row_spec, vec_spec, act_spec, act_spec, row_w, row_w, row_w],
        out_specs=[act_spec, act_spec, row_spec, vec_spec],
        out_shape=[act_shape, act_shape, _sds((t_len, D_MODEL), F32), _sds((1, D_MODEL), F32)],
        scratch_shapes=[pltpu.VMEM((tm, D_MODEL), BF16), pltpu.VMEM((tm, D_MODEL), F32)],
        semantics=("arbitrary", "arbitrary"), duty=duty,
    )(dxo, x, g_row, gate, up, w_gate_t, w_up_t, w_down)


def tn_matmul(name, a, b, a_spec, b_spec, out_shape, out_spec, grid, duty=None):
    def body(a_ref, b_ref, o_ref):
        @pl.when(pl.program_id(len(grid) - 1) == 0)
        def _():
            o_ref[...] = jnp.zeros_like(o_ref)

        res = _dot_tn(a_ref[...].astype(BF16), b_ref[...].astype(BF16))
        o_ref[...] += res.reshape(o_ref.shape)

    return _pcall(body, name=name, grid=grid, in_specs=[a_spec, b_spec], out_specs=out_spec,
                  out_shape=out_shape, semantics=("arbitrary",) * len(grid), duty=duty)(a, b)


def wgrad_full(name, a, b, duty=None):
    t_len, k = a.shape
    n = b.shape[1]
    tt = _wgrad_rows(t_len)
    return tn_matmul(name, a, b,
                     pl.BlockSpec((tt, k), lambda t: (t, 0)), pl.BlockSpec((tt, n), lambda t: (t, 0)),
                     _sds((k, n), F32), pl.BlockSpec((k, n), lambda t: (0, 0)), (t_len // tt,), duty)


def wgrad_col_sharded(name, a, b_sh, duty=None):
    t_len, k = a.shape
    n_sh, _, n = b_sh.shape
    tt = _wgrad_rows(t_len)
    return tn_matmul(name, a, b_sh,
                     pl.BlockSpec((tt, k), lambda s, t: (t, 0)), pl.BlockSpec((None, tt, n), lambda s, t: (s, t, 0)),
                     _sds((n_sh, k, n), F32), pl.BlockSpec((None, k, n), lambda s, t: (s, 0, 0)),
                     (n_sh, t_len // tt), duty)


def wgrad_row_sharded(name, a_sh, b, duty=None):
    t_len, n = b.shape
    n_sh, _, k = a_sh.shape
    tt = _wgrad_rows(t_len)

    def body(a_ref, b_ref, o_ref):
        s = pl.program_id(1)
        res = _dot_tn(a_ref[...], b_ref[...].astype(BF16))

        @pl.when(pl.program_id(0) == 0)
        def _():
            o_ref[s] = res

        @pl.when(pl.program_id(0) > 0)
        def _():
            o_ref[s] += res

    return _pcall(body, name=name, grid=(t_len // tt, n_sh),
                  in_specs=[pl.BlockSpec((None, tt, k), lambda t, s: (s, t, 0)),
                            pl.BlockSpec((tt, n), lambda t, s: (t, 0))],
                  out_specs=pl.BlockSpec((n_sh, k, n), lambda t, s: (0, 0, 0)),
                  out_shape=_sds((n_sh, k, n), F32), semantics=("arbitrary", "arbitrary"), duty=duty)(a_sh, b)


def wgrad_pool_group(name, p, dzs, duty=None):
    t_len = p.shape[0]
    tt = _wgrad_rows(t_len)
    gd = POOL_GROUP_DIM
    rows = gd // N_SHARDS
    return tn_matmul(name, p, dzs,
                     pl.BlockSpec((tt, gd), lambda g, t: (t, g)), pl.BlockSpec((tt, gd), lambda g, t: (t, g)),
                     _sds((N_SHARDS, 4, rows, gd), F32),
                     pl.BlockSpec((N_SHARDS, None, rows, gd), lambda g, t: (0, g, 0, 0)),
                     (4, t_len // tt), duty)


PAD_LANES = 384
QKV_PAD = 3 * PAD_LANES
N_SLABS = QKV_PAD // LANE
GROUP_REAL = tuple(GROUP_LANES[g + 1] - GROUP_LANES[g] for g in range(3))
Q_BLOCK = 1024


def pad_groups(w, axis):
    parts = []
    for g in range(3):
        blk = lax.slice_in_dim(w, GROUP_LANES[g], GROUP_LANES[g + 1], axis=axis)
        pad = [(0, 0)] * w.ndim
        pad[axis] = (0, PAD_LANES - GROUP_REAL[g])
        parts.append(jnp.pad(blk, pad))
    return parts


def unpad_groups(parts, axis):
    return jnp.concatenate([lax.slice_in_dim(p, 0, GROUP_REAL[g], axis=axis) for g, p in enumerate(parts)],
                           axis=axis)


def _qkv_pieces(group, part):
    lo, hi = part * D_MODEL + GROUP_LANES[group], part * D_MODEL + GROUP_LANES[group + 1]
    pieces = []
    while lo < hi:
        shard = lo // QKV_SHARD
        end = min(hi, (shard + 1) * QKV_SHARD)
        pieces.append((shard, lo - shard * QKV_SHARD, end - shard * QKV_SHARD))
        lo = end
    return pieces


def pad_qkv_weight(w_qkv_sh):
    groups = []
    for g in range(3):
        cols = []
        for part in range(3):
            cols += [w_qkv_sh[s][:, lo:hi] for s, lo, hi in _qkv_pieces(g, part)]
            if GROUP_REAL[g] < PAD_LANES:
                cols.append(jnp.zeros((D_MODEL, PAD_LANES - GROUP_REAL[g]), w_qkv_sh.dtype))
        groups.append(jnp.concatenate(cols, axis=1))
    return jnp.stack(groups)


def unpad_qkv_grad(g_pad):
    shard_cols = [[] for _ in range(N_SHARDS)]
    for part in range(3):
        for g in range(3):
            at = part * PAD_LANES
            for s, lo, hi in _qkv_pieces(g, part):
                shard_cols[s].append(g_pad[g][:, at:at + hi - lo])
                at += hi - lo
    return jnp.stack([jnp.concatenate(cols, axis=1) for cols in shard_cols])


def rope_tables(t_len):
    inv_freq = 1.0 / (ROPE_THETA ** (jnp.arange(0, HEAD_DIM, 2, dtype=F32) / HEAD_DIM))
    ang = jnp.arange(t_len, dtype=F32)[:, None] * inv_freq[None, :]
    cos_h, sin_h = lax.optimization_barrier((jnp.cos(ang), jnp.sin(ang)))
    reps = (1, 2 * LANE // HEAD_DIM)
    return jnp.tile(cos_h, reps), jnp.tile(sin_h, reps)


def _rot_half(v):
    n = v.shape[1]
    lane = lax.broadcasted_iota(jnp.int32, v.shape, 1)
    return jnp.where(lane % HEAD_DIM < HEAD_DIM // 2,
                     -pltpu.roll(v, n - HEAD_DIM // 2, 1), pltpu.roll(v, HEAD_DIM // 2, 1))


def _lane_cols(j):
    return slice(j * LANE, (j + 1) * LANE)


def _to_residue_major(slab, j_src, dst_ref, j_dst, dil, rows):
    for r in range(dil):
        dst_ref[r, :, _lane_cols(j_dst)] = slab[j_src, pl.ds(r, rows // dil, stride=dil), :].astype(dst_ref.dtype)


def _to_natural(src_ref, j_src, slab, j_dst, dil, rows):
    for r in range(dil):
        slab[j_dst, pl.ds(r, rows // dil, stride=dil), :] = src_ref[r, :, _lane_cols(j_src)].astype(F32)


def qkv_fwd(x, g_row, w_pad, cos, sin):
    t_len = x.shape[0]
    tm = _tile_rows(t_len)

    def body(x_ref, g_ref, w_ref, cos_ref, sin_ref, h_ref, o1_ref, o4_ref, o16_ref, slabs):
        h = _rms_fwd(x_ref[...], g_ref[...]).astype(BF16)
        h_ref[...] = h
        accs = [_dot(h, w_ref[gi]) for gi in range(3)]
        cos_t = cos_ref[...]
        sin_t = sin_ref[...]
        for gi, (dil, o_ref) in enumerate(zip(GROUP_DIL, (o1_ref, o4_ref, o16_ref))):
            slab = slabs.at[gi]
            for j in range(N_SLABS):
                a = accs[gi][:, _lane_cols(j)]
                if j < 6:
                    a = a * cos_t + _rot_half(a) * sin_t
                if j < 3:
                    a = a * (HEAD_DIM ** -0.5)
                if dil == 1:
                    o_ref[0, :, _lane_cols(j)] = a.astype(BF16)
                else:
                    slab[j] = a
                    _to_residue_major(slab, j, o_ref, j, dil, tm)

    row_spec = pl.BlockSpec((tm, D_MODEL), lambda t: (t, 0))
    tab_spec = pl.BlockSpec((tm, LANE), lambda t: (t, 0))
    out_specs = [row_spec] + [pl.BlockSpec((d, tm // d, QKV_PAD), lambda t: (0, t, 0)) for d in GROUP_DIL]
    out_shape = [_sds((t_len, D_MODEL), BF16)] + [_sds((d, t_len // d, QKV_PAD), BF16) for d in GROUP_DIL]
    return _pcall(
        body, name="qkv_fwd", grid=(t_len // tm,),
        in_specs=[row_spec, pl.BlockSpec((1, D_MODEL), lambda t: (0, 0)),
                  pl.BlockSpec((3, D_MODEL, QKV_PAD), lambda t: (0, 0, 0)), tab_spec, tab_spec],
        out_specs=out_specs, out_shape=out_shape,
        scratch_shapes=[pltpu.VMEM((3, N_SLABS, tm, LANE), F32)],
        semantics=("arbitrary",),
    )(x, g_row, w_pad, cos, sin)


def _band_masks():
    qi = lax.broadcasted_iota(jnp.int32, (ATTN_W, 2 * ATTN_W), 0)
    kj = lax.broadcasted_iota(jnp.int32, (ATTN_W, 2 * ATTN_W), 1)
    dist = ATTN_W + qi - kj
    band = (dist >= 0) & (dist <= ATTN_W)
    return band, band & (kj >= ATTN_W)


def _half_masks():
    lane = lax.broadcasted_iota(jnp.int32, (1, LANE), 1)
    return [lane < HEAD_DIM, lane >= HEAD_DIM]


def _live_halves(gi, j):
    hms = _half_masks()
    return hms if (gi == 0 or j < 2) else hms[:1]


def attn_fwd(qkv_g, gi, name):
    dil, l_len, _ = qkv_g.shape
    qb = min(Q_BLOCK, l_len)
    nsub = qb // ATTN_W

    def body(q_ref, kc_ref, kp_ref, vc_ref, vp_ref, o_ref, lse_ref, kbuf, vbuf):
        n = pl.program_id(1)
        kbuf[pl.ds(0, ATTN_W), :] = kp_ref[...]
        kbuf[pl.ds(ATTN_W, qb), :] = kc_ref[...]
        vbuf[pl.ds(0, ATTN_W), :] = vp_ref[...]
        vbuf[pl.ds(ATTN_W, qb), :] = vc_ref[...]
        band, band_first = _band_masks()

        def sub(b, carry):
            r0 = pl.multiple_of(b * ATTN_W, ATTN_W)
            mask = band_first | (band & (n + b > 0))
            krows = pl.ds(r0, 2 * ATTN_W)
            scores = []
            for j in range(3):
                q = q_ref[pl.ds(r0, ATTN_W), _lane_cols(j)]
                for hm in _live_halves(gi, j):
                    scores.append(_dot_nt(jnp.where(hm, q, jnp.zeros_like(q)), kbuf[krows, _lane_cols(j)]))
            scores = iter(scores)
            head_lane = lax.broadcasted_iota(jnp.int32, (1, LANE), 1)
            lse = jnp.zeros((ATTN_W, LANE), F32)
            for j in range(3):
                cols = _lane_cols(j)
                v = vbuf[krows, cols]
                o = jnp.zeros((ATTN_W, LANE), F32)
                for half, hm in enumerate(_live_halves(gi, j)):
                    s = jnp.where(mask, next(scores), NEG_INF)
                    m = jnp.max(s, axis=-1, keepdims=True)
                    e = jnp.exp(s - m)
                    den = jnp.sum(e, axis=-1, keepdims=True)
                    p = (e * (1.0 / den)).astype(BF16)
                    o = jnp.where(hm, _dot(p, v), o)
                    lse = jnp.where(head_lane == 2 * j + half, m + jnp.log(den), lse)
                o_ref[pl.ds(r0, ATTN_W), cols] = o.astype(BF16)
            lse_ref[pl.ds(r0, ATTN_W), :] = lse
            return carry

        lax.fori_loop(0, nsub, sub, 0)

    cur = lambda c: pl.BlockSpec((None, qb, PAD_LANES), lambda r, n: (r, n, c))
    prev = lambda c: pl.BlockSpec((None, ATTN_W, PAD_LANES), lambda r, n: (r, jnp.maximum(n * nsub - 1, 0), c))
    return _pcall(
        body, name=name, grid=(dil, l_len // qb),
        in_specs=[cur(0), cur(1), prev(1), cur(2), prev(2)],
        out_specs=[pl.BlockSpec((None, qb, PAD_LANES), lambda r, n: (r, n, 0)),
                   pl.BlockSpec((None, qb, LANE), lambda r, n: (r, n, 0))],
        out_shape=[_sds((dil, l_len, PAD_LANES), BF16), _sds((dil, l_len, LANE), F32)],
        scratch_shapes=[pltpu.VMEM((qb + ATTN_W, PAD_LANES), BF16), pltpu.VMEM((qb + ATTN_W, PAD_LANES), BF16)],
        semantics=("arbitrary", "arbitrary"),
    )(qkv_g, qkv_g, qkv_g, qkv_g, qkv_g)


def _group_stats(lses):
    head_lane = lax.broadcasted_iota(jnp.int32, (1, LANE), 1)
    fulls, glse = [], []
    for g in range(3):
        real = head_lane < GROUP_HEADS[g]
        mx = jnp.max(jnp.where(real, lses[g], -jnp.inf), axis=-1, keepdims=True)
        sm = jnp.sum(jnp.where(real, jnp.exp(lses[g] - mx), 0.0), axis=-1, keepdims=True)
        fulls.append(mx + jnp.log(sm))
        glse.append(fulls[g] - math.log(GROUP_HEADS[g]))
    top = jnp.maximum(jnp.maximum(glse[0], glse[1]), glse[2])
    ex = [jnp.exp(v - top) for v in glse]
    tot = ex[0] + ex[1] + ex[2]
    alpha = [v / tot for v in ex]
    lane = lax.broadcasted_iota(jnp.int32, (1, QKV_PAD), 1)
    scale = jnp.where(lane < PAD_LANES, 3.0 * alpha[0],
                      jnp.where(lane < 2 * PAD_LANES, 3.0 * alpha[1], 3.0 * alpha[2]))
    return alpha, fulls, scale


def attn_out_fwd(x, o_parts, lse_parts, w_out_pad):
    t_len = x.shape[0]
    tm = _tile_rows(t_len)

    def body(x_ref, o1, o4, o16, l1, l4, l16, w_ref, xo_ref, mg_ref, o_ref, lse_ref, o_slab, l_slab):
        for gi, (dil, og, lg) in enumerate(zip(GROUP_DIL, (o1, o4, o16), (l1, l4, l16))):
            for j in range(3):
                _to_natural(og, j, o_slab, 3 * gi + j, dil, tm)
            _to_natural(lg, 0, l_slab, gi, dil, tm)
        o = jnp.concatenate([o_slab[j] for j in range(N_SLABS)], axis=1)
        lses = [l_slab[gi] for gi in range(3)]
        o_ref[...] = o.astype(BF16)
        for gi in range(3):
            lse_ref[:, _lane_cols(gi)] = lses[gi]
        _, _, scale = _group_stats(lses)
        merged = (o * scale).astype(BF16)
        mg_ref[...] = merged
        xo_ref[...] = x_ref[...] + _dot(merged, w_ref[...])

    row_spec = pl.BlockSpec((tm, D_MODEL), lambda t: (t, 0))
    pad_spec = pl.BlockSpec((tm, QKV_PAD), lambda t: (t, 0))
    o_specs = [pl.BlockSpec((d, tm // d, PAD_LANES), lambda t: (0, t, 0)) for d in GROUP_DIL]
    lse_specs = [pl.BlockSpec((d, tm // d, LANE), lambda t: (0, t, 0)) for d in GROUP_DIL]
    return _pcall(
        body, name="attn_out_fwd", grid=(t_len // tm,),
        in_specs=[row_spec] + o_specs + lse_specs + [pl.BlockSpec((QKV_PAD, D_MODEL), lambda t: (0, 0))],
        out_specs=[row_spec, pad_spec, pad_spec, pl.BlockSpec((tm, 3 * LANE), lambda t: (t, 0))],
        out_shape=[_sds((t_len, D_MODEL), F32), _sds((t_len, QKV_PAD), BF16),
                   _sds((t_len, QKV_PAD), BF16), _sds((t_len, 3 * LANE), F32)],
        scratch_shapes=[pltpu.VMEM((N_SLABS, tm, LANE), F32), pltpu.VMEM((3, tm, LANE), F32)],
        semantics=("arbitrary",),
    )(x, *o_parts, *lse_parts, w_out_pad)


def attn_out_bwd(dxo, w_out_pad, o, lse, duty=None):
    t_len = dxo.shape[0]
    tm = _tile_rows(t_len)

    def body(dx_ref, w_ref, o_ref, lse_ref, d1, d4, d16, c1, c4, c16, slab):
        dmerged = _dot_nt(dx_ref[...].astype(BF16), w_ref[...])
        o_t = o_ref[...].astype(F32)
        lses = [lse_ref[:, _lane_cols(gi)] for gi in range(3)]
        alpha, fulls, scale = _group_stats(lses)
        e = dmerged * o_t
        lane = lax.broadcasted_iota(jnp.int32, (1, QKV_PAD), 1)
        dalpha = [3.0 * jnp.sum(jnp.where((lane >= g * PAD_LANES) & (lane < g * PAD_LANES + GROUP_REAL[g]), e, 0.0),
                                axis=-1, keepdims=True) for g in range(3)]
        mean_da = alpha[0] * dalpha[0] + alpha[1] * dalpha[1] + alpha[2] * dalpha[2]
        dglse = [alpha[g] * (dalpha[g] - mean_da) for g in range(3)]
        do = dmerged * scale
        es = e * scale
        for j in range(N_SLABS):
            slab[j] = do[:, _lane_cols(j)]
        for gi, (dil, dg) in enumerate(zip(GROUP_DIL, (d1, d4, d16))):
            for j in range(3):
                _to_residue_major(slab, 3 * gi + j, dg, j, dil, tm)
        head_lane = lax.broadcasted_iota(jnp.int32, (1, LANE), 1)
        first = head_lane < HEAD_DIM
        for gi, (dil, cg) in enumerate(zip(GROUP_DIL, (c1, c4, c16))):
            c_g = -(dglse[gi] * jnp.exp(lses[gi] - fulls[gi]))
            for j in range(3):
                blk = es[:, _lane_cols(3 * gi + j)]
                halves = (jnp.sum(jnp.where(first, blk, 0.0), axis=-1, keepdims=True),
                          jnp.sum(jnp.where(first, 0.0, blk), axis=-1, keepdims=True))
                for half in range(2):
                    c_g = c_g + jnp.where(head_lane == 2 * j + half, halves[half], 0.0)
            slab[gi] = c_g
            _to_residue_major(slab, gi, cg, 0, dil, tm)

    row_spec = pl.BlockSpec((tm, D_MODEL), lambda t: (t, 0))
    pad_spec = pl.BlockSpec((tm, QKV_PAD), lambda t: (t, 0))
    do_specs = [pl.BlockSpec((d, tm // d, PAD_LANES), lambda t: (0, t, 0)) for d in GROUP_DIL]
    c_specs = [pl.BlockSpec((d, tm // d, LANE), lambda t: (0, t, 0)) for d in GROUP_DIL]
    outs = _pcall(
        body, name="attn_out_bwd", grid=(t_len // tm,),
        in_specs=[row_spec, pl.BlockSpec((QKV_PAD, D_MODEL), lambda t: (0, 0)), pad_spec,
                  pl.BlockSpec((tm, 3 * LANE), lambda t: (t, 0))],
        out_specs=do_specs + c_specs,
        out_shape=[_sds((d, t_len // d, PAD_LANES), BF16) for d in GROUP_DIL]
                  + [_sds((d, t_len // d, LANE), F32) for d in GROUP_DIL],
        scratch_shapes=[pltpu.VMEM((N_SLABS, tm, LANE), F32)],
        semantics=("arbitrary",), duty=duty,
    )(dxo, w_out_pad, o, lse)
    if duty is None:
        return outs[:3], outs[3:]
    return (outs[0][:3], outs[0][3:]), outs[1]


def attn_bwd(qkv_g, do_g, lse_g, c_g, gi, name, duty=None):
    dil, l_len, _ = qkv_g.shape
    qb = min(Q_BLOCK, l_len)
    nsub = qb // ATTN_W
    nsb = l_len // qb

    def body(q_ref, kc_ref, kp_ref, vc_ref, vp_ref, do_ref, lse_ref, c_ref,
             qn_ref, don_ref, lsen_ref, cn_ref, o_ref, kbuf, vbuf, dkbuf, dvbuf):
        n = pl.program_id(1)
        kbuf[pl.ds(0, ATTN_W), :] = kp_ref[...]
        kbuf[pl.ds(ATTN_W, qb), :] = kc_ref[...]
        vbuf[pl.ds(0, ATTN_W), :] = vp_ref[...]
        vbuf[pl.ds(ATTN_W, qb), :] = vc_ref[...]
        dkbuf[...] = jnp.zeros_like(dkbuf)
        dvbuf[...] = jnp.zeros_like(dvbuf)

        def block(q_of, do_of, lse_of, c_of, krows, mask, dq_rows):
            heads = []
            for j in range(3):
                cols = _lane_cols(j)
                q, do_t, k, v = q_of(cols), do_of(cols), kbuf[krows, cols], vbuf[krows, cols]
                for half, hm in enumerate(_live_halves(gi, j)):
                    qh = jnp.where(hm, q, jnp.zeros_like(q))
                    doh = jnp.where(hm, do_t, jnp.zeros_like(do_t))
                    heads.append((j, 2 * j + half, hm, qh, doh, _dot_nt(qh, k), _dot_nt(doh, v)))
            head_lane = lax.broadcasted_iota(jnp.int32, (1, LANE), 1)
            lse_t, c_t = lse_of(), c_of()
            for j in range(3):
                cols = _lane_cols(j)
                k = kbuf[krows, cols]
                dq = jnp.zeros((ATTN_W, LANE), F32)
                dk = jnp.zeros((k.shape[0], LANE), F32)
                dv = jnp.zeros((k.shape[0], LANE), F32)
                for hj, head, hm, qh, doh, s, dp in heads:
                    if hj != j:
                        continue
                    lse_h = jnp.max(jnp.where(head_lane == head, lse_t, -jnp.inf), axis=-1, keepdims=True)
                    c_h = jnp.max(jnp.where(head_lane == head, c_t, -jnp.inf), axis=-1, keepdims=True)
                    p = jnp.exp(jnp.where(mask, s, NEG_INF) - lse_h)
                    ds = (p * (dp - c_h)).astype(BF16)
                    if dq_rows is not None:
                        dq = jnp.where(hm, _dot(ds, k), dq)
                    dk = dk + _dot_tn(ds, qh)
                    dv = dv + _dot_tn(p.astype(BF16), doh)
                if dq_rows is not None:
                    o_ref[dq_rows, cols] = dq.astype(BF16)
                dkbuf[krows, cols] += dk
                dvbuf[krows, cols] += dv

        band, band_first = _band_masks()

        def sub(b, carry):
            rows = pl.ds(pl.multiple_of(b * ATTN_W, ATTN_W), ATTN_W)
            krows = pl.ds(pl.multiple_of(b * ATTN_W, ATTN_W), 2 * ATTN_W)
            block(lambda c: q_ref[rows, c], lambda c: do_ref[rows, c], lambda: lse_ref[rows, :],
                  lambda: c_ref[rows, :], krows, band_first | (band & (n + b > 0)), rows)
            return carry

        lax.fori_loop(0, nsub, sub, 0)

        qi = lax.broadcasted_iota(jnp.int32, (ATTN_W, ATTN_W), 0)
        kj = lax.broadcasted_iota(jnp.int32, (ATTN_W, ATTN_W), 1)
        nmask = (qi <= kj) & (n < nsb - 1)
        block(lambda c: qn_ref[:, c], lambda c: don_ref[:, c], lambda: lsen_ref[...],
              lambda: cn_ref[...], pl.ds(qb, ATTN_W), nmask, None)
        o_ref[:, pl.ds(PAD_LANES, PAD_LANES)] = dkbuf[pl.ds(ATTN_W, qb), :].astype(BF16)
        o_ref[:, pl.ds(2 * PAD_LANES, PAD_LANES)] = dvbuf[pl.ds(ATTN_W, qb), :].astype(BF16)

    cur = lambda c: pl.BlockSpec((None, qb, PAD_LANES), lambda r, n: (r, n, c))
    prev = lambda c: pl.BlockSpec((None, ATTN_W, PAD_LANES), lambda r, n: (r, jnp.maximum(n * nsub - 1, 0), c))
    nxt_row = lambda r, n: (r, jnp.minimum((n + 1) * nsub, nsb * nsub - 1), 0)
    nxt = pl.BlockSpec((None, ATTN_W, PAD_LANES), nxt_row)
    head_cur = pl.BlockSpec((None, qb, LANE), lambda r, n: (r, n, 0))
    head_nxt = pl.BlockSpec((None, ATTN_W, LANE), nxt_row)
    return _pcall(
        body, name=name, grid=(dil, nsb),
        in_specs=[cur(0), cur(1), prev(1), cur(2), prev(2), cur(0), head_cur, head_cur, nxt, nxt, head_nxt, head_nxt],
        out_specs=pl.BlockSpec((None, qb, QKV_PAD), lambda r, n: (r, n, 0)),
        out_shape=_sds((dil, l_len, QKV_PAD), BF16),
        scratch_shapes=[pltpu.VMEM((qb + ATTN_W, PAD_LANES), BF16), pltpu.VMEM((qb + ATTN_W, PAD_LANES), BF16),
                        pltpu.VMEM((qb + ATTN_W, PAD_LANES), F32), pltpu.VMEM((qb + ATTN_W, PAD_LANES), F32)],
        semantics=("arbitrary", "arbitrary"), duty=duty,
    )(qkv_g, qkv_g, qkv_g, qkv_g, qkv_g, do_g, lse_g, c_g, qkv_g, do_g, lse_g, c_g)


def qkv_bwd(dqkv_parts, w_pad, dxo, x, g_row, cos, sin):
    t_len = x.shape[0]
    tm = _tile_rows(t_len)

    def body(p1, p4, p16, w_ref, dxo_ref, x_ref, g_ref, cos_ref, sin_ref, dq_ref, dx_ref, dn_ref, slabs):
        @pl.when(pl.program_id(0) == 0)
        def _():
            dn_ref[...] = jnp.zeros_like(dn_ref)

        cos_t = cos_ref[...]
        sin_t = sin_ref[...]
        dh = None
        for gi, (dil, part) in enumerate(zip(GROUP_DIL, (p1, p4, p16))):
            slab = slabs.at[gi]
            for j in range(N_SLABS):
                if dil == 1:
                    a = part[0, :, _lane_cols(j)].astype(F32)
                else:
                    _to_natural(part, j, slab, j, dil, tm)
                    a = slab[j]
                if j < 6:
                    a = a * cos_t - _rot_half(a * sin_t)
                if j < 3:
                    a = a * (HEAD_DIM ** -0.5)
                dq_ref[gi, :, _lane_cols(j)] = a.astype(BF16)
            contrib = _dot_nt(dq_ref[gi], w_ref[gi])
            dh = contrib if dh is None else dh + contrib
        dx, dn = _rms_bwd(dh, x_ref[...], g_ref[...])
        dx_ref[...] = dxo_ref[...] + dx
        dn_ref[...] += dn

    row_spec = pl.BlockSpec((tm, D_MODEL), lambda t: (t, 0))
    vec_spec = pl.BlockSpec((1, D_MODEL), lambda t: (0, 0))
    tab_spec = pl.BlockSpec((tm, LANE), lambda t: (t, 0))
    part_specs = [pl.BlockSpec((d, tm // d, QKV_PAD), lambda t: (0, t, 0)) for d in GROUP_DIL]
    return _pcall(
        body, name="qkv_bwd", grid=(t_len // tm,),
        in_specs=part_specs + [pl.BlockSpec((3, D_MODEL, QKV_PAD), lambda t: (0, 0, 0)),
                               row_spec, row_spec, vec_spec, tab_spec, tab_spec],
        out_specs=[pl.BlockSpec((3, tm, QKV_PAD), lambda t: (0, t, 0)), row_spec, vec_spec],
        out_shape=[_sds((3, t_len, QKV_PAD), BF16), _sds((t_len, D_MODEL), F32), _sds((1, D_MODEL), F32)],
        scratch_shapes=[pltpu.VMEM((3, N_SLABS, tm, LANE), F32)],
        semantics=("arbitrary",),
    )(*dqkv_parts, w_pad, dxo, x, g_row, cos, sin)


def final_fwd_bwd(x, g_row, target):
    t_len = x.shape[0]
    tm = _tile_rows(t_len)

    def body(x_ref, g_ref, tgt_ref, dx_ref, dn_ref, loss_ref):
        @pl.when(pl.program_id(0) == 0)
        def _():
            dn_ref[...] = jnp.zeros_like(dn_ref)
            loss_ref[...] = jnp.zeros_like(loss_ref)

        x_t = x_ref[...]
        g = g_ref[...]
        diff = _rms_fwd(x_t, g) - tgt_ref[...]
        loss_ref[...] += 0.5 * jnp.sum(jnp.mean(diff * diff, axis=-1, keepdims=True), axis=0, keepdims=True)
        dx, dn = _rms_bwd(diff * (1.0 / D_MODEL), x_t, g)
        dx_ref[...] = dx
        dn_ref[...] += dn

    row_spec = pl.BlockSpec((tm, D_MODEL), lambda t: (t, 0))
    vec_spec = pl.BlockSpec((1, D_MODEL), lambda t: (0, 0))
    return _pcall(
        body, name="final_fwd_bwd", grid=(t_len // tm,),
        in_specs=[row_spec, vec_spec, row_spec],
        out_specs=[row_spec, vec_spec, pl.BlockSpec((1, 1), lambda t: (0, 0))],
        out_shape=[_sds((t_len, D_MODEL), F32), _sds((1, D_MODEL), F32), _sds((1, 1), F32)],
        semantics=("arbitrary",),
    )(x, g_row, target)


def pool_bwd(dxo, x, g_row, w_in, w_grp, scale, w_out, zr, duty=None):
    t_len = x.shape[0]
    tm = _tile_rows(t_len)
    nt = t_len // tm

    def body(dxo_ref, x_ref, g_ref, win_ref, wgrp_ref, scale_ref, wout_ref, zr_ref,
             dzs_ref, du_ref, dx_ref, dn_ref, dsc_ref, ebuf):
        i = pl.program_id(0)
        t = nt - 1 - i

        @pl.when(i == 0)
        def _():
            ebuf[pl.ds(tm, POOL_HALO), :] = jnp.zeros((POOL_HALO, D_MODEL), F32)
            dn_ref[...] = jnp.zeros_like(dn_ref)
            dsc_ref[...] = jnp.zeros_like(dsc_ref)

        dxo_t = dxo_ref[...]
        dz = _dot_nt(dxo_t.astype(BF16), wout_ref[...])
        dsc_ref[...] += jnp.sum(dz * zr_ref[...].astype(F32), axis=0, keepdims=True)
        dzs_ref[...] = (dz * scale_ref[...]).astype(BF16)
        row = t * tm + lax.broadcasted_iota(jnp.int32, (tm, 1), 0)
        for gi, w in enumerate(POOL_WINDOWS):
            cols = pl.ds(gi * POOL_GROUP_DIM, POOL_GROUP_DIM)
            dp_g = _dot_nt(dzs_ref[:, cols], wgrp_ref[gi])
            inv_cnt = 1.0 / jnp.minimum(row + 1, w).astype(F32)
            ebuf[pl.ds(0, tm), cols] = dp_g * inv_cnt
            acc = -dp_g
            for j in range(w):
                acc = acc + ebuf[pl.ds(j, tm), cols]
            du_ref[:, cols] = acc.astype(BF16)
        ebuf[pl.ds(tm, POOL_HALO), :] = ebuf[pl.ds(0, POOL_HALO), :]
        dh = _dot_nt(du_ref[...], win_ref[...])
        dx, dn = _rms_bwd(dh, x_ref[...], g_ref[...])
        dx_ref[...] = dxo_t + dx
        dn_ref[...] += dn

    row_spec = pl.BlockSpec((tm, D_MODEL), lambda i: (nt - 1 - i, 0))
    full = lambda shape: pl.BlockSpec(shape, lambda i: (0,) * len(shape))
    vec = full((1, D_MODEL))
    return _pcall(
        body, name="pool_bwd", grid=(nt,),
        in_specs=[row_spec, row_spec, vec, full((D_MODEL, D_MODEL)), full((4, POOL_GROUP_DIM, POOL_GROUP_DIM)),
                  vec, full((D_MODEL, D_MODEL)), row_spec],
        out_specs=[row_spec, row_spec, row_spec, vec, vec],
        out_shape=[_sds((t_len, D_MODEL), BF16), _sds((t_len, D_MODEL), BF16), _sds((t_len, D_MODEL), F32),
                   _sds((1, D_MODEL), F32), _sds((1, D_MODEL), F32)],
        scratch_shapes=[pltpu.VMEM((tm + POOL_HALO, D_MODEL), F32)],
        semantics=("arbitrary",), duty=duty,
    )(dxo, x, g_row, w_in, w_grp, scale, w_out, zr)


def _mesh_pos():
    return lax.axis_index("x"), lax.axis_index("y"), lax.axis_index("c")


def _other_chips(x, y):
    return [(1 - x, y), (x, 1 - y), (1 - x, 1 - y)]


def _remote(src, dst, send_sem, recv_sem, device):
    return pltpu.make_async_remote_copy(src_ref=src, dst_ref=dst, send_sem=send_sem, recv_sem=recv_sem,
                                        device_id=device, device_id_type=MESH)


class _Duty:
    aliases = {}

    def mid(self, ins, outs, sems):
        pass


class Together(_Duty):
    def __init__(self, duties):
        self.duties = duties
        self.ins = [a for d in duties for a in d.ins]
        self.out_shape = [s for d in duties for s in d.out_shape]
        self.scratch = [s for d in duties for s in d.scratch]
        self.aliases = {}
        i0 = o0 = 0
        for d in duties:
            self.aliases.update({i0 + i: o0 + o for i, o in d.aliases.items()})
            i0 += len(d.ins)
            o0 += len(d.out_shape)

    def _each(self, ins, outs, sems):
        i0 = o0 = s0 = 0
        for d in self.duties:
            ni, no, ns = len(d.ins), len(d.out_shape), len(d.scratch)
            yield d, ins[i0:i0 + ni], outs[o0:o0 + no], sems[s0:s0 + ns]
            i0, o0, s0 = i0 + ni, o0 + no, s0 + ns

    def split(self, outs):
        return [list(o) for _, _, o, _ in self._each(self.ins, outs, self.scratch)]

    def start(self, ins, outs, sems):
        for d, i, o, s in self._each(ins, outs, sems):
            d.start(i, o, s)

    def mid(self, ins, outs, sems):
        for d, i, o, s in self._each(ins, outs, sems):
            d.mid(i, o, s)

    def finish(self, ins, outs, sems):
        for d, i, o, s in self._each(ins, outs, sems):
            d.finish(i, o, s)


def run_duty(duty, name):
    d_in, d_out = len(duty.ins), len(duty.out_shape)

    def body(*refs):
        ins, outs, sems = refs[:d_in], refs[d_in:d_in + d_out], refs[d_in + d_out:]
        duty.start(ins, outs, sems)
        duty.mid(ins, outs, sems)
        duty.finish(ins, outs, sems)

    return pl.pallas_call(
        body, name=name, out_shape=list(duty.out_shape), in_specs=[_ANY] * d_in, out_specs=[_ANY] * d_out,
        scratch_shapes=list(duty.scratch), input_output_aliases=dict(duty.aliases),
        compiler_params=pltpu.CompilerParams(has_side_effects=True),
    )(*duty.ins)


class GatherWeights(_Duty):
    def __init__(self, shards):
        n = self.n = len(shards)
        self.halves = [s.shape[0] // 2 for s in shards]
        my_slot = 2 * lax.axis_index("x") + lax.axis_index("y")
        staged = [lax.dynamic_update_slice(lax.empty((N_SHARDS,) + s.shape, s.dtype), s[None], (my_slot, 0, 0))
                  for s in shards]
        self.ins = list(shards) + staged
        self.out_shape = [_sds((N_SHARDS,) + s.shape, s.dtype) for s in shards]
        self.aliases = {n + a: a for a in range(n)}
        self.scratch = [pltpu.SemaphoreType.DMA((n, 6)), pltpu.SemaphoreType.DMA((n, 6))]

    def _over_ici(self, ins, outs, sems):
        x, y, c = _mesh_pos()
        return [_remote(ins[a].at[pl.ds(c * h, h)], outs[a].at[2 * x + y, pl.ds(c * h, h)],
                        sems[0].at[a, j], sems[1].at[a, j], (*chip, c))
                for a, h in enumerate(self.halves) for j, chip in enumerate(_other_chips(x, y))]

    def _forwards(self, outs, sems, half_of):
        x, y, c = _mesh_pos()
        cps = []
        for a, h in enumerate(self.halves):
            for j, chip in enumerate(_other_chips(x, y)):
                slot = outs[a].at[2 * chip[0] + chip[1], pl.ds(half_of(c) * h, h)]
                cps.append(_remote(slot, slot, sems[0].at[a, 3 + j], sems[1].at[a, 3 + j], (x, y, 1 - c)))
        return cps

    def start(self, ins, outs, sems):
        for cp in self._over_ici(ins, outs, sems):
            cp.start()

    def mid(self, ins, outs, sems):
        x, y, c = _mesh_pos()
        forwards = self._forwards(outs, sems, lambda core: core)
        k = 0
        for a, h in enumerate(self.halves):
            for j, chip in enumerate(_other_chips(x, y)):
                slot = outs[a].at[2 * chip[0] + chip[1], pl.ds(c * h, h)]
                _remote(slot, slot, sems[0].at[a, j], sems[1].at[a, j], (*chip, c)).wait_recv()
                forwards[k].start()
                k += 1

    def finish(self, ins, outs, sems):
        for cp in self._forwards(outs, sems, lambda core: 1 - core):
            cp.wait_recv()
        for cp in self._over_ici(ins, outs, sems) + self._forwards(outs, sems, lambda core: core):
            cp.wait_send()


class GradReducer:
    def __init__(self, c_idx, pos_idx):
        self.c_idx, self.pos_idx = c_idx, pos_idx
        self.in_flight = []
        self.done = {}

    def push(self, name, grad):
        self.in_flight.append(dict(name=name, stage="halves", data=grad))

    def _duties(self):
        make = {"halves": SiblingHalves, "exchange": ChipExchange, "share": SiblingShare}
        return Together([make[w["stage"]]([w["data"]]) for w in self.in_flight])

    def _advance(self, duties, outs):
        still = []
        for w, (res,) in zip(self.in_flight, duties.split(outs)):
            if w["stage"] == "halves":
                partial = add_my_half(w["data"], res, self.c_idx, f"rs_add_{w['name']}")
                still.append(dict(name=w["name"], stage="exchange", data=partial))
            elif w["stage"] == "exchange":
                reduced = sum_slots(res, w["data"], self.pos_idx, f"rs_sum_{w['name']}")
                still.append(dict(name=w["name"], stage="share", data=reduced))
            else:
                self.done[w["name"]] = res
        self.in_flight = still

    def carried_by(self, fn, *args, **kw):
        if not self.in_flight:
            return fn(*args, **kw)
        duties = self._duties()
        out, duty_outs = fn(*args, duty=duties, **kw)
        self._advance(duties, duty_outs)
        return out

    def drain(self, name):
        step = 0
        while self.in_flight:
            duties = self._duties()
            self._advance(duties, run_duty(duties, f"{name}{step}"))
            step += 1


class SiblingHalves(_Duty):
    def __init__(self, grads):
        n = len(grads)
        self.halves = [g.shape[1] // 2 for g in grads]
        self.ins = list(grads)
        self.out_shape = [_sds((N_SHARDS, h, g.shape[2]), g.dtype) for g, h in zip(grads, self.halves)]
        self.scratch = [pltpu.SemaphoreType.DMA((n,)), pltpu.SemaphoreType.DMA((n,))]

    def _copies(self, ins, outs, sems):
        x, y, c = _mesh_pos()
        return [_remote(ins[a].at[:, pl.ds((1 - c) * h, h)], outs[a], sems[0].at[a], sems[1].at[a], (x, y, 1 - c))
                for a, h in enumerate(self.halves)]

    def start(self, ins, outs, sems):
        for cp in self._copies(ins, outs, sems):
            cp.start()

    def finish(self, ins, outs, sems):
        for cp in self._copies(ins, outs, sems):
            cp.wait()


class ChipExchange(_Duty):
    def __init__(self, parts):
        n = self.n = len(parts)
        self.ins = list(parts)
        self.out_shape = [_sds(p.shape, p.dtype) for p in parts]
        self.scratch = [pltpu.SemaphoreType.DMA((n, 3)), pltpu.SemaphoreType.DMA((n, 3))]

    def _copies(self, ins, outs, sems, arriving):
        x, y, c = _mesh_pos()
        cps = []
        for a in range(self.n):
            for j, chip in enumerate(_other_chips(x, y)):
                theirs = 2 * chip[0] + chip[1]
                src = outs[a].at[theirs] if arriving else ins[a].at[theirs]
                dst = outs[a].at[theirs] if arriving else outs[a].at[2 * x + y]
                cps.append(_remote(src, dst, sems[0].at[a, j], sems[1].at[a, j], (*chip, c)))
        return cps

    def start(self, ins, outs, sems):
        for cp in self._copies(ins, outs, sems, False):
            cp.start()

    def finish(self, ins, outs, sems):
        for cp in self._copies(ins, outs, sems, True):
            cp.wait_recv()
        for cp in self._copies(ins, outs, sems, False):
            cp.wait_send()


class SiblingShare(_Duty):
    def __init__(self, reduced):
        n = self.n = len(reduced)
        self.ins = list(reduced)
        self.out_shape = [_sds(r.shape, r.dtype) for r in reduced]
        self.aliases = {a: a for a in range(n)}
        self.scratch = [pltpu.SemaphoreType.DMA((n,)), pltpu.SemaphoreType.DMA((n,))]

    def _copies(self, outs, sems, half_of):
        x, y, c = _mesh_pos()
        cps = []
        for a in range(self.n):
            h = outs[a].shape[0] // 2
            rows = outs[a].at[pl.ds(half_of(c) * h, h)]
            cps.append(_remote(rows, rows, sems[0].at[a], sems[1].at[a], (x, y, 1 - c)))
        return cps

    def start(self, ins, outs, sems):
        for cp in self._copies(outs, sems, lambda core: core):
            cp.start()

    def finish(self, ins, outs, sems):
        for cp in self._copies(outs, sems, lambda core: 1 - core):
            cp.wait_recv()
        for cp in self._copies(outs, sems, lambda core: core):
            cp.wait_send()


def allreduce_small(v):
    def body(v_ref, o_ref, buf, send_sems, recv_sems):
        x, y, c = _mesh_pos()
        me = 4 * x + 2 * y + c
        buf[me] = v_ref[...]
        flip = lambda p, f: 1 - p if f else p
        peers = [(flip(x, k & 4), flip(y, k & 2), flip(c, k & 1)) for k in range(1, N_DEV)]
        cps = []
        for k, peer in enumerate(peers):
            cp = _remote(v_ref, buf.at[me], send_sems.at[k], recv_sems.at[k], peer)
            cp.start()
            cps.append(cp)
        for k, peer in enumerate(peers):
            slot = buf.at[4 * peer[0] + 2 * peer[1] + peer[2]]
            _remote(slot, slot, send_sems.at[k], recv_sems.at[k], peer).wait_recv()
        for cp in cps:
            cp.wait_send()
        acc = buf[0]
        for i in range(1, N_DEV):
            acc = acc + buf[i]
        o_ref[...] = acc

    vm = pl.BlockSpec(memory_space=pltpu.VMEM)
    return pl.pallas_call(
        body, name="allreduce_small", out_shape=_sds(v.shape, v.dtype), in_specs=[vm], out_specs=vm,
        scratch_shapes=[pltpu.VMEM((N_DEV,) + v.shape, v.dtype),
                        pltpu.SemaphoreType.DMA((N_DEV - 1,)), pltpu.SemaphoreType.DMA((N_DEV - 1,))],
        compiler_params=pltpu.CompilerParams(has_side_effects=True),
    )(v)


def add_my_half(grad, theirs, c_idx, name):
    _, r, cols = grad.shape
    h = r // 2

    def body(c_ref, g_ref, t_ref, o_ref):
        o_ref[...] = (g_ref[...] + t_ref[...]).astype(BF16)

    slot = pl.BlockSpec((None, h, cols), lambda s, c: (s, 0, 0))
    grid_spec = pltpu.PrefetchScalarGridSpec(
        num_scalar_prefetch=1, grid=(N_SHARDS,),
        in_specs=[pl.BlockSpec((None, h, cols), lambda s, c: (s, c[0], 0)), slot], out_specs=slot)
    return pl.pallas_call(
        body, name=name, grid_spec=grid_spec, out_shape=_sds((N_SHARDS, h, cols), BF16),
        compiler_params=pltpu.CompilerParams(dimension_semantics=("arbitrary",), vmem_limit_bytes=VMEM_LIMIT_BYTES),
    )(c_idx, grad, theirs)


def sum_slots(received, mine, pos_idx, name):
    _, h, cols = received.shape

    def body(pos_ref, r_ref, m_ref, o_ref):
        acc = None
        for k in range(N_SHARDS):
            term = jnp.where(pos_ref[0] == k, m_ref[k], r_ref[k]).astype(F32)
            acc = term if acc is None else acc + term
        o_ref[...] = acc

    whole = pl.BlockSpec((N_SHARDS, h, cols), lambda i, pos: (0, 0, 0))
    grid_spec = pltpu.PrefetchScalarGridSpec(
        num_scalar_prefetch=1, grid=(1,), in_specs=[whole, whole],
        out_specs=pl.BlockSpec((h, cols), lambda i, pos: (pos[1], 0)))
    return pl.pallas_call(
        body, name=name, grid_spec=grid_spec, out_shape=_sds((2 * h, cols), F32),
        compiler_params=pltpu.CompilerParams(dimension_semantics=("arbitrary",), vmem_limit_bytes=VMEM_LIMIT_BYTES),
    )(pos_idx, received, mine)


def adamw(name, grads, w, m, v):
    n_layers, r, cols = w.shape
    tr = r // 2 if r % 16 == 0 else r
    bias1 = 1.0 - ADAM_B1 ** ADAM_STEP
    bias2 = 1.0 - ADAM_B2 ** ADAM_STEP

    def body(*refs):
        g_refs = refs[:n_layers]
        w_ref, m_ref, v_ref, go_ref, d_ref, mo_ref, vo_ref = refs[n_layers:]
        g = g_refs[0][...]
        for layer in range(1, n_layers):
            g = jnp.where(pl.program_id(0) == layer, g_refs[layer][...], g)
        m_new = ADAM_B1 * m_ref[...] + (1.0 - ADAM_B1) * g
        v_new = ADAM_B2 * v_ref[...] + (1.0 - ADAM_B2) * (g * g)
        m_hat = m_new / bias1
        v_hat = v_new / bias2
        go_ref[...] = g
        d_ref[...] = -ADAM_LR * (m_hat / (jnp.sqrt(v_hat) + ADAM_EPS) + ADAM_WD * w_ref[...])
        mo_ref[...] = m_new
        vo_ref[...] = v_new

    g_spec = pl.BlockSpec((tr, cols), lambda l, i: (i, 0))
    lay_spec = pl.BlockSpec((None, tr, cols), lambda l, i: (l, i, 0))
    shape = _sds((n_layers, r, cols), F32)
    return _pcall(
        body, name=name, grid=(n_layers, r // tr),
        in_specs=[g_spec] * n_layers + [lay_spec] * 3, out_specs=[lay_spec] * 4,
        out_shape=[shape] * 4, semantics=("arbitrary", "arbitrary"),
    )(*grads, w, m, v)


def kernel(x, norm_mix, norm_ffn, norm_final, pool_w_in, pool_w_group, pool_scale, pool_w_out, attn_w_qkv, attn_w_out, ffn_w_gate, ffn_w_up, ffn_w_down, loss_target, m_norm_mix, m_norm_ffn, m_norm_final, m_pool_w_in, m_pool_w_group, m_pool_scale, m_pool_w_out, m_attn_w_qkv, m_attn_w_out, m_ffn_w_gate, m_ffn_w_up, m_ffn_w_down, v_norm_mix, v_norm_ffn, v_norm_final, v_pool_w_in, v_pool_w_group, v_pool_scale, v_pool_w_out, v_attn_w_qkv, v_attn_w_out, v_ffn_w_gate, v_ffn_w_up, v_ffn_w_down):
    t_len = x.shape[1]
    x0 = x.reshape(t_len, D_MODEL)
    target = loss_target.reshape(t_len, D_MODEL)
    row = lambda a: a.reshape(1, D_MODEL)

    grp_rows = POOL_GROUP_DIM // N_SHARDS
    bf = lambda a: a.astype(BF16)
    gate_t, up_t = jnp.swapaxes(ffn_w_gate, 1, 2), jnp.swapaxes(ffn_w_up, 1, 2)
    pool_shards = [bf(pool_w_in[0]), bf(pool_w_group[0].reshape(4 * grp_rows, POOL_GROUP_DIM)), bf(pool_w_out[0])]
    ffn0_shards = [bf(gate_t[0]), bf(up_t[0]), bf(ffn_w_down[0])]
    late_shards = [bf(attn_w_qkv[0]), bf(attn_w_out[0]), bf(gate_t[1]), bf(up_t[1]), bf(ffn_w_down[1])]
    cos, sin = rope_tables(t_len)
    c_idx = lax.axis_index("c").astype(jnp.int32).reshape(1)
    pos_idx = jnp.stack([2 * lax.axis_index("x") + lax.axis_index("y"), lax.axis_index("c")]).astype(jnp.int32)
    chip_rows = lambda g: g.reshape(N_SHARDS, D_MODEL // N_SHARDS, D_MODEL)

    g_pool = run_duty(GatherWeights(pool_shards), "gather_pool")
    w_in = g_pool[0].reshape(D_MODEL, D_MODEL)
    w_grp = g_pool[1].reshape(N_SHARDS, 4, grp_rows, POOL_GROUP_DIM).transpose(1, 0, 2, 3).reshape(
        4, POOL_GROUP_DIM, POOL_GROUP_DIM)
    w_out = g_pool[2].reshape(D_MODEL, D_MODEL)
    (h0, p, zr, z, x1), ffn0 = pool_fwd(x0, row(norm_mix[0]), w_in, w_grp, pool_scale, w_out,
                                        duty=GatherWeights(ffn0_shards))
    (h1, gate0, up0, act0, x2), late = ffn_fwd(x1, row(norm_ffn[0]), *ffn0, "ffn_fwd0",
                                               duty=GatherWeights(late_shards))
    w_qkv = pad_qkv_weight(late[0])
    w_ao = jnp.concatenate(pad_groups(late[1].reshape(D_MODEL, D_MODEL), 0), axis=0)
    ffn1 = late[2:5]
    h2, *qkv_parts = qkv_fwd(x2, row(norm_mix[1]), w_qkv, cos, sin)
    o_parts, lse_parts = [], []
    for gi in range(3):
        o_g, lse_g = attn_fwd(qkv_parts[gi], gi, f"attn_fwd_g{gi}")
        o_parts.append(o_g)
        lse_parts.append(lse_g)
    x3, merged, o_nat, lse_nat = attn_out_fwd(x2, o_parts, lse_parts, w_ao)
    h3, gate1, up1, act1, x4 = ffn_fwd(x3, row(norm_ffn[1]), *ffn1, "ffn_fwd1")
    dx4, d_norm_final, loss_local = final_fwd_bwd(x4, row(norm_final), target)

    red = GradReducer(c_idx, pos_idx)
    dgate1, dup1, dx3, d_nf1 = ffn_bwd(dx4, x3, row(norm_ffn[1]), gate1, up1, *ffn1, "ffn_bwd1")
    g_gate1 = wgrad_row_sharded("wgrad_gate1", dgate1, h3)
    g_up1 = wgrad_row_sharded("wgrad_up1", dup1, h3)
    g_down1 = wgrad_row_sharded("wgrad_down1", act1, dx4)
    red.push("gate1", g_gate1)
    red.push("up1", g_up1)
    do_parts, c_parts = red.carried_by(attn_out_bwd, dx3, w_ao, o_nat, lse_nat)
    g_ao = wgrad_full("wgrad_attn_out", merged, dx3)
    red.push("down1", g_down1)
    red.push("attn_out", chip_rows(unpad_groups(jnp.split(g_ao, 3, axis=0), 0)))
    dqkv_parts = [red.carried_by(attn_bwd, qkv_parts[gi], do_parts[gi], lse_parts[gi], c_parts[gi], gi,
                                 f"attn_bwd_g{gi}") for gi in range(3)]
    dqkv, dx2, d_nm1 = qkv_bwd(dqkv_parts, w_qkv, dx3, x2, row(norm_mix[1]), cos, sin)
    red.push("qkv", unpad_qkv_grad(wgrad_col_sharded("wgrad_qkv", h2, dqkv)))

    dgate0, dup0, dx1, d_nf0 = red.carried_by(ffn_bwd, dx2, x1, row(norm_ffn[0]), gate0, up0, *ffn0, "ffn_bwd0")
    red.push("gate0", red.carried_by(wgrad_row_sharded, "wgrad_gate0", dgate0, h1))
    red.push("up0", red.carried_by(wgrad_row_sharded, "wgrad_up0", dup0, h1))
    red.push("down0", red.carried_by(wgrad_row_sharded, "wgrad_down0", act0, dx2))
    red.push("pool_out", chip_rows(red.carried_by(wgrad_full, "wgrad_pool_out", z, dx1)))
    dzs, du, dx0, d_nm0, d_scale = pool_bwd(dx1, x0, row(norm_mix[0]), w_in, w_grp, pool_scale, w_out, zr)
    g_grp = red.carried_by(wgrad_pool_group, "wgrad_pool_group", p, dzs)
    red.push("pool_group", g_grp.reshape(N_SHARDS, 4 * grp_rows, POOL_GROUP_DIM))
    red.push("pool_in", chip_rows(red.carried_by(wgrad_full, "wgrad_pool_in", h0, du)))
    red.drain("rs_tail")
    full = [red.done[nm] for nm in ("pool_in", "pool_group", "pool_out", "qkv", "attn_out",
                                    "gate0", "gate1", "up0", "up1", "down0", "down1")]

    zero_row = jnp.zeros((1, D_MODEL), F32)
    small = jnp.concatenate([d_nm0, d_nm1, d_nf0, d_nf1, d_norm_final, d_scale,
                             jnp.broadcast_to(loss_local, (1, D_MODEL)), zero_row], axis=0)
    small = allreduce_small(small)
    loss = small[6, 0]

    pack = lambda a, b, c, d: jnp.concatenate([a, b, row(c), d, zero_row, zero_row], axis=0)[None]
    sg, sd, sm, sv = adamw("adamw_small", [small],
                           pack(norm_mix, norm_ffn, norm_final, pool_scale),
                           pack(m_norm_mix, m_norm_ffn, m_norm_final, m_pool_scale),
                           pack(v_norm_mix, v_norm_ffn, v_norm_final, v_pool_scale))
    unpack = lambda a: (a[0, 0:2], a[0, 2:4], a[0, 4], a[0, 5:6])

    def update(name, grads, w, m, v, transposed=False):
        if transposed:
            w, m, v = (jnp.swapaxes(a, 1, 2) for a in (w, m, v))
        n_layers = len(grads)
        shp = (n_layers,) + grads[0].shape
        outs = [o.reshape(w.shape) for o in adamw(name, grads, w.reshape(shp), m.reshape(shp), v.reshape(shp))]
        return [jnp.swapaxes(o, 1, 2) for o in outs] if transposed else outs

    big = [
        update("adamw_pool_in", [full[0]], pool_w_in, m_pool_w_in, v_pool_w_in),
        update("adamw_pool_group", [full[1]], pool_w_group, m_pool_w_group, v_pool_w_group),
        update("adamw_pool_out", [full[2]], pool_w_out, m_pool_w_out, v_pool_w_out),
        update("adamw_qkv", [full[3]], attn_w_qkv, m_attn_w_qkv, v_attn_w_qkv),
        update("adamw_attn_out", [full[4]], attn_w_out, m_attn_w_out, v_attn_w_out),
        update("adamw_gate", [full[5], full[6]], ffn_w_gate, m_ffn_w_gate, v_ffn_w_gate, transposed=True),
        update("adamw_up", [full[7], full[8]], ffn_w_up, m_ffn_w_up, v_ffn_w_up, transposed=True),
        update("adamw_down", [full[9], full[10]], ffn_w_down, m_ffn_w_down, v_ffn_w_down),
    ]

    def leaves(k, small_vals):
        nm, nf, nfin, psc = unpack(small_vals)
        return [nm, nf, nfin, big[0][k], big[1][k], psc, big[2][k], big[3][k], big[4][k],
                big[5][k], big[6][k], big[7][k]]

    grad_x = dx0.reshape(x.shape)
    return (loss, grad_x, *leaves(0, sg), *leaves(1, sd), *leaves(2, sm), *leaves(3, sv))
```

```python
import math

import jax
import jax.numpy as jnp
from jax import lax
from jax.experimental import pallas as pl
from jax.experimental.pallas import tpu as pltpu

F32 = jnp.float32
BF16 = jnp.bfloat16

D_MODEL = 1024
N_SHARDS = 4
N_DEV = 8
D_FF = 2816
FF_SHARD = D_FF // N_SHARDS
HEAD_DIM = 64
QKV_SHARD = 3 * D_MODEL // N_SHARDS
POOL_WINDOWS = (2, 4, 8, 16)
POOL_GROUP_DIM = 256
POOL_HALO = 32
ATTN_W = 128
GROUP_LANES = (0, 384, 704, 1024)
GROUP_HEADS = (6, 5, 5)
GROUP_DIL = (1, 4, 16)
ROPE_THETA = 10000.0
EPS = 1e-6
NEG_INF = -1e30
LANE = 128
VMEM_LIMIT_BYTES = 60 * 1024 * 1024

ADAM_LR = 0.001
ADAM_B1 = 0.9
ADAM_B2 = 0.999
ADAM_EPS = 1e-08
ADAM_WD = 0.01
ADAM_STEP = 10

NT_DIMS = (((1,), (1,)), ((), ()))
TN_DIMS = (((0,), (0,)), ((), ()))
MESH = pl.DeviceIdType.MESH


_ANY = pl.BlockSpec(memory_space=pl.ANY)


def _pcall(body, *, name, out_shape, grid=None, in_specs=None, out_specs=None, scratch_shapes=(),
           semantics=None, duty=None):
    kw = {}
    if in_specs is not None and duty is None:
        kw["in_specs"] = in_specs
    if out_specs is not None and duty is None:
        kw["out_specs"] = out_specs
    if grid is not None:
        kw["grid"] = grid
    params = dict(dimension_semantics=semantics, vmem_limit_bytes=VMEM_LIMIT_BYTES)
    if duty is None:
        return pl.pallas_call(body, name=name, out_shape=out_shape, scratch_shapes=list(scratch_shapes),
                              compiler_params=pltpu.CompilerParams(**params), **kw)

    single = not isinstance(out_shape, (list, tuple))
    c_out_shape = [out_shape] if single else list(out_shape)
    c_out_specs = [out_specs] if single else list(out_specs)
    n_in, n_out, n_scr = len(in_specs), len(c_out_shape), len(scratch_shapes)
    d_in, d_out = len(duty.ins), len(duty.out_shape)
    total = math.prod(grid)
    mid_step = (5 * total) // 6

    def wrapped(*refs):
        c_in, d_ins = refs[:n_in], refs[n_in:n_in + d_in]
        o0 = n_in + d_in
        c_outs, d_outs = refs[o0:o0 + n_out], refs[o0 + n_out:o0 + n_out + d_out]
        s0 = o0 + n_out + d_out
        c_scr, d_sems = refs[s0:s0 + n_scr], refs[s0 + n_scr:]
        step = pl.program_id(0)
        for ax in range(1, len(grid)):
            step = step * grid[ax] + pl.program_id(ax)

        @pl.when(step == 0)
        def _():
            duty.start(d_ins, d_outs, d_sems)

        body(*c_in, *c_outs, *c_scr)

        @pl.when(step == mid_step)
        def _():
            duty.mid(d_ins, d_outs, d_sems)

        @pl.when(step == total - 1)
        def _():
            duty.finish(d_ins, d_outs, d_sems)

    call = pl.pallas_call(
        wrapped, name=name, grid=grid,
        in_specs=list(in_specs) + [_ANY] * d_in, out_specs=c_out_specs + [_ANY] * d_out,
        out_shape=c_out_shape + list(duty.out_shape),
        scratch_shapes=list(scratch_shapes) + list(duty.scratch),
        input_output_aliases={n_in + i: n_out + o for i, o in duty.aliases.items()},
        compiler_params=pltpu.CompilerParams(has_side_effects=True, **params))

    def run(*args):
        outs = call(*args, *duty.ins)
        c = outs[:n_out]
        return (c[0] if single else list(c)), list(outs[n_out:])

    return run


def _sds(shape, dtype):
    return jax.ShapeDtypeStruct(tuple(shape), dtype)


def _dot(a, b):
    return jnp.dot(a, b, preferred_element_type=F32)


def _dot_nt(a, b):
    return lax.dot_general(a, b, NT_DIMS, preferred_element_type=F32)


def _dot_tn(a, b):
    return lax.dot_general(a, b, TN_DIMS, preferred_element_type=F32)


def _rms_fwd(x, g):
    r = lax.rsqrt(jnp.mean(x * x, axis=-1, keepdims=True) + EPS)
    return x * r * g


def _rms_bwd(dh, x, g):
    r = lax.rsqrt(jnp.mean(x * x, axis=-1, keepdims=True) + EPS)
    xh = x * r
    dg = jnp.sum(dh * xh, axis=0, keepdims=True)
    dxh = dh * g
    dx = r * (dxh - xh * jnp.mean(dxh * xh, axis=-1, keepdims=True))
    return dx, dg


def _sigmoid(x):
    return 0.5 * jnp.tanh(0.5 * x) + 0.5


def _tile_rows(t):
    return min(512, t)


def _sub_tiles(tm, n_sub=2):
    rows = tm // n_sub
    return [pl.ds(i * rows, rows) for i in range(n_sub)]


def _wgrad_rows(t):
    return min(2048, t)


def _window_sums(buf, cols, tmps, levels, tm, back):
    src, src_cols = buf, cols
    for k in range(1, levels + 1):
        shift = 1 << (k - 1)
        last = k == levels
        if back:
            lo = POOL_HALO if last else 8 * k
            n = tm + POOL_HALO - lo
            val = src[pl.ds(lo, n), src_cols] + src[pl.ds(lo - shift, n), src_cols]
        else:
            lo = 0
            n = tm if last else tm + POOL_HALO - 8 * k
            val = src[pl.ds(0, n), src_cols] + src[pl.ds(shift, n), src_cols]
        if last:
            return val
        tmps[k % 2][pl.ds(lo, n), :] = val
        src, src_cols = tmps[k % 2], slice(None)
def pool_fwd(x, g_row, w_in, w_grp, scale, w_out, duty=None):
    t_len = x.shape[0]
    tm = _tile_rows(t_len)

    def body(x_ref, g_ref, win_ref, wgrp_ref, scale_ref, wout_ref,
             h_ref, p_ref, zr_ref, z_ref, xo_ref, ubuf, tmp_a, tmp_b):
        t = pl.program_id(0)

        @pl.when(t == 0)
        def _():
            ubuf[pl.ds(0, POOL_HALO), :] = jnp.zeros((POOL_HALO, D_MODEL), F32)

        x_t = x_ref[...]
        h = _rms_fwd(x_t, g_ref[...]).astype(BF16)
        h_ref[...] = h
        ubuf[pl.ds(POOL_HALO, tm), :] = _dot(h, win_ref[...])
        row = t * tm + lax.broadcasted_iota(jnp.int32, (tm, 1), 0)
        for gi, w in enumerate(POOL_WINDOWS):
            cols = pl.ds(gi * POOL_GROUP_DIM, POOL_GROUP_DIM)
            u_g = ubuf[pl.ds(POOL_HALO, tm), cols]
            acc = _window_sums(ubuf, cols, (tmp_a, tmp_b), gi + 1, tm, back=True)
            inv_cnt = 1.0 / jnp.minimum(row + 1, w).astype(F32)
            p_g = (acc * inv_cnt - u_g).astype(BF16)
            p_ref[:, cols] = p_g
            z_g = _dot(p_g, wgrp_ref[gi])
            zr_ref[:, cols] = z_g.astype(BF16)
            z_ref[:, cols] = (z_g * scale_ref[:, cols]).astype(BF16)
        ubuf[pl.ds(0, POOL_HALO), :] = ubuf[pl.ds(tm, POOL_HALO), :]
        xo_ref[...] = x_t + _dot(z_ref[...], wout_ref[...])

    row_spec = pl.BlockSpec((tm, D_MODEL), lambda t: (t, 0))
    full2 = lambda shape: pl.BlockSpec(shape, lambda t: (0,) * len(shape))
    return _pcall(
        body, name="pool_fwd", grid=(t_len // tm,),
        in_specs=[row_spec, full2((1, D_MODEL)), full2((D_MODEL, D_MODEL)),
                  full2((4, POOL_GROUP_DIM, POOL_GROUP_DIM)), full2((1, D_MODEL)), full2((D_MODEL, D_MODEL))],
        out_specs=[row_spec] * 5,
        out_shape=[_sds((t_len, D_MODEL), BF16)] * 4 + [_sds((t_len, D_MODEL), F32)],
        scratch_shapes=[pltpu.VMEM((tm + POOL_HALO, D_MODEL), F32)]
                       + [pltpu.VMEM((tm + POOL_HALO, POOL_GROUP_DIM), F32)] * 2,
        semantics=("arbitrary",), duty=duty,
    )(x, g_row, w_in, w_grp, scale, w_out)


def ffn_fwd(x, g_row, w_gate_t, w_up_t, w_down, name, duty=None):
    t_len = x.shape[0]
    tm = min(1024, t_len)

    def body(x_ref, g_ref, wg_ref, wu_ref, wd_ref, h_ref, go_ref, uo_ref, ao_ref, xo_ref, hbuf, acc):
        s = pl.program_id(1)

        @pl.when(s == 0)
        def _():
            h = _rms_fwd(x_ref[...], g_ref[...]).astype(BF16)
            hbuf[...] = h
            h_ref[...] = h
            acc[...] = jnp.zeros_like(acc)

        h = hbuf[...]
        gate = _dot_nt(h, wg_ref[...])
        up = _dot_nt(h, wu_ref[...])
        go_ref[...] = gate.astype(BF16)
        uo_ref[...] = up.astype(BF16)
        act = (gate * _sigmoid(gate) * up).astype(BF16)
        ao_ref[...] = act
        acc[...] += _dot(act, wd_ref[...])

        @pl.when(s == N_SHARDS - 1)
        def _():
            xo_ref[...] = x_ref[...] + acc[...]

    row_spec = pl.BlockSpec((tm, D_MODEL), lambda t, s: (t, 0))
    row_w = pl.BlockSpec((None, FF_SHARD, D_MODEL), lambda t, s: (s, 0, 0))
    act_spec = pl.BlockSpec((None, tm, FF_SHARD), lambda t, s: (s, t, 0))
    return _pcall(
        body, name=name, grid=(t_len // tm, N_SHARDS),
        in_specs=[row_spec, pl.BlockSpec((1, D_MODEL), lambda t, s: (0, 0)), row_w, row_w, row_w],
        out_specs=[row_spec, act_spec, act_spec, act_spec, row_spec],
        out_shape=[_sds((t_len, D_MODEL), BF16)] + [_sds((N_SHARDS, t_len, FF_SHARD), BF16)] * 3
                  + [_sds((t_len, D_MODEL), F32)],
        scratch_shapes=[pltpu.VMEM((tm, D_MODEL), BF16), pltpu.VMEM((tm, D_MODEL), F32)],
        semantics=("arbitrary", "arbitrary"), duty=duty,
    )(x, g_row, w_gate_t, w_up_t, w_down)


def ffn_bwd(dxo, x, g_row, gate, up, w_gate_t, w_up_t, w_down, name, duty=None):
    t_len = x.shape[0]
    tm = _tile_rows(t_len)

    def body(dxo_ref, x_ref, g_ref, gate_ref, up_ref, wg_ref, wu_ref, wd_ref,
             dg_ref, du_ref, dx_ref, dn_ref, dxb, dh):
        t = pl.program_id(0)
        s = pl.program_id(1)

        @pl.when(s == 0)
        def _():
            dxb[...] = dxo_ref[...].astype(BF16)
            dh[...] = jnp.zeros_like(dh)

        @pl.when(jnp.logical_and(s == 0, t == 0))
        def _():
            dn_ref[...] = jnp.zeros_like(dn_ref)

        sub_tiles = _sub_tiles(tm)
        dacts = [_dot_nt(dxb[rows, :], wd_ref[...]) for rows in sub_tiles]
        for rows, dact in zip(sub_tiles, dacts):
            gv = gate_ref[rows, :].astype(F32)
            uv = up_ref[rows, :].astype(F32)
            sg = _sigmoid(gv)
            dgv = (dact * uv * (sg * (1.0 + gv * (1.0 - sg)))).astype(BF16)
            duv = (dact * (gv * sg)).astype(BF16)
            dg_ref[rows, :] = dgv
            du_ref[rows, :] = duv
            dh[rows, :] += _dot(dgv, wg_ref[...]) + _dot(duv, wu_ref[...])

        @pl.when(s == N_SHARDS - 1)
        def _():
            dx, dn = _rms_bwd(dh[...], x_ref[...], g_ref[...])
            dx_ref[...] = dxo_ref[...] + dx
            dn_ref[...] += dn

    row_spec = pl.BlockSpec((tm, D_MODEL), lambda t, s: (t, 0))
    vec_spec = pl.BlockSpec((1, D_MODEL), lambda t, s: (0, 0))
    row_w = pl.BlockSpec((None, FF_SHARD, D_MODEL), lambda t, s: (s, 0, 0))
    act_spec = pl.BlockSpec((None, tm, FF_SHARD), lambda t, s: (s, t, 0))
    act_shape = _sds((N_SHARDS, t_len, FF_SHARD), BF16)
    return _pcall(
        body, name=name, grid=(t_len // tm, N_SHARDS),
        in_specs=[row_spec, row_spec, vec_spec, act_spec, act_spec, row_w, row_w, row_w],
        out_specs=[act_spec, act_spec, row_spec, vec_spec],
        out_shape=[act_shape, act_shape, _sds((t_len, D_MODEL), F32), _sds((1, D_MODEL), F32)],
        scratch_shapes=[pltpu.VMEM((tm, D_MODEL), BF16), pltpu.VMEM((tm, D_MODEL), F32)],
        semantics=("arbitrary", "arbitrary"), duty=duty,
    )(dxo, x, g_row, gate, up, w_gate_t, w_up_t, w_down)


def tn_matmul(name, a, b, a_spec, b_spec, out_shape, out_spec, grid, duty=None):
    def body(a_ref, b_ref, o_ref):
        @pl.when(pl.program_id(len(grid) - 1) == 0)
        def _():
            o_ref[...] = jnp.zeros_like(o_ref)

        res = _dot_tn(a_ref[...].astype(BF16), b_ref[...].astype(BF16))
        o_ref[...] += res.reshape(o_ref.shape)

    return _pcall(body, name=name, grid=grid, in_specs=[a_spec, b_spec], out_specs=out_spec,
                  out_shape=out_shape, semantics=("arbitrary",) * len(grid), duty=duty)(a, b)


def wgrad_full(name, a, b, duty=None):
    t_len, k = a.shape
    n = b.shape[1]
    tt = _wgrad_rows(t_len)
    return tn_matmul(name, a, b,
                     pl.BlockSpec((tt, k), lambda t: (t, 0)), pl.BlockSpec((tt, n), lambda t: (t, 0)),
                     _sds((k, n), F32), pl.BlockSpec((k, n), lambda t: (0, 0)), (t_len // tt,), duty)


def wgrad_col_sharded(name, a, b_sh, duty=None):
    t_len, k = a.shape
    n_sh, _, n = b_sh.shape
    tt = _wgrad_rows(t_len)
    return tn_matmul(name, a, b_sh,
                     pl.BlockSpec((tt, k), lambda s, t: (t, 0)), pl.BlockSpec((None, tt, n), lambda s, t: (s, t, 0)),
                     _sds((n_sh, k, n), F32), pl.BlockSpec((None, k, n), lambda s, t: (s, 0, 0)),
                     (n_sh, t_len // tt), duty)


def wgrad_row_sharded(name, a_sh, b, duty=None):
    t_len, n = b.shape
    n_sh, _, k = a_sh.shape
    tt = _wgrad_rows(t_len)

    def body(a_ref, b_ref, o_ref):
        s = pl.program_id(1)
        res = _dot_tn(a_ref[...], b_ref[...].astype(BF16))

        @pl.when(pl.program_id(0) == 0)
        def _():
            o_ref[s] = res

        @pl.when(pl.program_id(0) > 0)
        def _():
            o_ref[s] += res

    return _pcall(body, name=name, grid=(t_len // tt, n_sh),
                  in_specs=[pl.BlockSpec((None, tt, k), lambda t, s: (s, t, 0)),
                            pl.BlockSpec((tt, n), lambda t, s: (t, 0))],
                  out_specs=pl.BlockSpec((n_sh, k, n), lambda t, s: (0, 0, 0)),
                  out_shape=_sds((n_sh, k, n), F32), semantics=("arbitrary", "arbitrary"), duty=duty)(a_sh, b)


def wgrad_pool_group(name, p, dzs, duty=None):
    t_len = p.shape[0]
    tt = _wgrad_rows(t_len)
    gd = POOL_GROUP_DIM
    rows = gd // N_SHARDS
    return tn_matmul(name, p, dzs,
                     pl.BlockSpec((tt, gd), lambda g, t: (t, g)), pl.BlockSpec((tt, gd), lambda g, t: (t, g)),
                     _sds((N_SHARDS, 4, rows, gd), F32),
                     pl.BlockSpec((N_SHARDS, None, rows, gd), lambda g, t: (0, g, 0, 0)),
                     (4, t_len // tt), duty)


PAD_LANES = 384
QKV_PAD = 3 * PAD_LANES
N_SLABS = QKV_PAD // LANE
GROUP_REAL = tuple(GROUP_LANES[g + 1] - GROUP_LANES[g] for g in range(3))
Q_BLOCK = 512


def pad_groups(w, axis):
    parts = []
    for g in range(3):
        blk = lax.slice_in_dim(w, GROUP_LANES[g], GROUP_LANES[g + 1], axis=axis)
        pad = [(0, 0)] * w.ndim
        pad[axis] = (0, PAD_LANES - GROUP_REAL[g])
        parts.append(jnp.pad(blk, pad))
    return parts


def unpad_groups(parts, axis):
    return jnp.concatenate([lax.slice_in_dim(p, 0, GROUP_REAL[g], axis=axis) for g, p in enumerate(parts)],
                           axis=axis)


def _qkv_pieces(group, part):
    lo, hi = part * D_MODEL + GROUP_LANES[group], part * D_MODEL + GROUP_LANES[group + 1]
    pieces = []
    while lo < hi:
        shard = lo // QKV_SHARD
        end = min(hi, (shard + 1) * QKV_SHARD)
        pieces.append((shard, lo - shard * QKV_SHARD, end - shard * QKV_SHARD))
        lo = end
    return pieces


def pad_qkv_weight(w_qkv_sh):
    groups = []
    for g in range(3):
        cols = []
        for part in range(3):
            cols += [w_qkv_sh[s][:, lo:hi] for s, lo, hi in _qkv_pieces(g, part)]
            if GROUP_REAL[g] < PAD_LANES:
                cols.append(jnp.zeros((D_MODEL, PAD_LANES - GROUP_REAL[g]), w_qkv_sh.dtype))
        groups.append(jnp.concatenate(cols, axis=1))
    return jnp.stack(groups)


def unpad_qkv_grad(g_pad):
    shard_cols = [[] for _ in range(N_SHARDS)]
    for part in range(3):
        for g in range(3):
            at = part * PAD_LANES
            for s, lo, hi in _qkv_pieces(g, part):
                shard_cols[s].append(g_pad[g][:, at:at + hi - lo])
                at += hi - lo
    return jnp.stack([jnp.concatenate(cols, axis=1) for cols in shard_cols])


def rope_tables(t_len):
    inv_freq = 1.0 / (ROPE_THETA ** (jnp.arange(0, HEAD_DIM, 2, dtype=F32) / HEAD_DIM))
    ang = jnp.arange(t_len, dtype=F32)[:, None] * inv_freq[None, :]
    cos_h, sin_h = lax.optimization_barrier((jnp.cos(ang), jnp.sin(ang)))
    reps = (1, 2 * LANE // HEAD_DIM)
    return jnp.tile(cos_h, reps), jnp.tile(sin_h, reps)


def _rot_half(v):
    n = v.shape[1]
    lane = lax.broadcasted_iota(jnp.int32, v.shape, 1)
    return jnp.where(lane % HEAD_DIM < HEAD_DIM // 2,
                     -pltpu.roll(v, n - HEAD_DIM // 2, 1), pltpu.roll(v, HEAD_DIM // 2, 1))


def _lane_cols(j):
    return slice(j * LANE, (j + 1) * LANE)


def _to_residue_major(slab, j_src, dst_ref, j_dst, dil, rows):
    for r in range(dil):
        dst_ref[r, :, _lane_cols(j_dst)] = slab[j_src, pl.ds(r, rows // dil, stride=dil), :].astype(dst_ref.dtype)


def _to_natural(src_ref, j_src, slab, j_dst, dil, rows):
    for r in range(dil):
        slab[j_dst, pl.ds(r, rows // dil, stride=dil), :] = src_ref[r, :, _lane_cols(j_src)].astype(F32)


def qkv_fwd(x, g_row, w_pad, cos, sin):
    t_len = x.shape[0]
    tm = _tile_rows(t_len)

    def body(x_ref, g_ref, w_ref, cos_ref, sin_ref, h_ref, o1_ref, o4_ref, o16_ref, slabs):
        h = _rms_fwd(x_ref[...], g_ref[...]).astype(BF16)
        h_ref[...] = h
        accs = [_dot(h, w_ref[gi]) for gi in range(3)]
        cos_t = cos_ref[...]
        sin_t = sin_ref[...]
        for gi, (dil, o_ref) in enumerate(zip(GROUP_DIL, (o1_ref, o4_ref, o16_ref))):
            slab = slabs.at[gi]
            for j in range(N_SLABS):
                a = accs[gi][:, _lane_cols(j)]
                if j < 6:
                    a = a * cos_t + _rot_half(a) * sin_t
                if j < 3:
                    a = a * (HEAD_DIM ** -0.5)
                if dil == 1:
                    o_ref[0, :, _lane_cols(j)] = a.astype(BF16)
                else:
                    slab[j] = a
                    _to_residue_major(slab, j, o_ref, j, dil, tm)

    row_spec = pl.BlockSpec((tm, D_MODEL), lambda t: (t, 0))
    tab_spec = pl.BlockSpec((tm, LANE), lambda t: (t, 0))
    out_specs = [row_spec] + [pl.BlockSpec((d, tm // d, QKV_PAD), lambda t: (0, t, 0)) for d in GROUP_DIL]
    out_shape = [_sds((t_len, D_MODEL), BF16)] + [_sds((d, t_len // d, QKV_PAD), BF16) for d in GROUP_DIL]
    return _pcall(
        body, name="qkv_fwd", grid=(t_len // tm,),
        in_specs=[row_spec, pl.BlockSpec((1, D_MODEL), lambda t: (0, 0)),
                  pl.BlockSpec((3, D_MODEL, QKV_PAD), lambda t: (0, 0, 0)), tab_spec, tab_spec],
        out_specs=out_specs, out_shape=out_shape,
        scratch_shapes=[pltpu.VMEM((3, N_SLABS, tm, LANE), F32)],
        semantics=("arbitrary",),
    )(x, g_row, w_pad, cos, sin)


def _band_masks():
    qi = lax.broadcasted_iota(jnp.int32, (ATTN_W, 2 * ATTN_W), 0)
    kj = lax.broadcasted_iota(jnp.int32, (ATTN_W, 2 * ATTN_W), 1)
    dist = ATTN_W + qi - kj
    band = (dist >= 0) & (dist <= ATTN_W)
    return band, band & (kj >= ATTN_W)


def _half_masks():
    lane = lax.broadcasted_iota(jnp.int32, (1, LANE), 1)
    return [lane < HEAD_DIM, lane >= HEAD_DIM]


def _live_halves(gi, j):
    hms = _half_masks()
    return hms if (gi == 0 or j < 2) else hms[:1]


def attn_fwd(qkv_g, gi, name):
    dil, l_len, _ = qkv_g.shape
    qb = min(Q_BLOCK, l_len)
    nsub = qb // ATTN_W

    def body(q_ref, kc_ref, kp_ref, vc_ref, vp_ref, o_ref, lse_ref, kbuf, vbuf):
        n = pl.program_id(1)
        kbuf[pl.ds(0, ATTN_W), :] = kp_ref[...]
        kbuf[pl.ds(ATTN_W, qb), :] = kc_ref[...]
        vbuf[pl.ds(0, ATTN_W), :] = vp_ref[...]
        vbuf[pl.ds(ATTN_W, qb), :] = vc_ref[...]
        band, band_first = _band_masks()

        def sub(b, carry):
            r0 = pl.multiple_of(b * ATTN_W, ATTN_W)
            mask = band_first | (band & (n + b > 0))
            krows = pl.ds(r0, 2 * ATTN_W)
            scores = []
            for j in range(3):
                q = q_ref[pl.ds(r0, ATTN_W), _lane_cols(j)]
                for hm in _live_halves(gi, j):
                    scores.append(_dot_nt(jnp.where(hm, q, jnp.zeros_like(q)), kbuf[krows, _lane_cols(j)]))
            scores = iter(scores)
            head_lane = lax.broadcasted_iota(jnp.int32, (1, LANE), 1)
            lse = jnp.zeros((ATTN_W, LANE), F32)
            for j in range(3):
                cols = _lane_cols(j)
                v = vbuf[krows, cols]
                o = jnp.zeros((ATTN_W, LANE), F32)
                for half, hm in enumerate(_live_halves(gi, j)):
                    s = jnp.where(mask, next(scores), NEG_INF)
                    m = jnp.max(s, axis=-1, keepdims=True)
                    e = jnp.exp(s - m)
                    den = jnp.sum(e, axis=-1, keepdims=True)
                    p = (e * (1.0 / den)).astype(BF16)
                    o = jnp.where(hm, _dot(p, v), o)
                    lse = jnp.where(head_lane == 2 * j + half, m + jnp.log(den), lse)
                o_ref[pl.ds(r0, ATTN_W), cols] = o.astype(BF16)
            lse_ref[pl.ds(r0, ATTN_W), :] = lse
            return carry

        lax.fori_loop(0, nsub, sub, 0)

    cur = lambda c: pl.BlockSpec((None, qb, PAD_LANES), lambda r, n: (r, n, c))
    prev = lambda c: pl.BlockSpec((None, ATTN_W, PAD_LANES), lambda r, n: (r, jnp.maximum(n * nsub - 1, 0), c))
    return _pcall(
        body, name=name, grid=(dil, l_len // qb),
        in_specs=[cur(0), cur(1), prev(1), cur(2), prev(2)],
        out_specs=[pl.BlockSpec((None, qb, PAD_LANES), lambda r, n: (r, n, 0)),
                   pl.BlockSpec((None, qb, LANE), lambda r, n: (r, n, 0))],
        out_shape=[_sds((dil, l_len, PAD_LANES), BF16), _sds((dil, l_len, LANE), F32)],
        scratch_shapes=[pltpu.VMEM((qb + ATTN_W, PAD_LANES), BF16), pltpu.VMEM((qb + ATTN_W, PAD_LANES), BF16)],
        semantics=("arbitrary", "arbitrary"),
    )(qkv_g, qkv_g, qkv_g, qkv_g, qkv_g)


def _group_stats(lses):
    head_lane = lax.broadcasted_iota(jnp.int32, (1, LANE), 1)
    fulls, glse = [], []
    for g in range(3):
        real = head_lane < GROUP_HEADS[g]
        mx = jnp.max(jnp.where(real, lses[g], -jnp.inf), axis=-1, keepdims=True)
        sm = jnp.sum(jnp.where(real, jnp.exp(lses[g] - mx), 0.0), axis=-1, keepdims=True)
        fulls.append(mx + jnp.log(sm))
        glse.append(fulls[g] - math.log(GROUP_HEADS[g]))
    top = jnp.maximum(jnp.maximum(glse[0], glse[1]), glse[2])
    ex = [jnp.exp(v - top) for v in glse]
    tot = ex[0] + ex[1] + ex[2]
    alpha = [v / tot for v in ex]
    lane = lax.broadcasted_iota(jnp.int32, (1, QKV_PAD), 1)
    scale = jnp.where(lane < PAD_LANES, 3.0 * alpha[0],
                      jnp.where(lane < 2 * PAD_LANES, 3.0 * alpha[1], 3.0 * alpha[2]))
    return alpha, fulls, scale


def attn_out_fwd(x, o_parts, lse_parts, w_out_pad):
    t_len = x.shape[0]
    tm = _tile_rows(t_len)

    def body(x_ref, o1, o4, o16, l1, l4, l16, w_ref, xo_ref, mg_ref, o_ref, lse_ref, o_slab, l_slab):
        for gi, (dil, og, lg) in enumerate(zip(GROUP_DIL, (o1, o4, o16), (l1, l4, l16))):
            for j in range(3):
                _to_natural(og, j, o_slab, 3 * gi + j, dil, tm)
            _to_natural(lg, 0, l_slab, gi, dil, tm)
        o = jnp.concatenate([o_slab[j] for j in range(N_SLABS)], axis=1)
        lses = [l_slab[gi] for gi in range(3)]
        o_ref[...] = o.astype(BF16)
        for gi in range(3):
            lse_ref[:, _lane_cols(gi)] = lses[gi]
        _, _, scale = _group_stats(lses)
        merged = (o * scale).astype(BF16)
        mg_ref[...] = merged
        xo_ref[...] = x_ref[...] + _dot(merged, w_ref[...])

    row_spec = pl.BlockSpec((tm, D_MODEL), lambda t: (t, 0))
    pad_spec = pl.BlockSpec((tm, QKV_PAD), lambda t: (t, 0))
    o_specs = [pl.BlockSpec((d, tm // d, PAD_LANES), lambda t: (0, t, 0)) for d in GROUP_DIL]
    lse_specs = [pl.BlockSpec((d, tm // d, LANE), lambda t: (0, t, 0)) for d in GROUP_DIL]
    return _pcall(
        body, name="attn_out_fwd", grid=(t_len // tm,),
        in_specs=[row_spec] + o_specs + lse_specs + [pl.BlockSpec((QKV_PAD, D_MODEL), lambda t: (0, 0))],
        out_specs=[row_spec, pad_spec, pad_spec, pl.BlockSpec((tm, 3 * LANE), lambda t: (t, 0))],
        out_shape=[_sds((t_len, D_MODEL), F32), _sds((t_len, QKV_PAD), BF16),
                   _sds((t_len, QKV_PAD), BF16), _sds((t_len, 3 * LANE), F32)],
        scratch_shapes=[pltpu.VMEM((N_SLABS, tm, LANE), F32), pltpu.VMEM((3, tm, LANE), F32)],
        semantics=("arbitrary",),
    )(x, *o_parts, *lse_parts, w_out_pad)


def attn_out_bwd(dxo, w_out_pad, o, lse, duty=None):
    t_len = dxo.shape[0]
    tm = _tile_rows(t_len)

    def body(dx_ref, w_ref, o_ref, lse_ref, d1, d4, d16, c1, c4, c16, slab):
        dmerged = _dot_nt(dx_ref[...].astype(BF16), w_ref[...])
        o_t = o_ref[...].astype(F32)
        lses = [lse_ref[:, _lane_cols(gi)] for gi in range(3)]
        alpha, fulls, scale = _group_stats(lses)
        e = dmerged * o_t
        lane = lax.broadcasted_iota(jnp.int32, (1, QKV_PAD), 1)
        dalpha = [3.0 * jnp.sum(jnp.where((lane >= g * PAD_LANES) & (lane < g * PAD_LANES + GROUP_REAL[g]), e, 0.0),
                                axis=-1, keepdims=True) for g in range(3)]
        mean_da = alpha[0] * dalpha[0] + alpha[1] * dalpha[1] + alpha[2] * dalpha[2]
        dglse = [alpha[g] * (dalpha[g] - mean_da) for g in range(3)]
        do = dmerged * scale
        es = e * scale
        for j in range(N_SLABS):
            slab[j] = do[:, _lane_cols(j)]
        for gi, (dil, dg) in enumerate(zip(GROUP_DIL, (d1, d4, d16))):
            for j in range(3):
                _to_residue_major(slab, 3 * gi + j, dg, j, dil, tm)
        head_lane = lax.broadcasted_iota(jnp.int32, (1, LANE), 1)
        first = head_lane < HEAD_DIM
        for gi, (dil, cg) in enumerate(zip(GROUP_DIL, (c1, c4, c16))):
            c_g = -(dglse[gi] * jnp.exp(lses[gi] - fulls[gi]))
            for j in range(3):
                blk = es[:, _lane_cols(3 * gi + j)]
                halves = (jnp.sum(jnp.where(first, blk, 0.0), axis=-1, keepdims=True),
                          jnp.sum(jnp.where(first, 0.0, blk), axis=-1, keepdims=True))
                for half in range(2):
                    c_g = c_g + jnp.where(head_lane == 2 * j + half, halves[half], 0.0)
            slab[gi] = c_g
            _to_residue_major(slab, gi, cg, 0, dil, tm)

    row_spec = pl.BlockSpec((tm, D_MODEL), lambda t: (t, 0))
    pad_spec = pl.BlockSpec((tm, QKV_PAD), lambda t: (t, 0))
    do_specs = [pl.BlockSpec((d, tm // d, PAD_LANES), lambda t: (0, t, 0)) for d in GROUP_DIL]
    c_specs = [pl.BlockSpec((d, tm // d, LANE), lambda t: (0, t, 0)) for d in GROUP_DIL]
    outs = _pcall(
        body, name="attn_out_bwd", grid=(t_len // tm,),
        in_specs=[row_spec, pl.BlockSpec((QKV_PAD, D_MODEL), lambda t: (0, 0)), pad_spec,
                  pl.BlockSpec((tm, 3 * LANE), lambda t: (t, 0))],
        out_specs=do_specs + c_specs,
        out_shape=[_sds((d, t_len // d, PAD_LANES), BF16) for d in GROUP_DIL]
                  + [_sds((d, t_len // d, LANE), F32) for d in GROUP_DIL],
        scratch_shapes=[pltpu.VMEM((N_SLABS, tm, LANE), F32)],
        semantics=("arbitrary",), duty=duty,
    )(dxo, w_out_pad, o, lse)
    if duty is None:
        return outs[:3], outs[3:]
    return (outs[0][:3], outs[0][3:]), outs[1]


def attn_bwd(qkv_g, do_g, lse_g, c_g, gi, name, duty=None):
    dil, l_len, _ = qkv_g.shape
    qb = min(Q_BLOCK, l_len)
    nsub = qb // ATTN_W
    nsb = l_len // qb

    def body(q_ref, kc_ref, kp_ref, vc_ref, vp_ref, do_ref, lse_ref, c_ref,
             qn_ref, don_ref, lsen_ref, cn_ref, o_ref, kbuf, vbuf, dkbuf, dvbuf):
        n = pl.program_id(1)
        kbuf[pl.ds(0, ATTN_W), :] = kp_ref[...]
        kbuf[pl.ds(ATTN_W, qb), :] = kc_ref[...]
        vbuf[pl.ds(0, ATTN_W), :] = vp_ref[...]
        vbuf[pl.ds(ATTN_W, qb), :] = vc_ref[...]
        dkbuf[...] = jnp.zeros_like(dkbuf)
        dvbuf[...] = jnp.zeros_like(dvbuf)

        def block(q_of, do_of, lse_of, c_of, krows, mask, dq_rows):
            heads = []
            for j in range(3):
                cols = _lane_cols(j)
                q, do_t, k, v = q_of(cols), do_of(cols), kbuf[krows, cols], vbuf[krows, cols]
                for half, hm in enumerate(_live_halves(gi, j)):
                    qh = jnp.where(hm, q, jnp.zeros_like(q))
                    doh = jnp.where(hm, do_t, jnp.zeros_like(do_t))
                    heads.append((j, 2 * j + half, hm, qh, doh, _dot_nt(qh, k), _dot_nt(doh, v)))
            head_lane = lax.broadcasted_iota(jnp.int32, (1, LANE), 1)
            lse_t, c_t = lse_of(), c_of()
            for j in range(3):
                cols = _lane_cols(j)
                k = kbuf[krows, cols]
                dq = jnp.zeros((ATTN_W, LANE), F32)
                dk = jnp.zeros((k.shape[0], LANE), F32)
                dv = jnp.zeros((k.shape[0], LANE), F32)
                for hj, head, hm, qh, doh, s, dp in heads:
                    if hj != j:
                        continue
                    lse_h = jnp.max(jnp.where(head_lane == head, lse_t, -jnp.inf), axis=-1, keepdims=True)
                    c_h = jnp.max(jnp.where(head_lane == head, c_t, -jnp.inf), axis=-1, keepdims=True)
                    p = jnp.exp(jnp.where(mask, s, NEG_INF) - lse_h)
                    ds = (p * (dp - c_h)).astype(BF16)
                    if dq_rows is not None:
                        dq = jnp.where(hm, _dot(ds, k), dq)
                    dk = dk + _dot_tn(ds, qh)
                    dv = dv + _dot_tn(p.astype(BF16), doh)
                if dq_rows is not None:
                    o_ref[dq_rows, cols] = dq.astype(BF16)
                dkbuf[krows, cols] += dk
                dvbuf[krows, cols] += dv

        band, band_first = _band_masks()

        def sub(b, carry):
            rows = pl.ds(pl.multiple_of(b * ATTN_W, ATTN_W), ATTN_W)
            krows = pl.ds(pl.multiple_of(b * ATTN_W, ATTN_W), 2 * ATTN_W)
            block(lambda c: q_ref[rows, c], lambda c: do_ref[rows, c], lambda: lse_ref[rows, :],
                  lambda: c_ref[rows, :], krows, band_first | (band & (n + b > 0)), rows)
            return carry

        lax.fori_loop(0, nsub, sub, 0)

        qi = lax.broadcasted_iota(jnp.int32, (ATTN_W, ATTN_W), 0)
        kj = lax.broadcasted_iota(jnp.int32, (ATTN_W, ATTN_W), 1)
        nmask = (qi <= kj) & (n < nsb - 1)
        block(lambda c: qn_ref[:, c], lambda c: don_ref[:, c], lambda: lsen_ref[...],
              lambda: cn_ref[...], pl.ds(qb, ATTN_W), nmask, None)
        o_ref[:, pl.ds(PAD_LANES, PAD_LANES)] = dkbuf[pl.ds(ATTN_W, qb), :].astype(BF16)
        o_ref[:, pl.ds(2 * PAD_LANES, PAD_LANES)] = dvbuf[pl.ds(ATTN_W, qb), :].astype(BF16)

    cur = lambda c: pl.BlockSpec((None, qb, PAD_LANES), lambda r, n: (r, n, c))
    prev = lambda c: pl.BlockSpec((None, ATTN_W, PAD_LANES), lambda r, n: (r, jnp.maximum(n * nsub - 1, 0), c))
    nxt_row = lambda r, n: (r, jnp.minimum((n + 1) * nsub, nsb * nsub - 1), 0)
    nxt = pl.BlockSpec((None, ATTN_W, PAD_LANES), nxt_row)
    head_cur = pl.BlockSpec((None, qb, LANE), lambda r, n: (r, n, 0))
    head_nxt = pl.BlockSpec((None, ATTN_W, LANE), nxt_row)
    return _pcall(
        body, name=name, grid=(dil, nsb),
        in_specs=[cur(0), cur(1), prev(1), cur(2), prev(2), cur(0), head_cur, head_cur, nxt, nxt, head_nxt, head_nxt],
        out_specs=pl.BlockSpec((None, qb, QKV_PAD), lambda r, n: (r, n, 0)),
        out_shape=_sds((dil, l_len, QKV_PAD), BF16),
        scratch_shapes=[pltpu.VMEM((qb + ATTN_W, PAD_LANES), BF16), pltpu.VMEM((qb + ATTN_W, PAD_LANES), BF16),
                        pltpu.VMEM((qb + ATTN_W, PAD_LANES), F32), pltpu.VMEM((qb + ATTN_W, PAD_LANES), F32)],
        semantics=("arbitrary", "arbitrary"), duty=duty,
    )(qkv_g, qkv_g, qkv_g, qkv_g, qkv_g, do_g, lse_g, c_g, qkv_g, do_g, lse_g, c_g)


def qkv_bwd(dqkv_parts, w_pad, dxo, x, g_row, cos, sin):
    t_len = x.shape[0]
    tm = _tile_rows(t_len)

    def body(p1, p4, p16, w_ref, dxo_ref, x_ref, g_ref, cos_ref, sin_ref, dq_ref, dx_ref, dn_ref, slabs):
        @pl.when(pl.program_id(0) == 0)
        def _():
            dn_ref[...] = jnp.zeros_like(dn_ref)

        cos_t = cos_ref[...]
        sin_t = sin_ref[...]
        dh = None
        for gi, (dil, part) in enumerate(zip(GROUP_DIL, (p1, p4, p16))):
            slab = slabs.at[gi]
            for j in range(N_SLABS):
                if dil == 1:
                    a = part[0, :, _lane_cols(j)].astype(F32)
                else:
                    _to_natural(part, j, slab, j, dil, tm)
                    a = slab[j]
                if j < 6:
                    a = a * cos_t - _rot_half(a * sin_t)
                if j < 3:
                    a = a * (HEAD_DIM ** -0.5)
                dq_ref[gi, :, _lane_cols(j)] = a.astype(BF16)
            contrib = _dot_nt(dq_ref[gi], w_ref[gi])
            dh = contrib if dh is None else dh + contrib
        dx, dn = _rms_bwd(dh, x_ref[...], g_ref[...])
        dx_ref[...] = dxo_ref[...] + dx
        dn_ref[...] += dn

    row_spec = pl.BlockSpec((tm, D_MODEL), lambda t: (t, 0))
    vec_spec = pl.BlockSpec((1, D_MODEL), lambda t: (0, 0))
    tab_spec = pl.BlockSpec((tm, LANE), lambda t: (t, 0))
    part_specs = [pl.BlockSpec((d, tm // d, QKV_PAD), lambda t: (0, t, 0)) for d in GROUP_DIL]
    return _pcall(
        body, name="qkv_bwd", grid=(t_len // tm,),
        in_specs=part_specs + [pl.BlockSpec((3, D_MODEL, QKV_PAD), lambda t: (0, 0, 0)),
                               row_spec, row_spec, vec_spec, tab_spec, tab_spec],
        out_specs=[pl.BlockSpec((3, tm, QKV_PAD), lambda t: (0, t, 0)), row_spec, vec_spec],
        out_shape=[_sds((3, t_len, QKV_PAD), BF16), _sds((t_len, D_MODEL), F32), _sds((1, D_MODEL), F32)],
        scratch_shapes=[pltpu.VMEM((3, N_SLABS, tm, LANE), F32)],
        semantics=("arbitrary",),
    )(*dqkv_parts, w_pad, dxo, x, g_row, cos, sin)


def final_fwd_bwd(x, g_row, target):
    t_len = x.shape[0]
    tm = _tile_rows(t_len)

    def body(x_ref, g_ref, tgt_ref, dx_ref, dn_ref, loss_ref):
        @pl.when(pl.program_id(0) == 0)
        def _():
            dn_ref[...] = jnp.zeros_like(dn_ref)
            loss_ref[...] = jnp.zeros_like(loss_ref)

        x_t = x_ref[...]
        g = g_ref[...]
        diff = _rms_fwd(x_t, g) - tgt_ref[...]
        loss_ref[...] += 0.5 * jnp.sum(jnp.mean(diff * diff, axis=-1, keepdims=True), axis=0, keepdims=True)
        dx, dn = _rms_bwd(diff * (1.0 / D_MODEL), x_t, g)
        dx_ref[...] = dx
        dn_ref[...] += dn

    row_spec = pl.BlockSpec((tm, D_MODEL), lambda t: (t, 0))
    vec_spec = pl.BlockSpec((1, D_MODEL), lambda t: (0, 0))
    return _pcall(
        body, name="final_fwd_bwd", grid=(t_len // tm,),
        in_specs=[row_spec, vec_spec, row_spec],
        out_specs=[row_spec, vec_spec, pl.BlockSpec((1, 1), lambda t: (0, 0))],
        out_shape=[_sds((t_len, D_MODEL), F32), _sds((1, D_MODEL), F32), _sds((1, 1), F32)],
        semantics=("arbitrary",),
    )(x, g_row, target)


def pool_bwd(dxo, x, g_row, w_in, w_grp, scale, w_out, zr, duty=None):
    t_len = x.shape[0]
    tm = _tile_rows(t_len)
    nt = t_len // tm

    def body(dxo_ref, x_ref, g_ref, win_ref, wgrp_ref, scale_ref, wout_ref, zr_ref,
             dzs_ref, du_ref, dx_ref, dn_ref, dsc_ref, ebuf, tmp_a, tmp_b):
        i = pl.program_id(0)
        t = nt - 1 - i

        @pl.when(i == 0)
        def _():
            ebuf[pl.ds(tm, POOL_HALO), :] = jnp.zeros((POOL_HALO, D_MODEL), F32)
            dn_ref[...] = jnp.zeros_like(dn_ref)
            dsc_ref[...] = jnp.zeros_like(dsc_ref)

        dxo_t = dxo_ref[...]
        dz = _dot_nt(dxo_t.astype(BF16), wout_ref[...])
        dsc_ref[...] += jnp.sum(dz * zr_ref[...].astype(F32), axis=0, keepdims=True)
        dzs_ref[...] = (dz * scale_ref[...]).astype(BF16)
        row = t * tm + lax.broadcasted_iota(jnp.int32, (tm, 1), 0)
        for gi, w in enumerate(POOL_WINDOWS):
            cols = pl.ds(gi * POOL_GROUP_DIM, POOL_GROUP_DIM)
            dp_g = _dot_nt(dzs_ref[:, cols], wgrp_ref[gi])
            inv_cnt = 1.0 / jnp.minimum(row + 1, w).astype(F32)
            ebuf[pl.ds(0, tm), cols] = dp_g * inv_cnt
            acc = _window_sums(ebuf, cols, (tmp_a, tmp_b), gi + 1, tm, back=False) - dp_g
            du_ref[:, cols] = acc.astype(BF16)
        ebuf[pl.ds(tm, POOL_HALO), :] = ebuf[pl.ds(0, POOL_HALO), :]
        dh = _dot_nt(du_ref[...], win_ref[...])
        dx, dn = _rms_bwd(dh, x_ref[...], g_ref[...])
        dx_ref[...] = dxo_t + dx
        dn_ref[...] += dn

    row_spec = pl.BlockSpec((tm, D_MODEL), lambda i: (nt - 1 - i, 0))
    full = lambda shape: pl.BlockSpec(shape, lambda i: (0,) * len(shape))
    vec = full((1, D_MODEL))
    return _pcall(
        body, name="pool_bwd", grid=(nt,),
        in_specs=[row_spec, row_spec, vec, full((D_MODEL, D_MODEL)), full((4, POOL_GROUP_DIM, POOL_GROUP_DIM)),
                  vec, full((D_MODEL, D_MODEL)), row_spec],
        out_specs=[row_spec, row_spec, row_spec, vec, vec],
        out_shape=[_sds((t_len, D_MODEL), BF16), _sds((t_len, D_MODEL), BF16), _sds((t_len, D_MODEL), F32),
                   _sds((1, D_MODEL), F32), _sds((1, D_MODEL), F32)],
        scratch_shapes=[pltpu.VMEM((tm + POOL_HALO, D_MODEL), F32)]
                       + [pltpu.VMEM((tm + POOL_HALO, POOL_GROUP_DIM), F32)] * 2,
        semantics=("arbitrary",), duty=duty,
    )(dxo, x, g_row, w_in, w_grp, scale, w_out, zr)


def _mesh_pos():
    return lax.axis_index("x"), lax.axis_index("y"), lax.axis_index("c")


def _other_chips(x, y):
    return [(1 - x, y), (x, 1 - y), (1 - x, 1 - y)]


def _remote(src, dst, send_sem, recv_sem, device):
    return pltpu.make_async_remote_copy(src_ref=src, dst_ref=dst, send_sem=send_sem, recv_sem=recv_sem,
                                        device_id=device, device_id_type=MESH)


class _Duty:
    aliases = {}

    def mid(self, ins, outs, sems):
        pass


class Together(_Duty):
    def __init__(self, duties):
        self.duties = duties
        self.ins = [a for d in duties for a in d.ins]
        self.out_shape = [s for d in duties for s in d.out_shape]
        self.scratch = [s for d in duties for s in d.scratch]
        self.aliases = {}
        i0 = o0 = 0
        for d in duties:
            self.aliases.update({i0 + i: o0 + o for i, o in d.aliases.items()})
            i0 += len(d.ins)
            o0 += len(d.out_shape)

    def _each(self, ins, outs, sems):
        i0 = o0 = s0 = 0
        for d in self.duties:
            ni, no, ns = len(d.ins), len(d.out_shape), len(d.scratch)
            yield d, ins[i0:i0 + ni], outs[o0:o0 + no], sems[s0:s0 + ns]
            i0, o0, s0 = i0 + ni, o0 + no, s0 + ns

    def split(self, outs):
        return [list(o) for _, _, o, _ in self._each(self.ins, outs, self.scratch)]

    def start(self, ins, outs, sems):
        for d, i, o, s in self._each(ins, outs, sems):
            d.start(i, o, s)

    def mid(self, ins, outs, sems):
        for d, i, o, s in self._each(ins, outs, sems):
            d.mid(i, o, s)

    def finish(self, ins, outs, sems):
        for d, i, o, s in self._each(ins, outs, sems):
            d.finish(i, o, s)


def run_duty(duty, name):
    d_in, d_out = len(duty.ins), len(duty.out_shape)

    def body(*refs):
        ins, outs, sems = refs[:d_in], refs[d_in:d_in + d_out], refs[d_in + d_out:]
        duty.start(ins, outs, sems)
        duty.mid(ins, outs, sems)
        duty.finish(ins, outs, sems)

    return pl.pallas_call(
        body, name=name, out_shape=list(duty.out_shape), in_specs=[_ANY] * d_in, out_specs=[_ANY] * d_out,
        scratch_shapes=list(duty.scratch), input_output_aliases=dict(duty.aliases),
        compiler_params=pltpu.CompilerParams(has_side_effects=True),
    )(*duty.ins)


class GatherWeights(_Duty):
    def __init__(self, shards):
        n = self.n = len(shards)
        self.halves = [s.shape[0] // 2 for s in shards]
        my_slot = 2 * lax.axis_index("x") + lax.axis_index("y")
        staged = [lax.dynamic_update_slice(lax.empty((N_SHARDS,) + s.shape, s.dtype), s[None], (my_slot, 0, 0))
                  for s in shards]
        self.ins = list(shards) + staged
        self.out_shape = [_sds((N_SHARDS,) + s.shape, s.dtype) for s in shards]
        self.aliases = {n + a: a for a in range(n)}
        self.scratch = [pltpu.SemaphoreType.DMA((n, 6)), pltpu.SemaphoreType.DMA((n, 6))]

    def _over_ici(self, ins, outs, sems):
        x, y, c = _mesh_pos()
        return [_remote(ins[a].at[pl.ds(c * h, h)], outs[a].at[2 * x + y, pl.ds(c * h, h)],
                        sems[0].at[a, j], sems[1].at[a, j], (*chip, c))
                for a, h in enumerate(self.halves) for j, chip in enumerate(_other_chips(x, y))]

    def _forwards(self, outs, sems, half_of):
        x, y, c = _mesh_pos()
        cps = []
        for a, h in enumerate(self.halves):
            for j, chip in enumerate(_other_chips(x, y)):
                slot = outs[a].at[2 * chip[0] + chip[1], pl.ds(half_of(c) * h, h)]
                cps.append(_remote(slot, slot, sems[0].at[a, 3 + j], sems[1].at[a, 3 + j], (x, y, 1 - c)))
        return cps

    def start(self, ins, outs, sems):
        for cp in self._over_ici(ins, outs, sems):
            cp.start()

    def mid(self, ins, outs, sems):
        x, y, c = _mesh_pos()
        forwards = self._forwards(outs, sems, lambda core: core)
        k = 0
        for a, h in enumerate(self.halves):
            for j, chip in enumerate(_other_chips(x, y)):
                slot = outs[a].at[2 * chip[0] + chip[1], pl.ds(c * h, h)]
                _remote(slot, slot, sems[0].at[a, j], sems[1].at[a, j], (*chip, c)).wait_recv()
                forwards[k].start()
                k += 1

    def finish(self, ins, outs, sems):
        for cp in self._forwards(outs, sems, lambda core: 1 - core):
            cp.wait_recv()
        for cp in self._over_ici(ins, outs, sems) + self._forwards(outs, sems, lambda core: core):
            cp.wait_send()


class GradReducer:
    def __init__(self, c_idx, pos_idx):
        self.c_idx, self.pos_idx = c_idx, pos_idx
        self.in_flight = []
        self.done = {}

    def push(self, name, grad):
        self.in_flight.append(dict(name=name, stage="halves", data=grad))

    def _duties(self):
        make = {"halves": SiblingHalves, "exchange": ChipExchange, "share": SiblingShare}
        return Together([make[w["stage"]]([w["data"]]) for w in self.in_flight])

    def _advance(self, duties, outs):
        still = []
        for w, (res,) in zip(self.in_flight, duties.split(outs)):
            if w["stage"] == "halves":
                partial = add_my_half(w["data"], res, self.c_idx, f"rs_add_{w['name']}")
                still.append(dict(name=w["name"], stage="exchange", data=partial))
            elif w["stage"] == "exchange":
                reduced = sum_slots(res, w["data"], self.pos_idx, f"rs_sum_{w['name']}")
                still.append(dict(name=w["name"], stage="share", data=reduced))
            else:
                self.done[w["name"]] = res
        self.in_flight = still

    def carried_by(self, fn, *args, **kw):
        if not self.in_flight:
            return fn(*args, **kw)
        duties = self._duties()
        out, duty_outs = fn(*args, duty=duties, **kw)
        self._advance(duties, duty_outs)
        return out

    def drain(self, name):
        step = 0
        while self.in_flight:
            duties = self._duties()
            self._advance(duties, run_duty(duties, f"{name}{step}"))
            step += 1


class SiblingHalves(_Duty):
    def __init__(self, grads):
        n = len(grads)
        self.halves = [g.shape[1] // 2 for g in grads]
        self.ins = list(grads)
        self.out_shape = [_sds((N_SHARDS, h, g.shape[2]), g.dtype) for g, h in zip(grads, self.halves)]
        self.scratch = [pltpu.SemaphoreType.DMA((n,)), pltpu.SemaphoreType.DMA((n,))]

    def _copies(self, ins, outs, sems):
        x, y, c = _mesh_pos()
        return [_remote(ins[a].at[:, pl.ds((1 - c) * h, h)], outs[a], sems[0].at[a], sems[1].at[a], (x, y, 1 - c))
                for a, h in enumerate(self.halves)]

    def start(self, ins, outs, sems):
        for cp in self._copies(ins, outs, sems):
            cp.start()

    def finish(self, ins, outs, sems):
        for cp in self._copies(ins, outs, sems):
            cp.wait()


class ChipExchange(_Duty):
    def __init__(self, parts):
        n = self.n = len(parts)
        self.ins = list(parts)
        self.out_shape = [_sds(p.shape, p.dtype) for p in parts]
        self.scratch = [pltpu.SemaphoreType.DMA((n, 3)), pltpu.SemaphoreType.DMA((n, 3))]

    def _copies(self, ins, outs, sems, arriving):
        x, y, c = _mesh_pos()
        cps = []
        for a in range(self.n):
            for j, chip in enumerate(_other_chips(x, y)):
                theirs = 2 * chip[0] + chip[1]
                src = outs[a].at[theirs] if arriving else ins[a].at[theirs]
                dst = outs[a].at[theirs] if arriving else outs[a].at[2 * x + y]
                cps.append(_remote(src, dst, sems[0].at[a, j], sems[1].at[a, j], (*chip, c)))
        return cps

    def start(self, ins, outs, sems):
        for cp in self._copies(ins, outs, sems, False):
            cp.start()

    def finish(self, ins, outs, sems):
        for cp in self._copies(ins, outs, sems, True):
            cp.wait_recv()
        for cp in self._copies(ins, outs, sems, False):
            cp.wait_send()


class SiblingShare(_Duty):
    def __init__(self, reduced):
        n = self.n = len(reduced)
        self.ins = list(reduced)
        self.out_shape = [_sds(r.shape, r.dtype) for r in reduced]
        self.aliases = {a: a for a in range(n)}
        self.scratch = [pltpu.SemaphoreType.DMA((n,)), pltpu.SemaphoreType.DMA((n,))]

    def _copies(self, outs, sems, half_of):
        x, y, c = _mesh_pos()
        cps = []
        for a in range(self.n):
            h = outs[a].shape[0] // 2
            rows = outs[a].at[pl.ds(half_of(c) * h, h)]
            cps.append(_remote(rows, rows, sems[0].at[a], sems[1].at[a], (x, y, 1 - c)))
        return cps

    def start(self, ins, outs, sems):
        for cp in self._copies(outs, sems, lambda core: core):
            cp.start()

    def finish(self, ins, outs, sems):
        for cp in self._copies(outs, sems, lambda core: 1 - core):
            cp.wait_recv()
        for cp in self._copies(outs, sems, lambda core: core):
            cp.wait_send()


def allreduce_small(v):
    def body(v_ref, o_ref, buf, send_sems, recv_sems):
        x, y, c = _mesh_pos()
        me = 4 * x + 2 * y + c
        buf[me] = v_ref[...]
        flip = lambda p, f: 1 - p if f else p
        peers = [(flip(x, k & 4), flip(y, k & 2), flip(c, k & 1)) for k in range(1, N_DEV)]
        cps = []
        for k, peer in enumerate(peers):
            cp = _remote(v_ref, buf.at[me], send_sems.at[k], recv_sems.at[k], peer)
            cp.start()
            cps.append(cp)
        for k, peer in enumerate(peers):
            slot = buf.at[4 * peer[0] + 2 * peer[1] + peer[2]]
            _remote(slot, slot, send_sems.at[k], recv_sems.at[k], peer).wait_recv()
        for cp in cps:
            cp.wait_send()
        acc = buf[0]
        for i in range(1, N_DEV):
            acc = acc + buf[i]
        o_ref[...] = acc

    vm = pl.BlockSpec(memory_space=pltpu.VMEM)
    return pl.pallas_call(
        body, name="allreduce_small", out_shape=_sds(v.shape, v.dtype), in_specs=[vm], out_specs=vm,
        scratch_shapes=[pltpu.VMEM((N_DEV,) + v.shape, v.dtype),
                        pltpu.SemaphoreType.DMA((N_DEV - 1,)), pltpu.SemaphoreType.DMA((N_DEV - 1,))],
        compiler_params=pltpu.CompilerParams(has_side_effects=True),
    )(v)


def add_my_half(grad, theirs, c_idx, name):
    _, r, cols = grad.shape
    h = r // 2

    def body(c_ref, g_ref, t_ref, o_ref):
        o_ref[...] = (g_ref[...] + t_ref[...]).astype(BF16)

    slot = pl.BlockSpec((None, h, cols), lambda s, c: (s, 0, 0))
    grid_spec = pltpu.PrefetchScalarGridSpec(
        num_scalar_prefetch=1, grid=(N_SHARDS,),
        in_specs=[pl.BlockSpec((None, h, cols), lambda s, c: (s, c[0], 0)), slot], out_specs=slot)
    return pl.pallas_call(
        body, name=name, grid_spec=grid_spec, out_shape=_sds((N_SHARDS, h, cols), BF16),
        compiler_params=pltpu.CompilerParams(dimension_semantics=("arbitrary",), vmem_limit_bytes=VMEM_LIMIT_BYTES),
    )(c_idx, grad, theirs)


def sum_slots(received, mine, pos_idx, name):
    _, h, cols = received.shape

    def body(pos_ref, r_ref, m_ref, o_ref):
        acc = None
        for k in range(N_SHARDS):
            term = jnp.where(pos_ref[0] == k, m_ref[k], r_ref[k]).astype(F32)
            acc = term if acc is None else acc + term
        o_ref[...] = acc

    whole = pl.BlockSpec((N_SHARDS, h, cols), lambda i, pos: (0, 0, 0))
    grid_spec = pltpu.PrefetchScalarGridSpec(
        num_scalar_prefetch=1, grid=(1,), in_specs=[whole, whole],
        out_specs=pl.BlockSpec((h, cols), lambda i, pos: (pos[1], 0)))
    return pl.pallas_call(
        body, name=name, grid_spec=grid_spec, out_shape=_sds((2 * h, cols), F32),
        compiler_params=pltpu.CompilerParams(dimension_semantics=("arbitrary",), vmem_limit_bytes=VMEM_LIMIT_BYTES),
    )(pos_idx, received, mine)


def adamw(name, grads, w, m, v):
    n_layers, r, cols = w.shape
    tr = r // 2 if r % 16 == 0 else r
    bias1 = 1.0 - ADAM_B1 ** ADAM_STEP
    bias2 = 1.0 - ADAM_B2 ** ADAM_STEP

    def body(*refs):
        g_refs = refs[:n_layers]
        w_ref, m_ref, v_ref, go_ref, d_ref, mo_ref, vo_ref = refs[n_layers:]
        g = g_refs[0][...]
        for layer in range(1, n_layers):
            g = jnp.where(pl.program_id(0) == layer, g_refs[layer][...], g)
        m_new = ADAM_B1 * m_ref[...] + (1.0 - ADAM_B1) * g
        v_new = ADAM_B2 * v_ref[...] + (1.0 - ADAM_B2) * (g * g)
        m_hat = m_new / bias1
        v_hat = v_new / bias2
        go_ref[...] = g
        d_ref[...] = -ADAM_LR * (m_hat / (jnp.sqrt(v_hat) + ADAM_EPS) + ADAM_WD * w_ref[...])
        mo_ref[...] = m_new
        vo_ref[...] = v_new

    g_spec = pl.BlockSpec((tr, cols), lambda l, i: (i, 0))
    lay_spec = pl.BlockSpec((None, tr, cols), lambda l, i: (l, i, 0))
    shape = _sds((n_layers, r, cols), F32)
    return _pcall(
        body, name=name, grid=(n_layers, r // tr),
        in_specs=[g_spec] * n_layers + [lay_spec] * 3, out_specs=[lay_spec] * 4,
        out_shape=[shape] * 4, semantics=("arbitrary", "arbitrary"),
    )(*grads, w, m, v)


def kernel(x, norm_mix, norm_ffn, norm_final, pool_w_in, pool_w_group, pool_scale, pool_w_out, attn_w_qkv, attn_w_out, ffn_w_gate, ffn_w_up, ffn_w_down, loss_target, m_norm_mix, m_norm_ffn, m_norm_final, m_pool_w_in, m_pool_w_group, m_pool_scale, m_pool_w_out, m_attn_w_qkv, m_attn_w_out, m_ffn_w_gate, m_ffn_w_up, m_ffn_w_down, v_norm_mix, v_norm_ffn, v_norm_final, v_pool_w_in, v_pool_w_group, v_pool_scale, v_pool_w_out, v_attn_w_qkv, v_attn_w_out, v_ffn_w_gate, v_ffn_w_up, v_ffn_w_down):
    t_len = x.shape[1]
    x0 = x.reshape(t_len, D_MODEL)
    target = loss_target.reshape(t_len, D_MODEL)
    row = lambda a: a.reshape(1, D_MODEL)

    grp_rows = POOL_GROUP_DIM // N_SHARDS
    bf = lambda a: a.astype(BF16)
    gate_t, up_t = jnp.swapaxes(ffn_w_gate, 1, 2), jnp.swapaxes(ffn_w_up, 1, 2)
    pool_shards = [bf(pool_w_in[0]), bf(pool_w_group[0].reshape(4 * grp_rows, POOL_GROUP_DIM)), bf(pool_w_out[0])]
    ffn0_shards = [bf(gate_t[0]), bf(up_t[0]), bf(ffn_w_down[0])]
    late_shards = [bf(attn_w_qkv[0]), bf(attn_w_out[0]), bf(gate_t[1]), bf(up_t[1]), bf(ffn_w_down[1])]
    cos, sin = rope_tables(t_len)
    c_idx = lax.axis_index("c").astype(jnp.int32).reshape(1)
    pos_idx = jnp.stack([2 * lax.axis_index("x") + lax.axis_index("y"), lax.axis_index("c")]).astype(jnp.int32)
    chip_rows = lambda g: g.reshape(N_SHARDS, D_MODEL // N_SHARDS, D_MODEL)

    g_pool = run_duty(GatherWeights(pool_shards), "gather_pool")
    w_in = g_pool[0].reshape(D_MODEL, D_MODEL)
    w_grp = g_pool[1].reshape(N_SHARDS, 4, grp_rows, POOL_GROUP_DIM).transpose(1, 0, 2, 3).reshape(
        4, POOL_GROUP_DIM, POOL_GROUP_DIM)
    w_out = g_pool[2].reshape(D_MODEL, D_MODEL)
    (h0, p, zr, z, x1), ffn0 = pool_fwd(x0, row(norm_mix[0]), w_in, w_grp, pool_scale, w_out,
                                        duty=GatherWeights(ffn0_shards))
    (h1, gate0, up0, act0, x2), late = ffn_fwd(x1, row(norm_ffn[0]), *ffn0, "ffn_fwd0",
                                               duty=GatherWeights(late_shards))
    w_qkv = pad_qkv_weight(late[0])
    w_ao = jnp.concatenate(pad_groups(late[1].reshape(D_MODEL, D_MODEL), 0), axis=0)
    ffn1 = late[2:5]
    h2, *qkv_parts = qkv_fwd(x2, row(norm_mix[1]), w_qkv, cos, sin)
    o_parts, lse_parts = [], []
    for gi in range(3):
        o_g, lse_g = attn_fwd(qkv_parts[gi], gi, f"attn_fwd_g{gi}")
        o_parts.append(o_g)
        lse_parts.append(lse_g)
    x3, merged, o_nat, lse_nat = attn_out_fwd(x2, o_parts, lse_parts, w_ao)
    h3, gate1, up1, act1, x4 = ffn_fwd(x3, row(norm_ffn[1]), *ffn1, "ffn_fwd1")
    dx4, d_norm_final, loss_local = final_fwd_bwd(x4, row(norm_final), target)

    red = GradReducer(c_idx, pos_idx)
    dgate1, dup1, dx3, d_nf1 = ffn_bwd(dx4, x3, row(norm_ffn[1]), gate1, up1, *ffn1, "ffn_bwd1")
    g_gate1 = wgrad_row_sharded("wgrad_gate1", dgate1, h3)
    g_up1 = wgrad_row_sharded("wgrad_up1", dup1, h3)
    g_down1 = wgrad_row_sharded("wgrad_down1", act1, dx4)
    red.push("gate1", g_gate1)
    red.push("up1", g_up1)
    do_parts, c_parts = red.carried_by(attn_out_bwd, dx3, w_ao, o_nat, lse_nat)
    g_ao = wgrad_full("wgrad_attn_out", merged, dx3)
    red.push("down1", g_down1)
    red.push("attn_out", chip_rows(unpad_groups(jnp.split(g_ao, 3, axis=0), 0)))
    dqkv_parts = [red.carried_by(attn_bwd, qkv_parts[gi], do_parts[gi], lse_parts[gi], c_parts[gi], gi,
                                 f"attn_bwd_g{gi}") for gi in range(3)]
    dqkv, dx2, d_nm1 = qkv_bwd(dqkv_parts, w_qkv, dx3, x2, row(norm_mix[1]), cos, sin)
    red.push("qkv", unpad_qkv_grad(wgrad_col_sharded("wgrad_qkv", h2, dqkv)))

    dgate0, dup0, dx1, d_nf0 = red.carried_by(ffn_bwd, dx2, x1, row(norm_ffn[0]), gate0, up0, *ffn0, "ffn_bwd0")
    red.push("gate0", red.carried_by(wgrad_row_sharded, "wgrad_gate0", dgate0, h1))
    red.push("up0", red.carried_by(wgrad_row_sharded, "wgrad_up0", dup0, h1))
    red.push("down0", red.carried_by(wgrad_row_sharded, "wgrad_down0", act0, dx2))
    red.push("pool_out", chip_rows(red.carried_by(wgrad_full, "wgrad_pool_out", z, dx1)))
    dzs, du, dx0, d_nm0, d_scale = pool_bwd(dx1, x0, row(norm_mix[0]), w_in, w_grp, pool_scale, w_out, zr)
    g_grp = red.carried_by(wgrad_pool_group, "wgrad_pool_group", p, dzs)
    red.push("pool_group", g_grp.reshape(N_SHARDS, 4 * grp_rows, POOL_GROUP_DIM))
    red.push("pool_in", chip_rows(red.carried_by(wgrad_full, "wgrad_pool_in", h0, du)))
    red.drain("rs_tail")
    full = [red.done[nm] for nm in ("pool_in", "pool_group", "pool_out", "qkv", "attn_out",
                                    "gate0", "gate1", "up0", "up1", "down0", "down1")]

    zero_row = jnp.zeros((1, D_MODEL), F32)
    small = jnp.concatenate([d_nm0, d_nm1, d_nf0, d_nf1, d_norm_final, d_scale,
                             jnp.broadcast_to(loss_local, (1, D_MODEL)), zero_row], axis=0)
    small = allreduce_small(small)
    loss = small[6, 0]

    pack = lambda a, b, c, d: jnp.concatenate([a, b, row(c), d, zero_row, zero_row], axis=0)[None]
    sg, sd, sm, sv = adamw("adamw_small", [small],
                           pack(norm_mix, norm_ffn, norm_final, pool_scale),
                           pack(m_norm_mix, m_norm_ffn, m_norm_final, m_pool_scale),
                           pack(v_norm_mix, v_norm_ffn, v_norm_final, v_pool_scale))
    unpack = lambda a: (a[0, 0:2], a[0, 2:4], a[0, 4], a[0, 5:6])

    def update(name, grads, w, m, v, transposed=False):
        if transposed:
            w, m, v = (jnp.swapaxes(a, 1, 2) for a in (w, m, v))
        n_layers = len(grads)
        shp = (n_layers,) + grads[0].shape
        outs = [o.reshape(w.shape) for o in adamw(name, grads, w.reshape(shp), m.reshape(shp), v.reshape(shp))]
        return [jnp.swapaxes(o, 1, 2) for o in outs] if transposed else outs

    big = [
        update("adamw_pool_in", [full[0]], pool_w_in, m_pool_w_in, v_pool_w_in),
        update("adamw_pool_group", [full[1]], pool_w_group, m_pool_w_group, v_pool_w_group),
        update("adamw_pool_out", [full[2]], pool_w_out, m_pool_w_out, v_pool_w_out),
        update("adamw_qkv", [full[3]], attn_w_qkv, m_attn_w_qkv, v_attn_w_qkv),
        update("adamw_attn_out", [full[4]], attn_w_out, m_attn_w_out, v_attn_w_out),
        update("adamw_gate", [full[5], full[6]], ffn_w_gate, m_ffn_w_gate, v_ffn_w_gate, transposed=True),
        update("adamw_up", [full[7], full[8]], ffn_w_up, m_ffn_w_up, v_ffn_w_up, transposed=True),
        update("adamw_down", [full[9], full[10]], ffn_w_down, m_ffn_w_down, v_ffn_w_down),
    ]

    def leaves(k, small_vals):
        nm, nf, nfin, psc = unpack(small_vals)
        return [nm, nf, nfin, big[0][k], big[1][k], psc, big[2][k], big[3][k], big[4][k],
                big[5][k], big[6][k], big[7][k]]

    grad_x = dx0.reshape(x.shape)
    return (loss, grad_x, *leaves(0, sg), *leaves(1, sd), *leaves(2, sm), *leaves(3, sv))
```

```python
import math

import jax
import jax.numpy as jnp
from jax import lax
from jax.experimental import pallas as pl
from jax.experimental.pallas import tpu as pltpu

F32 = jnp.float32
BF16 = jnp.bfloat16

D_MODEL = 1024
N_SHARDS = 4
N_DEV = 8
D_FF = 2816
FF_SHARD = D_FF // N_SHARDS
HEAD_DIM = 64
QKV_SHARD = 3 * D_MODEL // N_SHARDS
POOL_WINDOWS = (2, 4, 8, 16)
POOL_GROUP_DIM = 256
POOL_HALO = 32
ATTN_W = 128
GROUP_LANES = (0, 384, 704, 1024)
GROUP_HEADS = (6, 5, 5)
GROUP_DIL = (1, 4, 16)
ROPE_THETA = 10000.0
EPS = 1e-6
NEG_INF = -1e30
LANE = 128
VMEM_LIMIT_BYTES = 60 * 1024 * 1024

ADAM_LR = 0.001
ADAM_B1 = 0.9
ADAM_B2 = 0.999
ADAM_EPS = 1e-08
ADAM_WD = 0.01
ADAM_STEP = 10

NT_DIMS = (((1,), (1,)), ((), ()))
TN_DIMS = (((0,), (0,)), ((), ()))
MESH = pl.DeviceIdType.MESH


_ANY = pl.BlockSpec(memory_space=pl.ANY)


def _pcall(body, *, name, out_shape, grid=None, in_specs=None, out_specs=None, scratch_shapes=(),
           semantics=None, duty=None):
    kw = {}
    if in_specs is not None and duty is None:
        kw["in_specs"] = in_specs
    if out_specs is not None and duty is None:
        kw["out_specs"] = out_specs
    if grid is not None:
        kw["grid"] = grid
    params = dict(dimension_semantics=semantics, vmem_limit_bytes=VMEM_LIMIT_BYTES)
    if duty is None:
        return pl.pallas_call(body, name=name, out_shape=out_shape, scratch_shapes=list(scratch_shapes),
                              compiler_params=pltpu.CompilerParams(**params), **kw)

    single = not isinstance(out_shape, (list, tuple))
    c_out_shape = [out_shape] if single else list(out_shape)
    c_out_specs = [out_specs] if single else list(out_specs)
    n_in, n_out, n_scr = len(in_specs), len(c_out_shape), len(scratch_shapes)
    d_in, d_out = len(duty.ins), len(duty.out_shape)
    total = math.prod(grid)
    mid_step = (2 * total) // 3

    def wrapped(*refs):
        c_in, d_ins = refs[:n_in], refs[n_in:n_in + d_in]
        o0 = n_in + d_in
        c_outs, d_outs = refs[o0:o0 + n_out], refs[o0 + n_out:o0 + n_out + d_out]
        s0 = o0 + n_out + d_out
        c_scr, d_sems = refs[s0:s0 + n_scr], refs[s0 + n_scr:]
        step = pl.program_id(0)
        for ax in range(1, len(grid)):
            step = step * grid[ax] + pl.program_id(ax)

        @pl.when(step == 0)
        def _():
            duty.start(d_ins, d_outs, d_sems)

        body(*c_in, *c_outs, *c_scr)

        @pl.when(step == mid_step)
        def _():
            duty.mid(d_ins, d_outs, d_sems)

        @pl.when(step == total - 1)
        def _():
            duty.finish(d_ins, d_outs, d_sems)

    call = pl.pallas_call(
        wrapped, name=name, grid=grid,
        in_specs=list(in_specs) + [_ANY] * d_in, out_specs=c_out_specs + [_ANY] * d_out,
        out_shape=c_out_shape + list(duty.out_shape),
        scratch_shapes=list(scratch_shapes) + list(duty.scratch),
        input_output_aliases={n_in + i: n_out + o for i, o in duty.aliases.items()},
        compiler_params=pltpu.CompilerParams(has_side_effects=True, **params))

    def run(*args):
        outs = call(*args, *duty.ins)
        c = outs[:n_out]
        return (c[0] if single else list(c)), list(outs[n_out:])

    return run


def _sds(shape, dtype):
    return jax.ShapeDtypeStruct(tuple(shape), dtype)


def _dot(a, b):
    return jnp.dot(a, b, preferred_element_type=F32)


def _dot_nt(a, b):
    return lax.dot_general(a, b, NT_DIMS, preferred_element_type=F32)


def _dot_tn(a, b):
    return lax.dot_general(a, b, TN_DIMS, preferred_element_type=F32)


def _rms_fwd(x, g):
    r = lax.rsqrt(jnp.mean(x * x, axis=-1, keepdims=True) + EPS)
    return x * r * g


def _rms_bwd(dh, x, g):
    r = lax.rsqrt(jnp.mean(x * x, axis=-1, keepdims=True) + EPS)
    xh = x * r
    dg = jnp.sum(dh * xh, axis=0, keepdims=True)
    dxh = dh * g
    dx = r * (dxh - xh * jnp.mean(dxh * xh, axis=-1, keepdims=True))
    return dx, dg


def _sigmoid(x):
    return 0.5 * jnp.tanh(0.5 * x) + 0.5


def _tile_rows(t):
    return min(512, t)


def _sub_tiles(tm, n_sub=2):
    rows = tm // n_sub
    return [pl.ds(i * rows, rows) for i in range(n_sub)]


def _wgrad_rows(t):
    return min(2048, t)


def _window_sums(buf, cols, tmps, levels, tm, back):
    src, src_cols = buf, cols
    for k in range(1, levels + 1):
        shift = 1 << (k - 1)
        last = k == levels
        if back:
            lo = POOL_HALO if last else 8 * k
            n = tm + POOL_HALO - lo
            val = src[pl.ds(lo, n), src_cols] + src[pl.ds(lo - shift, n), src_cols]
        else:
            lo = 0
            n = tm if last else tm + POOL_HALO - 8 * k
            val = src[pl.ds(0, n), src_cols] + src[pl.ds(shift, n), src_cols]
        if last:
            return val
        tmps[k % 2][pl.ds(lo, n), :] = val
        src, src_cols = tmps[k % 2], slice(None)
def pool_fwd(x, g_row, w_in, w_grp, scale, w_out, duty=None):
    t_len = x.shape[0]
    tm = _tile_rows(t_len)

    def body(x_ref, g_ref, win_ref, wgrp_ref, scale_ref, wout_ref,
             h_ref, p_ref, zr_ref, z_ref, xo_ref, ubuf, tmp_a, tmp_b):
        t = pl.program_id(0)

        @pl.when(t == 0)
        def _():
            ubuf[pl.ds(0, POOL_HALO), :] = jnp.zeros((POOL_HALO, D_MODEL), F32)

        x_t = x_ref[...]
        h = _rms_fwd(x_t, g_ref[...]).astype(BF16)
        h_ref[...] = h
        ubuf[pl.ds(POOL_HALO, tm), :] = _dot(h, win_ref[...])
        row = t * tm + lax.broadcasted_iota(jnp.int32, (tm, 1), 0)
        for gi, w in enumerate(POOL_WINDOWS):
            cols = pl.ds(gi * POOL_GROUP_DIM, POOL_GROUP_DIM)
            u_g = ubuf[pl.ds(POOL_HALO, tm), cols]
            acc = _window_sums(ubuf, cols, (tmp_a, tmp_b), gi + 1, tm, back=True)
            inv_cnt = 1.0 / jnp.minimum(row + 1, w).astype(F32)
            p_g = (acc * inv_cnt - u_g).astype(BF16)
            p_ref[:, cols] = p_g
            z_g = _dot(p_g, wgrp_ref[gi])
            zr_ref[:, cols] = z_g.astype(BF16)
            z_ref[:, cols] = (z_g * scale_ref[:, cols]).astype(BF16)
        ubuf[pl.ds(0, POOL_HALO), :] = ubuf[pl.ds(tm, POOL_HALO), :]
        xo_ref[...] = x_t + _dot(z_ref[...], wout_ref[...])

    row_spec = pl.BlockSpec((tm, D_MODEL), lambda t: (t, 0))
    full2 = lambda shape: pl.BlockSpec(shape, lambda t: (0,) * len(shape))
    return _pcall(
        body, name="pool_fwd", grid=(t_len // tm,),
        in_specs=[row_spec, full2((1, D_MODEL)), full2((D_MODEL, D_MODEL)),
                  full2((4, POOL_GROUP_DIM, POOL_GROUP_DIM)), full2((1, D_MODEL)), full2((D_MODEL, D_MODEL))],
        out_specs=[row_spec] * 5,
        out_shape=[_sds((t_len, D_MODEL), BF16)] * 4 + [_sds((t_len, D_MODEL), F32)],
        scratch_shapes=[pltpu.VMEM((tm + POOL_HALO, D_MODEL), F32)]
                       + [pltpu.VMEM((tm + POOL_HALO, POOL_GROUP_DIM), F32)] * 2,
        semantics=("arbitrary",), duty=duty,
    )(x, g_row, w_in, w_grp, scale, w_out)


def ffn_fwd(x, g_row, w_gate_t, w_up_t, w_down, name, duty=None):
    t_len = x.shape[0]
    tm = min(1024, t_len)

    def body(x_ref, g_ref, wg_ref, wu_ref, wd_ref, h_ref, go_ref, uo_ref, ao_ref, xo_ref, hbuf, acc):
        s = pl.program_id(1)

        @pl.when(s == 0)
        def _():
            h = _rms_fwd(x_ref[...], g_ref[...]).astype(BF16)
            hbuf[...] = h
            h_ref[...] = h
            acc[...] = jnp.zeros_like(acc)

        h = hbuf[...]
        gate = _dot_nt(h, wg_ref[...])
        up = _dot_nt(h, wu_ref[...])
        go_ref[...] = gate.astype(BF16)
        uo_ref[...] = up.astype(BF16)
        act = (gate * _sigmoid(gate) * up).astype(BF16)
        ao_ref[...] = act
        acc[...] += _dot(act, wd_ref[...])

        @pl.when(s == N_SHARDS - 1)
        def _():
            xo_ref[...] = x_ref[...] + acc[...]

    row_spec = pl.BlockSpec((tm, D_MODEL), lambda t, s: (t, 0))
    row_w = pl.BlockSpec((None, FF_SHARD, D_MODEL), lambda t, s: (s, 0, 0))
    act_spec = pl.BlockSpec((None, tm, FF_SHARD), lambda t, s: (s, t, 0))
    return _pcall(
        body, name=name, grid=(t_len // tm, N_SHARDS),
        in_specs=[row_spec, pl.BlockSpec((1, D_MODEL), lambda t, s: (0, 0)), row_w, row_w, row_w],
        out_specs=[row_spec, act_spec, act_spec, act_spec, row_spec],
        out_shape=[_sds((t_len, D_MODEL), BF16)] + [_sds((N_SHARDS, t_len, FF_SHARD), BF16)] * 3
                  + [_sds((t_len, D_MODEL), F32)],
        scratch_shapes=[pltpu.VMEM((tm, D_MODEL), BF16), pltpu.VMEM((tm, D_MODEL), F32)],
        semantics=("arbitrary", "arbitrary"), duty=duty,
    )(x, g_row, w_gate_t, w_up_t, w_down)


def ffn_bwd(dxo, x, g_row, gate, up, w_gate_t, w_up_t, w_down, name, duty=None):
    t_len = x.shape[0]
    tm = _tile_rows(t_len)

    def body(dxo_ref, x_ref, g_ref, gate_ref, up_ref, wg_ref, wu_ref, wd_ref,
             dg_ref, du_ref, dx_ref, dn_ref, dxb, dh):
        t = pl.program_id(0)
        s = pl.program_id(1)

        @pl.when(s == 0)
        def _():
            dxb[...] = dxo_ref[...].astype(BF16)
            dh[...] = jnp.zeros_like(dh)

        @pl.when(jnp.logical_and(s == 0, t == 0))
        def _():
            dn_ref[...] = jnp.zeros_like(dn_ref)

        sub_tiles = _sub_tiles(tm)
        dacts = [_dot_nt(dxb[rows, :], wd_ref[...]) for rows in sub_tiles]
        for rows, dact in zip(sub_tiles, dacts):
            gv = gate_ref[rows, :].astype(F32)
            uv = up_ref[rows, :].astype(F32)
            sg = _sigmoid(gv)
            dgv = (dact * uv * (sg * (1.0 + gv * (1.0 - sg)))).astype(BF16)
            duv = (dact * (gv * sg)).astype(BF16)
            dg_ref[rows, :] = dgv
            du_ref[rows, :] = duv
            dh[rows, :] += _dot(dgv, wg_ref[...]) + _dot(duv, wu_ref[...])

        @pl.when(s == N_SHARDS - 1)
        def _():
            dx, dn = _rms_bwd(dh[...], x_ref[...], g_ref[...])
            dx_ref[...] = dxo_ref[...] + dx
            dn_ref[...] += dn

    row_spec = pl.BlockSpec((tm, D_MODEL), lambda t, s: (t, 0))
    vec_spec = pl.BlockSpec((1, D_MODEL), lambda t, s: (0, 0))
    row_w = pl.BlockSpec((None, FF_SHARD, D_MODEL), lambda t, s: (s, 0, 0))
    act_spec = pl.BlockSpec((None, tm, FF_SHARD), lambda t, s: (s, t, 0))
    act_shape = _sds((N_SHARDS, t_len, FF_SHARD), BF16)
    return _pcall(
        body, name=name, grid=(t_len // tm, N_SHARDS),
        in_specs=[row_spec, row_spec, vec_spec, act_spec, act_spec, row_w, row_w, row_w],
        out_specs=[act_spec, act_spec, row_spec, vec_spec],
        out_shape=[act_shape, act_shape, _sds((t_len, D_MODEL), F32), _sds((1, D_MODEL), F32)],
        scratch_shapes=[pltpu.VMEM((tm, D_MODEL), BF16), pltpu.VMEM((tm, D_MODEL), F32)],
        semantics=("arbitrary", "arbitrary"), duty=duty,
    )(dxo, x, g_row, gate, up, w_gate_t, w_up_t, w_down)


def tn_matmul(name, a, b, a_spec, b_spec, out_shape, out_spec, grid, duty=None):
    def body(a_ref, b_ref, o_ref):
        @pl.when(pl.program_id(len(grid) - 1) == 0)
        def _():
            o_ref[...] = jnp.zeros_like(o_ref)

        res = _dot_tn(a_ref[...].astype(BF16), b_ref[...].astype(BF16))
        o_ref[...] += res.reshape(o_ref.shape)

    return _pcall(body, name=name, grid=grid, in_specs=[a_spec, b_spec], out_specs=out_spec,
                  out_shape=out_shape, semantics=("arbitrary",) * len(grid), duty=duty)(a, b)


def wgrad_full(name, a, b, duty=None):
    t_len, k = a.shape
    n = b.shape[1]
    tt = _wgrad_rows(t_len)
    return tn_matmul(name, a, b,
                     pl.BlockSpec((tt, k), lambda t: (t, 0)), pl.BlockSpec((tt, n), lambda t: (t, 0)),
                     _sds((k, n), F32), pl.BlockSpec((k, n), lambda t: (0, 0)), (t_len // tt,), duty)


def wgrad_col_sharded(name, a, b_sh, duty=None):
    t_len, k = a.shape
    n_sh, _, n = b_sh.shape
    tt = _wgrad_rows(t_len)
    return tn_matmul(name, a, b_sh,
                     pl.BlockSpec((tt, k), lambda s, t: (t, 0)), pl.BlockSpec((None, tt, n), lambda s, t: (s, t, 0)),
                     _sds((n_sh, k, n), F32), pl.BlockSpec((None, k, n), lambda s, t: (s, 0, 0)),
                     (n_sh, t_len // tt), duty)


def wgrad_row_sharded(name, a_sh, b, duty=None):
    t_len, n = b.shape
    n_sh, _, k = a_sh.shape
    tt = _wgrad_rows(t_len)

    def body(a_ref, b_ref, o_ref):
        s = pl.program_id(1)
        res = _dot_tn(a_ref[...], b_ref[...].astype(BF16))

        @pl.when(pl.program_id(0) == 0)
        def _():
            o_ref[s] = res

        @pl.when(pl.program_id(0) > 0)
        def _():
            o_ref[s] += res

    return _pcall(body, name=name, grid=(t_len // tt, n_sh),
                  in_specs=[pl.BlockSpec((None, tt, k), lambda t, s: (s, t, 0)),
                            pl.BlockSpec((tt, n), lambda t, s: (t, 0))],
                  out_specs=pl.BlockSpec((n_sh, k, n), lambda t, s: (0, 0, 0)),
                  out_shape=_sds((n_sh, k, n), F32), semantics=("arbitrary", "arbitrary"), duty=duty)(a_sh, b)


def wgrad_pool_group(name, p, dzs, duty=None):
    t_len = p.shape[0]
    tt = _wgrad_rows(t_len)
    gd = POOL_GROUP_DIM
    rows = gd // N_SHARDS
    return tn_matmul(name, p, dzs,
                     pl.BlockSpec((tt, gd), lambda g, t: (t, g)), pl.BlockSpec((tt, gd), lambda g, t: (t, g)),
                     _sds((N_SHARDS, 4, rows, gd), F32),
                     pl.BlockSpec((N_SHARDS, None, rows, gd), lambda g, t: (0, g, 0, 0)),
                     (4, t_len // tt), duty)


PAD_LANES = 384
QKV_PAD = 3 * PAD_LANES
N_SLABS = QKV_PAD // LANE
GROUP_REAL = tuple(GROUP_LANES[g + 1] - GROUP_LANES[g] for g in range(3))
Q_BLOCK = 512


def pad_groups(w, axis):
    parts = []
    for g in range(3):
        blk = lax.slice_in_dim(w, GROUP_LANES[g], GROUP_LANES[g + 1], axis=axis)
        pad = [(0, 0)] * w.ndim
        pad[axis] = (0, PAD_LANES - GROUP_REAL[g])
        parts.append(jnp.pad(blk, pad))
    return parts


def unpad_groups(parts, axis):
    return jnp.concatenate([lax.slice_in_dim(p, 0, GROUP_REAL[g], axis=axis) for g, p in enumerate(parts)],
                           axis=axis)


def _qkv_pieces(group, part):
    lo, hi = part * D_MODEL + GROUP_LANES[group], part * D_MODEL + GROUP_LANES[group + 1]
    pieces = []
    while lo < hi:
        shard = lo // QKV_SHARD
        end = min(hi, (shard + 1) * QKV_SHARD)
        pieces.append((shard, lo - shard * QKV_SHARD, end - shard * QKV_SHARD))
        lo = end
    return pieces


def pad_qkv_weight(w_qkv_sh):
    groups = []
    for g in range(3):
        cols = []
        for part in range(3):
            cols += [w_qkv_sh[s][:, lo:hi] for s, lo, hi in _qkv_pieces(g, part)]
            if GROUP_REAL[g] < PAD_LANES:
                cols.append(jnp.zeros((D_MODEL, PAD_LANES - GROUP_REAL[g]), w_qkv_sh.dtype))
        groups.append(jnp.concatenate(cols, axis=1))
    return jnp.stack(groups)


def unpad_qkv_grad(g_pad):
    shard_cols = [[] for _ in range(N_SHARDS)]
    for part in range(3):
        for g in range(3):
            at = part * PAD_LANES
            for s, lo, hi in _qkv_pieces(g, part):
                shard_cols[s].append(g_pad[g][:, at:at + hi - lo])
                at += hi - lo
    return jnp.stack([jnp.concatenate(cols, axis=1) for cols in shard_cols])


def rope_tables(t_len):
    inv_freq = 1.0 / (ROPE_THETA ** (jnp.arange(0, HEAD_DIM, 2, dtype=F32) / HEAD_DIM))
    ang = jnp.arange(t_len, dtype=F32)[:, None] * inv_freq[None, :]
    cos_h, sin_h = lax.optimization_barrier((jnp.cos(ang), jnp.sin(ang)))
    reps = (1, 2 * LANE // HEAD_DIM)
    return jnp.tile(cos_h, reps), jnp.tile(sin_h, reps)


def _rot_half(v):
    n = v.shape[1]
    lane = lax.broadcasted_iota(jnp.int32, v.shape, 1)
    return jnp.where(lane % HEAD_DIM < HEAD_DIM // 2,
                     -pltpu.roll(v, n - HEAD_DIM // 2, 1), pltpu.roll(v, HEAD_DIM // 2, 1))


def _lane_cols(j):
    return slice(j * LANE, (j + 1) * LANE)


def _to_residue_major(slab, j_src, dst_ref, j_dst, dil, rows):
    for r in range(dil):
        dst_ref[r, :, _lane_cols(j_dst)] = slab[j_src, pl.ds(r, rows // dil, stride=dil), :].astype(dst_ref.dtype)


def _to_natural(src_ref, j_src, slab, j_dst, dil, rows):
    for r in range(dil):
        slab[j_dst, pl.ds(r, rows // dil, stride=dil), :] = src_ref[r, :, _lane_cols(j_src)].astype(F32)


def qkv_fwd(x, g_row, w_pad, cos, sin):
    t_len = x.shape[0]
    tm = _tile_rows(t_len)

    def body(x_ref, g_ref, w_ref, cos_ref, sin_ref, h_ref, o1_ref, o4_ref, o16_ref, slabs):
        h = _rms_fwd(x_ref[...], g_ref[...]).astype(BF16)
        h_ref[...] = h
        accs = [_dot(h, w_ref[gi]) for gi in range(3)]
        cos_t = cos_ref[...]
        sin_t = sin_ref[...]
        for gi, (dil, o_ref) in enumerate(zip(GROUP_DIL, (o1_ref, o4_ref, o16_ref))):
            slab = slabs.at[gi]
            for j in range(N_SLABS):
                a = accs[gi][:, _lane_cols(j)]
                if j < 6:
                    a = a * cos_t + _rot_half(a) * sin_t
                if j < 3:
                    a = a * (HEAD_DIM ** -0.5)
                if dil == 1:
                    o_ref[0, :, _lane_cols(j)] = a.astype(BF16)
                else:
                    slab[j] = a
                    _to_residue_major(slab, j, o_ref, j, dil, tm)

    row_spec = pl.BlockSpec((tm, D_MODEL), lambda t: (t, 0))
    tab_spec = pl.BlockSpec((tm, LANE), lambda t: (t, 0))
    out_specs = [row_spec] + [pl.BlockSpec((d, tm // d, QKV_PAD), lambda t: (0, t, 0)) for d in GROUP_DIL]
    out_shape = [_sds((t_len, D_MODEL), BF16)] + [_sds((d, t_len // d, QKV_PAD), BF16) for d in GROUP_DIL]
    return _pcall(
        body, name="qkv_fwd", grid=(t_len // tm,),
        in_specs=[row_spec, pl.BlockSpec((1, D_MODEL), lambda t: (0, 0)),
                  pl.BlockSpec((3, D_MODEL, QKV_PAD), lambda t: (0, 0, 0)), tab_spec, tab_spec],
        out_specs=out_specs, out_shape=out_shape,
        scratch_shapes=[pltpu.VMEM((3, N_SLABS, tm, LANE), F32)],
        semantics=("arbitrary",),
    )(x, g_row, w_pad, cos, sin)


def _band_masks():
    qi = lax.broadcasted_iota(jnp.int32, (ATTN_W, 2 * ATTN_W), 0)
    kj = lax.broadcasted_iota(jnp.int32, (ATTN_W, 2 * ATTN_W), 1)
    dist = ATTN_W + qi - kj
    band = (dist >= 0) & (dist <= ATTN_W)
    return band, band & (kj >= ATTN_W)


def _half_masks():
    lane = lax.broadcasted_iota(jnp.int32, (1, LANE), 1)
    return [lane < HEAD_DIM, lane >= HEAD_DIM]


def _live_halves(gi, j):
    hms = _half_masks()
    return hms if (gi == 0 or j < 2) else hms[:1]


def attn_fwd(qkv_g, gi, name):
    dil, l_len, _ = qkv_g.shape
    qb = min(Q_BLOCK, l_len)
    nsub = qb // ATTN_W

    def body(q_ref, kc_ref, kp_ref, vc_ref, vp_ref, o_ref, lse_ref, kbuf, vbuf):
        n = pl.program_id(1)
        kbuf[pl.ds(0, ATTN_W), :] = kp_ref[...]
        kbuf[pl.ds(ATTN_W, qb), :] = kc_ref[...]
        vbuf[pl.ds(0, ATTN_W), :] = vp_ref[...]
        vbuf[pl.ds(ATTN_W, qb), :] = vc_ref[...]
        band, band_first = _band_masks()

        def sub(b, carry):
            r0 = pl.multiple_of(b * ATTN_W, ATTN_W)
            mask = band_first | (band & (n + b > 0))
            krows = pl.ds(r0, 2 * ATTN_W)
            scores = []
            for j in range(3):
                q = q_ref[pl.ds(r0, ATTN_W), _lane_cols(j)]
                for hm in _live_halves(gi, j):
                    scores.append(_dot_nt(jnp.where(hm, q, jnp.zeros_like(q)), kbuf[krows, _lane_cols(j)]))
            scores = iter(scores)
            head_lane = lax.broadcasted_iota(jnp.int32, (1, LANE), 1)
            lse = jnp.zeros((ATTN_W, LANE), F32)
            for j in range(3):
                cols = _lane_cols(j)
                v = vbuf[krows, cols]
                o = jnp.zeros((ATTN_W, LANE), F32)
                for half, hm in enumerate(_live_halves(gi, j)):
                    s = jnp.where(mask, next(scores), NEG_INF)
                    m = jnp.max(s, axis=-1, keepdims=True)
                    e = jnp.exp(s - m)
                    den = jnp.sum(e, axis=-1, keepdims=True)
                    p = (e * (1.0 / den)).astype(BF16)
                    o = jnp.where(hm, _dot(p, v), o)
                    lse = jnp.where(head_lane == 2 * j + half, m + jnp.log(den), lse)
                o_ref[pl.ds(r0, ATTN_W), cols] = o.astype(BF16)
            lse_ref[pl.ds(r0, ATTN_W), :] = lse
            return carry

        lax.fori_loop(0, nsub, sub, 0)

    cur = lambda c: pl.BlockSpec((None, qb, PAD_LANES), lambda r, n: (r, n, c))
    prev = lambda c: pl.BlockSpec((None, ATTN_W, PAD_LANES), lambda r, n: (r, jnp.maximum(n * nsub - 1, 0), c))
    return _pcall(
        body, name=name, grid=(dil, l_len // qb),
        in_specs=[cur(0), cur(1), prev(1), cur(2), prev(2)],
        out_specs=[pl.BlockSpec((None, qb, PAD_LANES), lambda r, n: (r, n, 0)),
                   pl.BlockSpec((None, qb, LANE), lambda r, n: (r, n, 0))],
        out_shape=[_sds((dil, l_len, PAD_LANES), BF16), _sds((dil, l_len, LANE), F32)],
        scratch_shapes=[pltpu.VMEM((qb + ATTN_W, PAD_LANES), BF16), pltpu.VMEM((qb + ATTN_W, PAD_LANES), BF16)],
        semantics=("arbitrary", "arbitrary"),
    )(qkv_g, qkv_g, qkv_g, qkv_g, qkv_g)


def _group_stats(lses):
    head_lane = lax.broadcasted_iota(jnp.int32, (1, LANE), 1)
    fulls, glse = [], []
    for g in range(3):
        real = head_lane < GROUP_HEADS[g]
        mx = jnp.max(jnp.where(real, lses[g], -jnp.inf), axis=-1, keepdims=True)
        sm = jnp.sum(jnp.where(real, jnp.exp(lses[g] - mx), 0.0), axis=-1, keepdims=True)
        fulls.append(mx + jnp.log(sm))
        glse.append(fulls[g] - math.log(GROUP_HEADS[g]))
    top = jnp.maximum(jnp.maximum(glse[0], glse[1]), glse[2])
    ex = [jnp.exp(v - top) for v in glse]
    tot = ex[0] + ex[1] + ex[2]
    alpha = [v / tot for v in ex]
    lane = lax.broadcasted_iota(jnp.int32, (1, QKV_PAD), 1)
    scale = jnp.where(lane < PAD_LANES, 3.0 * alpha[0],
                      jnp.where(lane < 2 * PAD_LANES, 3.0 * alpha[1], 3.0 * alpha[2]))
    return alpha, fulls, scale


def attn_out_fwd(x, o_parts, lse_parts, w_out_pad):
    t_len = x.shape[0]
    tm = _tile_rows(t_len)

    def body(x_ref, o1, o4, o16, l1, l4, l16, w_ref, xo_ref, mg_ref, o_ref, lse_ref, o_slab, l_slab):
        for gi, (dil, og, lg) in enumerate(zip(GROUP_DIL, (o1, o4, o16), (l1, l4, l16))):
            for j in range(3):
                _to_natural(og, j, o_slab, 3 * gi + j, dil, tm)
            _to_natural(lg, 0, l_slab, gi, dil, tm)
        o = jnp.concatenate([o_slab[j] for j in range(N_SLABS)], axis=1)
        lses = [l_slab[gi] for gi in range(3)]
        o_ref[...] = o.astype(BF16)
        for gi in range(3):
            lse_ref[:, _lane_cols(gi)] = lses[gi]
        _, _, scale = _group_stats(lses)
        merged = (o * scale).astype(BF16)
        mg_ref[...] = merged
        xo_ref[...] = x_ref[...] + _dot(merged, w_ref[...])

    row_spec = pl.BlockSpec((tm, D_MODEL), lambda t: (t, 0))
    pad_spec = pl.BlockSpec((tm, QKV_PAD), lambda t: (t, 0))
    o_specs = [pl.BlockSpec((d, tm // d, PAD_LANES), lambda t: (0, t, 0)) for d in GROUP_DIL]
    lse_specs = [pl.BlockSpec((d, tm // d, LANE), lambda t: (0, t, 0)) for d in GROUP_DIL]
    return _pcall(
        body, name="attn_out_fwd", grid=(t_len // tm,),
        in_specs=[row_spec] + o_specs + lse_specs + [pl.BlockSpec((QKV_PAD, D_MODEL), lambda t: (0, 0))],
        out_specs=[row_spec, pad_spec, pad_spec, pl.BlockSpec((tm, 3 * LANE), lambda t: (t, 0))],
        out_shape=[_sds((t_len, D_MODEL), F32), _sds((t_len, QKV_PAD), BF16),
                   _sds((t_len, QKV_PAD), BF16), _sds((t_len, 3 * LANE), F32)],
        scratch_shapes=[pltpu.VMEM((N_SLABS, tm, LANE), F32), pltpu.VMEM((3, tm, LANE), F32)],
        semantics=("arbitrary",),
    )(x, *o_parts, *lse_parts, w_out_pad)


def attn_out_bwd(dxo, w_out_pad, o, lse, duty=None):
    t_len = dxo.shape[0]
    tm = _tile_rows(t_len)

    def body(dx_ref, w_ref, o_ref, lse_ref, d1, d4, d16, c1, c4, c16, slab):
        dmerged = _dot_nt(dx_ref[...].astype(BF16), w_ref[...])
        o_t = o_ref[...].astype(F32)
        lses = [lse_ref[:, _lane_cols(gi)] for gi in range(3)]
        alpha, fulls, scale = _group_stats(lses)
        e = dmerged * o_t
        lane = lax.broadcasted_iota(jnp.int32, (1, QKV_PAD), 1)
        dalpha = [3.0 * jnp.sum(jnp.where((lane >= g * PAD_LANES) & (lane < g * PAD_LANES + GROUP_REAL[g]), e, 0.0),
                                axis=-1, keepdims=True) for g in range(3)]
        mean_da = alpha[0] * dalpha[0] + alpha[1] * dalpha[1] + alpha[2] * dalpha[2]
        dglse = [alpha[g] * (dalpha[g] - mean_da) for g in range(3)]
        do = dmerged * scale
        es = e * scale
        for j in range(N_SLABS):
            slab[j] = do[:, _lane_cols(j)]
        for gi, (dil, dg) in enumerate(zip(GROUP_DIL, (d1, d4, d16))):
            for j in range(3):
                _to_residue_major(slab, 3 * gi + j, dg, j, dil, tm)
        head_lane = lax.broadcasted_iota(jnp.int32, (1, LANE), 1)
        first = head_lane < HEAD_DIM
        for gi, (dil, cg) in enumerate(zip(GROUP_DIL, (c1, c4, c16))):
            c_g = -(dglse[gi] * jnp.exp(lses[gi] - fulls[gi]))
            for j in range(3):
                blk = es[:, _lane_cols(3 * gi + j)]
                halves = (jnp.sum(jnp.where(first, blk, 0.0), axis=-1, keepdims=True),
                          jnp.sum(jnp.where(first, 0.0, blk), axis=-1, keepdims=True))
                for half in range(2):
                    c_g = c_g + jnp.where(head_lane == 2 * j + half, halves[half], 0.0)
            slab[gi] = c_g
            _to_residue_major(slab, gi, cg, 0, dil, tm)

    row_spec = pl.BlockSpec((tm, D_MODEL), lambda t: (t, 0))
    pad_spec = pl.BlockSpec((tm, QKV_PAD), lambda t: (t, 0))
    do_specs = [pl.BlockSpec((d, tm // d, PAD_LANES), lambda t: (0, t, 0)) for d in GROUP_DIL]
    c_specs = [pl.BlockSpec((d, tm // d, LANE), lambda t: (0, t, 0)) for d in GROUP_DIL]
    outs = _pcall(
        body, name="attn_out_bwd", grid=(t_len // tm,),
        in_specs=[row_spec, pl.BlockSpec((QKV_PAD, D_MODEL), lambda t: (0, 0)), pad_spec,
                  pl.BlockSpec((tm, 3 * LANE), lambda t: (t, 0))],
        out_specs=do_specs + c_specs,
        out_shape=[_sds((d, t_len // d, PAD_LANES), BF16) for d in GROUP_DIL]
                  + [_sds((d, t_len // d, LANE), F32) for d in GROUP_DIL],
        scratch_shapes=[pltpu.VMEM((N_SLABS, tm, LANE), F32)],
        semantics=("arbitrary",), duty=duty,
    )(dxo, w_out_pad, o, lse)
    if duty is None:
        return outs[:3], outs[3:]
    return (outs[0][:3], outs[0][3:]), outs[1]


def attn_bwd(qkv_g, do_g, lse_g, c_g, gi, name, duty=None):
    dil, l_len, _ = qkv_g.shape
    qb = min(Q_BLOCK, l_len)
    nsub = qb // ATTN_W
    nsb = l_len // qb

    def body(q_ref, kc_ref, kp_ref, vc_ref, vp_ref, do_ref, lse_ref, c_ref,
             qn_ref, don_ref, lsen_ref, cn_ref, o_ref, kbuf, vbuf, dkbuf, dvbuf):
        n = pl.program_id(1)
        kbuf[pl.ds(0, ATTN_W), :] = kp_ref[...]
        kbuf[pl.ds(ATTN_W, qb), :] = kc_ref[...]
        vbuf[pl.ds(0, ATTN_W), :] = vp_ref[...]
        vbuf[pl.ds(ATTN_W, qb), :] = vc_ref[...]
        dkbuf[...] = jnp.zeros_like(dkbuf)
        dvbuf[...] = jnp.zeros_like(dvbuf)

        def block(q_of, do_of, lse_of, c_of, krows, mask, dq_rows):
            heads = []
            for j in range(3):
                cols = _lane_cols(j)
                q, do_t, k, v = q_of(cols), do_of(cols), kbuf[krows, cols], vbuf[krows, cols]
                for half, hm in enumerate(_live_halves(gi, j)):
                    qh = jnp.where(hm, q, jnp.zeros_like(q))
                    doh = jnp.where(hm, do_t, jnp.zeros_like(do_t))
                    heads.append((j, 2 * j + half, hm, qh, doh, _dot_nt(qh, k), _dot_nt(doh, v)))
            head_lane = lax.broadcasted_iota(jnp.int32, (1, LANE), 1)
            lse_t, c_t = lse_of(), c_of()
            for j in range(3):
                cols = _lane_cols(j)
                k = kbuf[krows, cols]
                dq = jnp.zeros((ATTN_W, LANE), F32)
                dk = jnp.zeros((k.shape[0], LANE), F32)
                dv = jnp.zeros((k.shape[0], LANE), F32)
                for hj, head, hm, qh, doh, s, dp in heads:
                    if hj != j:
                        continue
                    lse_h = jnp.max(jnp.where(head_lane == head, lse_t, -jnp.inf), axis=-1, keepdims=True)
                    c_h = jnp.max(jnp.where(head_lane == head, c_t, -jnp.inf), axis=-1, keepdims=True)
                    p = jnp.exp(jnp.where(mask, s, NEG_INF) - lse_h)
                    ds = (p * (dp - c_h)).astype(BF16)
                    if dq_rows is not None:
                        dq = jnp.where(hm, _dot(ds, k), dq)
                    dk = dk + _dot_tn(ds, qh)
                    dv = dv + _dot_tn(p.astype(BF16), doh)
                if dq_rows is not None:
                    o_ref[dq_rows, cols] = dq.astype(BF16)
                dkbuf[krows, cols] += dk
                dvbuf[krows, cols] += dv

        band, band_first = _band_masks()

        def sub(b, carry):
            rows = pl.ds(pl.multiple_of(b * ATTN_W, ATTN_W), ATTN_W)
            krows = pl.ds(pl.multiple_of(b * ATTN_W, ATTN_W), 2 * ATTN_W)
            block(lambda c: q_ref[rows, c], lambda c: do_ref[rows, c], lambda: lse_ref[rows, :],
                  lambda: c_ref[rows, :], krows, band_first | (band & (n + b > 0)), rows)
            return carry

        lax.fori_loop(0, nsub, sub, 0)

        qi = lax.broadcasted_iota(jnp.int32, (ATTN_W, ATTN_W), 0)
        kj = lax.broadcasted_iota(jnp.int32, (ATTN_W, ATTN_W), 1)
        nmask = (qi <= kj) & (n < nsb - 1)
        block(lambda c: qn_ref[:, c], lambda c: don_ref[:, c], lambda: lsen_ref[...],
              lambda: cn_ref[...], pl.ds(qb, ATTN_W), nmask, None)
        o_ref[:, pl.ds(PAD_LANES, PAD_LANES)] = dkbuf[pl.ds(ATTN_W, qb), :].astype(BF16)
        o_ref[:, pl.ds(2 * PAD_LANES, PAD_LANES)] = dvbuf[pl.ds(ATTN_W, qb), :].astype(BF16)

    cur = lambda c: pl.BlockSpec((None, qb, PAD_LANES), lambda r, n: (r, n, c))
    prev = lambda c: pl.BlockSpec((None, ATTN_W, PAD_LANES), lambda r, n: (r, jnp.maximum(n * nsub - 1, 0), c))
    nxt_row = lambda r, n: (r, jnp.minimum((n + 1) * nsub, nsb * nsub - 1), 0)
    nxt = pl.BlockSpec((None, ATTN_W, PAD_LANES), nxt_row)
    head_cur = pl.BlockSpec((None, qb, LANE), lambda r, n: (r, n, 0))
    head_nxt = pl.BlockSpec((None, ATTN_W, LANE), nxt_row)
    return _pcall(
        body, name=name, grid=(dil, nsb),
        in_specs=[cur(0), cur(1), prev(1), cur(2), prev(2), cur(0), head_cur, head_cur, nxt, nxt, head_nxt, head_nxt],
        out_specs=pl.BlockSpec((None, qb, QKV_PAD), lambda r, n: (r, n, 0)),
        out_shape=_sds((dil, l_len, QKV_PAD), BF16),
        scratch_shapes=[pltpu.VMEM((qb + ATTN_W, PAD_LANES), BF16), pltpu.VMEM((qb + ATTN_W, PAD_LANES), BF16),
                        pltpu.VMEM((qb + ATTN_W, PAD_LANES), F32), pltpu.VMEM((qb + ATTN_W, PAD_LANES), F32)],
        semantics=("arbitrary", "arbitrary"), duty=duty,
    )(qkv_g, qkv_g, qkv_g, qkv_g, qkv_g, do_g, lse_g, c_g, qkv_g, do_g, lse_g, c_g)


def qkv_bwd(dqkv_parts, w_pad, dxo, x, g_row, cos, sin):
    t_len = x.shape[0]
    tm = _tile_rows(t_len)

    def body(p1, p4, p16, w_ref, dxo_ref, x_ref, g_ref, cos_ref, sin_ref, dq_ref, dx_ref, dn_ref, slabs):
        @pl.when(pl.program_id(0) == 0)
        def _():
            dn_ref[...] = jnp.zeros_like(dn_ref)

        cos_t = cos_ref[...]
        sin_t = sin_ref[...]
        dh = None
        for gi, (dil, part) in enumerate(zip(GROUP_DIL, (p1, p4, p16))):
            slab = slabs.at[gi]
            for j in range(N_SLABS):
                if dil == 1:
                    a = part[0, :, _lane_cols(j)].astype(F32)
                else:
                    _to_natural(part, j, slab, j, dil, tm)
                    a = slab[j]
                if j < 6:
                    a = a * cos_t - _rot_half(a * sin_t)
                if j < 3:
                    a = a * (HEAD_DIM ** -0.5)
                dq_ref[gi, :, _lane_cols(j)] = a.astype(BF16)
            contrib = _dot_nt(dq_ref[gi], w_ref[gi])
            dh = contrib if dh is None else dh + contrib
        dx, dn = _rms_bwd(dh, x_ref[...], g_ref[...])
        dx_ref[...] = dxo_ref[...] + dx
        dn_ref[...] += dn

    row_spec = pl.BlockSpec((tm, D_MODEL), lambda t: (t, 0))
    vec_spec = pl.BlockSpec((1, D_MODEL), lambda t: (0, 0))
    tab_spec = pl.BlockSpec((tm, LANE), lambda t: (t, 0))
    part_specs = [pl.BlockSpec((d, tm // d, QKV_PAD), lambda t: (0, t, 0)) for d in GROUP_DIL]
    return _pcall(
        body, name="qkv_bwd", grid=(t_len // tm,),
        in_specs=part_specs + [pl.BlockSpec((3, D_MODEL, QKV_PAD), lambda t: (0, 0, 0)),
                               row_spec, row_spec, vec_spec, tab_spec, tab_spec],
        out_specs=[pl.BlockSpec((3, tm, QKV_PAD), lambda t: (0, t, 0)), row_spec, vec_spec],
        out_shape=[_sds((3, t_len, QKV_PAD), BF16), _sds((t_len, D_MODEL), F32), _sds((1, D_MODEL), F32)],
        scratch_shapes=[pltpu.VMEM((3, N_SLABS, tm, LANE), F32)],
        semantics=("arbitrary",),
    )(*dqkv_parts, w_pad, dxo, x, g_row, cos, sin)


def final_fwd_bwd(x, g_row, target):
    t_len = x.shape[0]
    tm = _tile_rows(t_len)

    def body(x_ref, g_ref, tgt_ref, dx_ref, dn_ref, loss_ref):
        @pl.when(pl.program_id(0) == 0)
        def _():
            dn_ref[...] = jnp.zeros_like(dn_ref)
            loss_ref[...] = jnp.zeros_like(loss_ref)

        x_t = x_ref[...]
        g = g_ref[...]
        diff = _rms_fwd(x_t, g) - tgt_ref[...]
        loss_ref[...] += 0.5 * jnp.sum(jnp.mean(diff * diff, axis=-1, keepdims=True), axis=0, keepdims=True)
        dx, dn = _rms_bwd(diff * (1.0 / D_MODEL), x_t, g)
        dx_ref[...] = dx
        dn_ref[...] += dn

    row_spec = pl.BlockSpec((tm, D_MODEL), lambda t: (t, 0))
    vec_spec = pl.BlockSpec((1, D_MODEL), lambda t: (0, 0))
    return _pcall(
        body, name="final_fwd_bwd", grid=(t_len // tm,),
        in_specs=[row_spec, vec_spec, row_spec],
        out_specs=[row_spec, vec_spec, pl.BlockSpec((1, 1), lambda t: (0, 0))],
        out_shape=[_sds((t_len, D_MODEL), F32), _sds((1, D_MODEL), F32), _sds((1, 1), F32)],
        semantics=("arbitrary",),
    )(x, g_row, target)


def pool_bwd(dxo, x, g_row, w_in, w_grp, scale, w_out, zr, duty=None):
    t_len = x.shape[0]
    tm = _tile_rows(t_len)
    nt = t_len // tm

    def body(dxo_ref, x_ref, g_ref, win_ref, wgrp_ref, scale_ref, wout_ref, zr_ref,
             dzs_ref, du_ref, dx_ref, dn_ref, dsc_ref, ebuf, tmp_a, tmp_b):
        i = pl.program_id(0)
        t = nt - 1 - i

        @pl.when(i == 0)
        def _():
            ebuf[pl.ds(tm, POOL_HALO), :] = jnp.zeros((POOL_HALO, D_MODEL), F32)
            dn_ref[...] = jnp.zeros_like(dn_ref)
            dsc_ref[...] = jnp.zeros_like(dsc_ref)

        dxo_t = dxo_ref[...]
        dz = _dot_nt(dxo_t.astype(BF16), wout_ref[...])
        dsc_ref[...] += jnp.sum(dz * zr_ref[...].astype(F32), axis=0, keepdims=True)
        dzs_ref[...] = (dz * scale_ref[...]).astype(BF16)
        row = t * tm + lax.broadcasted_iota(jnp.int32, (tm, 1), 0)
        for gi, w in enumerate(POOL_WINDOWS):
            cols = pl.ds(gi * POOL_GROUP_DIM, POOL_GROUP_DIM)
            dp_g = _dot_nt(dzs_ref[:, cols], wgrp_ref[gi])
            inv_cnt = 1.0 / jnp.minimum(row + 1, w).astype(F32)
            ebuf[pl.ds(0, tm), cols] = dp_g * inv_cnt
            acc = _window_sums(ebuf, cols, (tmp_a, tmp_b), gi + 1, tm, back=False) - dp_g
            du_ref[:, cols] = acc.astype(BF16)
        ebuf[pl.ds(tm, POOL_HALO), :] = ebuf[pl.ds(0, POOL_HALO), :]
        dh = _dot_nt(du_ref[...], win_ref[...])
        dx, dn = _rms_bwd(dh, x_ref[...], g_ref[...])
        dx_ref[...] = dxo_t + dx
        dn_ref[...] += dn

    row_spec = pl.BlockSpec((tm, D_MODEL), lambda i: (nt - 1 - i, 0))
    full = lambda shape: pl.BlockSpec(shape, lambda i: (0,) * len(shape))
    vec = full((1, D_MODEL))
    return _pcall(
        body, name="pool_bwd", grid=(nt,),
        in_specs=[row_spec, row_spec, vec, full((D_MODEL, D_MODEL)), full((4, POOL_GROUP_DIM, POOL_GROUP_DIM)),
                  vec, full((D_MODEL, D_MODEL)), row_spec],
        out_specs=[row_spec, row_spec, row_spec, vec, vec],
        out_shape=[_sds((t_len, D_MODEL), BF16), _sds((t_len, D_MODEL), BF16), _sds((t_len, D_MODEL), F32),
                   _sds((1, D_MODEL), F32), _sds((1, D_MODEL), F32)],
        scratch_shapes=[pltpu.VMEM((tm + POOL_HALO, D_MODEL), F32)]
                       + [pltpu.VMEM((tm + POOL_HALO, POOL_GROUP_DIM), F32)] * 2,
        semantics=("arbitrary",), duty=duty,
    )(dxo, x, g_row, w_in, w_grp, scale, w_out, zr)


def _mesh_pos():
    return lax.axis_index("x"), lax.axis_index("y"), lax.axis_index("c")


def _other_chips(x, y):
    return [(1 - x, y), (x, 1 - y), (1 - x, 1 - y)]


def _remote(src, dst, send_sem, recv_sem, device):
    return pltpu.make_async_remote_copy(src_ref=src, dst_ref=dst, send_sem=send_sem, recv_sem=recv_sem,
                                        device_id=device, device_id_type=MESH)


class _Duty:
    aliases = {}

    def mid(self, ins, outs, sems):
        pass


class Together(_Duty):
    def __init__(self, duties):
        self.duties = duties
        self.ins = [a for d in duties for a in d.ins]
        self.out_shape = [s for d in duties for s in d.out_shape]
        self.scratch = [s for d in duties for s in d.scratch]
        self.aliases = {}
        i0 = o0 = 0
        for d in duties:
            self.aliases.update({i0 + i: o0 + o for i, o in d.aliases.items()})
            i0 += len(d.ins)
            o0 += len(d.out_shape)

    def _each(self, ins, outs, sems):
        i0 = o0 = s0 = 0
        for d in self.duties:
            ni, no, ns = len(d.ins), len(d.out_shape), len(d.scratch)
            yield d, ins[i0:i0 + ni], outs[o0:o0 + no], sems[s0:s0 + ns]
            i0, o0, s0 = i0 + ni, o0 + no, s0 + ns

    def split(self, outs):
        return [list(o) for _, _, o, _ in self._each(self.ins, outs, self.scratch)]

    def start(self, ins, outs, sems):
        for d, i, o, s in self._each(ins, outs, sems):
            d.start(i, o, s)

    def mid(self, ins, outs, sems):
        for d, i, o, s in self._each(ins, outs, sems):
            d.mid(i, o, s)

    def finish(self, ins, outs, sems):
        for d, i, o, s in self._each(ins, outs, sems):
            d.finish(i, o, s)


def run_duty(duty, name):
    d_in, d_out = len(duty.ins), len(duty.out_shape)

    def body(*refs):
        ins, outs, sems = refs[:d_in], refs[d_in:d_in + d_out], refs[d_in + d_out:]
        duty.start(ins, outs, sems)
        duty.mid(ins, outs, sems)
        duty.finish(ins, outs, sems)

    return pl.pallas_call(
        body, name=name, out_shape=list(duty.out_shape), in_specs=[_ANY] * d_in, out_specs=[_ANY] * d_out,
        scratch_shapes=list(duty.scratch), input_output_aliases=dict(duty.aliases),
        compiler_params=pltpu.CompilerParams(has_side_effects=True),
    )(*duty.ins)


class GatherWeights(_Duty):
    N_COPIES = 7

    def __init__(self, shards):
        n = self.n = len(shards)
        self.halves = [s.shape[0] // 2 for s in shards]
        my_slot = 2 * lax.axis_index("x") + lax.axis_index("y")
        staged = [lax.dynamic_update_slice(lax.empty((N_SHARDS,) + s.shape, s.dtype), s[None], (my_slot, 0, 0))
                  for s in shards]
        self.ins = list(shards) + staged
        self.out_shape = [_sds((N_SHARDS,) + s.shape, s.dtype) for s in shards]
        self.aliases = {n + a: a for a in range(n)}
        self.scratch = [pltpu.SemaphoreType.DMA((n, self.N_COPIES)), pltpu.SemaphoreType.DMA((n, self.N_COPIES))]

    def _copies(self, ins, outs, sems, a):
        x, y, c = _mesh_pos()
        h = self.halves[a]
        q = h // 2
        sibling, to_x, to_y = (x, y, 1 - c), (1 - x, y, c), (x, 1 - y, c)
        me, of_x, of_y, of_d = 2 * x + y, 2 * (1 - x) + y, 2 * x + 1 - y, 2 * (1 - x) + 1 - y
        half = lambda core: pl.ds(core * h, h)
        quarter = lambda k: pl.ds(c * h + k * q, q)
        out = outs[a]
        mine = ins[a].at[half(c)]

        def copy(k, src, dst, device):
            return _remote(src, dst, sems[0].at[a, k], sems[1].at[a, k], device)

        same = lambda k, ref, device: copy(k, ref, ref, device)
        sent = [copy(0, mine, out.at[me, half(c)], to_x),
                copy(1, mine, out.at[me, half(c)], to_y),
                same(2, out.at[of_x, quarter(0)], to_y),
                same(3, out.at[of_y, quarter(1)], to_x),
                same(4, out.at[of_x, half(c)], sibling),
                same(5, out.at[of_y, half(c)], sibling),
                same(6, out.at[of_d, half(c)], sibling)]
        landing = [out.at[of_x, half(c)], out.at[of_y, half(c)], out.at[of_d, quarter(0)], out.at[of_d, quarter(1)],
                   out.at[of_x, half(1 - c)], out.at[of_y, half(1 - c)], out.at[of_d, half(1 - c)]]
        return sent, [same(k, ref, sibling) for k, ref in enumerate(landing)]

    def start(self, ins, outs, sems):
        for a in range(self.n):
            sent, _ = self._copies(ins, outs, sems, a)
            sent[0].start()
            sent[1].start()

    def mid(self, ins, outs, sems):
        for a in range(self.n):
            sent, lands = self._copies(ins, outs, sems, a)
            lands[0].wait_recv()
            sent[2].start()
            sent[4].start()
            lands[1].wait_recv()
            sent[3].start()
            sent[5].start()

    def finish(self, ins, outs, sems):
        for a in range(self.n):
            sent, lands = self._copies(ins, outs, sems, a)
            lands[2].wait_recv()
            lands[3].wait_recv()
            sent[6].start()
        for a in range(self.n):
            sent, lands = self._copies(ins, outs, sems, a)
            for cp in lands[4:]:
                cp.wait_recv()
            for cp in sent:
                cp.wait_send()


class GradReducer:
    def __init__(self, c_idx, pos_idx):
        self.c_idx, self.pos_idx = c_idx, pos_idx
        self.in_flight = []
        self.done = {}

    def push(self, name, grad):
        self.in_flight.append(dict(name=name, stage="halves", data=grad))

    def _duties(self):
        make = {"halves": SiblingHalves, "exchange": ChipExchange, "share": SiblingShare}
        return Together([make[w["stage"]]([w["data"]]) for w in self.in_flight])

    def _advance(self, duties, outs):
        still = []
        for w, (res,) in zip(self.in_flight, duties.split(outs)):
            if w["stage"] == "halves":
                partial = add_my_half(w["data"], res, self.c_idx, f"rs_add_{w['name']}")
                still.append(dict(name=w["name"], stage="exchange", data=partial))
            elif w["stage"] == "exchange":
                reduced = sum_slots(res, w["data"], self.pos_idx, f"rs_sum_{w['name']}")
                still.append(dict(name=w["name"], stage="share", data=reduced))
            else:
                self.done[w["name"]] = res
        self.in_flight = still

    def carried_by(self, fn, *args, **kw):
        if not self.in_flight:
            return fn(*args, **kw)
        duties = self._duties()
        out, duty_outs = fn(*args, duty=duties, **kw)
        self._advance(duties, duty_outs)
        return out

    def drain(self, name):
        step = 0
        while self.in_flight:
            duties = self._duties()
            self._advance(duties, run_duty(duties, f"{name}{step}"))
            step += 1


class SiblingHalves(_Duty):
    def __init__(self, grads):
        n = len(grads)
        self.halves = [g.shape[1] // 2 for g in grads]
        self.ins = list(grads)
        self.out_shape = [_sds((N_SHARDS, h, g.shape[2]), g.dtype) for g, h in zip(grads, self.halves)]
        self.scratch = [pltpu.SemaphoreType.DMA((n,)), pltpu.SemaphoreType.DMA((n,))]

    def _copies(self, ins, outs, sems):
        x, y, c = _mesh_pos()
        return [_remote(ins[a].at[:, pl.ds((1 - c) * h, h)], outs[a], sems[0].at[a], sems[1].at[a], (x, y, 1 - c))
                for a, h in enumerate(self.halves)]

    def start(self, ins, outs, sems):
        for cp in self._copies(ins, outs, sems):
            cp.start()

    def finish(self, ins, outs, sems):
        for cp in self._copies(ins, outs, sems):
            cp.wait()


class ChipExchange(_Duty):
    def __init__(self, parts):
        n = self.n = len(parts)
        self.ins = list(parts)
        self.out_shape = [_sds(p.shape, p.dtype) for p in parts]
        self.scratch = [pltpu.SemaphoreType.DMA((n, 3)), pltpu.SemaphoreType.DMA((n, 3))]

    def _copies(self, ins, outs, sems, arriving):
        x, y, c = _mesh_pos()
        cps = []
        for a in range(self.n):
            for j, chip in enumerate(_other_chips(x, y)):
                theirs = 2 * chip[0] + chip[1]
                src = outs[a].at[theirs] if arriving else ins[a].at[theirs]
                dst = outs[a].at[theirs] if arriving else outs[a].at[2 * x + y]
                cps.append(_remote(src, dst, sems[0].at[a, j], sems[1].at[a, j], (*chip, c)))
        return cps

    def start(self, ins, outs, sems):
        for cp in self._copies(ins, outs, sems, False):
            cp.start()

    def finish(self, ins, outs, sems):
        for cp in self._copies(ins, outs, sems, True):
            cp.wait_recv()
        for cp in self._copies(ins, outs, sems, False):
            cp.wait_send()


class SiblingShare(_Duty):
    def __init__(self, reduced):
        n = self.n = len(reduced)
        self.ins = list(reduced)
        self.out_shape = [_sds(r.shape, r.dtype) for r in reduced]
        self.aliases = {a: a for a in range(n)}
        self.scratch = [pltpu.SemaphoreType.DMA((n,)), pltpu.SemaphoreType.DMA((n,))]

    def _copies(self, outs, sems, half_of):
        x, y, c = _mesh_pos()
        cps = []
        for a in range(self.n):
            h = outs[a].shape[0] // 2
            rows = outs[a].at[pl.ds(half_of(c) * h, h)]
            cps.append(_remote(rows, rows, sems[0].at[a], sems[1].at[a], (x, y, 1 - c)))
        return cps

    def start(self, ins, outs, sems):
        for cp in self._copies(outs, sems, lambda core: core):
            cp.start()

    def finish(self, ins, outs, sems):
        for cp in self._copies(outs, sems, lambda core: 1 - core):
            cp.wait_recv()
        for cp in self._copies(outs, sems, lambda core: core):
            cp.wait_send()


def allreduce_small(v):
    def body(v_ref, o_ref, buf, send_sems, recv_sems):
        x, y, c = _mesh_pos()
        me = 4 * x + 2 * y + c
        buf[me] = v_ref[...]
        flip = lambda p, f: 1 - p if f else p
        peers = [(flip(x, k & 4), flip(y, k & 2), flip(c, k & 1)) for k in range(1, N_DEV)]
        cps = []
        for k, peer in enumerate(peers):
            cp = _remote(v_ref, buf.at[me], send_sems.at[k], recv_sems.at[k], peer)
            cp.start()
            cps.append(cp)
        for k, peer in enumerate(peers):
            slot = buf.at[4 * peer[0] + 2 * peer[1] + peer[2]]
            _remote(slot, slot, send_sems.at[k], recv_sems.at[k], peer).wait_recv()
        for cp in cps:
            cp.wait_send()
        acc = buf[0]
        for i in range(1, N_DEV):
            acc = acc + buf[i]
        o_ref[...] = acc

    vm = pl.BlockSpec(memory_space=pltpu.VMEM)
    return pl.pallas_call(
        body, name="allreduce_small", out_shape=_sds(v.shape, v.dtype), in_specs=[vm], out_specs=vm,
        scratch_shapes=[pltpu.VMEM((N_DEV,) + v.shape, v.dtype),
                        pltpu.SemaphoreType.DMA((N_DEV - 1,)), pltpu.SemaphoreType.DMA((N_DEV - 1,))],
        compiler_params=pltpu.CompilerParams(has_side_effects=True),
    )(v)


def add_my_half(grad, theirs, c_idx, name):
    _, r, cols = grad.shape
    h = r // 2

    def body(c_ref, g_ref, t_ref, o_ref):
        o_ref[...] = (g_ref[...] + t_ref[...]).astype(BF16)

    slot = pl.BlockSpec((None, h, cols), lambda s, c: (s, 0, 0))
    grid_spec = pltpu.PrefetchScalarGridSpec(
        num_scalar_prefetch=1, grid=(N_SHARDS,),
        in_specs=[pl.BlockSpec((None, h, cols), lambda s, c: (s, c[0], 0)), slot], out_specs=slot)
    return pl.pallas_call(
        body, name=name, grid_spec=grid_spec, out_shape=_sds((N_SHARDS, h, cols), BF16),
        compiler_params=pltpu.CompilerParams(dimension_semantics=("arbitrary",), vmem_limit_bytes=VMEM_LIMIT_BYTES),
    )(c_idx, grad, theirs)


def sum_slots(received, mine, pos_idx, name):
    _, h, cols = received.shape

    def body(pos_ref, r_ref, m_ref, o_ref):
        acc = None
        for k in range(N_SHARDS):
            term = jnp.where(pos_ref[0] == k, m_ref[k], r_ref[k]).astype(F32)
            acc = term if acc is None else acc + term
        o_ref[...] = acc

    whole = pl.BlockSpec((N_SHARDS, h, cols), lambda i, pos: (0, 0, 0))
    grid_spec = pltpu.PrefetchScalarGridSpec(
        num_scalar_prefetch=1, grid=(1,), in_specs=[whole, whole],
        out_specs=pl.BlockSpec((h, cols), lambda i, pos: (pos[1], 0)))
    return pl.pallas_call(
        body, name=name, grid_spec=grid_spec, out_shape=_sds((2 * h, cols), F32),
        compiler_params=pltpu.CompilerParams(dimension_semantics=("arbitrary",), vmem_limit_bytes=VMEM_LIMIT_BYTES),
    )(pos_idx, received, mine)


def adamw(name, grads, w, m, v):
    n_layers, r, cols = w.shape
    tr = r // 2 if r % 16 == 0 else r
    bias1 = 1.0 - ADAM_B1 ** ADAM_STEP
    bias2 = 1.0 - ADAM_B2 ** ADAM_STEP

    def body(*refs):
        g_refs = refs[:n_layers]
        w_ref, m_ref, v_ref, go_ref, d_ref, mo_ref, vo_ref = refs[n_layers:]
        g = g_refs[0][...]
        for layer in range(1, n_layers):
            g = jnp.where(pl.program_id(0) == layer, g_refs[layer][...], g)
        m_new = ADAM_B1 * m_ref[...] + (1.0 - ADAM_B1) * g
        v_new = ADAM_B2 * v_ref[...] + (1.0 - ADAM_B2) * (g * g)
        m_hat = m_new / bias1
        v_hat = v_new / bias2
        go_ref[...] = g
        d_ref[...] = -ADAM_LR * (m_hat / (jnp.sqrt(v_hat) + ADAM_EPS) + ADAM_WD * w_ref[...])
        mo_ref[...] = m_new
        vo_ref[...] = v_new

    g_spec = pl.BlockSpec((tr, cols), lambda l, i: (i, 0))
    lay_spec = pl.BlockSpec((None, tr, cols), lambda l, i: (l, i, 0))
    shape = _sds((n_layers, r, cols), F32)
    return _pcall(
        body, name=name, grid=(n_layers, r // tr),
        in_specs=[g_spec] * n_layers + [lay_spec] * 3, out_specs=[lay_spec] * 4,
        out_shape=[shape] * 4, semantics=("arbitrary", "arbitrary"),
    )(*grads, w, m, v)


def kernel(x, norm_mix, norm_ffn, norm_final, pool_w_in, pool_w_group, pool_scale, pool_w_out, attn_w_qkv, attn_w_out, ffn_w_gate, ffn_w_up, ffn_w_down, loss_target, m_norm_mix, m_norm_ffn, m_norm_final, m_pool_w_in, m_pool_w_group, m_pool_scale, m_pool_w_out, m_attn_w_qkv, m_attn_w_out, m_ffn_w_gate, m_ffn_w_up, m_ffn_w_down, v_norm_mix, v_norm_ffn, v_norm_final, v_pool_w_in, v_pool_w_group, v_pool_scale, v_pool_w_out, v_attn_w_qkv, v_attn_w_out, v_ffn_w_gate, v_ffn_w_up, v_ffn_w_down):
    t_len = x.shape[1]
    x0 = x.reshape(t_len, D_MODEL)
    target = loss_target.reshape(t_len, D_MODEL)
    row = lambda a: a.reshape(1, D_MODEL)

    grp_rows = POOL_GROUP_DIM // N_SHARDS
    bf = lambda a: a.astype(BF16)
    gate_t, up_t = jnp.swapaxes(ffn_w_gate, 1, 2), jnp.swapaxes(ffn_w_up, 1, 2)
    pool_shards = [bf(pool_w_in[0]), bf(pool_w_group[0].reshape(4 * grp_rows, POOL_GROUP_DIM)), bf(pool_w_out[0])]
    ffn0_shards = [bf(gate_t[0]), bf(up_t[0]), bf(ffn_w_down[0])]
    late_shards = [bf(attn_w_qkv[0]), bf(attn_w_out[0]), bf(gate_t[1]), bf(up_t[1]), bf(ffn_w_down[1])]
    cos, sin = rope_tables(t_len)
    c_idx = lax.axis_index("c").astype(jnp.int32).reshape(1)
    pos_idx = jnp.stack([2 * lax.axis_index("x") + lax.axis_index("y"), lax.axis_index("c")]).astype(jnp.int32)
    chip_rows = lambda g: g.reshape(N_SHARDS, D_MODEL // N_SHARDS, D_MODEL)

    g_pool = run_duty(GatherWeights(pool_shards), "gather_pool")
    w_in = g_pool[0].reshape(D_MODEL, D_MODEL)
    w_grp = g_pool[1].reshape(N_SHARDS, 4, grp_rows, POOL_GROUP_DIM).transpose(1, 0, 2, 3).reshape(
        4, POOL_GROUP_DIM, POOL_GROUP_DIM)
    w_out = g_pool[2].reshape(D_MODEL, D_MODEL)
    (h0, p, zr, z, x1), ffn0 = pool_fwd(x0, row(norm_mix[0]), w_in, w_grp, pool_scale, w_out,
                                        duty=GatherWeights(ffn0_shards))
    (h1, gate0, up0, act0, x2), late = ffn_fwd(x1, row(norm_ffn[0]), *ffn0, "ffn_fwd0",
                                               duty=GatherWeights(late_shards))
    w_qkv = pad_qkv_weight(late[0])
    w_ao = jnp.concatenate(pad_groups(late[1].reshape(D_MODEL, D_MODEL), 0), axis=0)
    ffn1 = late[2:5]
    h2, *qkv_parts = qkv_fwd(x2, row(norm_mix[1]), w_qkv, cos, sin)
    o_parts, lse_parts = [], []
    for gi in range(3):
        o_g, lse_g = attn_fwd(qkv_parts[gi], gi, f"attn_fwd_g{gi}")
        o_parts.append(o_g)
        lse_parts.append(lse_g)
    x3, merged, o_nat, lse_nat = attn_out_fwd(x2, o_parts, lse_parts, w_ao)
    h3, gate1, up1, act1, x4 = ffn_fwd(x3, row(norm_ffn[1]), *ffn1, "ffn_fwd1")
    dx4, d_norm_final, loss_local = final_fwd_bwd(x4, row(norm_final), target)

    red = GradReducer(c_idx, pos_idx)
    dgate1, dup1, dx3, d_nf1 = ffn_bwd(dx4, x3, row(norm_ffn[1]), gate1, up1, *ffn1, "ffn_bwd1")
    g_gate1 = wgrad_row_sharded("wgrad_gate1", dgate1, h3)
    g_up1 = wgrad_row_sharded("wgrad_up1", dup1, h3)
    g_down1 = wgrad_row_sharded("wgrad_down1", act1, dx4)
    red.push("gate1", g_gate1)
    red.push("up1", g_up1)
    do_parts, c_parts = red.carried_by(attn_out_bwd, dx3, w_ao, o_nat, lse_nat)
    g_ao = wgrad_full("wgrad_attn_out", merged, dx3)
    red.push("down1", g_down1)
    red.push("attn_out", chip_rows(unpad_groups(jnp.split(g_ao, 3, axis=0), 0)))
    dqkv_parts = [red.carried_by(attn_bwd, qkv_parts[gi], do_parts[gi], lse_parts[gi], c_parts[gi], gi,
                                 f"attn_bwd_g{gi}") for gi in range(3)]
    dqkv, dx2, d_nm1 = qkv_bwd(dqkv_parts, w_qkv, dx3, x2, row(norm_mix[1]), cos, sin)
    red.push("qkv", unpad_qkv_grad(wgrad_col_sharded("wgrad_qkv", h2, dqkv)))

    dgate0, dup0, dx1, d_nf0 = red.carried_by(ffn_bwd, dx2, x1, row(norm_ffn[0]), gate0, up0, *ffn0, "ffn_bwd0")
    red.push("gate0", red.carried_by(wgrad_row_sharded, "wgrad_gate0", dgate0, h1))
    red.push("up0", red.carried_by(wgrad_row_sharded, "wgrad_up0", dup0, h1))
    red.push("down0", red.carried_by(wgrad_row_sharded, "wgrad_down0", act0, dx2))
    red.push("pool_out", chip_rows(red.carried_by(wgrad_full, "wgrad_pool_out", z, dx1)))
    dzs, du, dx0, d_nm0, d_scale = pool_bwd(dx1, x0, row(norm_mix[0]), w_in, w_grp, pool_scale, w_out, zr)
    g_grp = red.carried_by(wgrad_pool_group, "wgrad_pool_group", p, dzs)
    red.push("pool_group", g_grp.reshape(N_SHARDS, 4 * grp_rows, POOL_GROUP_DIM))
    red.push("pool_in", chip_rows(red.carried_by(wgrad_full, "wgrad_pool_in", h0, du)))
    red.drain("rs_tail")
    full = [red.done[nm] for nm in ("pool_in", "pool_group", "pool_out", "qkv", "attn_out",
                                    "gate0", "gate1", "up0", "up1", "down0", "down1")]

    zero_row = jnp.zeros((1, D_MODEL), F32)
    small = jnp.concatenate([d_nm0, d_nm1, d_nf0, d_nf1, d_norm_final, d_scale,
                             jnp.broadcast_to(loss_local, (1, D_MODEL)), zero_row], axis=0)
    small = allreduce_small(small)
    loss = small[6, 0]

    pack = lambda a, b, c, d: jnp.concatenate([a, b, row(c), d, zero_row, zero_row], axis=0)[None]
    sg, sd, sm, sv = adamw("adamw_small", [small],
                           pack(norm_mix, norm_ffn, norm_final, pool_scale),
                           pack(m_norm_mix, m_norm_ffn, m_norm_final, m_pool_scale),
                           pack(v_norm_mix, v_norm_ffn, v_norm_final, v_pool_scale))
    unpack = lambda a: (a[0, 0:2], a[0, 2:4], a[0, 4], a[0, 5:6])

    def update(name, grads, w, m, v, transposed=False):
        if transposed:
            w, m, v = (jnp.swapaxes(a, 1, 2) for a in (w, m, v))
        n_layers = len(grads)
        shp = (n_layers,) + grads[0].shape
        outs = [o.reshape(w.shape) for o in adamw(name, grads, w.reshape(shp), m.reshape(shp), v.reshape(shp))]
        return [jnp.swapaxes(o, 1, 2) for o in outs] if transposed else outs

    big = [
        update("adamw_pool_in", [full[0]], pool_w_in, m_pool_w_in, v_pool_w_in),
        update("adamw_pool_group", [full[1]], pool_w_group, m_pool_w_group, v_pool_w_group),
        update("adamw_pool_out", [full[2]], pool_w_out, m_pool_w_out, v_pool_w_out),
        update("adamw_qkv", [full[3]], attn_w_qkv, m_attn_w_qkv, v_attn_w_qkv),
        update("adamw_attn_out", [full[4]], attn_w_out, m_attn_w_out, v_attn_w_out),
        update("adamw_gate", [full[5], full[6]], ffn_w_gate, m_ffn_w_gate, v_ffn_w_gate, transposed=True),
        update("adamw_up", [full[7], full[8]], ffn_w_up, m_ffn_w_up, v_ffn_w_up, transposed=True),
        update("adamw_down", [full[9], full[10]], ffn_w_down, m_ffn_w_down, v_ffn_w_down),
    ]

    def leaves(k, small_vals):
        nm, nf, nfin, psc = unpack(small_vals)
        return [nm, nf, nfin, big[0][k], big[1][k], psc, big[2][k], big[3][k], big[4][k],
                big[5][k], big[6][k], big[7][k]]

    grad_x = dx0.reshape(x.shape)
    return (loss, grad_x, *leaves(0, sg), *leaves(1, sd), *leaves(2, sm), *leaves(3, sv))
```

```python
import math

import jax
import jax.numpy as jnp
from jax import lax
from jax.experimental import pallas as pl
from jax.experimental.pallas import tpu as pltpu

F32 = jnp.float32
BF16 = jnp.bfloat16

D_MODEL = 1024
N_SHARDS = 4
N_DEV = 8
D_FF = 2816
FF_SHARD = D_FF // N_SHARDS
HEAD_DIM = 64
QKV_SHARD = 3 * D_MODEL // N_SHARDS
POOL_WINDOWS = (2, 4, 8, 16)
POOL_GROUP_DIM = 256
POOL_HALO = 32
ATTN_W = 128
GROUP_LANES = (0, 384, 704, 1024)
GROUP_HEADS = (6, 5, 5)
GROUP_DIL = (1, 4, 16)
ROPE_THETA = 10000.0
EPS = 1e-6
NEG_INF = -1e30
LANE = 128
VMEM_LIMIT_BYTES = 60 * 1024 * 1024

ADAM_LR = 0.001
ADAM_B1 = 0.9
ADAM_B2 = 0.999
ADAM_EPS = 1e-08
ADAM_WD = 0.01
ADAM_STEP = 10

NT_DIMS = (((1,), (1,)), ((), ()))
TN_DIMS = (((0,), (0,)), ((), ()))
MESH = pl.DeviceIdType.MESH


_ANY = pl.BlockSpec(memory_space=pl.ANY)


def _pcall(body, *, name, out_shape, grid=None, in_specs=None, out_specs=None, scratch_shapes=(),
           semantics=None, duty=None):
    kw = {}
    if in_specs is not None and duty is None:
        kw["in_specs"] = in_specs
    if out_specs is not None and duty is None:
        kw["out_specs"] = out_specs
    if grid is not None:
        kw["grid"] = grid
    params = dict(dimension_semantics=semantics, vmem_limit_bytes=VMEM_LIMIT_BYTES)
    if duty is None:
        return pl.pallas_call(body, name=name, out_shape=out_shape, scratch_shapes=list(scratch_shapes),
                              compiler_params=pltpu.CompilerParams(**params), **kw)

    single = not isinstance(out_shape, (list, tuple))
    c_out_shape = [out_shape] if single else list(out_shape)
    c_out_specs = [out_specs] if single else list(out_specs)
    n_in, n_out, n_scr = len(in_specs), len(c_out_shape), len(scratch_shapes)
    d_in, d_out = len(duty.ins), len(duty.out_shape)
    total = math.prod(grid)
    mid_step = int(total * duty.mid_at)

    def wrapped(*refs):
        c_in, d_ins = refs[:n_in], refs[n_in:n_in + d_in]
        o0 = n_in + d_in
        c_outs, d_outs = refs[o0:o0 + n_out], refs[o0 + n_out:o0 + n_out + d_out]
        s0 = o0 + n_out + d_out
        c_scr, d_sems = refs[s0:s0 + n_scr], refs[s0 + n_scr:]
        step = pl.program_id(0)
        for ax in range(1, len(grid)):
            step = step * grid[ax] + pl.program_id(ax)

        @pl.when(step == 0)
        def _():
            duty.start(d_ins, d_outs, d_sems)

        body(*c_in, *c_outs, *c_scr)

        @pl.when(step == mid_step)
        def _():
            duty.mid(d_ins, d_outs, d_sems)

        @pl.when(step == total - 1)
        def _():
            duty.finish(d_ins, d_outs, d_sems)

    call = pl.pallas_call(
        wrapped, name=name, grid=grid,
        in_specs=list(in_specs) + [_ANY] * d_in, out_specs=c_out_specs + [_ANY] * d_out,
        out_shape=c_out_shape + list(duty.out_shape),
        scratch_shapes=list(scratch_shapes) + list(duty.scratch),
        input_output_aliases={n_in + i: n_out + o for i, o in duty.aliases.items()},
        compiler_params=pltpu.CompilerParams(has_side_effects=True, **params))

    def run(*args):
        outs = call(*args, *duty.ins)
        c = outs[:n_out]
        return (c[0] if single else list(c)), list(outs[n_out:])

    return run


def _sds(shape, dtype):
    return jax.ShapeDtypeStruct(tuple(shape), dtype)


def _dot(a, b):
    return jnp.dot(a, b, preferred_element_type=F32)


def _dot_nt(a, b):
    return lax.dot_general(a, b, NT_DIMS, preferred_element_type=F32)


def _dot_tn(a, b):
    return lax.dot_general(a, b, TN_DIMS, preferred_element_type=F32)


def _rms_fwd(x, g):
    r = lax.rsqrt(jnp.mean(x * x, axis=-1, keepdims=True) + EPS)
    return x * r * g


def _rms_bwd(dh, x, g):
    r = lax.rsqrt(jnp.mean(x * x, axis=-1, keepdims=True) + EPS)
    xh = x * r
    dg = jnp.sum(dh * xh, axis=0, keepdims=True)
    dxh = dh * g
    dx = r * (dxh - xh * jnp.mean(dxh * xh, axis=-1, keepdims=True))
    return dx, dg


def _sigmoid(x):
    return 0.5 * jnp.tanh(0.5 * x) + 0.5


def _tile_rows(t):
    return min(512, t)


def _sub_tiles(tm, n_sub=2):
    rows = tm // n_sub
    return [pl.ds(i * rows, rows) for i in range(n_sub)]


def _wgrad_rows(t):
    return min(2048, t)


def _window_sums(buf, cols, tmps, levels, tm, back):
    src, src_cols = buf, cols
    for k in range(1, levels + 1):
        shift = 1 << (k - 1)
        last = k == levels
        if back:
            lo = POOL_HALO if last else 8 * k
            n = tm + POOL_HALO - lo
            val = src[pl.ds(lo, n), src_cols] + src[pl.ds(lo - shift, n), src_cols]
        else:
            lo = 0
            n = tm if last else tm + POOL_HALO - 8 * k
            val = src[pl.ds(0, n), src_cols] + src[pl.ds(shift, n), src_cols]
        if last:
            return val
        tmps[k % 2][pl.ds(lo, n), :] = val
        src, src_cols = tmps[k % 2], slice(None)
def pool_fwd(x, g_row, w_in, w_grp, scale, w_out, duty=None):
    t_len = x.shape[0]
    tm = _tile_rows(t_len)

    def body(x_ref, g_ref, win_ref, wgrp_ref, scale_ref, wout_ref,
             h_ref, p_ref, zr_ref, z_ref, xo_ref, ubuf, tmp_a, tmp_b):
        t = pl.program_id(0)

        @pl.when(t == 0)
        def _():
            ubuf[pl.ds(0, POOL_HALO), :] = jnp.zeros((POOL_HALO, D_MODEL), F32)

        x_t = x_ref[...]
        h = _rms_fwd(x_t, g_ref[...]).astype(BF16)
        h_ref[...] = h
        ubuf[pl.ds(POOL_HALO, tm), :] = _dot(h, win_ref[...])
        row = t * tm + lax.broadcasted_iota(jnp.int32, (tm, 1), 0)
        for gi, w in enumerate(POOL_WINDOWS):
            cols = pl.ds(gi * POOL_GROUP_DIM, POOL_GROUP_DIM)
            u_g = ubuf[pl.ds(POOL_HALO, tm), cols]
            acc = _window_sums(ubuf, cols, (tmp_a, tmp_b), gi + 1, tm, back=True)
            inv_cnt = 1.0 / jnp.minimum(row + 1, w).astype(F32)
            p_g = (acc * inv_cnt - u_g).astype(BF16)
            p_ref[:, cols] = p_g
            z_g = _dot(p_g, wgrp_ref[gi])
            zr_ref[:, cols] = z_g.astype(BF16)
            z_ref[:, cols] = (z_g * scale_ref[:, cols]).astype(BF16)
        ubuf[pl.ds(0, POOL_HALO), :] = ubuf[pl.ds(tm, POOL_HALO), :]
        xo_ref[...] = x_t + _dot(z_ref[...], wout_ref[...])

    row_spec = pl.BlockSpec((tm, D_MODEL), lambda t: (t, 0))
    full2 = lambda shape: pl.BlockSpec(shape, lambda t: (0,) * len(shape))
    return _pcall(
        body, name="pool_fwd", grid=(t_len // tm,),
        in_specs=[row_spec, full2((1, D_MODEL)), full2((D_MODEL, D_MODEL)),
                  full2((4, POOL_GROUP_DIM, POOL_GROUP_DIM)), full2((1, D_MODEL)), full2((D_MODEL, D_MODEL))],
        out_specs=[row_spec] * 5,
        out_shape=[_sds((t_len, D_MODEL), BF16)] * 4 + [_sds((t_len, D_MODEL), F32)],
        scratch_shapes=[pltpu.VMEM((tm + POOL_HALO, D_MODEL), F32)]
                       + [pltpu.VMEM((tm + POOL_HALO, POOL_GROUP_DIM), F32)] * 2,
        semantics=("arbitrary",), duty=duty,
    )(x, g_row, w_in, w_grp, scale, w_out)


def ffn_fwd(x, g_row, w_gate_t, w_up_t, w_down, name, duty=None):
    t_len = x.shape[0]
    tm = min(1024, t_len)

    def body(x_ref, g_ref, wg_ref, wu_ref, wd_ref, h_ref, go_ref, uo_ref, ao_ref, xo_ref, hbuf, acc):
        s = pl.program_id(1)

        @pl.when(s == 0)
        def _():
            h = _rms_fwd(x_ref[...], g_ref[...]).astype(BF16)
            hbuf[...] = h
            h_ref[...] = h
            acc[...] = jnp.zeros_like(acc)

        h = hbuf[...]
        gate = _dot_nt(h, wg_ref[...])
        up = _dot_nt(h, wu_ref[...])
        go_ref[...] = gate.astype(BF16)
        uo_ref[...] = up.astype(BF16)
        act = (gate * _sigmoid(gate) * up).astype(BF16)
        ao_ref[...] = act
        acc[...] += _dot(act, wd_ref[...])

        @pl.when(s == N_SHARDS - 1)
        def _():
            xo_ref[...] = x_ref[...] + acc[...]

    row_spec = pl.BlockSpec((tm, D_MODEL), lambda t, s: (t, 0))
    row_w = pl.BlockSpec((None, FF_SHARD, D_MODEL), lambda t, s: (s, 0, 0))
    act_spec = pl.BlockSpec((None, tm, FF_SHARD), lambda t, s: (s, t, 0))
    return _pcall(
        body, name=name, grid=(t_len // tm, N_SHARDS),
        in_specs=[row_spec, pl.BlockSpec((1, D_MODEL), lambda t, s: (0, 0)), row_w, row_w, row_w],
        out_specs=[row_spec, act_spec, act_spec, act_spec, row_spec],
        out_shape=[_sds((t_len, D_MODEL), BF16)] + [_sds((N_SHARDS, t_len, FF_SHARD), BF16)] * 3
                  + [_sds((t_len, D_MODEL), F32)],
        scratch_shapes=[pltpu.VMEM((tm, D_MODEL), BF16), pltpu.VMEM((tm, D_MODEL), F32)],
        semantics=("arbitrary", "arbitrary"), duty=duty,
    )(x, g_row, w_gate_t, w_up_t, w_down)


def ffn_bwd(dxo, x, g_row, gate, up, w_gate_t, w_up_t, w_down, name, duty=None):
    t_len = x.shape[0]
    tm = _tile_rows(t_len)

    def body(dxo_ref, x_ref, g_ref, gate_ref, up_ref, wg_ref, wu_ref, wd_ref,
             dg_ref, du_ref, dx_ref, dn_ref, dxb, dh):
        t = pl.program_id(0)
        s = pl.program_id(1)

        @pl.when(s == 0)
        def _():
            dxb[...] = dxo_ref[...].astype(BF16)
            dh[...] = jnp.zeros_like(dh)

        @pl.when(jnp.logical_and(s == 0, t == 0))
        def _():
            dn_ref[...] = jnp.zeros_like(dn_ref)

        sub_tiles = _sub_tiles(tm)
        dacts = [_dot_nt(dxb[rows, :], wd_ref[...]) for rows in sub_tiles]
        for rows, dact in zip(sub_tiles, dacts):
            gv = gate_ref[rows, :].astype(F32)
            uv = up_ref[rows, :].astype(F32)
            sg = _sigmoid(gv)
            dgv = (dact * uv * (sg * (1.0 + gv * (1.0 - sg)))).astype(BF16)
            duv = (dact * (gv * sg)).astype(BF16)
            dg_ref[rows, :] = dgv
            du_ref[rows, :] = duv
            dh[rows, :] += _dot(dgv, wg_ref[...]) + _dot(duv, wu_ref[...])

        @pl.when(s == N_SHARDS - 1)
        def _():
            dx, dn = _rms_bwd(dh[...], x_ref[...], g_ref[...])
            dx_ref[...] = dxo_ref[...] + dx
            dn_ref[...] += dn

    row_spec = pl.BlockSpec((tm, D_MODEL), lambda t, s: (t, 0))
    vec_spec = pl.BlockSpec((1, D_MODEL), lambda t, s: (0, 0))
    row_w = pl.BlockSpec((None, FF_SHARD, D_MODEL), lambda t, s: (s, 0, 0))
    act_spec = pl.BlockSpec((None, tm, FF_SHARD), lambda t, s: (s, t, 0))
    act_shape = _sds((N_SHARDS, t_len, FF_SHARD), BF16)
    return _pcall(
        body, name=name, grid=(t_len // tm, N_SHARDS),
        in_specs=[row_spec, row_spec, vec_spec, act_spec, act_spec, row_w, row_w, row_w],
        out_specs=[act_spec, act_spec, row_spec, vec_spec],
        out_shape=[act_shape, act_shape, _sds((t_len, D_MODEL), F32), _sds((1, D_MODEL), F32)],
        scratch_shapes=[pltpu.VMEM((tm, D_MODEL), BF16), pltpu.VMEM((tm, D_MODEL), F32)],
        semantics=("arbitrary", "arbitrary"), duty=duty,
    )(dxo, x, g_row, gate, up, w_gate_t, w_up_t, w_down)


def tn_matmul(name, a, b, a_spec, b_spec, out_shape, out_spec, grid, duty=None):
    def body(a_ref, b_ref, o_ref):
        @pl.when(pl.program_id(len(grid) - 1) == 0)
        def _():
            o_ref[...] = jnp.zeros_like(o_ref)

        res = _dot_tn(a_ref[...].astype(BF16), b_ref[...].astype(BF16))
        o_ref[...] += res.reshape(o_ref.shape)

    return _pcall(body, name=name, grid=grid, in_specs=[a_spec, b_spec], out_specs=out_spec,
                  out_shape=out_shape, semantics=("arbitrary",) * len(grid), duty=duty)(a, b)


def wgrad_full(name, a, b, duty=None):
    t_len, k = a.shape
    n = b.shape[1]
    tt = _wgrad_rows(t_len)
    return tn_matmul(name, a, b,
                     pl.BlockSpec((tt, k), lambda t: (t, 0)), pl.BlockSpec((tt, n), lambda t: (t, 0)),
                     _sds((k, n), F32), pl.BlockSpec((k, n), lambda t: (0, 0)), (t_len // tt,), duty)


def wgrad_col_sharded(name, a, b_sh, duty=None):
    t_len, k = a.shape
    n_sh, _, n = b_sh.shape
    tt = _wgrad_rows(t_len)
    return tn_matmul(name, a, b_sh,
                     pl.BlockSpec((tt, k), lambda s, t: (t, 0)), pl.BlockSpec((None, tt, n), lambda s, t: (s, t, 0)),
                     _sds((n_sh, k, n), F32), pl.BlockSpec((None, k, n), lambda s, t: (s, 0, 0)),
                     (n_sh, t_len // tt), duty)


def wgrad_row_sharded(name, a_sh, b, duty=None):
    t_len, n = b.shape
    n_sh, _, k = a_sh.shape
    tt = _wgrad_rows(t_len)

    def body(a_ref, b_ref, o_ref):
        s = pl.program_id(1)
        res = _dot_tn(a_ref[...], b_ref[...].astype(BF16))

        @pl.when(pl.program_id(0) == 0)
        def _():
            o_ref[s] = res

        @pl.when(pl.program_id(0) > 0)
        def _():
            o_ref[s] += res

    return _pcall(body, name=name, grid=(t_len // tt, n_sh),
                  in_specs=[pl.BlockSpec((None, tt, k), lambda t, s: (s, t, 0)),
                            pl.BlockSpec((tt, n), lambda t, s: (t, 0))],
                  out_specs=pl.BlockSpec((n_sh, k, n), lambda t, s: (0, 0, 0)),
                  out_shape=_sds((n_sh, k, n), F32), semantics=("arbitrary", "arbitrary"), duty=duty)(a_sh, b)


def wgrad_pool_group(name, p, dzs, duty=None):
    t_len = p.shape[0]
    tt = _wgrad_rows(t_len)
    gd = POOL_GROUP_DIM
    rows = gd // N_SHARDS
    return tn_matmul(name, p, dzs,
                     pl.BlockSpec((tt, gd), lambda g, t: (t, g)), pl.BlockSpec((tt, gd), lambda g, t: (t, g)),
                     _sds((N_SHARDS, 4, rows, gd), F32),
                     pl.BlockSpec((N_SHARDS, None, rows, gd), lambda g, t: (0, g, 0, 0)),
                     (4, t_len // tt), duty)


PAD_LANES = 384
QKV_PAD = 3 * PAD_LANES
N_SLABS = QKV_PAD // LANE
GROUP_REAL = tuple(GROUP_LANES[g + 1] - GROUP_LANES[g] for g in range(3))
Q_BLOCK = 512


def pad_groups(w, axis):
    parts = []
    for g in range(3):
        blk = lax.slice_in_dim(w, GROUP_LANES[g], GROUP_LANES[g + 1], axis=axis)
        pad = [(0, 0)] * w.ndim
        pad[axis] = (0, PAD_LANES - GROUP_REAL[g])
        parts.append(jnp.pad(blk, pad))
    return parts


def unpad_groups(parts, axis):
    return jnp.concatenate([lax.slice_in_dim(p, 0, GROUP_REAL[g], axis=axis) for g, p in enumerate(parts)],
                           axis=axis)


def _qkv_pieces(group, part):
    lo, hi = part * D_MODEL + GROUP_LANES[group], part * D_MODEL + GROUP_LANES[group + 1]
    pieces = []
    while lo < hi:
        shard = lo // QKV_SHARD
        end = min(hi, (shard + 1) * QKV_SHARD)
        pieces.append((shard, lo - shard * QKV_SHARD, end - shard * QKV_SHARD))
        lo = end
    return pieces


def pad_qkv_weight(w_qkv_sh):
    groups = []
    for g in range(3):
        cols = []
        for part in range(3):
            cols += [w_qkv_sh[s][:, lo:hi] for s, lo, hi in _qkv_pieces(g, part)]
            if GROUP_REAL[g] < PAD_LANES:
                cols.append(jnp.zeros((D_MODEL, PAD_LANES - GROUP_REAL[g]), w_qkv_sh.dtype))
        groups.append(jnp.concatenate(cols, axis=1))
    return jnp.stack(groups)


def unpad_qkv_grad(g_pad):
    shard_cols = [[] for _ in range(N_SHARDS)]
    for part in range(3):
        for g in range(3):
            at = part * PAD_LANES
            for s, lo, hi in _qkv_pieces(g, part):
                shard_cols[s].append(g_pad[g][:, at:at + hi - lo])
                at += hi - lo
    return jnp.stack([jnp.concatenate(cols, axis=1) for cols in shard_cols])


def rope_tables(t_len):
    inv_freq = 1.0 / (ROPE_THETA ** (jnp.arange(0, HEAD_DIM, 2, dtype=F32) / HEAD_DIM))
    ang = jnp.arange(t_len, dtype=F32)[:, None] * inv_freq[None, :]
    cos_h, sin_h = lax.optimization_barrier((jnp.cos(ang), jnp.sin(ang)))
    reps = (1, 2 * LANE // HEAD_DIM)
    return jnp.tile(cos_h, reps), jnp.tile(sin_h, reps)


def _rot_half(v):
    n = v.shape[1]
    lane = lax.broadcasted_iota(jnp.int32, v.shape, 1)
    return jnp.where(lane % HEAD_DIM < HEAD_DIM // 2,
                     -pltpu.roll(v, n - HEAD_DIM // 2, 1), pltpu.roll(v, HEAD_DIM // 2, 1))


def _lane_cols(j):
    return slice(j * LANE, (j + 1) * LANE)


def _to_residue_major(slab, j_src, dst_ref, j_dst, dil, rows):
    for r in range(dil):
        dst_ref[r, :, _lane_cols(j_dst)] = slab[j_src, pl.ds(r, rows // dil, stride=dil), :].astype(dst_ref.dtype)


def _to_natural(src_ref, j_src, slab, j_dst, dil, rows):
    for r in range(dil):
        slab[j_dst, pl.ds(r, rows // dil, stride=dil), :] = src_ref[r, :, _lane_cols(j_src)].astype(F32)


def qkv_fwd(x, g_row, w_pad, cos, sin):
    t_len = x.shape[0]
    tm = _tile_rows(t_len)

    def body(x_ref, g_ref, w_ref, cos_ref, sin_ref, h_ref, o1_ref, o4_ref, o16_ref, slabs):
        h = _rms_fwd(x_ref[...], g_ref[...]).astype(BF16)
        h_ref[...] = h
        accs = [_dot(h, w_ref[gi]) for gi in range(3)]
        cos_t = cos_ref[...]
        sin_t = sin_ref[...]
        for gi, (dil, o_ref) in enumerate(zip(GROUP_DIL, (o1_ref, o4_ref, o16_ref))):
            slab = slabs.at[gi]
            for j in range(N_SLABS):
                a = accs[gi][:, _lane_cols(j)]
                if j < 6:
                    a = a * cos_t + _rot_half(a) * sin_t
                if j < 3:
                    a = a * (HEAD_DIM ** -0.5)
                if dil == 1:
                    o_ref[0, :, _lane_cols(j)] = a.astype(BF16)
                else:
                    slab[j] = a
                    _to_residue_major(slab, j, o_ref, j, dil, tm)

    row_spec = pl.BlockSpec((tm, D_MODEL), lambda t: (t, 0))
    tab_spec = pl.BlockSpec((tm, LANE), lambda t: (t, 0))
    out_specs = [row_spec] + [pl.BlockSpec((d, tm // d, QKV_PAD), lambda t: (0, t, 0)) for d in GROUP_DIL]
    out_shape = [_sds((t_len, D_MODEL), BF16)] + [_sds((d, t_len // d, QKV_PAD), BF16) for d in GROUP_DIL]
    return _pcall(
        body, name="qkv_fwd", grid=(t_len // tm,),
        in_specs=[row_spec, pl.BlockSpec((1, D_MODEL), lambda t: (0, 0)),
                  pl.BlockSpec((3, D_MODEL, QKV_PAD), lambda t: (0, 0, 0)), tab_spec, tab_spec],
        out_specs=out_specs, out_shape=out_shape,
        scratch_shapes=[pltpu.VMEM((3, N_SLABS, tm, LANE), F32)],
        semantics=("arbitrary",),
    )(x, g_row, w_pad, cos, sin)


def _band_masks():
    qi = lax.broadcasted_iota(jnp.int32, (ATTN_W, 2 * ATTN_W), 0)
    kj = lax.broadcasted_iota(jnp.int32, (ATTN_W, 2 * ATTN_W), 1)
    dist = ATTN_W + qi - kj
    band = (dist >= 0) & (dist <= ATTN_W)
    return band, band & (kj >= ATTN_W)


def _half_masks():
    lane = lax.broadcasted_iota(jnp.int32, (1, LANE), 1)
    return [lane < HEAD_DIM, lane >= HEAD_DIM]


def _live_halves(gi, j):
    hms = _half_masks()
    return hms if (gi == 0 or j < 2) else hms[:1]


def attn_fwd(qkv_g, gi, name):
    dil, l_len, _ = qkv_g.shape
    qb = min(Q_BLOCK, l_len)
    nsub = qb // ATTN_W

    def body(q_ref, kc_ref, kp_ref, vc_ref, vp_ref, o_ref, lse_ref, kbuf, vbuf):
        n = pl.program_id(1)
        kbuf[pl.ds(0, ATTN_W), :] = kp_ref[...]
        kbuf[pl.ds(ATTN_W, qb), :] = kc_ref[...]
        vbuf[pl.ds(0, ATTN_W), :] = vp_ref[...]
        vbuf[pl.ds(ATTN_W, qb), :] = vc_ref[...]
        band, band_first = _band_masks()

        def sub(b, carry):
            r0 = pl.multiple_of(b * ATTN_W, ATTN_W)
            mask = band_first | (band & (n + b > 0))
            krows = pl.ds(r0, 2 * ATTN_W)
            scores = []
            for j in range(3):
                q = q_ref[pl.ds(r0, ATTN_W), _lane_cols(j)]
                for hm in _live_halves(gi, j):
                    scores.append(_dot_nt(jnp.where(hm, q, jnp.zeros_like(q)), kbuf[krows, _lane_cols(j)]))
            scores = iter(scores)
            head_lane = lax.broadcasted_iota(jnp.int32, (1, LANE), 1)
            lse = jnp.zeros((ATTN_W, LANE), F32)
            for j in range(3):
                cols = _lane_cols(j)
                v = vbuf[krows, cols]
                o = jnp.zeros((ATTN_W, LANE), F32)
                for half, hm in enumerate(_live_halves(gi, j)):
                    s = jnp.where(mask, next(scores), NEG_INF)
                    m = jnp.max(s, axis=-1, keepdims=True)
                    e = jnp.exp(s - m)
                    den = jnp.sum(e, axis=-1, keepdims=True)
                    p = (e * (1.0 / den)).astype(BF16)
                    o = jnp.where(hm, _dot(p, v), o)
                    lse = jnp.where(head_lane == 2 * j + half, m + jnp.log(den), lse)
                o_ref[pl.ds(r0, ATTN_W), cols] = o.astype(BF16)
            lse_ref[pl.ds(r0, ATTN_W), :] = lse
            return carry

        lax.fori_loop(0, nsub, sub, 0)

    cur = lambda c: pl.BlockSpec((None, qb, PAD_LANES), lambda r, n: (r, n, c))
    prev = lambda c: pl.BlockSpec((None, ATTN_W, PAD_LANES), lambda r, n: (r, jnp.maximum(n * nsub - 1, 0), c))
    return _pcall(
        body, name=name, grid=(dil, l_len // qb),
        in_specs=[cur(0), cur(1), prev(1), cur(2), prev(2)],
        out_specs=[pl.BlockSpec((None, qb, PAD_LANES), lambda r, n: (r, n, 0)),
                   pl.BlockSpec((None, qb, LANE), lambda r, n: (r, n, 0))],
        out_shape=[_sds((dil, l_len, PAD_LANES), BF16), _sds((dil, l_len, LANE), F32)],
        scratch_shapes=[pltpu.VMEM((qb + ATTN_W, PAD_LANES), BF16), pltpu.VMEM((qb + ATTN_W, PAD_LANES), BF16)],
        semantics=("arbitrary", "arbitrary"),
    )(qkv_g, qkv_g, qkv_g, qkv_g, qkv_g)


def _group_stats(lses):
    head_lane = lax.broadcasted_iota(jnp.int32, (1, LANE), 1)
    fulls, glse = [], []
    for g in range(3):
        real = head_lane < GROUP_HEADS[g]
        mx = jnp.max(jnp.where(real, lses[g], -jnp.inf), axis=-1, keepdims=True)
        sm = jnp.sum(jnp.where(real, jnp.exp(lses[g] - mx), 0.0), axis=-1, keepdims=True)
        fulls.append(mx + jnp.log(sm))
        glse.append(fulls[g] - math.log(GROUP_HEADS[g]))
    top = jnp.maximum(jnp.maximum(glse[0], glse[1]), glse[2])
    ex = [jnp.exp(v - top) for v in glse]
    tot = ex[0] + ex[1] + ex[2]
    alpha = [v / tot for v in ex]
    lane = lax.broadcasted_iota(jnp.int32, (1, QKV_PAD), 1)
    scale = jnp.where(lane < PAD_LANES, 3.0 * alpha[0],
                      jnp.where(lane < 2 * PAD_LANES, 3.0 * alpha[1], 3.0 * alpha[2]))
    return alpha, fulls, scale


def attn_out_fwd(x, o_parts, lse_parts, w_out_pad):
    t_len = x.shape[0]
    tm = _tile_rows(t_len)

    def body(x_ref, o1, o4, o16, l1, l4, l16, w_ref, xo_ref, mg_ref, o_ref, lse_ref, o_slab, l_slab):
        for gi, (dil, og, lg) in enumerate(zip(GROUP_DIL, (o1, o4, o16), (l1, l4, l16))):
            for j in range(3):
                _to_natural(og, j, o_slab, 3 * gi + j, dil, tm)
            _to_natural(lg, 0, l_slab, gi, dil, tm)
        o = jnp.concatenate([o_slab[j] for j in range(N_SLABS)], axis=1)
        lses = [l_slab[gi] for gi in range(3)]
        o_ref[...] = o.astype(BF16)
        for gi in range(3):
            lse_ref[:, _lane_cols(gi)] = lses[gi]
        _, _, scale = _group_stats(lses)
        merged = (o * scale).astype(BF16)
        mg_ref[...] = merged
        xo_ref[...] = x_ref[...] + _dot(merged, w_ref[...])

    row_spec = pl.BlockSpec((tm, D_MODEL), lambda t: (t, 0))
    pad_spec = pl.BlockSpec((tm, QKV_PAD), lambda t: (t, 0))
    o_specs = [pl.BlockSpec((d, tm // d, PAD_LANES), lambda t: (0, t, 0)) for d in GROUP_DIL]
    lse_specs = [pl.BlockSpec((d, tm // d, LANE), lambda t: (0, t, 0)) for d in GROUP_DIL]
    return _pcall(
        body, name="attn_out_fwd", grid=(t_len // tm,),
        in_specs=[row_spec] + o_specs + lse_specs + [pl.BlockSpec((QKV_PAD, D_MODEL), lambda t: (0, 0))],
        out_specs=[row_spec, pad_spec, pad_spec, pl.BlockSpec((tm, 3 * LANE), lambda t: (t, 0))],
        out_shape=[_sds((t_len, D_MODEL), F32), _sds((t_len, QKV_PAD), BF16),
                   _sds((t_len, QKV_PAD), BF16), _sds((t_len, 3 * LANE), F32)],
        scratch_shapes=[pltpu.VMEM((N_SLABS, tm, LANE), F32), pltpu.VMEM((3, tm, LANE), F32)],
        semantics=("arbitrary",),
    )(x, *o_parts, *lse_parts, w_out_pad)


def attn_out_bwd(dxo, w_out_pad, o, lse, duty=None):
    t_len = dxo.shape[0]
    tm = _tile_rows(t_len)

    def body(dx_ref, w_ref, o_ref, lse_ref, d1, d4, d16, c1, c4, c16, slab):
        dmerged = _dot_nt(dx_ref[...].astype(BF16), w_ref[...])
        o_t = o_ref[...].astype(F32)
        lses = [lse_ref[:, _lane_cols(gi)] for gi in range(3)]
        alpha, fulls, scale = _group_stats(lses)
        e = dmerged * o_t
        lane = lax.broadcasted_iota(jnp.int32, (1, QKV_PAD), 1)
        dalpha = [3.0 * jnp.sum(jnp.where((lane >= g * PAD_LANES) & (lane < g * PAD_LANES + GROUP_REAL[g]), e, 0.0),
                                axis=-1, keepdims=True) for g in range(3)]
        mean_da = alpha[0] * dalpha[0] + alpha[1] * dalpha[1] + alpha[2] * dalpha[2]
        dglse = [alpha[g] * (dalpha[g] - mean_da) for g in range(3)]
        do = dmerged * scale
        es = e * scale
        for j in range(N_SLABS):
            slab[j] = do[:, _lane_cols(j)]
        for gi, (dil, dg) in enumerate(zip(GROUP_DIL, (d1, d4, d16))):
            for j in range(3):
                _to_residue_major(slab, 3 * gi + j, dg, j, dil, tm)
        head_lane = lax.broadcasted_iota(jnp.int32, (1, LANE), 1)
        first = head_lane < HEAD_DIM
        for gi, (dil, cg) in enumerate(zip(GROUP_DIL, (c1, c4, c16))):
            c_g = -(dglse[gi] * jnp.exp(lses[gi] - fulls[gi]))
            for j in range(3):
                blk = es[:, _lane_cols(3 * gi + j)]
                halves = (jnp.sum(jnp.where(first, blk, 0.0), axis=-1, keepdims=True),
                          jnp.sum(jnp.where(first, 0.0, blk), axis=-1, keepdims=True))
                for half in range(2):
                    c_g = c_g + jnp.where(head_lane == 2 * j + half, halves[half], 0.0)
            slab[gi] = c_g
            _to_residue_major(slab, gi, cg, 0, dil, tm)

    row_spec = pl.BlockSpec((tm, D_MODEL), lambda t: (t, 0))
    pad_spec = pl.BlockSpec((tm, QKV_PAD), lambda t: (t, 0))
    do_specs = [pl.BlockSpec((d, tm // d, PAD_LANES), lambda t: (0, t, 0)) for d in GROUP_DIL]
    c_specs = [pl.BlockSpec((d, tm // d, LANE), lambda t: (0, t, 0)) for d in GROUP_DIL]
    outs = _pcall(
        body, name="attn_out_bwd", grid=(t_len // tm,),
        in_specs=[row_spec, pl.BlockSpec((QKV_PAD, D_MODEL), lambda t: (0, 0)), pad_spec,
                  pl.BlockSpec((tm, 3 * LANE), lambda t: (t, 0))],
        out_specs=do_specs + c_specs,
        out_shape=[_sds((d, t_len // d, PAD_LANES), BF16) for d in GROUP_DIL]
                  + [_sds((d, t_len // d, LANE), F32) for d in GROUP_DIL],
        scratch_shapes=[pltpu.VMEM((N_SLABS, tm, LANE), F32)],
        semantics=("arbitrary",), duty=duty,
    )(dxo, w_out_pad, o, lse)
    if duty is None:
        return outs[:3], outs[3:]
    return (outs[0][:3], outs[0][3:]), outs[1]


def attn_bwd(qkv_g, do_g, lse_g, c_g, gi, name, duty=None):
    dil, l_len, _ = qkv_g.shape
    qb = min(Q_BLOCK, l_len)
    nsub = qb // ATTN_W
    nsb = l_len // qb

    def body(q_ref, kc_ref, kp_ref, vc_ref, vp_ref, do_ref, lse_ref, c_ref,
             qn_ref, don_ref, lsen_ref, cn_ref, o_ref, kbuf, vbuf, dkbuf, dvbuf):
        n = pl.program_id(1)
        kbuf[pl.ds(0, ATTN_W), :] = kp_ref[...]
        kbuf[pl.ds(ATTN_W, qb), :] = kc_ref[...]
        vbuf[pl.ds(0, ATTN_W), :] = vp_ref[...]
        vbuf[pl.ds(ATTN_W, qb), :] = vc_ref[...]
        dkbuf[...] = jnp.zeros_like(dkbuf)
        dvbuf[...] = jnp.zeros_like(dvbuf)

        def block(q_of, do_of, lse_of, c_of, krows, mask, dq_rows):
            heads = []
            for j in range(3):
                cols = _lane_cols(j)
                q, do_t, k, v = q_of(cols), do_of(cols), kbuf[krows, cols], vbuf[krows, cols]
                for half, hm in enumerate(_live_halves(gi, j)):
                    qh = jnp.where(hm, q, jnp.zeros_like(q))
                    doh = jnp.where(hm, do_t, jnp.zeros_like(do_t))
                    heads.append((j, 2 * j + half, hm, qh, doh, _dot_nt(qh, k), _dot_nt(doh, v)))
            head_lane = lax.broadcasted_iota(jnp.int32, (1, LANE), 1)
            lse_t, c_t = lse_of(), c_of()
            for j in range(3):
                cols = _lane_cols(j)
                k = kbuf[krows, cols]
                dq = jnp.zeros((ATTN_W, LANE), F32)
                dk = jnp.zeros((k.shape[0], LANE), F32)
                dv = jnp.zeros((k.shape[0], LANE), F32)
                for hj, head, hm, qh, doh, s, dp in heads:
                    if hj != j:
                        continue
                    lse_h = jnp.max(jnp.where(head_lane == head, lse_t, -jnp.inf), axis=-1, keepdims=True)
                    c_h = jnp.max(jnp.where(head_lane == head, c_t, -jnp.inf), axis=-1, keepdims=True)
                    p = jnp.exp(jnp.where(mask, s, NEG_INF) - lse_h)
                    ds = (p * (dp - c_h)).astype(BF16)
                    if dq_rows is not None:
                        dq = jnp.where(hm, _dot(ds, k), dq)
                    dk = dk + _dot_tn(ds, qh)
                    dv = dv + _dot_tn(p.astype(BF16), doh)
                if dq_rows is not None:
                    o_ref[dq_rows, cols] = dq.astype(BF16)
                dkbuf[krows, cols] += dk
                dvbuf[krows, cols] += dv

        band, band_first = _band_masks()

        def sub(b, carry):
            rows = pl.ds(pl.multiple_of(b * ATTN_W, ATTN_W), ATTN_W)
            krows = pl.ds(pl.multiple_of(b * ATTN_W, ATTN_W), 2 * ATTN_W)
            block(lambda c: q_ref[rows, c], lambda c: do_ref[rows, c], lambda: lse_ref[rows, :],
                  lambda: c_ref[rows, :], krows, band_first | (band & (n + b > 0)), rows)
            return carry

        lax.fori_loop(0, nsub, sub, 0)

        qi = lax.broadcasted_iota(jnp.int32, (ATTN_W, ATTN_W), 0)
        kj = lax.broadcasted_iota(jnp.int32, (ATTN_W, ATTN_W), 1)
        nmask = (qi <= kj) & (n < nsb - 1)
        block(lambda c: qn_ref[:, c], lambda c: don_ref[:, c], lambda: lsen_ref[...],
              lambda: cn_ref[...], pl.ds(qb, ATTN_W), nmask, None)
        o_ref[:, pl.ds(PAD_LANES, PAD_LANES)] = dkbuf[pl.ds(ATTN_W, qb), :].astype(BF16)
        o_ref[:, pl.ds(2 * PAD_LANES, PAD_LANES)] = dvbuf[pl.ds(ATTN_W, qb), :].astype(BF16)

    cur = lambda c: pl.BlockSpec((None, qb, PAD_LANES), lambda r, n: (r, n, c))
    prev = lambda c: pl.BlockSpec((None, ATTN_W, PAD_LANES), lambda r, n: (r, jnp.maximum(n * nsub - 1, 0), c))
    nxt_row = lambda r, n: (r, jnp.minimum((n + 1) * nsub, nsb * nsub - 1), 0)
    nxt = pl.BlockSpec((None, ATTN_W, PAD_LANES), nxt_row)
    head_cur = pl.BlockSpec((None, qb, LANE), lambda r, n: (r, n, 0))
    head_nxt = pl.BlockSpec((None, ATTN_W, LANE), nxt_row)
    return _pcall(
        body, name=name, grid=(dil, nsb),
        in_specs=[cur(0), cur(1), prev(1), cur(2), prev(2), cur(0), head_cur, head_cur, nxt, nxt, head_nxt, head_nxt],
        out_specs=pl.BlockSpec((None, qb, QKV_PAD), lambda r, n: (r, n, 0)),
        out_shape=_sds((dil, l_len, QKV_PAD), BF16),
        scratch_shapes=[pltpu.VMEM((qb + ATTN_W, PAD_LANES), BF16), pltpu.VMEM((qb + ATTN_W, PAD_LANES), BF16),
                        pltpu.VMEM((qb + ATTN_W, PAD_LANES), F32), pltpu.VMEM((qb + ATTN_W, PAD_LANES), F32)],
        semantics=("arbitrary", "arbitrary"), duty=duty,
    )(qkv_g, qkv_g, qkv_g, qkv_g, qkv_g, do_g, lse_g, c_g, qkv_g, do_g, lse_g, c_g)


def qkv_bwd(dqkv_parts, w_pad, dxo, x, g_row, cos, sin):
    t_len = x.shape[0]
    tm = _tile_rows(t_len)

    def body(p1, p4, p16, w_ref, dxo_ref, x_ref, g_ref, cos_ref, sin_ref, dq_ref, dx_ref, dn_ref, slabs):
        @pl.when(pl.program_id(0) == 0)
        def _():
            dn_ref[...] = jnp.zeros_like(dn_ref)

        cos_t = cos_ref[...]
        sin_t = sin_ref[...]
        dh = None
        for gi, (dil, part) in enumerate(zip(GROUP_DIL, (p1, p4, p16))):
            slab = slabs.at[gi]
            for j in range(N_SLABS):
                if dil == 1:
                    a = part[0, :, _lane_cols(j)].astype(F32)
                else:
                    _to_natural(part, j, slab, j, dil, tm)
                    a = slab[j]
                if j < 6:
                    a = a * cos_t - _rot_half(a * sin_t)
                if j < 3:
                    a = a * (HEAD_DIM ** -0.5)
                dq_ref[gi, :, _lane_cols(j)] = a.astype(BF16)
            contrib = _dot_nt(dq_ref[gi], w_ref[gi])
            dh = contrib if dh is None else dh + contrib
        dx, dn = _rms_bwd(dh, x_ref[...], g_ref[...])
        dx_ref[...] = dxo_ref[...] + dx
        dn_ref[...] += dn

    row_spec = pl.BlockSpec((tm, D_MODEL), lambda t: (t, 0))
    vec_spec = pl.BlockSpec((1, D_MODEL), lambda t: (0, 0))
    tab_spec = pl.BlockSpec((tm, LANE), lambda t: (t, 0))
    part_specs = [pl.BlockSpec((d, tm // d, QKV_PAD), lambda t: (0, t, 0)) for d in GROUP_DIL]
    return _pcall(
        body, name="qkv_bwd", grid=(t_len // tm,),
        in_specs=part_specs + [pl.BlockSpec((3, D_MODEL, QKV_PAD), lambda t: (0, 0, 0)),
                               row_spec, row_spec, vec_spec, tab_spec, tab_spec],
        out_specs=[pl.BlockSpec((3, tm, QKV_PAD), lambda t: (0, t, 0)), row_spec, vec_spec],
        out_shape=[_sds((3, t_len, QKV_PAD), BF16), _sds((t_len, D_MODEL), F32), _sds((1, D_MODEL), F32)],
        scratch_shapes=[pltpu.VMEM((3, N_SLABS, tm, LANE), F32)],
        semantics=("arbitrary",),
    )(*dqkv_parts, w_pad, dxo, x, g_row, cos, sin)


def final_fwd_bwd(x, g_row, target):
    t_len = x.shape[0]
    tm = _tile_rows(t_len)

    def body(x_ref, g_ref, tgt_ref, dx_ref, dn_ref, loss_ref):
        @pl.when(pl.program_id(0) == 0)
        def _():
            dn_ref[...] = jnp.zeros_like(dn_ref)
            loss_ref[...] = jnp.zeros_like(loss_ref)

        x_t = x_ref[...]
        g = g_ref[...]
        diff = _rms_fwd(x_t, g) - tgt_ref[...]
        loss_ref[...] += 0.5 * jnp.sum(jnp.mean(diff * diff, axis=-1, keepdims=True), axis=0, keepdims=True)
        dx, dn = _rms_bwd(diff * (1.0 / D_MODEL), x_t, g)
        dx_ref[...] = dx
        dn_ref[...] += dn

    row_spec = pl.BlockSpec((tm, D_MODEL), lambda t: (t, 0))
    vec_spec = pl.BlockSpec((1, D_MODEL), lambda t: (0, 0))
    return _pcall(
        body, name="final_fwd_bwd", grid=(t_len // tm,),
        in_specs=[row_spec, vec_spec, row_spec],
        out_specs=[row_spec, vec_spec, pl.BlockSpec((1, 1), lambda t: (0, 0))],
        out_shape=[_sds((t_len, D_MODEL), F32), _sds((1, D_MODEL), F32), _sds((1, 1), F32)],
        semantics=("arbitrary",),
    )(x, g_row, target)


def pool_bwd(dxo, x, g_row, w_in, w_grp, scale, w_out, zr, duty=None):
    t_len = x.shape[0]
    tm = _tile_rows(t_len)
    nt = t_len // tm

    def body(dxo_ref, x_ref, g_ref, win_ref, wgrp_ref, scale_ref, wout_ref, zr_ref,
             dzs_ref, du_ref, dx_ref, dn_ref, dsc_ref, ebuf, tmp_a, tmp_b):
        i = pl.program_id(0)
        t = nt - 1 - i

        @pl.when(i == 0)
        def _():
            ebuf[pl.ds(tm, POOL_HALO), :] = jnp.zeros((POOL_HALO, D_MODEL), F32)
            dn_ref[...] = jnp.zeros_like(dn_ref)
            dsc_ref[...] = jnp.zeros_like(dsc_ref)

        dxo_t = dxo_ref[...]
        dz = _dot_nt(dxo_t.astype(BF16), wout_ref[...])
        dsc_ref[...] += jnp.sum(dz * zr_ref[...].astype(F32), axis=0, keepdims=True)
        dzs_ref[...] = (dz * scale_ref[...]).astype(BF16)
        row = t * tm + lax.broadcasted_iota(jnp.int32, (tm, 1), 0)
        for gi, w in enumerate(POOL_WINDOWS):
            cols = pl.ds(gi * POOL_GROUP_DIM, POOL_GROUP_DIM)
            dp_g = _dot_nt(dzs_ref[:, cols], wgrp_ref[gi])
            inv_cnt = 1.0 / jnp.minimum(row + 1, w).astype(F32)
            ebuf[pl.ds(0, tm), cols] = dp_g * inv_cnt
            acc = _window_sums(ebuf, cols, (tmp_a, tmp_b), gi + 1, tm, back=False) - dp_g
            du_ref[:, cols] = acc.astype(BF16)
        ebuf[pl.ds(tm, POOL_HALO), :] = ebuf[pl.ds(0, POOL_HALO), :]
        dh = _dot_nt(du_ref[...], win_ref[...])
        dx, dn = _rms_bwd(dh, x_ref[...], g_ref[...])
        dx_ref[...] = dxo_t + dx
        dn_ref[...] += dn

    row_spec = pl.BlockSpec((tm, D_MODEL), lambda i: (nt - 1 - i, 0))
    full = lambda shape: pl.BlockSpec(shape, lambda i: (0,) * len(shape))
    vec = full((1, D_MODEL))
    return _pcall(
        body, name="pool_bwd", grid=(nt,),
        in_specs=[row_spec, row_spec, vec, full((D_MODEL, D_MODEL)), full((4, POOL_GROUP_DIM, POOL_GROUP_DIM)),
                  vec, full((D_MODEL, D_MODEL)), row_spec],
        out_specs=[row_spec, row_spec, row_spec, vec, vec],
        out_shape=[_sds((t_len, D_MODEL), BF16), _sds((t_len, D_MODEL), BF16), _sds((t_len, D_MODEL), F32),
                   _sds((1, D_MODEL), F32), _sds((1, D_MODEL), F32)],
        scratch_shapes=[pltpu.VMEM((tm + POOL_HALO, D_MODEL), F32)]
                       + [pltpu.VMEM((tm + POOL_HALO, POOL_GROUP_DIM), F32)] * 2,
        semantics=("arbitrary",), duty=duty,
    )(dxo, x, g_row, w_in, w_grp, scale, w_out, zr)


def _mesh_pos():
    return lax.axis_index("x"), lax.axis_index("y"), lax.axis_index("c")


def _other_chips(x, y):
    return [(1 - x, y), (x, 1 - y), (1 - x, 1 - y)]


def _remote(src, dst, send_sem, recv_sem, device):
    return pltpu.make_async_remote_copy(src_ref=src, dst_ref=dst, send_sem=send_sem, recv_sem=recv_sem,
                                        device_id=device, device_id_type=MESH)


class _Duty:
    aliases = {}
    mid_at = 2 / 3

    def mid(self, ins, outs, sems):
        pass


class Together(_Duty):
    def __init__(self, duties):
        self.duties = duties
        self.mid_at = min(d.mid_at for d in duties)
        self.ins = [a for d in duties for a in d.ins]
        self.out_shape = [s for d in duties for s in d.out_shape]
        self.scratch = [s for d in duties for s in d.scratch]
        self.aliases = {}
        i0 = o0 = 0
        for d in duties:
            self.aliases.update({i0 + i: o0 + o for i, o in d.aliases.items()})
            i0 += len(d.ins)
            o0 += len(d.out_shape)

    def _each(self, ins, outs, sems):
        i0 = o0 = s0 = 0
        for d in self.duties:
            ni, no, ns = len(d.ins), len(d.out_shape), len(d.scratch)
            yield d, ins[i0:i0 + ni], outs[o0:o0 + no], sems[s0:s0 + ns]
            i0, o0, s0 = i0 + ni, o0 + no, s0 + ns

    def split(self, outs):
        return [list(o) for _, _, o, _ in self._each(self.ins, outs, self.scratch)]

    def start(self, ins, outs, sems):
        for d, i, o, s in self._each(ins, outs, sems):
            d.start(i, o, s)

    def mid(self, ins, outs, sems):
        for d, i, o, s in self._each(ins, outs, sems):
            d.mid(i, o, s)

    def finish(self, ins, outs, sems):
        for d, i, o, s in self._each(ins, outs, sems):
            d.finish(i, o, s)


def run_duty(duty, name):
    d_in, d_out = len(duty.ins), len(duty.out_shape)

    def body(*refs):
        ins, outs, sems = refs[:d_in], refs[d_in:d_in + d_out], refs[d_in + d_out:]
        duty.start(ins, outs, sems)
        duty.mid(ins, outs, sems)
        duty.finish(ins, outs, sems)

    return pl.pallas_call(
        body, name=name, out_shape=list(duty.out_shape), in_specs=[_ANY] * d_in, out_specs=[_ANY] * d_out,
        scratch_shapes=list(duty.scratch), input_output_aliases=dict(duty.aliases),
        compiler_params=pltpu.CompilerParams(has_side_effects=True),
    )(*duty.ins)


class GatherWeights(_Duty):
    N_COPIES = 7

    def __init__(self, shards, mid_at=_Duty.mid_at):
        n = self.n = len(shards)
        self.mid_at = mid_at
        self.halves = [s.shape[0] // 2 for s in shards]
        my_slot = 2 * lax.axis_index("x") + lax.axis_index("y")
        staged = [lax.dynamic_update_slice(lax.empty((N_SHARDS,) + s.shape, s.dtype), s[None], (my_slot, 0, 0))
                  for s in shards]
        self.ins = list(shards) + staged
        self.out_shape = [_sds((N_SHARDS,) + s.shape, s.dtype) for s in shards]
        self.aliases = {n + a: a for a in range(n)}
        self.scratch = [pltpu.SemaphoreType.DMA((n, self.N_COPIES)), pltpu.SemaphoreType.DMA((n, self.N_COPIES))]

    def _copies(self, ins, outs, sems, a):
        x, y, c = _mesh_pos()
        h = self.halves[a]
        q = h // 2
        sibling, to_x, to_y = (x, y, 1 - c), (1 - x, y, c), (x, 1 - y, c)
        me, of_x, of_y, of_d = 2 * x + y, 2 * (1 - x) + y, 2 * x + 1 - y, 2 * (1 - x) + 1 - y
        half = lambda core: pl.ds(core * h, h)
        quarter = lambda k: pl.ds(c * h + k * q, q)
        out = outs[a]
        mine = ins[a].at[half(c)]

        def copy(k, src, dst, device):
            return _remote(src, dst, sems[0].at[a, k], sems[1].at[a, k], device)

        same = lambda k, ref, device: copy(k, ref, ref, device)
        sent = [copy(0, mine, out.at[me, half(c)], to_x),
                copy(1, mine, out.at[me, half(c)], to_y),
                same(2, out.at[of_x, quarter(0)], to_y),
                same(3, out.at[of_y, quarter(1)], to_x),
                same(4, out.at[of_x, half(c)], sibling),
                same(5, out.at[of_y, half(c)], sibling),
                same(6, out.at[of_d, half(c)], sibling)]
        landing = [out.at[of_x, half(c)], out.at[of_y, half(c)], out.at[of_d, quarter(0)], out.at[of_d, quarter(1)],
                   out.at[of_x, half(1 - c)], out.at[of_y, half(1 - c)], out.at[of_d, half(1 - c)]]
        return sent, [same(k, ref, sibling) for k, ref in enumerate(landing)]

    def start(self, ins, outs, sems):
        for a in range(self.n):
            sent, _ = self._copies(ins, outs, sems, a)
            sent[0].start()
            sent[1].start()

    def mid(self, ins, outs, sems):
        for a in range(self.n):
            sent, lands = self._copies(ins, outs, sems, a)
            lands[0].wait_recv()
            sent[2].start()
            sent[4].start()
            lands[1].wait_recv()
            sent[3].start()
            sent[5].start()

    def finish(self, ins, outs, sems):
        for a in range(self.n):
            sent, lands = self._copies(ins, outs, sems, a)
            lands[2].wait_recv()
            lands[3].wait_recv()
            sent[6].start()
        for a in range(self.n):
            sent, lands = self._copies(ins, outs, sems, a)
            for cp in lands[4:]:
                cp.wait_recv()
            for cp in sent:
                cp.wait_send()


class GradReducer:
    def __init__(self, c_idx, pos_idx):
        self.c_idx, self.pos_idx = c_idx, pos_idx
        self.in_flight = []
        self.done = {}

    def push(self, name, grad):
        self.in_flight.append(dict(name=name, stage="halves", data=grad))

    def _duties(self):
        make = {"halves": SiblingHalves, "exchange": ChipExchange, "share": SiblingShare}
        return Together([make[w["stage"]]([w["data"]]) for w in self.in_flight])

    def _advance(self, duties, outs):
        still = []
        for w, (res,) in zip(self.in_flight, duties.split(outs)):
            if w["stage"] == "halves":
                partial = add_my_half(w["data"], res, self.c_idx, f"rs_add_{w['name']}")
                still.append(dict(name=w["name"], stage="exchange", data=partial))
            elif w["stage"] == "exchange":
                reduced = sum_slots(res, w["data"], self.pos_idx, f"rs_sum_{w['name']}")
                still.append(dict(name=w["name"], stage="share", data=reduced))
            else:
                self.done[w["name"]] = res
        self.in_flight = still

    def carried_by(self, fn, *args, **kw):
        if not self.in_flight:
            return fn(*args, **kw)
        duties = self._duties()
        out, duty_outs = fn(*args, duty=duties, **kw)
        self._advance(duties, duty_outs)
        return out

    def drain(self, name):
        step = 0
        while self.in_flight:
            duties = self._duties()
            self._advance(duties, run_duty(duties, f"{name}{step}"))
            step += 1


class SiblingHalves(_Duty):
    def __init__(self, grads):
        n = len(grads)
        self.halves = [g.shape[1] // 2 for g in grads]
        self.ins = list(grads)
        self.out_shape = [_sds((N_SHARDS, h, g.shape[2]), g.dtype) for g, h in zip(grads, self.halves)]
        self.scratch = [pltpu.SemaphoreType.DMA((n,)), pltpu.SemaphoreType.DMA((n,))]

    def _copies(self, ins, outs, sems):
        x, y, c = _mesh_pos()
        return [_remote(ins[a].at[:, pl.ds((1 - c) * h, h)], outs[a], sems[0].at[a], sems[1].at[a], (x, y, 1 - c))
                for a, h in enumerate(self.halves)]

    def start(self, ins, outs, sems):
        for cp in self._copies(ins, outs, sems):
            cp.start()

    def finish(self, ins, outs, sems):
        for cp in self._copies(ins, outs, sems):
            cp.wait()


class ChipExchange(_Duty):
    def __init__(self, parts):
        n = self.n = len(parts)
        self.ins = list(parts)
        self.out_shape = [_sds(p.shape, p.dtype) for p in parts]
        self.scratch = [pltpu.SemaphoreType.DMA((n, 3)), pltpu.SemaphoreType.DMA((n, 3))]

    def _copies(self, ins, outs, sems, arriving):
        x, y, c = _mesh_pos()
        cps = []
        for a in range(self.n):
            for j, chip in enumerate(_other_chips(x, y)):
                theirs = 2 * chip[0] + chip[1]
                src = outs[a].at[theirs] if arriving else ins[a].at[theirs]
                dst = outs[a].at[theirs] if arriving else outs[a].at[2 * x + y]
                cps.append(_remote(src, dst, sems[0].at[a, j], sems[1].at[a, j], (*chip, c)))
        return cps

    def start(self, ins, outs, sems):
        for cp in self._copies(ins, outs, sems, False):
            cp.start()

    def finish(self, ins, outs, sems):
        for cp in self._copies(ins, outs, sems, True):
            cp.wait_recv()
        for cp in self._copies(ins, outs, sems, False):
            cp.wait_send()


class SiblingShare(_Duty):
    def __init__(self, reduced):
        n = self.n = len(reduced)
        self.ins = list(reduced)
        self.out_shape = [_sds(r.shape, r.dtype) for r in reduced]
        self.aliases = {a: a for a in range(n)}
        self.scratch = [pltpu.SemaphoreType.DMA((n,)), pltpu.SemaphoreType.DMA((n,))]

    def _copies(self, outs, sems, half_of):
        x, y, c = _mesh_pos()
        cps = []
        for a in range(self.n):
            h = outs[a].shape[0] // 2
            rows = outs[a].at[pl.ds(half_of(c) * h, h)]
            cps.append(_remote(rows, rows, sems[0].at[a], sems[1].at[a], (x, y, 1 - c)))
        return cps

    def start(self, ins, outs, sems):
        for cp in self._copies(outs, sems, lambda core: core):
            cp.start()

    def finish(self, ins, outs, sems):
        for cp in self._copies(outs, sems, lambda core: 1 - core):
            cp.wait_recv()
        for cp in self._copies(outs, sems, lambda core: core):
            cp.wait_send()


def allreduce_small(v):
    def body(v_ref, o_ref, buf, send_sems, recv_sems):
        x, y, c = _mesh_pos()
        me = 4 * x + 2 * y + c
        buf[me] = v_ref[...]
        flip = lambda p, f: 1 - p if f else p
        peers = [(flip(x, k & 4), flip(y, k & 2), flip(c, k & 1)) for k in range(1, N_DEV)]
        cps = []
        for k, peer in enumerate(peers):
            cp = _remote(v_ref, buf.at[me], send_sems.at[k], recv_sems.at[k], peer)
            cp.start()
            cps.append(cp)
        for k, peer in enumerate(peers):
            slot = buf.at[4 * peer[0] + 2 * peer[1] + peer[2]]
            _remote(slot, slot, send_sems.at[k], recv_sems.at[k], peer).wait_recv()
        for cp in cps:
            cp.wait_send()
        acc = buf[0]
        for i in range(1, N_DEV):
            acc = acc + buf[i]
        o_ref[...] = acc

    vm = pl.BlockSpec(memory_space=pltpu.VMEM)
    return pl.pallas_call(
        body, name="allreduce_small", out_shape=_sds(v.shape, v.dtype), in_specs=[vm], out_specs=vm,
        scratch_shapes=[pltpu.VMEM((N_DEV,) + v.shape, v.dtype),
                        pltpu.SemaphoreType.DMA((N_DEV - 1,)), pltpu.SemaphoreType.DMA((N_DEV - 1,))],
        compiler_params=pltpu.CompilerParams(has_side_effects=True),
    )(v)


def add_my_half(grad, theirs, c_idx, name):
    _, r, cols = grad.shape
    h = r // 2

    def body(c_ref, g_ref, t_ref, o_ref):
        o_ref[...] = (g_ref[...] + t_ref[...]).astype(BF16)

    slot = pl.BlockSpec((None, h, cols), lambda s, c: (s, 0, 0))
    grid_spec = pltpu.PrefetchScalarGridSpec(
        num_scalar_prefetch=1, grid=(N_SHARDS,),
        in_specs=[pl.BlockSpec((None, h, cols), lambda s, c: (s, c[0], 0)), slot], out_specs=slot)
    return pl.pallas_call(
        body, name=name, grid_spec=grid_spec, out_shape=_sds((N_SHARDS, h, cols), BF16),
        compiler_params=pltpu.CompilerParams(dimension_semantics=("arbitrary",), vmem_limit_bytes=VMEM_LIMIT_BYTES),
    )(c_idx, grad, theirs)


def sum_slots(received, mine, pos_idx, name):
    _, h, cols = received.shape

    def body(pos_ref, r_ref, m_ref, o_ref):
        acc = None
        for k in range(N_SHARDS):
            term = jnp.where(pos_ref[0] == k, m_ref[k], r_ref[k]).astype(F32)
            acc = term if acc is None else acc + term
        o_ref[...] = acc

    whole = pl.BlockSpec((N_SHARDS, h, cols), lambda i, pos: (0, 0, 0))
    grid_spec = pltpu.PrefetchScalarGridSpec(
        num_scalar_prefetch=1, grid=(1,), in_specs=[whole, whole],
        out_specs=pl.BlockSpec((h, cols), lambda i, pos: (pos[1], 0)))
    return pl.pallas_call(
        body, name=name, grid_spec=grid_spec, out_shape=_sds((2 * h, cols), F32),
        compiler_params=pltpu.CompilerParams(dimension_semantics=("arbitrary",), vmem_limit_bytes=VMEM_LIMIT_BYTES),
    )(pos_idx, received, mine)


def adamw(name, grads, w, m, v):
    n_layers, r, cols = w.shape
    tr = r // 2 if r % 16 == 0 else r
    bias1 = 1.0 - ADAM_B1 ** ADAM_STEP
    bias2 = 1.0 - ADAM_B2 ** ADAM_STEP

    def body(*refs):
        g_refs = refs[:n_layers]
        w_ref, m_ref, v_ref, go_ref, d_ref, mo_ref, vo_ref = refs[n_layers:]
        g = g_refs[0][...]
        for layer in range(1, n_layers):
            g = jnp.where(pl.program_id(0) == layer, g_refs[layer][...], g)
        m_new = ADAM_B1 * m_ref[...] + (1.0 - ADAM_B1) * g
        v_new = ADAM_B2 * v_ref[...] + (1.0 - ADAM_B2) * (g * g)
        m_hat = m_new / bias1
        v_hat = v_new / bias2
        go_ref[...] = g
        d_ref[...] = -ADAM_LR * (m_hat / (jnp.sqrt(v_hat) + ADAM_EPS) + ADAM_WD * w_ref[...])
        mo_ref[...] = m_new
        vo_ref[...] = v_new

    g_spec = pl.BlockSpec((tr, cols), lambda l, i: (i, 0))
    lay_spec = pl.BlockSpec((None, tr, cols), lambda l, i: (l, i, 0))
    shape = _sds((n_layers, r, cols), F32)
    return _pcall(
        body, name=name, grid=(n_layers, r // tr),
        in_specs=[g_spec] * n_layers + [lay_spec] * 3, out_specs=[lay_spec] * 4,
        out_shape=[shape] * 4, semantics=("arbitrary", "arbitrary"),
    )(*grads, w, m, v)


def kernel(x, norm_mix, norm_ffn, norm_final, pool_w_in, pool_w_group, pool_scale, pool_w_out, attn_w_qkv, attn_w_out, ffn_w_gate, ffn_w_up, ffn_w_down, loss_target, m_norm_mix, m_norm_ffn, m_norm_final, m_pool_w_in, m_pool_w_group, m_pool_scale, m_pool_w_out, m_attn_w_qkv, m_attn_w_out, m_ffn_w_gate, m_ffn_w_up, m_ffn_w_down, v_norm_mix, v_norm_ffn, v_norm_final, v_pool_w_in, v_pool_w_group, v_pool_scale, v_pool_w_out, v_attn_w_qkv, v_attn_w_out, v_ffn_w_gate, v_ffn_w_up, v_ffn_w_down):
    t_len = x.shape[1]
    x0 = x.reshape(t_len, D_MODEL)
    target = loss_target.reshape(t_len, D_MODEL)
    row = lambda a: a.reshape(1, D_MODEL)

    grp_rows = POOL_GROUP_DIM // N_SHARDS
    bf = lambda a: a.astype(BF16)
    gate_t, up_t = jnp.swapaxes(ffn_w_gate, 1, 2), jnp.swapaxes(ffn_w_up, 1, 2)
    pool_shards = [bf(pool_w_in[0]), bf(pool_w_group[0].reshape(4 * grp_rows, POOL_GROUP_DIM)), bf(pool_w_out[0])]
    ffn0_shards = [bf(gate_t[0]), bf(up_t[0]), bf(ffn_w_down[0])]
    late_shards = [bf(attn_w_qkv[0]), bf(attn_w_out[0]), bf(gate_t[1]), bf(up_t[1]), bf(ffn_w_down[1])]
    cos, sin = rope_tables(t_len)
    c_idx = lax.axis_index("c").astype(jnp.int32).reshape(1)
    pos_idx = jnp.stack([2 * lax.axis_index("x") + lax.axis_index("y"), lax.axis_index("c")]).astype(jnp.int32)
    chip_rows = lambda g: g.reshape(N_SHARDS, D_MODEL // N_SHARDS, D_MODEL)

    g_pool = run_duty(GatherWeights(pool_shards), "gather_pool")
    w_in = g_pool[0].reshape(D_MODEL, D_MODEL)
    w_grp = g_pool[1].reshape(N_SHARDS, 4, grp_rows, POOL_GROUP_DIM).transpose(1, 0, 2, 3).reshape(
        4, POOL_GROUP_DIM, POOL_GROUP_DIM)
    w_out = g_pool[2].reshape(D_MODEL, D_MODEL)
    (h0, p, zr, z, x1), ffn0 = pool_fwd(x0, row(norm_mix[0]), w_in, w_grp, pool_scale, w_out,
                                        duty=GatherWeights(ffn0_shards))
    (h1, gate0, up0, act0, x2), late = ffn_fwd(x1, row(norm_ffn[0]), *ffn0, "ffn_fwd0",
                                               duty=GatherWeights(late_shards, mid_at=0.45))
    w_qkv = pad_qkv_weight(late[0])
    w_ao = jnp.concatenate(pad_groups(late[1].reshape(D_MODEL, D_MODEL), 0), axis=0)
    ffn1 = late[2:5]
    h2, *qkv_parts = qkv_fwd(x2, row(norm_mix[1]), w_qkv, cos, sin)
    o_parts, lse_parts = [], []
    for gi in range(3):
        o_g, lse_g = attn_fwd(qkv_parts[gi], gi, f"attn_fwd_g{gi}")
        o_parts.append(o_g)
        lse_parts.append(lse_g)
    x3, merged, o_nat, lse_nat = attn_out_fwd(x2, o_parts, lse_parts, w_ao)
    h3, gate1, up1, act1, x4 = ffn_fwd(x3, row(norm_ffn[1]), *ffn1, "ffn_fwd1")
    dx4, d_norm_final, loss_local = final_fwd_bwd(x4, row(norm_final), target)

    red = GradReducer(c_idx, pos_idx)
    dgate1, dup1, dx3, d_nf1 = ffn_bwd(dx4, x3, row(norm_ffn[1]), gate1, up1, *ffn1, "ffn_bwd1")
    g_gate1 = wgrad_row_sharded("wgrad_gate1", dgate1, h3)
    g_up1 = wgrad_row_sharded("wgrad_up1", dup1, h3)
    g_down1 = wgrad_row_sharded("wgrad_down1", act1, dx4)
    red.push("gate1", g_gate1)
    red.push("up1", g_up1)
    do_parts, c_parts = red.carried_by(attn_out_bwd, dx3, w_ao, o_nat, lse_nat)
    g_ao = wgrad_full("wgrad_attn_out", merged, dx3)
    red.push("down1", g_down1)
    red.push("attn_out", chip_rows(unpad_groups(jnp.split(g_ao, 3, axis=0), 0)))
    dqkv_parts = [red.carried_by(attn_bwd, qkv_parts[gi], do_parts[gi], lse_parts[gi], c_parts[gi], gi,
                                 f"attn_bwd_g{gi}") for gi in range(3)]
    dqkv, dx2, d_nm1 = qkv_bwd(dqkv_parts, w_qkv, dx3, x2, row(norm_mix[1]), cos, sin)
    red.push("qkv", unpad_qkv_grad(wgrad_col_sharded("wgrad_qkv", h2, dqkv)))

    dgate0, dup0, dx1, d_nf0 = red.carried_by(ffn_bwd, dx2, x1, row(norm_ffn[0]), gate0, up0, *ffn0, "ffn_bwd0")
    red.push("gate0", red.carried_by(wgrad_row_sharded, "wgrad_gate0", dgate0, h1))
    red.push("up0", red.carried_by(wgrad_row_sharded, "wgrad_up0", dup0, h1))
    red.push("down0", red.carried_by(wgrad_row_sharded, "wgrad_down0", act0, dx2))
    red.push("pool_out", chip_rows(red.carried_by(wgrad_full, "wgrad_pool_out", z, dx1)))
    dzs, du, dx0, d_nm0, d_scale = pool_bwd(dx1, x0, row(norm_mix[0]), w_in, w_grp, pool_scale, w_out, zr)
    g_grp = red.carried_by(wgrad_pool_group, "wgrad_pool_group", p, dzs)
    red.push("pool_group", g_grp.reshape(N_SHARDS, 4 * grp_rows, POOL_GROUP_DIM))
    red.push("pool_in", chip_rows(red.carried_by(wgrad_full, "wgrad_pool_in", h0, du)))
    red.drain("rs_tail")
    full = [red.done[nm] for nm in ("pool_in", "pool_group", "pool_out", "qkv", "attn_out",
                                    "gate0", "gate1", "up0", "up1", "down0", "down1")]

    zero_row = jnp.zeros((1, D_MODEL), F32)
    small = jnp.concatenate([d_nm0, d_nm1, d_nf0, d_nf1, d_norm_final, d_scale,
                             jnp.broadcast_to(loss_local, (1, D_MODEL)), zero_row], axis=0)
    small = allreduce_small(small)
    loss = small[6, 0]

    pack = lambda a, b, c, d: jnp.concatenate([a, b, row(c), d, zero_row, zero_row], axis=0)[None]
    sg, sd, sm, sv = adamw("adamw_small", [small],
                           pack(norm_mix, norm_ffn, norm_final, pool_scale),
                           pack(m_norm_mix, m_norm_ffn, m_norm_final, m_pool_scale),
                           pack(v_norm_mix, v_norm_ffn, v_norm_final, v_pool_scale))
    unpack = lambda a: (a[0, 0:2], a[0, 2:4], a[0, 4], a[0, 5:6])

    def update(name, grads, w, m, v, transposed=False):
        if transposed:
            w, m, v = (jnp.swapaxes(a, 1, 2) for a in (w, m, v))
        n_layers = len(grads)
        shp = (n_layers,) + grads[0].shape
        outs = [o.reshape(w.shape) for o in adamw(name, grads, w.reshape(shp), m.reshape(shp), v.reshape(shp))]
        return [jnp.swapaxes(o, 1, 2) for o in outs] if transposed else outs

    big = [
        update("adamw_pool_in", [full[0]], pool_w_in, m_pool_w_in, v_pool_w_in),
        update("adamw_pool_group", [full[1]], pool_w_group, m_pool_w_group, v_pool_w_group),
        update("adamw_pool_out", [full[2]], pool_w_out, m_pool_w_out, v_pool_w_out),
        update("adamw_qkv", [full[3]], attn_w_qkv, m_attn_w_qkv, v_attn_w_qkv),
        update("adamw_attn_out", [full[4]], attn_w_out, m_attn_w_out, v_attn_w_out),
        update("adamw_gate", [full[5], full[6]], ffn_w_gate, m_ffn_w_gate, v_ffn_w_gate, transposed=True),
        update("adamw_up", [full[7], full[8]], ffn_w_up, m_ffn_w_up, v_ffn_w_up, transposed=True),
        update("adamw_down", [full[9], full[10]], ffn_w_down, m_ffn_w_down, v_ffn_w_down),
    ]

    def leaves(k, small_vals):
        nm, nf, nfin, psc = unpack(small_vals)
        return [nm, nf, nfin, big[0][k], big[1][k], psc, big[2][k], big[3][k], big[4][k],
                big[5][k], big[6][k], big[7][k]]

    grad_x = dx0.reshape(x.shape)
    return (loss, grad_x, *leaves(0, sg), *leaves(1, sd), *leaves(2, sm), *leaves(3, sv))
```

```python
import math

import jax
import jax.numpy as jnp
from jax import lax
from jax.experimental import pallas as pl
from jax.experimental.pallas import tpu as pltpu

F32 = jnp.float32
BF16 = jnp.bfloat16

D_MODEL = 1024
N_SHARDS = 4
N_DEV = 8
D_FF = 2816
FF_SHARD = D_FF // N_SHARDS
HEAD_DIM = 64
QKV_SHARD = 3 * D_MODEL // N_SHARDS
POOL_WINDOWS = (2, 4, 8, 16)
POOL_GROUP_DIM = 256
POOL_HALO = 32
ATTN_W = 128
GROUP_LANES = (0, 384, 704, 1024)
GROUP_HEADS = (6, 5, 5)
GROUP_DIL = (1, 4, 16)
ROPE_THETA = 10000.0
EPS = 1e-6
NEG_INF = -1e30
LANE = 128
VMEM_LIMIT_BYTES = 60 * 1024 * 1024

ADAM_LR = 0.001
ADAM_B1 = 0.9
ADAM_B2 = 0.999
ADAM_EPS = 1e-08
ADAM_WD = 0.01
ADAM_STEP = 10

NT_DIMS = (((1,), (1,)), ((), ()))
TN_DIMS = (((0,), (0,)), ((), ()))
MESH = pl.DeviceIdType.MESH


_ANY = pl.BlockSpec(memory_space=pl.ANY)


def _pcall(body, *, name, out_shape, grid=None, in_specs=None, out_specs=None, scratch_shapes=(),
           semantics=None, duty=None):
    kw = {}
    if in_specs is not None and duty is None:
        kw["in_specs"] = in_specs
    if out_specs is not None and duty is None:
        kw["out_specs"] = out_specs
    if grid is not None:
        kw["grid"] = grid
    params = dict(dimension_semantics=semantics, vmem_limit_bytes=VMEM_LIMIT_BYTES)
    if duty is None:
        return pl.pallas_call(body, name=name, out_shape=out_shape, scratch_shapes=list(scratch_shapes),
                              compiler_params=pltpu.CompilerParams(**params), **kw)

    single = not isinstance(out_shape, (list, tuple))
    c_out_shape = [out_shape] if single else list(out_shape)
    c_out_specs = [out_specs] if single else list(out_specs)
    n_in, n_out, n_scr = len(in_specs), len(c_out_shape), len(scratch_shapes)
    d_in, d_out = len(duty.ins), len(duty.out_shape)
    total = math.prod(grid)
    mid_step = (2 * total) // 3

    def wrapped(*refs):
        c_in, d_ins = refs[:n_in], refs[n_in:n_in + d_in]
        o0 = n_in + d_in
        c_outs, d_outs = refs[o0:o0 + n_out], refs[o0 + n_out:o0 + n_out + d_out]
        s0 = o0 + n_out + d_out
        c_scr, d_sems = refs[s0:s0 + n_scr], refs[s0 + n_scr:]
        step = pl.program_id(0)
        for ax in range(1, len(grid)):
            step = step * grid[ax] + pl.program_id(ax)

        @pl.when(step == 0)
        def _():
            duty.start(d_ins, d_outs, d_sems)

        body(*c_in, *c_outs, *c_scr)

        @pl.when(step == mid_step)
        def _():
            duty.mid(d_ins, d_outs, d_sems)

        @pl.when(step == total - 1)
        def _():
            duty.finish(d_ins, d_outs, d_sems)

    call = pl.pallas_call(
        wrapped, name=name, grid=grid,
        in_specs=list(in_specs) + [_ANY] * d_in, out_specs=c_out_specs + [_ANY] * d_out,
        out_shape=c_out_shape + list(duty.out_shape),
        scratch_shapes=list(scratch_shapes) + list(duty.scratch),
        input_output_aliases={n_in + i: n_out + o for i, o in duty.aliases.items()},
        compiler_params=pltpu.CompilerParams(has_side_effects=True, **params))

    def run(*args):
        outs = call(*args, *duty.ins)
        c = outs[:n_out]
        return (c[0] if single else list(c)), list(outs[n_out:])

    return run


def _sds(shape, dtype):
    return jax.ShapeDtypeStruct(tuple(shape), dtype)


def _dot(a, b):
    return jnp.dot(a, b, preferred_element_type=F32)


def _dot_nt(a, b):
    return lax.dot_general(a, b, NT_DIMS, preferred_element_type=F32)


def _dot_tn(a, b):
    return lax.dot_general(a, b, TN_DIMS, preferred_element_type=F32)


def _rms_fwd(x, g):
    r = lax.rsqrt(jnp.mean(x * x, axis=-1, keepdims=True) + EPS)
    return x * r * g


def _rms_bwd(dh, x, g):
    r = lax.rsqrt(jnp.mean(x * x, axis=-1, keepdims=True) + EPS)
    xh = x * r
    dg = jnp.sum(dh * xh, axis=0, keepdims=True)
    dxh = dh * g
    dx = r * (dxh - xh * jnp.mean(dxh * xh, axis=-1, keepdims=True))
    return dx, dg


def _sigmoid(x):
    return 0.5 * jnp.tanh(0.5 * x) + 0.5


def _tile_rows(t):
    return min(512, t)


def _sub_tiles(tm, n_sub=2):
    rows = tm // n_sub
    return [pl.ds(i * rows, rows) for i in range(n_sub)]


def _wgrad_rows(t):
    return min(2048, t)


def _window_sums(buf, cols, tmps, levels, tm, back):
    src, src_cols = buf, cols
    for k in range(1, levels + 1):
        shift = 1 << (k - 1)
        last = k == levels
        if back:
            lo = POOL_HALO if last else 8 * k
            n = tm + POOL_HALO - lo
            val = src[pl.ds(lo, n), src_cols] + src[pl.ds(lo - shift, n), src_cols]
        else:
            lo = 0
            n = tm if last else tm + POOL_HALO - 8 * k
            val = src[pl.ds(0, n), src_cols] + src[pl.ds(shift, n), src_cols]
        if last:
            return val
        tmps[k % 2][pl.ds(lo, n), :] = val
        src, src_cols = tmps[k % 2], slice(None)
def rms_norm(x, g_row, name, duty=None):
    t_len = x.shape[0]
    tm = _tile_rows(t_len)

    def body(x_ref, g_ref, h_ref):
        h_ref[...] = _rms_fwd(x_ref[...], g_ref[...]).astype(BF16)

    row_spec = pl.BlockSpec((tm, D_MODEL), lambda t: (t, 0))
    return _pcall(body, name=name, grid=(t_len // tm,),
                  in_specs=[row_spec, pl.BlockSpec((1, D_MODEL), lambda t: (0, 0))], out_specs=row_spec,
                  out_shape=_sds((t_len, D_MODEL), BF16), semantics=("arbitrary",), duty=duty)(x, g_row)


def pool_fwd(x, h, w_in, w_grp, scale, w_out, duty=None):
    t_len = x.shape[0]
    tm = _tile_rows(t_len)

    def body(x_ref, hin_ref, win_ref, wgrp_ref, scale_ref, wout_ref,
             p_ref, zr_ref, z_ref, xo_ref, ubuf, tmp_a, tmp_b):
        t = pl.program_id(0)

        @pl.when(t == 0)
        def _():
            ubuf[pl.ds(0, POOL_HALO), :] = jnp.zeros((POOL_HALO, D_MODEL), F32)

        x_t = x_ref[...]
        ubuf[pl.ds(POOL_HALO, tm), :] = _dot(hin_ref[...], win_ref[...])
        row = t * tm + lax.broadcasted_iota(jnp.int32, (tm, 1), 0)
        for gi, w in enumerate(POOL_WINDOWS):
            cols = pl.ds(gi * POOL_GROUP_DIM, POOL_GROUP_DIM)
            u_g = ubuf[pl.ds(POOL_HALO, tm), cols]
            acc = _window_sums(ubuf, cols, (tmp_a, tmp_b), gi + 1, tm, back=True)
            inv_cnt = 1.0 / jnp.minimum(row + 1, w).astype(F32)
            p_g = (acc * inv_cnt - u_g).astype(BF16)
            p_ref[:, cols] = p_g
            z_g = _dot(p_g, wgrp_ref[gi])
            zr_ref[:, cols] = z_g.astype(BF16)
            z_ref[:, cols] = (z_g * scale_ref[:, cols]).astype(BF16)
        ubuf[pl.ds(0, POOL_HALO), :] = ubuf[pl.ds(tm, POOL_HALO), :]
        xo_ref[...] = x_t + _dot(z_ref[...], wout_ref[...])

    row_spec = pl.BlockSpec((tm, D_MODEL), lambda t: (t, 0))
    full2 = lambda shape: pl.BlockSpec(shape, lambda t: (0,) * len(shape))
    return _pcall(
        body, name="pool_fwd", grid=(t_len // tm,),
        in_specs=[row_spec, row_spec, full2((D_MODEL, D_MODEL)),
                  full2((4, POOL_GROUP_DIM, POOL_GROUP_DIM)), full2((1, D_MODEL)), full2((D_MODEL, D_MODEL))],
        out_specs=[row_spec] * 4,
        out_shape=[_sds((t_len, D_MODEL), BF16)] * 3 + [_sds((t_len, D_MODEL), F32)],
        scratch_shapes=[pltpu.VMEM((tm + POOL_HALO, D_MODEL), F32)]
                       + [pltpu.VMEM((tm + POOL_HALO, POOL_GROUP_DIM), F32)] * 2,
        semantics=("arbitrary",), duty=duty,
    )(x, h, w_in, w_grp, scale, w_out)


def ffn_fwd(x, g_row, w_gate_t, w_up_t, w_down, name, duty=None):
    t_len = x.shape[0]
    tm = min(1024, t_len)

    def body(x_ref, g_ref, wg_ref, wu_ref, wd_ref, h_ref, go_ref, uo_ref, ao_ref, xo_ref, hbuf, acc):
        s = pl.program_id(1)

        @pl.when(s == 0)
        def _():
            h = _rms_fwd(x_ref[...], g_ref[...]).astype(BF16)
            hbuf[...] = h
            h_ref[...] = h
            acc[...] = jnp.zeros_like(acc)

        h = hbuf[...]
        gate = _dot_nt(h, wg_ref[...])
        up = _dot_nt(h, wu_ref[...])
        go_ref[...] = gate.astype(BF16)
        uo_ref[...] = up.astype(BF16)
        act = (gate * _sigmoid(gate) * up).astype(BF16)
        ao_ref[...] = act
        acc[...] += _dot(act, wd_ref[...])

        @pl.when(s == N_SHARDS - 1)
        def _():
            xo_ref[...] = x_ref[...] + acc[...]

    row_spec = pl.BlockSpec((tm, D_MODEL), lambda t, s: (t, 0))
    row_w = pl.BlockSpec((None, FF_SHARD, D_MODEL), lambda t, s: (s, 0, 0))
    act_spec = pl.BlockSpec((None, tm, FF_SHARD), lambda t, s: (s, t, 0))
    return _pcall(
        body, name=name, grid=(t_len // tm, N_SHARDS),
        in_specs=[row_spec, pl.BlockSpec((1, D_MODEL), lambda t, s: (0, 0)), row_w, row_w, row_w],
        out_specs=[row_spec, act_spec, act_spec, act_spec, row_spec],
        out_shape=[_sds((t_len, D_MODEL), BF16)] + [_sds((N_SHARDS, t_len, FF_SHARD), BF16)] * 3
                  + [_sds((t_len, D_MODEL), F32)],
        scratch_shapes=[pltpu.VMEM((tm, D_MODEL), BF16), pltpu.VMEM((tm, D_MODEL), F32)],
        semantics=("arbitrary", "arbitrary"), duty=duty,
    )(x, g_row, w_gate_t, w_up_t, w_down)


def ffn_bwd(dxo, x, g_row, gate, up, w_gate_t, w_up_t, w_down, name, duty=None):
    t_len = x.shape[0]
    tm = _tile_rows(t_len)

    def body(dxo_ref, x_ref, g_ref, gate_ref, up_ref, wg_ref, wu_ref, wd_ref,
             dg_ref, du_ref, dx_ref, dn_ref, dxb, dh):
        t = pl.program_id(0)
        s = pl.program_id(1)

        @pl.when(s == 0)
        def _():
            dxb[...] = dxo_ref[...].astype(BF16)
            dh[...] = jnp.zeros_like(dh)

        @pl.when(jnp.logical_and(s == 0, t == 0))
        def _():
            dn_ref[...] = jnp.zeros_like(dn_ref)

        sub_tiles = _sub_tiles(tm)
        dacts = [_dot_nt(dxb[rows, :], wd_ref[...]) for rows in sub_tiles]
        for rows, dact in zip(sub_tiles, dacts):
            gv = gate_ref[rows, :].astype(F32)
            uv = up_ref[rows, :].astype(F32)
            sg = _sigmoid(gv)
            dgv = (dact * uv * (sg * (1.0 + gv * (1.0 - sg)))).astype(BF16)
            duv = (dact * (gv * sg)).astype(BF16)
            dg_ref[rows, :] = dgv
            du_ref[rows, :] = duv
            dh[rows, :] += _dot(dgv, wg_ref[...]) + _dot(duv, wu_ref[...])

        @pl.when(s == N_SHARDS - 1)
        def _():
            dx, dn = _rms_bwd(dh[...], x_ref[...], g_ref[...])
            dx_ref[...] = dxo_ref[...] + dx
            dn_ref[...] += dn

    row_spec = pl.BlockSpec((tm, D_MODEL), lambda t, s: (t, 0))
    vec_spec = pl.BlockSpec((1, D_MODEL), lambda t, s: (0, 0))
    row_w = pl.BlockSpec((None, FF_SHARD, D_MODEL), lambda t, s: (s, 0, 0))
    act_spec = pl.BlockSpec((None, tm, FF_SHARD), lambda t, s: (s, t, 0))
    act_shape = _sds((N_SHARDS, t_len, FF_SHARD), BF16)
    return _pcall(
        body, name=name, grid=(t_len // tm, N_SHARDS),
        in_specs=[row_spec, row_spec, vec_spec, act_spec, act_spec, row_w, row_w, row_w],
        out_specs=[act_spec, act_spec, row_spec, vec_spec],
        out_shape=[act_shape, act_shape, _sds((t_len, D_MODEL), F32), _sds((1, D_MODEL), F32)],
        scratch_shapes=[pltpu.VMEM((tm, D_MODEL), BF16), pltpu.VMEM((tm, D_MODEL), F32)],
        semantics=("arbitrary", "arbitrary"), duty=duty,
    )(dxo, x, g_row, gate, up, w_gate_t, w_up_t, w_down)


def tn_matmul(name, a, b, a_spec, b_spec, out_shape, out_spec, grid, duty=None):
    def body(a_ref, b_ref, o_ref):
        @pl.when(pl.program_id(len(grid) - 1) == 0)
        def _():
            o_ref[...] = jnp.zeros_like(o_ref)

        res = _dot_tn(a_ref[...].astype(BF16), b_ref[...].astype(BF16))
        o_ref[...] += res.reshape(o_ref.shape)

    return _pcall(body, name=name, grid=grid, in_specs=[a_spec, b_spec], out_specs=out_spec,
                  out_shape=out_shape, semantics=("arbitrary",) * len(grid), duty=duty)(a, b)


def wgrad_full(name, a, b, duty=None):
    t_len, k = a.shape
    n = b.shape[1]
    tt = _wgrad_rows(t_len)
    return tn_matmul(name, a, b,
                     pl.BlockSpec((tt, k), lambda t: (t, 0)), pl.BlockSpec((tt, n), lambda t: (t, 0)),
                     _sds((k, n), F32), pl.BlockSpec((k, n), lambda t: (0, 0)), (t_len // tt,), duty)


def wgrad_col_sharded(name, a, b_sh, duty=None):
    t_len, k = a.shape
    n_sh, _, n = b_sh.shape
    tt = _wgrad_rows(t_len)
    return tn_matmul(name, a, b_sh,
                     pl.BlockSpec((tt, k), lambda s, t: (t, 0)), pl.BlockSpec((None, tt, n), lambda s, t: (s, t, 0)),
                     _sds((n_sh, k, n), F32), pl.BlockSpec((None, k, n), lambda s, t: (s, 0, 0)),
                     (n_sh, t_len // tt), duty)


def wgrad_row_sharded(name, a_sh, b, duty=None):
    t_len, n = b.shape
    n_sh, _, k = a_sh.shape
    tt = _wgrad_rows(t_len)

    def body(a_ref, b_ref, o_ref):
        s = pl.program_id(1)
        res = _dot_tn(a_ref[...], b_ref[...].astype(BF16))

        @pl.when(pl.program_id(0) == 0)
        def _():
            o_ref[s] = res

        @pl.when(pl.program_id(0) > 0)
        def _():
            o_ref[s] += res

    return _pcall(body, name=name, grid=(t_len // tt, n_sh),
                  in_specs=[pl.BlockSpec((None, tt, k), lambda t, s: (s, t, 0)),
                            pl.BlockSpec((tt, n), lambda t, s: (t, 0))],
                  out_specs=pl.BlockSpec((n_sh, k, n), lambda t, s: (0, 0, 0)),
                  out_shape=_sds((n_sh, k, n), F32), semantics=("arbitrary", "arbitrary"), duty=duty)(a_sh, b)


def wgrad_pool_group(name, p, dzs, duty=None):
    t_len = p.shape[0]
    tt = _wgrad_rows(t_len)
    gd = POOL_GROUP_DIM
    rows = gd // N_SHARDS
    return tn_matmul(name, p, dzs,
                     pl.BlockSpec((tt, gd), lambda g, t: (t, g)), pl.BlockSpec((tt, gd), lambda g, t: (t, g)),
                     _sds((N_SHARDS, 4, rows, gd), F32),
                     pl.BlockSpec((N_SHARDS, None, rows, gd), lambda g, t: (0, g, 0, 0)),
                     (4, t_len // tt), duty)


PAD_LANES = 384
QKV_PAD = 3 * PAD_LANES
N_SLABS = QKV_PAD // LANE
GROUP_REAL = tuple(GROUP_LANES[g + 1] - GROUP_LANES[g] for g in range(3))
Q_BLOCK = 512


def pad_groups(w, axis):
    parts = []
    for g in range(3):
        blk = lax.slice_in_dim(w, GROUP_LANES[g], GROUP_LANES[g + 1], axis=axis)
        pad = [(0, 0)] * w.ndim
        pad[axis] = (0, PAD_LANES - GROUP_REAL[g])
        parts.append(jnp.pad(blk, pad))
    return parts


def unpad_groups(parts, axis):
    return jnp.concatenate([lax.slice_in_dim(p, 0, GROUP_REAL[g], axis=axis) for g, p in enumerate(parts)],
                           axis=axis)


def _qkv_pieces(group, part):
    lo, hi = part * D_MODEL + GROUP_LANES[group], part * D_MODEL + GROUP_LANES[group + 1]
    pieces = []
    while lo < hi:
        shard = lo // QKV_SHARD
        end = min(hi, (shard + 1) * QKV_SHARD)
        pieces.append((shard, lo - shard * QKV_SHARD, end - shard * QKV_SHARD))
        lo = end
    return pieces


def pad_qkv_weight(w_qkv_sh):
    groups = []
    for g in range(3):
        cols = []
        for part in range(3):
            cols += [w_qkv_sh[s][:, lo:hi] for s, lo, hi in _qkv_pieces(g, part)]
            if GROUP_REAL[g] < PAD_LANES:
                cols.append(jnp.zeros((D_MODEL, PAD_LANES - GROUP_REAL[g]), w_qkv_sh.dtype))
        groups.append(jnp.concatenate(cols, axis=1))
    return jnp.stack(groups)


def unpad_qkv_grad(g_pad):
    shard_cols = [[] for _ in range(N_SHARDS)]
    for part in range(3):
        for g in range(3):
            at = part * PAD_LANES
            for s, lo, hi in _qkv_pieces(g, part):
                shard_cols[s].append(g_pad[g][:, at:at + hi - lo])
                at += hi - lo
    return jnp.stack([jnp.concatenate(cols, axis=1) for cols in shard_cols])


def rope_tables(t_len):
    inv_freq = 1.0 / (ROPE_THETA ** (jnp.arange(0, HEAD_DIM, 2, dtype=F32) / HEAD_DIM))
    ang = jnp.arange(t_len, dtype=F32)[:, None] * inv_freq[None, :]
    cos_h, sin_h = lax.optimization_barrier((jnp.cos(ang), jnp.sin(ang)))
    reps = (1, 2 * LANE // HEAD_DIM)
    return jnp.tile(cos_h, reps), jnp.tile(sin_h, reps)


def _rot_half(v):
    n = v.shape[1]
    lane = lax.broadcasted_iota(jnp.int32, v.shape, 1)
    return jnp.where(lane % HEAD_DIM < HEAD_DIM // 2,
                     -pltpu.roll(v, n - HEAD_DIM // 2, 1), pltpu.roll(v, HEAD_DIM // 2, 1))


def _lane_cols(j):
    return slice(j * LANE, (j + 1) * LANE)


def _to_residue_major(slab, j_src, dst_ref, j_dst, dil, rows):
    for r in range(dil):
        dst_ref[r, :, _lane_cols(j_dst)] = slab[j_src, pl.ds(r, rows // dil, stride=dil), :].astype(dst_ref.dtype)


def _to_natural(src_ref, j_src, slab, j_dst, dil, rows):
    for r in range(dil):
        slab[j_dst, pl.ds(r, rows // dil, stride=dil), :] = src_ref[r, :, _lane_cols(j_src)].astype(F32)


def qkv_fwd(x, g_row, w_pad, cos, sin):
    t_len = x.shape[0]
    tm = _tile_rows(t_len)

    def body(x_ref, g_ref, w_ref, cos_ref, sin_ref, h_ref, o1_ref, o4_ref, o16_ref, slabs):
        h = _rms_fwd(x_ref[...], g_ref[...]).astype(BF16)
        h_ref[...] = h
        accs = [_dot(h, w_ref[gi]) for gi in range(3)]
        cos_t = cos_ref[...]
        sin_t = sin_ref[...]
        for gi, (dil, o_ref) in enumerate(zip(GROUP_DIL, (o1_ref, o4_ref, o16_ref))):
            slab = slabs.at[gi]
            for j in range(N_SLABS):
                a = accs[gi][:, _lane_cols(j)]
                if j < 6:
                    a = a * cos_t + _rot_half(a) * sin_t
                if j < 3:
                    a = a * (HEAD_DIM ** -0.5)
                if dil == 1:
                    o_ref[0, :, _lane_cols(j)] = a.astype(BF16)
                else:
                    slab[j] = a
                    _to_residue_major(slab, j, o_ref, j, dil, tm)

    row_spec = pl.BlockSpec((tm, D_MODEL), lambda t: (t, 0))
    tab_spec = pl.BlockSpec((tm, LANE), lambda t: (t, 0))
    out_specs = [row_spec] + [pl.BlockSpec((d, tm // d, QKV_PAD), lambda t: (0, t, 0)) for d in GROUP_DIL]
    out_shape = [_sds((t_len, D_MODEL), BF16)] + [_sds((d, t_len // d, QKV_PAD), BF16) for d in GROUP_DIL]
    return _pcall(
        body, name="qkv_fwd", grid=(t_len // tm,),
        in_specs=[row_spec, pl.BlockSpec((1, D_MODEL), lambda t: (0, 0)),
                  pl.BlockSpec((3, D_MODEL, QKV_PAD), lambda t: (0, 0, 0)), tab_spec, tab_spec],
        out_specs=out_specs, out_shape=out_shape,
        scratch_shapes=[pltpu.VMEM((3, N_SLABS, tm, LANE), F32)],
        semantics=("arbitrary",),
    )(x, g_row, w_pad, cos, sin)


def _band_masks():
    qi = lax.broadcasted_iota(jnp.int32, (ATTN_W, 2 * ATTN_W), 0)
    kj = lax.broadcasted_iota(jnp.int32, (ATTN_W, 2 * ATTN_W), 1)
    dist = ATTN_W + qi - kj
    band = (dist >= 0) & (dist <= ATTN_W)
    return band, band & (kj >= ATTN_W)


def _half_masks():
    lane = lax.broadcasted_iota(jnp.int32, (1, LANE), 1)
    return [lane < HEAD_DIM, lane >= HEAD_DIM]


def _live_halves(gi, j):
    hms = _half_masks()
    return hms if (gi == 0 or j < 2) else hms[:1]


def attn_fwd(qkv_g, gi, name):
    dil, l_len, _ = qkv_g.shape
    qb = min(Q_BLOCK, l_len)
    nsub = qb // ATTN_W

    def body(q_ref, kc_ref, kp_ref, vc_ref, vp_ref, o_ref, lse_ref, kbuf, vbuf):
        n = pl.program_id(1)
        kbuf[pl.ds(0, ATTN_W), :] = kp_ref[...]
        kbuf[pl.ds(ATTN_W, qb), :] = kc_ref[...]
        vbuf[pl.ds(0, ATTN_W), :] = vp_ref[...]
        vbuf[pl.ds(ATTN_W, qb), :] = vc_ref[...]
        band, band_first = _band_masks()

        def sub(b, carry):
            r0 = pl.multiple_of(b * ATTN_W, ATTN_W)
            mask = band_first | (band & (n + b > 0))
            krows = pl.ds(r0, 2 * ATTN_W)
            scores = []
            for j in range(3):
                q = q_ref[pl.ds(r0, ATTN_W), _lane_cols(j)]
                for hm in _live_halves(gi, j):
                    scores.append(_dot_nt(jnp.where(hm, q, jnp.zeros_like(q)), kbuf[krows, _lane_cols(j)]))
            scores = iter(scores)
            head_lane = lax.broadcasted_iota(jnp.int32, (1, LANE), 1)
            lse = jnp.zeros((ATTN_W, LANE), F32)
            for j in range(3):
                cols = _lane_cols(j)
                v = vbuf[krows, cols]
                o = jnp.zeros((ATTN_W, LANE), F32)
                for half, hm in enumerate(_live_halves(gi, j)):
                    s = jnp.where(mask, next(scores), NEG_INF)
                    m = jnp.max(s, axis=-1, keepdims=True)
                    e = jnp.exp(s - m)
                    den = jnp.sum(e, axis=-1, keepdims=True)
                    p = (e * (1.0 / den)).astype(BF16)
                    o = jnp.where(hm, _dot(p, v), o)
                    lse = jnp.where(head_lane == 2 * j + half, m + jnp.log(den), lse)
                o_ref[pl.ds(r0, ATTN_W), cols] = o.astype(BF16)
            lse_ref[pl.ds(r0, ATTN_W), :] = lse
            return carry

        lax.fori_loop(0, nsub, sub, 0)

    cur = lambda c: pl.BlockSpec((None, qb, PAD_LANES), lambda r, n: (r, n, c))
    prev = lambda c: pl.BlockSpec((None, ATTN_W, PAD_LANES), lambda r, n: (r, jnp.maximum(n * nsub - 1, 0), c))
    return _pcall(
        body, name=name, grid=(dil, l_len // qb),
        in_specs=[cur(0), cur(1), prev(1), cur(2), prev(2)],
        out_specs=[pl.BlockSpec((None, qb, PAD_LANES), lambda r, n: (r, n, 0)),
                   pl.BlockSpec((None, qb, LANE), lambda r, n: (r, n, 0))],
        out_shape=[_sds((dil, l_len, PAD_LANES), BF16), _sds((dil, l_len, LANE), F32)],
        scratch_shapes=[pltpu.VMEM((qb + ATTN_W, PAD_LANES), BF16), pltpu.VMEM((qb + ATTN_W, PAD_LANES), BF16)],
        semantics=("arbitrary", "arbitrary"),
    )(qkv_g, qkv_g, qkv_g, qkv_g, qkv_g)


def _group_stats(lses):
    head_lane = lax.broadcasted_iota(jnp.int32, (1, LANE), 1)
    fulls, glse = [], []
    for g in range(3):
        real = head_lane < GROUP_HEADS[g]
        mx = jnp.max(jnp.where(real, lses[g], -jnp.inf), axis=-1, keepdims=True)
        sm = jnp.sum(jnp.where(real, jnp.exp(lses[g] - mx), 0.0), axis=-1, keepdims=True)
        fulls.append(mx + jnp.log(sm))
        glse.append(fulls[g] - math.log(GROUP_HEADS[g]))
    top = jnp.maximum(jnp.maximum(glse[0], glse[1]), glse[2])
    ex = [jnp.exp(v - top) for v in glse]
    tot = ex[0] + ex[1] + ex[2]
    alpha = [v / tot for v in ex]
    lane = lax.broadcasted_iota(jnp.int32, (1, QKV_PAD), 1)
    scale = jnp.where(lane < PAD_LANES, 3.0 * alpha[0],
                      jnp.where(lane < 2 * PAD_LANES, 3.0 * alpha[1], 3.0 * alpha[2]))
    return alpha, fulls, scale


def attn_out_fwd(x, o_parts, lse_parts, w_out_pad):
    t_len = x.shape[0]
    tm = _tile_rows(t_len)

    def body(x_ref, o1, o4, o16, l1, l4, l16, w_ref, xo_ref, mg_ref, o_ref, lse_ref, o_slab, l_slab):
        for gi, (dil, og, lg) in enumerate(zip(GROUP_DIL, (o1, o4, o16), (l1, l4, l16))):
            for j in range(3):
                _to_natural(og, j, o_slab, 3 * gi + j, dil, tm)
            _to_natural(lg, 0, l_slab, gi, dil, tm)
        o = jnp.concatenate([o_slab[j] for j in range(N_SLABS)], axis=1)
        lses = [l_slab[gi] for gi in range(3)]
        o_ref[...] = o.astype(BF16)
        for gi in range(3):
            lse_ref[:, _lane_cols(gi)] = lses[gi]
        _, _, scale = _group_stats(lses)
        merged = (o * scale).astype(BF16)
        mg_ref[...] = merged
        xo_ref[...] = x_ref[...] + _dot(merged, w_ref[...])

    row_spec = pl.BlockSpec((tm, D_MODEL), lambda t: (t, 0))
    pad_spec = pl.BlockSpec((tm, QKV_PAD), lambda t: (t, 0))
    o_specs = [pl.BlockSpec((d, tm // d, PAD_LANES), lambda t: (0, t, 0)) for d in GROUP_DIL]
    lse_specs = [pl.BlockSpec((d, tm // d, LANE), lambda t: (0, t, 0)) for d in GROUP_DIL]
    return _pcall(
        body, name="attn_out_fwd", grid=(t_len // tm,),
        in_specs=[row_spec] + o_specs + lse_specs + [pl.BlockSpec((QKV_PAD, D_MODEL), lambda t: (0, 0))],
        out_specs=[row_spec, pad_spec, pad_spec, pl.BlockSpec((tm, 3 * LANE), lambda t: (t, 0))],
        out_shape=[_sds((t_len, D_MODEL), F32), _sds((t_len, QKV_PAD), BF16),
                   _sds((t_len, QKV_PAD), BF16), _sds((t_len, 3 * LANE), F32)],
        scratch_shapes=[pltpu.VMEM((N_SLABS, tm, LANE), F32), pltpu.VMEM((3, tm, LANE), F32)],
        semantics=("arbitrary",),
    )(x, *o_parts, *lse_parts, w_out_pad)


def attn_out_bwd(dxo, w_out_pad, o, lse, duty=None):
    t_len = dxo.shape[0]
    tm = _tile_rows(t_len)

    def body(dx_ref, w_ref, o_ref, lse_ref, d1, d4, d16, c1, c4, c16, slab):
        dmerged = _dot_nt(dx_ref[...].astype(BF16), w_ref[...])
        o_t = o_ref[...].astype(F32)
        lses = [lse_ref[:, _lane_cols(gi)] for gi in range(3)]
        alpha, fulls, scale = _group_stats(lses)
        e = dmerged * o_t
        lane = lax.broadcasted_iota(jnp.int32, (1, QKV_PAD), 1)
        dalpha = [3.0 * jnp.sum(jnp.where((lane >= g * PAD_LANES) & (lane < g * PAD_LANES + GROUP_REAL[g]), e, 0.0),
                                axis=-1, keepdims=True) for g in range(3)]
        mean_da = alpha[0] * dalpha[0] + alpha[1] * dalpha[1] + alpha[2] * dalpha[2]
        dglse = [alpha[g] * (dalpha[g] - mean_da) for g in range(3)]
        do = dmerged * scale
        es = e * scale
        for j in range(N_SLABS):
            slab[j] = do[:, _lane_cols(j)]
        for gi, (dil, dg) in enumerate(zip(GROUP_DIL, (d1, d4, d16))):
            for j in range(3):
                _to_residue_major(slab, 3 * gi + j, dg, j, dil, tm)
        head_lane = lax.broadcasted_iota(jnp.int32, (1, LANE), 1)
        first = head_lane < HEAD_DIM
        for gi, (dil, cg) in enumerate(zip(GROUP_DIL, (c1, c4, c16))):
            c_g = -(dglse[gi] * jnp.exp(lses[gi] - fulls[gi]))
            for j in range(3):
                blk = es[:, _lane_cols(3 * gi + j)]
                halves = (jnp.sum(jnp.where(first, blk, 0.0), axis=-1, keepdims=True),
                          jnp.sum(jnp.where(first, 0.0, blk), axis=-1, keepdims=True))
                for half in range(2):
                    c_g = c_g + jnp.where(head_lane == 2 * j + half, halves[half], 0.0)
            slab[gi] = c_g
            _to_residue_major(slab, gi, cg, 0, dil, tm)

    row_spec = pl.BlockSpec((tm, D_MODEL), lambda t: (t, 0))
    pad_spec = pl.BlockSpec((tm, QKV_PAD), lambda t: (t, 0))
    do_specs = [pl.BlockSpec((d, tm // d, PAD_LANES), lambda t: (0, t, 0)) for d in GROUP_DIL]
    c_specs = [pl.BlockSpec((d, tm // d, LANE), lambda t: (0, t, 0)) for d in GROUP_DIL]
    outs = _pcall(
        body, name="attn_out_bwd", grid=(t_len // tm,),
        in_specs=[row_spec, pl.BlockSpec((QKV_PAD, D_MODEL), lambda t: (0, 0)), pad_spec,
                  pl.BlockSpec((tm, 3 * LANE), lambda t: (t, 0))],
        out_specs=do_specs + c_specs,
        out_shape=[_sds((d, t_len // d, PAD_LANES), BF16) for d in GROUP_DIL]
                  + [_sds((d, t_len // d, LANE), F32) for d in GROUP_DIL],
        scratch_shapes=[pltpu.VMEM((N_SLABS, tm, LANE), F32)],
        semantics=("arbitrary",), duty=duty,
    )(dxo, w_out_pad, o, lse)
    if duty is None:
        return outs[:3], outs[3:]
    return (outs[0][:3], outs[0][3:]), outs[1]


def attn_bwd(qkv_g, do_g, lse_g, c_g, gi, name, duty=None):
    dil, l_len, _ = qkv_g.shape
    qb = min(Q_BLOCK, l_len)
    nsub = qb // ATTN_W
    nsb = l_len // qb

    def body(q_ref, kc_ref, kp_ref, vc_ref, vp_ref, do_ref, lse_ref, c_ref,
             qn_ref, don_ref, lsen_ref, cn_ref, o_ref, kbuf, vbuf, dkbuf, dvbuf):
        n = pl.program_id(1)
        kbuf[pl.ds(0, ATTN_W), :] = kp_ref[...]
        kbuf[pl.ds(ATTN_W, qb), :] = kc_ref[...]
        vbuf[pl.ds(0, ATTN_W), :] = vp_ref[...]
        vbuf[pl.ds(ATTN_W, qb), :] = vc_ref[...]
        dkbuf[...] = jnp.zeros_like(dkbuf)
        dvbuf[...] = jnp.zeros_like(dvbuf)

        def block(q_of, do_of, lse_of, c_of, krows, mask, dq_rows):
            heads = []
            for j in range(3):
                cols = _lane_cols(j)
                q, do_t, k, v = q_of(cols), do_of(cols), kbuf[krows, cols], vbuf[krows, cols]
                for half, hm in enumerate(_live_halves(gi, j)):
                    qh = jnp.where(hm, q, jnp.zeros_like(q))
                    doh = jnp.where(hm, do_t, jnp.zeros_like(do_t))
                    heads.append((j, 2 * j + half, hm, qh, doh, _dot_nt(qh, k), _dot_nt(doh, v)))
            head_lane = lax.broadcasted_iota(jnp.int32, (1, LANE), 1)
            lse_t, c_t = lse_of(), c_of()
            for j in range(3):
                cols = _lane_cols(j)
                k = kbuf[krows, cols]
                dq = jnp.zeros((ATTN_W, LANE), F32)
                dk = jnp.zeros((k.shape[0], LANE), F32)
                dv = jnp.zeros((k.shape[0], LANE), F32)
                for hj, head, hm, qh, doh, s, dp in heads:
                    if hj != j:
                        continue
                    lse_h = jnp.max(jnp.where(head_lane == head, lse_t, -jnp.inf), axis=-1, keepdims=True)
                    c_h = jnp.max(jnp.where(head_lane == head, c_t, -jnp.inf), axis=-1, keepdims=True)
                    p = jnp.exp(jnp.where(mask, s, NEG_INF) - lse_h)
                    ds = (p * (dp - c_h)).astype(BF16)
                    if dq_rows is not None:
                        dq = jnp.where(hm, _dot(ds, k), dq)
                    dk = dk + _dot_tn(ds, qh)
                    dv = dv + _dot_tn(p.astype(BF16), doh)
                if dq_rows is not None:
                    o_ref[dq_rows, cols] = dq.astype(BF16)
                dkbuf[krows, cols] += dk
                dvbuf[krows, cols] += dv

        band, band_first = _band_masks()

        def sub(b, carry):
            rows = pl.ds(pl.multiple_of(b * ATTN_W, ATTN_W), ATTN_W)
            krows = pl.ds(pl.multiple_of(b * ATTN_W, ATTN_W), 2 * ATTN_W)
            block(lambda c: q_ref[rows, c], lambda c: do_ref[rows, c], lambda: lse_ref[rows, :],
                  lambda: c_ref[rows, :], krows, band_first | (band & (n + b > 0)), rows)
            return carry

        lax.fori_loop(0, nsub, sub, 0)

        qi = lax.broadcasted_iota(jnp.int32, (ATTN_W, ATTN_W), 0)
        kj = lax.broadcasted_iota(jnp.int32, (ATTN_W, ATTN_W), 1)
        nmask = (qi <= kj) & (n < nsb - 1)
        block(lambda c: qn_ref[:, c], lambda c: don_ref[:, c], lambda: lsen_ref[...],
              lambda: cn_ref[...], pl.ds(qb, ATTN_W), nmask, None)
        o_ref[:, pl.ds(PAD_LANES, PAD_LANES)] = dkbuf[pl.ds(ATTN_W, qb), :].astype(BF16)
        o_ref[:, pl.ds(2 * PAD_LANES, PAD_LANES)] = dvbuf[pl.ds(ATTN_W, qb), :].astype(BF16)

    cur = lambda c: pl.BlockSpec((None, qb, PAD_LANES), lambda r, n: (r, n, c))
    prev = lambda c: pl.BlockSpec((None, ATTN_W, PAD_LANES), lambda r, n: (r, jnp.maximum(n * nsub - 1, 0), c))
    nxt_row = lambda r, n: (r, jnp.minimum((n + 1) * nsub, nsb * nsub - 1), 0)
    nxt = pl.BlockSpec((None, ATTN_W, PAD_LANES), nxt_row)
    head_cur = pl.BlockSpec((None, qb, LANE), lambda r, n: (r, n, 0))
    head_nxt = pl.BlockSpec((None, ATTN_W, LANE), nxt_row)
    return _pcall(
        body, name=name, grid=(dil, nsb),
        in_specs=[cur(0), cur(1), prev(1), cur(2), prev(2), cur(0), head_cur, head_cur, nxt, nxt, head_nxt, head_nxt],
        out_specs=pl.BlockSpec((None, qb, QKV_PAD), lambda r, n: (r, n, 0)),
        out_shape=_sds((dil, l_len, QKV_PAD), BF16),
        scratch_shapes=[pltpu.VMEM((qb + ATTN_W, PAD_LANES), BF16), pltpu.VMEM((qb + ATTN_W, PAD_LANES), BF16),
                        pltpu.VMEM((qb + ATTN_W, PAD_LANES), F32), pltpu.VMEM((qb + ATTN_W, PAD_LANES), F32)],
        semantics=("arbitrary", "arbitrary"), duty=duty,
    )(qkv_g, qkv_g, qkv_g, qkv_g, qkv_g, do_g, lse_g, c_g, qkv_g, do_g, lse_g, c_g)


def qkv_bwd(dqkv_parts, w_pad, dxo, x, g_row, cos, sin):
    t_len = x.shape[0]
    tm = _tile_rows(t_len)

    def body(p1, p4, p16, w_ref, dxo_ref, x_ref, g_ref, cos_ref, sin_ref, dq_ref, dx_ref, dn_ref, slabs):
        @pl.when(pl.program_id(0) == 0)
        def _():
            dn_ref[...] = jnp.zeros_like(dn_ref)

        cos_t = cos_ref[...]
        sin_t = sin_ref[...]
        dh = None
        for gi, (dil, part) in enumerate(zip(GROUP_DIL, (p1, p4, p16))):
            slab = slabs.at[gi]
            for j in range(N_SLABS):
                if dil == 1:
                    a = part[0, :, _lane_cols(j)].astype(F32)
                else:
                    _to_natural(part, j, slab, j, dil, tm)
                    a = slab[j]
                if j < 6:
                    a = a * cos_t - _rot_half(a * sin_t)
                if j < 3:
                    a = a * (HEAD_DIM ** -0.5)
                dq_ref[gi, :, _lane_cols(j)] = a.astype(BF16)
            contrib = _dot_nt(dq_ref[gi], w_ref[gi])
            dh = contrib if dh is None else dh + contrib
        dx, dn = _rms_bwd(dh, x_ref[...], g_ref[...])
        dx_ref[...] = dxo_ref[...] + dx
        dn_ref[...] += dn

    row_spec = pl.BlockSpec((tm, D_MODEL), lambda t: (t, 0))
    vec_spec = pl.BlockSpec((1, D_MODEL), lambda t: (0, 0))
    tab_spec = pl.BlockSpec((tm, LANE), lambda t: (t, 0))
    part_specs = [pl.BlockSpec((d, tm // d, QKV_PAD), lambda t: (0, t, 0)) for d in GROUP_DIL]
    return _pcall(
        body, name="qkv_bwd", grid=(t_len // tm,),
        in_specs=part_specs + [pl.BlockSpec((3, D_MODEL, QKV_PAD), lambda t: (0, 0, 0)),
                               row_spec, row_spec, vec_spec, tab_spec, tab_spec],
        out_specs=[pl.BlockSpec((3, tm, QKV_PAD), lambda t: (0, t, 0)), row_spec, vec_spec],
        out_shape=[_sds((3, t_len, QKV_PAD), BF16), _sds((t_len, D_MODEL), F32), _sds((1, D_MODEL), F32)],
        scratch_shapes=[pltpu.VMEM((3, N_SLABS, tm, LANE), F32)],
        semantics=("arbitrary",),
    )(*dqkv_parts, w_pad, dxo, x, g_row, cos, sin)


def final_fwd_bwd(x, g_row, target):
    t_len = x.shape[0]
    tm = _tile_rows(t_len)

    def body(x_ref, g_ref, tgt_ref, dx_ref, dn_ref, loss_ref):
        @pl.when(pl.program_id(0) == 0)
        def _():
            dn_ref[...] = jnp.zeros_like(dn_ref)
            loss_ref[...] = jnp.zeros_like(loss_ref)

        x_t = x_ref[...]
        g = g_ref[...]
        diff = _rms_fwd(x_t, g) - tgt_ref[...]
        loss_ref[...] += 0.5 * jnp.sum(jnp.mean(diff * diff, axis=-1, keepdims=True), axis=0, keepdims=True)
        dx, dn = _rms_bwd(diff * (1.0 / D_MODEL), x_t, g)
        dx_ref[...] = dx
        dn_ref[...] += dn

    row_spec = pl.BlockSpec((tm, D_MODEL), lambda t: (t, 0))
    vec_spec = pl.BlockSpec((1, D_MODEL), lambda t: (0, 0))
    return _pcall(
        body, name="final_fwd_bwd", grid=(t_len // tm,),
        in_specs=[row_spec, vec_spec, row_spec],
        out_specs=[row_spec, vec_spec, pl.BlockSpec((1, 1), lambda t: (0, 0))],
        out_shape=[_sds((t_len, D_MODEL), F32), _sds((1, D_MODEL), F32), _sds((1, 1), F32)],
        semantics=("arbitrary",),
    )(x, g_row, target)


def pool_bwd(dxo, x, g_row, w_in, w_grp, scale, w_out, zr, duty=None):
    t_len = x.shape[0]
    tm = _tile_rows(t_len)
    nt = t_len // tm

    def body(dxo_ref, x_ref, g_ref, win_ref, wgrp_ref, scale_ref, wout_ref, zr_ref,
             dzs_ref, du_ref, dx_ref, dn_ref, dsc_ref, ebuf, tmp_a, tmp_b):
        i = pl.program_id(0)
        t = nt - 1 - i

        @pl.when(i == 0)
        def _():
            ebuf[pl.ds(tm, POOL_HALO), :] = jnp.zeros((POOL_HALO, D_MODEL), F32)
            dn_ref[...] = jnp.zeros_like(dn_ref)
            dsc_ref[...] = jnp.zeros_like(dsc_ref)

        dxo_t = dxo_ref[...]
        dz = _dot_nt(dxo_t.astype(BF16), wout_ref[...])
        dsc_ref[...] += jnp.sum(dz * zr_ref[...].astype(F32), axis=0, keepdims=True)
        dzs_ref[...] = (dz * scale_ref[...]).astype(BF16)
        row = t * tm + lax.broadcasted_iota(jnp.int32, (tm, 1), 0)
        for gi, w in enumerate(POOL_WINDOWS):
            cols = pl.ds(gi * POOL_GROUP_DIM, POOL_GROUP_DIM)
            dp_g = _dot_nt(dzs_ref[:, cols], wgrp_ref[gi])
            inv_cnt = 1.0 / jnp.minimum(row + 1, w).astype(F32)
            ebuf[pl.ds(0, tm), cols] = dp_g * inv_cnt
            acc = _window_sums(ebuf, cols, (tmp_a, tmp_b), gi + 1, tm, back=False) - dp_g
            du_ref[:, cols] = acc.astype(BF16)
        ebuf[pl.ds(tm, POOL_HALO), :] = ebuf[pl.ds(0, POOL_HALO), :]
        dh = _dot_nt(du_ref[...], win_ref[...])
        dx, dn = _rms_bwd(dh, x_ref[...], g_ref[...])
        dx_ref[...] = dxo_t + dx
        dn_ref[...] += dn

    row_spec = pl.BlockSpec((tm, D_MODEL), lambda i: (nt - 1 - i, 0))
    full = lambda shape: pl.BlockSpec(shape, lambda i: (0,) * len(shape))
    vec = full((1, D_MODEL))
    return _pcall(
        body, name="pool_bwd", grid=(nt,),
        in_specs=[row_spec, row_spec, vec, full((D_MODEL, D_MODEL)), full((4, POOL_GROUP_DIM, POOL_GROUP_DIM)),
                  vec, full((D_MODEL, D_MODEL)), row_spec],
        out_specs=[row_spec, row_spec, row_spec, vec, vec],
        out_shape=[_sds((t_len, D_MODEL), BF16), _sds((t_len, D_MODEL), BF16), _sds((t_len, D_MODEL), F32),
                   _sds((1, D_MODEL), F32), _sds((1, D_MODEL), F32)],
        scratch_shapes=[pltpu.VMEM((tm + POOL_HALO, D_MODEL), F32)]
                       + [pltpu.VMEM((tm + POOL_HALO, POOL_GROUP_DIM), F32)] * 2,
        semantics=("arbitrary",), duty=duty,
    )(dxo, x, g_row, w_in, w_grp, scale, w_out, zr)


def _mesh_pos():
    return lax.axis_index("x"), lax.axis_index("y"), lax.axis_index("c")


def _other_chips(x, y):
    return [(1 - x, y), (x, 1 - y), (1 - x, 1 - y)]


def _remote(src, dst, send_sem, recv_sem, device):
    return pltpu.make_async_remote_copy(src_ref=src, dst_ref=dst, send_sem=send_sem, recv_sem=recv_sem,
                                        device_id=device, device_id_type=MESH)


class _Duty:
    aliases = {}

    def mid(self, ins, outs, sems):
        pass


class Together(_Duty):
    def __init__(self, duties):
        self.duties = duties
        self.ins = [a for d in duties for a in d.ins]
        self.out_shape = [s for d in duties for s in d.out_shape]
        self.scratch = [s for d in duties for s in d.scratch]
        self.aliases = {}
        i0 = o0 = 0
        for d in duties:
            self.aliases.update({i0 + i: o0 + o for i, o in d.aliases.items()})
            i0 += len(d.ins)
            o0 += len(d.out_shape)

    def _each(self, ins, outs, sems):
        i0 = o0 = s0 = 0
        for d in self.duties:
            ni, no, ns = len(d.ins), len(d.out_shape), len(d.scratch)
            yield d, ins[i0:i0 + ni], outs[o0:o0 + no], sems[s0:s0 + ns]
            i0, o0, s0 = i0 + ni, o0 + no, s0 + ns

    def split(self, outs):
        return [list(o) for _, _, o, _ in self._each(self.ins, outs, self.scratch)]

    def start(self, ins, outs, sems):
        for d, i, o, s in self._each(ins, outs, sems):
            d.start(i, o, s)

    def mid(self, ins, outs, sems):
        for d, i, o, s in self._each(ins, outs, sems):
            d.mid(i, o, s)

    def finish(self, ins, outs, sems):
        for d, i, o, s in self._each(ins, outs, sems):
            d.finish(i, o, s)


def run_duty(duty, name):
    d_in, d_out = len(duty.ins), len(duty.out_shape)

    def body(*refs):
        ins, outs, sems = refs[:d_in], refs[d_in:d_in + d_out], refs[d_in + d_out:]
        duty.start(ins, outs, sems)
        duty.mid(ins, outs, sems)
        duty.finish(ins, outs, sems)

    return pl.pallas_call(
        body, name=name, out_shape=list(duty.out_shape), in_specs=[_ANY] * d_in, out_specs=[_ANY] * d_out,
        scratch_shapes=list(duty.scratch), input_output_aliases=dict(duty.aliases),
        compiler_params=pltpu.CompilerParams(has_side_effects=True),
    )(*duty.ins)


class GatherWeights(_Duty):
    N_COPIES = 7

    def __init__(self, shards):
        n = self.n = len(shards)
        self.halves = [s.shape[0] // 2 for s in shards]
        my_slot = 2 * lax.axis_index("x") + lax.axis_index("y")
        staged = [lax.dynamic_update_slice(lax.empty((N_SHARDS,) + s.shape, s.dtype), s[None], (my_slot, 0, 0))
                  for s in shards]
        self.ins = list(shards) + staged
        self.out_shape = [_sds((N_SHARDS,) + s.shape, s.dtype) for s in shards]
        self.aliases = {n + a: a for a in range(n)}
        self.scratch = [pltpu.SemaphoreType.DMA((n, self.N_COPIES)), pltpu.SemaphoreType.DMA((n, self.N_COPIES))]

    def _copies(self, ins, outs, sems, a):
        x, y, c = _mesh_pos()
        h = self.halves[a]
        q = h // 2
        sibling, to_x, to_y = (x, y, 1 - c), (1 - x, y, c), (x, 1 - y, c)
        me, of_x, of_y, of_d = 2 * x + y, 2 * (1 - x) + y, 2 * x + 1 - y, 2 * (1 - x) + 1 - y
        half = lambda core: pl.ds(core * h, h)
        quarter = lambda k: pl.ds(c * h + k * q, q)
        out = outs[a]
        mine = ins[a].at[half(c)]

        def copy(k, src, dst, device):
            return _remote(src, dst, sems[0].at[a, k], sems[1].at[a, k], device)

        same = lambda k, ref, device: copy(k, ref, ref, device)
        sent = [copy(0, mine, out.at[me, half(c)], to_x),
                copy(1, mine, out.at[me, half(c)], to_y),
                same(2, out.at[of_x, quarter(0)], to_y),
                same(3, out.at[of_y, quarter(1)], to_x),
                same(4, out.at[of_x, half(c)], sibling),
                same(5, out.at[of_y, half(c)], sibling),
                same(6, out.at[of_d, half(c)], sibling)]
        landing = [out.at[of_x, half(c)], out.at[of_y, half(c)], out.at[of_d, quarter(0)], out.at[of_d, quarter(1)],
                   out.at[of_x, half(1 - c)], out.at[of_y, half(1 - c)], out.at[of_d, half(1 - c)]]
        return sent, [same(k, ref, sibling) for k, ref in enumerate(landing)]

    def start(self, ins, outs, sems):
        for a in range(self.n):
            sent, _ = self._copies(ins, outs, sems, a)
            sent[0].start()
            sent[1].start()

    def mid(self, ins, outs, sems):
        for a in range(self.n):
            sent, lands = self._copies(ins, outs, sems, a)
            lands[0].wait_recv()
            sent[2].start()
            sent[4].start()
            lands[1].wait_recv()
            sent[3].start()
            sent[5].start()

    def finish(self, ins, outs, sems):
        for a in range(self.n):
            sent, lands = self._copies(ins, outs, sems, a)
            lands[2].wait_recv()
            lands[3].wait_recv()
            sent[6].start()
        for a in range(self.n):
            sent, lands = self._copies(ins, outs, sems, a)
            for cp in lands[4:]:
                cp.wait_recv()
            for cp in sent:
                cp.wait_send()


class GradReducer:
    def __init__(self, c_idx, pos_idx):
        self.c_idx, self.pos_idx = c_idx, pos_idx
        self.in_flight = []
        self.done = {}

    def push(self, name, grad):
        self.in_flight.append(dict(name=name, stage="halves", data=grad))

    def _duties(self):
        make = {"halves": SiblingHalves, "exchange": ChipExchange, "share": SiblingShare}
        return Together([make[w["stage"]]([w["data"]]) for w in self.in_flight])

    def _advance(self, duties, outs):
        still = []
        for w, (res,) in zip(self.in_flight, duties.split(outs)):
            if w["stage"] == "halves":
                partial = add_my_half(w["data"], res, self.c_idx, f"rs_add_{w['name']}")
                still.append(dict(name=w["name"], stage="exchange", data=partial))
            elif w["stage"] == "exchange":
                reduced = sum_slots(res, w["data"], self.pos_idx, f"rs_sum_{w['name']}")
                still.append(dict(name=w["name"], stage="share", data=reduced))
            else:
                self.done[w["name"]] = res
        self.in_flight = still

    def carried_by(self, fn, *args, **kw):
        if not self.in_flight:
            return fn(*args, **kw)
        duties = self._duties()
        out, duty_outs = fn(*args, duty=duties, **kw)
        self._advance(duties, duty_outs)
        return out

    def drain(self, name):
        step = 0
        while self.in_flight:
            duties = self._duties()
            self._advance(duties, run_duty(duties, f"{name}{step}"))
            step += 1


class SiblingHalves(_Duty):
    def __init__(self, grads):
        n = len(grads)
        self.halves = [g.shape[1] // 2 for g in grads]
        self.ins = list(grads)
        self.out_shape = [_sds((N_SHARDS, h, g.shape[2]), g.dtype) for g, h in zip(grads, self.halves)]
        self.scratch = [pltpu.SemaphoreType.DMA((n,)), pltpu.SemaphoreType.DMA((n,))]

    def _copies(self, ins, outs, sems):
        x, y, c = _mesh_pos()
        return [_remote(ins[a].at[:, pl.ds((1 - c) * h, h)], outs[a], sems[0].at[a], sems[1].at[a], (x, y, 1 - c))
                for a, h in enumerate(self.halves)]

    def start(self, ins, outs, sems):
        for cp in self._copies(ins, outs, sems):
            cp.start()

    def finish(self, ins, outs, sems):
        for cp in self._copies(ins, outs, sems):
            cp.wait()


class ChipExchange(_Duty):
    def __init__(self, parts):
        n = self.n = len(parts)
        self.ins = list(parts)
        self.out_shape = [_sds(p.shape, p.dtype) for p in parts]
        self.scratch = [pltpu.SemaphoreType.DMA((n, 3)), pltpu.SemaphoreType.DMA((n, 3))]

    def _copies(self, ins, outs, sems, arriving):
        x, y, c = _mesh_pos()
        cps = []
        for a in range(self.n):
            for j, chip in enumerate(_other_chips(x, y)):
                theirs = 2 * chip[0] + chip[1]
                src = outs[a].at[theirs] if arriving else ins[a].at[theirs]
                dst = outs[a].at[theirs] if arriving else outs[a].at[2 * x + y]
                cps.append(_remote(src, dst, sems[0].at[a, j], sems[1].at[a, j], (*chip, c)))
        return cps

    def start(self, ins, outs, sems):
        for cp in self._copies(ins, outs, sems, False):
            cp.start()

    def finish(self, ins, outs, sems):
        for cp in self._copies(ins, outs, sems, True):
            cp.wait_recv()
        for cp in self._copies(ins, outs, sems, False):
            cp.wait_send()


class SiblingShare(_Duty):
    def __init__(self, reduced):
        n = self.n = len(reduced)
        self.ins = list(reduced)
        self.out_shape = [_sds(r.shape, r.dtype) for r in reduced]
        self.aliases = {a: a for a in range(n)}
        self.scratch = [pltpu.SemaphoreType.DMA((n,)), pltpu.SemaphoreType.DMA((n,))]

    def _copies(self, outs, sems, half_of):
        x, y, c = _mesh_pos()
        cps = []
        for a in range(self.n):
            h = outs[a].shape[0] // 2
            rows = outs[a].at[pl.ds(half_of(c) * h, h)]
            cps.append(_remote(rows, rows, sems[0].at[a], sems[1].at[a], (x, y, 1 - c)))
        return cps

    def start(self, ins, outs, sems):
        for cp in self._copies(outs, sems, lambda core: core):
            cp.start()

    def finish(self, ins, outs, sems):
        for cp in self._copies(outs, sems, lambda core: 1 - core):
            cp.wait_recv()
        for cp in self._copies(outs, sems, lambda core: core):
            cp.wait_send()


def allreduce_small(v):
    def body(v_ref, o_ref, buf, send_sems, recv_sems):
        x, y, c = _mesh_pos()
        me = 4 * x + 2 * y + c
        buf[me] = v_ref[...]
        flip = lambda p, f: 1 - p if f else p
        peers = [(flip(x, k & 4), flip(y, k & 2), flip(c, k & 1)) for k in range(1, N_DEV)]
        cps = []
        for k, peer in enumerate(peers):
            cp = _remote(v_ref, buf.at[me], send_sems.at[k], recv_sems.at[k], peer)
            cp.start()
            cps.append(cp)
        for k, peer in enumerate(peers):
            slot = buf.at[4 * peer[0] + 2 * peer[1] + peer[2]]
            _remote(slot, slot, send_sems.at[k], recv_sems.at[k], peer).wait_recv()
        for cp in cps:
            cp.wait_send()
        acc = buf[0]
        for i in range(1, N_DEV):
            acc = acc + buf[i]
        o_ref[...] = acc

    vm = pl.BlockSpec(memory_space=pltpu.VMEM)
    return pl.pallas_call(
        body, name="allreduce_small", out_shape=_sds(v.shape, v.dtype), in_specs=[vm], out_specs=vm,
        scratch_shapes=[pltpu.VMEM((N_DEV,) + v.shape, v.dtype),
                        pltpu.SemaphoreType.DMA((N_DEV - 1,)), pltpu.SemaphoreType.DMA((N_DEV - 1,))],
        compiler_params=pltpu.CompilerParams(has_side_effects=True),
    )(v)


def add_my_half(grad, theirs, c_idx, name):
    _, r, cols = grad.shape
    h = r // 2

    def body(c_ref, g_ref, t_ref, o_ref):
        o_ref[...] = (g_ref[...] + t_ref[...]).astype(BF16)

    slot = pl.BlockSpec((None, h, cols), lambda s, c: (s, 0, 0))
    grid_spec = pltpu.PrefetchScalarGridSpec(
        num_scalar_prefetch=1, grid=(N_SHARDS,),
        in_specs=[pl.BlockSpec((None, h, cols), lambda s, c: (s, c[0], 0)), slot], out_specs=slot)
    return pl.pallas_call(
        body, name=name, grid_spec=grid_spec, out_shape=_sds((N_SHARDS, h, cols), BF16),
        compiler_params=pltpu.CompilerParams(dimension_semantics=("arbitrary",), vmem_limit_bytes=VMEM_LIMIT_BYTES),
    )(c_idx, grad, theirs)


def sum_slots(received, mine, pos_idx, name):
    _, h, cols = received.shape

    def body(pos_ref, r_ref, m_ref, o_ref):
        acc = None
        for k in range(N_SHARDS):
            term = jnp.where(pos_ref[0] == k, m_ref[k], r_ref[k]).astype(F32)
            acc = term if acc is None else acc + term
        o_ref[...] = acc

    whole = pl.BlockSpec((N_SHARDS, h, cols), lambda i, pos: (0, 0, 0))
    grid_spec = pltpu.PrefetchScalarGridSpec(
        num_scalar_prefetch=1, grid=(1,), in_specs=[whole, whole],
        out_specs=pl.BlockSpec((h, cols), lambda i, pos: (pos[1], 0)))
    return pl.pallas_call(
        body, name=name, grid_spec=grid_spec, out_shape=_sds((2 * h, cols), F32),
        compiler_params=pltpu.CompilerParams(dimension_semantics=("arbitrary",), vmem_limit_bytes=VMEM_LIMIT_BYTES),
    )(pos_idx, received, mine)


def adamw(name, grads, w, m, v):
    n_layers, r, cols = w.shape
    tr = r // 2 if r % 16 == 0 else r
    bias1 = 1.0 - ADAM_B1 ** ADAM_STEP
    bias2 = 1.0 - ADAM_B2 ** ADAM_STEP

    def body(*refs):
        g_refs = refs[:n_layers]
        w_ref, m_ref, v_ref, go_ref, d_ref, mo_ref, vo_ref = refs[n_layers:]
        g = g_refs[0][...]
        for layer in range(1, n_layers):
            g = jnp.where(pl.program_id(0) == layer, g_refs[layer][...], g)
        m_new = ADAM_B1 * m_ref[...] + (1.0 - ADAM_B1) * g
        v_new = ADAM_B2 * v_ref[...] + (1.0 - ADAM_B2) * (g * g)
        m_hat = m_new / bias1
        v_hat = v_new / bias2
        go_ref[...] = g
        d_ref[...] = -ADAM_LR * (m_hat / (jnp.sqrt(v_hat) + ADAM_EPS) + ADAM_WD * w_ref[...])
        mo_ref[...] = m_new
        vo_ref[...] = v_new

    g_spec = pl.BlockSpec((tr, cols), lambda l, i: (i, 0))
    lay_spec = pl.BlockSpec((None, tr, cols), lambda l, i: (l, i, 0))
    shape = _sds((n_layers, r, cols), F32)
    return _pcall(
        body, name=name, grid=(n_layers, r // tr),
        in_specs=[g_spec] * n_layers + [lay_spec] * 3, out_specs=[lay_spec] * 4,
        out_shape=[shape] * 4, semantics=("arbitrary", "arbitrary"),
    )(*grads, w, m, v)


def kernel(x, norm_mix, norm_ffn, norm_final, pool_w_in, pool_w_group, pool_scale, pool_w_out, attn_w_qkv, attn_w_out, ffn_w_gate, ffn_w_up, ffn_w_down, loss_target, m_norm_mix, m_norm_ffn, m_norm_final, m_pool_w_in, m_pool_w_group, m_pool_scale, m_pool_w_out, m_attn_w_qkv, m_attn_w_out, m_ffn_w_gate, m_ffn_w_up, m_ffn_w_down, v_norm_mix, v_norm_ffn, v_norm_final, v_pool_w_in, v_pool_w_group, v_pool_scale, v_pool_w_out, v_attn_w_qkv, v_attn_w_out, v_ffn_w_gate, v_ffn_w_up, v_ffn_w_down):
    t_len = x.shape[1]
    x0 = x.reshape(t_len, D_MODEL)
    target = loss_target.reshape(t_len, D_MODEL)
    row = lambda a: a.reshape(1, D_MODEL)

    grp_rows = POOL_GROUP_DIM // N_SHARDS
    bf = lambda a: a.astype(BF16)
    gate_t, up_t = jnp.swapaxes(ffn_w_gate, 1, 2), jnp.swapaxes(ffn_w_up, 1, 2)
    pool_shards = [bf(pool_w_in[0]), bf(pool_w_group[0].reshape(4 * grp_rows, POOL_GROUP_DIM)), bf(pool_w_out[0])]
    ffn0_shards = [bf(gate_t[0]), bf(up_t[0]), bf(ffn_w_down[0])]
    late_shards = [bf(attn_w_qkv[0]), bf(attn_w_out[0]), bf(gate_t[1]), bf(up_t[1]), bf(ffn_w_down[1])]
    cos, sin = rope_tables(t_len)
    c_idx = lax.axis_index("c").astype(jnp.int32).reshape(1)
    pos_idx = jnp.stack([2 * lax.axis_index("x") + lax.axis_index("y"), lax.axis_index("c")]).astype(jnp.int32)
    chip_rows = lambda g: g.reshape(N_SHARDS, D_MODEL // N_SHARDS, D_MODEL)

    h0, g_pool = rms_norm(x0, row(norm_mix[0]), "rms_pool", duty=GatherWeights(pool_shards))
    w_in = g_pool[0].reshape(D_MODEL, D_MODEL)
    w_grp = g_pool[1].reshape(N_SHARDS, 4, grp_rows, POOL_GROUP_DIM).transpose(1, 0, 2, 3).reshape(
        4, POOL_GROUP_DIM, POOL_GROUP_DIM)
    w_out = g_pool[2].reshape(D_MODEL, D_MODEL)
    (p, zr, z, x1), ffn0 = pool_fwd(x0, h0, w_in, w_grp, pool_scale, w_out, duty=GatherWeights(ffn0_shards))
    (h1, gate0, up0, act0, x2), late = ffn_fwd(x1, row(norm_ffn[0]), *ffn0, "ffn_fwd0",
                                               duty=GatherWeights(late_shards))
    w_qkv = pad_qkv_weight(late[0])
    w_ao = jnp.concatenate(pad_groups(late[1].reshape(D_MODEL, D_MODEL), 0), axis=0)
    ffn1 = late[2:5]
    h2, *qkv_parts = qkv_fwd(x2, row(norm_mix[1]), w_qkv, cos, sin)
    o_parts, lse_parts = [], []
    for gi in range(3):
        o_g, lse_g = attn_fwd(qkv_parts[gi], gi, f"attn_fwd_g{gi}")
        o_parts.append(o_g)
        lse_parts.append(lse_g)
    x3, merged, o_nat, lse_nat = attn_out_fwd(x2, o_parts, lse_parts, w_ao)
    h3, gate1, up1, act1, x4 = ffn_fwd(x3, row(norm_ffn[1]), *ffn1, "ffn_fwd1")
    dx4, d_norm_final, loss_local = final_fwd_bwd(x4, row(norm_final), target)

    red = GradReducer(c_idx, pos_idx)
    dgate1, dup1, dx3, d_nf1 = ffn_bwd(dx4, x3, row(norm_ffn[1]), gate1, up1, *ffn1, "ffn_bwd1")
    g_gate1 = wgrad_row_sharded("wgrad_gate1", dgate1, h3)
    g_up1 = wgrad_row_sharded("wgrad_up1", dup1, h3)
    g_down1 = wgrad_row_sharded("wgrad_down1", act1, dx4)
    red.push("gate1", g_gate1)
    red.push("up1", g_up1)
    do_parts, c_parts = red.carried_by(attn_out_bwd, dx3, w_ao, o_nat, lse_nat)
    g_ao = wgrad_full("wgrad_attn_out", merged, dx3)
    red.push("down1", g_down1)
    red.push("attn_out", chip_rows(unpad_groups(jnp.split(g_ao, 3, axis=0), 0)))
    dqkv_parts = [red.carried_by(attn_bwd, qkv_parts[gi], do_parts[gi], lse_parts[gi], c_parts[gi], gi,
                                 f"attn_bwd_g{gi}") for gi in range(3)]
    dqkv, dx2, d_nm1 = qkv_bwd(dqkv_parts, w_qkv, dx3, x2, row(norm_mix[1]), cos, sin)
    red.push("qkv", unpad_qkv_grad(wgrad_col_sharded("wgrad_qkv", h2, dqkv)))

    dgate0, dup0, dx1, d_nf0 = red.carried_by(ffn_bwd, dx2, x1, row(norm_ffn[0]), gate0, up0, *ffn0, "ffn_bwd0")
    red.push("gate0", red.carried_by(wgrad_row_sharded, "wgrad_gate0", dgate0, h1))
    red.push("up0", red.carried_by(wgrad_row_sharded, "wgrad_up0", dup0, h1))
    red.push("down0", red.carried_by(wgrad_row_sharded, "wgrad_down0", act0, dx2))
    red.push("pool_out", chip_rows(red.carried_by(wgrad_full, "wgrad_pool_out", z, dx1)))
    dzs, du, dx0, d_nm0, d_scale = pool_bwd(dx1, x0, row(norm_mix[0]), w_in, w_grp, pool_scale, w_out, zr)
    g_grp = red.carried_by(wgrad_pool_group, "wgrad_pool_group", p, dzs)
    red.push("pool_group", g_grp.reshape(N_SHARDS, 4 * grp_rows, POOL_GROUP_DIM))
    red.push("pool_in", chip_rows(red.carried_by(wgrad_full, "wgrad_pool_in", h0, du)))
    red.drain("rs_tail")
    full = [red.done[nm] for nm in ("pool_in", "pool_group", "pool_out", "qkv", "attn_out",
                                    "gate0", "gate1", "up0", "up1", "down0", "down1")]

    zero_row = jnp.zeros((1, D_MODEL), F32)
    small = jnp.concatenate([d_nm0, d_nm1, d_nf0, d_nf1, d_norm_final, d_scale,
                             jnp.broadcast_to(loss_local, (1, D_MODEL)), zero_row], axis=0)
    small = allreduce_small(small)
    loss = small[6, 0]

    pack = lambda a, b, c, d: jnp.concatenate([a, b, row(c), d, zero_row, zero_row], axis=0)[None]
    sg, sd, sm, sv = adamw("adamw_small", [small],
                           pack(norm_mix, norm_ffn, norm_final, pool_scale),
                           pack(m_norm_mix, m_norm_ffn, m_norm_final, m_pool_scale),
                           pack(v_norm_mix, v_norm_ffn, v_norm_final, v_pool_scale))
    unpack = lambda a: (a[0, 0:2], a[0, 2:4], a[0, 4], a[0, 5:6])

    def update(name, grads, w, m, v, transposed=False):
        if transposed:
            w, m, v = (jnp.swapaxes(a, 1, 2) for a in (w, m, v))
        n_layers = len(grads)
        shp = (n_layers,) + grads[0].shape
        outs = [o.reshape(w.shape) for o in adamw(name, grads, w.reshape(shp), m.reshape(shp), v.reshape(shp))]
        return [jnp.swapaxes(o, 1, 2) for o in outs] if transposed else outs

    big = [
        update("adamw_pool_in", [full[0]], pool_w_in, m_pool_w_in, v_pool_w_in),
        update("adamw_pool_group", [full[1]], pool_w_group, m_pool_w_group, v_pool_w_group),
        update("adamw_pool_out", [full[2]], pool_w_out, m_pool_w_out, v_pool_w_out),
        update("adamw_qkv", [full[3]], attn_w_qkv, m_attn_w_qkv, v_attn_w_qkv),
        update("adamw_attn_out", [full[4]], attn_w_out, m_attn_w_out, v_attn_w_out),
        update("adamw_gate", [full[5], full[6]], ffn_w_gate, m_ffn_w_gate, v_ffn_w_gate, transposed=True),
        update("adamw_up", [full[7], full[8]], ffn_w_up, m_ffn_w_up, v_ffn_w_up, transposed=True),
        update("adamw_down", [full[9], full[10]], ffn_w_down, m_ffn_w_down, v_ffn_w_down),
    ]

    def leaves(k, small_vals):
        nm, nf, nfin, psc = unpack(small_vals)
        return [nm, nf, nfin, big[0][k], big[1][k], psc, big[2][k], big[3][k], big[4][k],
                big[5][k], big[6][k], big[7][k]]

    grad_x = dx0.reshape(x.shape)
    return (loss, grad_x, *leaves(0, sg), *leaves(1, sd), *leaves(2, sm), *leaves(3, sv))
```

```python
import math

import jax
import jax.numpy as jnp
from jax import lax
from jax.experimental import pallas as pl
from jax.experimental.pallas import tpu as pltpu

F32 = jnp.float32
BF16 = jnp.bfloat16

D_MODEL = 1024
N_SHARDS = 4
N_DEV = 8
D_FF = 2816
FF_SHARD = D_FF // N_SHARDS
HEAD_DIM = 64
QKV_SHARD = 3 * D_MODEL // N_SHARDS
POOL_WINDOWS = (2, 4, 8, 16)
POOL_GROUP_DIM = 256
POOL_HALO = 32
ATTN_W = 128
GROUP_LANES = (0, 384, 704, 1024)
GROUP_HEADS = (6, 5, 5)
GROUP_DIL = (1, 4, 16)
ROPE_THETA = 10000.0
EPS = 1e-6
NEG_INF = -1e30
LANE = 128
VMEM_LIMIT_BYTES = 60 * 1024 * 1024

ADAM_LR = 0.001
ADAM_B1 = 0.9
ADAM_B2 = 0.999
ADAM_EPS = 1e-08
ADAM_WD = 0.01
ADAM_STEP = 10

NT_DIMS = (((1,), (1,)), ((), ()))
TN_DIMS = (((0,), (0,)), ((), ()))
MESH = pl.DeviceIdType.MESH


_ANY = pl.BlockSpec(memory_space=pl.ANY)


def _pcall(body, *, name, out_shape, grid=None, in_specs=None, out_specs=None, scratch_shapes=(),
           semantics=None, duty=None):
    kw = {}
    if in_specs is not None and duty is None:
        kw["in_specs"] = in_specs
    if out_specs is not None and duty is None:
        kw["out_specs"] = out_specs
    if grid is not None:
        kw["grid"] = grid
    params = dict(dimension_semantics=semantics, vmem_limit_bytes=VMEM_LIMIT_BYTES)
    if duty is None:
        return pl.pallas_call(body, name=name, out_shape=out_shape, scratch_shapes=list(scratch_shapes),
                              compiler_params=pltpu.CompilerParams(**params), **kw)

    single = not isinstance(out_shape, (list, tuple))
    c_out_shape = [out_shape] if single else list(out_shape)
    c_out_specs = [out_specs] if single else list(out_specs)
    n_in, n_out, n_scr = len(in_specs), len(c_out_shape), len(scratch_shapes)
    d_in, d_out = len(duty.ins), len(duty.out_shape)
    total = math.prod(grid)
    mid_step = (2 * total) // 3

    def wrapped(*refs):
        c_in, d_ins = refs[:n_in], refs[n_in:n_in + d_in]
        o0 = n_in + d_in
        c_outs, d_outs = refs[o0:o0 + n_out], refs[o0 + n_out:o0 + n_out + d_out]
        s0 = o0 + n_out + d_out
        c_scr, d_sems = refs[s0:s0 + n_scr], refs[s0 + n_scr:]
        step = pl.program_id(0)
        for ax in range(1, len(grid)):
            step = step * grid[ax] + pl.program_id(ax)

        @pl.when(step == 0)
        def _():
            duty.start(d_ins, d_outs, d_sems)

        body(*c_in, *c_outs, *c_scr)

        @pl.when(step == mid_step)
        def _():
            duty.mid(d_ins, d_outs, d_sems)

        @pl.when(step == total - 1)
        def _():
            duty.finish(d_ins, d_outs, d_sems)

    call = pl.pallas_call(
        wrapped, name=name, grid=grid,
        in_specs=list(in_specs) + [_ANY] * d_in, out_specs=c_out_specs + [_ANY] * d_out,
        out_shape=c_out_shape + list(duty.out_shape),
        scratch_shapes=list(scratch_shapes) + list(duty.scratch),
        input_output_aliases={n_in + i: n_out + o for i, o in duty.aliases.items()},
        compiler_params=pltpu.CompilerParams(has_side_effects=True, **params))

    def run(*args):
        outs = call(*args, *duty.ins)
        c = outs[:n_out]
        return (c[0] if single else list(c)), list(outs[n_out:])

    return run


def _sds(shape, dtype):
    return jax.ShapeDtypeStruct(tuple(shape), dtype)


def _dot(a, b):
    return jnp.dot(a, b, preferred_element_type=F32)


def _dot_nt(a, b):
    return lax.dot_general(a, b, NT_DIMS, preferred_element_type=F32)


def _dot_tn(a, b):
    return lax.dot_general(a, b, TN_DIMS, preferred_element_type=F32)


def _rms_fwd(x, g):
    r = lax.rsqrt(jnp.mean(x * x, axis=-1, keepdims=True) + EPS)
    return x * r * g


def _rms_bwd(dh, x, g):
    r = lax.rsqrt(jnp.mean(x * x, axis=-1, keepdims=True) + EPS)
    xh = x * r
    dg = jnp.sum(dh * xh, axis=0, keepdims=True)
    dxh = dh * g
    dx = r * (dxh - xh * jnp.mean(dxh * xh, axis=-1, keepdims=True))
    return dx, dg


def _sigmoid(x):
    return 0.5 * jnp.tanh(0.5 * x) + 0.5


def _tile_rows(t):
    return min(512, t)


def _sub_tiles(tm, n_sub=2):
    rows = tm // n_sub
    return [pl.ds(i * rows, rows) for i in range(n_sub)]


def _wgrad_rows(t):
    return min(2048, t)


def _window_sums(buf, cols, tmps, levels, tm, back):
    src, src_cols = buf, cols
    for k in range(1, levels + 1):
        shift = 1 << (k - 1)
        last = k == levels
        if back:
            lo = POOL_HALO if last else 8 * k
            n = tm + POOL_HALO - lo
            val = src[pl.ds(lo, n), src_cols] + src[pl.ds(lo - shift, n), src_cols]
        else:
            lo = 0
            n = tm if last else tm + POOL_HALO - 8 * k
            val = src[pl.ds(0, n), src_cols] + src[pl.ds(shift, n), src_cols]
        if last:
            return val
        tmps[k % 2][pl.ds(lo, n), :] = val
        src, src_cols = tmps[k % 2], slice(None)
def pool_fwd(x, g_row, w_in, w_grp, scale, w_out, duty=None):
    t_len = x.shape[0]
    tm = _tile_rows(t_len)

    def body(x_ref, g_ref, win_ref, wgrp_ref, scale_ref, wout_ref,
             h_ref, p_ref, zr_ref, z_ref, xo_ref, ubuf, tmp_a, tmp_b):
        t = pl.program_id(0)

        @pl.when(t == 0)
        def _():
            ubuf[pl.ds(0, POOL_HALO), :] = jnp.zeros((POOL_HALO, D_MODEL), F32)

        x_t = x_ref[...]
        h = _rms_fwd(x_t, g_ref[...]).astype(BF16)
        h_ref[...] = h
        ubuf[pl.ds(POOL_HALO, tm), :] = _dot(h, win_ref[...])
        row = t * tm + lax.broadcasted_iota(jnp.int32, (tm, 1), 0)
        for gi, w in enumerate(POOL_WINDOWS):
            cols = pl.ds(gi * POOL_GROUP_DIM, POOL_GROUP_DIM)
            u_g = ubuf[pl.ds(POOL_HALO, tm), cols]
            acc = _window_sums(ubuf, cols, (tmp_a, tmp_b), gi + 1, tm, back=True)
            inv_cnt = 1.0 / jnp.minimum(row + 1, w).astype(F32)
            p_g = (acc * inv_cnt - u_g).astype(BF16)
            p_ref[:, cols] = p_g
            z_g = _dot(p_g, wgrp_ref[gi])
            zr_ref[:, cols] = z_g.astype(BF16)
            z_ref[:, cols] = (z_g * scale_ref[:, cols]).astype(BF16)
        ubuf[pl.ds(0, POOL_HALO), :] = ubuf[pl.ds(tm, POOL_HALO), :]
        xo_ref[...] = x_t + _dot(z_ref[...], wout_ref[...])

    row_spec = pl.BlockSpec((tm, D_MODEL), lambda t: (t, 0))
    full2 = lambda shape: pl.BlockSpec(shape, lambda t: (0,) * len(shape))
    return _pcall(
        body, name="pool_fwd", grid=(t_len // tm,),
        in_specs=[row_spec, full2((1, D_MODEL)), full2((D_MODEL, D_MODEL)),
                  full2((4, POOL_GROUP_DIM, POOL_GROUP_DIM)), full2((1, D_MODEL)), full2((D_MODEL, D_MODEL))],
        out_specs=[row_spec] * 5,
        out_shape=[_sds((t_len, D_MODEL), BF16)] * 4 + [_sds((t_len, D_MODEL), F32)],
        scratch_shapes=[pltpu.VMEM((tm + POOL_HALO, D_MODEL), F32)]
                       + [pltpu.VMEM((tm + POOL_HALO, POOL_GROUP_DIM), F32)] * 2,
        semantics=("arbitrary",), duty=duty,
    )(x, g_row, w_in, w_grp, scale, w_out)


def ffn_fwd(x, g_row, w_gate_t, w_up_t, w_down, name, duty=None):
    t_len = x.shape[0]
    tm = min(1024, t_len)

    def body(x_ref, g_ref, wg_ref, wu_ref, wd_ref, h_ref, go_ref, uo_ref, ao_ref, xo_ref, hbuf, acc):
        s = pl.program_id(1)

        @pl.when(s == 0)
        def _():
            h = _rms_fwd(x_ref[...], g_ref[...]).astype(BF16)
            hbuf[...] = h
            h_ref[...] = h
            acc[...] = jnp.zeros_like(acc)

        h = hbuf[...]
        gate = _dot_nt(h, wg_ref[...])
        up = _dot_nt(h, wu_ref[...])
        go_ref[...] = gate.astype(BF16)
        uo_ref[...] = up.astype(BF16)
        act = (gate * _sigmoid(gate) * up).astype(BF16)
        ao_ref[...] = act
        acc[...] += _dot(act, wd_ref[...])

        @pl.when(s == N_SHARDS - 1)
        def _():
            xo_ref[...] = x_ref[...] + acc[...]

    row_spec = pl.BlockSpec((tm, D_MODEL), lambda t, s: (t, 0))
    row_w = pl.BlockSpec((None, FF_SHARD, D_MODEL), lambda t, s: (s, 0, 0))
    act_spec = pl.BlockSpec((None, tm, FF_SHARD), lambda t, s: (s, t, 0))
    return _pcall(
        body, name=name, grid=(t_len // tm, N_SHARDS),
        in_specs=[row_spec, pl.BlockSpec((1, D_MODEL), lambda t, s: (0, 0)), row_w, row_w, row_w],
        out_specs=[row_spec, act_spec, act_spec, act_spec, row_spec],
        out_shape=[_sds((t_len, D_MODEL), BF16)] + [_sds((N_SHARDS, t_len, FF_SHARD), BF16)] * 3
                  + [_sds((t_len, D_MODEL), F32)],
        scratch_shapes=[pltpu.VMEM((tm, D_MODEL), BF16), pltpu.VMEM((tm, D_MODEL), F32)],
        semantics=("arbitrary", "arbitrary"), duty=duty,
    )(x, g_row, w_gate_t, w_up_t, w_down)


def ffn_bwd(dxo, x, g_row, gate, up, w_gate_t, w_up_t, w_down, name, duty=None):
    t_len = x.shape[0]
    tm = _tile_rows(t_len)

    def body(dxo_ref, x_ref, g_ref, gate_ref, up_ref, wg_ref, wu_ref, wd_ref,
             dg_ref, du_ref, dx_ref, dn_ref, dxb, dh):
        t = pl.program_id(0)
        s = pl.program_id(1)

        @pl.when(s == 0)
        def _():
            dxb[...] = dxo_ref[...].astype(BF16)
            dh[...] = jnp.zeros_like(dh)

        @pl.when(jnp.logical_and(s == 0, t == 0))
        def _():
            dn_ref[...] = jnp.zeros_like(dn_ref)

        sub_tiles = _sub_tiles(tm)
        dacts = [_dot_nt(dxb[rows, :], wd_ref[...]) for rows in sub_tiles]
        for rows, dact in zip(sub_tiles, dacts):
            gv = gate_ref[rows, :].astype(F32)
            uv = up_ref[rows, :].astype(F32)
            sg = _sigmoid(gv)
            dgv = (dact * uv * (sg * (1.0 + gv * (1.0 - sg)))).astype(BF16)
            duv = (dact * (gv * sg)).astype(BF16)
            dg_ref[rows, :] = dgv
            du_ref[rows, :] = duv
            dh[rows, :] += _dot(dgv, wg_ref[...]) + _dot(duv, wu_ref[...])

        @pl.when(s == N_SHARDS - 1)
        def _():
            dx, dn = _rms_bwd(dh[...], x_ref[...], g_ref[...])
            dx_ref[...] = dxo_ref[...] + dx
            dn_ref[...] += dn

    row_spec = pl.BlockSpec((tm, D_MODEL), lambda t, s: (t, 0))
    vec_spec = pl.BlockSpec((1, D_MODEL), lambda t, s: (0, 0))
    row_w = pl.BlockSpec((None, FF_SHARD, D_MODEL), lambda t, s: (s, 0, 0))
    act_spec = pl.BlockSpec((None, tm, FF_SHARD), lambda t, s: (s, t, 0))
    act_shape = _sds((N_SHARDS, t_len, FF_SHARD), BF16)
    return _pcall(
        body, name=name, grid=(t_len // tm, N_SHARDS),
        in_specs=[row_spec, row_spec, vec_spec, act_spec, act_spec, row_w, row_w, row_w],
        out_specs=[act_spec, act_spec, row_spec, vec_spec],
        out_shape=[act_shape, act_shape, _sds((t_len, D_MODEL), F32), _sds((1, D_MODEL), F32)],
        scratch_shapes=[pltpu.VMEM((tm, D_MODEL), BF16), pltpu.VMEM((tm, D_MODEL), F32)],
        semantics=("arbitrary", "arbitrary"), duty=duty,
    )(dxo, x, g_row, gate, up, w_gate_t, w_up_t, w_down)


def tn_matmul(name, a, b, a_spec, b_spec, out_shape, out_spec, grid, duty=None):
    def body(a_ref, b_ref, o_ref):
        @pl.when(pl.program_id(len(grid) - 1) == 0)
        def _():
            o_ref[...] = jnp.zeros_like(o_ref)

        res = _dot_tn(a_ref[...].astype(BF16), b_ref[...].astype(BF16))
        o_ref[...] += res.reshape(o_ref.shape)

    return _pcall(body, name=name, grid=grid, in_specs=[a_spec, b_spec], out_specs=out_spec,
                  out_shape=out_shape, semantics=("arbitrary",) * len(grid), duty=duty)(a, b)


def wgrad_full(name, a, b, duty=None):
    t_len, k = a.shape
    n = b.shape[1]
    tt = _wgrad_rows(t_len)
    return tn_matmul(name, a, b,
                     pl.BlockSpec((tt, k), lambda t: (t, 0)), pl.BlockSpec((tt, n), lambda t: (t, 0)),
                     _sds((k, n), F32), pl.BlockSpec((k, n), lambda t: (0, 0)), (t_len // tt,), duty)


def wgrad_col_sharded(name, a, b_sh, duty=None):
    t_len, k = a.shape
    n_sh, _, n = b_sh.shape
    tt = _wgrad_rows(t_len)
    return tn_matmul(name, a, b_sh,
                     pl.BlockSpec((tt, k), lambda s, t: (t, 0)), pl.BlockSpec((None, tt, n), lambda s, t: (s, t, 0)),
                     _sds((n_sh, k, n), F32), pl.BlockSpec((None, k, n), lambda s, t: (s, 0, 0)),
                     (n_sh, t_len // tt), duty)


def wgrad_row_sharded(name, a_sh, b, duty=None):
    t_len, n = b.shape
    n_sh, _, k = a_sh.shape
    tt = _wgrad_rows(t_len)
    if b.dtype == BF16:
        return tn_matmul(name, a_sh, b,
                         pl.BlockSpec((None, tt, k), lambda s, t: (s, t, 0)), pl.BlockSpec((tt, n), lambda s, t: (t, 0)),
                         _sds((n_sh, k, n), F32), pl.BlockSpec((None, k, n), lambda s, t: (s, 0, 0)),
                         (n_sh, t_len // tt), duty)

    def body(a_ref, b_ref, o_ref):
        s = pl.program_id(1)
        res = _dot_tn(a_ref[...], b_ref[...].astype(BF16))

        @pl.when(pl.program_id(0) == 0)
        def _():
            o_ref[s] = res

        @pl.when(pl.program_id(0) > 0)
        def _():
            o_ref[s] += res

    return _pcall(body, name=name, grid=(t_len // tt, n_sh),
                  in_specs=[pl.BlockSpec((None, tt, k), lambda t, s: (s, t, 0)),
                            pl.BlockSpec((tt, n), lambda t, s: (t, 0))],
                  out_specs=pl.BlockSpec((n_sh, k, n), lambda t, s: (0, 0, 0)),
                  out_shape=_sds((n_sh, k, n), F32), semantics=("arbitrary", "arbitrary"), duty=duty)(a_sh, b)


def wgrad_pool_group(name, p, dzs, duty=None):
    t_len = p.shape[0]
    tt = _wgrad_rows(t_len)
    gd = POOL_GROUP_DIM
    rows = gd // N_SHARDS
    return tn_matmul(name, p, dzs,
                     pl.BlockSpec((tt, gd), lambda g, t: (t, g)), pl.BlockSpec((tt, gd), lambda g, t: (t, g)),
                     _sds((N_SHARDS, 4, rows, gd), F32),
                     pl.BlockSpec((N_SHARDS, None, rows, gd), lambda g, t: (0, g, 0, 0)),
                     (4, t_len // tt), duty)


PAD_LANES = 384
QKV_PAD = 3 * PAD_LANES
N_SLABS = QKV_PAD // LANE
GROUP_REAL = tuple(GROUP_LANES[g + 1] - GROUP_LANES[g] for g in range(3))
Q_BLOCK = 512


def pad_groups(w, axis):
    parts = []
    for g in range(3):
        blk = lax.slice_in_dim(w, GROUP_LANES[g], GROUP_LANES[g + 1], axis=axis)
        pad = [(0, 0)] * w.ndim
        pad[axis] = (0, PAD_LANES - GROUP_REAL[g])
        parts.append(jnp.pad(blk, pad))
    return parts


def unpad_groups(parts, axis):
    return jnp.concatenate([lax.slice_in_dim(p, 0, GROUP_REAL[g], axis=axis) for g, p in enumerate(parts)],
                           axis=axis)


def _qkv_pieces(group, part):
    lo, hi = part * D_MODEL + GROUP_LANES[group], part * D_MODEL + GROUP_LANES[group + 1]
    pieces = []
    while lo < hi:
        shard = lo // QKV_SHARD
        end = min(hi, (shard + 1) * QKV_SHARD)
        pieces.append((shard, lo - shard * QKV_SHARD, end - shard * QKV_SHARD))
        lo = end
    return pieces


def pad_qkv_weight(w_qkv_sh):
    groups = []
    for g in range(3):
        cols = []
        for part in range(3):
            cols += [w_qkv_sh[s][:, lo:hi] for s, lo, hi in _qkv_pieces(g, part)]
            if GROUP_REAL[g] < PAD_LANES:
                cols.append(jnp.zeros((D_MODEL, PAD_LANES - GROUP_REAL[g]), w_qkv_sh.dtype))
        groups.append(jnp.concatenate(cols, axis=1))
    return jnp.stack(groups)


def unpad_qkv_grad(g_pad):
    shard_cols = [[] for _ in range(N_SHARDS)]
    for part in range(3):
        for g in range(3):
            at = part * PAD_LANES
            for s, lo, hi in _qkv_pieces(g, part):
                shard_cols[s].append(g_pad[g][:, at:at + hi - lo])
                at += hi - lo
    return jnp.stack([jnp.concatenate(cols, axis=1) for cols in shard_cols])


def rope_tables(t_len):
    inv_freq = 1.0 / (ROPE_THETA ** (jnp.arange(0, HEAD_DIM, 2, dtype=F32) / HEAD_DIM))
    ang = jnp.arange(t_len, dtype=F32)[:, None] * inv_freq[None, :]
    cos_h, sin_h = lax.optimization_barrier((jnp.cos(ang), jnp.sin(ang)))
    reps = (1, 2 * LANE // HEAD_DIM)
    return jnp.tile(cos_h, reps), jnp.tile(sin_h, reps)


def _rot_half(v):
    n = v.shape[1]
    lane = lax.broadcasted_iota(jnp.int32, v.shape, 1)
    return jnp.where(lane % HEAD_DIM < HEAD_DIM // 2,
                     -pltpu.roll(v, n - HEAD_DIM // 2, 1), pltpu.roll(v, HEAD_DIM // 2, 1))


def _lane_cols(j):
    return slice(j * LANE, (j + 1) * LANE)


def _to_residue_major(slab, j_src, dst_ref, j_dst, dil, rows):
    for r in range(dil):
        dst_ref[r, :, _lane_cols(j_dst)] = slab[j_src, pl.ds(r, rows // dil, stride=dil), :].astype(dst_ref.dtype)


def _to_natural(src_ref, j_src, slab, j_dst, dil, rows):
    for r in range(dil):
        slab[j_dst, pl.ds(r, rows // dil, stride=dil), :] = src_ref[r, :, _lane_cols(j_src)].astype(F32)


def qkv_fwd(x, g_row, w_pad, cos, sin):
    t_len = x.shape[0]
    tm = _tile_rows(t_len)

    def body(x_ref, g_ref, w_ref, cos_ref, sin_ref, h_ref, o1_ref, o4_ref, o16_ref, slabs):
        h = _rms_fwd(x_ref[...], g_ref[...]).astype(BF16)
        h_ref[...] = h
        accs = [_dot(h, w_ref[gi]) for gi in range(3)]
        cos_t = cos_ref[...]
        sin_t = sin_ref[...]
        for gi, (dil, o_ref) in enumerate(zip(GROUP_DIL, (o1_ref, o4_ref, o16_ref))):
            slab = slabs.at[gi]
            for j in range(N_SLABS):
                a = accs[gi][:, _lane_cols(j)]
                if j < 6:
                    a = a * cos_t + _rot_half(a) * sin_t
                if j < 3:
                    a = a * (HEAD_DIM ** -0.5)
                if dil == 1:
                    o_ref[0, :, _lane_cols(j)] = a.astype(BF16)
                else:
                    slab[j] = a
                    _to_residue_major(slab, j, o_ref, j, dil, tm)

    row_spec = pl.BlockSpec((tm, D_MODEL), lambda t: (t, 0))
    tab_spec = pl.BlockSpec((tm, LANE), lambda t: (t, 0))
    out_specs = [row_spec] + [pl.BlockSpec((d, tm // d, QKV_PAD), lambda t: (0, t, 0)) for d in GROUP_DIL]
    out_shape = [_sds((t_len, D_MODEL), BF16)] + [_sds((d, t_len // d, QKV_PAD), BF16) for d in GROUP_DIL]
    return _pcall(
        body, name="qkv_fwd", grid=(t_len // tm,),
        in_specs=[row_spec, pl.BlockSpec((1, D_MODEL), lambda t: (0, 0)),
                  pl.BlockSpec((3, D_MODEL, QKV_PAD), lambda t: (0, 0, 0)), tab_spec, tab_spec],
        out_specs=out_specs, out_shape=out_shape,
        scratch_shapes=[pltpu.VMEM((3, N_SLABS, tm, LANE), F32)],
        semantics=("arbitrary",),
    )(x, g_row, w_pad, cos, sin)


def _band_masks():
    qi = lax.broadcasted_iota(jnp.int32, (ATTN_W, 2 * ATTN_W), 0)
    kj = lax.broadcasted_iota(jnp.int32, (ATTN_W, 2 * ATTN_W), 1)
    dist = ATTN_W + qi - kj
    band = (dist >= 0) & (dist <= ATTN_W)
    return band, band & (kj >= ATTN_W)


def _half_masks():
    lane = lax.broadcasted_iota(jnp.int32, (1, LANE), 1)
    return [lane < HEAD_DIM, lane >= HEAD_DIM]


def _live_halves(gi, j):
    hms = _half_masks()
    return hms if (gi == 0 or j < 2) else hms[:1]


def attn_fwd(qkv_g, gi, name):
    dil, l_len, _ = qkv_g.shape
    qb = min(Q_BLOCK, l_len)
    nsub = qb // ATTN_W

    def body(q_ref, kc_ref, kp_ref, vc_ref, vp_ref, o_ref, lse_ref, kbuf, vbuf):
        n = pl.program_id(1)
        kbuf[pl.ds(0, ATTN_W), :] = kp_ref[...]
        kbuf[pl.ds(ATTN_W, qb), :] = kc_ref[...]
        vbuf[pl.ds(0, ATTN_W), :] = vp_ref[...]
        vbuf[pl.ds(ATTN_W, qb), :] = vc_ref[...]
        band, band_first = _band_masks()

        def sub(b, carry):
            r0 = pl.multiple_of(b * ATTN_W, ATTN_W)
            mask = band_first | (band & (n + b > 0))
            krows = pl.ds(r0, 2 * ATTN_W)
            scores = []
            for j in range(3):
                q = q_ref[pl.ds(r0, ATTN_W), _lane_cols(j)]
                for hm in _live_halves(gi, j):
                    scores.append(_dot_nt(jnp.where(hm, q, jnp.zeros_like(q)), kbuf[krows, _lane_cols(j)]))
            scores = iter(scores)
            head_lane = lax.broadcasted_iota(jnp.int32, (1, LANE), 1)
            lse = jnp.zeros((ATTN_W, LANE), F32)
            for j in range(3):
                cols = _lane_cols(j)
                v = vbuf[krows, cols]
                o = jnp.zeros((ATTN_W, LANE), F32)
                for half, hm in enumerate(_live_halves(gi, j)):
                    s = jnp.where(mask, next(scores), NEG_INF)
                    m = jnp.max(s, axis=-1, keepdims=True)
                    e = jnp.exp(s - m)
                    den = jnp.sum(e, axis=-1, keepdims=True)
                    p = (e * (1.0 / den)).astype(BF16)
                    o = jnp.where(hm, _dot(p, v), o)
                    lse = jnp.where(head_lane == 2 * j + half, m + jnp.log(den), lse)
                o_ref[pl.ds(r0, ATTN_W), cols] = o.astype(BF16)
            lse_ref[pl.ds(r0, ATTN_W), :] = lse
            return carry

        lax.fori_loop(0, nsub, sub, 0)

    cur = lambda c: pl.BlockSpec((None, qb, PAD_LANES), lambda r, n: (r, n, c))
    prev = lambda c: pl.BlockSpec((None, ATTN_W, PAD_LANES), lambda r, n: (r, jnp.maximum(n * nsub - 1, 0), c))
    return _pcall(
        body, name=name, grid=(dil, l_len // qb),
        in_specs=[cur(0), cur(1), prev(1), cur(2), prev(2)],
        out_specs=[pl.BlockSpec((None, qb, PAD_LANES), lambda r, n: (r, n, 0)),
                   pl.BlockSpec((None, qb, LANE), lambda r, n: (r, n, 0))],
        out_shape=[_sds((dil, l_len, PAD_LANES), BF16), _sds((dil, l_len, LANE), F32)],
        scratch_shapes=[pltpu.VMEM((qb + ATTN_W, PAD_LANES), BF16), pltpu.VMEM((qb + ATTN_W, PAD_LANES), BF16)],
        semantics=("arbitrary", "arbitrary"),
    )(qkv_g, qkv_g, qkv_g, qkv_g, qkv_g)


def _group_stats(lses):
    head_lane = lax.broadcasted_iota(jnp.int32, (1, LANE), 1)
    fulls, glse = [], []
    for g in range(3):
        real = head_lane < GROUP_HEADS[g]
        mx = jnp.max(jnp.where(real, lses[g], -jnp.inf), axis=-1, keepdims=True)
        sm = jnp.sum(jnp.where(real, jnp.exp(lses[g] - mx), 0.0), axis=-1, keepdims=True)
        fulls.append(mx + jnp.log(sm))
        glse.append(fulls[g] - math.log(GROUP_HEADS[g]))
    top = jnp.maximum(jnp.maximum(glse[0], glse[1]), glse[2])
    ex = [jnp.exp(v - top) for v in glse]
    tot = ex[0] + ex[1] + ex[2]
    alpha = [v / tot for v in ex]
    lane = lax.broadcasted_iota(jnp.int32, (1, QKV_PAD), 1)
    scale = jnp.where(lane < PAD_LANES, 3.0 * alpha[0],
                      jnp.where(lane < 2 * PAD_LANES, 3.0 * alpha[1], 3.0 * alpha[2]))
    return alpha, fulls, scale


def attn_out_fwd(x, o_parts, lse_parts, w_out_pad):
    t_len = x.shape[0]
    tm = _tile_rows(t_len)

    def body(x_ref, o1, o4, o16, l1, l4, l16, w_ref, xo_ref, mg_ref, o_ref, lse_ref, o_slab, l_slab):
        for gi, (dil, og, lg) in enumerate(zip(GROUP_DIL, (o1, o4, o16), (l1, l4, l16))):
            for j in range(3):
                _to_natural(og, j, o_slab, 3 * gi + j, dil, tm)
            _to_natural(lg, 0, l_slab, gi, dil, tm)
        o = jnp.concatenate([o_slab[j] for j in range(N_SLABS)], axis=1)
        lses = [l_slab[gi] for gi in range(3)]
        o_ref[...] = o.astype(BF16)
        for gi in range(3):
            lse_ref[:, _lane_cols(gi)] = lses[gi]
        _, _, scale = _group_stats(lses)
        merged = (o * scale).astype(BF16)
        mg_ref[...] = merged
        xo_ref[...] = x_ref[...] + _dot(merged, w_ref[...])

    row_spec = pl.BlockSpec((tm, D_MODEL), lambda t: (t, 0))
    pad_spec = pl.BlockSpec((tm, QKV_PAD), lambda t: (t, 0))
    o_specs = [pl.BlockSpec((d, tm // d, PAD_LANES), lambda t: (0, t, 0)) for d in GROUP_DIL]
    lse_specs = [pl.BlockSpec((d, tm // d, LANE), lambda t: (0, t, 0)) for d in GROUP_DIL]
    return _pcall(
        body, name="attn_out_fwd", grid=(t_len // tm,),
        in_specs=[row_spec] + o_specs + lse_specs + [pl.BlockSpec((QKV_PAD, D_MODEL), lambda t: (0, 0))],
        out_specs=[row_spec, pad_spec, pad_spec, pl.BlockSpec((tm, 3 * LANE), lambda t: (t, 0))],
        out_shape=[_sds((t_len, D_MODEL), F32), _sds((t_len, QKV_PAD), BF16),
                   _sds((t_len, QKV_PAD), BF16), _sds((t_len, 3 * LANE), F32)],
        scratch_shapes=[pltpu.VMEM((N_SLABS, tm, LANE), F32), pltpu.VMEM((3, tm, LANE), F32)],
        semantics=("arbitrary",),
    )(x, *o_parts, *lse_parts, w_out_pad)


def attn_out_bwd(dxo, w_out_pad, o, lse, duty=None):
    t_len = dxo.shape[0]
    tm = _tile_rows(t_len)

    def body(dx_ref, w_ref, o_ref, lse_ref, d1, d4, d16, c1, c4, c16, slab):
        dmerged = _dot_nt(dx_ref[...].astype(BF16), w_ref[...])
        o_t = o_ref[...].astype(F32)
        lses = [lse_ref[:, _lane_cols(gi)] for gi in range(3)]
        alpha, fulls, scale = _group_stats(lses)
        e = dmerged * o_t
        lane = lax.broadcasted_iota(jnp.int32, (1, QKV_PAD), 1)
        dalpha = [3.0 * jnp.sum(jnp.where((lane >= g * PAD_LANES) & (lane < g * PAD_LANES + GROUP_REAL[g]), e, 0.0),
                                axis=-1, keepdims=True) for g in range(3)]
        mean_da = alpha[0] * dalpha[0] + alpha[1] * dalpha[1] + alpha[2] * dalpha[2]
        dglse = [alpha[g] * (dalpha[g] - mean_da) for g in range(3)]
        do = dmerged * scale
        es = e * scale
        for j in range(N_SLABS):
            slab[j] = do[:, _lane_cols(j)]
        for gi, (dil, dg) in enumerate(zip(GROUP_DIL, (d1, d4, d16))):
            for j in range(3):
                _to_residue_major(slab, 3 * gi + j, dg, j, dil, tm)
        head_lane = lax.broadcasted_iota(jnp.int32, (1, LANE), 1)
        first = head_lane < HEAD_DIM
        for gi, (dil, cg) in enumerate(zip(GROUP_DIL, (c1, c4, c16))):
            c_g = -(dglse[gi] * jnp.exp(lses[gi] - fulls[gi]))
            for j in range(3):
                blk = es[:, _lane_cols(3 * gi + j)]
                halves = (jnp.sum(jnp.where(first, blk, 0.0), axis=-1, keepdims=True),
                          jnp.sum(jnp.where(first, 0.0, blk), axis=-1, keepdims=True))
                for half in range(2):
                    c_g = c_g + jnp.where(head_lane == 2 * j + half, halves[half], 0.0)
            slab[gi] = c_g
            _to_residue_major(slab, gi, cg, 0, dil, tm)

    row_spec = pl.BlockSpec((tm, D_MODEL), lambda t: (t, 0))
    pad_spec = pl.BlockSpec((tm, QKV_PAD), lambda t: (t, 0))
    do_specs = [pl.BlockSpec((d, tm // d, PAD_LANES), lambda t: (0, t, 0)) for d in GROUP_DIL]
    c_specs = [pl.BlockSpec((d, tm // d, LANE), lambda t: (0, t, 0)) for d in GROUP_DIL]
    outs = _pcall(
        body, name="attn_out_bwd", grid=(t_len // tm,),
        in_specs=[row_spec, pl.BlockSpec((QKV_PAD, D_MODEL), lambda t: (0, 0)), pad_spec,
                  pl.BlockSpec((tm, 3 * LANE), lambda t: (t, 0))],
        out_specs=do_specs + c_specs,
        out_shape=[_sds((d, t_len // d, PAD_LANES), BF16) for d in GROUP_DIL]
                  + [_sds((d, t_len // d, LANE), F32) for d in GROUP_DIL],
        scratch_shapes=[pltpu.VMEM((N_SLABS, tm, LANE), F32)],
        semantics=("arbitrary",), duty=duty,
    )(dxo, w_out_pad, o, lse)
    if duty is None:
        return outs[:3], outs[3:]
    return (outs[0][:3], outs[0][3:]), outs[1]


def attn_bwd(qkv_g, do_g, lse_g, c_g, gi, name, duty=None):
    dil, l_len, _ = qkv_g.shape
    qb = min(Q_BLOCK, l_len)
    nsub = qb // ATTN_W
    nsb = l_len // qb

    def body(q_ref, kc_ref, kp_ref, vc_ref, vp_ref, do_ref, lse_ref, c_ref,
             qn_ref, don_ref, lsen_ref, cn_ref, o_ref, kbuf, vbuf, dkbuf, dvbuf):
        n = pl.program_id(1)
        kbuf[pl.ds(0, ATTN_W), :] = kp_ref[...]
        kbuf[pl.ds(ATTN_W, qb), :] = kc_ref[...]
        vbuf[pl.ds(0, ATTN_W), :] = vp_ref[...]
        vbuf[pl.ds(ATTN_W, qb), :] = vc_ref[...]
        dkbuf[...] = jnp.zeros_like(dkbuf)
        dvbuf[...] = jnp.zeros_like(dvbuf)

        def block(q_of, do_of, lse_of, c_of, krows, mask, dq_rows):
            heads = []
            for j in range(3):
                cols = _lane_cols(j)
                q, do_t, k, v = q_of(cols), do_of(cols), kbuf[krows, cols], vbuf[krows, cols]
                for half, hm in enumerate(_live_halves(gi, j)):
                    qh = jnp.where(hm, q, jnp.zeros_like(q))
                    doh = jnp.where(hm, do_t, jnp.zeros_like(do_t))
                    heads.append((j, 2 * j + half, hm, qh, doh, _dot_nt(qh, k), _dot_nt(doh, v)))
            head_lane = lax.broadcasted_iota(jnp.int32, (1, LANE), 1)
            lse_t, c_t = lse_of(), c_of()
            for j in range(3):
                cols = _lane_cols(j)
                k = kbuf[krows, cols]
                dq = jnp.zeros((ATTN_W, LANE), F32)
                dk = jnp.zeros((k.shape[0], LANE), F32)
                dv = jnp.zeros((k.shape[0], LANE), F32)
                for hj, head, hm, qh, doh, s, dp in heads:
                    if hj != j:
                        continue
                    lse_h = jnp.max(jnp.where(head_lane == head, lse_t, -jnp.inf), axis=-1, keepdims=True)
                    c_h = jnp.max(jnp.where(head_lane == head, c_t, -jnp.inf), axis=-1, keepdims=True)
                    p = jnp.exp(jnp.where(mask, s, NEG_INF) - lse_h)
                    ds = (p * (dp - c_h)).astype(BF16)
                    if dq_rows is not None:
                        dq = jnp.where(hm, _dot(ds, k), dq)
                    dk = dk + _dot_tn(ds, qh)
                    dv = dv + _dot_tn(p.astype(BF16), doh)
                if dq_rows is not None:
                    o_ref[dq_rows, cols] = dq.astype(BF16)
                dkbuf[krows, cols] += dk
                dvbuf[krows, cols] += dv

        band, band_first = _band_masks()

        def sub(b, carry):
            rows = pl.ds(pl.multiple_of(b * ATTN_W, ATTN_W), ATTN_W)
            krows = pl.ds(pl.multiple_of(b * ATTN_W, ATTN_W), 2 * ATTN_W)
            block(lambda c: q_ref[rows, c], lambda c: do_ref[rows, c], lambda: lse_ref[rows, :],
                  lambda: c_ref[rows, :], krows, band_first | (band & (n + b > 0)), rows)
            return carry

        lax.fori_loop(0, nsub, sub, 0)

        qi = lax.broadcasted_iota(jnp.int32, (ATTN_W, ATTN_W), 0)
        kj = lax.broadcasted_iota(jnp.int32, (ATTN_W, ATTN_W), 1)
        nmask = (qi <= kj) & (n < nsb - 1)
        block(lambda c: qn_ref[:, c], lambda c: don_ref[:, c], lambda: lsen_ref[...],
              lambda: cn_ref[...], pl.ds(qb, ATTN_W), nmask, None)
        o_ref[:, pl.ds(PAD_LANES, PAD_LANES)] = dkbuf[pl.ds(ATTN_W, qb), :].astype(BF16)
        o_ref[:, pl.ds(2 * PAD_LANES, PAD_LANES)] = dvbuf[pl.ds(ATTN_W, qb), :].astype(BF16)

    cur = lambda c: pl.BlockSpec((None, qb, PAD_LANES), lambda r, n: (r, n, c))
    prev = lambda c: pl.BlockSpec((None, ATTN_W, PAD_LANES), lambda r, n: (r, jnp.maximum(n * nsub - 1, 0), c))
    nxt_row = lambda r, n: (r, jnp.minimum((n + 1) * nsub, nsb * nsub - 1), 0)
    nxt = pl.BlockSpec((None, ATTN_W, PAD_LANES), nxt_row)
    head_cur = pl.BlockSpec((None, qb, LANE), lambda r, n: (r, n, 0))
    head_nxt = pl.BlockSpec((None, ATTN_W, LANE), nxt_row)
    return _pcall(
        body, name=name, grid=(dil, nsb),
        in_specs=[cur(0), cur(1), prev(1), cur(2), prev(2), cur(0), head_cur, head_cur, nxt, nxt, head_nxt, head_nxt],
        out_specs=pl.BlockSpec((None, qb, QKV_PAD), lambda r, n: (r, n, 0)),
        out_shape=_sds((dil, l_len, QKV_PAD), BF16),
        scratch_shapes=[pltpu.VMEM((qb + ATTN_W, PAD_LANES), BF16), pltpu.VMEM((qb + ATTN_W, PAD_LANES), BF16),
                        pltpu.VMEM((qb + ATTN_W, PAD_LANES), F32), pltpu.VMEM((qb + ATTN_W, PAD_LANES), F32)],
        semantics=("arbitrary", "arbitrary"), duty=duty,
    )(qkv_g, qkv_g, qkv_g, qkv_g, qkv_g, do_g, lse_g, c_g, qkv_g, do_g, lse_g, c_g)


def qkv_bwd(dqkv_parts, w_pad, dxo, x, g_row, cos, sin):
    t_len = x.shape[0]
    tm = _tile_rows(t_len)

    def body(p1, p4, p16, w_ref, dxo_ref, x_ref, g_ref, cos_ref, sin_ref, dq_ref, dx_ref, dn_ref, slabs):
        @pl.when(pl.program_id(0) == 0)
        def _():
            dn_ref[...] = jnp.zeros_like(dn_ref)

        cos_t = cos_ref[...]
        sin_t = sin_ref[...]
        dh = None
        for gi, (dil, part) in enumerate(zip(GROUP_DIL, (p1, p4, p16))):
            slab = slabs.at[gi]
            for j in range(N_SLABS):
                if dil == 1:
                    a = part[0, :, _lane_cols(j)].astype(F32)
                else:
                    _to_natural(part, j, slab, j, dil, tm)
                    a = slab[j]
                if j < 6:
                    a = a * cos_t - _rot_half(a * sin_t)
                if j < 3:
                    a = a * (HEAD_DIM ** -0.5)
                dq_ref[gi, :, _lane_cols(j)] = a.astype(BF16)
            contrib = _dot_nt(dq_ref[gi], w_ref[gi])
            dh = contrib if dh is None else dh + contrib
        dx, dn = _rms_bwd(dh, x_ref[...], g_ref[...])
        dx_ref[...] = dxo_ref[...] + dx
        dn_ref[...] += dn

    row_spec = pl.BlockSpec((tm, D_MODEL), lambda t: (t, 0))
    vec_spec = pl.BlockSpec((1, D_MODEL), lambda t: (0, 0))
    tab_spec = pl.BlockSpec((tm, LANE), lambda t: (t, 0))
    part_specs = [pl.BlockSpec((d, tm // d, QKV_PAD), lambda t: (0, t, 0)) for d in GROUP_DIL]
    return _pcall(
        body, name="qkv_bwd", grid=(t_len // tm,),
        in_specs=part_specs + [pl.BlockSpec((3, D_MODEL, QKV_PAD), lambda t: (0, 0, 0)),
                               row_spec, row_spec, vec_spec, tab_spec, tab_spec],
        out_specs=[pl.BlockSpec((3, tm, QKV_PAD), lambda t: (0, t, 0)), row_spec, vec_spec],
        out_shape=[_sds((3, t_len, QKV_PAD), BF16), _sds((t_len, D_MODEL), F32), _sds((1, D_MODEL), F32)],
        scratch_shapes=[pltpu.VMEM((3, N_SLABS, tm, LANE), F32)],
        semantics=("arbitrary",),
    )(*dqkv_parts, w_pad, dxo, x, g_row, cos, sin)


def final_fwd_bwd(x, g_row, target):
    t_len = x.shape[0]
    tm = _tile_rows(t_len)

    def body(x_ref, g_ref, tgt_ref, dx_ref, dn_ref, loss_ref):
        @pl.when(pl.program_id(0) == 0)
        def _():
            dn_ref[...] = jnp.zeros_like(dn_ref)
            loss_ref[...] = jnp.zeros_like(loss_ref)

        x_t = x_ref[...]
        g = g_ref[...]
        diff = _rms_fwd(x_t, g) - tgt_ref[...]
        loss_ref[...] += 0.5 * jnp.sum(jnp.mean(diff * diff, axis=-1, keepdims=True), axis=0, keepdims=True)
        dx, dn = _rms_bwd(diff * (1.0 / D_MODEL), x_t, g)
        dx_ref[...] = dx
        dn_ref[...] += dn

    row_spec = pl.BlockSpec((tm, D_MODEL), lambda t: (t, 0))
    vec_spec = pl.BlockSpec((1, D_MODEL), lambda t: (0, 0))
    return _pcall(
        body, name="final_fwd_bwd", grid=(t_len // tm,),
        in_specs=[row_spec, vec_spec, row_spec],
        out_specs=[row_spec, vec_spec, pl.BlockSpec((1, 1), lambda t: (0, 0))],
        out_shape=[_sds((t_len, D_MODEL), F32), _sds((1, D_MODEL), F32), _sds((1, 1), F32)],
        semantics=("arbitrary",),
    )(x, g_row, target)


def pool_bwd(dxo, x, g_row, w_in, w_grp, scale, w_out, zr, duty=None):
    t_len = x.shape[0]
    tm = _tile_rows(t_len)
    nt = t_len // tm

    def body(dxo_ref, x_ref, g_ref, win_ref, wgrp_ref, scale_ref, wout_ref, zr_ref,
             dzs_ref, du_ref, dx_ref, dn_ref, dsc_ref, ebuf, tmp_a, tmp_b):
        i = pl.program_id(0)
        t = nt - 1 - i

        @pl.when(i == 0)
        def _():
            ebuf[pl.ds(tm, POOL_HALO), :] = jnp.zeros((POOL_HALO, D_MODEL), F32)
            dn_ref[...] = jnp.zeros_like(dn_ref)
            dsc_ref[...] = jnp.zeros_like(dsc_ref)

        dxo_t = dxo_ref[...]
        dz = _dot_nt(dxo_t.astype(BF16), wout_ref[...])
        dsc_ref[...] += jnp.sum(dz * zr_ref[...].astype(F32), axis=0, keepdims=True)
        dzs_ref[...] = (dz * scale_ref[...]).astype(BF16)
        row = t * tm + lax.broadcasted_iota(jnp.int32, (tm, 1), 0)
        for gi, w in enumerate(POOL_WINDOWS):
            cols = pl.ds(gi * POOL_GROUP_DIM, POOL_GROUP_DIM)
            dp_g = _dot_nt(dzs_ref[:, cols], wgrp_ref[gi])
            inv_cnt = 1.0 / jnp.minimum(row + 1, w).astype(F32)
            ebuf[pl.ds(0, tm), cols] = dp_g * inv_cnt
            acc = _window_sums(ebuf, cols, (tmp_a, tmp_b), gi + 1, tm, back=False) - dp_g
            du_ref[:, cols] = acc.astype(BF16)
        ebuf[pl.ds(tm, POOL_HALO), :] = ebuf[pl.ds(0, POOL_HALO), :]
        dh = _dot_nt(du_ref[...], win_ref[...])
        dx, dn = _rms_bwd(dh, x_ref[...], g_ref[...])
        dx_ref[...] = dxo_t + dx
        dn_ref[...] += dn

    row_spec = pl.BlockSpec((tm, D_MODEL), lambda i: (nt - 1 - i, 0))
    full = lambda shape: pl.BlockSpec(shape, lambda i: (0,) * len(shape))
    vec = full((1, D_MODEL))
    return _pcall(
        body, name="pool_bwd", grid=(nt,),
        in_specs=[row_spec, row_spec, vec, full((D_MODEL, D_MODEL)), full((4, POOL_GROUP_DIM, POOL_GROUP_DIM)),
                  vec, full((D_MODEL, D_MODEL)), row_spec],
        out_specs=[row_spec, row_spec, row_spec, vec, vec],
        out_shape=[_sds((t_len, D_MODEL), BF16), _sds((t_len, D_MODEL), BF16), _sds((t_len, D_MODEL), F32),
                   _sds((1, D_MODEL), F32), _sds((1, D_MODEL), F32)],
        scratch_shapes=[pltpu.VMEM((tm + POOL_HALO, D_MODEL), F32)]
                       + [pltpu.VMEM((tm + POOL_HALO, POOL_GROUP_DIM), F32)] * 2,
        semantics=("arbitrary",), duty=duty,
    )(dxo, x, g_row, w_in, w_grp, scale, w_out, zr)


def _mesh_pos():
    return lax.axis_index("x"), lax.axis_index("y"), lax.axis_index("c")


def _other_chips(x, y):
    return [(1 - x, y), (x, 1 - y), (1 - x, 1 - y)]


def _remote(src, dst, send_sem, recv_sem, device):
    return pltpu.make_async_remote_copy(src_ref=src, dst_ref=dst, send_sem=send_sem, recv_sem=recv_sem,
                                        device_id=device, device_id_type=MESH)


class _Duty:
    aliases = {}

    def mid(self, ins, outs, sems):
        pass


class Together(_Duty):
    def __init__(self, duties):
        self.duties = duties
        self.ins = [a for d in duties for a in d.ins]
        self.out_shape = [s for d in duties for s in d.out_shape]
        self.scratch = [s for d in duties for s in d.scratch]
        self.aliases = {}
        i0 = o0 = 0
        for d in duties:
            self.aliases.update({i0 + i: o0 + o for i, o in d.aliases.items()})
            i0 += len(d.ins)
            o0 += len(d.out_shape)

    def _each(self, ins, outs, sems):
        i0 = o0 = s0 = 0
        for d in self.duties:
            ni, no, ns = len(d.ins), len(d.out_shape), len(d.scratch)
            yield d, ins[i0:i0 + ni], outs[o0:o0 + no], sems[s0:s0 + ns]
            i0, o0, s0 = i0 + ni, o0 + no, s0 + ns

    def split(self, outs):
        return [list(o) for _, _, o, _ in self._each(self.ins, outs, self.scratch)]

    def start(self, ins, outs, sems):
        for d, i, o, s in self._each(ins, outs, sems):
            d.start(i, o, s)

    def mid(self, ins, outs, sems):
        for d, i, o, s in self._each(ins, outs, sems):
            d.mid(i, o, s)

    def finish(self, ins, outs, sems):
        for d, i, o, s in self._each(ins, outs, sems):
            d.finish(i, o, s)


def run_duty(duty, name):
    d_in, d_out = len(duty.ins), len(duty.out_shape)

    def body(*refs):
        ins, outs, sems = refs[:d_in], refs[d_in:d_in + d_out], refs[d_in + d_out:]
        duty.start(ins, outs, sems)
        duty.mid(ins, outs, sems)
        duty.finish(ins, outs, sems)

    return pl.pallas_call(
        body, name=name, out_shape=list(duty.out_shape), in_specs=[_ANY] * d_in, out_specs=[_ANY] * d_out,
        scratch_shapes=list(duty.scratch), input_output_aliases=dict(duty.aliases),
        compiler_params=pltpu.CompilerParams(has_side_effects=True),
    )(*duty.ins)


class GatherWeights(_Duty):
    N_COPIES = 7

    def __init__(self, shards):
        n = self.n = len(shards)
        self.halves = [s.shape[0] // 2 for s in shards]
        my_slot = 2 * lax.axis_index("x") + lax.axis_index("y")
        staged = [lax.dynamic_update_slice(lax.empty((N_SHARDS,) + s.shape, s.dtype), s[None], (my_slot, 0, 0))
                  for s in shards]
        self.ins = list(shards) + staged
        self.out_shape = [_sds((N_SHARDS,) + s.shape, s.dtype) for s in shards]
        self.aliases = {n + a: a for a in range(n)}
        self.scratch = [pltpu.SemaphoreType.DMA((n, self.N_COPIES)), pltpu.SemaphoreType.DMA((n, self.N_COPIES))]

    def _copies(self, ins, outs, sems, a):
        x, y, c = _mesh_pos()
        h = self.halves[a]
        q = h // 2
        sibling, to_x, to_y = (x, y, 1 - c), (1 - x, y, c), (x, 1 - y, c)
        me, of_x, of_y, of_d = 2 * x + y, 2 * (1 - x) + y, 2 * x + 1 - y, 2 * (1 - x) + 1 - y
        half = lambda core: pl.ds(core * h, h)
        quarter = lambda k: pl.ds(c * h + k * q, q)
        out = outs[a]
        mine = ins[a].at[half(c)]

        def copy(k, src, dst, device):
            return _remote(src, dst, sems[0].at[a, k], sems[1].at[a, k], device)

        same = lambda k, ref, device: copy(k, ref, ref, device)
        sent = [copy(0, mine, out.at[me, half(c)], to_x),
                copy(1, mine, out.at[me, half(c)], to_y),
                same(2, out.at[of_x, quarter(0)], to_y),
                same(3, out.at[of_y, quarter(1)], to_x),
                same(4, out.at[of_x, half(c)], sibling),
                same(5, out.at[of_y, half(c)], sibling),
                same(6, out.at[of_d, half(c)], sibling)]
        landing = [out.at[of_x, half(c)], out.at[of_y, half(c)], out.at[of_d, quarter(0)], out.at[of_d, quarter(1)],
                   out.at[of_x, half(1 - c)], out.at[of_y, half(1 - c)], out.at[of_d, half(1 - c)]]
        return sent, [same(k, ref, sibling) for k, ref in enumerate(landing)]

    def start(self, ins, outs, sems):
        for a in range(self.n):
            sent, _ = self._copies(ins, outs, sems, a)
            sent[0].start()
            sent[1].start()

    def mid(self, ins, outs, sems):
        for a in range(self.n):
            sent, lands = self._copies(ins, outs, sems, a)
            lands[0].wait_recv()
            sent[2].start()
            sent[4].start()
            lands[1].wait_recv()
            sent[3].start()
            sent[5].start()

    def finish(self, ins, outs, sems):
        for a in range(self.n):
            sent, lands = self._copies(ins, outs, sems, a)
            lands[2].wait_recv()
            lands[3].wait_recv()
            sent[6].start()
        for a in range(self.n):
            sent, lands = self._copies(ins, outs, sems, a)
            for cp in lands[4:]:
                cp.wait_recv()
            for cp in sent:
                cp.wait_send()


class GradReducer:
    def __init__(self, c_idx, pos_idx):
        self.c_idx, self.pos_idx = c_idx, pos_idx
        self.in_flight = []
        self.done = {}

    def push(self, name, grad):
        self.in_flight.append(dict(name=name, stage="halves", data=grad))

    def _duties(self):
        make = {"halves": SiblingHalves, "exchange": ChipExchange, "share": SiblingShare}
        return Together([make[w["stage"]]([w["data"]]) for w in self.in_flight])

    def _advance(self, duties, outs):
        still = []
        for w, (res,) in zip(self.in_flight, duties.split(outs)):
            if w["stage"] == "halves":
                partial = add_my_half(w["data"], res, self.c_idx, f"rs_add_{w['name']}")
                still.append(dict(name=w["name"], stage="exchange", data=partial))
            elif w["stage"] == "exchange":
                reduced = sum_slots(res, w["data"], self.pos_idx, f"rs_sum_{w['name']}")
                still.append(dict(name=w["name"], stage="share", data=reduced))
            else:
                self.done[w["name"]] = res
        self.in_flight = still

    def carried_by(self, fn, *args, **kw):
        if not self.in_flight:
            return fn(*args, **kw)
        duties = self._duties()
        out, duty_outs = fn(*args, duty=duties, **kw)
        self._advance(duties, duty_outs)
        return out

    def drain(self, name):
        step = 0
        while self.in_flight:
            duties = self._duties()
            self._advance(duties, run_duty(duties, f"{name}{step}"))
            step += 1


class SiblingHalves(_Duty):
    def __init__(self, grads):
        n = len(grads)
        self.halves = [g.shape[1] // 2 for g in grads]
        self.ins = list(grads)
        self.out_shape = [_sds((N_SHARDS, h, g.shape[2]), g.dtype) for g, h in zip(grads, self.halves)]
        self.scratch = [pltpu.SemaphoreType.DMA((n,)), pltpu.SemaphoreType.DMA((n,))]

    def _copies(self, ins, outs, sems):
        x, y, c = _mesh_pos()
        return [_remote(ins[a].at[:, pl.ds((1 - c) * h, h)], outs[a], sems[0].at[a], sems[1].at[a], (x, y, 1 - c))
                for a, h in enumerate(self.halves)]

    def start(self, ins, outs, sems):
        for cp in self._copies(ins, outs, sems):
            cp.start()

    def finish(self, ins, outs, sems):
        for cp in self._copies(ins, outs, sems):
            cp.wait()


class ChipExchange(_Duty):
    def __init__(self, parts):
        n = self.n = len(parts)
        self.ins = list(parts)
        self.out_shape = [_sds(p.shape, p.dtype) for p in parts]
        self.scratch = [pltpu.SemaphoreType.DMA((n, 3)), pltpu.SemaphoreType.DMA((n, 3))]

    def _copies(self, ins, outs, sems, arriving):
        x, y, c = _mesh_pos()
        cps = []
        for a in range(self.n):
            for j, chip in enumerate(_other_chips(x, y)):
                theirs = 2 * chip[0] + chip[1]
                src = outs[a].at[theirs] if arriving else ins[a].at[theirs]
                dst = outs[a].at[theirs] if arriving else outs[a].at[2 * x + y]
                cps.append(_remote(src, dst, sems[0].at[a, j], sems[1].at[a, j], (*chip, c)))
        return cps

    def start(self, ins, outs, sems):
        for cp in self._copies(ins, outs, sems, False):
            cp.start()

    def finish(self, ins, outs, sems):
        for cp in self._copies(ins, outs, sems, True):
            cp.wait_recv()
        for cp in self._copies(ins, outs, sems, False):
            cp.wait_send()


class SiblingShare(_Duty):
    def __init__(self, reduced):
        n = self.n = len(reduced)
        self.ins = list(reduced)
        self.out_shape = [_sds(r.shape, r.dtype) for r in reduced]
        self.aliases = {a: a for a in range(n)}
        self.scratch = [pltpu.SemaphoreType.DMA((n,)), pltpu.SemaphoreType.DMA((n,))]

    def _copies(self, outs, sems, half_of):
        x, y, c = _mesh_pos()
        cps = []
        for a in range(self.n):
            h = outs[a].shape[0] // 2
            rows = outs[a].at[pl.ds(half_of(c) * h, h)]
            cps.append(_remote(rows, rows, sems[0].at[a], sems[1].at[a], (x, y, 1 - c)))
        return cps

    def start(self, ins, outs, sems):
        for cp in self._copies(outs, sems, lambda core: core):
            cp.start()

    def finish(self, ins, outs, sems):
        for cp in self._copies(outs, sems, lambda core: 1 - core):
            cp.wait_recv()
        for cp in self._copies(outs, sems, lambda core: core):
            cp.wait_send()


def allreduce_small(v):
    def body(v_ref, o_ref, buf, send_sems, recv_sems):
        x, y, c = _mesh_pos()
        me = 4 * x + 2 * y + c
        buf[me] = v_ref[...]
        flip = lambda p, f: 1 - p if f else p
        peers = [(flip(x, k & 4), flip(y, k & 2), flip(c, k & 1)) for k in range(1, N_DEV)]
        cps = []
        for k, peer in enumerate(peers):
            cp = _remote(v_ref, buf.at[me], send_sems.at[k], recv_sems.at[k], peer)
            cp.start()
            cps.append(cp)
        for k, peer in enumerate(peers):
            slot = buf.at[4 * peer[0] + 2 * peer[1] + peer[2]]
            _remote(slot, slot, send_sems.at[k], recv_sems.at[k], peer).wait_recv()
        for cp in cps:
            cp.wait_send()
        acc = buf[0]
        for i in range(1, N_DEV):
            acc = acc + buf[i]
        o_ref[...] = acc

    vm = pl.BlockSpec(memory_space=pltpu.VMEM)
    return pl.pallas_call(
        body, name="allreduce_small", out_shape=_sds(v.shape, v.dtype), in_specs=[vm], out_specs=vm,
        scratch_shapes=[pltpu.VMEM((N_DEV,) + v.shape, v.dtype),
                        pltpu.SemaphoreType.DMA((N_DEV - 1,)), pltpu.SemaphoreType.DMA((N_DEV - 1,))],
        compiler_params=pltpu.CompilerParams(has_side_effects=True),
    )(v)


def add_my_half(grad, theirs, c_idx, name):
    _, r, cols = grad.shape
    h = r // 2

    def body(c_ref, g_ref, t_ref, o_ref):
        o_ref[...] = (g_ref[...] + t_ref[...]).astype(BF16)

    slot = pl.BlockSpec((None, h, cols), lambda s, c: (s, 0, 0))
    grid_spec = pltpu.PrefetchScalarGridSpec(
        num_scalar_prefetch=1, grid=(N_SHARDS,),
        in_specs=[pl.BlockSpec((None, h, cols), lambda s, c: (s, c[0], 0)), slot], out_specs=slot)
    return pl.pallas_call(
        body, name=name, grid_spec=grid_spec, out_shape=_sds((N_SHARDS, h, cols), BF16),
        compiler_params=pltpu.CompilerParams(dimension_semantics=("arbitrary",), vmem_limit_bytes=VMEM_LIMIT_BYTES),
    )(c_idx, grad, theirs)


def sum_slots(received, mine, pos_idx, name):
    _, h, cols = received.shape

    def body(pos_ref, r_ref, m_ref, o_ref):
        acc = None
        for k in range(N_SHARDS):
            term = jnp.where(pos_ref[0] == k, m_ref[k], r_ref[k]).astype(F32)
            acc = term if acc is None else acc + term
        o_ref[...] = acc

    whole = pl.BlockSpec((N_SHARDS, h, cols), lambda i, pos: (0, 0, 0))
    grid_spec = pltpu.PrefetchScalarGridSpec(
        num_scalar_prefetch=1, grid=(1,), in_specs=[whole, whole],
        out_specs=pl.BlockSpec((h, cols), lambda i, pos: (pos[1], 0)))
    return pl.pallas_call(
        body, name=name, grid_spec=grid_spec, out_shape=_sds((2 * h, cols), F32),
        compiler_params=pltpu.CompilerParams(dimension_semantics=("arbitrary",), vmem_limit_bytes=VMEM_LIMIT_BYTES),
    )(pos_idx, received, mine)


def adamw(name, grads, w, m, v):
    n_layers, r, cols = w.shape
    tr = r // 2 if r % 16 == 0 else r
    bias1 = 1.0 - ADAM_B1 ** ADAM_STEP
    bias2 = 1.0 - ADAM_B2 ** ADAM_STEP

    def body(*refs):
        g_refs = refs[:n_layers]
        w_ref, m_ref, v_ref, go_ref, d_ref, mo_ref, vo_ref = refs[n_layers:]
        g = g_refs[0][...]
        for layer in range(1, n_layers):
            g = jnp.where(pl.program_id(0) == layer, g_refs[layer][...], g)
        m_new = ADAM_B1 * m_ref[...] + (1.0 - ADAM_B1) * g
        v_new = ADAM_B2 * v_ref[...] + (1.0 - ADAM_B2) * (g * g)
        m_hat = m_new / bias1
        v_hat = v_new / bias2
        go_ref[...] = g
        d_ref[...] = -ADAM_LR * (m_hat / (jnp.sqrt(v_hat) + ADAM_EPS) + ADAM_WD * w_ref[...])
        mo_ref[...] = m_new
        vo_ref[...] = v_new

    g_spec = pl.BlockSpec((tr, cols), lambda l, i: (i, 0))
    lay_spec = pl.BlockSpec((None, tr, cols), lambda l, i: (l, i, 0))
    shape = _sds((n_layers, r, cols), F32)
    return _pcall(
        body, name=name, grid=(n_layers, r // tr),
        in_specs=[g_spec] * n_layers + [lay_spec] * 3, out_specs=[lay_spec] * 4,
        out_shape=[shape] * 4, semantics=("arbitrary", "arbitrary"),
    )(*grads, w, m, v)


def kernel(x, norm_mix, norm_ffn, norm_final, pool_w_in, pool_w_group, pool_scale, pool_w_out, attn_w_qkv, attn_w_out, ffn_w_gate, ffn_w_up, ffn_w_down, loss_target, m_norm_mix, m_norm_ffn, m_norm_final, m_pool_w_in, m_pool_w_group, m_pool_scale, m_pool_w_out, m_attn_w_qkv, m_attn_w_out, m_ffn_w_gate, m_ffn_w_up, m_ffn_w_down, v_norm_mix, v_norm_ffn, v_norm_final, v_pool_w_in, v_pool_w_group, v_pool_scale, v_pool_w_out, v_attn_w_qkv, v_attn_w_out, v_ffn_w_gate, v_ffn_w_up, v_ffn_w_down):
    t_len = x.shape[1]
    x0 = x.reshape(t_len, D_MODEL)
    target = loss_target.reshape(t_len, D_MODEL)
    row = lambda a: a.reshape(1, D_MODEL)

    grp_rows = POOL_GROUP_DIM // N_SHARDS
    bf = lambda a: a.astype(BF16)
    gate_t, up_t = jnp.swapaxes(ffn_w_gate, 1, 2), jnp.swapaxes(ffn_w_up, 1, 2)
    pool_shards = [bf(pool_w_in[0]), bf(pool_w_group[0].reshape(4 * grp_rows, POOL_GROUP_DIM)), bf(pool_w_out[0])]
    ffn0_shards = [bf(gate_t[0]), bf(up_t[0]), bf(ffn_w_down[0])]
    late_shards = [bf(attn_w_qkv[0]), bf(attn_w_out[0]), bf(gate_t[1]), bf(up_t[1]), bf(ffn_w_down[1])]
    cos, sin = rope_tables(t_len)
    c_idx = lax.axis_index("c").astype(jnp.int32).reshape(1)
    pos_idx = jnp.stack([2 * lax.axis_index("x") + lax.axis_index("y"), lax.axis_index("c")]).astype(jnp.int32)
    chip_rows = lambda g: g.reshape(N_SHARDS, D_MODEL // N_SHARDS, D_MODEL)

    g_pool = run_duty(GatherWeights(pool_shards), "gather_pool")
    w_in = g_pool[0].reshape(D_MODEL, D_MODEL)
    w_grp = g_pool[1].reshape(N_SHARDS, 4, grp_rows, POOL_GROUP_DIM).transpose(1, 0, 2, 3).reshape(
        4, POOL_GROUP_DIM, POOL_GROUP_DIM)
    w_out = g_pool[2].reshape(D_MODEL, D_MODEL)
    (h0, p, zr, z, x1), ffn0 = pool_fwd(x0, row(norm_mix[0]), w_in, w_grp, pool_scale, w_out,
                                        duty=GatherWeights(ffn0_shards))
    (h1, gate0, up0, act0, x2), late = ffn_fwd(x1, row(norm_ffn[0]), *ffn0, "ffn_fwd0",
                                               duty=GatherWeights(late_shards))
    w_qkv = pad_qkv_weight(late[0])
    w_ao = jnp.concatenate(pad_groups(late[1].reshape(D_MODEL, D_MODEL), 0), axis=0)
    ffn1 = late[2:5]
    h2, *qkv_parts = qkv_fwd(x2, row(norm_mix[1]), w_qkv, cos, sin)
    o_parts, lse_parts = [], []
    for gi in range(3):
        o_g, lse_g = attn_fwd(qkv_parts[gi], gi, f"attn_fwd_g{gi}")
        o_parts.append(o_g)
        lse_parts.append(lse_g)
    x3, merged, o_nat, lse_nat = attn_out_fwd(x2, o_parts, lse_parts, w_ao)
    h3, gate1, up1, act1, x4 = ffn_fwd(x3, row(norm_ffn[1]), *ffn1, "ffn_fwd1")
    dx4, d_norm_final, loss_local = final_fwd_bwd(x4, row(norm_final), target)

    red = GradReducer(c_idx, pos_idx)
    dgate1, dup1, dx3, d_nf1 = ffn_bwd(dx4, x3, row(norm_ffn[1]), gate1, up1, *ffn1, "ffn_bwd1")
    g_gate1 = wgrad_row_sharded("wgrad_gate1", dgate1, h3)
    g_up1 = wgrad_row_sharded("wgrad_up1", dup1, h3)
    g_down1 = wgrad_row_sharded("wgrad_down1", act1, dx4)
    red.push("gate1", g_gate1)
    red.push("up1", g_up1)
    do_parts, c_parts = red.carried_by(attn_out_bwd, dx3, w_ao, o_nat, lse_nat)
    g_ao = wgrad_full("wgrad_attn_out", merged, dx3)
    red.push("down1", g_down1)
    red.push("attn_out", chip_rows(unpad_groups(jnp.split(g_ao, 3, axis=0), 0)))
    dqkv_parts = [red.carried_by(attn_bwd, qkv_parts[gi], do_parts[gi], lse_parts[gi], c_parts[gi], gi,
                                 f"attn_bwd_g{gi}") for gi in range(3)]
    dqkv, dx2, d_nm1 = qkv_bwd(dqkv_parts, w_qkv, dx3, x2, row(norm_mix[1]), cos, sin)
    red.push("qkv", unpad_qkv_grad(wgrad_col_sharded("wgrad_qkv", h2, dqkv)))

    dgate0, dup0, dx1, d_nf0 = red.carried_by(ffn_bwd, dx2, x1, row(norm_ffn[0]), gate0, up0, *ffn0, "ffn_bwd0")
    red.push("gate0", red.carried_by(wgrad_row_sharded, "wgrad_gate0", dgate0, h1))
    red.push("up0", red.carried_by(wgrad_row_sharded, "wgrad_up0", dup0, h1))
    red.push("down0", red.carried_by(wgrad_row_sharded, "wgrad_down0", act0, dx2))
    red.push("pool_out", chip_rows(red.carried_by(wgrad_full, "wgrad_pool_out", z, dx1)))
    dzs, du, dx0, d_nm0, d_scale = pool_bwd(dx1, x0, row(norm_mix[0]), w_in, w_grp, pool_scale, w_out, zr)
    g_grp = red.carried_by(wgrad_pool_group, "wgrad_pool_group", p, dzs)
    red.push("pool_group", g_grp.reshape(N_SHARDS, 4 * grp_rows, POOL_GROUP_DIM))
    red.push("pool_in", chip_rows(red.carried_by(wgrad_full, "wgrad_pool_in", h0, du)))
    red.drain("rs_tail")
    full = [red.done[nm] for nm in ("pool_in", "pool_group", "pool_out", "qkv", "attn_out",
                                    "gate0", "gate1", "up0", "up1", "down0", "down1")]

    zero_row = jnp.zeros((1, D_MODEL), F32)
    small = jnp.concatenate([d_nm0, d_nm1, d_nf0, d_nf1, d_norm_final, d_scale,
                             jnp.broadcast_to(loss_local, (1, D_MODEL)), zero_row], axis=0)
    small = allreduce_small(small)
    loss = small[6, 0]

    pack = lambda a, b, c, d: jnp.concatenate([a, b, row(c), d, zero_row, zero_row], axis=0)[None]
    sg, sd, sm, sv = adamw("adamw_small", [small],
                           pack(norm_mix, norm_ffn, norm_final, pool_scale),
                           pack(m_norm_mix, m_norm_ffn, m_norm_final, m_pool_scale),
                           pack(v_norm_mix, v_norm_ffn, v_norm_final, v_pool_scale))
    unpack = lambda a: (a[0, 0:2], a[0, 2:4], a[0, 4], a[0, 5:6])

    def update(name, grads, w, m, v, transposed=False):
        if transposed:
            w, m, v = (jnp.swapaxes(a, 1, 2) for a in (w, m, v))
        n_layers = len(grads)
        shp = (n_layers,) + grads[0].shape
        outs = [o.reshape(w.shape) for o in adamw(name, grads, w.reshape(shp), m.reshape(shp), v.reshape(shp))]
        return [jnp.swapaxes(o, 1, 2) for o in outs] if transposed else outs

    big = [
        update("adamw_pool_in", [full[0]], pool_w_in, m_pool_w_in, v_pool_w_in),
        update("adamw_pool_group", [full[1]], pool_w_group, m_pool_w_group, v_pool_w_group),
        update("adamw_pool_out", [full[2]], pool_w_out, m_pool_w_out, v_pool_w_out),
        update("adamw_qkv", [full[3]], attn_w_qkv, m_attn_w_qkv, v_attn_w_qkv),
        update("adamw_attn_out", [full[4]], attn_w_out, m_attn_w_out, v_attn_w_out),
        update("adamw_gate", [full[5], full[6]], ffn_w_gate, m_ffn_w_gate, v_ffn_w_gate, transposed=True),
        update("adamw_up", [full[7], full[8]], ffn_w_up, m_ffn_w_up, v_ffn_w_up, transposed=True),
        update("adamw_down", [full[9], full[10]], ffn_w_down, m_ffn_w_down, v_ffn_w_down),
    ]

    def leaves(k, small_vals):
        nm, nf, nfin, psc = unpack(small_vals)
        return [nm, nf, nfin, big[0][k], big[1][k], psc, big[2][k], big[3][k], big[4][k],
                big[5][k], big[6][k], big[7][k]]

    grad_x = dx0.reshape(x.shape)
    return (loss, grad_x, *leaves(0, sg), *leaves(1, sd), *leaves(2, sm), *leaves(3, sv))
```
